```python
import math
import jax, jax.numpy as jnp
from jax import lax
import numpy as np

D_MODEL = 1024
BATCH = 16
SEQ = 256
DEPTH = 2
DEC_BATCH = 2
DEC_SEQ = 1024
PAST_LEN = 256

GRID_W = 64
N_EVEN = (DEPTH + 1) // 2
N_ODD = DEPTH // 2
N_MOD = 9
FF_DIM = 2816
EPS = 1e-6
ROPE_BASE = 10000.0
Q_BLOCK = 128

MLA_HEADS = 8
MLA_NOPE = 64
MLA_ROPE = 32
MLA_V = 64
MLA_Q_RANK = 384
MLA_KV_RANK = 256

S5_WIDTH = 512
S5_GROUP = 16
S5_GROUPS = S5_WIDTH // S5_GROUP
S5_N = 64

HY_WIDTH = 512
HY_ORDER = 2
HY_TAPS = 3
HY_BANDS = 16
HY_EMB = 2 * HY_BANDS + 1
HY_FH = 64
HY_DECAY_MIN = 3.07
HY_DECAY_MAX = 15.35

DF_HEADS = 8
DF_DH = 32
DF_V = 2 * DF_DH

IN_A = MLA_Q_RANK + MLA_KV_RANK + MLA_ROPE + S5_WIDTH
MIX_A = MLA_HEADS * MLA_V + S5_WIDTH
SPLIT_A = (MLA_Q_RANK, MLA_Q_RANK + MLA_KV_RANK, MLA_Q_RANK + MLA_KV_RANK + MLA_ROPE)
IN_B = 3 * HY_WIDTH + DF_HEADS * (4 * DF_DH + DF_V)
MIX_B = HY_WIDTH + DF_HEADS * DF_V
SPLIT_B = (3 * HY_WIDTH, 3 * HY_WIDTH + DF_HEADS * 2 * DF_DH, 3 * HY_WIDTH + DF_HEADS * 4 * DF_DH)

kernel_name = 'hybrid_mla_s5_hyena_diffattn_prefix_dit'


def rmsnorm(x, g):
    xf = x.astype(jnp.float32)
    y = xf * lax.rsqrt(jnp.mean(xf * xf, axis=-1, keepdims=True) + EPS)
    return (y * g.astype(jnp.float32)).astype(x.dtype)


def adaln(cvec, w, b):
    return (jax.nn.silu(cvec) @ w + b).reshape(cvec.shape[0], N_MOD, D_MODEL)


def modulate(x, g, shift, scale):
    return rmsnorm(x, g) * (1.0 + scale[:, None]) + shift[:, None]


def swiglu(h, w_in, w_out):
    gate, up = jnp.split(h @ w_in, 2, axis=-1)
    return (jax.nn.silu(gate) * up) @ w_out


def half_ffn(y, m, g, w_in, w_out, base):
    h = modulate(y, g, m[:, base], m[:, base + 1])
    return y + 0.5 * m[:, base + 2, None] * swiglu(h, w_in, w_out)


def grid_rope(rows, dim):
    n_freq = dim // 4
    inv = 1.0 / (ROPE_BASE ** (jnp.arange(n_freq, dtype=jnp.float32) / n_freq))
    row = jnp.repeat(jnp.arange(rows, dtype=jnp.float32), GRID_W)
    col = (jnp.arange(rows * GRID_W) % GRID_W).astype(jnp.float32)
    ang = jnp.concatenate([row[:, None] * inv, col[:, None] * inv], axis=-1)
    return jnp.cos(ang), jnp.sin(ang)


def apply_rope(x, cos, sin):
    shp = (x.shape[1],) + (1,) * (x.ndim - 3) + (cos.shape[-1],)
    cos = cos.reshape(shp).astype(x.dtype)
    sin = sin.reshape(shp).astype(x.dtype)
    x1, x2 = x[..., 0::2], x[..., 1::2]
    return jnp.stack([x1 * cos - x2 * sin, x1 * sin + x2 * cos], axis=-1).reshape(x.shape)


def map_query_blocks(fn, q):
    B, L = q.shape[:2]
    nb = L // Q_BLOCK
    qb = jnp.moveaxis(q.reshape((B, nb, Q_BLOCK) + q.shape[2:]), 1, 0)
    out = lax.map(fn, qb)
    return jnp.moveaxis(out, 0, 1).reshape((B, L) + out.shape[3:])


def attend(q, k, v):
    scale = q.shape[-1] ** -0.5

    def block(qb):
        s = jnp.einsum('bqhd,bkhd->bhqk', qb, k).astype(jnp.float32) * scale
        p = jax.nn.softmax(s, axis=-1).astype(v.dtype)
        return jnp.einsum('bhqk,bkhd->bqhd', p, v)

    return map_query_blocks(block, q)


def diff_attend(q, k, v, lam):
    scale = DF_DH ** -0.5

    def block(qb):
        s = jnp.einsum('bqhid,bkhid->ibhqk', qb, k).astype(jnp.float32) * scale
        p = jax.nn.softmax(s, axis=-1)
        w = (p[0] - lam * p[1]).astype(v.dtype)
        return jnp.einsum('bhqk,bkhd->bqhd', w, v)

    return map_query_blocks(block, q)


def mla_queries(q_lat, q_norm, w_uq, rope):
    B, L = q_lat.shape[:2]
    q = (rmsnorm(q_lat, q_norm) @ w_uq).reshape(B, L, MLA_HEADS, MLA_NOPE + MLA_ROPE)
    if rope is not None:
        q = jnp.concatenate([q[..., :MLA_NOPE], apply_rope(q[..., MLA_NOPE:], *rope)], axis=-1)
    return q


def mla_keys_values(ckv, krope, w_ukv):
    B, L = ckv.shape[:2]
    kv = (ckv @ w_ukv).reshape(B, L, MLA_HEADS, MLA_NOPE + MLA_V)
    kr = jnp.broadcast_to(krope[:, :, None, :], (B, L, MLA_HEADS, MLA_ROPE))
    return jnp.concatenate([kv[..., :MLA_NOPE], kr], axis=-1), kv[..., MLA_NOPE:]


def _s5_combine(e1, e2):
    a1r, a1i, b1r, b1i = e1
    a2r, a2i, b2r, b2i = e2
    return (a2r * a1r - a2i * a1i, a2r * a1i + a2i * a1r,
            a2r * b1r - a2i * b1i + b2r, a2r * b1i + a2i * b1r + b2i)


def s5_scan(ug, lam_re, lam_im, log_step, b_re, b_im, c_re, c_im, h0, reverse):
    f32 = jnp.float32
    L = ug.shape[1]
    step = jnp.exp(log_step.astype(f32))[:, None]
    lr = jnp.minimum(lam_re.astype(f32), -1e-4)
    li = lam_im.astype(f32)
    mag = jnp.exp(lr * step)
    ar, ai = mag * jnp.cos(li * step), mag * jnp.sin(li * step)
    den = lr * lr + li * li
    fr = ((ar - 1.0) * lr + ai * li) / den
    fi = (ai * lr - (ar - 1.0) * li) / den
    br, bi = b_re.astype(f32), b_im.astype(f32)
    bbr = fr[..., None] * br - fi[..., None] * bi
    bbi = fr[..., None] * bi + fi[..., None] * br
    xr = jnp.einsum('blgi,gni->blgn', ug, bbr)
    xi = jnp.einsum('blgi,gni->blgn', ug, bbi)
    if h0 is not None:
        t0 = L - 1 if reverse else 0
        xr = xr.at[:, t0].add(ar * h0[:, 0] - ai * h0[:, 1])
        xi = xi.at[:, t0].add(ar * h0[:, 1] + ai * h0[:, 0])
    elems = (jnp.broadcast_to(ar, xr.shape), jnp.broadcast_to(ai, xr.shape), xr, xi)
    _, _, hr, hi = lax.associative_scan(_s5_combine, elems, axis=1, reverse=reverse)
    t_end = 0 if reverse else L - 1
    y = (jnp.einsum('blgn,gin->blgi', hr, c_re.astype(f32))
         - jnp.einsum('blgn,gin->blgi', hi, c_im.astype(f32)))
    return y, jnp.stack([hr[:, t_end], hi[:, t_end]], axis=1)


def s5_mixer(u, ps5, h0):
    a_re, a_im, log_step, b_re, b_im, c_re, c_im, d, w_glu = ps5
    B, L = u.shape[:2]
    uf = u.astype(jnp.float32)
    ug = uf.reshape(B, L, S5_GROUPS, S5_GROUP)
    y = d.astype(jnp.float32) * uf
    finals = []
    for dr in range(2):
        h0_dir = None if h0 is None else h0[:, dr].astype(jnp.float32)
        yd, hf = s5_scan(ug, a_re[dr], a_im[dr], log_step[dr], b_re[dr], b_im[dr],
                         c_re[dr], c_im[dr], h0_dir, dr == 1)
        y = y + yd.reshape(B, L, S5_WIDTH)
        finals.append(hf)
    y = jax.nn.gelu(y)
    y = y * jax.nn.sigmoid(y @ w_glu.astype(jnp.float32))
    return y.astype(u.dtype), jnp.stack(finals, axis=1).astype(u.dtype)


def short_conv(u, w):
    L = u.shape[1]
    pad = HY_TAPS // 2
    up = jnp.pad(u, ((0, 0), (pad, pad), (0, 0)))
    return sum(w[j] * up[:, j:j + L] for j in range(HY_TAPS))


def hyena_filters(L, w1, b1, w2, b2, freq, w3, decay):
    f32 = jnp.float32
    t = jnp.arange(L, dtype=f32) / L
    bands = jnp.arange(1, HY_BANDS + 1, dtype=f32)
    ang = 2.0 * math.pi * t[:, None] * bands
    feat = jnp.concatenate([t[:, None], jnp.cos(ang), jnp.sin(ang)], axis=-1)
    fq = freq.astype(f32)
    h = jnp.sin(fq * (feat @ w1.astype(f32) + b1.astype(f32)))
    h = jnp.sin(fq * (h @ w2.astype(f32) + b2.astype(f32)))
    h = (h @ w3.astype(f32)).reshape(L, 2, HY_ORDER, HY_WIDTH)
    window = jnp.exp(-t[:, None] * jnp.abs(decay.astype(f32)))
    return h * window[:, None, None, :]


def long_conv(z, hf, hb):
    L = z.shape[1]
    k = jnp.concatenate([hf, jnp.zeros_like(hf[:1]), hb[:0:-1]], axis=0)
    kf = jnp.fft.rfft(k, axis=0)
    zf = jnp.fft.rfft(z, n=2 * L, axis=1)
    return jnp.fft.irfft(zf * kf[None], n=2 * L, axis=1)[:, :L]


def hyena(u, conv_w, w1, b1, w2, b2, freq, w3, decay, bias):
    L = u.shape[1]
    u = short_conv(u, conv_w).astype(jnp.float32)
    v, x1, x2 = jnp.split(u, 3, axis=-1)
    h = hyena_filters(L, w1, b1, w2, b2, freq, w3, decay)
    z = v
    for n, gate in enumerate((x1, x2)):
        z = gate * (long_conv(z, h[:, 0, n], h[:, 1, n]) + bias[n].astype(jnp.float32) * z)
    return z


def even_mixer(h, ctx_ckv, ctx_krope, ctx_state, pa, ps5, rope):
    w_in, w_out, q_norm, w_uq, kv_norm, w_ukv = pa
    B, L = h.shape[:2]
    q_lat, ckv_raw, kr, u = jnp.split(h @ w_in, SPLIT_A, axis=-1)
    q = mla_queries(q_lat, q_norm, w_uq, rope)
    ckv = rmsnorm(ckv_raw, kv_norm)
    if rope is not None:
        kr = apply_rope(kr[:, :, None, :], *rope)[:, :, 0]
    if ctx_ckv is None:
        k, v = mla_keys_values(ckv, kr, w_ukv)
    else:
        k, v = mla_keys_values(jnp.concatenate([ctx_ckv, ckv], axis=1),
                               jnp.concatenate([ctx_krope, kr], axis=1), w_ukv)
    att = attend(q, k, v).reshape(B, L, MLA_HEADS * MLA_V)
    s5y, s5_state = s5_mixer(u, ps5, ctx_state)
    out = (jnp.concatenate([att, s5y.astype(att.dtype)], axis=-1) @ w_out).astype(h.dtype)
    return out, ckv, kr, s5_state


def odd_mixer(h, ctx_k, ctx_v, pb, phy, rope):
    w_in, w_out, lam_p, subln, lam_init = pb
    B, L = h.shape[:2]
    hy_u, q, k, v = jnp.split(h @ w_in, SPLIT_B, axis=-1)
    hy = hyena(hy_u, *phy)
    q = q.reshape(B, L, DF_HEADS, 2, DF_DH)
    k = k.reshape(B, L, DF_HEADS, 2, DF_DH)
    v = v.reshape(B, L, DF_HEADS, DF_V)
    if rope is not None:
        q = apply_rope(q, *rope)
        k = apply_rope(k, *rope)
    if ctx_k is None:
        k_all, v_all = k, v
    else:
        k_all = jnp.concatenate([ctx_k, k], axis=1)
        v_all = jnp.concatenate([ctx_v, v], axis=1)
    lp = lam_p.astype(jnp.float32)
    lam = jnp.exp(jnp.sum(lp[0] * lp[1])) - jnp.exp(jnp.sum(lp[2] * lp[3])) + lam_init
    att = diff_attend(q, k_all, v_all, lam)
    att = (rmsnorm(att, subln) * (1.0 - lam_init)).reshape(B, L, DF_HEADS * DF_V)
    out = (jnp.concatenate([hy.astype(att.dtype), att], axis=-1) @ w_out).astype(h.dtype)
    return out, k, v


def setup_inputs(seed: int = 0):
    key = jax.random.key(seed)
    keys = iter(jax.random.split(key, 64))
    f32 = jnp.float32

    def nrm(shape, scale):
        return jax.random.normal(next(keys), shape, f32) * scale

    def gain(shape):
        return 1.0 + nrm(shape, 0.02)

    def unif(shape, lo, hi):
        return jax.random.uniform(next(keys), shape, f32, lo, hi)

    n_idx = jnp.arange(S5_N, dtype=f32)
    return {
        'x_prompt': nrm((BATCH, SEQ, D_MODEL), 1.0),
        'x_sample': nrm((DEC_BATCH, DEC_SEQ, D_MODEL), 1.0),
        'c': nrm((DEC_BATCH, D_MODEL), 1.0),
        'c_ctx': nrm((D_MODEL,), 1.0),
        'cache_mla_ckv': nrm((DEC_BATCH, N_EVEN, PAST_LEN, MLA_KV_RANK), 1.0),
        'cache_mla_krope': nrm((DEC_BATCH, N_EVEN, PAST_LEN, MLA_ROPE), 1.0),
        'state_s5': nrm((DEC_BATCH, N_EVEN, 2, 2, S5_GROUPS, S5_N), 0.3),
        'cache_diff_k': nrm((DEC_BATCH, N_ODD, PAST_LEN, DF_HEADS, 2, DF_DH), 1.0),
        'cache_diff_v': nrm((DEC_BATCH, N_ODD, PAST_LEN, DF_HEADS, DF_V), 1.0),
        'ada_w': nrm((DEPTH, D_MODEL, N_MOD * D_MODEL), 0.5 * D_MODEL ** -0.5),
        'ada_b': nrm((DEPTH, N_MOD * D_MODEL), 0.02),
        'norm_g': gain((DEPTH, 3, D_MODEL)),
        'ff_w_in': nrm((DEPTH, 2, D_MODEL, 2 * FF_DIM), D_MODEL ** -0.5),
        'ff_w_out': nrm((DEPTH, 2, FF_DIM, D_MODEL), FF_DIM ** -0.5),
        'w_in_a': nrm((N_EVEN, D_MODEL, IN_A), D_MODEL ** -0.5),
        'w_out_a': nrm((N_EVEN, MIX_A, D_MODEL), MIX_A ** -0.5),
        'mla_q_norm': gain((N_EVEN, MLA_Q_RANK)),
        'mla_w_uq': nrm((N_EVEN, MLA_Q_RANK, MLA_HEADS * (MLA_NOPE + MLA_ROPE)), MLA_Q_RANK ** -0.5),
        'mla_kv_norm': gain((N_EVEN, MLA_KV_RANK)),
        'mla_w_ukv': nrm((N_EVEN, MLA_KV_RANK, MLA_HEADS * (MLA_NOPE + MLA_V)), MLA_KV_RANK ** -0.5),
        's5_a_re': -0.5 + nrm((N_EVEN, 2, S5_GROUPS, S5_N), 0.01),
        's5_a_im': math.pi * n_idx + nrm((N_EVEN, 2, S5_GROUPS, S5_N), 0.01),
        's5_log_step': unif((N_EVEN, 2, S5_GROUPS), math.log(1e-3), math.log(1e-1)),
        's5_b_re': nrm((N_EVEN, 2, S5_GROUPS, S5_N, S5_GROUP), (2.0 * S5_GROUP) ** -0.5),
        's5_b_im': nrm((N_EVEN, 2, S5_GROUPS, S5_N, S5_GROUP), (2.0 * S5_GROUP) ** -0.5),
        's5_c_re': nrm((N_EVEN, 2, S5_GROUPS, S5_GROUP, S5_N), S5_N ** -0.5),
        's5_c_im': nrm((N_EVEN, 2, S5_GROUPS, S5_GROUP, S5_N), S5_N ** -0.5),
        's5_d': nrm((N_EVEN, S5_WIDTH), 0.5),
        's5_w_glu': nrm((N_EVEN, S5_WIDTH, S5_WIDTH), S5_WIDTH ** -0.5),
        'w_in_b': nrm((N_ODD, D_MODEL, IN_B), D_MODEL ** -0.5),
        'w_out_b': nrm((N_ODD, MIX_B, D_MODEL), MIX_B ** -0.5),
        'hy_conv': nrm((N_ODD, HY_TAPS, 3 * HY_WIDTH), 0.5),
        'hy_w1': nrm((N_ODD, HY_EMB, HY_FH), HY_EMB ** -0.5),
        'hy_b1': nrm((N_ODD, HY_FH), 0.02),
        'hy_w2': nrm((N_ODD, HY_FH, HY_FH), HY_FH ** -0.5),
        'hy_b2': nrm((N_ODD, HY_FH), 0.02),
        'hy_freq': gain((N_ODD, HY_FH)),
        'hy_w3': nrm((N_ODD, HY_FH, 2 * HY_ORDER * HY_WIDTH), 0.1 * HY_FH ** -0.5),
        'hy_decay': unif((N_ODD, HY_WIDTH), HY_DECAY_MIN, HY_DECAY_MAX),
        'hy_bias': nrm((N_ODD, HY_ORDER, HY_WIDTH), 0.5),
        'df_lambda': nrm((N_ODD, 4, DF_DH), 0.1),
        'df_subln': gain((N_ODD, DF_V)),
        'final_norm': gain((D_MODEL,)),
    }


def reference(x_prompt, x_sample, c, c_ctx, cache_mla_ckv, cache_mla_krope, state_s5,
              cache_diff_k, cache_diff_v, ada_w, ada_b, norm_g, ff_w_in, ff_w_out,
              w_in_a, w_out_a, mla_q_norm, mla_w_uq, mla_kv_norm, mla_w_ukv,
              s5_a_re, s5_a_im, s5_log_step, s5_b_re, s5_b_im, s5_c_re, s5_c_im, s5_d, s5_w_glu,
              w_in_b, w_out_b, hy_conv, hy_w1, hy_b1, hy_w2, hy_b2, hy_freq, hy_w3, hy_decay, hy_bias,
              df_lambda, df_subln, final_norm):
    rows = x_sample.shape[1] // GRID_W
    rope_mla = grid_rope(rows, MLA_ROPE)
    rope_df = grid_rope(rows, DF_DH)
    yp, ys = x_prompt, x_sample
    new_ckv, new_krope, new_s5, new_dk, new_dv = [], [], [], [], []
    for l in range(DEPTH):
        m_ctx = adaln(c_ctx[None], ada_w[l], ada_b[l])
        m_lat = adaln(c, ada_w[l], ada_b[l])
        yp = half_ffn(yp, m_ctx, norm_g[l, 0], ff_w_in[l, 0], ff_w_out[l, 0], 0)
        ys = half_ffn(ys, m_lat, norm_g[l, 0], ff_w_in[l, 0], ff_w_out[l, 0], 0)
        hp = modulate(yp, norm_g[l, 1], m_ctx[:, 3], m_ctx[:, 4])
        hs = modulate(ys, norm_g[l, 1], m_lat[:, 3], m_lat[:, 4])
        if l % 2 == 0:
            e = l // 2
            pa = (w_in_a[e], w_out_a[e], mla_q_norm[e], mla_w_uq[e], mla_kv_norm[e], mla_w_ukv[e])
            ps5 = (s5_a_re[e], s5_a_im[e], s5_log_step[e], s5_b_re[e], s5_b_im[e],
                   s5_c_re[e], s5_c_im[e], s5_d[e], s5_w_glu[e])
            out_p, ckv, krope, st = even_mixer(hp, None, None, None, pa, ps5, None)
            out_s, _, _, _ = even_mixer(hs, cache_mla_ckv[:, e], cache_mla_krope[:, e],
                                        state_s5[:, e], pa, ps5, rope_mla)
            new_ckv.append(ckv)
            new_krope.append(krope)
            new_s5.append(st)
        else:
            o = l // 2
            lam_init = 0.8 - 0.6 * math.exp(-0.3 * l)
            pb = (w_in_b[o], w_out_b[o], df_lambda[o], df_subln[o], lam_init)
            phy = (hy_conv[o], hy_w1[o], hy_b1[o], hy_w2[o], hy_b2[o], hy_freq[o],
                   hy_w3[o], hy_decay[o], hy_bias[o])
            out_p, dk, dv = odd_mixer(hp, None, None, pb, phy, None)
            out_s, _, _ = odd_mixer(hs, cache_diff_k[:, o], cache_diff_v[:, o], pb, phy, rope_df)
            new_dk.append(dk)
            new_dv.append(dv)
        yp = yp + m_ctx[:, 5, None] * out_p
        ys = ys + m_lat[:, 5, None] * out_s
        yp = half_ffn(yp, m_ctx, norm_g[l, 2], ff_w_in[l, 1], ff_w_out[l, 1], 6)
        ys = half_ffn(ys, m_lat, norm_g[l, 2], ff_w_in[l, 1], ff_w_out[l, 1], 6)
    y_prompt = rmsnorm(yp, final_norm)
    y_sample = rmsnorm(ys, final_norm)
    return (y_prompt, y_sample, jnp.stack(new_ckv, axis=1), jnp.stack(new_krope, axis=1),
            jnp.stack(new_s5, axis=1), jnp.stack(new_dk, axis=1), jnp.stack(new_dv, axis=1))
```

```python
import functools
import math

import ml_dtypes
import numpy as np
import jax
import jax.numpy as jnp
from jax import lax
from jax.experimental import pallas as pl
from jax.experimental.pallas import tpu as pltpu

F32 = jnp.float32
BF16 = jnp.bfloat16

D = 1024
NB_P, L_P = 16, 256
NB_S, L_S = 2, 1024
PAST = 256
GRID_W = 64
N_MOD = 9
FF = 2816
EPS = 1e-6
ROPE_BASE = 10000.0

MLA_HEADS, MLA_NOPE, MLA_ROPE, MLA_V = 8, 64, 32, 64
MLA_Q_RANK, MLA_KV_RANK = 384, 256
S5_WIDTH, S5_GROUP, S5_N = 512, 16, 64
S5_GROUPS = S5_WIDTH // S5_GROUP
HY_WIDTH, HY_BANDS, HY_FH = 512, 16, 64
HY_EMB = 2 * HY_BANDS + 1
DF_HEADS, DF_DH = 8, 32
DF_V = 2 * DF_DH

TOK_P = NB_P * L_P
TOK_S = NB_S * L_S
TOK = TOK_P + TOK_S
TM = 512
NT = TOK // TM
NT_P = TOK_P // TM
TILES_PER_SAMPLE = L_S // TM

LANES = 128
S5_T = 16
S5_CW = S5_T * S5_GROUP
CH_P = L_P // S5_T
CH_S = L_S // S5_T
S5_ROWS = NB_P * CH_P + NB_S * CH_S
S5_ROWS_P = NB_P * CH_P

VMEM_LIMIT = 56 * 1024 * 1024


def _params(n_grid, vmem=None):
    return pltpu.CompilerParams(dimension_semantics=("arbitrary",) * n_grid,
                                vmem_limit_bytes=vmem)


def _const_spec(shape):
    nd = len(shape)
    return pl.BlockSpec(shape, lambda *_: (0,) * nd, pipeline_mode=pl.Buffered(1))


def _mod_index(i):
    return jnp.where(i < NT_P, 0, 1 + (i - NT_P) // TILES_PER_SAMPLE)


def _pos_index(i):
    return jnp.where(i < NT_P, 0, 1 + (i - NT_P) % TILES_PER_SAMPLE)


def _dot(a, b):
    return jnp.dot(a.astype(BF16), b.astype(BF16), preferred_element_type=F32)


def _dot_nt(a, b):
    return lax.dot_general(a, b, (((1,), (1,)), ((), ())), preferred_element_type=F32)


def _split(x):
    hi = x.astype(BF16)
    lo = (x - hi.astype(F32)).astype(BF16)
    return hi, lo


def _dot3(a, b):
    ah, al = _split(a)
    bh, bl = _split(b)
    d = functools.partial(jnp.dot, preferred_element_type=F32)
    return d(ah, bh) + d(ah, bl) + d(al, bh)


def _dot3_const(ch, cl, x):
    xh, xl = _split(x)
    d = functools.partial(jnp.dot, preferred_element_type=F32)
    return d(ch, xh) + d(ch, xl) + d(cl, xh)


def _rmsnorm(x, g):
    return x * lax.rsqrt(jnp.mean(x * x, axis=-1, keepdims=True) + EPS) * g


def _modulate(y, g, shift, scale):
    return _rmsnorm(y, g) * (1.0 + scale) + shift


def _pair_swap(x):
    n = x.shape[-1]
    lane = lax.broadcasted_iota(jnp.int32, x.shape, x.ndim - 1)
    return jnp.where((lane & 1) == 0, pltpu.roll(x, n - 1, x.ndim - 1), pltpu.roll(x, 1, x.ndim - 1))


def _rope(x, cos, sin_signed):
    return x * cos + _pair_swap(x) * sin_signed


def _rope_angles():
    n_freq = MLA_ROPE // 4
    inv = 1.0 / (ROPE_BASE ** (np.arange(n_freq, dtype=np.float64) / n_freq))
    pos = np.arange(L_S)
    row = (pos // GRID_W).astype(np.float64)
    col = (pos % GRID_W).astype(np.float64)
    ang = np.concatenate([row[:, None] * inv, col[:, None] * inv], axis=-1)
    return np.cos(ang), np.sin(ang)


@functools.lru_cache(maxsize=None)
def _rope_tables(width, starts):
    cos, sin = _rope_angles()
    c = np.ones((TM + L_S, width), np.float32)
    s = np.zeros((TM + L_S, width), np.float32)
    sign = np.where(np.arange(MLA_ROPE) % 2 == 0, -1.0, 1.0)
    unit_c = np.repeat(cos, 2, axis=1)
    unit_s = np.repeat(sin, 2, axis=1) * sign
    for st in starts:
        c[TM:, st:st + MLA_ROPE] = unit_c
        s[TM:, st:st + MLA_ROPE] = unit_s
    return c, s


@functools.lru_cache(maxsize=None)
def _dft_tables(L):
    f = np.arange(L)[:, None]
    s = np.arange(L)[None, :]
    ang = np.pi * ((f * s) % (2 * L)).astype(np.float64) / L
    cs = np.concatenate([np.cos(ang), np.sin(ang)], axis=0)
    cs[L, :] = np.where(np.arange(L) % 2 == 0, 1.0, -1.0)
    hi = cs.astype(ml_dtypes.bfloat16)
    lo = (cs - hi.astype(np.float64)).astype(ml_dtypes.bfloat16)
    return hi, lo, np.ascontiguousarray(hi.T), np.ascontiguousarray(lo.T)


@functools.lru_cache(maxsize=None)
def _hyena_features(L):
    t = np.arange(L, dtype=np.float64) / L
    bands = np.arange(1, HY_BANDS + 1, dtype=np.float64)
    ang = 2.0 * math.pi * t[:, None] * bands
    feat = np.zeros((L, LANES), np.float32)
    feat[:, 0] = t
    feat[:, 1:1 + HY_BANDS] = np.cos(ang)
    feat[:, 1 + HY_BANDS:HY_EMB] = np.sin(ang)
    return feat


def _adaln_kernel(c_ref, w_ref, b_ref, o_ref):
    w = w_ref[0]
    o_ref[0] = jnp.zeros(o_ref.shape[1:], F32)
    for m in range(c_ref.shape[0]):
        col = jax.nn.silu(c_ref[m])
        o_ref[0, m:m + 1, :] = jnp.sum(w * col, axis=0, keepdims=True) + b_ref[0]


def _adaln(cvecs, ada_w, ada_b):
    depth = ada_w.shape[0]
    n_vec = cvecs.shape[0]
    tn = D
    out = pl.pallas_call(
        _adaln_kernel,
        out_shape=jax.ShapeDtypeStruct((depth, 8, N_MOD * D), F32),
        grid=(depth, N_MOD),
        in_specs=[pl.BlockSpec((n_vec, D, 1), lambda l, j: (0, 0, 0)),
                  pl.BlockSpec((1, D, tn), lambda l, j: (l, 0, j)),
                  pl.BlockSpec((1, 1, tn), lambda l, j: (l, 0, j))],
        out_specs=pl.BlockSpec((1, 8, tn), lambda l, j: (l, 0, j)),
        compiler_params=_params(2),
        name="adaln",
    )(cvecs[:, :, None], ada_w, ada_b[:, None, :])
    return out[:, :n_vec].reshape(depth, n_vec, N_MOD, D)


FF_CHUNK = FF // 2


def _ffn_kernel(base, final, y_ref, mod_ref, g_ref, win_ref, wout_ref, fg_ref, o_ref):
    y = y_ref[...]
    mod = mod_ref[0]
    h = _modulate(y, g_ref[...], mod[base:base + 1], mod[base + 1:base + 2]).astype(BF16)
    acc = jnp.zeros(y.shape, F32)
    for c in range(FF // FF_CHUNK):
        lo = c * FF_CHUNK
        gate = jnp.dot(h, win_ref[:, lo:lo + FF_CHUNK], preferred_element_type=F32)
        up = jnp.dot(h, win_ref[:, FF + lo:FF + lo + FF_CHUNK], preferred_element_type=F32)
        a = (jax.nn.silu(gate) * up).astype(BF16)
        acc = acc + jnp.dot(a, wout_ref[lo:lo + FF_CHUNK, :], preferred_element_type=F32)
    out = y + 0.5 * mod[base + 2:base + 3] * acc
    if final:
        out = _rmsnorm(out, fg_ref[...])
    o_ref[...] = out


def _half_ffn(y, mods_l, g, w_in, w_out, base, final_g=None):
    final = final_g is not None
    fg = final_g if final else g
    return pl.pallas_call(
        functools.partial(_ffn_kernel, base, final),
        out_shape=jax.ShapeDtypeStruct((TOK, D), F32),
        grid=(NT,),
        in_specs=[pl.BlockSpec((TM, D), lambda i: (i, 0)),
                  pl.BlockSpec((1, N_MOD, D), lambda i: (_mod_index(i), 0, 0)),
                  _const_spec((1, D)),
                  _const_spec((D, 2 * FF)),
                  _const_spec((FF, D)),
                  _const_spec((1, D))],
        out_specs=pl.BlockSpec((TM, D), lambda i: (i, 0)),
        compiler_params=_params(1, VMEM_LIMIT),
        name="half_ffn",
    )(y, mods_l, g[None], w_in.astype(BF16), w_out.astype(BF16), fg[None])


def _linear_kernel(x_ref, w_ref, o_ref):
    o_ref[...] = _dot(x_ref[...], w_ref[...])


def _linear(x, w, tm):
    m, k = x.shape
    n = w.shape[1]
    return pl.pallas_call(
        _linear_kernel,
        out_shape=jax.ShapeDtypeStruct((m, n), F32),
        grid=(m // tm,),
        in_specs=[pl.BlockSpec((tm, k), lambda i: (i, 0)), _const_spec((k, n))],
        out_specs=pl.BlockSpec((tm, n), lambda i: (i, 0)),
        compiler_params=_params(1),
        name="linear",
    )(x, w.astype(BF16))


QW = MLA_HEADS * LANES
KR_AT = MLA_NOPE
IN_A_PAD = MLA_Q_RANK + MLA_KV_RANK + S5_WIDTH + LANES


def _inproj_a_kernel(y_ref, mod_ref, g_ref, win_ref, qn_ref, wuq_ref, kvn_ref, wk_ref, wv_ref,
                     cq_ref, sq_ref, ck_ref, sk_ref,
                     q_ref, ckv_ref, kru_ref, krr_ref, kn_ref, v_ref, u_ref):
    mod = mod_ref[0]
    h = _modulate(y_ref[...], g_ref[...], mod[3:4], mod[4:5]).astype(BF16)
    p = jnp.dot(h, win_ref[...], preferred_element_type=F32)
    o1 = MLA_Q_RANK
    o2 = o1 + MLA_KV_RANK
    o3 = o2 + S5_WIDTH
    q = _dot(_rmsnorm(p[:, :o1], qn_ref[...]), wuq_ref[...])
    q_ref[...] = _rope(q, cq_ref[...], sq_ref[...])
    ckv = _rmsnorm(p[:, o1:o2], kvn_ref[...])
    ckv_ref[...] = ckv
    ckv_b = ckv.astype(BF16)
    kn_ref[...] = jnp.dot(ckv_b, wk_ref[...], preferred_element_type=F32)
    v_ref[...] = jnp.dot(ckv_b, wv_ref[...], preferred_element_type=F32)
    u_ref[...] = p[:, o2:o3]
    krp = p[:, o3:]
    kru_ref[...] = krp
    krr_ref[...] = _rope(krp, ck_ref[...], sk_ref[...])


def _inproj_a(y, mods_l, g, w_in, q_norm, w_uq, kv_norm, w_ukv):
    o1 = MLA_Q_RANK
    o2 = o1 + MLA_KV_RANK
    o3 = o2 + MLA_ROPE
    kr_cols = jnp.pad(w_in[:, o2:o3], ((0, 0), (KR_AT, LANES - KR_AT - MLA_ROPE)))
    w_ext = jnp.concatenate([w_in[:, :o2], w_in[:, o3:], kr_cols], axis=1).astype(BF16)
    dq = MLA_NOPE + MLA_ROPE
    w_uq_pad = jnp.pad(w_uq.reshape(MLA_Q_RANK, MLA_HEADS, dq),
                       ((0, 0), (0, 0), (0, LANES - dq))).reshape(MLA_Q_RANK, QW).astype(BF16)
    w_kv = w_ukv.reshape(MLA_KV_RANK, MLA_HEADS, MLA_NOPE + MLA_V)
    w_k = jnp.pad(w_kv[:, :, :MLA_NOPE], ((0, 0), (0, 0), (0, LANES - MLA_NOPE))).reshape(MLA_KV_RANK, QW)
    w_v = w_kv[:, :, MLA_NOPE:].reshape(MLA_KV_RANK, MLA_HEADS * MLA_V)
    w_k, w_v = w_k.astype(BF16), w_v.astype(BF16)
    cq, sq = _rope_tables(QW, tuple(h * LANES + MLA_NOPE for h in range(MLA_HEADS)))
    ck, sk = _rope_tables(LANES, (KR_AT,))
    row = lambda i: (i, 0)
    pos = lambda i: (_pos_index(i), 0)
    widths = (QW, MLA_KV_RANK, LANES, LANES, QW, MLA_HEADS * MLA_V, S5_WIDTH)
    outs = pl.pallas_call(
        _inproj_a_kernel,
        out_shape=[jax.ShapeDtypeStruct((TOK, w), F32) for w in widths],
        grid=(NT,),
        in_specs=[pl.BlockSpec((TM, D), row),
                  pl.BlockSpec((1, N_MOD, D), lambda i: (_mod_index(i), 0, 0)),
                  _const_spec((1, D)),
                  _const_spec((D, IN_A_PAD)),
                  _const_spec((1, MLA_Q_RANK)),
                  _const_spec((MLA_Q_RANK, QW)),
                  _const_spec((1, MLA_KV_RANK)),
                  _const_spec((MLA_KV_RANK, QW)),
                  _const_spec((MLA_KV_RANK, MLA_HEADS * MLA_V)),
                  pl.BlockSpec((TM, QW), pos), pl.BlockSpec((TM, QW), pos),
                  pl.BlockSpec((TM, LANES), pos), pl.BlockSpec((TM, LANES), pos)],
        out_specs=[pl.BlockSpec((TM, w), row) for w in widths],
        compiler_params=_params(1, VMEM_LIMIT),
        name="inproj_even",
    )(y, mods_l, g[None], w_ext, q_norm[None], w_uq_pad, kv_norm[None], w_k, w_v,
      jnp.asarray(cq), jnp.asarray(sq), jnp.asarray(ck), jnp.asarray(sk))
    q, ckv, kr_unrot, kr_rot, kn, v, u = outs
    return q, ckv, kr_unrot, kr_rot, kn, v, u, (w_k, w_v)


def _mla_attn_kernel(nseg, q_ref, *refs):
    o_ref = refs[-1]
    tq = q_ref.shape[0]
    scale = (MLA_NOPE + MLA_ROPE) ** -0.5
    lane = lax.broadcasted_iota(jnp.int32, (tq, LANES), 1)
    for pair in range(MLA_HEADS // 2):
        outs = []
        for hh in range(2):
            h = 2 * pair + hh
            hs = slice(h * LANES, (h + 1) * LANES)
            qh = (q_ref[:, hs] * scale).astype(BF16)
            scores = []
            for s in range(nseg):
                kn_ref, kr_ref = refs[3 * s], refs[3 * s + 1]
                kh = (kn_ref[:, hs] + kr_ref[...]).astype(BF16)
                scores.append(_dot_nt(qh, kh))
            m = functools.reduce(jnp.maximum, [jnp.max(s, axis=-1, keepdims=True) for s in scores])
            es = [jnp.exp(s - m) for s in scores]
            l = functools.reduce(jnp.add, [jnp.sum(e, axis=-1, keepdims=True) for e in es])
            o = None
            for s in range(nseg):
                v_ref = refs[3 * s + 2]
                part = _dot(es[s], v_ref[:, pair * LANES:(pair + 1) * LANES])
                o = part if o is None else o + part
            outs.append(o / l)
        o_ref[:, pair * LANES:(pair + 1) * LANES] = jnp.where(lane < MLA_V, outs[0], outs[1])


def _mla_attention(q, kn, kr, v, n_batch, seq, tq, row0, ctx=None):
    qt = seq // tq
    qb0, kb0 = row0 // tq, row0 // seq
    in_specs = [pl.BlockSpec((tq, QW), lambda b, j: (qb0 + b * qt + j, 0))]
    args = [q]
    segs = []
    if ctx is not None:
        segs.append((ctx, PAST, 0))
    segs.append(((kn, kr, v), seq, kb0))
    for (a_kn, a_kr, a_v), ln, off in segs:
        idx = lambda b, j, off=off: (off + b, 0)
        in_specs += [pl.BlockSpec((ln, QW), idx), pl.BlockSpec((ln, LANES), idx),
                     pl.BlockSpec((ln, MLA_HEADS * MLA_V), idx)]
        args += [a_kn, a_kr, a_v]
    return pl.pallas_call(
        functools.partial(_mla_attn_kernel, len(segs)),
        out_shape=jax.ShapeDtypeStruct((n_batch * seq, MLA_HEADS * MLA_V), F32),
        grid=(n_batch, qt),
        in_specs=in_specs,
        out_specs=pl.BlockSpec((tq, MLA_HEADS * MLA_V), lambda b, j: (b * qt + j, 0)),
        compiler_params=_params(2, VMEM_LIMIT),
        name="mla_attention",
    )(*args)


def _cpow(ar, ai, e, nbits):
    rr = jnp.ones_like(ar)
    ri = jnp.zeros_like(ar)
    br, bi = ar, ai
    for k in range(nbits):
        bit = ((e >> k) & 1) == 1
        nr = rr * br - ri * bi
        ni = rr * bi + ri * br
        rr = jnp.where(bit, nr, rr)
        ri = jnp.where(bit, ni, ri)
        if k + 1 < nbits:
            br, bi = br * br - bi * bi, 2.0 * br * bi
    return rr, ri


def _s5_abar(lam_re, lam_im, log_step):
    step = jnp.exp(log_step)
    lr = jnp.minimum(lam_re, -1e-4)
    mag = jnp.exp(lr * step)
    return lr, mag * jnp.cos(lam_im * step), mag * jnp.sin(lam_im * step)


def _s5_prep_kernel(lrr_ref, lir_ref, lrc_ref, lic_ref, ls_ref, btr_ref, bti_ref, ctr_ref, cti_ref,
                    wi_ref, ws_ref, wo_ref, ap_ref):
    n2 = 2 * S5_N
    lane_s = lax.broadcasted_iota(jnp.int32, (S5_CW, n2), 1)
    row_s = lax.broadcasted_iota(jnp.int32, (S5_CW, n2), 0) >> 4
    lane_o = lax.broadcasted_iota(jnp.int32, (n2, S5_CW), 1)
    blk_o = lane_o >> 4
    row_o = lax.broadcasted_iota(jnp.int32, (n2, S5_CW), 0)
    lane_k = lax.broadcasted_iota(jnp.int32, (S5_GROUP, S5_CW), 1)
    lane_b = lax.broadcasted_iota(jnp.int32, (S5_GROUP, n2), 1)
    lane_a = lax.broadcasted_iota(jnp.int32, (1, n2), 1)
    intra = [None] * S5_T
    for d in range(2):
        ls = ls_ref[d, 0]
        lr, ar, ai = _s5_abar(lrr_ref[d, 0], lir_ref[d, 0], ls)
        li = lir_ref[d, 0]
        den = lr * lr + li * li
        fr = ((ar - 1.0) * lr + ai * li) / den
        fi = (ai * lr - (ar - 1.0) * li) / den
        btr, bti = btr_ref[d, 0], bti_ref[d, 0]
        bbr = fr * btr - fi * bti
        bbi = fr * bti + fi * btr
        e_state = (S5_T - 1 - row_s) if d == 0 else row_s
        pr, pi = _cpow(jnp.broadcast_to(ar, (S5_CW, n2)), jnp.broadcast_to(ai, (S5_CW, n2)), e_state, 4)
        ws_ref[d, 0] = jnp.where(lane_s < S5_N, pr * bbr - pi * bbi, pr * bbi + pi * bbr).astype(BF16)

        _, arc, aic = _s5_abar(lrc_ref[d, 0], lic_ref[d, 0], ls)
        arc = jnp.broadcast_to(arc, (n2, S5_CW))
        aic = jnp.broadcast_to(aic, (n2, S5_CW))
        ctr, cti = ctr_ref[d, 0], cti_ref[d, 0]
        e_out = (blk_o + 1) if d == 0 else (S5_T - blk_o)
        por, poi = _cpow(arc, aic, e_out, 5)
        wo_ref[d, 0] = jnp.where(row_o < S5_N, por * ctr - poi * cti, -(por * cti + poi * ctr)).astype(BF16)

        e_lag = blk_o if d == 0 else (S5_T - 1 - blk_o)
        pqr, pqi = _cpow(arc, aic, e_lag, 4)
        q_stack = jnp.where(row_o < S5_N, pqr * ctr - pqi * cti, pqr * cti + pqi * ctr)
        bb_mix = jnp.where(lane_b < S5_N, bbr[:S5_GROUP], -bbi[:S5_GROUP])
        kt = _dot3(bb_mix, q_stack)
        for s in range(S5_T):
            if d == 0:
                blk = jnp.where(lane_k >= S5_GROUP * s, pltpu.roll(kt, S5_GROUP * s, 1), 0.0)
            else:
                blk = jnp.where(lane_k < S5_GROUP * (s + 1),
                                pltpu.roll(kt, (S5_GROUP * (s + 1)) % S5_CW, 1), 0.0)
            intra[s] = blk if intra[s] is None else intra[s] + blk

        pr1, pi1 = ar, ai
        for _ in range(4):
            pr1, pi1 = pr1 * pr1 - pi1 * pi1, 2.0 * pr1 * pi1
        for k in range(6):
            ap_ref[d, 0, k:k + 1, :] = pr1
            ap_ref[d, 0, 8 + k:9 + k, :] = jnp.where(lane_a < S5_N, -pi1, pi1)
            pr1, pi1 = pr1 * pr1 - pi1 * pi1, 2.0 * pr1 * pi1
        ap_ref[d, 0, 6:8, :] = jnp.zeros((2, n2), F32)
        ap_ref[d, 0, 14:16, :] = jnp.zeros((2, n2), F32)
    for s in range(S5_T):
        wi_ref[0, s * S5_GROUP:(s + 1) * S5_GROUP, :] = intra[s].astype(BF16)


def _s5_prep(a_re, a_im, log_step, b_re, b_im, c_re, c_im):
    g, n2 = S5_GROUPS, 2 * S5_N
    dup_row = lambda x: jnp.concatenate([x, x], axis=-1)[:, :, None, :]
    dup_col = lambda x: jnp.concatenate([x, x], axis=-1)[:, :, :, None]
    bt = lambda b: jnp.tile(jnp.concatenate([jnp.swapaxes(b, 2, 3)] * 2, axis=-1), (1, 1, S5_T, 1))
    ct = lambda c: jnp.tile(jnp.concatenate([jnp.swapaxes(c, 2, 3)] * 2, axis=2), (1, 1, 1, S5_T))
    spec4 = lambda r, c: pl.BlockSpec((2, 1, r, c), lambda i: (0, i, 0, 0))
    return pl.pallas_call(
        _s5_prep_kernel,
        out_shape=[jax.ShapeDtypeStruct((g, S5_CW, S5_CW), BF16),
                   jax.ShapeDtypeStruct((2, g, S5_CW, n2), BF16),
                   jax.ShapeDtypeStruct((2, g, n2, S5_CW), BF16),
                   jax.ShapeDtypeStruct((2, g, 16, n2), F32)],
        grid=(g,),
        in_specs=[spec4(1, n2), spec4(1, n2), spec4(n2, 1), spec4(n2, 1), spec4(1, 1),
                  spec4(S5_CW, n2), spec4(S5_CW, n2), spec4(n2, S5_CW), spec4(n2, S5_CW)],
        out_specs=[pl.BlockSpec((1, S5_CW, S5_CW), lambda i: (i, 0, 0)),
                   spec4(S5_CW, n2), spec4(n2, S5_CW), spec4(16, n2)],
        compiler_params=_params(1),
        name="s5_prep",
    )(dup_row(a_re), dup_row(a_im), dup_col(a_re), dup_col(a_im), log_step[:, :, None, None],
      bt(b_re), bt(b_im), ct(c_re), ct(c_im))


def _cmul_rows(x, p1, p2):
    return x * p1 + pltpu.roll(x, S5_N, 1) * p2


def _s5_core_kernel(ug_ref, wi_ref, ws_ref, wo_ref, ap_ref, h0_ref, dv_ref, yg_ref, fin_ref,
                    z_ref, sp_ref):
    u = ug_ref[0]
    ub = u.astype(BF16)
    y = jnp.dot(ub, wi_ref[0], preferred_element_type=F32) + dv_ref[0] * u
    n2 = 2 * S5_N
    r = lax.broadcasted_iota(jnp.int32, (S5_ROWS, n2), 0)
    in_p = r < S5_ROWS_P
    rib = jnp.where(in_p, r & (CH_P - 1), (r - S5_ROWS_P) & (CH_S - 1))
    nch = jnp.where(in_p, CH_P, CH_S)
    for d in range(2):
        z_ref[...] = jnp.dot(ub, ws_ref[d, 0], preferred_element_type=F32)
        p1, p2 = ap_ref[d, 0, 0:1, :], ap_ref[d, 0, 8:9, :]
        edge = [S5_ROWS_P + CH_S * b + (0 if d == 0 else CH_S - 1) for b in range(NB_S)]
        for b in range(NB_S):
            h0 = h0_ref[0, d, b:b + 1, :]
            z_ref[edge[b]:edge[b] + 1, :] = z_ref[edge[b]:edge[b] + 1, :] + _cmul_rows(h0, p1, p2)
        s = z_ref[...]
        for k in range(6):
            sh = 1 << k
            if d == 0:
                t = jnp.where(rib >= sh, pltpu.roll(s, sh, 0), 0.0)
            else:
                t = jnp.where(rib < nch - sh, pltpu.roll(s, S5_ROWS - sh, 0), 0.0)
            s = s + _cmul_rows(t, ap_ref[d, 0, k:k + 1, :], ap_ref[d, 0, 8 + k:9 + k, :])
        z_ref[...] = s
        first = CH_P - 1 if d == 0 else 0
        fin_ref[0, d] = z_ref[pl.ds(first, NB_P, stride=CH_P), :]
        if d == 0:
            sp_ref[...] = jnp.where(rib >= 1, pltpu.roll(s, 1, 0), 0.0)
        else:
            sp_ref[...] = jnp.where(rib < nch - 1, pltpu.roll(s, S5_ROWS - 1, 0), 0.0)
        for b in range(NB_S):
            sp_ref[edge[b]:edge[b] + 1, :] = h0_ref[0, d, b:b + 1, :]
        y = y + jnp.dot(sp_ref[...].astype(BF16), wo_ref[d, 0], preferred_element_type=F32)
    yg_ref[0] = jax.nn.gelu(y)


def _s5_core(u, prep, h0, dvec):
    w_intra, w_state, w_out, apow = prep
    g, n2 = S5_GROUPS, 2 * S5_N
    ug = u.reshape(S5_ROWS, S5_T, g, S5_GROUP).transpose(2, 0, 1, 3).reshape(g, S5_ROWS, S5_CW)
    spec4 = lambda r, c: pl.BlockSpec((2, 1, r, c), lambda i: (0, i, 0, 0))
    yg, fin = pl.pallas_call(
        _s5_core_kernel,
        out_shape=[jax.ShapeDtypeStruct((g, S5_ROWS, S5_CW), F32),
                   jax.ShapeDtypeStruct((g, 2, NB_P, n2), F32)],
        grid=(g,),
        in_specs=[pl.BlockSpec((1, S5_ROWS, S5_CW), lambda i: (i, 0, 0)),
                  pl.BlockSpec((1, S5_CW, S5_CW), lambda i: (i, 0, 0)),
                  spec4(S5_CW, n2), spec4(n2, S5_CW), spec4(16, n2),
                  pl.BlockSpec((1, 2, 8, n2), lambda i: (i, 0, 0, 0)),
                  pl.BlockSpec((1, 1, S5_CW), lambda i: (i, 0, 0))],
        out_specs=[pl.BlockSpec((1, S5_ROWS, S5_CW), lambda i: (i, 0, 0)),
                   pl.BlockSpec((1, 2, NB_P, n2), lambda i: (i, 0, 0, 0))],
        scratch_shapes=[pltpu.VMEM((S5_ROWS, n2), F32), pltpu.VMEM((S5_ROWS, n2), F32)],
        compiler_params=_params(1),
        name="s5_scan",
    )(ug, w_intra, w_state, w_out, apow, h0, dvec)
    y = yg.reshape(g, S5_ROWS, S5_T, S5_GROUP).transpose(1, 2, 0, 3).reshape(TOK, S5_WIDTH)
    return y, fin


def _outproj_kernel(glu, y_ref, mod_ref, a1_ref, a2_ref, w1_ref, w2_ref, wg_ref, o_ref):
    a2 = a2_ref[...]
    if glu:
        a2 = a2 * jax.nn.sigmoid(_dot(a2, wg_ref[...]))
    out = _dot(a1_ref[...], w1_ref[...]) + _dot(a2, w2_ref[...])
    o_ref[...] = y_ref[...] + mod_ref[0][5:6] * out


def _outproj(y, mods_l, a1, a2, w_out, w_glu=None):
    glu = w_glu is not None
    k1, k2 = a1.shape[1], a2.shape[1]
    wg = (w_glu if glu else jnp.zeros((8, LANES), F32)).astype(BF16)
    row = lambda i: (i, 0)
    return pl.pallas_call(
        functools.partial(_outproj_kernel, glu),
        out_shape=jax.ShapeDtypeStruct((TOK, D), F32),
        grid=(NT,),
        in_specs=[pl.BlockSpec((TM, D), row),
                  pl.BlockSpec((1, N_MOD, D), lambda i: (_mod_index(i), 0, 0)),
                  pl.BlockSpec((TM, k1), row), pl.BlockSpec((TM, k2), row),
                  _const_spec((k1, D)), _const_spec((k2, D)), _const_spec(wg.shape)],
        out_specs=pl.BlockSpec((TM, D), row),
        compiler_params=_params(1, VMEM_LIMIT),
        name="outproj",
    )(y, mods_l, a1, a2, w_out[:k1].astype(BF16), w_out[k1:].astype(BF16), wg)


DFW = DF_HEADS * 2 * DF_DH
IN_B = 3 * HY_WIDTH + 2 * DFW + DF_HEADS * DF_V


def _inproj_b_kernel(y_ref, mod_ref, g_ref, win_ref, c_ref, s_ref, hy_ref, q_ref, k_ref, v_ref):
    mod = mod_ref[0]
    h = _modulate(y_ref[...], g_ref[...], mod[3:4], mod[4:5]).astype(BF16)
    p = jnp.dot(h, win_ref[...], preferred_element_type=F32)
    o1 = 3 * HY_WIDTH
    hy_ref[...] = p[:, :o1]
    q_ref[...] = _rope(p[:, o1:o1 + DFW], c_ref[...], s_ref[...])
    k_ref[...] = _rope(p[:, o1 + DFW:o1 + 2 * DFW], c_ref[...], s_ref[...])
    v_ref[...] = p[:, o1 + 2 * DFW:]


def _inproj_b(y, mods_l, g, w_in):
    cs, sn = _rope_tables(DFW, tuple(range(0, DFW, DF_DH)))
    row = lambda i: (i, 0)
    pos = lambda i: (_pos_index(i), 0)
    widths = (3 * HY_WIDTH, DFW, DFW, DF_HEADS * DF_V)
    return pl.pallas_call(
        _inproj_b_kernel,
        out_shape=[jax.ShapeDtypeStruct((TOK, w), F32) for w in widths],
        grid=(NT,),
        in_specs=[pl.BlockSpec((TM, D), row),
                  pl.BlockSpec((1, N_MOD, D), lambda i: (_mod_index(i), 0, 0)),
                  _const_spec((1, D)), _const_spec((D, IN_B)),
                  pl.BlockSpec((TM, DFW), pos), pl.BlockSpec((TM, DFW), pos)],
        out_specs=[pl.BlockSpec((TM, w), row) for w in widths],
        compiler_params=_params(1, VMEM_LIMIT),
        name="inproj_odd",
    )(y, mods_l, g[None], w_in.astype(BF16), jnp.asarray(cs), jnp.asarray(sn))


def _diff_attn_kernel(nseg, lam_init, q_ref, lam_ref, sub_ref, *refs):
    o_ref = refs[-1]
    tq = q_ref.shape[0]
    scale = DF_DH ** -0.5
    lp = lam_ref[...]
    lam = (jnp.exp(jnp.sum(lp[0:1] * lp[1:2], axis=-1, keepdims=True))
           - jnp.exp(jnp.sum(lp[2:3] * lp[3:4], axis=-1, keepdims=True)) + lam_init)
    lane = lax.broadcasted_iota(jnp.int32, (tq, LANES), 1)
    for pair in range(DF_HEADS // 2):
        cs = slice(pair * LANES, (pair + 1) * LANES)
        q = (q_ref[:, cs] * scale).astype(BF16)
        ks = [refs[2 * s][:, cs].astype(BF16) for s in range(nseg)]
        vs = [refs[2 * s + 1][:, cs].astype(BF16) for s in range(nseg)]
        outs = []
        for hh in range(2):
            probs = []
            for half in range(2):
                unit = 2 * hh + half
                qm = jnp.where((lane >> 5) == unit, q, jnp.zeros_like(q))
                scores = [_dot_nt(qm, k) for k in ks]
                m = functools.reduce(jnp.maximum, [jnp.max(s, axis=-1, keepdims=True) for s in scores])
                es = [jnp.exp(s - m) for s in scores]
                l = functools.reduce(jnp.add, [jnp.sum(e, axis=-1, keepdims=True) for e in es])
                inv = 1.0 / l
                probs.append([e * inv for e in es])
            o = None
            for s in range(nseg):
                w = (probs[0][s] - lam * probs[1][s]).astype(BF16)
                part = jnp.dot(w, vs[s], preferred_element_type=F32)
                o = part if o is None else o + part
            mine = (lane >> 6) == hh
            ms = jnp.sum(jnp.where(mine, o * o, 0.0), axis=-1, keepdims=True) * (1.0 / DF_V)
            outs.append(o * lax.rsqrt(ms + EPS))
        o_ref[:, cs] = jnp.where(lane < DF_V, outs[0], outs[1]) * sub_ref[...] * (1.0 - lam_init)


def _diff_attention(q, k, v, lam_p, subln, lam_init, n_batch, seq, tq, row0, ctx=None):
    qt = seq // tq
    qb0, kb0 = row0 // tq, row0 // seq
    in_specs = [pl.BlockSpec((tq, DFW), lambda b, j: (qb0 + b * qt + j, 0)),
                pl.BlockSpec((4, DF_DH), lambda b, j: (0, 0)),
                pl.BlockSpec((1, LANES), lambda b, j: (0, 0))]
    args = [q, lam_p, jnp.concatenate([subln, subln])[None]]
    segs = []
    if ctx is not None:
        segs.append((ctx, PAST, 0))
    segs.append(((k, v), seq, kb0))
    for (a_k, a_v), ln, off in segs:
        idx = lambda b, j, off=off: (off + b, 0)
        in_specs += [pl.BlockSpec((ln, DFW), idx), pl.BlockSpec((ln, DF_HEADS * DF_V), idx)]
        args += [a_k, a_v]
    return pl.pallas_call(
        functools.partial(_diff_attn_kernel, len(segs), lam_init),
        out_shape=jax.ShapeDtypeStruct((n_batch * seq, DF_HEADS * DF_V), F32),
        grid=(n_batch, qt),
        in_specs=in_specs,
        out_specs=pl.BlockSpec((tq, DF_HEADS * DF_V), lambda b, j: (b * qt + j, 0)),
        compiler_params=_params(2, VMEM_LIMIT),
        name="diff_attention",
    )(*args)


def _hy_filter_kernel(feat_ref, w1_ref, b1_ref, w2_ref, b2_ref, fq_ref, w3_ref, dec_ref, o_ref):
    feat = feat_ref[...]
    fq = fq_ref[...]
    h = jnp.sin(fq * (_dot3(feat, w1_ref[...]) + b1_ref[...]))
    h = jnp.sin(fq * (_dot3(h, w2_ref[...]) + b2_ref[...]))
    window = jnp.exp(-feat[:, 0:1] * jnp.abs(dec_ref[...]))
    for j in range(4):
        cs = slice(j * HY_WIDTH, (j + 1) * HY_WIDTH)
        o_ref[:, cs] = _dot3(h, w3_ref[:, cs]) * window


def _hy_spectrum_kernel(L, csh_ref, csl_ref, hf_ref, hb_ref, o_ref):
    row = lax.broadcasted_iota(jnp.int32, (L, HY_WIDTH), 0)
    first = row == 0
    tf = _dot3_const(csh_ref[...], csl_ref[...], hf_ref[...])
    tb = _dot3_const(csh_ref[...], csl_ref[...], jnp.where(first, 0.0, hb_ref[...]))
    ka = tf[:L] + tb[:L]
    kb = jnp.where(first, tf[L:] + tb[L:], tf[L:] - tb[L:])
    wv = jnp.where(first, 1.0 / (2 * L), 2.0 / (2 * L))
    o_ref[0, 0] = ka * wv
    o_ref[0, 1] = jnp.where(first, 0.0, kb) * wv
    o_ref[0, 2] = jnp.where(first, kb, ka) * wv


def _hy_conv_kernel(L, csh_ref, csl_ref, cth_ref, ctl_ref, kf_ref, v_ref, x1_ref, x2_ref,
                    wv_ref, w1_ref, w2_ref, bias_ref, o_ref):
    row = lax.broadcasted_iota(jnp.int32, v_ref.shape, 0)

    def short(x_ref, w_ref):
        x = x_ref[...]
        prev = jnp.where(row >= 1, pltpu.roll(x, 1, 0), 0.0)
        nxt = jnp.where(row <= L - 2, pltpu.roll(x, L - 1, 0), 0.0)
        return w_ref[0:1] * prev + w_ref[1:2] * x + w_ref[2:3] * nxt

    z = short(v_ref, wv_ref)
    gates = (short(x1_ref, w1_ref), short(x2_ref, w2_ref))
    for n in range(2):
        ab = _dot3_const(csh_ref[...], csl_ref[...], z)
        a, b = ab[:L], ab[L:]
        ka, kb1, ka2 = kf_ref[n, 0], kf_ref[n, 1], kf_ref[n, 2]
        pq = jnp.concatenate([a * ka - b * kb1, a * kb1 + b * ka2], axis=0)
        conv = _dot3_const(cth_ref[...], ctl_ref[...], pq)
        z = gates[n] * (conv + bias_ref[n:n + 1] * z)
    o_ref[...] = z


def _hyena_spectrum(L, phy):
    conv_w, w1, b1, w2, b2, freq, w3, decay, bias = phy
    feat = jnp.asarray(_hyena_features(L))
    w1p = jnp.pad(w1, ((0, LANES - HY_EMB), (0, 0)))
    filt = pl.pallas_call(
        _hy_filter_kernel,
        out_shape=jax.ShapeDtypeStruct((L, 4 * HY_WIDTH), F32),
        grid=(1,),
        in_specs=[_const_spec((L, LANES)), _const_spec((LANES, HY_FH)), _const_spec((1, HY_FH)),
                  _const_spec((HY_FH, HY_FH)), _const_spec((1, HY_FH)), _const_spec((1, HY_FH)),
                  _const_spec((HY_FH, 4 * HY_WIDTH)), _const_spec((1, HY_WIDTH))],
        out_specs=pl.BlockSpec((L, 4 * HY_WIDTH), lambda i: (0, 0)),
        compiler_params=_params(1, VMEM_LIMIT),
        name="hyena_filter",
    )(feat, w1p, b1[None], w2, b2[None], freq[None], w3, decay[None])
    csh, csl, _, _ = (jnp.asarray(t) for t in _dft_tables(L))
    return pl.pallas_call(
        functools.partial(_hy_spectrum_kernel, L),
        out_shape=jax.ShapeDtypeStruct((2, 3, L, HY_WIDTH), F32),
        grid=(2,),
        in_specs=[_const_spec((2 * L, L)), _const_spec((2 * L, L)),
                  pl.BlockSpec((L, HY_WIDTH), lambda n: (0, n)),
                  pl.BlockSpec((L, HY_WIDTH), lambda n: (0, 2 + n))],
        out_specs=pl.BlockSpec((1, 3, L, HY_WIDTH), lambda n: (n, 0, 0, 0)),
        compiler_params=_params(1, VMEM_LIMIT),
        name="hyena_spectrum",
    )(csh, csl, filt, filt)


def _hyena_conv(hy_u, spec, phy, n_batch, L, cb, row0):
    conv_w, bias = phy[0], phy[8]
    csh, csl, cth, ctl = (jnp.asarray(t) for t in _dft_tables(L))
    nc = HY_WIDTH // cb
    rb0 = row0 // L
    col = lambda off: (lambda b, c: (0, off * nc + c))
    tok = lambda off: (lambda b, c: (rb0 + b, off * nc + c))
    return pl.pallas_call(
        functools.partial(_hy_conv_kernel, L),
        out_shape=jax.ShapeDtypeStruct((n_batch * L, HY_WIDTH), F32),
        grid=(n_batch, nc),
        in_specs=[_const_spec((2 * L, L)), _const_spec((2 * L, L)),
                  _const_spec((L, 2 * L)), _const_spec((L, 2 * L)),
                  pl.BlockSpec((2, 3, L, cb), lambda b, c: (0, 0, 0, c)),
                  pl.BlockSpec((L, cb), tok(0)), pl.BlockSpec((L, cb), tok(1)), pl.BlockSpec((L, cb), tok(2)),
                  pl.BlockSpec((3, cb), col(0)), pl.BlockSpec((3, cb), col(1)), pl.BlockSpec((3, cb), col(2)),
                  pl.BlockSpec((2, cb), col(0))],
        out_specs=pl.BlockSpec((L, cb), lambda b, c: (b, c)),
        compiler_params=_params(2, VMEM_LIMIT),
        name="hyena_conv",
    )(csh, csl, cth, ctl, spec, hy_u, hy_u, hy_u, conv_w, conv_w, conv_w, bias)


def _even_mixer(y, mods_l, g, pa, ps5, ctx_ckv, ctx_krope, ctx_state):
    w_in, w_out, q_norm, w_uq, kv_norm, w_ukv = pa
    a_re, a_im, log_step, b_re, b_im, c_re, c_im, d_skip, w_glu = ps5
    q, ckv, kr_unrot, kr_rot, kn, v, u, (w_k, w_v) = _inproj_a(y, mods_l, g, w_in, q_norm, w_uq, kv_norm, w_ukv)

    ctx_flat = ctx_ckv.reshape(NB_S * PAST, MLA_KV_RANK)
    ctx_kn = _linear(ctx_flat, w_k, PAST)
    ctx_v = _linear(ctx_flat, w_v, PAST)
    ctx_kr = jnp.pad(ctx_krope.reshape(NB_S * PAST, MLA_ROPE), ((0, 0), (KR_AT, LANES - KR_AT - MLA_ROPE)))
    att_p = _mla_attention(q, kn, kr_rot, v, NB_P, L_P, L_P, 0)
    att_s = _mla_attention(q, kn, kr_rot, v, NB_S, L_S, TM, TOK_P, ctx=(ctx_kn, ctx_kr, ctx_v))
    att = jnp.concatenate([att_p, att_s], axis=0)

    prep = _s5_prep(a_re, a_im, log_step, b_re, b_im, c_re, c_im)
    h0 = ctx_state.transpose(3, 1, 0, 2, 4).reshape(S5_GROUPS, 2, NB_S, 2 * S5_N)
    h0 = jnp.pad(h0, ((0, 0), (0, 0), (0, 8 - NB_S), (0, 0)))
    dvec = jnp.tile(d_skip.reshape(S5_GROUPS, 1, S5_GROUP), (1, 1, S5_T))
    s5y, fin = _s5_core(u, prep, h0, dvec)

    y = _outproj(y, mods_l, att, s5y, w_out, w_glu)
    new_ckv = ckv[:TOK_P].reshape(NB_P, L_P, MLA_KV_RANK)
    new_krope = kr_unrot[:TOK_P, KR_AT:KR_AT + MLA_ROPE].reshape(NB_P, L_P, MLA_ROPE)
    new_state = fin.reshape(S5_GROUPS, 2, NB_P, 2, S5_N).transpose(2, 1, 3, 0, 4)
    return y, new_ckv, new_krope, new_state


def _odd_mixer(y, mods_l, g, pb, phy, ctx_k, ctx_v, lam_init):
    w_in, w_out, lam_p, subln = pb
    hy_u, q, k, v = _inproj_b(y, mods_l, g, w_in)
    hy_p = _hyena_conv(hy_u, _hyena_spectrum(L_P, phy), phy, NB_P, L_P, HY_WIDTH, 0)
    hy_s = _hyena_conv(hy_u, _hyena_spectrum(L_S, phy), phy, NB_S, L_S, HY_WIDTH // 2, TOK_P)
    hy = jnp.concatenate([hy_p, hy_s], axis=0)
    ctx = (ctx_k.reshape(NB_S * PAST, DFW), ctx_v.reshape(NB_S * PAST, DF_HEADS * DF_V))
    att_p = _diff_attention(q, k, v, lam_p, subln, lam_init, NB_P, L_P, L_P, 0)
    att_s = _diff_attention(q, k, v, lam_p, subln, lam_init, NB_S, L_S, TM // 2, TOK_P, ctx=ctx)
    att = jnp.concatenate([att_p, att_s], axis=0)
    y = _outproj(y, mods_l, hy, att, w_out)
    new_k = k[:TOK_P].reshape(NB_P, L_P, DF_HEADS, 2, DF_DH)
    new_v = v[:TOK_P].reshape(NB_P, L_P, DF_HEADS, DF_V)
    return y, new_k, new_v


def kernel(x_prompt, x_sample, c, c_ctx, cache_mla_ckv, cache_mla_krope, state_s5, cache_diff_k, cache_diff_v, ada_w, ada_b, norm_g, ff_w_in, ff_w_out, w_in_a, w_out_a, mla_q_norm, mla_w_uq, mla_kv_norm, mla_w_ukv, s5_a_re, s5_a_im, s5_log_step, s5_b_re, s5_b_im, s5_c_re, s5_c_im, s5_d, s5_w_glu, w_in_b, w_out_b, hy_conv, hy_w1, hy_b1, hy_w2, hy_b2, hy_freq, hy_w3, hy_decay, hy_bias, df_lambda, df_subln, final_norm):
    depth = ada_w.shape[0]
    y = jnp.concatenate([x_prompt.reshape(TOK_P, D), x_sample.reshape(TOK_S, D)], axis=0)
    mods = _adaln(jnp.concatenate([c_ctx[None], c], axis=0), ada_w, ada_b)
    new_ckv, new_krope, new_s5, new_dk, new_dv = [], [], [], [], []
    for l in range(depth):
        y = _half_ffn(y, mods[l], norm_g[l, 0], ff_w_in[l, 0], ff_w_out[l, 0], 0)
        if l % 2 == 0:
            e = l // 2
            pa = (w_in_a[e], w_out_a[e], mla_q_norm[e], mla_w_uq[e], mla_kv_norm[e], mla_w_ukv[e])
            ps5 = (s5_a_re[e], s5_a_im[e], s5_log_step[e], s5_b_re[e], s5_b_im[e],
                   s5_c_re[e], s5_c_im[e], s5_d[e], s5_w_glu[e])
            y, ckv, krope, st = _even_mixer(y, mods[l], norm_g[l, 1], pa, ps5, cache_mla_ckv[:, e],
                                            cache_mla_krope[:, e], state_s5[:, e])
            new_ckv.append(ckv)
            new_krope.append(krope)
            new_s5.append(st)
        else:
            o = l // 2
            lam_init = 0.8 - 0.6 * math.exp(-0.3 * l)
            pb = (w_in_b[o], w_out_b[o], df_lambda[o], df_subln[o])
            phy = (hy_conv[o], hy_w1[o], hy_b1[o], hy_w2[o], hy_b2[o], hy_freq[o],
                   hy_w3[o], hy_decay[o], hy_bias[o])
            y, dk, dv = _odd_mixer(y, mods[l], norm_g[l, 1], pb, phy, cache_diff_k[:, o],
                                   cache_diff_v[:, o], lam_init)
            new_dk.append(dk)
            new_dv.append(dv)
        last = l == depth - 1
        y = _half_ffn(y, mods[l], norm_g[l, 2], ff_w_in[l, 1], ff_w_out[l, 1], 6,
                      final_g=final_norm if last else None)
    y_prompt = y[:TOK_P].reshape(NB_P, L_P, D)
    y_sample = y[TOK_P:].reshape(NB_S, L_S, D)
    return (y_prompt, y_sample, jnp.stack(new_ckv, axis=1), jnp.stack(new_krope, axis=1),
            jnp.stack(new_s5, axis=1), jnp.stack(new_dk, axis=1), jnp.stack(new_dv, axis=1))
```

```python
import functools
import math

import ml_dtypes
import numpy as np
import jax
import jax.numpy as jnp
from jax import lax
from jax.experimental import pallas as pl
from jax.experimental.pallas import tpu as pltpu

F32 = jnp.float32
BF16 = jnp.bfloat16

D = 1024
NB_P, L_P = 16, 256
NB_S, L_S = 2, 1024
PAST = 256
GRID_W = 64
N_MOD = 9
FF = 2816
EPS = 1e-6
ROPE_BASE = 10000.0

MLA_HEADS, MLA_NOPE, MLA_ROPE, MLA_V = 8, 64, 32, 64
MLA_Q_RANK, MLA_KV_RANK = 384, 256
S5_WIDTH, S5_GROUP, S5_N = 512, 16, 64
S5_GROUPS = S5_WIDTH // S5_GROUP
HY_WIDTH, HY_BANDS, HY_FH = 512, 16, 64
HY_EMB = 2 * HY_BANDS + 1
DF_HEADS, DF_DH = 8, 32
DF_V = 2 * DF_DH

TOK_P = NB_P * L_P
TOK_S = NB_S * L_S
TOK = TOK_P + TOK_S
TM = 512
NT = TOK // TM
NT_P = TOK_P // TM
TILES_PER_SAMPLE = L_S // TM

LANES = 128
S5_T = 16
S5_CW = S5_T * S5_GROUP
CH_P = L_P // S5_T
CH_S = L_S // S5_T
S5_ROWS = NB_P * CH_P + NB_S * CH_S
S5_ROWS_P = NB_P * CH_P

VMEM_LIMIT = 56 * 1024 * 1024


def _params(n_grid, vmem=None):
    return pltpu.CompilerParams(dimension_semantics=("arbitrary",) * n_grid,
                                vmem_limit_bytes=vmem)


def _const_spec(shape):
    nd = len(shape)
    return pl.BlockSpec(shape, lambda *_: (0,) * nd, pipeline_mode=pl.Buffered(1))


def _mod_index(i):
    return jnp.where(i < NT_P, 0, 1 + (i - NT_P) // TILES_PER_SAMPLE)


def _pos_index(i):
    return jnp.where(i < NT_P, 0, 1 + (i - NT_P) % TILES_PER_SAMPLE)


def _row(i):
    return (i, 0)


def _row_p(i):
    return (jnp.minimum(i, NT_P - 1), 0)


def _row_s(i):
    return (jnp.maximum(i - NT_P, 0), 0)


def _tok_specs(x, width):
    if isinstance(x, tuple):
        return [pl.BlockSpec((TM, width), _row_p), pl.BlockSpec((TM, width), _row_s)], list(x)
    return [pl.BlockSpec((TM, width), _row)], [x]


def _tok_read(refs, split):
    if split:
        return jnp.where(pl.program_id(0) < NT_P, refs[0][...], refs[1][...]), refs[2:]
    return refs[0][...], refs[1:]


def _tok_write(p_ref, s_ref, value):
    i = pl.program_id(0)

    @pl.when(i < NT_P)
    def _():
        p_ref[...] = value

    @pl.when(i >= NT_P)
    def _():
        s_ref[...] = value


def _split_out(width):
    shapes = [jax.ShapeDtypeStruct((TOK_P, width), F32), jax.ShapeDtypeStruct((TOK_S, width), F32)]
    specs = [pl.BlockSpec((TM, width), _row_p), pl.BlockSpec((TM, width), _row_s)]
    return shapes, specs


def _dot(a, b):
    return jnp.dot(a.astype(BF16), b.astype(BF16), preferred_element_type=F32)


def _dot_nt(a, b):
    return lax.dot_general(a, b, (((1,), (1,)), ((), ())), preferred_element_type=F32)


def _split(x):
    hi = x.astype(BF16)
    lo = (x - hi.astype(F32)).astype(BF16)
    return hi, lo


def _dot3(a, b):
    ah, al = _split(a)
    bh, bl = _split(b)
    d = functools.partial(jnp.dot, preferred_element_type=F32)
    return d(ah, bh) + d(ah, bl) + d(al, bh)


def _dot3_const(ch, cl, x):
    xh, xl = _split(x)
    d = functools.partial(jnp.dot, preferred_element_type=F32)
    return d(ch, xh) + d(ch, xl) + d(cl, xh)


def _rmsnorm(x, g):
    return x * lax.rsqrt(jnp.mean(x * x, axis=-1, keepdims=True) + EPS) * g


def _modulate(y, g, shift, scale):
    return _rmsnorm(y, g) * (1.0 + scale) + shift


def _pair_swap(x):
    n = x.shape[-1]
    lane = lax.broadcasted_iota(jnp.int32, x.shape, x.ndim - 1)
    return jnp.where((lane & 1) == 0, pltpu.roll(x, n - 1, x.ndim - 1), pltpu.roll(x, 1, x.ndim - 1))


def _rope(x, cos, sin_signed):
    return x * cos + _pair_swap(x) * sin_signed


def _rope_angles():
    n_freq = MLA_ROPE // 4
    inv = 1.0 / (ROPE_BASE ** (np.arange(n_freq, dtype=np.float64) / n_freq))
    pos = np.arange(L_S)
    row = (pos // GRID_W).astype(np.float64)
    col = (pos % GRID_W).astype(np.float64)
    ang = np.concatenate([row[:, None] * inv, col[:, None] * inv], axis=-1)
    return np.cos(ang), np.sin(ang)


@functools.lru_cache(maxsize=None)
def _rope_tables(width, starts):
    cos, sin = _rope_angles()
    c = np.ones((TM + L_S, width), np.float32)
    s = np.zeros((TM + L_S, width), np.float32)
    sign = np.where(np.arange(MLA_ROPE) % 2 == 0, -1.0, 1.0)
    unit_c = np.repeat(cos, 2, axis=1)
    unit_s = np.repeat(sin, 2, axis=1) * sign
    for st in starts:
        c[TM:, st:st + MLA_ROPE] = unit_c
        s[TM:, st:st + MLA_ROPE] = unit_s
    return c, s


@functools.lru_cache(maxsize=None)
def _dft_tables(L):
    f = np.arange(L)[:, None]
    s = np.arange(L)[None, :]
    ang = np.pi * ((f * s) % (2 * L)).astype(np.float64) / L
    cs = np.concatenate([np.cos(ang), np.sin(ang)], axis=0)
    cs[L, :] = np.where(np.arange(L) % 2 == 0, 1.0, -1.0)
    hi = cs.astype(ml_dtypes.bfloat16)
    lo = (cs - hi.astype(np.float64)).astype(ml_dtypes.bfloat16)
    return hi, lo, np.ascontiguousarray(hi.T), np.ascontiguousarray(lo.T)


@functools.lru_cache(maxsize=None)
def _hyena_features(L):
    t = np.arange(L, dtype=np.float64) / L
    bands = np.arange(1, HY_BANDS + 1, dtype=np.float64)
    ang = 2.0 * math.pi * t[:, None] * bands
    feat = np.zeros((L, LANES), np.float32)
    feat[:, 0] = t
    feat[:, 1:1 + HY_BANDS] = np.cos(ang)
    feat[:, 1 + HY_BANDS:HY_EMB] = np.sin(ang)
    return feat


def _adaln_kernel(c_ref, w_ref, b_ref, o_ref):
    w = w_ref[0]
    o_ref[0] = jnp.zeros(o_ref.shape[1:], F32)
    for m in range(c_ref.shape[0]):
        col = jax.nn.silu(c_ref[m])
        o_ref[0, m:m + 1, :] = jnp.sum(w * col, axis=0, keepdims=True) + b_ref[0]


def _adaln(cvecs, ada_w, ada_b):
    depth = ada_w.shape[0]
    n_vec = cvecs.shape[0]
    tn = D
    out = pl.pallas_call(
        _adaln_kernel,
        out_shape=jax.ShapeDtypeStruct((depth, 8, N_MOD * D), F32),
        grid=(depth, N_MOD),
        in_specs=[pl.BlockSpec((n_vec, D, 1), lambda l, j: (0, 0, 0)),
                  pl.BlockSpec((1, D, tn), lambda l, j: (l, 0, j)),
                  pl.BlockSpec((1, 1, tn), lambda l, j: (l, 0, j))],
        out_specs=pl.BlockSpec((1, 8, tn), lambda l, j: (l, 0, j)),
        compiler_params=_params(2),
        name="adaln",
    )(cvecs[:, :, None], ada_w, ada_b[:, None, :])
    return out[:, :n_vec].reshape(depth, n_vec, N_MOD, D)


FF_CHUNK = FF // 2


def _ffn_kernel(base, final, split_in, *refs):
    y, refs = _tok_read(refs, split_in)
    mod_ref, g_ref, win_ref, wout_ref, fg_ref = refs[:5]
    outs = refs[5:]
    mod = mod_ref[0]
    h = _modulate(y, g_ref[...], mod[base:base + 1], mod[base + 1:base + 2]).astype(BF16)
    acc = jnp.zeros(y.shape, F32)
    for c in range(FF // FF_CHUNK):
        lo = c * FF_CHUNK
        gate = jnp.dot(h, win_ref[:, lo:lo + FF_CHUNK], preferred_element_type=F32)
        up = jnp.dot(h, win_ref[:, FF + lo:FF + lo + FF_CHUNK], preferred_element_type=F32)
        a = (jax.nn.silu(gate) * up).astype(BF16)
        acc = acc + jnp.dot(a, wout_ref[lo:lo + FF_CHUNK, :], preferred_element_type=F32)
    out = y + 0.5 * mod[base + 2:base + 3] * acc
    if final:
        _tok_write(outs[0], outs[1], _rmsnorm(out, fg_ref[...]))
    else:
        outs[0][...] = out


def _half_ffn(y, mods_l, g, w_in, w_out, base, final_g=None):
    final = final_g is not None
    fg = final_g if final else g
    y_specs, y_args = _tok_specs(y, D)
    if final:
        out_shape, out_specs = _split_out(D)
    else:
        out_shape, out_specs = jax.ShapeDtypeStruct((TOK, D), F32), pl.BlockSpec((TM, D), _row)
    return pl.pallas_call(
        functools.partial(_ffn_kernel, base, final, isinstance(y, tuple)),
        out_shape=out_shape,
        grid=(NT,),
        in_specs=y_specs + [pl.BlockSpec((1, N_MOD, D), lambda i: (_mod_index(i), 0, 0)),
                            _const_spec((1, D)),
                            _const_spec((D, 2 * FF)),
                            _const_spec((FF, D)),
                            _const_spec((1, D))],
        out_specs=out_specs,
        compiler_params=_params(1, VMEM_LIMIT),
        name="half_ffn",
    )(*y_args, mods_l, g[None], w_in.astype(BF16), w_out.astype(BF16), fg[None])


def _linear_kernel(x_ref, w_ref, o_ref):
    o_ref[...] = _dot(x_ref[...], w_ref[...])


def _linear(x, w, tm):
    m, k = x.shape
    n = w.shape[1]
    return pl.pallas_call(
        _linear_kernel,
        out_shape=jax.ShapeDtypeStruct((m, n), F32),
        grid=(m // tm,),
        in_specs=[pl.BlockSpec((tm, k), lambda i: (i, 0)), _const_spec((k, n))],
        out_specs=pl.BlockSpec((tm, n), lambda i: (i, 0)),
        compiler_params=_params(1),
        name="linear",
    )(x, w.astype(BF16))


QW = MLA_HEADS * LANES
KR_AT = MLA_NOPE
IN_A_PAD = MLA_Q_RANK + MLA_KV_RANK + S5_WIDTH + LANES


def _inproj_a_kernel(y_ref, mod_ref, g_ref, win_ref, qn_ref, wuq_ref, kvn_ref, wk_ref, wv_ref,
                     cq_ref, sq_ref, ck_ref, sk_ref,
                     q_ref, ckv_ref, kru_ref, krr_ref, kn_ref, v_ref, u_ref):
    mod = mod_ref[0]
    h = _modulate(y_ref[...], g_ref[...], mod[3:4], mod[4:5]).astype(BF16)
    p = jnp.dot(h, win_ref[...], preferred_element_type=F32)
    o1 = MLA_Q_RANK
    o2 = o1 + MLA_KV_RANK
    o3 = o2 + S5_WIDTH
    q = _dot(_rmsnorm(p[:, :o1], qn_ref[...]), wuq_ref[...])
    q_ref[...] = _rope(q, cq_ref[...], sq_ref[...])
    ckv = _rmsnorm(p[:, o1:o2], kvn_ref[...])
    ckv_b = ckv.astype(BF16)
    kn_ref[...] = jnp.dot(ckv_b, wk_ref[...], preferred_element_type=F32)
    v_ref[...] = jnp.dot(ckv_b, wv_ref[...], preferred_element_type=F32)
    u_ref[...] = p[:, o2:o3]
    krp = p[:, o3:]
    krr_ref[...] = _rope(krp, ck_ref[...], sk_ref[...])

    @pl.when(pl.program_id(0) < NT_P)
    def _():
        ckv_ref[...] = ckv
        kru_ref[...] = krp


def _inproj_a(y, mods_l, g, w_in, q_norm, w_uq, kv_norm, w_ukv):
    o1 = MLA_Q_RANK
    o2 = o1 + MLA_KV_RANK
    o3 = o2 + MLA_ROPE
    kr_cols = jnp.pad(w_in[:, o2:o3], ((0, 0), (KR_AT, LANES - KR_AT - MLA_ROPE)))
    w_ext = jnp.concatenate([w_in[:, :o2], w_in[:, o3:], kr_cols], axis=1).astype(BF16)
    dq = MLA_NOPE + MLA_ROPE
    w_uq_pad = jnp.pad(w_uq.reshape(MLA_Q_RANK, MLA_HEADS, dq),
                       ((0, 0), (0, 0), (0, LANES - dq))).reshape(MLA_Q_RANK, QW).astype(BF16)
    w_kv = w_ukv.reshape(MLA_KV_RANK, MLA_HEADS, MLA_NOPE + MLA_V)
    w_k = jnp.pad(w_kv[:, :, :MLA_NOPE], ((0, 0), (0, 0), (0, LANES - MLA_NOPE))).reshape(MLA_KV_RANK, QW)
    w_v = w_kv[:, :, MLA_NOPE:].reshape(MLA_KV_RANK, MLA_HEADS * MLA_V)
    w_k, w_v = w_k.astype(BF16), w_v.astype(BF16)
    cq, sq = _rope_tables(QW, tuple(h * LANES + MLA_NOPE for h in range(MLA_HEADS)))
    ck, sk = _rope_tables(LANES, (KR_AT,))
    row = _row
    pos = lambda i: (_pos_index(i), 0)
    widths = (QW, MLA_KV_RANK, LANES, LANES, QW, MLA_HEADS * MLA_V, S5_WIDTH)
    prompt_only = (1, 2)
    outs = pl.pallas_call(
        _inproj_a_kernel,
        out_shape=[jax.ShapeDtypeStruct((TOK_P if k in prompt_only else TOK, w), F32)
                   for k, w in enumerate(widths)],
        grid=(NT,),
        in_specs=[pl.BlockSpec((TM, D), row),
                  pl.BlockSpec((1, N_MOD, D), lambda i: (_mod_index(i), 0, 0)),
                  _const_spec((1, D)),
                  _const_spec((D, IN_A_PAD)),
                  _const_spec((1, MLA_Q_RANK)),
                  _const_spec((MLA_Q_RANK, QW)),
                  _const_spec((1, MLA_KV_RANK)),
                  _const_spec((MLA_KV_RANK, QW)),
                  _const_spec((MLA_KV_RANK, MLA_HEADS * MLA_V)),
                  pl.BlockSpec((TM, QW), pos), pl.BlockSpec((TM, QW), pos),
                  pl.BlockSpec((TM, LANES), pos), pl.BlockSpec((TM, LANES), pos)],
        out_specs=[pl.BlockSpec((TM, w), _row_p if k in prompt_only else row)
                   for k, w in enumerate(widths)],
        compiler_params=_params(1, VMEM_LIMIT),
        name="inproj_even",
    )(y, mods_l, g[None], w_ext, q_norm[None], w_uq_pad, kv_norm[None], w_k, w_v,
      jnp.asarray(cq), jnp.asarray(sq), jnp.asarray(ck), jnp.asarray(sk))
    q, ckv, kr_unrot, kr_rot, kn, v, u = outs
    return q, ckv, kr_unrot, kr_rot, kn, v, u, (w_k, w_v)


def _mla_attn_kernel(nseg, q_ref, *refs):
    o_ref = refs[-1]
    tq = q_ref.shape[0]
    scale = (MLA_NOPE + MLA_ROPE) ** -0.5
    lane = lax.broadcasted_iota(jnp.int32, (tq, LANES), 1)
    for pair in range(MLA_HEADS // 2):
        outs = []
        for hh in range(2):
            h = 2 * pair + hh
            hs = slice(h * LANES, (h + 1) * LANES)
            qh = (q_ref[:, hs] * scale).astype(BF16)
            scores = []
            for s in range(nseg):
                kn_ref, kr_ref = refs[3 * s], refs[3 * s + 1]
                kh = (kn_ref[:, hs] + kr_ref[...]).astype(BF16)
                scores.append(_dot_nt(qh, kh))
            m = functools.reduce(jnp.maximum, [jnp.max(s, axis=-1, keepdims=True) for s in scores])
            es = [jnp.exp(s - m) for s in scores]
            l = functools.reduce(jnp.add, [jnp.sum(e, axis=-1, keepdims=True) for e in es])
            o = None
            for s in range(nseg):
                v_ref = refs[3 * s + 2]
                part = _dot(es[s], v_ref[:, pair * LANES:(pair + 1) * LANES])
                o = part if o is None else o + part
            outs.append(o / l)
        o_ref[:, pair * LANES:(pair + 1) * LANES] = jnp.where(lane < MLA_V, outs[0], outs[1])


def _mla_attention(q, kn, kr, v, n_batch, seq, tq, row0, ctx=None):
    qt = seq // tq
    qb0, kb0 = row0 // tq, row0 // seq
    in_specs = [pl.BlockSpec((tq, QW), lambda b, j: (qb0 + b * qt + j, 0))]
    args = [q]
    segs = []
    if ctx is not None:
        segs.append((ctx, PAST, 0))
    segs.append(((kn, kr, v), seq, kb0))
    for (a_kn, a_kr, a_v), ln, off in segs:
        idx = lambda b, j, off=off: (off + b, 0)
        in_specs += [pl.BlockSpec((ln, QW), idx), pl.BlockSpec((ln, LANES), idx),
                     pl.BlockSpec((ln, MLA_HEADS * MLA_V), idx)]
        args += [a_kn, a_kr, a_v]
    return pl.pallas_call(
        functools.partial(_mla_attn_kernel, len(segs)),
        out_shape=jax.ShapeDtypeStruct((n_batch * seq, MLA_HEADS * MLA_V), F32),
        grid=(n_batch, qt),
        in_specs=in_specs,
        out_specs=pl.BlockSpec((tq, MLA_HEADS * MLA_V), lambda b, j: (b * qt + j, 0)),
        compiler_params=_params(2, VMEM_LIMIT),
        name="mla_attention",
    )(*args)


def _cpow(ar, ai, e, nbits):
    rr = jnp.ones_like(ar)
    ri = jnp.zeros_like(ar)
    br, bi = ar, ai
    for k in range(nbits):
        bit = ((e >> k) & 1) == 1
        nr = rr * br - ri * bi
        ni = rr * bi + ri * br
        rr = jnp.where(bit, nr, rr)
        ri = jnp.where(bit, ni, ri)
        if k + 1 < nbits:
            br, bi = br * br - bi * bi, 2.0 * br * bi
    return rr, ri


def _s5_abar_kernel(lr_ref, li_ref, ls_ref, o_ref):
    step = jnp.exp(ls_ref[...])
    lr = jnp.minimum(lr_ref[...], -1e-4)
    li = li_ref[...]
    mag = jnp.exp(lr * step)
    ar = mag * jnp.cos(li * step)
    ai = mag * jnp.sin(li * step)
    den = lr * lr + li * li
    o_ref[0] = ar
    o_ref[1] = ai
    o_ref[2] = ((ar - 1.0) * lr + ai * li) / den
    o_ref[3] = (ai * lr - (ar - 1.0) * li) / den


def _s5_prep_kernel(arow_ref, acol_ref, btr_ref, bti_ref, ctr_ref, cti_ref,
                    wi_ref, ws_ref, wo_ref, ap_ref):
    n2 = 2 * S5_N
    lane_o = lax.broadcasted_iota(jnp.int32, (n2, S5_CW), 1)
    blk_o = lane_o >> 4
    row_o = lax.broadcasted_iota(jnp.int32, (n2, S5_CW), 0)
    lane_k = lax.broadcasted_iota(jnp.int32, (S5_GROUP, S5_CW), 1)
    row_k = lax.broadcasted_iota(jnp.int32, (S5_GROUP, S5_CW), 0)
    lane_b = lax.broadcasted_iota(jnp.int32, (S5_GROUP, n2), 1)
    lane_a = lax.broadcasted_iota(jnp.int32, (1, n2), 1)
    rep = ((lane_k & (S5_GROUP - 1)) == row_k).astype(BF16)

    def tile16(x):
        hi = x.astype(BF16)
        r1 = x - hi.astype(F32)
        mid = r1.astype(BF16)
        lo = (r1 - mid.astype(F32)).astype(BF16)
        d = functools.partial(jnp.dot, preferred_element_type=F32)
        return d(hi, rep) + d(mid, rep) + d(lo, rep)

    intra = [None] * S5_T
    for d in range(2):
        ar, ai, fr, fi = (arow_ref[d, 0, k:k + 1, :] for k in range(4))
        btr, bti = btr_ref[d, 0], bti_ref[d, 0]
        bbr = fr * btr - fi * bti
        bbi = fr * bti + fi * btr
        pws = [(jnp.ones_like(ar), jnp.zeros_like(ar))]
        for _ in range(S5_T):
            pr, pi = pws[-1]
            pws.append((pr * ar - pi * ai, pr * ai + pi * ar))
        for s in range(S5_T):
            pr, pi = pws[S5_T - 1 - s] if d == 0 else pws[s]
            ws_ref[d, 0, s * S5_GROUP:(s + 1) * S5_GROUP, :] = jnp.where(
                lane_b < S5_N, pr * bbr - pi * bbi, pr * bbi + pi * bbr).astype(BF16)

        acol = acol_ref[d, 0]
        arc = jnp.broadcast_to(acol[:, 0:1], (n2, S5_CW))
        aic = jnp.broadcast_to(acol[:, 1:2], (n2, S5_CW))
        ctr, cti = tile16(ctr_ref[d, 0]), tile16(cti_ref[d, 0])
        e_lag = blk_o if d == 0 else (S5_T - 1 - blk_o)
        pqr, pqi = _cpow(arc, aic, e_lag, 4)
        qr = pqr * ctr - pqi * cti
        qi = pqr * cti + pqi * ctr
        wo_ref[d, 0] = jnp.where(row_o < S5_N, qr * arc - qi * aic, -(qr * aic + qi * arc)).astype(BF16)
        q_stack = jnp.where(row_o < S5_N, qr, qi)
        bb_mix = jnp.where(lane_b < S5_N, bbr, -bbi)
        kt = _dot3(bb_mix, q_stack)
        for s in range(S5_T):
            if d == 0:
                blk = jnp.where(lane_k >= S5_GROUP * s, pltpu.roll(kt, S5_GROUP * s, 1), 0.0)
            else:
                blk = jnp.where(lane_k < S5_GROUP * (s + 1),
                                pltpu.roll(kt, (S5_GROUP * (s + 1)) % S5_CW, 1), 0.0)
            intra[s] = blk if intra[s] is None else intra[s] + blk

        pr1, pi1 = pws[S5_T]
        for k in range(6):
            ap_ref[d, 0, k:k + 1, :] = pr1
            ap_ref[d, 0, 8 + k:9 + k, :] = jnp.where(lane_a < S5_N, -pi1, pi1)
            pr1, pi1 = pr1 * pr1 - pi1 * pi1, 2.0 * pr1 * pi1
        ap_ref[d, 0, 6:8, :] = jnp.zeros((2, n2), F32)
        ap_ref[d, 0, 14:16, :] = jnp.zeros((2, n2), F32)
    for s in range(S5_T):
        wi_ref[0, s * S5_GROUP:(s + 1) * S5_GROUP, :] = intra[s].astype(BF16)


def _s5_prep(a_re, a_im, log_step, b_re, b_im, c_re, c_im):
    g, n, n2 = S5_GROUPS, S5_N, 2 * S5_N
    abar = pl.pallas_call(
        _s5_abar_kernel,
        out_shape=jax.ShapeDtypeStruct((4, 2 * g, n), F32),
        grid=(1,),
        in_specs=[_const_spec((2 * g, n)), _const_spec((2 * g, n)), _const_spec((2 * g, 1))],
        out_specs=pl.BlockSpec((4, 2 * g, n), lambda i: (0, 0, 0)),
        compiler_params=_params(1),
        name="s5_abar",
    )(a_re.reshape(2 * g, n), a_im.reshape(2 * g, n), log_step.reshape(2 * g, 1))
    abar = jnp.concatenate([abar, abar], axis=-1).reshape(4, 2, g, n2)
    arow = abar.transpose(1, 2, 0, 3)
    acol = abar[:2].transpose(1, 2, 3, 0)
    bt = lambda b: jnp.concatenate([jnp.swapaxes(b, 2, 3)] * 2, axis=-1)
    ct = lambda c: jnp.concatenate([jnp.swapaxes(c, 2, 3)] * 2, axis=2)
    spec4 = lambda r, c: pl.BlockSpec((2, 1, r, c), lambda i: (0, i, 0, 0))
    return pl.pallas_call(
        _s5_prep_kernel,
        out_shape=[jax.ShapeDtypeStruct((g, S5_CW, S5_CW), BF16),
                   jax.ShapeDtypeStruct((2, g, S5_CW, n2), BF16),
                   jax.ShapeDtypeStruct((2, g, n2, S5_CW), BF16),
                   jax.ShapeDtypeStruct((2, g, 16, n2), F32)],
        grid=(g,),
        in_specs=[spec4(4, n2), spec4(n2, 2),
                  spec4(S5_GROUP, n2), spec4(S5_GROUP, n2), spec4(n2, S5_GROUP), spec4(n2, S5_GROUP)],
        out_specs=[pl.BlockSpec((1, S5_CW, S5_CW), lambda i: (i, 0, 0)),
                   spec4(S5_CW, n2), spec4(n2, S5_CW), spec4(16, n2)],
        compiler_params=_params(1),
        name="s5_prep",
    )(arow, acol, bt(b_re), bt(b_im), ct(c_re), ct(c_im))


def _cmul_rows(x, p1, p2):
    return x * p1 + pltpu.roll(x, S5_N, 1) * p2


S5_OCT = LANES // S5_GROUP
S5_RB = 48


def _s5_core_kernel(u_ref, wi_ref, ws_ref, wo_ref, ap_ref, h0_ref, d_ref, y_ref, fin_ref,
                    ug_ref, yg_ref, z_ref, sp_ref):
    n2 = 2 * S5_N
    blk = lax.broadcasted_iota(jnp.int32, (S5_RB, LANES), 1) >> 4

    def tok_rows(r0, t):
        return pl.ds(r0 * S5_T + t, S5_RB, stride=S5_T)

    def gather(rb, carry):
        r0 = pl.multiple_of(rb * S5_RB, S5_RB)
        for half in range(2):
            xs = [u_ref[tok_rows(r0, S5_OCT * half + tt), :] for tt in range(S5_OCT)]
            for gl in range(S5_OCT):
                acc = None
                for tt in range(S5_OCT):
                    sh = (S5_GROUP * (tt - gl)) % LANES
                    piece = pltpu.roll(xs[tt], sh, 1) if sh else xs[tt]
                    acc = piece if acc is None else jnp.where(blk == tt, piece, acc)
                ug_ref[gl, pl.ds(r0, S5_RB), half * LANES:(half + 1) * LANES] = acc
        return carry

    lax.fori_loop(0, S5_ROWS // S5_RB, gather, 0)

    r = lax.broadcasted_iota(jnp.int32, (S5_ROWS, n2), 0)
    in_p = r < S5_ROWS_P
    rib = jnp.where(in_p, r & (CH_P - 1), (r - S5_ROWS_P) & (CH_S - 1))
    nch = jnp.where(in_p, CH_P, CH_S)

    def group(gl, carry):
        ub = ug_ref[gl].astype(BF16)
        y = jnp.dot(ub, wi_ref[gl], preferred_element_type=F32)
        for d in range(2):
            z_ref[...] = jnp.dot(ub, ws_ref[d, gl], preferred_element_type=F32)
            p1, p2 = ap_ref[d, gl, 0:1, :], ap_ref[d, gl, 8:9, :]
            edge = [S5_ROWS_P + CH_S * b + (0 if d == 0 else CH_S - 1) for b in range(NB_S)]
            for b in range(NB_S):
                h0 = h0_ref[gl, d, b:b + 1, :]
                z_ref[edge[b]:edge[b] + 1, :] = z_ref[edge[b]:edge[b] + 1, :] + _cmul_rows(h0, p1, p2)
            s = z_ref[...]
            for k in range(6):
                sh = 1 << k
                if d == 0:
                    t = jnp.where(rib >= sh, pltpu.roll(s, sh, 0), 0.0)
                else:
                    t = jnp.where(rib < nch - sh, pltpu.roll(s, S5_ROWS - sh, 0), 0.0)
                s = s + _cmul_rows(t, ap_ref[d, gl, k:k + 1, :], ap_ref[d, gl, 8 + k:9 + k, :])
            z_ref[...] = s
            first = CH_P - 1 if d == 0 else 0
            fin_ref[gl, d] = z_ref[pl.ds(first, NB_P, stride=CH_P), :]
            if d == 0:
                sp_ref[...] = jnp.where(rib >= 1, pltpu.roll(s, 1, 0), 0.0)
            else:
                sp_ref[...] = jnp.where(rib < nch - 1, pltpu.roll(s, S5_ROWS - 1, 0), 0.0)
            for b in range(NB_S):
                sp_ref[edge[b]:edge[b] + 1, :] = h0_ref[gl, d, b:b + 1, :]
            y = y + jnp.dot(sp_ref[...].astype(BF16), wo_ref[d, gl], preferred_element_type=F32)
        yg_ref[gl] = y
        return carry

    lax.fori_loop(0, S5_OCT, group, 0)

    def scatter(rb, carry):
        r0 = pl.multiple_of(rb * S5_RB, S5_RB)
        for half in range(2):
            ys = [yg_ref[gl, pl.ds(r0, S5_RB), half * LANES:(half + 1) * LANES] for gl in range(S5_OCT)]
            for tt in range(S5_OCT):
                acc = None
                for gl in range(S5_OCT):
                    sh = (S5_GROUP * (gl - tt)) % LANES
                    piece = pltpu.roll(ys[gl], sh, 1) if sh else ys[gl]
                    acc = piece if acc is None else jnp.where(blk == gl, piece, acc)
                rows = tok_rows(r0, S5_OCT * half + tt)
                y_ref[rows, :] = jax.nn.gelu(acc + d_ref[...] * u_ref[rows, :])
        return carry

    lax.fori_loop(0, S5_ROWS // S5_RB, scatter, 0)


def _s5_core(u, prep, h0, d_skip):
    w_intra, w_state, w_out, apow = prep
    g, n2 = S5_GROUPS, 2 * S5_N
    spec4 = lambda r, c: pl.BlockSpec((2, S5_OCT, r, c), lambda i: (0, i, 0, 0))
    slab = pl.BlockSpec((TOK, LANES), lambda i: (0, i))
    return pl.pallas_call(
        _s5_core_kernel,
        out_shape=[jax.ShapeDtypeStruct((TOK, S5_WIDTH), F32),
                   jax.ShapeDtypeStruct((g, 2, NB_P, n2), F32)],
        grid=(g // S5_OCT,),
        in_specs=[slab,
                  pl.BlockSpec((S5_OCT, S5_CW, S5_CW), lambda i: (i, 0, 0)),
                  spec4(S5_CW, n2), spec4(n2, S5_CW), spec4(16, n2),
                  pl.BlockSpec((S5_OCT, 2, 8, n2), lambda i: (i, 0, 0, 0)),
                  pl.BlockSpec((1, LANES), lambda i: (0, i))],
        out_specs=[slab, pl.BlockSpec((S5_OCT, 2, NB_P, n2), lambda i: (i, 0, 0, 0))],
        scratch_shapes=[pltpu.VMEM((S5_OCT, S5_ROWS, S5_CW), F32), pltpu.VMEM((S5_OCT, S5_ROWS, S5_CW), F32),
                        pltpu.VMEM((S5_ROWS, n2), F32), pltpu.VMEM((S5_ROWS, n2), F32)],
        compiler_params=_params(1, VMEM_LIMIT),
        name="s5_scan",
    )(u, w_intra, w_state, w_out, apow, h0, d_skip[None])


def _outproj_kernel(glu, split1, split2, y_ref, mod_ref, *refs):
    a1, refs = _tok_read(refs, split1)
    a2, refs = _tok_read(refs, split2)
    w1_ref, w2_ref, wg_ref, o_ref = refs
    if glu:
        a2 = a2 * jax.nn.sigmoid(_dot(a2, wg_ref[...]))
    out = _dot(a1, w1_ref[...]) + _dot(a2, w2_ref[...])
    o_ref[...] = y_ref[...] + mod_ref[0][5:6] * out


def _outproj(y, mods_l, a1, a2, w_out, w_glu=None):
    glu = w_glu is not None
    k1 = k2 = w_out.shape[0] // 2
    wg = (w_glu if glu else jnp.zeros((8, LANES), F32)).astype(BF16)
    s1, a1_args = _tok_specs(a1, k1)
    s2, a2_args = _tok_specs(a2, k2)
    return pl.pallas_call(
        functools.partial(_outproj_kernel, glu, isinstance(a1, tuple), isinstance(a2, tuple)),
        out_shape=jax.ShapeDtypeStruct((TOK, D), F32),
        grid=(NT,),
        in_specs=[pl.BlockSpec((TM, D), _row),
                  pl.BlockSpec((1, N_MOD, D), lambda i: (_mod_index(i), 0, 0))] + s1 + s2
                 + [_const_spec((k1, D)), _const_spec((k2, D)), _const_spec(wg.shape)],
        out_specs=pl.BlockSpec((TM, D), _row),
        compiler_params=_params(1, VMEM_LIMIT),
        name="outproj",
    )(y, mods_l, *a1_args, *a2_args, w_out[:k1].astype(BF16), w_out[k1:].astype(BF16), wg)


DFW = DF_HEADS * 2 * DF_DH
IN_B = 3 * HY_WIDTH + 2 * DFW + DF_HEADS * DF_V


def _inproj_b_kernel(y_ref, mod_ref, g_ref, win_ref, c_ref, s_ref,
                     hy_ref, q_ref, kp_ref, ks_ref, vp_ref, vs_ref):
    mod = mod_ref[0]
    h = _modulate(y_ref[...], g_ref[...], mod[3:4], mod[4:5]).astype(BF16)
    p = jnp.dot(h, win_ref[...], preferred_element_type=F32)
    o1 = 3 * HY_WIDTH
    hy_ref[...] = p[:, :o1]
    q_ref[...] = _rope(p[:, o1:o1 + DFW], c_ref[...], s_ref[...])
    _tok_write(kp_ref, ks_ref, _rope(p[:, o1 + DFW:o1 + 2 * DFW], c_ref[...], s_ref[...]))
    _tok_write(vp_ref, vs_ref, p[:, o1 + 2 * DFW:])


def _inproj_b(y, mods_l, g, w_in):
    cs, sn = _rope_tables(DFW, tuple(range(0, DFW, DF_DH)))
    pos = lambda i: (_pos_index(i), 0)
    k_shapes, k_specs = _split_out(DFW)
    v_shapes, v_specs = _split_out(DF_HEADS * DF_V)
    hy_u, q, kp, ks, vp, vs = pl.pallas_call(
        _inproj_b_kernel,
        out_shape=[jax.ShapeDtypeStruct((TOK, 3 * HY_WIDTH), F32), jax.ShapeDtypeStruct((TOK, DFW), F32)]
                  + k_shapes + v_shapes,
        grid=(NT,),
        in_specs=[pl.BlockSpec((TM, D), _row),
                  pl.BlockSpec((1, N_MOD, D), lambda i: (_mod_index(i), 0, 0)),
                  _const_spec((1, D)), _const_spec((D, IN_B)),
                  pl.BlockSpec((TM, DFW), pos), pl.BlockSpec((TM, DFW), pos)],
        out_specs=[pl.BlockSpec((TM, 3 * HY_WIDTH), _row), pl.BlockSpec((TM, DFW), _row)] + k_specs + v_specs,
        compiler_params=_params(1, VMEM_LIMIT),
        name="inproj_odd",
    )(y, mods_l, g[None], w_in.astype(BF16), jnp.asarray(cs), jnp.asarray(sn))
    return hy_u, q, (kp, ks), (vp, vs)


def _diff_attn_kernel(nseg, lam_init, q_ref, lam_ref, sub_ref, *refs):
    o_ref = refs[-1]
    tq = q_ref.shape[0]
    scale = DF_DH ** -0.5
    lp = lam_ref[...]
    lam = (jnp.exp(jnp.sum(lp[0:1] * lp[1:2], axis=-1, keepdims=True))
           - jnp.exp(jnp.sum(lp[2:3] * lp[3:4], axis=-1, keepdims=True)) + lam_init)
    lane = lax.broadcasted_iota(jnp.int32, (tq, LANES), 1)
    for pair in range(DF_HEADS // 2):
        cs = slice(pair * LANES, (pair + 1) * LANES)
        q = (q_ref[:, cs] * scale).astype(BF16)
        ks = [refs[2 * s][:, cs].astype(BF16) for s in range(nseg)]
        vs = [refs[2 * s + 1][:, cs].astype(BF16) for s in range(nseg)]
        outs = []
        for hh in range(2):
            probs = []
            for half in range(2):
                unit = 2 * hh + half
                qm = jnp.where((lane >> 5) == unit, q, jnp.zeros_like(q))
                scores = [_dot_nt(qm, k) for k in ks]
                m = functools.reduce(jnp.maximum, [jnp.max(s, axis=-1, keepdims=True) for s in scores])
                es = [jnp.exp(s - m) for s in scores]
                l = functools.reduce(jnp.add, [jnp.sum(e, axis=-1, keepdims=True) for e in es])
                inv = 1.0 / l
                probs.append([e * inv for e in es])
            o = None
            for s in range(nseg):
                w = (probs[0][s] - lam * probs[1][s]).astype(BF16)
                part = jnp.dot(w, vs[s], preferred_element_type=F32)
                o = part if o is None else o + part
            mine = (lane >> 6) == hh
            ms = jnp.sum(jnp.where(mine, o * o, 0.0), axis=-1, keepdims=True) * (1.0 / DF_V)
            outs.append(o * lax.rsqrt(ms + EPS))
        o_ref[:, cs] = jnp.where(lane < DF_V, outs[0], outs[1]) * sub_ref[...] * (1.0 - lam_init)


def _diff_attention(q, k, v, lam_p, subln, lam_init, n_batch, seq, tq, row0, ctx=None):
    qt = seq // tq
    qb0, kb0 = row0 // tq, 0
    in_specs = [pl.BlockSpec((tq, DFW), lambda b, j: (qb0 + b * qt + j, 0)),
                pl.BlockSpec((4, DF_DH), lambda b, j: (0, 0)),
                pl.BlockSpec((1, LANES), lambda b, j: (0, 0))]
    args = [q, lam_p, jnp.concatenate([subln, subln])[None]]
    segs = []
    if ctx is not None:
        segs.append((ctx, PAST, 0))
    segs.append(((k, v), seq, kb0))
    for (a_k, a_v), ln, off in segs:
        idx = lambda b, j, off=off: (off + b, 0)
        in_specs += [pl.BlockSpec((ln, DFW), idx), pl.BlockSpec((ln, DF_HEADS * DF_V), idx)]
        args += [a_k, a_v]
    return pl.pallas_call(
        functools.partial(_diff_attn_kernel, len(segs), lam_init),
        out_shape=jax.ShapeDtypeStruct((n_batch * seq, DF_HEADS * DF_V), F32),
        grid=(n_batch, qt),
        in_specs=in_specs,
        out_specs=pl.BlockSpec((tq, DF_HEADS * DF_V), lambda b, j: (b * qt + j, 0)),
        compiler_params=_params(2, VMEM_LIMIT),
        name="diff_attention",
    )(*args)


def _hy_filter_kernel(feat_ref, w1_ref, b1_ref, w2_ref, b2_ref, fq_ref, w3_ref, dec_ref, o_ref):
    feat = feat_ref[...]
    fq = fq_ref[...]
    h = jnp.sin(fq * (_dot3(feat, w1_ref[...]) + b1_ref[...]))
    h = jnp.sin(fq * (_dot3(h, w2_ref[...]) + b2_ref[...]))
    window = jnp.exp(-feat[:, 0:1] * jnp.abs(dec_ref[...]))
    for j in range(4):
        cs = slice(j * HY_WIDTH, (j + 1) * HY_WIDTH)
        o_ref[:, cs] = _dot3(h, w3_ref[:, cs]) * window


def _hy_spectrum_kernel(L, csh_ref, csl_ref, hf_ref, hb_ref, o_ref):
    row = lax.broadcasted_iota(jnp.int32, (L, HY_WIDTH), 0)
    first = row == 0
    tf = _dot3_const(csh_ref[...], csl_ref[...], hf_ref[...])
    tb = _dot3_const(csh_ref[...], csl_ref[...], jnp.where(first, 0.0, hb_ref[...]))
    ka = tf[:L] + tb[:L]
    kb = jnp.where(first, tf[L:] + tb[L:], tf[L:] - tb[L:])
    wv = jnp.where(first, 1.0 / (2 * L), 2.0 / (2 * L))
    o_ref[0, 0] = ka * wv
    o_ref[0, 1] = jnp.where(first, 0.0, kb) * wv
    o_ref[0, 2] = jnp.where(first, kb, ka) * wv


def _hy_conv_kernel(L, csh_ref, csl_ref, cth_ref, ctl_ref, kf_ref, v_ref, x1_ref, x2_ref,
                    wv_ref, w1_ref, w2_ref, bias_ref, o_ref):
    row = lax.broadcasted_iota(jnp.int32, v_ref.shape, 0)

    def short(x_ref, w_ref):
        x = x_ref[...]
        prev = jnp.where(row >= 1, pltpu.roll(x, 1, 0), 0.0)
        nxt = jnp.where(row <= L - 2, pltpu.roll(x, L - 1, 0), 0.0)
        return w_ref[0:1] * prev + w_ref[1:2] * x + w_ref[2:3] * nxt

    z = short(v_ref, wv_ref)
    gates = (short(x1_ref, w1_ref), short(x2_ref, w2_ref))
    for n in range(2):
        ab = _dot3_const(csh_ref[...], csl_ref[...], z)
        a, b = ab[:L], ab[L:]
        ka, kb1, ka2 = kf_ref[n, 0], kf_ref[n, 1], kf_ref[n, 2]
        pq = jnp.concatenate([a * ka - b * kb1, a * kb1 + b * ka2], axis=0)
        conv = _dot3_const(cth_ref[...], ctl_ref[...], pq)
        z = gates[n] * (conv + bias_ref[n:n + 1] * z)
    o_ref[...] = z


def _hyena_spectrum(L, phy):
    conv_w, w1, b1, w2, b2, freq, w3, decay, bias = phy
    feat = jnp.asarray(_hyena_features(L))
    w1p = jnp.pad(w1, ((0, LANES - HY_EMB), (0, 0)))
    filt = pl.pallas_call(
        _hy_filter_kernel,
        out_shape=jax.ShapeDtypeStruct((L, 4 * HY_WIDTH), F32),
        grid=(1,),
        in_specs=[_const_spec((L, LANES)), _const_spec((LANES, HY_FH)), _const_spec((1, HY_FH)),
                  _const_spec((HY_FH, HY_FH)), _const_spec((1, HY_FH)), _const_spec((1, HY_FH)),
                  _const_spec((HY_FH, 4 * HY_WIDTH)), _const_spec((1, HY_WIDTH))],
        out_specs=pl.BlockSpec((L, 4 * HY_WIDTH), lambda i: (0, 0)),
        compiler_params=_params(1, VMEM_LIMIT),
        name="hyena_filter",
    )(feat, w1p, b1[None], w2, b2[None], freq[None], w3, decay[None])
    csh, csl, _, _ = (jnp.asarray(t) for t in _dft_tables(L))
    return pl.pallas_call(
        functools.partial(_hy_spectrum_kernel, L),
        out_shape=jax.ShapeDtypeStruct((2, 3, L, HY_WIDTH), F32),
        grid=(2,),
        in_specs=[_const_spec((2 * L, L)), _const_spec((2 * L, L)),
                  pl.BlockSpec((L, HY_WIDTH), lambda n: (0, n)),
                  pl.BlockSpec((L, HY_WIDTH), lambda n: (0, 2 + n))],
        out_specs=pl.BlockSpec((1, 3, L, HY_WIDTH), lambda n: (n, 0, 0, 0)),
        compiler_params=_params(1, VMEM_LIMIT),
        name="hyena_spectrum",
    )(csh, csl, filt, filt)


def _hyena_conv(hy_u, spec, phy, n_batch, L, cb, row0):
    conv_w, bias = phy[0], phy[8]
    csh, csl, cth, ctl = (jnp.asarray(t) for t in _dft_tables(L))
    nc = HY_WIDTH // cb
    rb0 = row0 // L
    col = lambda off: (lambda b, c: (0, off * nc + c))
    tok = lambda off: (lambda b, c: (rb0 + b, off * nc + c))
    return pl.pallas_call(
        functools.partial(_hy_conv_kernel, L),
        out_shape=jax.ShapeDtypeStruct((n_batch * L, HY_WIDTH), F32),
        grid=(n_batch, nc),
        in_specs=[_const_spec((2 * L, L)), _const_spec((2 * L, L)),
                  _const_spec((L, 2 * L)), _const_spec((L, 2 * L)),
                  pl.BlockSpec((2, 3, L, cb), lambda b, c: (0, 0, 0, c)),
                  pl.BlockSpec((L, cb), tok(0)), pl.BlockSpec((L, cb), tok(1)), pl.BlockSpec((L, cb), tok(2)),
                  pl.BlockSpec((3, cb), col(0)), pl.BlockSpec((3, cb), col(1)), pl.BlockSpec((3, cb), col(2)),
                  pl.BlockSpec((2, cb), col(0))],
        out_specs=pl.BlockSpec((L, cb), lambda b, c: (b, c)),
        compiler_params=_params(2, VMEM_LIMIT),
        name="hyena_conv",
    )(csh, csl, cth, ctl, spec, hy_u, hy_u, hy_u, conv_w, conv_w, conv_w, bias)


def _even_mixer(y, mods_l, g, pa, ps5, ctx_ckv, ctx_krope, ctx_state):
    w_in, w_out, q_norm, w_uq, kv_norm, w_ukv = pa
    a_re, a_im, log_step, b_re, b_im, c_re, c_im, d_skip, w_glu = ps5
    q, ckv, kr_unrot, kr_rot, kn, v, u, (w_k, w_v) = _inproj_a(y, mods_l, g, w_in, q_norm, w_uq, kv_norm, w_ukv)

    ctx_flat = ctx_ckv.reshape(NB_S * PAST, MLA_KV_RANK)
    ctx_kn = _linear(ctx_flat, w_k, PAST)
    ctx_v = _linear(ctx_flat, w_v, PAST)
    ctx_kr = jnp.pad(ctx_krope.reshape(NB_S * PAST, MLA_ROPE), ((0, 0), (KR_AT, LANES - KR_AT - MLA_ROPE)))
    att_p = _mla_attention(q, kn, kr_rot, v, NB_P, L_P, L_P, 0)
    att_s = _mla_attention(q, kn, kr_rot, v, NB_S, L_S, TM, TOK_P, ctx=(ctx_kn, ctx_kr, ctx_v))

    prep = _s5_prep(a_re, a_im, log_step, b_re, b_im, c_re, c_im)
    h0 = ctx_state.transpose(3, 1, 0, 2, 4).reshape(S5_GROUPS, 2, NB_S, 2 * S5_N)
    h0 = jnp.pad(h0, ((0, 0), (0, 0), (0, 8 - NB_S), (0, 0)))
    s5y, fin = _s5_core(u, prep, h0, d_skip)

    y = _outproj(y, mods_l, (att_p, att_s), s5y, w_out, w_glu)
    new_ckv = ckv.reshape(NB_P, L_P, MLA_KV_RANK)
    new_krope = kr_unrot[:, KR_AT:KR_AT + MLA_ROPE].reshape(NB_P, L_P, MLA_ROPE)
    new_state = fin.reshape(S5_GROUPS, 2, NB_P, 2, S5_N).transpose(2, 1, 3, 0, 4)
    return y, new_ckv, new_krope, new_state


def _odd_mixer(y, mods_l, g, pb, phy, ctx_k, ctx_v, lam_init):
    w_in, w_out, lam_p, subln = pb
    hy_u, q, (k_p, k_s), (v_p, v_s) = _inproj_b(y, mods_l, g, w_in)
    hy_p = _hyena_conv(hy_u, _hyena_spectrum(L_P, phy), phy, NB_P, L_P, HY_WIDTH, 0)
    hy_s = _hyena_conv(hy_u, _hyena_spectrum(L_S, phy), phy, NB_S, L_S, HY_WIDTH // 2, TOK_P)
    ctx = (ctx_k.reshape(NB_S * PAST, DFW), ctx_v.reshape(NB_S * PAST, DF_HEADS * DF_V))
    att_p = _diff_attention(q, k_p, v_p, lam_p, subln, lam_init, NB_P, L_P, L_P, 0)
    att_s = _diff_attention(q, k_s, v_s, lam_p, subln, lam_init, NB_S, L_S, TM // 2, TOK_P, ctx=ctx)
    y = _outproj(y, mods_l, (hy_p, hy_s), (att_p, att_s), w_out)
    new_k = k_p.reshape(NB_P, L_P, DF_HEADS, 2, DF_DH)
    new_v = v_p.reshape(NB_P, L_P, DF_HEADS, DF_V)
    return y, new_k, new_v


def kernel(x_prompt, x_sample, c, c_ctx, cache_mla_ckv, cache_mla_krope, state_s5, cache_diff_k, cache_diff_v, ada_w, ada_b, norm_g, ff_w_in, ff_w_out, w_in_a, w_out_a, mla_q_norm, mla_w_uq, mla_kv_norm, mla_w_ukv, s5_a_re, s5_a_im, s5_log_step, s5_b_re, s5_b_im, s5_c_re, s5_c_im, s5_d, s5_w_glu, w_in_b, w_out_b, hy_conv, hy_w1, hy_b1, hy_w2, hy_b2, hy_freq, hy_w3, hy_decay, hy_bias, df_lambda, df_subln, final_norm):
    depth = ada_w.shape[0]
    y = (x_prompt.reshape(TOK_P, D), x_sample.reshape(TOK_S, D))
    mods = _adaln(jnp.concatenate([c_ctx[None], c], axis=0), ada_w, ada_b)
    new_ckv, new_krope, new_s5, new_dk, new_dv = [], [], [], [], []
    for l in range(depth):
        y = _half_ffn(y, mods[l], norm_g[l, 0], ff_w_in[l, 0], ff_w_out[l, 0], 0)
        if l % 2 == 0:
            e = l // 2
            pa = (w_in_a[e], w_out_a[e], mla_q_norm[e], mla_w_uq[e], mla_kv_norm[e], mla_w_ukv[e])
            ps5 = (s5_a_re[e], s5_a_im[e], s5_log_step[e], s5_b_re[e], s5_b_im[e],
                   s5_c_re[e], s5_c_im[e], s5_d[e], s5_w_glu[e])
            y, ckv, krope, st = _even_mixer(y, mods[l], norm_g[l, 1], pa, ps5, cache_mla_ckv[:, e],
                                            cache_mla_krope[:, e], state_s5[:, e])
            new_ckv.append(ckv)
            new_krope.append(krope)
            new_s5.append(st)
        else:
            o = l // 2
            lam_init = 0.8 - 0.6 * math.exp(-0.3 * l)
            pb = (w_in_b[o], w_out_b[o], df_lambda[o], df_subln[o])
            phy = (hy_conv[o], hy_w1[o], hy_b1[o], hy_w2[o], hy_b2[o], hy_freq[o],
                   hy_w3[o], hy_decay[o], hy_bias[o])
            y, dk, dv = _odd_mixer(y, mods[l], norm_g[l, 1], pb, phy, cache_diff_k[:, o],
                                   cache_diff_v[:, o], lam_init)
            new_dk.append(dk)
            new_dv.append(dv)
        last = l == depth - 1
        y = _half_ffn(y, mods[l], norm_g[l, 2], ff_w_in[l, 1], ff_w_out[l, 1], 6,
                      final_g=final_norm if last else None)
    y_prompt = y[0].reshape(NB_P, L_P, D)
    y_sample = y[1].reshape(NB_S, L_S, D)
    return (y_prompt, y_sample, jnp.stack(new_ckv, axis=1), jnp.stack(new_krope, axis=1),
            jnp.stack(new_s5, axis=1), jnp.stack(new_dk, axis=1), jnp.stack(new_dv, axis=1))
```

```python
import functools
import math

import ml_dtypes
import numpy as np
import jax
import jax.numpy as jnp
from jax import lax
from jax.experimental import pallas as pl
from jax.experimental.pallas import tpu as pltpu

F32 = jnp.float32
BF16 = jnp.bfloat16

D = 1024
NB_P, L_P = 16, 256
NB_S, L_S = 2, 1024
PAST = 256
GRID_W = 64
N_MOD = 9
FF = 2816
EPS = 1e-6
ROPE_BASE = 10000.0

MLA_HEADS, MLA_NOPE, MLA_ROPE, MLA_V = 8, 64, 32, 64
MLA_Q_RANK, MLA_KV_RANK = 384, 256
S5_WIDTH, S5_GROUP, S5_N = 512, 16, 64
S5_GROUPS = S5_WIDTH // S5_GROUP
HY_WIDTH, HY_BANDS, HY_FH = 512, 16, 64
HY_EMB = 2 * HY_BANDS + 1
DF_HEADS, DF_DH = 8, 32
DF_V = 2 * DF_DH

TOK_P = NB_P * L_P
TOK_S = NB_S * L_S
TOK = TOK_P + TOK_S
TM = 512
NT = TOK // TM
NT_P = TOK_P // TM
TILES_PER_SAMPLE = L_S // TM

LANES = 128
S5_T = 16
S5_CW = S5_T * S5_GROUP
CH_P = L_P // S5_T
CH_S = L_S // S5_T
S5_ROWS = NB_P * CH_P + NB_S * CH_S
S5_ROWS_P = NB_P * CH_P

VMEM_LIMIT = 56 * 1024 * 1024


def _params(n_grid, vmem=None):
    return pltpu.CompilerParams(dimension_semantics=("arbitrary",) * n_grid,
                                vmem_limit_bytes=vmem)


def _const_spec(shape):
    nd = len(shape)
    return pl.BlockSpec(shape, lambda *_: (0,) * nd, pipeline_mode=pl.Buffered(1))


def _mod_index(i):
    return jnp.where(i < NT_P, 0, 1 + (i - NT_P) // TILES_PER_SAMPLE)


def _pos_index(i):
    return jnp.where(i < NT_P, 0, 1 + (i - NT_P) % TILES_PER_SAMPLE)


def _row(i):
    return (i, 0)


def _row_p(i):
    return (jnp.minimum(i, NT_P - 1), 0)


def _row_s(i):
    return (jnp.maximum(i - NT_P, 0), 0)


def _tok_specs(x, width):
    if isinstance(x, tuple):
        return [pl.BlockSpec((TM, width), _row_p), pl.BlockSpec((TM, width), _row_s)], list(x)
    return [pl.BlockSpec((TM, width), _row)], [x]


def _tok_read(refs, split):
    if split:
        return jnp.where(pl.program_id(0) < NT_P, refs[0][...], refs[1][...]), refs[2:]
    return refs[0][...], refs[1:]


def _tok_write(p_ref, s_ref, value):
    i = pl.program_id(0)

    @pl.when(i < NT_P)
    def _():
        p_ref[...] = value

    @pl.when(i >= NT_P)
    def _():
        s_ref[...] = value


def _split_out(width):
    shapes = [jax.ShapeDtypeStruct((TOK_P, width), F32), jax.ShapeDtypeStruct((TOK_S, width), F32)]
    specs = [pl.BlockSpec((TM, width), _row_p), pl.BlockSpec((TM, width), _row_s)]
    return shapes, specs


def _dot(a, b):
    return jnp.dot(a.astype(BF16), b.astype(BF16), preferred_element_type=F32)


def _dot_nt(a, b):
    return lax.dot_general(a, b, (((1,), (1,)), ((), ())), preferred_element_type=F32)


def _split(x):
    hi = x.astype(BF16)
    lo = (x - hi.astype(F32)).astype(BF16)
    return hi, lo


def _dot3(a, b):
    ah, al = _split(a)
    bh, bl = _split(b)
    d = functools.partial(jnp.dot, preferred_element_type=F32)
    return d(ah, bh) + d(ah, bl) + d(al, bh)


def _dot3_const(ch, cl, x):
    xh, xl = _split(x)
    d = functools.partial(jnp.dot, preferred_element_type=F32)
    return d(ch, xh) + d(ch, xl) + d(cl, xh)


def _rmsnorm(x, g):
    return x * lax.rsqrt(jnp.mean(x * x, axis=-1, keepdims=True) + EPS) * g


def _modulate(y, g, shift, scale):
    return _rmsnorm(y, g) * (1.0 + scale) + shift


def _pair_swap(x):
    n = x.shape[-1]
    lane = lax.broadcasted_iota(jnp.int32, x.shape, x.ndim - 1)
    return jnp.where((lane & 1) == 0, pltpu.roll(x, n - 1, x.ndim - 1), pltpu.roll(x, 1, x.ndim - 1))


def _rope(x, cos, sin_signed):
    return x * cos + _pair_swap(x) * sin_signed


def _rope_angles():
    n_freq = MLA_ROPE // 4
    inv = 1.0 / (ROPE_BASE ** (np.arange(n_freq, dtype=np.float64) / n_freq))
    pos = np.arange(L_S)
    row = (pos // GRID_W).astype(np.float64)
    col = (pos % GRID_W).astype(np.float64)
    ang = np.concatenate([row[:, None] * inv, col[:, None] * inv], axis=-1)
    return np.cos(ang), np.sin(ang)


@functools.lru_cache(maxsize=None)
def _rope_tables(width, starts):
    cos, sin = _rope_angles()
    c = np.ones((TM + L_S, width), np.float32)
    s = np.zeros((TM + L_S, width), np.float32)
    sign = np.where(np.arange(MLA_ROPE) % 2 == 0, -1.0, 1.0)
    unit_c = np.repeat(cos, 2, axis=1)
    unit_s = np.repeat(sin, 2, axis=1) * sign
    for st in starts:
        c[TM:, st:st + MLA_ROPE] = unit_c
        s[TM:, st:st + MLA_ROPE] = unit_s
    return c, s


@functools.lru_cache(maxsize=None)
def _dft_tables(L):
    f = np.arange(L)[:, None]
    s = np.arange(L)[None, :]
    ang = np.pi * ((f * s) % (2 * L)).astype(np.float64) / L
    cs = np.concatenate([np.cos(ang), np.sin(ang)], axis=0)
    cs[L, :] = np.where(np.arange(L) % 2 == 0, 1.0, -1.0)
    hi = cs.astype(ml_dtypes.bfloat16)
    lo = (cs - hi.astype(np.float64)).astype(ml_dtypes.bfloat16)
    return hi, lo, np.ascontiguousarray(hi.T), np.ascontiguousarray(lo.T)


@functools.lru_cache(maxsize=None)
def _hyena_features(L):
    t = np.arange(L, dtype=np.float64) / L
    bands = np.arange(1, HY_BANDS + 1, dtype=np.float64)
    ang = 2.0 * math.pi * t[:, None] * bands
    feat = np.zeros((L, LANES), np.float32)
    feat[:, 0] = t
    feat[:, 1:1 + HY_BANDS] = np.cos(ang)
    feat[:, 1 + HY_BANDS:HY_EMB] = np.sin(ang)
    return feat


def _adaln_kernel(c_ref, w_ref, b_ref, o_ref):
    w = w_ref[0]
    o_ref[0] = jnp.zeros(o_ref.shape[1:], F32)
    for m in range(c_ref.shape[0]):
        col = jax.nn.silu(c_ref[m])
        o_ref[0, m:m + 1, :] = jnp.sum(w * col, axis=0, keepdims=True) + b_ref[0]


def _adaln(cvecs, ada_w, ada_b):
    depth = ada_w.shape[0]
    n_vec = cvecs.shape[0]
    tn = D
    out = pl.pallas_call(
        _adaln_kernel,
        out_shape=jax.ShapeDtypeStruct((depth, 8, N_MOD * D), F32),
        grid=(depth, N_MOD),
        in_specs=[pl.BlockSpec((n_vec, D, 1), lambda l, j: (0, 0, 0)),
                  pl.BlockSpec((1, D, tn), lambda l, j: (l, 0, j)),
                  pl.BlockSpec((1, 1, tn), lambda l, j: (l, 0, j))],
        out_specs=pl.BlockSpec((1, 8, tn), lambda l, j: (l, 0, j)),
        compiler_params=_params(2),
        name="adaln",
    )(cvecs[:, :, None], ada_w, ada_b[:, None, :])
    return out[:, :n_vec].reshape(depth, n_vec, N_MOD, D)


FF_CHUNK = FF // 2
FF_LOADS = 11
W_IN_PIECE = 2 * FF // FF_LOADS
W_OUT_PIECE = FF // FF_LOADS


def _ffn_kernel(base, final, split_in, layer, which, *refs):
    y, refs = _tok_read(refs, split_in)
    mod_ref, g_ref, win_hbm, wout_hbm, fg_ref = refs[:5]
    n_out = 2 if final else 1
    outs = refs[5:5 + n_out]
    win_ref, wout_ref, stage_in, stage_out, sems = refs[5 + n_out:]

    @pl.when(pl.program_id(0) == 0)
    def _():
        def copies(c, slot):
            return (pltpu.make_async_copy(win_hbm.at[layer, which, :, pl.ds(c * W_IN_PIECE, W_IN_PIECE)],
                                          stage_in.at[slot], sems.at[0, slot]),
                    pltpu.make_async_copy(wout_hbm.at[layer, which, pl.ds(c * W_OUT_PIECE, W_OUT_PIECE), :],
                                          stage_out.at[slot], sems.at[1, slot]))

        for cp in copies(0, 0):
            cp.start()
        for c in range(FF_LOADS):
            slot = c % 2
            if c + 1 < FF_LOADS:
                for cp in copies(c + 1, 1 - slot):
                    cp.start()
            for cp in copies(c, slot):
                cp.wait()
            win_ref[:, c * W_IN_PIECE:(c + 1) * W_IN_PIECE] = stage_in[slot].astype(BF16)
            wout_ref[c * W_OUT_PIECE:(c + 1) * W_OUT_PIECE, :] = stage_out[slot].astype(BF16)

    mod = mod_ref[0]
    h = _modulate(y, g_ref[...], mod[base:base + 1], mod[base + 1:base + 2]).astype(BF16)
    acc = jnp.zeros(y.shape, F32)
    for c in range(FF // FF_CHUNK):
        lo = c * FF_CHUNK
        gate = jnp.dot(h, win_ref[:, lo:lo + FF_CHUNK], preferred_element_type=F32)
        up = jnp.dot(h, win_ref[:, FF + lo:FF + lo + FF_CHUNK], preferred_element_type=F32)
        a = (jax.nn.silu(gate) * up).astype(BF16)
        acc = acc + jnp.dot(a, wout_ref[lo:lo + FF_CHUNK, :], preferred_element_type=F32)
    out = y + 0.5 * mod[base + 2:base + 3] * acc
    if final:
        _tok_write(outs[0], outs[1], _rmsnorm(out, fg_ref[...]))
    else:
        outs[0][...] = out


def _half_ffn(y, mods_l, g, ff_w_in, ff_w_out, layer, which, final_g=None):
    final = final_g is not None
    fg = final_g if final else g
    y_specs, y_args = _tok_specs(y, D)
    if final:
        out_shape, out_specs = _split_out(D)
    else:
        out_shape, out_specs = jax.ShapeDtypeStruct((TOK, D), F32), pl.BlockSpec((TM, D), _row)
    return pl.pallas_call(
        functools.partial(_ffn_kernel, 6 * which, final, isinstance(y, tuple), layer, which),
        out_shape=out_shape,
        grid=(NT,),
        in_specs=y_specs + [pl.BlockSpec((1, N_MOD, D), lambda i: (_mod_index(i), 0, 0)),
                            _const_spec((1, D)),
                            pl.BlockSpec(memory_space=pl.ANY),
                            pl.BlockSpec(memory_space=pl.ANY),
                            _const_spec((1, D))],
        out_specs=out_specs,
        scratch_shapes=[pltpu.VMEM((D, 2 * FF), BF16), pltpu.VMEM((FF, D), BF16),
                        pltpu.VMEM((2, D, W_IN_PIECE), F32), pltpu.VMEM((2, W_OUT_PIECE, D), F32),
                        pltpu.SemaphoreType.DMA((2, 2))],
        compiler_params=_params(1, VMEM_LIMIT),
        name="half_ffn",
    )(*y_args, mods_l, g[None], ff_w_in, ff_w_out, fg[None])


def _linear_kernel(x_ref, w_ref, o_ref):
    o_ref[...] = _dot(x_ref[...], w_ref[...])


def _linear(x, w, tm):
    m, k = x.shape
    n = w.shape[1]
    return pl.pallas_call(
        _linear_kernel,
        out_shape=jax.ShapeDtypeStruct((m, n), F32),
        grid=(m // tm,),
        in_specs=[pl.BlockSpec((tm, k), lambda i: (i, 0)), _const_spec((k, n))],
        out_specs=pl.BlockSpec((tm, n), lambda i: (i, 0)),
        compiler_params=_params(1),
        name="linear",
    )(x, w.astype(BF16))


QW = MLA_HEADS * LANES
KR_AT = MLA_NOPE
IN_A_PAD = MLA_Q_RANK + MLA_KV_RANK + S5_WIDTH + LANES


def _inproj_a_kernel(y_ref, mod_ref, g_ref, win_ref, qn_ref, wuq_ref, kvn_ref, wk_ref, wv_ref,
                     cq_ref, sq_ref, ck_ref, sk_ref,
                     q_ref, ckv_ref, kru_ref, krr_ref, kn_ref, v_ref, u_ref):
    mod = mod_ref[0]
    h = _modulate(y_ref[...], g_ref[...], mod[3:4], mod[4:5]).astype(BF16)
    p = jnp.dot(h, win_ref[...], preferred_element_type=F32)
    o1 = MLA_Q_RANK
    o2 = o1 + MLA_KV_RANK
    o3 = o2 + S5_WIDTH
    q = _dot(_rmsnorm(p[:, :o1], qn_ref[...]), wuq_ref[...])
    q_ref[...] = _rope(q, cq_ref[...], sq_ref[...])
    ckv = _rmsnorm(p[:, o1:o2], kvn_ref[...])
    ckv_b = ckv.astype(BF16)
    kn_ref[...] = jnp.dot(ckv_b, wk_ref[...], preferred_element_type=F32)
    v_ref[...] = jnp.dot(ckv_b, wv_ref[...], preferred_element_type=F32)
    u_ref[...] = p[:, o2:o3]
    krp = p[:, o3:]
    krr_ref[...] = _rope(krp, ck_ref[...], sk_ref[...])

    @pl.when(pl.program_id(0) < NT_P)
    def _():
        ckv_ref[...] = ckv
        kru_ref[...] = krp


def _inproj_a(y, mods_l, g, w_in, q_norm, w_uq, kv_norm, w_ukv):
    o1 = MLA_Q_RANK
    o2 = o1 + MLA_KV_RANK
    o3 = o2 + MLA_ROPE
    kr_cols = jnp.pad(w_in[:, o2:o3], ((0, 0), (KR_AT, LANES - KR_AT - MLA_ROPE)))
    w_ext = jnp.concatenate([w_in[:, :o2], w_in[:, o3:], kr_cols], axis=1).astype(BF16)
    dq = MLA_NOPE + MLA_ROPE
    w_uq_pad = jnp.pad(w_uq.reshape(MLA_Q_RANK, MLA_HEADS, dq),
                       ((0, 0), (0, 0), (0, LANES - dq))).reshape(MLA_Q_RANK, QW).astype(BF16)
    w_kv = w_ukv.reshape(MLA_KV_RANK, MLA_HEADS, MLA_NOPE + MLA_V)
    w_k = jnp.pad(w_kv[:, :, :MLA_NOPE], ((0, 0), (0, 0), (0, LANES - MLA_NOPE))).reshape(MLA_KV_RANK, QW)
    w_v = w_kv[:, :, MLA_NOPE:].reshape(MLA_KV_RANK, MLA_HEADS * MLA_V)
    w_k, w_v = w_k.astype(BF16), w_v.astype(BF16)
    cq, sq = _rope_tables(QW, tuple(h * LANES + MLA_NOPE for h in range(MLA_HEADS)))
    ck, sk = _rope_tables(LANES, (KR_AT,))
    row = _row
    pos = lambda i: (_pos_index(i), 0)
    widths = (QW, MLA_KV_RANK, LANES, LANES, QW, MLA_HEADS * MLA_V, S5_WIDTH)
    prompt_only = (1, 2)
    outs = pl.pallas_call(
        _inproj_a_kernel,
        out_shape=[jax.ShapeDtypeStruct((TOK_P if k in prompt_only else TOK, w), F32)
                   for k, w in enumerate(widths)],
        grid=(NT,),
        in_specs=[pl.BlockSpec((TM, D), row),
                  pl.BlockSpec((1, N_MOD, D), lambda i: (_mod_index(i), 0, 0)),
                  _const_spec((1, D)),
                  _const_spec((D, IN_A_PAD)),
                  _const_spec((1, MLA_Q_RANK)),
                  _const_spec((MLA_Q_RANK, QW)),
                  _const_spec((1, MLA_KV_RANK)),
                  _const_spec((MLA_KV_RANK, QW)),
                  _const_spec((MLA_KV_RANK, MLA_HEADS * MLA_V)),
                  pl.BlockSpec((TM, QW), pos), pl.BlockSpec((TM, QW), pos),
                  pl.BlockSpec((TM, LANES), pos), pl.BlockSpec((TM, LANES), pos)],
        out_specs=[pl.BlockSpec((TM, w), _row_p if k in prompt_only else row)
                   for k, w in enumerate(widths)],
        compiler_params=_params(1, VMEM_LIMIT),
        name="inproj_even",
    )(y, mods_l, g[None], w_ext, q_norm[None], w_uq_pad, kv_norm[None], w_k, w_v,
      jnp.asarray(cq), jnp.asarray(sq), jnp.asarray(ck), jnp.asarray(sk))
    q, ckv, kr_unrot, kr_rot, kn, v, u = outs
    return q, ckv, kr_unrot, kr_rot, kn, v, u, (w_k, w_v)


def _mla_attn_kernel(nseg, q_ref, *refs):
    o_ref = refs[-1]
    tq = q_ref.shape[0]
    scale = (MLA_NOPE + MLA_ROPE) ** -0.5
    lane = lax.broadcasted_iota(jnp.int32, (tq, LANES), 1)
    for pair in range(MLA_HEADS // 2):
        outs = []
        for hh in range(2):
            h = 2 * pair + hh
            hs = slice(h * LANES, (h + 1) * LANES)
            qh = (q_ref[:, hs] * scale).astype(BF16)
            scores = []
            for s in range(nseg):
                kn_ref, kr_ref = refs[3 * s], refs[3 * s + 1]
                kh = (kn_ref[:, hs] + kr_ref[...]).astype(BF16)
                scores.append(_dot_nt(qh, kh))
            m = functools.reduce(jnp.maximum, [jnp.max(s, axis=-1, keepdims=True) for s in scores])
            es = [jnp.exp(s - m) for s in scores]
            l = functools.reduce(jnp.add, [jnp.sum(e, axis=-1, keepdims=True) for e in es])
            o = None
            for s in range(nseg):
                v_ref = refs[3 * s + 2]
                part = _dot(es[s], v_ref[:, pair * LANES:(pair + 1) * LANES])
                o = part if o is None else o + part
            outs.append(o / l)
        o_ref[:, pair * LANES:(pair + 1) * LANES] = jnp.where(lane < MLA_V, outs[0], outs[1])


def _mla_attention(q, kn, kr, v, n_batch, seq, tq, row0, ctx=None):
    qt = seq // tq
    qb0, kb0 = row0 // tq, row0 // seq
    in_specs = [pl.BlockSpec((tq, QW), lambda b, j: (qb0 + b * qt + j, 0))]
    args = [q]
    segs = []
    if ctx is not None:
        segs.append((ctx, PAST, 0))
    segs.append(((kn, kr, v), seq, kb0))
    for (a_kn, a_kr, a_v), ln, off in segs:
        idx = lambda b, j, off=off: (off + b, 0)
        in_specs += [pl.BlockSpec((ln, QW), idx), pl.BlockSpec((ln, LANES), idx),
                     pl.BlockSpec((ln, MLA_HEADS * MLA_V), idx)]
        args += [a_kn, a_kr, a_v]
    return pl.pallas_call(
        functools.partial(_mla_attn_kernel, len(segs)),
        out_shape=jax.ShapeDtypeStruct((n_batch * seq, MLA_HEADS * MLA_V), F32),
        grid=(n_batch, qt),
        in_specs=in_specs,
        out_specs=pl.BlockSpec((tq, MLA_HEADS * MLA_V), lambda b, j: (b * qt + j, 0)),
        compiler_params=_params(2, VMEM_LIMIT),
        name="mla_attention",
    )(*args)


def _cpow(ar, ai, e, nbits):
    rr = jnp.ones_like(ar)
    ri = jnp.zeros_like(ar)
    br, bi = ar, ai
    for k in range(nbits):
        bit = ((e >> k) & 1) == 1
        nr = rr * br - ri * bi
        ni = rr * bi + ri * br
        rr = jnp.where(bit, nr, rr)
        ri = jnp.where(bit, ni, ri)
        if k + 1 < nbits:
            br, bi = br * br - bi * bi, 2.0 * br * bi
    return rr, ri


def _s5_abar_kernel(lr_ref, li_ref, ls_ref, o_ref):
    step = jnp.exp(ls_ref[...])
    lr = jnp.minimum(lr_ref[...], -1e-4)
    li = li_ref[...]
    mag = jnp.exp(lr * step)
    ar = mag * jnp.cos(li * step)
    ai = mag * jnp.sin(li * step)
    den = lr * lr + li * li
    o_ref[0] = ar
    o_ref[1] = ai
    o_ref[2] = ((ar - 1.0) * lr + ai * li) / den
    o_ref[3] = (ai * lr - (ar - 1.0) * li) / den


def _s5_prep_kernel(arow_ref, acol_ref, btr_ref, bti_ref, ctr_ref, cti_ref,
                    wi_ref, ws_ref, wo_ref, ap_ref):
    n2 = 2 * S5_N
    lane_o = lax.broadcasted_iota(jnp.int32, (n2, S5_CW), 1)
    blk_o = lane_o >> 4
    row_o = lax.broadcasted_iota(jnp.int32, (n2, S5_CW), 0)
    lane_k = lax.broadcasted_iota(jnp.int32, (S5_GROUP, S5_CW), 1)
    row_k = lax.broadcasted_iota(jnp.int32, (S5_GROUP, S5_CW), 0)
    lane_b = lax.broadcasted_iota(jnp.int32, (S5_GROUP, n2), 1)
    lane_a = lax.broadcasted_iota(jnp.int32, (1, n2), 1)
    rep = ((lane_k & (S5_GROUP - 1)) == row_k).astype(BF16)

    def tile16(x):
        hi = x.astype(BF16)
        r1 = x - hi.astype(F32)
        mid = r1.astype(BF16)
        lo = (r1 - mid.astype(F32)).astype(BF16)
        d = functools.partial(jnp.dot, preferred_element_type=F32)
        return d(hi, rep) + d(mid, rep) + d(lo, rep)

    intra = [None] * S5_T
    for d in range(2):
        ar, ai, fr, fi = (arow_ref[d, 0, k:k + 1, :] for k in range(4))
        btr, bti = btr_ref[d, 0], bti_ref[d, 0]
        bbr = fr * btr - fi * bti
        bbi = fr * bti + fi * btr
        pws = [(jnp.ones_like(ar), jnp.zeros_like(ar))]
        for _ in range(S5_T):
            pr, pi = pws[-1]
            pws.append((pr * ar - pi * ai, pr * ai + pi * ar))
        for s in range(S5_T):
            pr, pi = pws[S5_T - 1 - s] if d == 0 else pws[s]
            ws_ref[d, 0, s * S5_GROUP:(s + 1) * S5_GROUP, :] = jnp.where(
                lane_b < S5_N, pr * bbr - pi * bbi, pr * bbi + pi * bbr).astype(BF16)

        acol = acol_ref[d, 0]
        arc = jnp.broadcast_to(acol[:, 0:1], (n2, S5_CW))
        aic = jnp.broadcast_to(acol[:, 1:2], (n2, S5_CW))
        ctr, cti = tile16(ctr_ref[d, 0]), tile16(cti_ref[d, 0])
        e_lag = blk_o if d == 0 else (S5_T - 1 - blk_o)
        pqr, pqi = _cpow(arc, aic, e_lag, 4)
        qr = pqr * ctr - pqi * cti
        qi = pqr * cti + pqi * ctr
        wo_ref[d, 0] = jnp.where(row_o < S5_N, qr * arc - qi * aic, -(qr * aic + qi * arc)).astype(BF16)
        q_stack = jnp.where(row_o < S5_N, qr, qi)
        bb_mix = jnp.where(lane_b < S5_N, bbr, -bbi)
        kt = _dot3(bb_mix, q_stack)
        for s in range(S5_T):
            if d == 0:
                blk = jnp.where(lane_k >= S5_GROUP * s, pltpu.roll(kt, S5_GROUP * s, 1), 0.0)
            else:
                blk = jnp.where(lane_k < S5_GROUP * (s + 1),
                                pltpu.roll(kt, (S5_GROUP * (s + 1)) % S5_CW, 1), 0.0)
            intra[s] = blk if intra[s] is None else intra[s] + blk

        pr1, pi1 = pws[S5_T]
        for k in range(6):
            ap_ref[d, 0, k:k + 1, :] = pr1
            ap_ref[d, 0, 8 + k:9 + k, :] = jnp.where(lane_a < S5_N, -pi1, pi1)
            pr1, pi1 = pr1 * pr1 - pi1 * pi1, 2.0 * pr1 * pi1
        ap_ref[d, 0, 6:8, :] = jnp.zeros((2, n2), F32)
        ap_ref[d, 0, 14:16, :] = jnp.zeros((2, n2), F32)
    for s in range(S5_T):
        wi_ref[0, s * S5_GROUP:(s + 1) * S5_GROUP, :] = intra[s].astype(BF16)


def _s5_prep(a_re, a_im, log_step, b_re, b_im, c_re, c_im):
    g, n, n2 = S5_GROUPS, S5_N, 2 * S5_N
    abar = pl.pallas_call(
        _s5_abar_kernel,
        out_shape=jax.ShapeDtypeStruct((4, 2 * g, n), F32),
        grid=(1,),
        in_specs=[_const_spec((2 * g, n)), _const_spec((2 * g, n)), _const_spec((2 * g, 1))],
        out_specs=pl.BlockSpec((4, 2 * g, n), lambda i: (0, 0, 0)),
        compiler_params=_params(1),
        name="s5_abar",
    )(a_re.reshape(2 * g, n), a_im.reshape(2 * g, n), log_step.reshape(2 * g, 1))
    abar = jnp.concatenate([abar, abar], axis=-1).reshape(4, 2, g, n2)
    arow = abar.transpose(1, 2, 0, 3)
    acol = abar[:2].transpose(1, 2, 3, 0)
    bt = lambda b: jnp.concatenate([jnp.swapaxes(b, 2, 3)] * 2, axis=-1)
    ct = lambda c: jnp.concatenate([jnp.swapaxes(c, 2, 3)] * 2, axis=2)
    spec4 = lambda r, c: pl.BlockSpec((2, 1, r, c), lambda i: (0, i, 0, 0))
    return pl.pallas_call(
        _s5_prep_kernel,
        out_shape=[jax.ShapeDtypeStruct((g, S5_CW, S5_CW), BF16),
                   jax.ShapeDtypeStruct((2, g, S5_CW, n2), BF16),
                   jax.ShapeDtypeStruct((2, g, n2, S5_CW), BF16),
                   jax.ShapeDtypeStruct((2, g, 16, n2), F32)],
        grid=(g,),
        in_specs=[spec4(4, n2), spec4(n2, 2),
                  spec4(S5_GROUP, n2), spec4(S5_GROUP, n2), spec4(n2, S5_GROUP), spec4(n2, S5_GROUP)],
        out_specs=[pl.BlockSpec((1, S5_CW, S5_CW), lambda i: (i, 0, 0)),
                   spec4(S5_CW, n2), spec4(n2, S5_CW), spec4(16, n2)],
        compiler_params=_params(1),
        name="s5_prep",
    )(arow, acol, bt(b_re), bt(b_im), ct(c_re), ct(c_im))


def _cmul_rows(x, p1, p2):
    return x * p1 + pltpu.roll(x, S5_N, 1) * p2


S5_OCT = LANES // S5_GROUP
S5_RB = 48


def _s5_core_kernel(u_ref, wi_ref, ws_ref, wo_ref, ap_ref, h0_ref, d_ref, y_ref, fin_ref,
                    ug_ref, yg_ref, z_ref, sp_ref):
    n2 = 2 * S5_N
    blk = lax.broadcasted_iota(jnp.int32, (S5_RB, LANES), 1) >> 4

    def tok_rows(r0, t):
        return pl.ds(r0 * S5_T + t, S5_RB, stride=S5_T)

    def block_transpose(xs):
        for b in range(3):
            s = 1 << b
            odd = ((blk >> b) & 1) == 1
            new = list(xs)
            for i in range(S5_OCT):
                if not i & s:
                    new[i] = jnp.where(odd, pltpu.roll(xs[i + s], s * S5_GROUP, 1), xs[i])
                    new[i + s] = jnp.where(odd, xs[i + s], pltpu.roll(xs[i], LANES - s * S5_GROUP, 1))
            xs = new
        return xs

    def gather(rb, carry):
        r0 = pl.multiple_of(rb * S5_RB, S5_RB)
        for half in range(2):
            xs = [u_ref[tok_rows(r0, S5_OCT * half + tt), :] for tt in range(S5_OCT)]
            for gl, x in enumerate(block_transpose(xs)):
                ug_ref[gl, pl.ds(r0, S5_RB), half * LANES:(half + 1) * LANES] = x
        return carry

    lax.fori_loop(0, S5_ROWS // S5_RB, gather, 0)

    r = lax.broadcasted_iota(jnp.int32, (S5_ROWS, n2), 0)
    in_p = r < S5_ROWS_P
    rib = jnp.where(in_p, r & (CH_P - 1), (r - S5_ROWS_P) & (CH_S - 1))
    nch = jnp.where(in_p, CH_P, CH_S)

    def group(gl, carry):
        ub = ug_ref[gl].astype(BF16)
        y = jnp.dot(ub, wi_ref[gl], preferred_element_type=F32)
        for d in range(2):
            z_ref[...] = jnp.dot(ub, ws_ref[d, gl], preferred_element_type=F32)
            p1, p2 = ap_ref[d, gl, 0:1, :], ap_ref[d, gl, 8:9, :]
            edge = [S5_ROWS_P + CH_S * b + (0 if d == 0 else CH_S - 1) for b in range(NB_S)]
            for b in range(NB_S):
                h0 = h0_ref[gl, d, b:b + 1, :]
                z_ref[edge[b]:edge[b] + 1, :] = z_ref[edge[b]:edge[b] + 1, :] + _cmul_rows(h0, p1, p2)
            s = z_ref[...]
            for k in range(6):
                sh = 1 << k
                if d == 0:
                    t = jnp.where(rib >= sh, pltpu.roll(s, sh, 0), 0.0)
                else:
                    t = jnp.where(rib < nch - sh, pltpu.roll(s, S5_ROWS - sh, 0), 0.0)
                s = s + _cmul_rows(t, ap_ref[d, gl, k:k + 1, :], ap_ref[d, gl, 8 + k:9 + k, :])
            z_ref[...] = s
            first = CH_P - 1 if d == 0 else 0
            fin_ref[gl, d] = z_ref[pl.ds(first, NB_P, stride=CH_P), :]
            if d == 0:
                sp_ref[...] = jnp.where(rib >= 1, pltpu.roll(s, 1, 0), 0.0)
            else:
                sp_ref[...] = jnp.where(rib < nch - 1, pltpu.roll(s, S5_ROWS - 1, 0), 0.0)
            for b in range(NB_S):
                sp_ref[edge[b]:edge[b] + 1, :] = h0_ref[gl, d, b:b + 1, :]
            y = y + jnp.dot(sp_ref[...].astype(BF16), wo_ref[d, gl], preferred_element_type=F32)
        yg_ref[gl] = y
        return carry

    lax.fori_loop(0, S5_OCT, group, 0)

    def scatter(rb, carry):
        r0 = pl.multiple_of(rb * S5_RB, S5_RB)
        for half in range(2):
            ys = [yg_ref[gl, pl.ds(r0, S5_RB), half * LANES:(half + 1) * LANES] for gl in range(S5_OCT)]
            for tt, acc in enumerate(block_transpose(ys)):
                rows = tok_rows(r0, S5_OCT * half + tt)
                y_ref[rows, :] = jax.nn.gelu(acc + d_ref[...] * u_ref[rows, :])
        return carry

    lax.fori_loop(0, S5_ROWS // S5_RB, scatter, 0)


def _s5_core(u, prep, h0, d_skip):
    w_intra, w_state, w_out, apow = prep
    g, n2 = S5_GROUPS, 2 * S5_N
    spec4 = lambda r, c: pl.BlockSpec((2, S5_OCT, r, c), lambda i: (0, i, 0, 0))
    slab = pl.BlockSpec((TOK, LANES), lambda i: (0, i))
    return pl.pallas_call(
        _s5_core_kernel,
        out_shape=[jax.ShapeDtypeStruct((TOK, S5_WIDTH), F32),
                   jax.ShapeDtypeStruct((g, 2, NB_P, n2), F32)],
        grid=(g // S5_OCT,),
        in_specs=[slab,
                  pl.BlockSpec((S5_OCT, S5_CW, S5_CW), lambda i: (i, 0, 0)),
                  spec4(S5_CW, n2), spec4(n2, S5_CW), spec4(16, n2),
                  pl.BlockSpec((S5_OCT, 2, 8, n2), lambda i: (i, 0, 0, 0)),
                  pl.BlockSpec((1, LANES), lambda i: (0, i))],
        out_specs=[slab, pl.BlockSpec((S5_OCT, 2, NB_P, n2), lambda i: (i, 0, 0, 0))],
        scratch_shapes=[pltpu.VMEM((S5_OCT, S5_ROWS, S5_CW), F32), pltpu.VMEM((S5_OCT, S5_ROWS, S5_CW), F32),
                        pltpu.VMEM((S5_ROWS, n2), F32), pltpu.VMEM((S5_ROWS, n2), F32)],
        compiler_params=_params(1, VMEM_LIMIT),
        name="s5_scan",
    )(u, w_intra, w_state, w_out, apow, h0, d_skip[None])


def _outproj_kernel(glu, split1, split2, y_ref, mod_ref, *refs):
    a1, refs = _tok_read(refs, split1)
    a2, refs = _tok_read(refs, split2)
    w1_ref, w2_ref, wg_ref, o_ref = refs
    if glu:
        a2 = a2 * jax.nn.sigmoid(_dot(a2, wg_ref[...]))
    out = _dot(a1, w1_ref[...]) + _dot(a2, w2_ref[...])
    o_ref[...] = y_ref[...] + mod_ref[0][5:6] * out


def _outproj(y, mods_l, a1, a2, w_out, w_glu=None):
    glu = w_glu is not None
    k1 = k2 = w_out.shape[0] // 2
    wg = (w_glu if glu else jnp.zeros((8, LANES), F32)).astype(BF16)
    s1, a1_args = _tok_specs(a1, k1)
    s2, a2_args = _tok_specs(a2, k2)
    return pl.pallas_call(
        functools.partial(_outproj_kernel, glu, isinstance(a1, tuple), isinstance(a2, tuple)),
        out_shape=jax.ShapeDtypeStruct((TOK, D), F32),
        grid=(NT,),
        in_specs=[pl.BlockSpec((TM, D), _row),
                  pl.BlockSpec((1, N_MOD, D), lambda i: (_mod_index(i), 0, 0))] + s1 + s2
                 + [_const_spec((k1, D)), _const_spec((k2, D)), _const_spec(wg.shape)],
        out_specs=pl.BlockSpec((TM, D), _row),
        compiler_params=_params(1, VMEM_LIMIT),
        name="outproj",
    )(y, mods_l, *a1_args, *a2_args, w_out[:k1].astype(BF16), w_out[k1:].astype(BF16), wg)


DFW = DF_HEADS * 2 * DF_DH
IN_B = 3 * HY_WIDTH + 2 * DFW + DF_HEADS * DF_V


def _inproj_b_kernel(y_ref, mod_ref, g_ref, win_ref, c_ref, s_ref,
                     hy_ref, q_ref, kp_ref, ks_ref, vp_ref, vs_ref):
    mod = mod_ref[0]
    h = _modulate(y_ref[...], g_ref[...], mod[3:4], mod[4:5]).astype(BF16)
    p = jnp.dot(h, win_ref[...], preferred_element_type=F32)
    o1 = 3 * HY_WIDTH
    hy_ref[...] = p[:, :o1]
    q_ref[...] = _rope(p[:, o1:o1 + DFW], c_ref[...], s_ref[...])
    _tok_write(kp_ref, ks_ref, _rope(p[:, o1 + DFW:o1 + 2 * DFW], c_ref[...], s_ref[...]))
    _tok_write(vp_ref, vs_ref, p[:, o1 + 2 * DFW:])


def _inproj_b(y, mods_l, g, w_in):
    cs, sn = _rope_tables(DFW, tuple(range(0, DFW, DF_DH)))
    pos = lambda i: (_pos_index(i), 0)
    k_shapes, k_specs = _split_out(DFW)
    v_shapes, v_specs = _split_out(DF_HEADS * DF_V)
    hy_u, q, kp, ks, vp, vs = pl.pallas_call(
        _inproj_b_kernel,
        out_shape=[jax.ShapeDtypeStruct((TOK, 3 * HY_WIDTH), F32), jax.ShapeDtypeStruct((TOK, DFW), F32)]
                  + k_shapes + v_shapes,
        grid=(NT,),
        in_specs=[pl.BlockSpec((TM, D), _row),
                  pl.BlockSpec((1, N_MOD, D), lambda i: (_mod_index(i), 0, 0)),
                  _const_spec((1, D)), _const_spec((D, IN_B)),
                  pl.BlockSpec((TM, DFW), pos), pl.BlockSpec((TM, DFW), pos)],
        out_specs=[pl.BlockSpec((TM, 3 * HY_WIDTH), _row), pl.BlockSpec((TM, DFW), _row)] + k_specs + v_specs,
        compiler_params=_params(1, VMEM_LIMIT),
        name="inproj_odd",
    )(y, mods_l, g[None], w_in.astype(BF16), jnp.asarray(cs), jnp.asarray(sn))
    return hy_u, q, (kp, ks), (vp, vs)


def _diff_attn_kernel(nseg, lam_init, q_ref, lam_ref, sub_ref, *refs):
    o_ref = refs[-1]
    tq = q_ref.shape[0]
    scale = DF_DH ** -0.5
    lp = lam_ref[...]
    lam = (jnp.exp(jnp.sum(lp[0:1] * lp[1:2], axis=-1, keepdims=True))
           - jnp.exp(jnp.sum(lp[2:3] * lp[3:4], axis=-1, keepdims=True)) + lam_init)
    lane = lax.broadcasted_iota(jnp.int32, (tq, LANES), 1)
    for pair in range(DF_HEADS // 2):
        cs = slice(pair * LANES, (pair + 1) * LANES)
        q = (q_ref[:, cs] * scale).astype(BF16)
        ks = [refs[2 * s][:, cs].astype(BF16) for s in range(nseg)]
        vs = [refs[2 * s + 1][:, cs].astype(BF16) for s in range(nseg)]
        outs = []
        for hh in range(2):
            probs = []
            for half in range(2):
                unit = 2 * hh + half
                qm = jnp.where((lane >> 5) == unit, q, jnp.zeros_like(q))
                scores = [_dot_nt(qm, k) for k in ks]
                m = functools.reduce(jnp.maximum, [jnp.max(s, axis=-1, keepdims=True) for s in scores])
                es = [jnp.exp(s - m) for s in scores]
                l = functools.reduce(jnp.add, [jnp.sum(e, axis=-1, keepdims=True) for e in es])
                inv = 1.0 / l
                probs.append([e * inv for e in es])
            o = None
            for s in range(nseg):
                w = (probs[0][s] - lam * probs[1][s]).astype(BF16)
                part = jnp.dot(w, vs[s], preferred_element_type=F32)
                o = part if o is None else o + part
            mine = (lane >> 6) == hh
            ms = jnp.sum(jnp.where(mine, o * o, 0.0), axis=-1, keepdims=True) * (1.0 / DF_V)
            outs.append(o * lax.rsqrt(ms + EPS))
        o_ref[:, cs] = jnp.where(lane < DF_V, outs[0], outs[1]) * sub_ref[...] * (1.0 - lam_init)


def _diff_attention(q, k, v, lam_p, subln, lam_init, n_batch, seq, tq, row0, ctx=None):
    qt = seq // tq
    qb0, kb0 = row0 // tq, 0
    in_specs = [pl.BlockSpec((tq, DFW), lambda b, j: (qb0 + b * qt + j, 0)),
                pl.BlockSpec((4, DF_DH), lambda b, j: (0, 0)),
                pl.BlockSpec((1, LANES), lambda b, j: (0, 0))]
    args = [q, lam_p, jnp.concatenate([subln, subln])[None]]
    segs = []
    if ctx is not None:
        segs.append((ctx, PAST, 0))
    segs.append(((k, v), seq, kb0))
    for (a_k, a_v), ln, off in segs:
        idx = lambda b, j, off=off: (off + b, 0)
        in_specs += [pl.BlockSpec((ln, DFW), idx), pl.BlockSpec((ln, DF_HEADS * DF_V), idx)]
        args += [a_k, a_v]
    return pl.pallas_call(
        functools.partial(_diff_attn_kernel, len(segs), lam_init),
        out_shape=jax.ShapeDtypeStruct((n_batch * seq, DF_HEADS * DF_V), F32),
        grid=(n_batch, qt),
        in_specs=in_specs,
        out_specs=pl.BlockSpec((tq, DF_HEADS * DF_V), lambda b, j: (b * qt + j, 0)),
        compiler_params=_params(2, VMEM_LIMIT),
        name="diff_attention",
    )(*args)


def _hy_filter_kernel(feat_ref, w1_ref, b1_ref, w2_ref, b2_ref, fq_ref, w3_ref, dec_ref, o_ref):
    feat = feat_ref[...]
    fq = fq_ref[...]
    h = jnp.sin(fq * (_dot3(feat, w1_ref[...]) + b1_ref[...]))
    h = jnp.sin(fq * (_dot3(h, w2_ref[...]) + b2_ref[...]))
    window = jnp.exp(-feat[:, 0:1] * jnp.abs(dec_ref[...]))
    for j in range(4):
        cs = slice(j * HY_WIDTH, (j + 1) * HY_WIDTH)
        o_ref[:, cs] = _dot3(h, w3_ref[:, cs]) * window


def _hy_spectrum_kernel(L, csh_ref, csl_ref, hf_ref, hb_ref, o_ref):
    row = lax.broadcasted_iota(jnp.int32, (L, HY_WIDTH), 0)
    first = row == 0
    tf = _dot3_const(csh_ref[...], csl_ref[...], hf_ref[...])
    tb = _dot3_const(csh_ref[...], csl_ref[...], jnp.where(first, 0.0, hb_ref[...]))
    ka = tf[:L] + tb[:L]
    kb = jnp.where(first, tf[L:] + tb[L:], tf[L:] - tb[L:])
    wv = jnp.where(first, 1.0 / (2 * L), 2.0 / (2 * L))
    o_ref[0, 0] = ka * wv
    o_ref[0, 1] = jnp.where(first, 0.0, kb) * wv
    o_ref[0, 2] = jnp.where(first, kb, ka) * wv


def _hy_conv_kernel(L, csh_ref, csl_ref, cth_ref, ctl_ref, kf_ref, v_ref, x1_ref, x2_ref,
                    wv_ref, w1_ref, w2_ref, bias_ref, o_ref):
    row = lax.broadcasted_iota(jnp.int32, v_ref.shape, 0)

    def short(x_ref, w_ref):
        x = x_ref[...]
        prev = jnp.where(row >= 1, pltpu.roll(x, 1, 0), 0.0)
        nxt = jnp.where(row <= L - 2, pltpu.roll(x, L - 1, 0), 0.0)
        return w_ref[0:1] * prev + w_ref[1:2] * x + w_ref[2:3] * nxt

    z = short(v_ref, wv_ref)
    gates = (short(x1_ref, w1_ref), short(x2_ref, w2_ref))
    for n in range(2):
        ab = _dot3_const(csh_ref[...], csl_ref[...], z)
        a, b = ab[:L], ab[L:]
        ka, kb1, ka2 = kf_ref[n, 0], kf_ref[n, 1], kf_ref[n, 2]
        pq = jnp.concatenate([a * ka - b * kb1, a * kb1 + b * ka2], axis=0)
        conv = _dot3_const(cth_ref[...], ctl_ref[...], pq)
        z = gates[n] * (conv + bias_ref[n:n + 1] * z)
    o_ref[...] = z


def _hyena_spectrum(L, phy):
    conv_w, w1, b1, w2, b2, freq, w3, decay, bias = phy
    feat = jnp.asarray(_hyena_features(L))
    w1p = jnp.pad(w1, ((0, LANES - HY_EMB), (0, 0)))
    filt = pl.pallas_call(
        _hy_filter_kernel,
        out_shape=jax.ShapeDtypeStruct((L, 4 * HY_WIDTH), F32),
        grid=(1,),
        in_specs=[_const_spec((L, LANES)), _const_spec((LANES, HY_FH)), _const_spec((1, HY_FH)),
                  _const_spec((HY_FH, HY_FH)), _const_spec((1, HY_FH)), _const_spec((1, HY_FH)),
                  _const_spec((HY_FH, 4 * HY_WIDTH)), _const_spec((1, HY_WIDTH))],
        out_specs=pl.BlockSpec((L, 4 * HY_WIDTH), lambda i: (0, 0)),
        compiler_params=_params(1, VMEM_LIMIT),
        name="hyena_filter",
    )(feat, w1p, b1[None], w2, b2[None], freq[None], w3, decay[None])
    csh, csl, _, _ = (jnp.asarray(t) for t in _dft_tables(L))
    return pl.pallas_call(
        functools.partial(_hy_spectrum_kernel, L),
        out_shape=jax.ShapeDtypeStruct((2, 3, L, HY_WIDTH), F32),
        grid=(2,),
        in_specs=[_const_spec((2 * L, L)), _const_spec((2 * L, L)),
                  pl.BlockSpec((L, HY_WIDTH), lambda n: (0, n)),
                  pl.BlockSpec((L, HY_WIDTH), lambda n: (0, 2 + n))],
        out_specs=pl.BlockSpec((1, 3, L, HY_WIDTH), lambda n: (n, 0, 0, 0)),
        compiler_params=_params(1, VMEM_LIMIT),
        name="hyena_spectrum",
    )(csh, csl, filt, filt)


def _hyena_conv(hy_u, spec, phy, n_batch, L, cb, row0):
    conv_w, bias = phy[0], phy[8]
    csh, csl, cth, ctl = (jnp.asarray(t) for t in _dft_tables(L))
    nc = HY_WIDTH // cb
    rb0 = row0 // L
    col = lambda off: (lambda b, c: (0, off * nc + c))
    tok = lambda off: (lambda b, c: (rb0 + b, off * nc + c))
    return pl.pallas_call(
        functools.partial(_hy_conv_kernel, L),
        out_shape=jax.ShapeDtypeStruct((n_batch * L, HY_WIDTH), F32),
        grid=(n_batch, nc),
        in_specs=[_const_spec((2 * L, L)), _const_spec((2 * L, L)),
                  _const_spec((L, 2 * L)), _const_spec((L, 2 * L)),
                  pl.BlockSpec((2, 3, L, cb), lambda b, c: (0, 0, 0, c)),
                  pl.BlockSpec((L, cb), tok(0)), pl.BlockSpec((L, cb), tok(1)), pl.BlockSpec((L, cb), tok(2)),
                  pl.BlockSpec((3, cb), col(0)), pl.BlockSpec((3, cb), col(1)), pl.BlockSpec((3, cb), col(2)),
                  pl.BlockSpec((2, cb), col(0))],
        out_specs=pl.BlockSpec((L, cb), lambda b, c: (b, c)),
        compiler_params=_params(2, VMEM_LIMIT),
        name="hyena_conv",
    )(csh, csl, cth, ctl, spec, hy_u, hy_u, hy_u, conv_w, conv_w, conv_w, bias)


def _even_mixer(y, mods_l, g, pa, ps5, ctx_ckv, ctx_krope, ctx_state):
    w_in, w_out, q_norm, w_uq, kv_norm, w_ukv = pa
    a_re, a_im, log_step, b_re, b_im, c_re, c_im, d_skip, w_glu = ps5
    q, ckv, kr_unrot, kr_rot, kn, v, u, (w_k, w_v) = _inproj_a(y, mods_l, g, w_in, q_norm, w_uq, kv_norm, w_ukv)

    ctx_flat = ctx_ckv.reshape(NB_S * PAST, MLA_KV_RANK)
    ctx_kn = _linear(ctx_flat, w_k, PAST)
    ctx_v = _linear(ctx_flat, w_v, PAST)
    ctx_kr = jnp.pad(ctx_krope.reshape(NB_S * PAST, MLA_ROPE), ((0, 0), (KR_AT, LANES - KR_AT - MLA_ROPE)))
    att_p = _mla_attention(q, kn, kr_rot, v, NB_P, L_P, L_P, 0)
    att_s = _mla_attention(q, kn, kr_rot, v, NB_S, L_S, TM, TOK_P, ctx=(ctx_kn, ctx_kr, ctx_v))

    prep = _s5_prep(a_re, a_im, log_step, b_re, b_im, c_re, c_im)
    h0 = ctx_state.transpose(3, 1, 0, 2, 4).reshape(S5_GROUPS, 2, NB_S, 2 * S5_N)
    h0 = jnp.pad(h0, ((0, 0), (0, 0), (0, 8 - NB_S), (0, 0)))
    s5y, fin = _s5_core(u, prep, h0, d_skip)

    y = _outproj(y, mods_l, (att_p, att_s), s5y, w_out, w_glu)
    new_ckv = ckv.reshape(NB_P, L_P, MLA_KV_RANK)
    new_krope = kr_unrot[:, KR_AT:KR_AT + MLA_ROPE].reshape(NB_P, L_P, MLA_ROPE)
    new_state = fin.reshape(S5_GROUPS, 2, NB_P, 2, S5_N).transpose(2, 1, 3, 0, 4)
    return y, new_ckv, new_krope, new_state


def _odd_mixer(y, mods_l, g, pb, phy, ctx_k, ctx_v, lam_init):
    w_in, w_out, lam_p, subln = pb
    hy_u, q, (k_p, k_s), (v_p, v_s) = _inproj_b(y, mods_l, g, w_in)
    hy_p = _hyena_conv(hy_u, _hyena_spectrum(L_P, phy), phy, NB_P, L_P, HY_WIDTH, 0)
    hy_s = _hyena_conv(hy_u, _hyena_spectrum(L_S, phy), phy, NB_S, L_S, HY_WIDTH // 2, TOK_P)
    ctx = (ctx_k.reshape(NB_S * PAST, DFW), ctx_v.reshape(NB_S * PAST, DF_HEADS * DF_V))
    att_p = _diff_attention(q, k_p, v_p, lam_p, subln, lam_init, NB_P, L_P, L_P, 0)
    att_s = _diff_attention(q, k_s, v_s, lam_p, subln, lam_init, NB_S, L_S, TM // 2, TOK_P, ctx=ctx)
    y = _outproj(y, mods_l, (hy_p, hy_s), (att_p, att_s), w_out)
    new_k = k_p.reshape(NB_P, L_P, DF_HEADS, 2, DF_DH)
    new_v = v_p.reshape(NB_P, L_P, DF_HEADS, DF_V)
    return y, new_k, new_v


def kernel(x_prompt, x_sample, c, c_ctx, cache_mla_ckv, cache_mla_krope, state_s5, cache_diff_k, cache_diff_v, ada_w, ada_b, norm_g, ff_w_in, ff_w_out, w_in_a, w_out_a, mla_q_norm, mla_w_uq, mla_kv_norm, mla_w_ukv, s5_a_re, s5_a_im, s5_log_step, s5_b_re, s5_b_im, s5_c_re, s5_c_im, s5_d, s5_w_glu, w_in_b, w_out_b, hy_conv, hy_w1, hy_b1, hy_w2, hy_b2, hy_freq, hy_w3, hy_decay, hy_bias, df_lambda, df_subln, final_norm):
    depth = ada_w.shape[0]
    y = (x_prompt.reshape(TOK_P, D), x_sample.reshape(TOK_S, D))
    mods = _adaln(jnp.concatenate([c_ctx[None], c], axis=0), ada_w, ada_b)
    new_ckv, new_krope, new_s5, new_dk, new_dv = [], [], [], [], []
    for l in range(depth):
        y = _half_ffn(y, mods[l], norm_g[l, 0], ff_w_in, ff_w_out, l, 0)
        if l % 2 == 0:
            e = l // 2
            pa = (w_in_a[e], w_out_a[e], mla_q_norm[e], mla_w_uq[e], mla_kv_norm[e], mla_w_ukv[e])
            ps5 = (s5_a_re[e], s5_a_im[e], s5_log_step[e], s5_b_re[e], s5_b_im[e],
                   s5_c_re[e], s5_c_im[e], s5_d[e], s5_w_glu[e])
            y, ckv, krope, st = _even_mixer(y, mods[l], norm_g[l, 1], pa, ps5, cache_mla_ckv[:, e],
                                            cache_mla_krope[:, e], state_s5[:, e])
            new_ckv.append(ckv)
            new_krope.append(krope)
            new_s5.append(st)
        else:
            o = l // 2
            lam_init = 0.8 - 0.6 * math.exp(-0.3 * l)
            pb = (w_in_b[o], w_out_b[o], df_lambda[o], df_subln[o])
            phy = (hy_conv[o], hy_w1[o], hy_b1[o], hy_w2[o], hy_b2[o], hy_freq[o],
                   hy_w3[o], hy_decay[o], hy_bias[o])
            y, dk, dv = _odd_mixer(y, mods[l], norm_g[l, 1], pb, phy, cache_diff_k[:, o],
                                   cache_diff_v[:, o], lam_init)
            new_dk.append(dk)
            new_dv.append(dv)
        last = l == depth - 1
        y = _half_ffn(y, mods[l], norm_g[l, 2], ff_w_in, ff_w_out, l, 1,
                      final_g=final_norm if last else None)
    y_prompt = y[0].reshape(NB_P, L_P, D)
    y_sample = y[1].reshape(NB_S, L_S, D)
    return (y_prompt, y_sample, jnp.stack(new_ckv, axis=1), jnp.stack(new_krope, axis=1),
            jnp.stack(new_s5, axis=1), jnp.stack(new_dk, axis=1), jnp.stack(new_dv, axis=1))
```

```python
import functools
import math

import numpy as np
import jax
import jax.numpy as jnp
from jax import lax
from jax.experimental import pallas as pl
from jax.experimental.pallas import tpu as pltpu

F32 = jnp.float32
BF16 = jnp.bfloat16

D = 1024
NB_P, L_P = 16, 256
NB_S, L_S = 2, 1024
PAST = 256
GRID_W = 64
N_MOD = 9
FF = 2816
EPS = 1e-6
ROPE_BASE = 10000.0

MLA_HEADS, MLA_NOPE, MLA_ROPE, MLA_V = 8, 64, 32, 64
MLA_Q_RANK, MLA_KV_RANK = 384, 256
S5_WIDTH, S5_GROUP, S5_N = 512, 16, 64
S5_GROUPS = S5_WIDTH // S5_GROUP
HY_WIDTH, HY_BANDS, HY_FH = 512, 16, 64
HY_EMB = 2 * HY_BANDS + 1
DF_HEADS, DF_DH = 8, 32
DF_V = 2 * DF_DH

TOK_P = NB_P * L_P
TOK_S = NB_S * L_S
TOK = TOK_P + TOK_S
TM = 512
NT = TOK // TM
NT_P = TOK_P // TM
TILES_PER_SAMPLE = L_S // TM

LANES = 128
S5_T = 16
S5_CW = S5_T * S5_GROUP
CH_P = L_P // S5_T
CH_S = L_S // S5_T
S5_ROWS = NB_P * CH_P + NB_S * CH_S
S5_ROWS_P = NB_P * CH_P

VMEM_LIMIT = 56 * 1024 * 1024


def _params(n_grid, vmem=None):
    return pltpu.CompilerParams(dimension_semantics=("arbitrary",) * n_grid,
                                vmem_limit_bytes=vmem)


def _const_spec(shape):
    nd = len(shape)
    return pl.BlockSpec(shape, lambda *_: (0,) * nd, pipeline_mode=pl.Buffered(1))


def _mod_index(i):
    return jnp.where(i < NT_P, 0, 1 + (i - NT_P) // TILES_PER_SAMPLE)


def _pos_index(i):
    return jnp.where(i < NT_P, 0, 1 + (i - NT_P) % TILES_PER_SAMPLE)


def _row(i):
    return (i, 0)


def _row_p(i):
    return (jnp.minimum(i, NT_P - 1), 0)


def _row_s(i):
    return (jnp.maximum(i - NT_P, 0), 0)


def _tok_specs(x, width):
    if isinstance(x, tuple):
        return [pl.BlockSpec((TM, width), _row_p), pl.BlockSpec((TM, width), _row_s)], list(x)
    return [pl.BlockSpec((TM, width), _row)], [x]


def _tok_read(refs, split):
    if split:
        return jnp.where(pl.program_id(0) < NT_P, refs[0][...], refs[1][...]), refs[2:]
    return refs[0][...], refs[1:]


def _tok_write(p_ref, s_ref, value):
    i = pl.program_id(0)

    @pl.when(i < NT_P)
    def _():
        p_ref[...] = value

    @pl.when(i >= NT_P)
    def _():
        s_ref[...] = value.astype(s_ref.dtype)


def _split_out(width, sample_dtype=F32):
    shapes = [jax.ShapeDtypeStruct((TOK_P, width), F32), jax.ShapeDtypeStruct((TOK_S, width), sample_dtype)]
    specs = [pl.BlockSpec((TM, width), _row_p), pl.BlockSpec((TM, width), _row_s)]
    return shapes, specs


def _dot(a, b):
    return jnp.dot(a.astype(BF16), b.astype(BF16), preferred_element_type=F32)


def _dot_nt(a, b):
    return lax.dot_general(a, b, (((1,), (1,)), ((), ())), preferred_element_type=F32)


def _split(x):
    hi = x.astype(BF16)
    lo = (x - hi.astype(F32)).astype(BF16)
    return hi, lo


def _dot3(a, b):
    ah, al = _split(a)
    bh, bl = _split(b)
    d = functools.partial(jnp.dot, preferred_element_type=F32)
    return d(ah, bh) + d(ah, bl) + d(al, bh)


def _rmsnorm(x, g):
    return x * lax.rsqrt(jnp.mean(x * x, axis=-1, keepdims=True) + EPS) * g


def _modulate(y, g, shift, scale):
    return _rmsnorm(y, g) * (1.0 + scale) + shift


def _pair_swap(x):
    n = x.shape[-1]
    lane = lax.broadcasted_iota(jnp.int32, x.shape, x.ndim - 1)
    return jnp.where((lane & 1) == 0, pltpu.roll(x, n - 1, x.ndim - 1), pltpu.roll(x, 1, x.ndim - 1))


def _rope(x, cos, sin_signed):
    return x * cos + _pair_swap(x) * sin_signed


def _rope_angles():
    n_freq = MLA_ROPE // 4
    inv = 1.0 / (ROPE_BASE ** (np.arange(n_freq, dtype=np.float64) / n_freq))
    pos = np.arange(L_S)
    row = (pos // GRID_W).astype(np.float64)
    col = (pos % GRID_W).astype(np.float64)
    ang = np.concatenate([row[:, None] * inv, col[:, None] * inv], axis=-1)
    return np.cos(ang), np.sin(ang)


@functools.lru_cache(maxsize=None)
def _rope_tables(width, starts):
    cos, sin = _rope_angles()
    c = np.ones((TM + L_S, width), np.float32)
    s = np.zeros((TM + L_S, width), np.float32)
    sign = np.where(np.arange(MLA_ROPE) % 2 == 0, -1.0, 1.0)
    unit_c = np.repeat(cos, 2, axis=1)
    unit_s = np.repeat(sin, 2, axis=1) * sign
    for st in starts:
        c[TM:, st:st + MLA_ROPE] = unit_c
        s[TM:, st:st + MLA_ROPE] = unit_s
    return c, s


@functools.lru_cache(maxsize=None)
def _dft_tables(L):
    f = np.arange(L)[:, None]
    s = np.arange(L)[None, :]
    ang = np.pi * ((f * s) % (2 * L)).astype(np.float64) / L
    cs = np.concatenate([np.cos(ang), np.sin(ang)], axis=0)
    cs[L, :] = np.where(np.arange(L) % 2 == 0, 1.0, -1.0)
    cs = cs.astype(np.float32)
    return cs, np.ascontiguousarray(cs.T)


@functools.lru_cache(maxsize=None)
def _hyena_features(L):
    t = np.arange(L, dtype=np.float64) / L
    bands = np.arange(1, HY_BANDS + 1, dtype=np.float64)
    ang = 2.0 * math.pi * t[:, None] * bands
    feat = np.zeros((L, LANES), np.float32)
    feat[:, 0] = t
    feat[:, 1:1 + HY_BANDS] = np.cos(ang)
    feat[:, 1 + HY_BANDS:HY_EMB] = np.sin(ang)
    return feat


def _adaln_kernel(c_ref, w_ref, b_ref, o_ref):
    w = w_ref[0]
    o_ref[0] = jnp.zeros(o_ref.shape[1:], F32)
    for m in range(c_ref.shape[0]):
        col = jax.nn.silu(c_ref[m])
        o_ref[0, m:m + 1, :] = jnp.sum(w * col, axis=0, keepdims=True) + b_ref[0]


def _adaln(cvecs, ada_w, ada_b):
    depth = ada_w.shape[0]
    n_vec = cvecs.shape[0]
    tn = D
    out = pl.pallas_call(
        _adaln_kernel,
        out_shape=jax.ShapeDtypeStruct((depth, 8, N_MOD * D), F32),
        grid=(depth, N_MOD),
        in_specs=[pl.BlockSpec((n_vec, D, 1), lambda l, j: (0, 0, 0)),
                  pl.BlockSpec((1, D, tn), lambda l, j: (l, 0, j)),
                  pl.BlockSpec((1, 1, tn), lambda l, j: (l, 0, j))],
        out_specs=pl.BlockSpec((1, 8, tn), lambda l, j: (l, 0, j)),
        compiler_params=_params(2),
        name="adaln",
    )(cvecs[:, :, None], ada_w, ada_b[:, None, :])
    return out[:, :n_vec].reshape(depth, n_vec, N_MOD, D)


FF_CHUNK = FF
FF_LOADS = 11
W_IN_PIECE = 2 * FF // FF_LOADS
W_OUT_PIECE = FF // FF_LOADS


def _ffn_kernel(base, final, split_in, layer, which, *refs):
    y, refs = _tok_read(refs, split_in)
    mod_ref, g_ref, win_hbm, wout_hbm, fg_ref = refs[:5]
    n_out = 2 if final else 1
    outs = refs[5:5 + n_out]
    win_ref, wout_ref, stage_in, stage_out, sems = refs[5 + n_out:]

    @pl.when(pl.program_id(0) == 0)
    def _():
        def copies(c, slot):
            return (pltpu.make_async_copy(win_hbm.at[layer, which, :, pl.ds(c * W_IN_PIECE, W_IN_PIECE)],
                                          stage_in.at[slot], sems.at[0, slot]),
                    pltpu.make_async_copy(wout_hbm.at[layer, which, pl.ds(c * W_OUT_PIECE, W_OUT_PIECE), :],
                                          stage_out.at[slot], sems.at[1, slot]))

        for cp in copies(0, 0):
            cp.start()
        for c in range(FF_LOADS):
            slot = c % 2
            if c + 1 < FF_LOADS:
                for cp in copies(c + 1, 1 - slot):
                    cp.start()
            for cp in copies(c, slot):
                cp.wait()
            win_ref[:, c * W_IN_PIECE:(c + 1) * W_IN_PIECE] = stage_in[slot].astype(BF16)
            wout_ref[c * W_OUT_PIECE:(c + 1) * W_OUT_PIECE, :] = stage_out[slot].astype(BF16)

    mod = mod_ref[0]
    h = _modulate(y, g_ref[...], mod[base:base + 1], mod[base + 1:base + 2]).astype(BF16)
    acc = jnp.zeros(y.shape, F32)
    for c in range(FF // FF_CHUNK):
        lo = c * FF_CHUNK
        gate = jnp.dot(h, win_ref[:, lo:lo + FF_CHUNK], preferred_element_type=F32)
        up = jnp.dot(h, win_ref[:, FF + lo:FF + lo + FF_CHUNK], preferred_element_type=F32)
        a = (jax.nn.silu(gate) * up).astype(BF16)
        acc = acc + jnp.dot(a, wout_ref[lo:lo + FF_CHUNK, :], preferred_element_type=F32)
    out = y + 0.5 * mod[base + 2:base + 3] * acc
    if final:
        _tok_write(outs[0], outs[1], _rmsnorm(out, fg_ref[...]))
    else:
        outs[0][...] = out


def _half_ffn(y, mods_l, g, ff_w_in, ff_w_out, layer, which, final_g=None):
    final = final_g is not None
    fg = final_g if final else g
    y_specs, y_args = _tok_specs(y, D)
    if final:
        out_shape, out_specs = _split_out(D)
    else:
        out_shape, out_specs = jax.ShapeDtypeStruct((TOK, D), F32), pl.BlockSpec((TM, D), _row)
    return pl.pallas_call(
        functools.partial(_ffn_kernel, 6 * which, final, isinstance(y, tuple), layer, which),
        out_shape=out_shape,
        grid=(NT,),
        in_specs=y_specs + [pl.BlockSpec((1, N_MOD, D), lambda i: (_mod_index(i), 0, 0)),
                            _const_spec((1, D)),
                            pl.BlockSpec(memory_space=pl.ANY),
                            pl.BlockSpec(memory_space=pl.ANY),
                            _const_spec((1, D))],
        out_specs=out_specs,
        scratch_shapes=[pltpu.VMEM((D, 2 * FF), BF16), pltpu.VMEM((FF, D), BF16),
                        pltpu.VMEM((2, D, W_IN_PIECE), F32), pltpu.VMEM((2, W_OUT_PIECE, D), F32),
                        pltpu.SemaphoreType.DMA((2, 2))],
        compiler_params=_params(1, VMEM_LIMIT),
        name="half_ffn",
    )(*y_args, mods_l, g[None], ff_w_in, ff_w_out, fg[None])


def _linear_kernel(x_ref, w_ref, o_ref):
    o_ref[...] = _dot(x_ref[...], w_ref[...]).astype(o_ref.dtype)


def _linear(x, w, tm, out_dtype):
    m, k = x.shape
    n = w.shape[1]
    return pl.pallas_call(
        _linear_kernel,
        out_shape=jax.ShapeDtypeStruct((m, n), out_dtype),
        grid=(m // tm,),
        in_specs=[pl.BlockSpec((tm, k), lambda i: (i, 0)), _const_spec((k, n))],
        out_specs=pl.BlockSpec((tm, n), lambda i: (i, 0)),
        compiler_params=_params(1),
        name="linear",
    )(x, w.astype(BF16))


MLA_SCALE = (MLA_NOPE + MLA_ROPE) ** -0.5
QW = MLA_HEADS * LANES
KR_AT = MLA_NOPE
IN_A_PAD = MLA_Q_RANK + MLA_KV_RANK + S5_WIDTH + LANES


def _inproj_a_kernel(y_ref, mod_ref, g_ref, win_ref, qn_ref, wuq_ref, kvn_ref, wk_ref, wv_ref,
                     cq_ref, sq_ref, ck_ref, sk_ref,
                     q_ref, ckv_ref, kru_ref, krr_ref, kn_ref, v_ref, u_ref):
    mod = mod_ref[0]
    h = _modulate(y_ref[...], g_ref[...], mod[3:4], mod[4:5]).astype(BF16)
    p = jnp.dot(h, win_ref[...], preferred_element_type=F32)
    o1 = MLA_Q_RANK
    o2 = o1 + MLA_KV_RANK
    o3 = o2 + S5_WIDTH
    q = _dot(_rmsnorm(p[:, :o1], qn_ref[...]), wuq_ref[...])
    q_ref[...] = (_rope(q, cq_ref[...], sq_ref[...]) * MLA_SCALE).astype(BF16)
    ckv = _rmsnorm(p[:, o1:o2], kvn_ref[...])
    ckv_b = ckv.astype(BF16)
    kn_ref[...] = jnp.dot(ckv_b, wk_ref[...], preferred_element_type=F32).astype(BF16)
    v_ref[...] = jnp.dot(ckv_b, wv_ref[...], preferred_element_type=F32).astype(BF16)
    u_ref[...] = p[:, o2:o3]
    krp = p[:, o3:]
    krr_ref[...] = _rope(krp, ck_ref[...], sk_ref[...]).astype(BF16)

    @pl.when(pl.program_id(0) < NT_P)
    def _():
        ckv_ref[...] = ckv
        kru_ref[...] = krp


def _inproj_a(y, mods_l, g, w_in, q_norm, w_uq, kv_norm, w_ukv):
    o1 = MLA_Q_RANK
    o2 = o1 + MLA_KV_RANK
    o3 = o2 + MLA_ROPE
    kr_cols = jnp.pad(w_in[:, o2:o3], ((0, 0), (KR_AT, LANES - KR_AT - MLA_ROPE)))
    w_ext = jnp.concatenate([w_in[:, :o2], w_in[:, o3:], kr_cols], axis=1).astype(BF16)
    dq = MLA_NOPE + MLA_ROPE
    w_uq_pad = jnp.pad(w_uq.reshape(MLA_Q_RANK, MLA_HEADS, dq),
                       ((0, 0), (0, 0), (0, LANES - dq))).reshape(MLA_Q_RANK, QW).astype(BF16)
    w_kv = w_ukv.reshape(MLA_KV_RANK, MLA_HEADS, MLA_NOPE + MLA_V)
    w_k = jnp.pad(w_kv[:, :, :MLA_NOPE], ((0, 0), (0, 0), (0, LANES - MLA_NOPE))).reshape(MLA_KV_RANK, QW)
    w_v = w_kv[:, :, MLA_NOPE:].reshape(MLA_KV_RANK, MLA_HEADS * MLA_V)
    w_k, w_v = w_k.astype(BF16), w_v.astype(BF16)
    cq, sq = _rope_tables(QW, tuple(h * LANES + MLA_NOPE for h in range(MLA_HEADS)))
    ck, sk = _rope_tables(LANES, (KR_AT,))
    row = _row
    pos = lambda i: (_pos_index(i), 0)
    widths = (QW, MLA_KV_RANK, LANES, LANES, QW, MLA_HEADS * MLA_V, S5_WIDTH)
    prompt_only = (1, 2)
    mxu_only = (0, 3, 4, 5)
    outs = pl.pallas_call(
        _inproj_a_kernel,
        out_shape=[jax.ShapeDtypeStruct((TOK_P if k in prompt_only else TOK, w), BF16 if k in mxu_only else F32)
                   for k, w in enumerate(widths)],
        grid=(NT,),
        in_specs=[pl.BlockSpec((TM, D), row),
                  pl.BlockSpec((1, N_MOD, D), lambda i: (_mod_index(i), 0, 0)),
                  _const_spec((1, D)),
                  _const_spec((D, IN_A_PAD)),
                  _const_spec((1, MLA_Q_RANK)),
                  _const_spec((MLA_Q_RANK, QW)),
                  _const_spec((1, MLA_KV_RANK)),
                  _const_spec((MLA_KV_RANK, QW)),
                  _const_spec((MLA_KV_RANK, MLA_HEADS * MLA_V)),
                  pl.BlockSpec((TM, QW), pos), pl.BlockSpec((TM, QW), pos),
                  pl.BlockSpec((TM, LANES), pos), pl.BlockSpec((TM, LANES), pos)],
        out_specs=[pl.BlockSpec((TM, w), _row_p if k in prompt_only else row)
                   for k, w in enumerate(widths)],
        compiler_params=_params(1, VMEM_LIMIT),
        name="inproj_even",
    )(y, mods_l, g[None], w_ext, q_norm[None], w_uq_pad, kv_norm[None], w_k, w_v,
      jnp.asarray(cq), jnp.asarray(sq), jnp.asarray(ck), jnp.asarray(sk))
    q, ckv, kr_unrot, kr_rot, kn, v, u = outs
    return q, ckv, kr_unrot, kr_rot, kn, v, u, (w_k, w_v)


def _mla_attn_kernel(nseg, q_ref, *refs):
    o_ref = refs[-1]
    tq = q_ref.shape[0]
    lane = lax.broadcasted_iota(jnp.int32, (tq, LANES), 1)
    for pair in range(MLA_HEADS // 2):
        outs = []
        for hh in range(2):
            h = 2 * pair + hh
            hs = slice(h * LANES, (h + 1) * LANES)
            qh = q_ref[:, hs]
            scores = []
            for s in range(nseg):
                kn_ref, kr_ref = refs[3 * s], refs[3 * s + 1]
                kh = (kn_ref[:, hs] + kr_ref[...]).astype(BF16)
                scores.append(_dot_nt(qh, kh))
            m = functools.reduce(jnp.maximum, [jnp.max(s, axis=-1, keepdims=True) for s in scores])
            es = [jnp.exp(s - m) for s in scores]
            l = functools.reduce(jnp.add, [jnp.sum(e, axis=-1, keepdims=True) for e in es])
            o = None
            for s in range(nseg):
                v_ref = refs[3 * s + 2]
                part = _dot(es[s], v_ref[:, pair * LANES:(pair + 1) * LANES])
                o = part if o is None else o + part
            outs.append(o / l)
        o_ref[:, pair * LANES:(pair + 1) * LANES] = jnp.where(lane < MLA_V, outs[0], outs[1])


def _mla_attention(q, kn, kr, v, n_batch, seq, tq, row0, ctx=None):
    qt = seq // tq
    qb0, kb0 = row0 // tq, row0 // seq
    in_specs = [pl.BlockSpec((tq, QW), lambda b, j: (qb0 + b * qt + j, 0))]
    args = [q]
    segs = []
    if ctx is not None:
        segs.append((ctx, PAST, 0))
    segs.append(((kn, kr, v), seq, kb0))
    for (a_kn, a_kr, a_v), ln, off in segs:
        idx = lambda b, j, off=off: (off + b, 0)
        in_specs += [pl.BlockSpec((ln, QW), idx), pl.BlockSpec((ln, LANES), idx),
                     pl.BlockSpec((ln, MLA_HEADS * MLA_V), idx)]
        args += [a_kn, a_kr, a_v]
    return pl.pallas_call(
        functools.partial(_mla_attn_kernel, len(segs)),
        out_shape=jax.ShapeDtypeStruct((n_batch * seq, MLA_HEADS * MLA_V), F32),
        grid=(n_batch, qt),
        in_specs=in_specs,
        out_specs=pl.BlockSpec((tq, MLA_HEADS * MLA_V), lambda b, j: (b * qt + j, 0)),
        compiler_params=_params(2, VMEM_LIMIT),
        name="mla_attention",
    )(*args)


def _cpow(ar, ai, e, nbits):
    rr = jnp.ones_like(ar)
    ri = jnp.zeros_like(ar)
    br, bi = ar, ai
    for k in range(nbits):
        bit = ((e >> k) & 1) == 1
        nr = rr * br - ri * bi
        ni = rr * bi + ri * br
        rr = jnp.where(bit, nr, rr)
        ri = jnp.where(bit, ni, ri)
        if k + 1 < nbits:
            br, bi = br * br - bi * bi, 2.0 * br * bi
    return rr, ri


def _s5_abar_kernel(lr_ref, li_ref, ls_ref, o_ref):
    step = jnp.exp(ls_ref[...])
    lr = jnp.minimum(lr_ref[...], -1e-4)
    li = li_ref[...]
    mag = jnp.exp(lr * step)
    ar = mag * jnp.cos(li * step)
    ai = mag * jnp.sin(li * step)
    den = lr * lr + li * li
    o_ref[0] = ar
    o_ref[1] = ai
    o_ref[2] = ((ar - 1.0) * lr + ai * li) / den
    o_ref[3] = (ai * lr - (ar - 1.0) * li) / den


def _s5_prep_kernel(arow_ref, acol_ref, btr_ref, bti_ref, ctr_ref, cti_ref,
                    wi_ref, ws_ref, wo_ref, ap_ref):
    n2 = 2 * S5_N
    lane_o = lax.broadcasted_iota(jnp.int32, (n2, S5_CW), 1)
    blk_o = lane_o >> 4
    row_o = lax.broadcasted_iota(jnp.int32, (n2, S5_CW), 0)
    lane_k = lax.broadcasted_iota(jnp.int32, (S5_GROUP, S5_CW), 1)
    row_k = lax.broadcasted_iota(jnp.int32, (S5_GROUP, S5_CW), 0)
    lane_b = lax.broadcasted_iota(jnp.int32, (S5_GROUP, n2), 1)
    lane_a = lax.broadcasted_iota(jnp.int32, (1, n2), 1)
    rep = ((lane_k & (S5_GROUP - 1)) == row_k).astype(BF16)

    def tile16(x):
        hi = x.astype(BF16)
        r1 = x - hi.astype(F32)
        mid = r1.astype(BF16)
        lo = (r1 - mid.astype(F32)).astype(BF16)
        d = functools.partial(jnp.dot, preferred_element_type=F32)
        return d(hi, rep) + d(mid, rep) + d(lo, rep)

    intra = [None] * S5_T
    for d in range(2):
        ar, ai, fr, fi = (arow_ref[d, 0, k:k + 1, :] for k in range(4))
        btr, bti = btr_ref[d, 0], bti_ref[d, 0]
        bbr = fr * btr - fi * bti
        bbi = fr * bti + fi * btr
        pws = [(jnp.ones_like(ar), jnp.zeros_like(ar))]
        for _ in range(S5_T):
            pr, pi = pws[-1]
            pws.append((pr * ar - pi * ai, pr * ai + pi * ar))
        for s in range(S5_T):
            pr, pi = pws[S5_T - 1 - s] if d == 0 else pws[s]
            ws_ref[d, 0, s * S5_GROUP:(s + 1) * S5_GROUP, :] = jnp.where(
                lane_b < S5_N, pr * bbr - pi * bbi, pr * bbi + pi * bbr).astype(BF16)

        acol = acol_ref[d, 0]
        arc = jnp.broadcast_to(acol[:, 0:1], (n2, S5_CW))
        aic = jnp.broadcast_to(acol[:, 1:2], (n2, S5_CW))
        ctr, cti = tile16(ctr_ref[d, 0]), tile16(cti_ref[d, 0])
        e_lag = blk_o if d == 0 else (S5_T - 1 - blk_o)
        pqr, pqi = _cpow(arc, aic, e_lag, 4)
        qr = pqr * ctr - pqi * cti
        qi = pqr * cti + pqi * ctr
        wo_ref[d, 0] = jnp.where(row_o < S5_N, qr * arc - qi * aic, -(qr * aic + qi * arc)).astype(BF16)
        q_stack = jnp.where(row_o < S5_N, qr, qi)
        bb_mix = jnp.where(lane_b < S5_N, bbr, -bbi)
        kt = _dot3(bb_mix, q_stack)
        for s in range(S5_T):
            if d == 0:
                blk = jnp.where(lane_k >= S5_GROUP * s, pltpu.roll(kt, S5_GROUP * s, 1), 0.0)
            else:
                blk = jnp.where(lane_k < S5_GROUP * (s + 1),
                                pltpu.roll(kt, (S5_GROUP * (s + 1)) % S5_CW, 1), 0.0)
            intra[s] = blk if intra[s] is None else intra[s] + blk

        pr1, pi1 = pws[S5_T]
        for k in range(6):
            ap_ref[d, 0, k:k + 1, :] = pr1
            ap_ref[d, 0, 8 + k:9 + k, :] = jnp.where(lane_a < S5_N, -pi1, pi1)
            pr1, pi1 = pr1 * pr1 - pi1 * pi1, 2.0 * pr1 * pi1
        ap_ref[d, 0, 6:8, :] = jnp.zeros((2, n2), F32)
        ap_ref[d, 0, 14:16, :] = jnp.zeros((2, n2), F32)
    for s in range(S5_T):
        wi_ref[0, s * S5_GROUP:(s + 1) * S5_GROUP, :] = intra[s].astype(BF16)


def _s5_prep(a_re, a_im, log_step, b_re, b_im, c_re, c_im):
    g, n, n2 = S5_GROUPS, S5_N, 2 * S5_N
    abar = pl.pallas_call(
        _s5_abar_kernel,
        out_shape=jax.ShapeDtypeStruct((4, 2 * g, n), F32),
        grid=(1,),
        in_specs=[_const_spec((2 * g, n)), _const_spec((2 * g, n)), _const_spec((2 * g, 1))],
        out_specs=pl.BlockSpec((4, 2 * g, n), lambda i: (0, 0, 0)),
        compiler_params=_params(1),
        name="s5_abar",
    )(a_re.reshape(2 * g, n), a_im.reshape(2 * g, n), log_step.reshape(2 * g, 1))
    abar = jnp.concatenate([abar, abar], axis=-1).reshape(4, 2, g, n2)
    arow = abar.transpose(1, 2, 0, 3)
    acol = abar[:2].transpose(1, 2, 3, 0)
    bt = lambda b: jnp.concatenate([jnp.swapaxes(b, 2, 3)] * 2, axis=-1)
    ct = lambda c: jnp.concatenate([jnp.swapaxes(c, 2, 3)] * 2, axis=2)
    spec4 = lambda r, c: pl.BlockSpec((2, 1, r, c), lambda i: (0, i, 0, 0))
    return pl.pallas_call(
        _s5_prep_kernel,
        out_shape=[jax.ShapeDtypeStruct((g, S5_CW, S5_CW), BF16),
                   jax.ShapeDtypeStruct((2, g, S5_CW, n2), BF16),
                   jax.ShapeDtypeStruct((2, g, n2, S5_CW), BF16),
                   jax.ShapeDtypeStruct((2, g, 16, n2), F32)],
        grid=(g,),
        in_specs=[spec4(4, n2), spec4(n2, 2),
                  spec4(S5_GROUP, n2), spec4(S5_GROUP, n2), spec4(n2, S5_GROUP), spec4(n2, S5_GROUP)],
        out_specs=[pl.BlockSpec((1, S5_CW, S5_CW), lambda i: (i, 0, 0)),
                   spec4(S5_CW, n2), spec4(n2, S5_CW), spec4(16, n2)],
        compiler_params=_params(1),
        name="s5_prep",
    )(arow, acol, bt(b_re), bt(b_im), ct(c_re), ct(c_im))


def _cmul_rows(x, p1, p2):
    return x * p1 + pltpu.roll(x, S5_N, 1) * p2


S5_OCT = LANES // S5_GROUP
S5_RB = 48


def _s5_core_kernel(u_ref, wi_ref, ws_ref, wo_ref, ap_ref, h0_ref, d_ref, y_ref, fin_ref,
                    ug_ref, yg_ref, z_ref, sp_ref):
    n2 = 2 * S5_N
    blk = lax.broadcasted_iota(jnp.int32, (S5_RB, LANES), 1) >> 4

    def tok_rows(r0, t):
        return pl.ds(r0 * S5_T + t, S5_RB, stride=S5_T)

    def block_transpose(xs):
        for b in range(3):
            s = 1 << b
            odd = ((blk >> b) & 1) == 1
            new = list(xs)
            for i in range(S5_OCT):
                if not i & s:
                    new[i] = jnp.where(odd, pltpu.roll(xs[i + s], s * S5_GROUP, 1), xs[i])
                    new[i + s] = jnp.where(odd, xs[i + s], pltpu.roll(xs[i], LANES - s * S5_GROUP, 1))
            xs = new
        return xs

    def gather(rb, carry):
        r0 = pl.multiple_of(rb * S5_RB, S5_RB)
        for half in range(2):
            xs = [u_ref[tok_rows(r0, S5_OCT * half + tt), :] for tt in range(S5_OCT)]
            for gl, x in enumerate(block_transpose(xs)):
                ug_ref[gl, pl.ds(r0, S5_RB), half * LANES:(half + 1) * LANES] = x
        return carry

    lax.fori_loop(0, S5_ROWS // S5_RB, gather, 0)

    r = lax.broadcasted_iota(jnp.int32, (S5_ROWS, n2), 0)
    in_p = r < S5_ROWS_P
    rib = jnp.where(in_p, r & (CH_P - 1), (r - S5_ROWS_P) & (CH_S - 1))
    nch = jnp.where(in_p, CH_P, CH_S)

    def group(gl, carry):
        ub = ug_ref[gl].astype(BF16)
        y = jnp.dot(ub, wi_ref[gl], preferred_element_type=F32)
        for d in range(2):
            z_ref[...] = jnp.dot(ub, ws_ref[d, gl], preferred_element_type=F32)
            p1, p2 = ap_ref[d, gl, 0:1, :], ap_ref[d, gl, 8:9, :]
            edge = [S5_ROWS_P + CH_S * b + (0 if d == 0 else CH_S - 1) for b in range(NB_S)]
            for b in range(NB_S):
                h0 = h0_ref[gl, d, b:b + 1, :]
                z_ref[edge[b]:edge[b] + 1, :] = z_ref[edge[b]:edge[b] + 1, :] + _cmul_rows(h0, p1, p2)
            s = z_ref[...]
            for k in range(6):
                sh = 1 << k
                if d == 0:
                    t = jnp.where(rib >= sh, pltpu.roll(s, sh, 0), 0.0)
                else:
                    t = jnp.where(rib < nch - sh, pltpu.roll(s, S5_ROWS - sh, 0), 0.0)
                s = s + _cmul_rows(t, ap_ref[d, gl, k:k + 1, :], ap_ref[d, gl, 8 + k:9 + k, :])
            z_ref[...] = s
            first = CH_P - 1 if d == 0 else 0
            fin_ref[gl, d] = z_ref[pl.ds(first, NB_P, stride=CH_P), :]
            if d == 0:
                sp_ref[...] = jnp.where(rib >= 1, pltpu.roll(s, 1, 0), 0.0)
            else:
                sp_ref[...] = jnp.where(rib < nch - 1, pltpu.roll(s, S5_ROWS - 1, 0), 0.0)
            for b in range(NB_S):
                sp_ref[edge[b]:edge[b] + 1, :] = h0_ref[gl, d, b:b + 1, :]
            y = y + jnp.dot(sp_ref[...].astype(BF16), wo_ref[d, gl], preferred_element_type=F32)
        yg_ref[gl] = y
        return carry

    lax.fori_loop(0, S5_OCT, group, 0)

    def scatter(rb, carry):
        r0 = pl.multiple_of(rb * S5_RB, S5_RB)
        for half in range(2):
            ys = [yg_ref[gl, pl.ds(r0, S5_RB), half * LANES:(half + 1) * LANES] for gl in range(S5_OCT)]
            for tt, acc in enumerate(block_transpose(ys)):
                rows = tok_rows(r0, S5_OCT * half + tt)
                y_ref[rows, :] = jax.nn.gelu(acc + d_ref[...] * u_ref[rows, :])
        return carry

    lax.fori_loop(0, S5_ROWS // S5_RB, scatter, 0)


def _s5_core(u, prep, h0, d_skip):
    w_intra, w_state, w_out, apow = prep
    g, n2 = S5_GROUPS, 2 * S5_N
    spec4 = lambda r, c: pl.BlockSpec((2, S5_OCT, r, c), lambda i: (0, i, 0, 0))
    slab = pl.BlockSpec((TOK, LANES), lambda i: (0, i))
    return pl.pallas_call(
        _s5_core_kernel,
        out_shape=[jax.ShapeDtypeStruct((TOK, S5_WIDTH), F32),
                   jax.ShapeDtypeStruct((g, 2, NB_P, n2), F32)],
        grid=(g // S5_OCT,),
        in_specs=[slab,
                  pl.BlockSpec((S5_OCT, S5_CW, S5_CW), lambda i: (i, 0, 0)),
                  spec4(S5_CW, n2), spec4(n2, S5_CW), spec4(16, n2),
                  pl.BlockSpec((S5_OCT, 2, 8, n2), lambda i: (i, 0, 0, 0)),
                  pl.BlockSpec((1, LANES), lambda i: (0, i))],
        out_specs=[slab, pl.BlockSpec((S5_OCT, 2, NB_P, n2), lambda i: (i, 0, 0, 0))],
        scratch_shapes=[pltpu.VMEM((S5_OCT, S5_ROWS, S5_CW), F32), pltpu.VMEM((S5_OCT, S5_ROWS, S5_CW), F32),
                        pltpu.VMEM((S5_ROWS, n2), F32), pltpu.VMEM((S5_ROWS, n2), F32)],
        compiler_params=_params(1, VMEM_LIMIT),
        name="s5_scan",
    )(u, w_intra, w_state, w_out, apow, h0, d_skip[None])


def _outproj_kernel(glu, split1, split2, y_ref, mod_ref, *refs):
    a1, refs = _tok_read(refs, split1)
    a2, refs = _tok_read(refs, split2)
    w1_ref, w2_ref, wg_ref, o_ref = refs
    if glu:
        a2 = a2 * jax.nn.sigmoid(_dot(a2, wg_ref[...]))
    out = _dot(a1, w1_ref[...]) + _dot(a2, w2_ref[...])
    o_ref[...] = y_ref[...] + mod_ref[0][5:6] * out


def _outproj(y, mods_l, a1, a2, w_out, w_glu=None):
    glu = w_glu is not None
    k1 = k2 = w_out.shape[0] // 2
    wg = (w_glu if glu else jnp.zeros((8, LANES), F32)).astype(BF16)
    s1, a1_args = _tok_specs(a1, k1)
    s2, a2_args = _tok_specs(a2, k2)
    return pl.pallas_call(
        functools.partial(_outproj_kernel, glu, isinstance(a1, tuple), isinstance(a2, tuple)),
        out_shape=jax.ShapeDtypeStruct((TOK, D), F32),
        grid=(NT,),
        in_specs=[pl.BlockSpec((TM, D), _row),
                  pl.BlockSpec((1, N_MOD, D), lambda i: (_mod_index(i), 0, 0))] + s1 + s2
                 + [_const_spec((k1, D)), _const_spec((k2, D)), _const_spec(wg.shape)],
        out_specs=pl.BlockSpec((TM, D), _row),
        compiler_params=_params(1, VMEM_LIMIT),
        name="outproj",
    )(y, mods_l, *a1_args, *a2_args, w_out[:k1].astype(BF16), w_out[k1:].astype(BF16), wg)


DF_SCALE = DF_DH ** -0.5
DFW = DF_HEADS * 2 * DF_DH
IN_B = 3 * HY_WIDTH + 2 * DFW + DF_HEADS * DF_V


def _inproj_b_kernel(y_ref, mod_ref, g_ref, win_ref, c_ref, s_ref,
                     hy_ref, q_ref, kp_ref, ks_ref, vp_ref, vs_ref):
    mod = mod_ref[0]
    h = _modulate(y_ref[...], g_ref[...], mod[3:4], mod[4:5]).astype(BF16)
    p = jnp.dot(h, win_ref[...], preferred_element_type=F32)
    o1 = 3 * HY_WIDTH
    hy_ref[...] = p[:, :o1]
    q_ref[...] = (_rope(p[:, o1:o1 + DFW], c_ref[...], s_ref[...]) * DF_SCALE).astype(BF16)
    _tok_write(kp_ref, ks_ref, _rope(p[:, o1 + DFW:o1 + 2 * DFW], c_ref[...], s_ref[...]))
    _tok_write(vp_ref, vs_ref, p[:, o1 + 2 * DFW:])


def _inproj_b(y, mods_l, g, w_in):
    cs, sn = _rope_tables(DFW, tuple(range(0, DFW, DF_DH)))
    pos = lambda i: (_pos_index(i), 0)
    k_shapes, k_specs = _split_out(DFW, BF16)
    v_shapes, v_specs = _split_out(DF_HEADS * DF_V, BF16)
    hy_u, q, kp, ks, vp, vs = pl.pallas_call(
        _inproj_b_kernel,
        out_shape=[jax.ShapeDtypeStruct((TOK, 3 * HY_WIDTH), F32), jax.ShapeDtypeStruct((TOK, DFW), BF16)]
                  + k_shapes + v_shapes,
        grid=(NT,),
        in_specs=[pl.BlockSpec((TM, D), _row),
                  pl.BlockSpec((1, N_MOD, D), lambda i: (_mod_index(i), 0, 0)),
                  _const_spec((1, D)), _const_spec((D, IN_B)),
                  pl.BlockSpec((TM, DFW), pos), pl.BlockSpec((TM, DFW), pos)],
        out_specs=[pl.BlockSpec((TM, 3 * HY_WIDTH), _row), pl.BlockSpec((TM, DFW), _row)] + k_specs + v_specs,
        compiler_params=_params(1, VMEM_LIMIT),
        name="inproj_odd",
    )(y, mods_l, g[None], w_in.astype(BF16), jnp.asarray(cs), jnp.asarray(sn))
    return hy_u, q, (kp, ks), (vp, vs)


def _diff_attn_kernel(nseg, lam_init, q_ref, lam_ref, sub_ref, *refs):
    o_ref = refs[-1]
    tq = q_ref.shape[0]
    lp = lam_ref[...]
    lam = (jnp.exp(jnp.sum(lp[0:1] * lp[1:2], axis=-1, keepdims=True))
           - jnp.exp(jnp.sum(lp[2:3] * lp[3:4], axis=-1, keepdims=True)) + lam_init)
    lane = lax.broadcasted_iota(jnp.int32, (tq, LANES), 1)
    for pair in range(DF_HEADS // 2):
        cs = slice(pair * LANES, (pair + 1) * LANES)
        q = q_ref[:, cs]
        ks = [refs[2 * s][:, cs].astype(BF16) for s in range(nseg)]
        vs = [refs[2 * s + 1][:, cs].astype(BF16) for s in range(nseg)]
        outs = []
        for hh in range(2):
            parts = []
            for half in range(2):
                unit = 2 * hh + half
                qm = jnp.where((lane >> 5) == unit, q, jnp.zeros_like(q))
                scores = [_dot_nt(qm, k) for k in ks]
                m = functools.reduce(jnp.maximum, [jnp.max(s, axis=-1, keepdims=True) for s in scores])
                es = [jnp.exp(s - m) for s in scores]
                l = functools.reduce(jnp.add, [jnp.sum(e, axis=-1, keepdims=True) for e in es])
                pv = functools.reduce(jnp.add, [_dot(e, v) for e, v in zip(es, vs)])
                parts.append(pv * (1.0 / l))
            o = parts[0] - lam * parts[1]
            mine = (lane >> 6) == hh
            ms = jnp.sum(jnp.where(mine, o * o, 0.0), axis=-1, keepdims=True) * (1.0 / DF_V)
            outs.append(o * lax.rsqrt(ms + EPS))
        o_ref[:, cs] = jnp.where(lane < DF_V, outs[0], outs[1]) * sub_ref[...] * (1.0 - lam_init)


def _diff_attention(q, k, v, lam_p, subln, lam_init, n_batch, seq, tq, row0, ctx=None):
    qt = seq // tq
    qb0, kb0 = row0 // tq, 0
    in_specs = [pl.BlockSpec((tq, DFW), lambda b, j: (qb0 + b * qt + j, 0)),
                pl.BlockSpec((4, DF_DH), lambda b, j: (0, 0)),
                pl.BlockSpec((1, LANES), lambda b, j: (0, 0))]
    args = [q, lam_p, jnp.concatenate([subln, subln])[None]]
    segs = []
    if ctx is not None:
        segs.append((ctx, PAST, 0))
    segs.append(((k, v), seq, kb0))
    for (a_k, a_v), ln, off in segs:
        idx = lambda b, j, off=off: (off + b, 0)
        in_specs += [pl.BlockSpec((ln, DFW), idx), pl.BlockSpec((ln, DF_HEADS * DF_V), idx)]
        args += [a_k, a_v]
    return pl.pallas_call(
        functools.partial(_diff_attn_kernel, len(segs), lam_init),
        out_shape=jax.ShapeDtypeStruct((n_batch * seq, DF_HEADS * DF_V), F32),
        grid=(n_batch, qt),
        in_specs=in_specs,
        out_specs=pl.BlockSpec((tq, DF_HEADS * DF_V), lambda b, j: (b * qt + j, 0)),
        compiler_params=_params(2, VMEM_LIMIT),
        name="diff_attention",
    )(*args)


def _hy_filter_kernel(feat_ref, w1_ref, b1_ref, w2_ref, b2_ref, fq_ref, w3_ref, dec_ref, o_ref):
    feat = feat_ref[...]
    fq = fq_ref[...]
    h = jnp.sin(fq * (_dot3(feat, w1_ref[...]) + b1_ref[...]))
    h = jnp.sin(fq * (_dot3(h, w2_ref[...]) + b2_ref[...]))
    window = jnp.exp(-feat[:, 0:1] * jnp.abs(dec_ref[...]))
    for j in range(4):
        cs = slice(j * HY_WIDTH, (j + 1) * HY_WIDTH)
        o_ref[:, cs] = _dot3(h, w3_ref[:, cs]) * window


def _hy_spectrum_kernel(L, cs_ref, hf_ref, hb_ref, o_ref):
    row = lax.broadcasted_iota(jnp.int32, (L, HY_WIDTH), 0)
    first = row == 0
    tf = _dot(cs_ref[...], hf_ref[...])
    tb = _dot(cs_ref[...], jnp.where(first, 0.0, hb_ref[...]))
    ka = tf[:L] + tb[:L]
    kb = jnp.where(first, tf[L:] + tb[L:], tf[L:] - tb[L:])
    wv = jnp.where(first, 1.0 / (2 * L), 2.0 / (2 * L))
    o_ref[0, 0] = ka * wv
    o_ref[0, 1] = jnp.where(first, 0.0, kb) * wv
    o_ref[0, 2] = jnp.where(first, kb, ka) * wv


def _hy_conv_kernel(L, cs_ref, ct_ref, kf_ref, v_ref, x1_ref, x2_ref,
                    wv_ref, w1_ref, w2_ref, bias_ref, o_ref):
    row = lax.broadcasted_iota(jnp.int32, v_ref.shape, 0)

    def short(x_ref, w_ref):
        x = x_ref[...]
        prev = jnp.where(row >= 1, pltpu.roll(x, 1, 0), 0.0)
        nxt = jnp.where(row <= L - 2, pltpu.roll(x, L - 1, 0), 0.0)
        return w_ref[0:1] * prev + w_ref[1:2] * x + w_ref[2:3] * nxt

    z = short(v_ref, wv_ref)
    gates = (short(x1_ref, w1_ref), short(x2_ref, w2_ref))
    for n in range(2):
        ab = _dot(cs_ref[...], z)
        a, b = ab[:L], ab[L:]
        ka, kb1, ka2 = kf_ref[n, 0], kf_ref[n, 1], kf_ref[n, 2]
        pq = jnp.concatenate([a * ka - b * kb1, a * kb1 + b * ka2], axis=0)
        conv = _dot(ct_ref[...], pq)
        z = gates[n] * (conv + bias_ref[n:n + 1] * z)
    o_ref[...] = z


def _hyena_spectrum(L, phy):
    conv_w, w1, b1, w2, b2, freq, w3, decay, bias = phy
    feat = jnp.asarray(_hyena_features(L))
    w1p = jnp.pad(w1, ((0, LANES - HY_EMB), (0, 0)))
    filt = pl.pallas_call(
        _hy_filter_kernel,
        out_shape=jax.ShapeDtypeStruct((L, 4 * HY_WIDTH), F32),
        grid=(1,),
        in_specs=[_const_spec((L, LANES)), _const_spec((LANES, HY_FH)), _const_spec((1, HY_FH)),
                  _const_spec((HY_FH, HY_FH)), _const_spec((1, HY_FH)), _const_spec((1, HY_FH)),
                  _const_spec((HY_FH, 4 * HY_WIDTH)), _const_spec((1, HY_WIDTH))],
        out_specs=pl.BlockSpec((L, 4 * HY_WIDTH), lambda i: (0, 0)),
        compiler_params=_params(1, VMEM_LIMIT),
        name="hyena_filter",
    )(feat, w1p, b1[None], w2, b2[None], freq[None], w3, decay[None])
    cs = jnp.asarray(_dft_tables(L)[0]).astype(BF16)
    return pl.pallas_call(
        functools.partial(_hy_spectrum_kernel, L),
        out_shape=jax.ShapeDtypeStruct((2, 3, L, HY_WIDTH), F32),
        grid=(2,),
        in_specs=[_const_spec((2 * L, L)),
                  pl.BlockSpec((L, HY_WIDTH), lambda n: (0, n)),
                  pl.BlockSpec((L, HY_WIDTH), lambda n: (0, 2 + n))],
        out_specs=pl.BlockSpec((1, 3, L, HY_WIDTH), lambda n: (n, 0, 0, 0)),
        compiler_params=_params(1, VMEM_LIMIT),
        name="hyena_spectrum",
    )(cs, filt, filt)


def _hyena_conv(hy_u, spec, phy, n_batch, L, cb, row0):
    conv_w, bias = phy[0], phy[8]
    cs, ct = (jnp.asarray(t).astype(BF16) for t in _dft_tables(L))
    nc = HY_WIDTH // cb
    rb0 = row0 // L
    col = lambda off: (lambda b, c: (0, off * nc + c))
    tok = lambda off: (lambda b, c: (rb0 + b, off * nc + c))
    return pl.pallas_call(
        functools.partial(_hy_conv_kernel, L),
        out_shape=jax.ShapeDtypeStruct((n_batch * L, HY_WIDTH), F32),
        grid=(n_batch, nc),
        in_specs=[_const_spec((2 * L, L)), _const_spec((L, 2 * L)),
                  pl.BlockSpec((2, 3, L, cb), lambda b, c: (0, 0, 0, c)),
                  pl.BlockSpec((L, cb), tok(0)), pl.BlockSpec((L, cb), tok(1)), pl.BlockSpec((L, cb), tok(2)),
                  pl.BlockSpec((3, cb), col(0)), pl.BlockSpec((3, cb), col(1)), pl.BlockSpec((3, cb), col(2)),
                  pl.BlockSpec((2, cb), col(0))],
        out_specs=pl.BlockSpec((L, cb), lambda b, c: (b, c)),
        compiler_params=_params(2, VMEM_LIMIT),
        name="hyena_conv",
    )(cs, ct, spec, hy_u, hy_u, hy_u, conv_w, conv_w, conv_w, bias)


def _even_mixer(y, mods_l, g, pa, ps5, ctx_ckv, ctx_krope, ctx_state):
    w_in, w_out, q_norm, w_uq, kv_norm, w_ukv = pa
    a_re, a_im, log_step, b_re, b_im, c_re, c_im, d_skip, w_glu = ps5
    q, ckv, kr_unrot, kr_rot, kn, v, u, (w_k, w_v) = _inproj_a(y, mods_l, g, w_in, q_norm, w_uq, kv_norm, w_ukv)

    ctx_flat = ctx_ckv.reshape(NB_S * PAST, MLA_KV_RANK)
    ctx_kn = _linear(ctx_flat, w_k, PAST, BF16)
    ctx_v = _linear(ctx_flat, w_v, PAST, BF16)
    ctx_kr = jnp.pad(ctx_krope.reshape(NB_S * PAST, MLA_ROPE),
                     ((0, 0), (KR_AT, LANES - KR_AT - MLA_ROPE))).astype(BF16)
    att_p = _mla_attention(q, kn, kr_rot, v, NB_P, L_P, L_P, 0)
    att_s = _mla_attention(q, kn, kr_rot, v, NB_S, L_S, TM, TOK_P, ctx=(ctx_kn, ctx_kr, ctx_v))

    prep = _s5_prep(a_re, a_im, log_step, b_re, b_im, c_re, c_im)
    h0 = ctx_state.transpose(3, 1, 0, 2, 4).reshape(S5_GROUPS, 2, NB_S, 2 * S5_N)
    h0 = jnp.pad(h0, ((0, 0), (0, 0), (0, 8 - NB_S), (0, 0)))
    s5y, fin = _s5_core(u, prep, h0, d_skip)

    y = _outproj(y, mods_l, (att_p, att_s), s5y, w_out, w_glu)
    new_ckv = ckv.reshape(NB_P, L_P, MLA_KV_RANK)
    new_krope = kr_unrot[:, KR_AT:KR_AT + MLA_ROPE].reshape(NB_P, L_P, MLA_ROPE)
    new_state = fin.reshape(S5_GROUPS, 2, NB_P, 2, S5_N).transpose(2, 1, 3, 0, 4)
    return y, new_ckv, new_krope, new_state


def _odd_mixer(y, mods_l, g, pb, phy, ctx_k, ctx_v, lam_init):
    w_in, w_out, lam_p, subln = pb
    hy_u, q, (k_p, k_s), (v_p, v_s) = _inproj_b(y, mods_l, g, w_in)
    hy_p = _hyena_conv(hy_u, _hyena_spectrum(L_P, phy), phy, NB_P, L_P, HY_WIDTH, 0)
    hy_s = _hyena_conv(hy_u, _hyena_spectrum(L_S, phy), phy, NB_S, L_S, HY_WIDTH // 2, TOK_P)
    ctx = (ctx_k.reshape(NB_S * PAST, DFW), ctx_v.reshape(NB_S * PAST, DF_HEADS * DF_V))
    att_p = _diff_attention(q, k_p, v_p, lam_p, subln, lam_init, NB_P, L_P, L_P, 0)
    att_s = _diff_attention(q, k_s, v_s, lam_p, subln, lam_init, NB_S, L_S, TM // 2, TOK_P, ctx=ctx)
    y = _outproj(y, mods_l, (hy_p, hy_s), (att_p, att_s), w_out)
    new_k = k_p.reshape(NB_P, L_P, DF_HEADS, 2, DF_DH)
    new_v = v_p.reshape(NB_P, L_P, DF_HEADS, DF_V)
    return y, new_k, new_v


def kernel(x_prompt, x_sample, c, c_ctx, cache_mla_ckv, cache_mla_krope, state_s5, cache_diff_k, cache_diff_v, ada_w, ada_b, norm_g, ff_w_in, ff_w_out, w_in_a, w_out_a, mla_q_norm, mla_w_uq, mla_kv_norm, mla_w_ukv, s5_a_re, s5_a_im, s5_log_step, s5_b_re, s5_b_im, s5_c_re, s5_c_im, s5_d, s5_w_glu, w_in_b, w_out_b, hy_conv, hy_w1, hy_b1, hy_w2, hy_b2, hy_freq, hy_w3, hy_decay, hy_bias, df_lambda, df_subln, final_norm):
    depth = ada_w.shape[0]
    y = (x_prompt.reshape(TOK_P, D), x_sample.reshape(TOK_S, D))
    mods = _adaln(jnp.concatenate([c_ctx[None], c], axis=0), ada_w, ada_b)
    new_ckv, new_krope, new_s5, new_dk, new_dv = [], [], [], [], []
    for l in range(depth):
        y = _half_ffn(y, mods[l], norm_g[l, 0], ff_w_in, ff_w_out, l, 0)
        if l % 2 == 0:
            e = l // 2
            pa = (w_in_a[e], w_out_a[e], mla_q_norm[e], mla_w_uq[e], mla_kv_norm[e], mla_w_ukv[e])
            ps5 = (s5_a_re[e], s5_a_im[e], s5_log_step[e], s5_b_re[e], s5_b_im[e],
                   s5_c_re[e], s5_c_im[e], s5_d[e], s5_w_glu[e])
            y, ckv, krope, st = _even_mixer(y, mods[l], norm_g[l, 1], pa, ps5, cache_mla_ckv[:, e],
                                            cache_mla_krope[:, e], state_s5[:, e])
            new_ckv.append(ckv)
            new_krope.append(krope)
            new_s5.append(st)
        else:
            o = l // 2
            lam_init = 0.8 - 0.6 * math.exp(-0.3 * l)
            pb = (w_in_b[o], w_out_b[o], df_lambda[o], df_subln[o])
            phy = (hy_conv[o], hy_w1[o], hy_b1[o], hy_w2[o], hy_b2[o], hy_freq[o],
                   hy_w3[o], hy_decay[o], hy_bias[o])
            y, dk, dv = _odd_mixer(y, mods[l], norm_g[l, 1], pb, phy, cache_diff_k[:, o],
                                   cache_diff_v[:, o], lam_init)
            new_dk.append(dk)
            new_dv.append(dv)
        last = l == depth - 1
        y = _half_ffn(y, mods[l], norm_g[l, 2], ff_w_in, ff_w_out, l, 1,
                      final_g=final_norm if last else None)
    y_prompt = y[0].reshape(NB_P, L_P, D)
    y_sample = y[1].reshape(NB_S, L_S, D)
    return (y_prompt, y_sample, jnp.stack(new_ckv, axis=1), jnp.stack(new_krope, axis=1),
            jnp.stack(new_s5, axis=1), jnp.stack(new_dk, axis=1), jnp.stack(new_dv, axis=1))
```

```python
import functools
import math

import numpy as np
import jax
import jax.numpy as jnp
from jax import lax
from jax.experimental import pallas as pl
from jax.experimental.pallas import tpu as pltpu

F32 = jnp.float32
BF16 = jnp.bfloat16

D = 1024
NB_P, L_P = 16, 256
NB_S, L_S = 2, 1024
PAST = 256
GRID_W = 64
N_MOD = 9
FF = 2816
EPS = 1e-6
ROPE_BASE = 10000.0

MLA_HEADS, MLA_NOPE, MLA_ROPE, MLA_V = 8, 64, 32, 64
MLA_Q_RANK, MLA_KV_RANK = 384, 256
S5_WIDTH, S5_GROUP, S5_N = 512, 16, 64
S5_GROUPS = S5_WIDTH // S5_GROUP
HY_WIDTH, HY_BANDS, HY_FH = 512, 16, 64
HY_EMB = 2 * HY_BANDS + 1
DF_HEADS, DF_DH = 8, 32
DF_V = 2 * DF_DH

TOK_P = NB_P * L_P
TOK_S = NB_S * L_S
TOK = TOK_P + TOK_S
TM = 512
NT = TOK // TM
NT_P = TOK_P // TM
TILES_PER_SAMPLE = L_S // TM

LANES = 128
S5_T = 16
S5_CW = S5_T * S5_GROUP
CH_P = L_P // S5_T
CH_S = L_S // S5_T
S5_ROWS = NB_P * CH_P + NB_S * CH_S
S5_ROWS_P = NB_P * CH_P

VMEM_LIMIT = 56 * 1024 * 1024


def _params(n_grid, vmem=None):
    return pltpu.CompilerParams(dimension_semantics=("arbitrary",) * n_grid,
                                vmem_limit_bytes=vmem)


def _const_spec(shape):
    nd = len(shape)
    return pl.BlockSpec(shape, lambda *_: (0,) * nd, pipeline_mode=pl.Buffered(1))


def _mod_index(i):
    return jnp.where(i < NT_P, 0, 1 + (i - NT_P) // TILES_PER_SAMPLE)


def _pos_index(i):
    return jnp.where(i < NT_P, 0, 1 + (i - NT_P) % TILES_PER_SAMPLE)


def _row(i):
    return (i, 0)


def _row_p(i):
    return (jnp.minimum(i, NT_P - 1), 0)


def _row_s(i):
    return (jnp.maximum(i - NT_P, 0), 0)


def _tok_specs(x, width):
    if isinstance(x, tuple):
        return [pl.BlockSpec((TM, width), _row_p), pl.BlockSpec((TM, width), _row_s)], list(x)
    return [pl.BlockSpec((TM, width), _row)], [x]


def _tok_read(refs, split):
    if split:
        return jnp.where(pl.program_id(0) < NT_P, refs[0][...], refs[1][...]), refs[2:]
    return refs[0][...], refs[1:]


def _tok_write(p_ref, s_ref, value):
    i = pl.program_id(0)

    @pl.when(i < NT_P)
    def _():
        p_ref[...] = value

    @pl.when(i >= NT_P)
    def _():
        s_ref[...] = value.astype(s_ref.dtype)


def _split_out(width, sample_dtype=F32):
    shapes = [jax.ShapeDtypeStruct((TOK_P, width), F32), jax.ShapeDtypeStruct((TOK_S, width), sample_dtype)]
    specs = [pl.BlockSpec((TM, width), _row_p), pl.BlockSpec((TM, width), _row_s)]
    return shapes, specs


def _dot(a, b):
    return jnp.dot(a.astype(BF16), b.astype(BF16), preferred_element_type=F32)


def _dot_nt(a, b):
    return lax.dot_general(a, b, (((1,), (1,)), ((), ())), preferred_element_type=F32)


def _split(x):
    hi = x.astype(BF16)
    lo = (x - hi.astype(F32)).astype(BF16)
    return hi, lo


def _dot3(a, b):
    ah, al = _split(a)
    bh, bl = _split(b)
    d = functools.partial(jnp.dot, preferred_element_type=F32)
    return d(ah, bh) + d(ah, bl) + d(al, bh)


def _rmsnorm(x, g):
    return x * lax.rsqrt(jnp.mean(x * x, axis=-1, keepdims=True) + EPS) * g


def _modulate(y, g, shift, scale):
    return _rmsnorm(y, g) * (1.0 + scale) + shift


def _pair_swap(x):
    n = x.shape[-1]
    lane = lax.broadcasted_iota(jnp.int32, x.shape, x.ndim - 1)
    return jnp.where((lane & 1) == 0, pltpu.roll(x, n - 1, x.ndim - 1), pltpu.roll(x, 1, x.ndim - 1))


def _rope(x, cos, sin_signed):
    return x * cos + _pair_swap(x) * sin_signed


def _rope_angles():
    n_freq = MLA_ROPE // 4
    inv = 1.0 / (ROPE_BASE ** (np.arange(n_freq, dtype=np.float64) / n_freq))
    pos = np.arange(L_S)
    row = (pos // GRID_W).astype(np.float64)
    col = (pos % GRID_W).astype(np.float64)
    ang = np.concatenate([row[:, None] * inv, col[:, None] * inv], axis=-1)
    return np.cos(ang), np.sin(ang)


@functools.lru_cache(maxsize=None)
def _rope_tables(width, starts):
    cos, sin = _rope_angles()
    c = np.ones((TM + L_S, width), np.float32)
    s = np.zeros((TM + L_S, width), np.float32)
    sign = np.where(np.arange(MLA_ROPE) % 2 == 0, -1.0, 1.0)
    unit_c = np.repeat(cos, 2, axis=1)
    unit_s = np.repeat(sin, 2, axis=1) * sign
    for st in starts:
        c[TM:, st:st + MLA_ROPE] = unit_c
        s[TM:, st:st + MLA_ROPE] = unit_s
    return c, s


@functools.lru_cache(maxsize=None)
def _dft_tables(L):
    f = np.arange(L)[:, None]
    s = np.arange(L)[None, :]
    ang = np.pi * ((f * s) % (2 * L)).astype(np.float64) / L
    cs = np.concatenate([np.cos(ang), np.sin(ang)], axis=0)
    cs[L, :] = np.where(np.arange(L) % 2 == 0, 1.0, -1.0)
    cs = cs.astype(np.float32)
    return cs, np.ascontiguousarray(cs.T)


@functools.lru_cache(maxsize=None)
def _hyena_features(L):
    t = np.arange(L, dtype=np.float64) / L
    bands = np.arange(1, HY_BANDS + 1, dtype=np.float64)
    ang = 2.0 * math.pi * t[:, None] * bands
    feat = np.zeros((L, LANES), np.float32)
    feat[:, 0] = t
    feat[:, 1:1 + HY_BANDS] = np.cos(ang)
    feat[:, 1 + HY_BANDS:HY_EMB] = np.sin(ang)
    return feat


def _adaln_kernel(c_ref, w_ref, b_ref, o_ref, s_ref):
    n_vec = c_ref.shape[0]
    tn = w_ref.shape[2]
    s_ref[...] = jax.nn.silu(c_ref[...])

    def body(k, accs):
        rows = pl.ds(pl.multiple_of(k * 8, 8), 8)
        w = w_ref[0, rows, :]
        return tuple(acc + w * s_ref[m, rows, :] for m, acc in enumerate(accs))

    accs = lax.fori_loop(0, D // 8, body, tuple(jnp.zeros((8, tn), F32) for _ in range(n_vec)))
    o_ref[0] = jnp.zeros(o_ref.shape[1:], F32)
    for m in range(n_vec):
        o_ref[0, m:m + 1, :] = jnp.sum(accs[m], axis=0, keepdims=True) + b_ref[0]


def _adaln(cvecs, ada_w, ada_b):
    depth = ada_w.shape[0]
    n_vec = cvecs.shape[0]
    tn = N_MOD * D // 4
    out = pl.pallas_call(
        _adaln_kernel,
        out_shape=jax.ShapeDtypeStruct((depth, 8, N_MOD * D), F32),
        grid=(depth, N_MOD * D // tn),
        in_specs=[pl.BlockSpec((n_vec, D, 1), lambda l, j: (0, 0, 0)),
                  pl.BlockSpec((1, D, tn), lambda l, j: (l, 0, j)),
                  pl.BlockSpec((1, 1, tn), lambda l, j: (l, 0, j))],
        out_specs=pl.BlockSpec((1, 8, tn), lambda l, j: (l, 0, j)),
        scratch_shapes=[pltpu.VMEM((n_vec, D, 1), F32)],
        compiler_params=_params(2, VMEM_LIMIT),
        name="adaln",
    )(cvecs[:, :, None], ada_w, ada_b[:, None, :])
    return out[:, :n_vec].reshape(depth, n_vec, N_MOD, D)


FF_PIECE = 256
FF_LOADS = FF // FF_PIECE


def _ffn_kernel(base, final, split_in, layer, which, *refs):
    y, refs = _tok_read(refs, split_in)
    mod_ref, g_ref, win_hbm, wout_hbm, fg_ref = refs[:5]
    n_out = 2 if final else 1
    outs = refs[5:5 + n_out]
    win_ref, wout_ref, stage_g, stage_u, stage_o, sems = refs[5 + n_out:]
    mod = mod_ref[0]
    h = _modulate(y, g_ref[...], mod[base:base + 1], mod[base + 1:base + 2]).astype(BF16)

    def hidden(lo, width):
        gate = jnp.dot(h, win_ref[:, lo:lo + width], preferred_element_type=F32)
        up = jnp.dot(h, win_ref[:, FF + lo:FF + lo + width], preferred_element_type=F32)
        a = (jax.nn.silu(gate) * up).astype(BF16)
        return jnp.dot(a, wout_ref[lo:lo + width, :], preferred_element_type=F32)

    def finish(acc):
        out = y + 0.5 * mod[base + 2:base + 3] * acc
        if final:
            _tok_write(outs[0], outs[1], _rmsnorm(out, fg_ref[...]))
        else:
            outs[0][...] = out

    @pl.when(pl.program_id(0) == 0)
    def _():
        def copies(c, slot):
            cols = pl.ds(c * FF_PIECE, FF_PIECE)
            return (pltpu.make_async_copy(win_hbm.at[layer, which, :, cols], stage_g.at[slot], sems.at[0, slot]),
                    pltpu.make_async_copy(win_hbm.at[layer, which, :, pl.ds(FF + c * FF_PIECE, FF_PIECE)],
                                          stage_u.at[slot], sems.at[1, slot]),
                    pltpu.make_async_copy(wout_hbm.at[layer, which, cols, :], stage_o.at[slot], sems.at[2, slot]))

        for cp in copies(0, 0):
            cp.start()
        acc = jnp.zeros(y.shape, F32)
        for c in range(FF_LOADS):
            slot = c % 2
            lo = c * FF_PIECE
            if c + 1 < FF_LOADS:
                for cp in copies(c + 1, 1 - slot):
                    cp.start()
            for cp in copies(c, slot):
                cp.wait()
            win_ref[:, lo:lo + FF_PIECE] = stage_g[slot].astype(BF16)
            win_ref[:, FF + lo:FF + lo + FF_PIECE] = stage_u[slot].astype(BF16)
            wout_ref[lo:lo + FF_PIECE, :] = stage_o[slot].astype(BF16)
            acc = acc + hidden(lo, FF_PIECE)
        finish(acc)

    @pl.when(pl.program_id(0) > 0)
    def _():
        finish(hidden(0, FF))


def _half_ffn(y, mods_l, g, ff_w_in, ff_w_out, layer, which, final_g=None):
    final = final_g is not None
    fg = final_g if final else g
    y_specs, y_args = _tok_specs(y, D)
    if final:
        out_shape, out_specs = _split_out(D)
    else:
        out_shape, out_specs = jax.ShapeDtypeStruct((TOK, D), F32), pl.BlockSpec((TM, D), _row)
    return pl.pallas_call(
        functools.partial(_ffn_kernel, 6 * which, final, isinstance(y, tuple), layer, which),
        out_shape=out_shape,
        grid=(NT,),
        in_specs=y_specs + [pl.BlockSpec((1, N_MOD, D), lambda i: (_mod_index(i), 0, 0)),
                            _const_spec((1, D)),
                            pl.BlockSpec(memory_space=pl.ANY),
                            pl.BlockSpec(memory_space=pl.ANY),
                            _const_spec((1, D))],
        out_specs=out_specs,
        scratch_shapes=[pltpu.VMEM((D, 2 * FF), BF16), pltpu.VMEM((FF, D), BF16),
                        pltpu.VMEM((2, D, FF_PIECE), F32), pltpu.VMEM((2, D, FF_PIECE), F32),
                        pltpu.VMEM((2, FF_PIECE, D), F32), pltpu.SemaphoreType.DMA((3, 2))],
        compiler_params=_params(1, VMEM_LIMIT),
        name="half_ffn",
    )(*y_args, mods_l, g[None], ff_w_in, ff_w_out, fg[None])


def _linear_kernel(x_ref, w_ref, o_ref):
    o_ref[...] = _dot(x_ref[...], w_ref[...]).astype(o_ref.dtype)


def _linear(x, w, tm, out_dtype):
    m, k = x.shape
    n = w.shape[1]
    return pl.pallas_call(
        _linear_kernel,
        out_shape=jax.ShapeDtypeStruct((m, n), out_dtype),
        grid=(m // tm,),
        in_specs=[pl.BlockSpec((tm, k), lambda i: (i, 0)), _const_spec((k, n))],
        out_specs=pl.BlockSpec((tm, n), lambda i: (i, 0)),
        compiler_params=_params(1),
        name="linear",
    )(x, w.astype(BF16))


MLA_SCALE = (MLA_NOPE + MLA_ROPE) ** -0.5
QW = MLA_HEADS * LANES
KR_AT = MLA_NOPE
IN_A_PAD = MLA_Q_RANK + MLA_KV_RANK + S5_WIDTH + LANES


def _inproj_a_kernel(y_ref, mod_ref, g_ref, win_ref, qn_ref, wuq_ref, kvn_ref, wk_ref, wv_ref,
                     cq_ref, sq_ref, ck_ref, sk_ref,
                     q_ref, ckv_ref, kru_ref, krr_ref, kn_ref, v_ref, u_ref):
    mod = mod_ref[0]
    h = _modulate(y_ref[...], g_ref[...], mod[3:4], mod[4:5]).astype(BF16)
    p = jnp.dot(h, win_ref[...], preferred_element_type=F32)
    o1 = MLA_Q_RANK
    o2 = o1 + MLA_KV_RANK
    o3 = o2 + S5_WIDTH
    q = _dot(_rmsnorm(p[:, :o1], qn_ref[...]), wuq_ref[...])
    q_ref[...] = (_rope(q, cq_ref[...], sq_ref[...]) * MLA_SCALE).astype(BF16)
    ckv = _rmsnorm(p[:, o1:o2], kvn_ref[...])
    ckv_b = ckv.astype(BF16)
    kn_ref[...] = jnp.dot(ckv_b, wk_ref[...], preferred_element_type=F32).astype(BF16)
    v_ref[...] = jnp.dot(ckv_b, wv_ref[...], preferred_element_type=F32).astype(BF16)
    u_ref[...] = p[:, o2:o3]
    krp = p[:, o3:]
    krr_ref[...] = _rope(krp, ck_ref[...], sk_ref[...]).astype(BF16)

    @pl.when(pl.program_id(0) < NT_P)
    def _():
        ckv_ref[...] = ckv
        kru_ref[...] = krp


def _inproj_a(y, mods_l, g, w_in, q_norm, w_uq, kv_norm, w_ukv):
    o1 = MLA_Q_RANK
    o2 = o1 + MLA_KV_RANK
    o3 = o2 + MLA_ROPE
    kr_cols = jnp.pad(w_in[:, o2:o3], ((0, 0), (KR_AT, LANES - KR_AT - MLA_ROPE)))
    w_ext = jnp.concatenate([w_in[:, :o2], w_in[:, o3:], kr_cols], axis=1).astype(BF16)
    dq = MLA_NOPE + MLA_ROPE
    w_uq_pad = jnp.pad(w_uq.reshape(MLA_Q_RANK, MLA_HEADS, dq),
                       ((0, 0), (0, 0), (0, LANES - dq))).reshape(MLA_Q_RANK, QW).astype(BF16)
    w_kv = w_ukv.reshape(MLA_KV_RANK, MLA_HEADS, MLA_NOPE + MLA_V)
    w_k = jnp.pad(w_kv[:, :, :MLA_NOPE], ((0, 0), (0, 0), (0, LANES - MLA_NOPE))).reshape(MLA_KV_RANK, QW)
    w_v = w_kv[:, :, MLA_NOPE:].reshape(MLA_KV_RANK, MLA_HEADS * MLA_V)
    w_k, w_v = w_k.astype(BF16), w_v.astype(BF16)
    cq, sq = _rope_tables(QW, tuple(h * LANES + MLA_NOPE for h in range(MLA_HEADS)))
    ck, sk = _rope_tables(LANES, (KR_AT,))
    row = _row
    pos = lambda i: (_pos_index(i), 0)
    widths = (QW, MLA_KV_RANK, LANES, LANES, QW, MLA_HEADS * MLA_V, S5_WIDTH)
    prompt_only = (1, 2)
    mxu_only = (0, 3, 4, 5)
    outs = pl.pallas_call(
        _inproj_a_kernel,
        out_shape=[jax.ShapeDtypeStruct((TOK_P if k in prompt_only else TOK, w), BF16 if k in mxu_only else F32)
                   for k, w in enumerate(widths)],
        grid=(NT,),
        in_specs=[pl.BlockSpec((TM, D), row),
                  pl.BlockSpec((1, N_MOD, D), lambda i: (_mod_index(i), 0, 0)),
                  _const_spec((1, D)),
                  _const_spec((D, IN_A_PAD)),
                  _const_spec((1, MLA_Q_RANK)),
                  _const_spec((MLA_Q_RANK, QW)),
                  _const_spec((1, MLA_KV_RANK)),
                  _const_spec((MLA_KV_RANK, QW)),
                  _const_spec((MLA_KV_RANK, MLA_HEADS * MLA_V)),
                  pl.BlockSpec((TM, QW), pos), pl.BlockSpec((TM, QW), pos),
                  pl.BlockSpec((TM, LANES), pos), pl.BlockSpec((TM, LANES), pos)],
        out_specs=[pl.BlockSpec((TM, w), _row_p if k in prompt_only else row)
                   for k, w in enumerate(widths)],
        compiler_params=_params(1, VMEM_LIMIT),
        name="inproj_even",
    )(y, mods_l, g[None], w_ext, q_norm[None], w_uq_pad, kv_norm[None], w_k, w_v,
      jnp.asarray(cq), jnp.asarray(sq), jnp.asarray(ck), jnp.asarray(sk))
    q, ckv, kr_unrot, kr_rot, kn, v, u = outs
    return q, ckv, kr_unrot, kr_rot, kn, v, u, (w_k, w_v)


def _mla_attn_kernel(nseg, q_ref, *refs):
    o_ref = refs[-1]
    tq = q_ref.shape[0]
    lane = lax.broadcasted_iota(jnp.int32, (tq, LANES), 1)
    for pair in range(MLA_HEADS // 2):
        outs = []
        for hh in range(2):
            h = 2 * pair + hh
            hs = slice(h * LANES, (h + 1) * LANES)
            qh = q_ref[:, hs]
            scores = []
            for s in range(nseg):
                kn_ref, kr_ref = refs[3 * s], refs[3 * s + 1]
                kh = (kn_ref[:, hs] + kr_ref[...]).astype(BF16)
                scores.append(_dot_nt(qh, kh))
            m = functools.reduce(jnp.maximum, [jnp.max(s, axis=-1, keepdims=True) for s in scores])
            es = [jnp.exp(s - m) for s in scores]
            l = functools.reduce(jnp.add, [jnp.sum(e, axis=-1, keepdims=True) for e in es])
            o = None
            for s in range(nseg):
                v_ref = refs[3 * s + 2]
                part = _dot(es[s], v_ref[:, pair * LANES:(pair + 1) * LANES])
                o = part if o is None else o + part
            outs.append(o / l)
        o_ref[:, pair * LANES:(pair + 1) * LANES] = jnp.where(lane < MLA_V, outs[0], outs[1])


def _mla_attention(q, kn, kr, v, n_batch, seq, tq, row0, ctx=None):
    qt = seq // tq
    qb0, kb0 = row0 // tq, row0 // seq
    in_specs = [pl.BlockSpec((tq, QW), lambda b, j: (qb0 + b * qt + j, 0))]
    args = [q]
    segs = []
    if ctx is not None:
        segs.append((ctx, PAST, 0))
    segs.append(((kn, kr, v), seq, kb0))
    for (a_kn, a_kr, a_v), ln, off in segs:
        idx = lambda b, j, off=off: (off + b, 0)
        in_specs += [pl.BlockSpec((ln, QW), idx), pl.BlockSpec((ln, LANES), idx),
                     pl.BlockSpec((ln, MLA_HEADS * MLA_V), idx)]
        args += [a_kn, a_kr, a_v]
    return pl.pallas_call(
        functools.partial(_mla_attn_kernel, len(segs)),
        out_shape=jax.ShapeDtypeStruct((n_batch * seq, MLA_HEADS * MLA_V), F32),
        grid=(n_batch, qt),
        in_specs=in_specs,
        out_specs=pl.BlockSpec((tq, MLA_HEADS * MLA_V), lambda b, j: (b * qt + j, 0)),
        compiler_params=_params(2, VMEM_LIMIT),
        name="mla_attention",
    )(*args)


def _cpow(ar, ai, e, nbits):
    rr = jnp.ones_like(ar)
    ri = jnp.zeros_like(ar)
    br, bi = ar, ai
    for k in range(nbits):
        bit = ((e >> k) & 1) == 1
        nr = rr * br - ri * bi
        ni = rr * bi + ri * br
        rr = jnp.where(bit, nr, rr)
        ri = jnp.where(bit, ni, ri)
        if k + 1 < nbits:
            br, bi = br * br - bi * bi, 2.0 * br * bi
    return rr, ri


def _s5_abar_kernel(lr_ref, li_ref, ls_ref, o_ref):
    step = jnp.exp(ls_ref[...])
    lr = jnp.minimum(lr_ref[...], -1e-4)
    li = li_ref[...]
    mag = jnp.exp(lr * step)
    ar = mag * jnp.cos(li * step)
    ai = mag * jnp.sin(li * step)
    den = lr * lr + li * li
    o_ref[0] = ar
    o_ref[1] = ai
    o_ref[2] = ((ar - 1.0) * lr + ai * li) / den
    o_ref[3] = (ai * lr - (ar - 1.0) * li) / den


def _s5_prep_kernel(arow_ref, acol_ref, btr_ref, bti_ref, ctr_ref, cti_ref,
                    wi_ref, ws_ref, wo_ref, ap_ref):
    n2 = 2 * S5_N
    lane_o = lax.broadcasted_iota(jnp.int32, (n2, S5_CW), 1)
    blk_o = lane_o >> 4
    row_o = lax.broadcasted_iota(jnp.int32, (n2, S5_CW), 0)
    lane_k = lax.broadcasted_iota(jnp.int32, (S5_GROUP, S5_CW), 1)
    row_k = lax.broadcasted_iota(jnp.int32, (S5_GROUP, S5_CW), 0)
    lane_b = lax.broadcasted_iota(jnp.int32, (S5_GROUP, n2), 1)
    lane_a = lax.broadcasted_iota(jnp.int32, (1, n2), 1)
    rep = ((lane_k & (S5_GROUP - 1)) == row_k).astype(BF16)

    def tile16(x):
        hi = x.astype(BF16)
        r1 = x - hi.astype(F32)
        mid = r1.astype(BF16)
        lo = (r1 - mid.astype(F32)).astype(BF16)
        d = functools.partial(jnp.dot, preferred_element_type=F32)
        return d(hi, rep) + d(mid, rep) + d(lo, rep)

    intra = [None] * S5_T
    for d in range(2):
        ar, ai, fr, fi = (arow_ref[d, 0, k:k + 1, :] for k in range(4))
        btr, bti = btr_ref[d, 0], bti_ref[d, 0]
        bbr = fr * btr - fi * bti
        bbi = fr * bti + fi * btr
        pws = [(jnp.ones_like(ar), jnp.zeros_like(ar))]
        for _ in range(S5_T):
            pr, pi = pws[-1]
            pws.append((pr * ar - pi * ai, pr * ai + pi * ar))
        for s in range(S5_T):
            pr, pi = pws[S5_T - 1 - s] if d == 0 else pws[s]
            ws_ref[d, 0, s * S5_GROUP:(s + 1) * S5_GROUP, :] = jnp.where(
                lane_b < S5_N, pr * bbr - pi * bbi, pr * bbi + pi * bbr).astype(BF16)

        acol = acol_ref[d, 0]
        arc = jnp.broadcast_to(acol[:, 0:1], (n2, S5_CW))
        aic = jnp.broadcast_to(acol[:, 1:2], (n2, S5_CW))
        ctr, cti = tile16(ctr_ref[d, 0]), tile16(cti_ref[d, 0])
        e_lag = blk_o if d == 0 else (S5_T - 1 - blk_o)
        pqr, pqi = _cpow(arc, aic, e_lag, 4)
        qr = pqr * ctr - pqi * cti
        qi = pqr * cti + pqi * ctr
        wo_ref[d, 0] = jnp.where(row_o < S5_N, qr * arc - qi * aic, -(qr * aic + qi * arc)).astype(BF16)
        q_stack = jnp.where(row_o < S5_N, qr, qi)
        bb_mix = jnp.where(lane_b < S5_N, bbr, -bbi)
        kt = _dot3(bb_mix, q_stack)
        for s in range(S5_T):
            if d == 0:
                blk = jnp.where(lane_k >= S5_GROUP * s, pltpu.roll(kt, S5_GROUP * s, 1), 0.0)
            else:
                blk = jnp.where(lane_k < S5_GROUP * (s + 1),
                                pltpu.roll(kt, (S5_GROUP * (s + 1)) % S5_CW, 1), 0.0)
            intra[s] = blk if intra[s] is None else intra[s] + blk

        pr1, pi1 = pws[S5_T]
        for k in range(6):
            ap_ref[d, 0, k:k + 1, :] = pr1
            ap_ref[d, 0, 8 + k:9 + k, :] = jnp.where(lane_a < S5_N, -pi1, pi1)
            pr1, pi1 = pr1 * pr1 - pi1 * pi1, 2.0 * pr1 * pi1
        ap_ref[d, 0, 6:8, :] = jnp.zeros((2, n2), F32)
        ap_ref[d, 0, 14:16, :] = jnp.zeros((2, n2), F32)
    for s in range(S5_T):
        wi_ref[0, s * S5_GROUP:(s + 1) * S5_GROUP, :] = intra[s].astype(BF16)


def _s5_prep(a_re, a_im, log_step, b_re, b_im, c_re, c_im):
    g, n, n2 = S5_GROUPS, S5_N, 2 * S5_N
    abar = pl.pallas_call(
        _s5_abar_kernel,
        out_shape=jax.ShapeDtypeStruct((4, 2 * g, n), F32),
        grid=(1,),
        in_specs=[_const_spec((2 * g, n)), _const_spec((2 * g, n)), _const_spec((2 * g, 1))],
        out_specs=pl.BlockSpec((4, 2 * g, n), lambda i: (0, 0, 0)),
        compiler_params=_params(1),
        name="s5_abar",
    )(a_re.reshape(2 * g, n), a_im.reshape(2 * g, n), log_step.reshape(2 * g, 1))
    abar = jnp.concatenate([abar, abar], axis=-1).reshape(4, 2, g, n2)
    arow = abar.transpose(1, 2, 0, 3)
    acol = abar[:2].transpose(1, 2, 3, 0)
    bt = lambda b: jnp.concatenate([jnp.swapaxes(b, 2, 3)] * 2, axis=-1)
    ct = lambda c: jnp.concatenate([jnp.swapaxes(c, 2, 3)] * 2, axis=2)
    spec4 = lambda r, c: pl.BlockSpec((2, 1, r, c), lambda i: (0, i, 0, 0))
    return pl.pallas_call(
        _s5_prep_kernel,
        out_shape=[jax.ShapeDtypeStruct((g, S5_CW, S5_CW), BF16),
                   jax.ShapeDtypeStruct((2, g, S5_CW, n2), BF16),
                   jax.ShapeDtypeStruct((2, g, n2, S5_CW), BF16),
                   jax.ShapeDtypeStruct((2, g, 16, n2), F32)],
        grid=(g,),
        in_specs=[spec4(4, n2), spec4(n2, 2),
                  spec4(S5_GROUP, n2), spec4(S5_GROUP, n2), spec4(n2, S5_GROUP), spec4(n2, S5_GROUP)],
        out_specs=[pl.BlockSpec((1, S5_CW, S5_CW), lambda i: (i, 0, 0)),
                   spec4(S5_CW, n2), spec4(n2, S5_CW), spec4(16, n2)],
        compiler_params=_params(1),
        name="s5_prep",
    )(arow, acol, bt(b_re), bt(b_im), ct(c_re), ct(c_im))


def _cmul_rows(x, p1, p2):
    return x * p1 + pltpu.roll(x, S5_N, 1) * p2


S5_OCT = LANES // S5_GROUP
S5_RB = 48


def _s5_core_kernel(u_ref, wi_ref, ws_ref, wo_ref, ap_ref, h0_ref, d_ref, y_ref, fin_ref,
                    ug_ref, yg_ref, z_ref, sp_ref):
    n2 = 2 * S5_N
    blk = lax.broadcasted_iota(jnp.int32, (S5_RB, LANES), 1) >> 4

    def tok_rows(r0, t):
        return pl.ds(r0 * S5_T + t, S5_RB, stride=S5_T)

    def block_transpose(xs):
        for b in range(3):
            s = 1 << b
            odd = ((blk >> b) & 1) == 1
            new = list(xs)
            for i in range(S5_OCT):
                if not i & s:
                    new[i] = jnp.where(odd, pltpu.roll(xs[i + s], s * S5_GROUP, 1), xs[i])
                    new[i + s] = jnp.where(odd, xs[i + s], pltpu.roll(xs[i], LANES - s * S5_GROUP, 1))
            xs = new
        return xs

    def gather(rb, carry):
        r0 = pl.multiple_of(rb * S5_RB, S5_RB)
        for half in range(2):
            xs = [u_ref[tok_rows(r0, S5_OCT * half + tt), :] for tt in range(S5_OCT)]
            for gl, x in enumerate(block_transpose(xs)):
                ug_ref[gl, pl.ds(r0, S5_RB), half * LANES:(half + 1) * LANES] = x
        return carry

    lax.fori_loop(0, S5_ROWS // S5_RB, gather, 0)

    r = lax.broadcasted_iota(jnp.int32, (S5_ROWS, n2), 0)
    in_p = r < S5_ROWS_P
    rib = jnp.where(in_p, r & (CH_P - 1), (r - S5_ROWS_P) & (CH_S - 1))
    nch = jnp.where(in_p, CH_P, CH_S)

    def group(gl, carry):
        ub = ug_ref[gl].astype(BF16)
        y = jnp.dot(ub, wi_ref[gl], preferred_element_type=F32)
        for d in range(2):
            z_ref[...] = jnp.dot(ub, ws_ref[d, gl], preferred_element_type=F32)
            p1, p2 = ap_ref[d, gl, 0:1, :], ap_ref[d, gl, 8:9, :]
            edge = [S5_ROWS_P + CH_S * b + (0 if d == 0 else CH_S - 1) for b in range(NB_S)]
            for b in range(NB_S):
                h0 = h0_ref[gl, d, b:b + 1, :]
                z_ref[edge[b]:edge[b] + 1, :] = z_ref[edge[b]:edge[b] + 1, :] + _cmul_rows(h0, p1, p2)
            s = z_ref[...]
            for k in range(6):
                sh = 1 << k
                if d == 0:
                    t = jnp.where(rib >= sh, pltpu.roll(s, sh, 0), 0.0)
                else:
                    t = jnp.where(rib < nch - sh, pltpu.roll(s, S5_ROWS - sh, 0), 0.0)
                s = s + _cmul_rows(t, ap_ref[d, gl, k:k + 1, :], ap_ref[d, gl, 8 + k:9 + k, :])
            z_ref[...] = s
            first = CH_P - 1 if d == 0 else 0
            fin_ref[gl, d] = z_ref[pl.ds(first, NB_P, stride=CH_P), :]
            if d == 0:
                sp_ref[...] = jnp.where(rib >= 1, pltpu.roll(s, 1, 0), 0.0)
            else:
                sp_ref[...] = jnp.where(rib < nch - 1, pltpu.roll(s, S5_ROWS - 1, 0), 0.0)
            for b in range(NB_S):
                sp_ref[edge[b]:edge[b] + 1, :] = h0_ref[gl, d, b:b + 1, :]
            y = y + jnp.dot(sp_ref[...].astype(BF16), wo_ref[d, gl], preferred_element_type=F32)
        yg_ref[gl] = y
        return carry

    lax.fori_loop(0, S5_OCT, group, 0)

    def scatter(rb, carry):
        r0 = pl.multiple_of(rb * S5_RB, S5_RB)
        for half in range(2):
            ys = [yg_ref[gl, pl.ds(r0, S5_RB), half * LANES:(half + 1) * LANES] for gl in range(S5_OCT)]
            for tt, acc in enumerate(block_transpose(ys)):
                rows = tok_rows(r0, S5_OCT * half + tt)
                y_ref[rows, :] = jax.nn.gelu(acc + d_ref[...] * u_ref[rows, :])
        return carry

    lax.fori_loop(0, S5_ROWS // S5_RB, scatter, 0)


def _s5_core(u, prep, h0, d_skip):
    w_intra, w_state, w_out, apow = prep
    g, n2 = S5_GROUPS, 2 * S5_N
    spec4 = lambda r, c: pl.BlockSpec((2, S5_OCT, r, c), lambda i: (0, i, 0, 0))
    slab = pl.BlockSpec((TOK, LANES), lambda i: (0, i))
    return pl.pallas_call(
        _s5_core_kernel,
        out_shape=[jax.ShapeDtypeStruct((TOK, S5_WIDTH), F32),
                   jax.ShapeDtypeStruct((g, 2, NB_P, n2), F32)],
        grid=(g // S5_OCT,),
        in_specs=[slab,
                  pl.BlockSpec((S5_OCT, S5_CW, S5_CW), lambda i: (i, 0, 0)),
                  spec4(S5_CW, n2), spec4(n2, S5_CW), spec4(16, n2),
                  pl.BlockSpec((S5_OCT, 2, 8, n2), lambda i: (i, 0, 0, 0)),
                  pl.BlockSpec((1, LANES), lambda i: (0, i))],
        out_specs=[slab, pl.BlockSpec((S5_OCT, 2, NB_P, n2), lambda i: (i, 0, 0, 0))],
        scratch_shapes=[pltpu.VMEM((S5_OCT, S5_ROWS, S5_CW), F32), pltpu.VMEM((S5_OCT, S5_ROWS, S5_CW), F32),
                        pltpu.VMEM((S5_ROWS, n2), F32), pltpu.VMEM((S5_ROWS, n2), F32)],
        compiler_params=_params(1, VMEM_LIMIT),
        name="s5_scan",
    )(u, w_intra, w_state, w_out, apow, h0, d_skip[None])


def _outproj_kernel(glu, split1, split2, y_ref, mod_ref, *refs):
    a1, refs = _tok_read(refs, split1)
    a2, refs = _tok_read(refs, split2)
    w1_ref, w2_ref, wg_ref, o_ref = refs
    if glu:
        a2 = a2 * jax.nn.sigmoid(_dot(a2, wg_ref[...]))
    out = _dot(a1, w1_ref[...]) + _dot(a2, w2_ref[...])
    o_ref[...] = y_ref[...] + mod_ref[0][5:6] * out


def _outproj(y, mods_l, a1, a2, w_out, w_glu=None):
    glu = w_glu is not None
    k1 = k2 = w_out.shape[0] // 2
    wg = (w_glu if glu else jnp.zeros((8, LANES), F32)).astype(BF16)
    s1, a1_args = _tok_specs(a1, k1)
    s2, a2_args = _tok_specs(a2, k2)
    return pl.pallas_call(
        functools.partial(_outproj_kernel, glu, isinstance(a1, tuple), isinstance(a2, tuple)),
        out_shape=jax.ShapeDtypeStruct((TOK, D), F32),
        grid=(NT,),
        in_specs=[pl.BlockSpec((TM, D), _row),
                  pl.BlockSpec((1, N_MOD, D), lambda i: (_mod_index(i), 0, 0))] + s1 + s2
                 + [_const_spec((k1, D)), _const_spec((k2, D)), _const_spec(wg.shape)],
        out_specs=pl.BlockSpec((TM, D), _row),
        compiler_params=_params(1, VMEM_LIMIT),
        name="outproj",
    )(y, mods_l, *a1_args, *a2_args, w_out[:k1].astype(BF16), w_out[k1:].astype(BF16), wg)


DF_SCALE = DF_DH ** -0.5
DFW = DF_HEADS * 2 * DF_DH
IN_B = 3 * HY_WIDTH + 2 * DFW + DF_HEADS * DF_V


def _inproj_b_kernel(y_ref, mod_ref, g_ref, win_ref, c_ref, s_ref,
                     hy_ref, q_ref, kp_ref, ks_ref, vp_ref, vs_ref):
    mod = mod_ref[0]
    h = _modulate(y_ref[...], g_ref[...], mod[3:4], mod[4:5]).astype(BF16)
    p = jnp.dot(h, win_ref[...], preferred_element_type=F32)
    o1 = 3 * HY_WIDTH
    hy_ref[...] = p[:, :o1]
    q_ref[...] = (_rope(p[:, o1:o1 + DFW], c_ref[...], s_ref[...]) * DF_SCALE).astype(BF16)
    _tok_write(kp_ref, ks_ref, _rope(p[:, o1 + DFW:o1 + 2 * DFW], c_ref[...], s_ref[...]))
    _tok_write(vp_ref, vs_ref, p[:, o1 + 2 * DFW:])


def _inproj_b(y, mods_l, g, w_in):
    cs, sn = _rope_tables(DFW, tuple(range(0, DFW, DF_DH)))
    pos = lambda i: (_pos_index(i), 0)
    k_shapes, k_specs = _split_out(DFW, BF16)
    v_shapes, v_specs = _split_out(DF_HEADS * DF_V, BF16)
    hy_u, q, kp, ks, vp, vs = pl.pallas_call(
        _inproj_b_kernel,
        out_shape=[jax.ShapeDtypeStruct((TOK, 3 * HY_WIDTH), F32), jax.ShapeDtypeStruct((TOK, DFW), BF16)]
                  + k_shapes + v_shapes,
        grid=(NT,),
        in_specs=[pl.BlockSpec((TM, D), _row),
                  pl.BlockSpec((1, N_MOD, D), lambda i: (_mod_index(i), 0, 0)),
                  _const_spec((1, D)), _const_spec((D, IN_B)),
                  pl.BlockSpec((TM, DFW), pos), pl.BlockSpec((TM, DFW), pos)],
        out_specs=[pl.BlockSpec((TM, 3 * HY_WIDTH), _row), pl.BlockSpec((TM, DFW), _row)] + k_specs + v_specs,
        compiler_params=_params(1, VMEM_LIMIT),
        name="inproj_odd",
    )(y, mods_l, g[None], w_in.astype(BF16), jnp.asarray(cs), jnp.asarray(sn))
    return hy_u, q, (kp, ks), (vp, vs)


def _diff_attn_kernel(nseg, lam_init, q_ref, lam_ref, sub_ref, *refs):
    o_ref = refs[-1]
    tq = q_ref.shape[0]
    lp = lam_ref[...]
    lam = (jnp.exp(jnp.sum(lp[0:1] * lp[1:2], axis=-1, keepdims=True))
           - jnp.exp(jnp.sum(lp[2:3] * lp[3:4], axis=-1, keepdims=True)) + lam_init)
    lane = lax.broadcasted_iota(jnp.int32, (tq, LANES), 1)
    for pair in range(DF_HEADS // 2):
        cs = slice(pair * LANES, (pair + 1) * LANES)
        q = q_ref[:, cs]
        ks = [refs[2 * s][:, cs].astype(BF16) for s in range(nseg)]
        vs = [refs[2 * s + 1][:, cs].astype(BF16) for s in range(nseg)]
        outs = []
        for hh in range(2):
            parts = []
            for half in range(2):
                unit = 2 * hh + half
                qm = jnp.where((lane >> 5) == unit, q, jnp.zeros_like(q))
                scores = [_dot_nt(qm, k) for k in ks]
                m = functools.reduce(jnp.maximum, [jnp.max(s, axis=-1, keepdims=True) for s in scores])
                es = [jnp.exp(s - m) for s in scores]
                l = functools.reduce(jnp.add, [jnp.sum(e, axis=-1, keepdims=True) for e in es])
                pv = functools.reduce(jnp.add, [_dot(e, v) for e, v in zip(es, vs)])
                parts.append(pv * (1.0 / l))
            o = parts[0] - lam * parts[1]
            mine = (lane >> 6) == hh
            ms = jnp.sum(jnp.where(mine, o * o, 0.0), axis=-1, keepdims=True) * (1.0 / DF_V)
            outs.append(o * lax.rsqrt(ms + EPS))
        o_ref[:, cs] = jnp.where(lane < DF_V, outs[0], outs[1]) * sub_ref[...] * (1.0 - lam_init)


def _diff_attention(q, k, v, lam_p, subln, lam_init, n_batch, seq, tq, row0, ctx=None):
    qt = seq // tq
    qb0, kb0 = row0 // tq, 0
    in_specs = [pl.BlockSpec((tq, DFW), lambda b, j: (qb0 + b * qt + j, 0)),
                pl.BlockSpec((4, DF_DH), lambda b, j: (0, 0)),
                pl.BlockSpec((1, LANES), lambda b, j: (0, 0))]
    args = [q, lam_p, jnp.concatenate([subln, subln])[None]]
    segs = []
    if ctx is not None:
        segs.append((ctx, PAST, 0))
    segs.append(((k, v), seq, kb0))
    for (a_k, a_v), ln, off in segs:
        idx = lambda b, j, off=off: (off + b, 0)
        in_specs += [pl.BlockSpec((ln, DFW), idx), pl.BlockSpec((ln, DF_HEADS * DF_V), idx)]
        args += [a_k, a_v]
    return pl.pallas_call(
        functools.partial(_diff_attn_kernel, len(segs), lam_init),
        out_shape=jax.ShapeDtypeStruct((n_batch * seq, DF_HEADS * DF_V), F32),
        grid=(n_batch, qt),
        in_specs=in_specs,
        out_specs=pl.BlockSpec((tq, DF_HEADS * DF_V), lambda b, j: (b * qt + j, 0)),
        compiler_params=_params(2, VMEM_LIMIT),
        name="diff_attention",
    )(*args)


def _hy_filter_kernel(feat_ref, w1_ref, b1_ref, w2_ref, b2_ref, fq_ref, w3_ref, dec_ref, o_ref):
    feat = feat_ref[...]
    fq = fq_ref[...]
    h = jnp.sin(fq * (_dot3(feat, w1_ref[...]) + b1_ref[...]))
    h = jnp.sin(fq * (_dot3(h, w2_ref[...]) + b2_ref[...]))
    window = jnp.exp(-feat[:, 0:1] * jnp.abs(dec_ref[...]))
    for j in range(4):
        cs = slice(j * HY_WIDTH, (j + 1) * HY_WIDTH)
        o_ref[:, cs] = _dot3(h, w3_ref[:, cs]) * window


def _hy_spectrum_kernel(L, cs_ref, hf_ref, hb_ref, o_ref):
    row = lax.broadcasted_iota(jnp.int32, (L, HY_WIDTH), 0)
    first = row == 0
    tf = _dot(cs_ref[...], hf_ref[...])
    tb = _dot(cs_ref[...], jnp.where(first, 0.0, hb_ref[...]))
    ka = tf[:L] + tb[:L]
    kb = jnp.where(first, tf[L:] + tb[L:], tf[L:] - tb[L:])
    wv = jnp.where(first, 1.0 / (2 * L), 2.0 / (2 * L))
    o_ref[0, 0] = ka * wv
    o_ref[0, 1] = jnp.where(first, 0.0, kb) * wv
    o_ref[0, 2] = jnp.where(first, kb, ka) * wv


def _hy_conv_kernel(L, cs_ref, ct_ref, kf_ref, v_ref, x1_ref, x2_ref,
                    wv_ref, w1_ref, w2_ref, bias_ref, o_ref):
    row = lax.broadcasted_iota(jnp.int32, v_ref.shape, 0)

    def short(x_ref, w_ref):
        x = x_ref[...]
        prev = jnp.where(row >= 1, pltpu.roll(x, 1, 0), 0.0)
        nxt = jnp.where(row <= L - 2, pltpu.roll(x, L - 1, 0), 0.0)
        return w_ref[0:1] * prev + w_ref[1:2] * x + w_ref[2:3] * nxt

    z = short(v_ref, wv_ref)
    gates = (short(x1_ref, w1_ref), short(x2_ref, w2_ref))
    for n in range(2):
        ab = _dot(cs_ref[...], z)
        a, b = ab[:L], ab[L:]
        ka, kb1, ka2 = kf_ref[n, 0], kf_ref[n, 1], kf_ref[n, 2]
        pq = jnp.concatenate([a * ka - b * kb1, a * kb1 + b * ka2], axis=0)
        conv = _dot(ct_ref[...], pq)
        z = gates[n] * (conv + bias_ref[n:n + 1] * z)
    o_ref[...] = z


def _hyena_spectrum(L, phy):
    conv_w, w1, b1, w2, b2, freq, w3, decay, bias = phy
    feat = jnp.asarray(_hyena_features(L))
    w1p = jnp.pad(w1, ((0, LANES - HY_EMB), (0, 0)))
    filt = pl.pallas_call(
        _hy_filter_kernel,
        out_shape=jax.ShapeDtypeStruct((L, 4 * HY_WIDTH), F32),
        grid=(1,),
        in_specs=[_const_spec((L, LANES)), _const_spec((LANES, HY_FH)), _const_spec((1, HY_FH)),
                  _const_spec((HY_FH, HY_FH)), _const_spec((1, HY_FH)), _const_spec((1, HY_FH)),
                  _const_spec((HY_FH, 4 * HY_WIDTH)), _const_spec((1, HY_WIDTH))],
        out_specs=pl.BlockSpec((L, 4 * HY_WIDTH), lambda i: (0, 0)),
        compiler_params=_params(1, VMEM_LIMIT),
        name="hyena_filter",
    )(feat, w1p, b1[None], w2, b2[None], freq[None], w3, decay[None])
    cs = jnp.asarray(_dft_tables(L)[0]).astype(BF16)
    return pl.pallas_call(
        functools.partial(_hy_spectrum_kernel, L),
        out_shape=jax.ShapeDtypeStruct((2, 3, L, HY_WIDTH), F32),
        grid=(2,),
        in_specs=[_const_spec((2 * L, L)),
                  pl.BlockSpec((L, HY_WIDTH), lambda n: (0, n)),
                  pl.BlockSpec((L, HY_WIDTH), lambda n: (0, 2 + n))],
        out_specs=pl.BlockSpec((1, 3, L, HY_WIDTH), lambda n: (n, 0, 0, 0)),
        compiler_params=_params(1, VMEM_LIMIT),
        name="hyena_spectrum",
    )(cs, filt, filt)


def _hyena_conv(hy_u, spec, phy, n_batch, L, cb, row0):
    conv_w, bias = phy[0], phy[8]
    cs, ct = (jnp.asarray(t).astype(BF16) for t in _dft_tables(L))
    nc = HY_WIDTH // cb
    rb0 = row0 // L
    col = lambda off: (lambda b, c: (0, off * nc + c))
    tok = lambda off: (lambda b, c: (rb0 + b, off * nc + c))
    return pl.pallas_call(
        functools.partial(_hy_conv_kernel, L),
        out_shape=jax.ShapeDtypeStruct((n_batch * L, HY_WIDTH), F32),
        grid=(n_batch, nc),
        in_specs=[_const_spec((2 * L, L)), _const_spec((L, 2 * L)),
                  pl.BlockSpec((2, 3, L, cb), lambda b, c: (0, 0, 0, c)),
                  pl.BlockSpec((L, cb), tok(0)), pl.BlockSpec((L, cb), tok(1)), pl.BlockSpec((L, cb), tok(2)),
                  pl.BlockSpec((3, cb), col(0)), pl.BlockSpec((3, cb), col(1)), pl.BlockSpec((3, cb), col(2)),
                  pl.BlockSpec((2, cb), col(0))],
        out_specs=pl.BlockSpec((L, cb), lambda b, c: (b, c)),
        compiler_params=_params(2, VMEM_LIMIT),
        name="hyena_conv",
    )(cs, ct, spec, hy_u, hy_u, hy_u, conv_w, conv_w, conv_w, bias)


def _even_mixer(y, mods_l, g, pa, ps5, ctx_ckv, ctx_krope, ctx_state):
    w_in, w_out, q_norm, w_uq, kv_norm, w_ukv = pa
    a_re, a_im, log_step, b_re, b_im, c_re, c_im, d_skip, w_glu = ps5
    q, ckv, kr_unrot, kr_rot, kn, v, u, (w_k, w_v) = _inproj_a(y, mods_l, g, w_in, q_norm, w_uq, kv_norm, w_ukv)

    ctx_flat = ctx_ckv.reshape(NB_S * PAST, MLA_KV_RANK)
    ctx_kn = _linear(ctx_flat, w_k, PAST, BF16)
    ctx_v = _linear(ctx_flat, w_v, PAST, BF16)
    ctx_kr = jnp.pad(ctx_krope.reshape(NB_S * PAST, MLA_ROPE),
                     ((0, 0), (KR_AT, LANES - KR_AT - MLA_ROPE))).astype(BF16)
    att_p = _mla_attention(q, kn, kr_rot, v, NB_P, L_P, L_P, 0)
    att_s = _mla_attention(q, kn, kr_rot, v, NB_S, L_S, TM, TOK_P, ctx=(ctx_kn, ctx_kr, ctx_v))

    prep = _s5_prep(a_re, a_im, log_step, b_re, b_im, c_re, c_im)
    h0 = ctx_state.transpose(3, 1, 0, 2, 4).reshape(S5_GROUPS, 2, NB_S, 2 * S5_N)
    h0 = jnp.pad(h0, ((0, 0), (0, 0), (0, 8 - NB_S), (0, 0)))
    s5y, fin = _s5_core(u, prep, h0, d_skip)

    y = _outproj(y, mods_l, (att_p, att_s), s5y, w_out, w_glu)
    new_ckv = ckv.reshape(NB_P, L_P, MLA_KV_RANK)
    new_krope = kr_unrot[:, KR_AT:KR_AT + MLA_ROPE].reshape(NB_P, L_P, MLA_ROPE)
    new_state = fin.reshape(S5_GROUPS, 2, NB_P, 2, S5_N).transpose(2, 1, 3, 0, 4)
    return y, new_ckv, new_krope, new_state


def _odd_mixer(y, mods_l, g, pb, phy, ctx_k, ctx_v, lam_init):
    w_in, w_out, lam_p, subln = pb
    hy_u, q, (k_p, k_s), (v_p, v_s) = _inproj_b(y, mods_l, g, w_in)
    hy_p = _hyena_conv(hy_u, _hyena_spectrum(L_P, phy), phy, NB_P, L_P, HY_WIDTH, 0)
    hy_s = _hyena_conv(hy_u, _hyena_spectrum(L_S, phy), phy, NB_S, L_S, HY_WIDTH // 2, TOK_P)
    ctx = (ctx_k.reshape(NB_S * PAST, DFW), ctx_v.reshape(NB_S * PAST, DF_HEADS * DF_V))
    att_p = _diff_attention(q, k_p, v_p, lam_p, subln, lam_init, NB_P, L_P, L_P, 0)
    att_s = _diff_attention(q, k_s, v_s, lam_p, subln, lam_init, NB_S, L_S, TM // 2, TOK_P, ctx=ctx)
    y = _outproj(y, mods_l, (hy_p, hy_s), (att_p, att_s), w_out)
    new_k = k_p.reshape(NB_P, L_P, DF_HEADS, 2, DF_DH)
    new_v = v_p.reshape(NB_P, L_P, DF_HEADS, DF_V)
    return y, new_k, new_v


def kernel(x_prompt, x_sample, c, c_ctx, cache_mla_ckv, cache_mla_krope, state_s5, cache_diff_k, cache_diff_v, ada_w, ada_b, norm_g, ff_w_in, ff_w_out, w_in_a, w_out_a, mla_q_norm, mla_w_uq, mla_kv_norm, mla_w_ukv, s5_a_re, s5_a_im, s5_log_step, s5_b_re, s5_b_im, s5_c_re, s5_c_im, s5_d, s5_w_glu, w_in_b, w_out_b, hy_conv, hy_w1, hy_b1, hy_w2, hy_b2, hy_freq, hy_w3, hy_decay, hy_bias, df_lambda, df_subln, final_norm):
    depth = ada_w.shape[0]
    y = (x_prompt.reshape(TOK_P, D), x_sample.reshape(TOK_S, D))
    mods = _adaln(jnp.concatenate([c_ctx[None], c], axis=0), ada_w, ada_b)
    new_ckv, new_krope, new_s5, new_dk, new_dv = [], [], [], [], []
    for l in range(depth):
        y = _half_ffn(y, mods[l], norm_g[l, 0], ff_w_in, ff_w_out, l, 0)
        if l % 2 == 0:
            e = l // 2
            pa = (w_in_a[e], w_out_a[e], mla_q_norm[e], mla_w_uq[e], mla_kv_norm[e], mla_w_ukv[e])
            ps5 = (s5_a_re[e], s5_a_im[e], s5_log_step[e], s5_b_re[e], s5_b_im[e],
                   s5_c_re[e], s5_c_im[e], s5_d[e], s5_w_glu[e])
            y, ckv, krope, st = _even_mixer(y, mods[l], norm_g[l, 1], pa, ps5, cache_mla_ckv[:, e],
                                            cache_mla_krope[:, e], state_s5[:, e])
            new_ckv.append(ckv)
            new_krope.append(krope)
            new_s5.append(st)
        else:
            o = l // 2
            lam_init = 0.8 - 0.6 * math.exp(-0.3 * l)
            pb = (w_in_b[o], w_out_b[o], df_lambda[o], df_subln[o])
            phy = (hy_conv[o], hy_w1[o], hy_b1[o], hy_w2[o], hy_b2[o], hy_freq[o],
                   hy_w3[o], hy_decay[o], hy_bias[o])
            y, dk, dv = _odd_mixer(y, mods[l], norm_g[l, 1], pb, phy, cache_diff_k[:, o],
                                   cache_diff_v[:, o], lam_init)
            new_dk.append(dk)
            new_dv.append(dv)
        last = l == depth - 1
        y = _half_ffn(y, mods[l], norm_g[l, 2], ff_w_in, ff_w_out, l, 1,
                      final_g=final_norm if last else None)
    y_prompt = y[0].reshape(NB_P, L_P, D)
    y_sample = y[1].reshape(NB_S, L_S, D)
    return (y_prompt, y_sample, jnp.stack(new_ckv, axis=1), jnp.stack(new_krope, axis=1),
            jnp.stack(new_s5, axis=1), jnp.stack(new_dk, axis=1), jnp.stack(new_dv, axis=1))
```

```python
import functools
import math

import numpy as np
import jax
import jax.numpy as jnp
from jax import lax
from jax.experimental import pallas as pl
from jax.experimental.pallas import tpu as pltpu

F32 = jnp.float32
BF16 = jnp.bfloat16

D = 1024
NB_P, L_P = 16, 256
NB_S, L_S = 2, 1024
PAST = 256
GRID_W = 64
N_MOD = 9
FF = 2816
EPS = 1e-6
ROPE_BASE = 10000.0

MLA_HEADS, MLA_NOPE, MLA_ROPE, MLA_V = 8, 64, 32, 64
MLA_Q_RANK, MLA_KV_RANK = 384, 256
S5_WIDTH, S5_GROUP, S5_N = 512, 16, 64
S5_GROUPS = S5_WIDTH // S5_GROUP
HY_WIDTH, HY_BANDS, HY_FH = 512, 16, 64
HY_EMB = 2 * HY_BANDS + 1
DF_HEADS, DF_DH = 8, 32
DF_V = 2 * DF_DH

TOK_P = NB_P * L_P
TOK_S = NB_S * L_S
TOK = TOK_P + TOK_S
TM = 512
NT = TOK // TM
NT_P = TOK_P // TM
TILES_PER_SAMPLE = L_S // TM

LANES = 128
S5_T = 16
S5_CW = S5_T * S5_GROUP
CH_P = L_P // S5_T
CH_S = L_S // S5_T
S5_ROWS = NB_P * CH_P + NB_S * CH_S
S5_ROWS_P = NB_P * CH_P

VMEM_LIMIT = 56 * 1024 * 1024


def _params(n_grid, vmem=None):
    return pltpu.CompilerParams(dimension_semantics=("arbitrary",) * n_grid,
                                vmem_limit_bytes=vmem)


def _const_spec(shape):
    nd = len(shape)
    return pl.BlockSpec(shape, lambda *_: (0,) * nd, pipeline_mode=pl.Buffered(1))


def _mod_index(i):
    return jnp.where(i < NT_P, 0, 1 + (i - NT_P) // TILES_PER_SAMPLE)


def _pos_index(i):
    return jnp.where(i < NT_P, 0, 1 + (i - NT_P) % TILES_PER_SAMPLE)


def _row(i):
    return (i, 0)


def _row_p(i):
    return (jnp.minimum(i, NT_P - 1), 0)


def _row_s(i):
    return (jnp.maximum(i - NT_P, 0), 0)


def _tok_specs(x, width):
    if isinstance(x, tuple):
        return [pl.BlockSpec((TM, width), _row_p), pl.BlockSpec((TM, width), _row_s)], list(x)
    return [pl.BlockSpec((TM, width), _row)], [x]


def _tok_read(refs, split):
    if split:
        return jnp.where(pl.program_id(0) < NT_P, refs[0][...], refs[1][...]), refs[2:]
    return refs[0][...], refs[1:]


def _tok_write(p_ref, s_ref, value):
    i = pl.program_id(0)

    @pl.when(i < NT_P)
    def _():
        p_ref[...] = value

    @pl.when(i >= NT_P)
    def _():
        s_ref[...] = value.astype(s_ref.dtype)


def _split_out(width, sample_dtype=F32):
    shapes = [jax.ShapeDtypeStruct((TOK_P, width), F32), jax.ShapeDtypeStruct((TOK_S, width), sample_dtype)]
    specs = [pl.BlockSpec((TM, width), _row_p), pl.BlockSpec((TM, width), _row_s)]
    return shapes, specs


def _dot(a, b):
    return jnp.dot(a.astype(BF16), b.astype(BF16), preferred_element_type=F32)


def _dot_nt(a, b):
    return lax.dot_general(a, b, (((1,), (1,)), ((), ())), preferred_element_type=F32)


def _split(x):
    hi = x.astype(BF16)
    lo = (x - hi.astype(F32)).astype(BF16)
    return hi, lo


def _dot3(a, b):
    ah, al = _split(a)
    bh, bl = _split(b)
    d = functools.partial(jnp.dot, preferred_element_type=F32)
    return d(ah, bh) + d(ah, bl) + d(al, bh)


def _rmsnorm(x, g):
    return x * lax.rsqrt(jnp.mean(x * x, axis=-1, keepdims=True) + EPS) * g


def _modulate(y, g, shift, scale):
    return _rmsnorm(y, g) * (1.0 + scale) + shift


def _pair_swap(x):
    n = x.shape[-1]
    lane = lax.broadcasted_iota(jnp.int32, x.shape, x.ndim - 1)
    return jnp.where((lane & 1) == 0, pltpu.roll(x, n - 1, x.ndim - 1), pltpu.roll(x, 1, x.ndim - 1))


def _rope(x, cos, sin_signed):
    return x * cos + _pair_swap(x) * sin_signed


def _rope_angles():
    n_freq = MLA_ROPE // 4
    inv = 1.0 / (ROPE_BASE ** (np.arange(n_freq, dtype=np.float64) / n_freq))
    pos = np.arange(L_S)
    row = (pos // GRID_W).astype(np.float64)
    col = (pos % GRID_W).astype(np.float64)
    ang = np.concatenate([row[:, None] * inv, col[:, None] * inv], axis=-1)
    return np.cos(ang), np.sin(ang)


@functools.lru_cache(maxsize=None)
def _rope_tables(width, starts):
    cos, sin = _rope_angles()
    c = np.ones((TM + L_S, width), np.float32)
    s = np.zeros((TM + L_S, width), np.float32)
    sign = np.where(np.arange(MLA_ROPE) % 2 == 0, -1.0, 1.0)
    unit_c = np.repeat(cos, 2, axis=1)
    unit_s = np.repeat(sin, 2, axis=1) * sign
    for st in starts:
        c[TM:, st:st + MLA_ROPE] = unit_c
        s[TM:, st:st + MLA_ROPE] = unit_s
    return c, s


@functools.lru_cache(maxsize=None)
def _dft_tables(L):
    f = np.arange(L)[:, None]
    s = np.arange(L)[None, :]
    ang = np.pi * ((f * s) % (2 * L)).astype(np.float64) / L
    cs = np.concatenate([np.cos(ang), np.sin(ang)], axis=0)
    cs[L, :] = np.where(np.arange(L) % 2 == 0, 1.0, -1.0)
    cs = cs.astype(np.float32)
    return cs, np.ascontiguousarray(cs.T)


@functools.lru_cache(maxsize=None)
def _hyena_features(L):
    t = np.arange(L, dtype=np.float64) / L
    bands = np.arange(1, HY_BANDS + 1, dtype=np.float64)
    ang = 2.0 * math.pi * t[:, None] * bands
    feat = np.zeros((L, LANES), np.float32)
    feat[:, 0] = t
    feat[:, 1:1 + HY_BANDS] = np.cos(ang)
    feat[:, 1 + HY_BANDS:HY_EMB] = np.sin(ang)
    return feat


def _adaln_kernel(c_ref, w_ref, b_ref, o_ref):
    w = w_ref[0]
    o_ref[0] = jnp.zeros(o_ref.shape[1:], F32)
    for m in range(c_ref.shape[0]):
        col = jax.nn.silu(c_ref[m])
        o_ref[0, m:m + 1, :] = jnp.sum(w * col, axis=0, keepdims=True) + b_ref[0]


def _adaln(cvecs, ada_w, ada_b):
    depth = ada_w.shape[0]
    n_vec = cvecs.shape[0]
    tn = D
    out = pl.pallas_call(
        _adaln_kernel,
        out_shape=jax.ShapeDtypeStruct((depth, 8, N_MOD * D), F32),
        grid=(depth, N_MOD * D // tn),
        in_specs=[pl.BlockSpec((n_vec, D, 1), lambda l, j: (0, 0, 0)),
                  pl.BlockSpec((1, D, tn), lambda l, j: (l, 0, j)),
                  pl.BlockSpec((1, 1, tn), lambda l, j: (l, 0, j))],
        out_specs=pl.BlockSpec((1, 8, tn), lambda l, j: (l, 0, j)),
        compiler_params=_params(2),
        name="adaln",
    )(cvecs[:, :, None], ada_w, ada_b[:, None, :])
    return out[:, :n_vec].reshape(depth, n_vec, N_MOD, D)


FF_PIECE = 256
FF_LOADS = FF // FF_PIECE


def _ffn_kernel(base, final, split_in, layer, which, *refs):
    y, refs = _tok_read(refs, split_in)
    mod_ref, g_ref, win_hbm, wout_hbm, fg_ref = refs[:5]
    n_out = 2 if final else 1
    outs = refs[5:5 + n_out]
    win_ref, wout_ref, stage_g, stage_u, stage_o, sems = refs[5 + n_out:]
    mod = mod_ref[0]
    h = _modulate(y, g_ref[...], mod[base:base + 1], mod[base + 1:base + 2]).astype(BF16)

    def hidden(lo, width):
        gate = jnp.dot(h, win_ref[:, lo:lo + width], preferred_element_type=F32)
        up = jnp.dot(h, win_ref[:, FF + lo:FF + lo + width], preferred_element_type=F32)
        a = (jax.nn.silu(gate) * up).astype(BF16)
        return jnp.dot(a, wout_ref[lo:lo + width, :], preferred_element_type=F32)

    def finish(acc):
        out = y + 0.5 * mod[base + 2:base + 3] * acc
        if final:
            _tok_write(outs[0], outs[1], _rmsnorm(out, fg_ref[...]))
        else:
            outs[0][...] = out

    @pl.when(pl.program_id(0) == 0)
    def _():
        def copies(c, slot):
            cols = pl.ds(c * FF_PIECE, FF_PIECE)
            return (pltpu.make_async_copy(win_hbm.at[layer, which, :, cols], stage_g.at[slot], sems.at[0, slot]),
                    pltpu.make_async_copy(win_hbm.at[layer, which, :, pl.ds(FF + c * FF_PIECE, FF_PIECE)],
                                          stage_u.at[slot], sems.at[1, slot]),
                    pltpu.make_async_copy(wout_hbm.at[layer, which, cols, :], stage_o.at[slot], sems.at[2, slot]))

        for cp in copies(0, 0):
            cp.start()
        acc = jnp.zeros(y.shape, F32)
        for c in range(FF_LOADS):
            slot = c % 2
            lo = c * FF_PIECE
            if c + 1 < FF_LOADS:
                for cp in copies(c + 1, 1 - slot):
                    cp.start()
            for cp in copies(c, slot):
                cp.wait()
            win_ref[:, lo:lo + FF_PIECE] = stage_g[slot].astype(BF16)
            win_ref[:, FF + lo:FF + lo + FF_PIECE] = stage_u[slot].astype(BF16)
            wout_ref[lo:lo + FF_PIECE, :] = stage_o[slot].astype(BF16)
            acc = acc + hidden(lo, FF_PIECE)
        finish(acc)

    @pl.when(pl.program_id(0) > 0)
    def _():
        finish(hidden(0, FF))


def _half_ffn(y, mods_l, g, ff_w_in, ff_w_out, layer, which, final_g=None):
    final = final_g is not None
    fg = final_g if final else g
    y_specs, y_args = _tok_specs(y, D)
    if final:
        out_shape, out_specs = _split_out(D)
    else:
        out_shape, out_specs = jax.ShapeDtypeStruct((TOK, D), F32), pl.BlockSpec((TM, D), _row)
    return pl.pallas_call(
        functools.partial(_ffn_kernel, 6 * which, final, isinstance(y, tuple), layer, which),
        out_shape=out_shape,
        grid=(NT,),
        in_specs=y_specs + [pl.BlockSpec((1, N_MOD, D), lambda i: (_mod_index(i), 0, 0)),
                            _const_spec((1, D)),
                            pl.BlockSpec(memory_space=pl.ANY),
                            pl.BlockSpec(memory_space=pl.ANY),
                            _const_spec((1, D))],
        out_specs=out_specs,
        scratch_shapes=[pltpu.VMEM((D, 2 * FF), BF16), pltpu.VMEM((FF, D), BF16),
                        pltpu.VMEM((2, D, FF_PIECE), F32), pltpu.VMEM((2, D, FF_PIECE), F32),
                        pltpu.VMEM((2, FF_PIECE, D), F32), pltpu.SemaphoreType.DMA((3, 2))],
        compiler_params=_params(1, VMEM_LIMIT),
        name="half_ffn",
    )(*y_args, mods_l, g[None], ff_w_in, ff_w_out, fg[None])


def _linear_kernel(x_ref, w_ref, o_ref):
    o_ref[...] = _dot(x_ref[...], w_ref[...]).astype(o_ref.dtype)


def _linear(x, w, tm, out_dtype):
    m, k = x.shape
    n = w.shape[1]
    return pl.pallas_call(
        _linear_kernel,
        out_shape=jax.ShapeDtypeStruct((m, n), out_dtype),
        grid=(m // tm,),
        in_specs=[pl.BlockSpec((tm, k), lambda i: (i, 0)), _const_spec((k, n))],
        out_specs=pl.BlockSpec((tm, n), lambda i: (i, 0)),
        compiler_params=_params(1),
        name="linear",
    )(x, w.astype(BF16))


MLA_SCALE = (MLA_NOPE + MLA_ROPE) ** -0.5
QW = MLA_HEADS * LANES
KR_AT = MLA_NOPE
IN_A_PAD = MLA_Q_RANK + MLA_KV_RANK + S5_WIDTH + LANES


def _inproj_a_kernel(y_ref, mod_ref, g_ref, win_ref, qn_ref, wuq_ref, kvn_ref, wk_ref, wv_ref,
                     cq_ref, sq_ref, ck_ref, sk_ref,
                     q_ref, ckv_ref, kru_ref, krr_ref, kn_ref, v_ref, u_ref):
    mod = mod_ref[0]
    h = _modulate(y_ref[...], g_ref[...], mod[3:4], mod[4:5]).astype(BF16)
    p = jnp.dot(h, win_ref[...], preferred_element_type=F32)
    o1 = MLA_Q_RANK
    o2 = o1 + MLA_KV_RANK
    o3 = o2 + S5_WIDTH
    q = _dot(_rmsnorm(p[:, :o1], qn_ref[...]), wuq_ref[...])
    q_ref[...] = (_rope(q, cq_ref[...], sq_ref[...]) * MLA_SCALE).astype(BF16)
    ckv = _rmsnorm(p[:, o1:o2], kvn_ref[...])
    ckv_b = ckv.astype(BF16)
    kn_ref[...] = jnp.dot(ckv_b, wk_ref[...], preferred_element_type=F32).astype(BF16)
    v_ref[...] = jnp.dot(ckv_b, wv_ref[...], preferred_element_type=F32).astype(BF16)
    u_ref[...] = p[:, o2:o3]
    krp = p[:, o3:]
    krr_ref[...] = _rope(krp, ck_ref[...], sk_ref[...]).astype(BF16)

    @pl.when(pl.program_id(0) < NT_P)
    def _():
        ckv_ref[...] = ckv
        kru_ref[...] = krp


def _inproj_a(y, mods_l, g, w_in, q_norm, w_uq, kv_norm, w_ukv):
    o1 = MLA_Q_RANK
    o2 = o1 + MLA_KV_RANK
    o3 = o2 + MLA_ROPE
    kr_cols = jnp.pad(w_in[:, o2:o3], ((0, 0), (KR_AT, LANES - KR_AT - MLA_ROPE)))
    w_ext = jnp.concatenate([w_in[:, :o2], w_in[:, o3:], kr_cols], axis=1).astype(BF16)
    dq = MLA_NOPE + MLA_ROPE
    w_uq_pad = jnp.pad(w_uq.reshape(MLA_Q_RANK, MLA_HEADS, dq),
                       ((0, 0), (0, 0), (0, LANES - dq))).reshape(MLA_Q_RANK, QW).astype(BF16)
    w_kv = w_ukv.reshape(MLA_KV_RANK, MLA_HEADS, MLA_NOPE + MLA_V)
    w_k = jnp.pad(w_kv[:, :, :MLA_NOPE], ((0, 0), (0, 0), (0, LANES - MLA_NOPE))).reshape(MLA_KV_RANK, QW)
    w_v = w_kv[:, :, MLA_NOPE:].reshape(MLA_KV_RANK, MLA_HEADS * MLA_V)
    w_k, w_v = w_k.astype(BF16), w_v.astype(BF16)
    cq, sq = _rope_tables(QW, tuple(h * LANES + MLA_NOPE for h in range(MLA_HEADS)))
    ck, sk = _rope_tables(LANES, (KR_AT,))
    row = _row
    pos = lambda i: (_pos_index(i), 0)
    widths = (QW, MLA_KV_RANK, LANES, LANES, QW, MLA_HEADS * MLA_V, S5_WIDTH)
    prompt_only = (1, 2)
    mxu_only = (0, 3, 4, 5)
    outs = pl.pallas_call(
        _inproj_a_kernel,
        out_shape=[jax.ShapeDtypeStruct((TOK_P if k in prompt_only else TOK, w), BF16 if k in mxu_only else F32)
                   for k, w in enumerate(widths)],
        grid=(NT,),
        in_specs=[pl.BlockSpec((TM, D), row),
                  pl.BlockSpec((1, N_MOD, D), lambda i: (_mod_index(i), 0, 0)),
                  _const_spec((1, D)),
                  _const_spec((D, IN_A_PAD)),
                  _const_spec((1, MLA_Q_RANK)),
                  _const_spec((MLA_Q_RANK, QW)),
                  _const_spec((1, MLA_KV_RANK)),
                  _const_spec((MLA_KV_RANK, QW)),
                  _const_spec((MLA_KV_RANK, MLA_HEADS * MLA_V)),
                  pl.BlockSpec((TM, QW), pos), pl.BlockSpec((TM, QW), pos),
                  pl.BlockSpec((TM, LANES), pos), pl.BlockSpec((TM, LANES), pos)],
        out_specs=[pl.BlockSpec((TM, w), _row_p if k in prompt_only else row)
                   for k, w in enumerate(widths)],
        compiler_params=_params(1, VMEM_LIMIT),
        name="inproj_even",
    )(y, mods_l, g[None], w_ext, q_norm[None], w_uq_pad, kv_norm[None], w_k, w_v,
      jnp.asarray(cq), jnp.asarray(sq), jnp.asarray(ck), jnp.asarray(sk))
    q, ckv, kr_unrot, kr_rot, kn, v, u = outs
    return q, ckv, kr_unrot, kr_rot, kn, v, u, (w_k, w_v)


def _mla_attn_kernel(nseg, q_ref, *refs):
    o_ref = refs[-1]
    tq = q_ref.shape[0]
    lane = lax.broadcasted_iota(jnp.int32, (tq, LANES), 1)
    for pair in range(MLA_HEADS // 2):
        outs = []
        for hh in range(2):
            h = 2 * pair + hh
            hs = slice(h * LANES, (h + 1) * LANES)
            qh = q_ref[:, hs]
            scores = []
            for s in range(nseg):
                kn_ref, kr_ref = refs[3 * s], refs[3 * s + 1]
                kh = (kn_ref[:, hs] + kr_ref[...]).astype(BF16)
                scores.append(_dot_nt(qh, kh))
            m = functools.reduce(jnp.maximum, [jnp.max(s, axis=-1, keepdims=True) for s in scores])
            es = [jnp.exp(s - m) for s in scores]
            l = functools.reduce(jnp.add, [jnp.sum(e, axis=-1, keepdims=True) for e in es])
            o = None
            for s in range(nseg):
                v_ref = refs[3 * s + 2]
                part = _dot(es[s], v_ref[:, pair * LANES:(pair + 1) * LANES])
                o = part if o is None else o + part
            outs.append(o / l)
        o_ref[:, pair * LANES:(pair + 1) * LANES] = jnp.where(lane < MLA_V, outs[0], outs[1])


def _mla_attention(q, kn, kr, v, n_batch, seq, tq, row0, ctx=None):
    qt = seq // tq
    qb0, kb0 = row0 // tq, row0 // seq
    in_specs = [pl.BlockSpec((tq, QW), lambda b, j: (qb0 + b * qt + j, 0))]
    args = [q]
    segs = []
    if ctx is not None:
        segs.append((ctx, PAST, 0))
    segs.append(((kn, kr, v), seq, kb0))
    for (a_kn, a_kr, a_v), ln, off in segs:
        idx = lambda b, j, off=off: (off + b, 0)
        in_specs += [pl.BlockSpec((ln, QW), idx), pl.BlockSpec((ln, LANES), idx),
                     pl.BlockSpec((ln, MLA_HEADS * MLA_V), idx)]
        args += [a_kn, a_kr, a_v]
    return pl.pallas_call(
        functools.partial(_mla_attn_kernel, len(segs)),
        out_shape=jax.ShapeDtypeStruct((n_batch * seq, MLA_HEADS * MLA_V), F32),
        grid=(n_batch, qt),
        in_specs=in_specs,
        out_specs=pl.BlockSpec((tq, MLA_HEADS * MLA_V), lambda b, j: (b * qt + j, 0)),
        compiler_params=_params(2, VMEM_LIMIT),
        name="mla_attention",
    )(*args)


def _cpow(ar, ai, e, nbits):
    rr = jnp.ones_like(ar)
    ri = jnp.zeros_like(ar)
    br, bi = ar, ai
    for k in range(nbits):
        bit = ((e >> k) & 1) == 1
        nr = rr * br - ri * bi
        ni = rr * bi + ri * br
        rr = jnp.where(bit, nr, rr)
        ri = jnp.where(bit, ni, ri)
        if k + 1 < nbits:
            br, bi = br * br - bi * bi, 2.0 * br * bi
    return rr, ri


def _s5_abar_kernel(lr_ref, li_ref, ls_ref, o_ref):
    step = jnp.exp(ls_ref[...])
    lr = jnp.minimum(lr_ref[...], -1e-4)
    li = li_ref[...]
    mag = jnp.exp(lr * step)
    ar = mag * jnp.cos(li * step)
    ai = mag * jnp.sin(li * step)
    den = lr * lr + li * li
    o_ref[0] = ar
    o_ref[1] = ai
    o_ref[2] = ((ar - 1.0) * lr + ai * li) / den
    o_ref[3] = (ai * lr - (ar - 1.0) * li) / den


def _s5_prep_kernel(arow_ref, acol_ref, btr_ref, bti_ref, ctr_ref, cti_ref,
                    wi_ref, ws_ref, wo_ref, ap_ref):
    n2 = 2 * S5_N
    lane_o = lax.broadcasted_iota(jnp.int32, (n2, S5_CW), 1)
    blk_o = lane_o >> 4
    row_o = lax.broadcasted_iota(jnp.int32, (n2, S5_CW), 0)
    lane_k = lax.broadcasted_iota(jnp.int32, (S5_GROUP, S5_CW), 1)
    row_k = lax.broadcasted_iota(jnp.int32, (S5_GROUP, S5_CW), 0)
    lane_b = lax.broadcasted_iota(jnp.int32, (S5_GROUP, n2), 1)
    lane_a = lax.broadcasted_iota(jnp.int32, (1, n2), 1)
    rep = ((lane_k & (S5_GROUP - 1)) == row_k).astype(BF16)

    def tile16(x):
        hi = x.astype(BF16)
        r1 = x - hi.astype(F32)
        mid = r1.astype(BF16)
        lo = (r1 - mid.astype(F32)).astype(BF16)
        d = functools.partial(jnp.dot, preferred_element_type=F32)
        return d(hi, rep) + d(mid, rep) + d(lo, rep)

    intra = [None] * S5_T
    for d in range(2):
        ar, ai, fr, fi = (arow_ref[d, 0, k:k + 1, :] for k in range(4))
        btr, bti = btr_ref[d, 0], bti_ref[d, 0]
        bbr = fr * btr - fi * bti
        bbi = fr * bti + fi * btr
        pws = [(jnp.ones_like(ar), jnp.zeros_like(ar))]
        for _ in range(S5_T):
            pr, pi = pws[-1]
            pws.append((pr * ar - pi * ai, pr * ai + pi * ar))
        for s in range(S5_T):
            pr, pi = pws[S5_T - 1 - s] if d == 0 else pws[s]
            ws_ref[d, 0, s * S5_GROUP:(s + 1) * S5_GROUP, :] = jnp.where(
                lane_b < S5_N, pr * bbr - pi * bbi, pr * bbi + pi * bbr).astype(BF16)

        acol = acol_ref[d, 0]
        arc = jnp.broadcast_to(acol[:, 0:1], (n2, S5_CW))
        aic = jnp.broadcast_to(acol[:, 1:2], (n2, S5_CW))
        ctr, cti = tile16(ctr_ref[d, 0]), tile16(cti_ref[d, 0])
        e_lag = blk_o if d == 0 else (S5_T - 1 - blk_o)
        pqr, pqi = _cpow(arc, aic, e_lag, 4)
        qr = pqr * ctr - pqi * cti
        qi = pqr * cti + pqi * ctr
        wo_ref[d, 0] = jnp.where(row_o < S5_N, qr * arc - qi * aic, -(qr * aic + qi * arc)).astype(BF16)
        q_stack = jnp.where(row_o < S5_N, qr, qi)
        bb_mix = jnp.where(lane_b < S5_N, bbr, -bbi)
        kt = _dot3(bb_mix, q_stack)
        for s in range(S5_T):
            if d == 0:
                blk = jnp.where(lane_k >= S5_GROUP * s, pltpu.roll(kt, S5_GROUP * s, 1), 0.0)
            else:
                blk = jnp.where(lane_k < S5_GROUP * (s + 1),
                                pltpu.roll(kt, (S5_GROUP * (s + 1)) % S5_CW, 1), 0.0)
            intra[s] = blk if intra[s] is None else intra[s] + blk

        pr1, pi1 = pws[S5_T]
        for k in range(6):
            ap_ref[d, 0, k:k + 1, :] = pr1
            ap_ref[d, 0, 8 + k:9 + k, :] = jnp.where(lane_a < S5_N, -pi1, pi1)
            pr1, pi1 = pr1 * pr1 - pi1 * pi1, 2.0 * pr1 * pi1
        ap_ref[d, 0, 6:8, :] = jnp.zeros((2, n2), F32)
        ap_ref[d, 0, 14:16, :] = jnp.zeros((2, n2), F32)
    for s in range(S5_T):
        wi_ref[0, s * S5_GROUP:(s + 1) * S5_GROUP, :] = intra[s].astype(BF16)


def _s5_prep(a_re, a_im, log_step, b_re, b_im, c_re, c_im):
    g, n, n2 = S5_GROUPS, S5_N, 2 * S5_N
    abar = pl.pallas_call(
        _s5_abar_kernel,
        out_shape=jax.ShapeDtypeStruct((4, 2 * g, n), F32),
        grid=(1,),
        in_specs=[_const_spec((2 * g, n)), _const_spec((2 * g, n)), _const_spec((2 * g, 1))],
        out_specs=pl.BlockSpec((4, 2 * g, n), lambda i: (0, 0, 0)),
        compiler_params=_params(1),
        name="s5_abar",
    )(a_re.reshape(2 * g, n), a_im.reshape(2 * g, n), log_step.reshape(2 * g, 1))
    abar = jnp.concatenate([abar, abar], axis=-1).reshape(4, 2, g, n2)
    arow = abar.transpose(1, 2, 0, 3)
    acol = abar[:2].transpose(1, 2, 3, 0)
    bt = lambda b: jnp.concatenate([jnp.swapaxes(b, 2, 3)] * 2, axis=-1)
    ct = lambda c: jnp.concatenate([jnp.swapaxes(c, 2, 3)] * 2, axis=2)
    spec4 = lambda r, c: pl.BlockSpec((2, 1, r, c), lambda i: (0, i, 0, 0))
    return pl.pallas_call(
        _s5_prep_kernel,
        out_shape=[jax.ShapeDtypeStruct((g, S5_CW, S5_CW), BF16),
                   jax.ShapeDtypeStruct((2, g, S5_CW, n2), BF16),
                   jax.ShapeDtypeStruct((2, g, n2, S5_CW), BF16),
                   jax.ShapeDtypeStruct((2, g, 16, n2), F32)],
        grid=(g,),
        in_specs=[spec4(4, n2), spec4(n2, 2),
                  spec4(S5_GROUP, n2), spec4(S5_GROUP, n2), spec4(n2, S5_GROUP), spec4(n2, S5_GROUP)],
        out_specs=[pl.BlockSpec((1, S5_CW, S5_CW), lambda i: (i, 0, 0)),
                   spec4(S5_CW, n2), spec4(n2, S5_CW), spec4(16, n2)],
        compiler_params=_params(1),
        name="s5_prep",
    )(arow, acol, bt(b_re), bt(b_im), ct(c_re), ct(c_im))


def _cmul_rows(x, p1, p2):
    return x * p1 + pltpu.roll(x, S5_N, 1) * p2


S5_OCT = LANES // S5_GROUP
S5_RB_IN = 96
S5_RB_OUT = 48


def _s5_core_kernel(u_ref, wi_ref, ws_ref, wo_ref, ap_ref, h0_ref, d_ref, y_ref, fin_ref,
                    ug_ref, yg_ref, z_ref):
    n2 = 2 * S5_N

    def tok_rows(r0, t, nrows):
        return pl.ds(r0 * S5_T + t, nrows, stride=S5_T)

    def block_transpose(xs):
        n = S5_OCT
        blk = lax.broadcasted_iota(jnp.int32, xs[0].shape, 1) >> 4
        a = [pltpu.roll(x, i * S5_GROUP, 1) if i else x for i, x in enumerate(xs)]
        ys = []
        for d in range(n):
            diag = a[-d % n]
            for b in range(1, n):
                diag = jnp.where(blk == b, a[(b - d) % n], diag)
            ys.append(pltpu.roll(diag, LANES - d * S5_GROUP, 1) if d else diag)
        return ys

    def gather(rb, carry):
        r0 = pl.multiple_of(rb * S5_RB_IN, S5_RB_IN)
        for half in range(2):
            xs = [u_ref[tok_rows(r0, S5_OCT * half + tt, S5_RB_IN), :] for tt in range(S5_OCT)]
            for gl, x in enumerate(block_transpose(xs)):
                ug_ref[gl, pl.ds(r0, S5_RB_IN), half * LANES:(half + 1) * LANES] = x
        return carry

    lax.fori_loop(0, S5_ROWS // S5_RB_IN, gather, 0)

    r = lax.broadcasted_iota(jnp.int32, (S5_ROWS, n2), 0)
    in_p = r < S5_ROWS_P
    rib = jnp.where(in_p, r & (CH_P - 1), (r - S5_ROWS_P) & (CH_S - 1))
    nch = jnp.where(in_p, CH_P, CH_S)

    def one_group(gl, slot):
        ub = ug_ref[gl].astype(BF16)
        y = jnp.dot(ub, wi_ref[gl], preferred_element_type=F32)
        for d in range(2):
            p1, p2 = ap_ref[d, gl, 0:1, :], ap_ref[d, gl, 8:9, :]
            edge = [S5_ROWS_P + CH_S * b + (0 if d == 0 else CH_S - 1) for b in range(NB_S)]
            h0 = [h0_ref[gl, d, b:b + 1, :] for b in range(NB_S)]
            s = jnp.dot(ub, ws_ref[d, gl], preferred_element_type=F32)
            for b in range(NB_S):
                s = s + jnp.where(r == edge[b], _cmul_rows(h0[b], p1, p2), 0.0)
            for k in range(6):
                sh = 1 << k
                if d == 0:
                    t = jnp.where(rib >= sh, pltpu.roll(s, sh, 0), 0.0)
                else:
                    t = jnp.where(rib < nch - sh, pltpu.roll(s, S5_ROWS - sh, 0), 0.0)
                s = s + _cmul_rows(t, ap_ref[d, gl, k:k + 1, :], ap_ref[d, gl, 8 + k:9 + k, :])
            z_ref[slot, d] = s
            first = CH_P - 1 if d == 0 else 0
            fin_ref[gl, d] = z_ref[slot, d, pl.ds(first, NB_P, stride=CH_P), :]
            if d == 0:
                sp = jnp.where(rib >= 1, pltpu.roll(s, 1, 0), 0.0)
            else:
                sp = jnp.where(rib < nch - 1, pltpu.roll(s, S5_ROWS - 1, 0), 0.0)
            for b in range(NB_S):
                sp = jnp.where(r == edge[b], h0[b], sp)
            y = y + jnp.dot(sp.astype(BF16), wo_ref[d, gl], preferred_element_type=F32)
        yg_ref[gl] = y

    def group_pair(gp, carry):
        for slot in range(2):
            one_group(2 * gp + slot, slot)
        return carry

    lax.fori_loop(0, S5_OCT // 2, group_pair, 0)

    def scatter(rb, carry):
        r0 = pl.multiple_of(rb * S5_RB_OUT, S5_RB_OUT)
        for half in range(2):
            ys = [yg_ref[gl, pl.ds(r0, S5_RB_OUT), half * LANES:(half + 1) * LANES] for gl in range(S5_OCT)]
            for tt, acc in enumerate(block_transpose(ys)):
                rows = tok_rows(r0, S5_OCT * half + tt, S5_RB_OUT)
                y_ref[rows, :] = acc + d_ref[...] * u_ref[rows, :]
        return carry

    lax.fori_loop(0, S5_ROWS // S5_RB_OUT, scatter, 0)


def _s5_core(u, prep, h0, d_skip):
    w_intra, w_state, w_out, apow = prep
    g, n2 = S5_GROUPS, 2 * S5_N
    spec4 = lambda r, c: pl.BlockSpec((2, S5_OCT, r, c), lambda i: (0, i, 0, 0))
    slab = pl.BlockSpec((TOK, LANES), lambda i: (0, i))
    return pl.pallas_call(
        _s5_core_kernel,
        out_shape=[jax.ShapeDtypeStruct((TOK, S5_WIDTH), F32),
                   jax.ShapeDtypeStruct((g, 2, NB_P, n2), F32)],
        grid=(g // S5_OCT,),
        in_specs=[slab,
                  pl.BlockSpec((S5_OCT, S5_CW, S5_CW), lambda i: (i, 0, 0)),
                  spec4(S5_CW, n2), spec4(n2, S5_CW), spec4(16, n2),
                  pl.BlockSpec((S5_OCT, 2, 8, n2), lambda i: (i, 0, 0, 0)),
                  pl.BlockSpec((1, LANES), lambda i: (0, i))],
        out_specs=[slab, pl.BlockSpec((S5_OCT, 2, NB_P, n2), lambda i: (i, 0, 0, 0))],
        scratch_shapes=[pltpu.VMEM((S5_OCT, S5_ROWS, S5_CW), F32), pltpu.VMEM((S5_OCT, S5_ROWS, S5_CW), F32),
                        pltpu.VMEM((2, 2, S5_ROWS, n2), F32)],
        compiler_params=_params(1, VMEM_LIMIT),
        name="s5_scan",
    )(u, w_intra, w_state, w_out, apow, h0, d_skip[None])


def _outproj_kernel(glu, split1, split2, y_ref, mod_ref, *refs):
    a1, refs = _tok_read(refs, split1)
    a2, refs = _tok_read(refs, split2)
    w1_ref, w2_ref, wg_ref, o_ref = refs
    if glu:
        a2 = jax.nn.gelu(a2)
        a2 = a2 * jax.nn.sigmoid(_dot(a2, wg_ref[...]))
    out = _dot(a1, w1_ref[...]) + _dot(a2, w2_ref[...])
    o_ref[...] = y_ref[...] + mod_ref[0][5:6] * out


def _outproj(y, mods_l, a1, a2, w_out, w_glu=None):
    glu = w_glu is not None
    k1 = k2 = w_out.shape[0] // 2
    wg = (w_glu if glu else jnp.zeros((8, LANES), F32)).astype(BF16)
    s1, a1_args = _tok_specs(a1, k1)
    s2, a2_args = _tok_specs(a2, k2)
    return pl.pallas_call(
        functools.partial(_outproj_kernel, glu, isinstance(a1, tuple), isinstance(a2, tuple)),
        out_shape=jax.ShapeDtypeStruct((TOK, D), F32),
        grid=(NT,),
        in_specs=[pl.BlockSpec((TM, D), _row),
                  pl.BlockSpec((1, N_MOD, D), lambda i: (_mod_index(i), 0, 0))] + s1 + s2
                 + [_const_spec((k1, D)), _const_spec((k2, D)), _const_spec(wg.shape)],
        out_specs=pl.BlockSpec((TM, D), _row),
        compiler_params=_params(1, VMEM_LIMIT),
        name="outproj",
    )(y, mods_l, *a1_args, *a2_args, w_out[:k1].astype(BF16), w_out[k1:].astype(BF16), wg)


DF_SCALE = DF_DH ** -0.5
DFW = DF_HEADS * 2 * DF_DH
IN_B = 3 * HY_WIDTH + 2 * DFW + DF_HEADS * DF_V


def _inproj_b_kernel(y_ref, mod_ref, g_ref, win_ref, c_ref, s_ref,
                     hy_ref, q_ref, kp_ref, ks_ref, vp_ref, vs_ref):
    mod = mod_ref[0]
    h = _modulate(y_ref[...], g_ref[...], mod[3:4], mod[4:5]).astype(BF16)
    p = jnp.dot(h, win_ref[...], preferred_element_type=F32)
    o1 = 3 * HY_WIDTH
    hy_ref[...] = p[:, :o1]
    q_ref[...] = (_rope(p[:, o1:o1 + DFW], c_ref[...], s_ref[...]) * DF_SCALE).astype(BF16)
    _tok_write(kp_ref, ks_ref, _rope(p[:, o1 + DFW:o1 + 2 * DFW], c_ref[...], s_ref[...]))
    _tok_write(vp_ref, vs_ref, p[:, o1 + 2 * DFW:])


def _inproj_b(y, mods_l, g, w_in):
    cs, sn = _rope_tables(DFW, tuple(range(0, DFW, DF_DH)))
    pos = lambda i: (_pos_index(i), 0)
    k_shapes, k_specs = _split_out(DFW, BF16)
    v_shapes, v_specs = _split_out(DF_HEADS * DF_V, BF16)
    hy_u, q, kp, ks, vp, vs = pl.pallas_call(
        _inproj_b_kernel,
        out_shape=[jax.ShapeDtypeStruct((TOK, 3 * HY_WIDTH), F32), jax.ShapeDtypeStruct((TOK, DFW), BF16)]
                  + k_shapes + v_shapes,
        grid=(NT,),
        in_specs=[pl.BlockSpec((TM, D), _row),
                  pl.BlockSpec((1, N_MOD, D), lambda i: (_mod_index(i), 0, 0)),
                  _const_spec((1, D)), _const_spec((D, IN_B)),
                  pl.BlockSpec((TM, DFW), pos), pl.BlockSpec((TM, DFW), pos)],
        out_specs=[pl.BlockSpec((TM, 3 * HY_WIDTH), _row), pl.BlockSpec((TM, DFW), _row)] + k_specs + v_specs,
        compiler_params=_params(1, VMEM_LIMIT),
        name="inproj_odd",
    )(y, mods_l, g[None], w_in.astype(BF16), jnp.asarray(cs), jnp.asarray(sn))
    return hy_u, q, (kp, ks), (vp, vs)


def _diff_attn_kernel(nseg, lam_init, q_ref, lam_ref, sub_ref, *refs):
    o_ref = refs[-1]
    lp = lam_ref[...]
    lam = (jnp.exp(jnp.sum(lp[0:1] * lp[1:2], axis=-1, keepdims=True))
           - jnp.exp(jnp.sum(lp[2:3] * lp[3:4], axis=-1, keepdims=True)) + lam_init)
    tq = q_ref.shape[0]
    lane = lax.broadcasted_iota(jnp.int32, (tq, LANES), 1)
    for pair in range(DF_HEADS // 2):
        cs = slice(pair * LANES, (pair + 1) * LANES)
        q = q_ref[:, cs]
        ks = [refs[2 * s][:, cs].astype(BF16) for s in range(nseg)]
        vs = [refs[2 * s + 1][:, cs].astype(BF16) for s in range(nseg)]
        outs = []
        for hh in range(2):
            parts = []
            for half in range(2):
                unit = 2 * hh + half
                qm = jnp.where((lane >> 5) == unit, q, jnp.zeros_like(q))
                scores = [_dot_nt(qm, k) for k in ks]
                m = functools.reduce(jnp.maximum, [jnp.max(s, axis=-1, keepdims=True) for s in scores])
                es = [jnp.exp(s - m) for s in scores]
                l = functools.reduce(jnp.add, [jnp.sum(e, axis=-1, keepdims=True) for e in es])
                pv = functools.reduce(jnp.add, [_dot(e, v) for e, v in zip(es, vs)])
                parts.append(pv * (1.0 / l))
            o = parts[0] - lam * parts[1]
            mine = (lane >> 6) == hh
            ms = jnp.sum(jnp.where(mine, o * o, 0.0), axis=-1, keepdims=True) * (1.0 / DF_V)
            outs.append(o * lax.rsqrt(ms + EPS))
        o_ref[:, cs] = jnp.where(lane < DF_V, outs[0], outs[1]) * sub_ref[...] * (1.0 - lam_init)


def _diff_attention(q, k, v, lam_p, subln, lam_init, n_batch, seq, tq, row0, ctx=None):
    qt = seq // tq
    qb0, kb0 = row0 // tq, 0
    in_specs = [pl.BlockSpec((tq, DFW), lambda b, j: (qb0 + b * qt + j, 0)),
                pl.BlockSpec((4, DF_DH), lambda b, j: (0, 0)),
                pl.BlockSpec((1, LANES), lambda b, j: (0, 0))]
    args = [q, lam_p, jnp.concatenate([subln, subln])[None]]
    segs = []
    if ctx is not None:
        segs.append((ctx, PAST, 0))
    segs.append(((k, v), seq, kb0))
    for (a_k, a_v), ln, off in segs:
        idx = lambda b, j, off=off: (off + b, 0)
        in_specs += [pl.BlockSpec((ln, DFW), idx), pl.BlockSpec((ln, DF_HEADS * DF_V), idx)]
        args += [a_k, a_v]
    return pl.pallas_call(
        functools.partial(_diff_attn_kernel, len(segs), lam_init),
        out_shape=jax.ShapeDtypeStruct((n_batch * seq, DF_HEADS * DF_V), F32),
        grid=(n_batch, qt),
        in_specs=in_specs,
        out_specs=pl.BlockSpec((tq, DF_HEADS * DF_V), lambda b, j: (b * qt + j, 0)),
        compiler_params=_params(2, VMEM_LIMIT),
        name="diff_attention",
    )(*args)


def _hy_filter_kernel(feat_ref, w1_ref, b1_ref, w2_ref, b2_ref, fq_ref, w3_ref, dec_ref, o_ref):
    feat = feat_ref[...]
    fq = fq_ref[...]
    h = jnp.sin(fq * (_dot3(feat, w1_ref[...]) + b1_ref[...]))
    h = jnp.sin(fq * (_dot3(h, w2_ref[...]) + b2_ref[...]))
    window = jnp.exp(-feat[:, 0:1] * jnp.abs(dec_ref[...]))
    for j in range(4):
        cs = slice(j * HY_WIDTH, (j + 1) * HY_WIDTH)
        o_ref[:, cs] = _dot3(h, w3_ref[:, cs]) * window


def _hy_spectrum_kernel(L, cs_ref, hf_ref, hb_ref, o_ref):
    row = lax.broadcasted_iota(jnp.int32, (L, HY_WIDTH), 0)
    first = row == 0
    tf = _dot(cs_ref[...], hf_ref[...])
    tb = _dot(cs_ref[...], jnp.where(first, 0.0, hb_ref[...]))
    ka = tf[:L] + tb[:L]
    kb = jnp.where(first, tf[L:] + tb[L:], tf[L:] - tb[L:])
    wv = jnp.where(first, 1.0 / (2 * L), 2.0 / (2 * L))
    o_ref[0, 0] = ka * wv
    o_ref[0, 1] = jnp.where(first, 0.0, kb) * wv
    o_ref[0, 2] = jnp.where(first, kb, ka) * wv


def _hy_conv_kernel(L, cs_ref, ct_ref, kf_ref, v_ref, x1_ref, x2_ref,
                    wv_ref, w1_ref, w2_ref, bias_ref, o_ref):
    row = lax.broadcasted_iota(jnp.int32, v_ref.shape, 0)

    def short(x_ref, w_ref):
        x = x_ref[...]
        prev = jnp.where(row >= 1, pltpu.roll(x, 1, 0), 0.0)
        nxt = jnp.where(row <= L - 2, pltpu.roll(x, L - 1, 0), 0.0)
        return w_ref[0:1] * prev + w_ref[1:2] * x + w_ref[2:3] * nxt

    z = short(v_ref, wv_ref)
    gates = (short(x1_ref, w1_ref), short(x2_ref, w2_ref))
    for n in range(2):
        ab = _dot(cs_ref[...], z)
        a, b = ab[:L], ab[L:]
        ka, kb1, ka2 = kf_ref[n, 0], kf_ref[n, 1], kf_ref[n, 2]
        pq = jnp.concatenate([a * ka - b * kb1, a * kb1 + b * ka2], axis=0)
        conv = _dot(ct_ref[...], pq)
        z = gates[n] * (conv + bias_ref[n:n + 1] * z)
    o_ref[...] = z


def _hyena_spectrum(L, phy):
    conv_w, w1, b1, w2, b2, freq, w3, decay, bias = phy
    feat = jnp.asarray(_hyena_features(L))
    w1p = jnp.pad(w1, ((0, LANES - HY_EMB), (0, 0)))
    filt = pl.pallas_call(
        _hy_filter_kernel,
        out_shape=jax.ShapeDtypeStruct((L, 4 * HY_WIDTH), F32),
        grid=(1,),
        in_specs=[_const_spec((L, LANES)), _const_spec((LANES, HY_FH)), _const_spec((1, HY_FH)),
                  _const_spec((HY_FH, HY_FH)), _const_spec((1, HY_FH)), _const_spec((1, HY_FH)),
                  _const_spec((HY_FH, 4 * HY_WIDTH)), _const_spec((1, HY_WIDTH))],
        out_specs=pl.BlockSpec((L, 4 * HY_WIDTH), lambda i: (0, 0)),
        compiler_params=_params(1, VMEM_LIMIT),
        name="hyena_filter",
    )(feat, w1p, b1[None], w2, b2[None], freq[None], w3, decay[None])
    cs = jnp.asarray(_dft_tables(L)[0]).astype(BF16)
    return pl.pallas_call(
        functools.partial(_hy_spectrum_kernel, L),
        out_shape=jax.ShapeDtypeStruct((2, 3, L, HY_WIDTH), F32),
        grid=(2,),
        in_specs=[_const_spec((2 * L, L)),
                  pl.BlockSpec((L, HY_WIDTH), lambda n: (0, n)),
                  pl.BlockSpec((L, HY_WIDTH), lambda n: (0, 2 + n))],
        out_specs=pl.BlockSpec((1, 3, L, HY_WIDTH), lambda n: (n, 0, 0, 0)),
        compiler_params=_params(1, VMEM_LIMIT),
        name="hyena_spectrum",
    )(cs, filt, filt)


def _hyena_conv(hy_u, spec, phy, n_batch, L, cb, row0):
    conv_w, bias = phy[0], phy[8]
    cs, ct = (jnp.asarray(t).astype(BF16) for t in _dft_tables(L))
    nc = HY_WIDTH // cb
    rb0 = row0 // L
    col = lambda off: (lambda b, c: (0, off * nc + c))
    tok = lambda off: (lambda b, c: (rb0 + b, off * nc + c))
    return pl.pallas_call(
        functools.partial(_hy_conv_kernel, L),
        out_shape=jax.ShapeDtypeStruct((n_batch * L, HY_WIDTH), F32),
        grid=(n_batch, nc),
        in_specs=[_const_spec((2 * L, L)), _const_spec((L, 2 * L)),
                  pl.BlockSpec((2, 3, L, cb), lambda b, c: (0, 0, 0, c)),
                  pl.BlockSpec((L, cb), tok(0)), pl.BlockSpec((L, cb), tok(1)), pl.BlockSpec((L, cb), tok(2)),
                  pl.BlockSpec((3, cb), col(0)), pl.BlockSpec((3, cb), col(1)), pl.BlockSpec((3, cb), col(2)),
                  pl.BlockSpec((2, cb), col(0))],
        out_specs=pl.BlockSpec((L, cb), lambda b, c: (b, c)),
        compiler_params=_params(2, VMEM_LIMIT),
        name="hyena_conv",
    )(cs, ct, spec, hy_u, hy_u, hy_u, conv_w, conv_w, conv_w, bias)


def _even_mixer(y, mods_l, g, pa, ps5, ctx_ckv, ctx_krope, ctx_state):
    w_in, w_out, q_norm, w_uq, kv_norm, w_ukv = pa
    a_re, a_im, log_step, b_re, b_im, c_re, c_im, d_skip, w_glu = ps5
    q, ckv, kr_unrot, kr_rot, kn, v, u, (w_k, w_v) = _inproj_a(y, mods_l, g, w_in, q_norm, w_uq, kv_norm, w_ukv)

    ctx_flat = ctx_ckv.reshape(NB_S * PAST, MLA_KV_RANK)
    ctx_kn = _linear(ctx_flat, w_k, PAST, BF16)
    ctx_v = _linear(ctx_flat, w_v, PAST, BF16)
    ctx_kr = jnp.pad(ctx_krope.reshape(NB_S * PAST, MLA_ROPE),
                     ((0, 0), (KR_AT, LANES - KR_AT - MLA_ROPE))).astype(BF16)
    att_p = _mla_attention(q, kn, kr_rot, v, NB_P, L_P, L_P, 0)
    att_s = _mla_attention(q, kn, kr_rot, v, NB_S, L_S, TM, TOK_P, ctx=(ctx_kn, ctx_kr, ctx_v))

    prep = _s5_prep(a_re, a_im, log_step, b_re, b_im, c_re, c_im)
    h0 = ctx_state.transpose(3, 1, 0, 2, 4).reshape(S5_GROUPS, 2, NB_S, 2 * S5_N)
    h0 = jnp.pad(h0, ((0, 0), (0, 0), (0, 8 - NB_S), (0, 0)))
    s5y, fin = _s5_core(u, prep, h0, d_skip)

    y = _outproj(y, mods_l, (att_p, att_s), s5y, w_out, w_glu)
    new_ckv = ckv.reshape(NB_P, L_P, MLA_KV_RANK)
    new_krope = kr_unrot[:, KR_AT:KR_AT + MLA_ROPE].reshape(NB_P, L_P, MLA_ROPE)
    new_state = fin.reshape(S5_GROUPS, 2, NB_P, 2, S5_N).transpose(2, 1, 3, 0, 4)
    return y, new_ckv, new_krope, new_state


def _odd_mixer(y, mods_l, g, pb, phy, ctx_k, ctx_v, lam_init):
    w_in, w_out, lam_p, subln = pb
    hy_u, q, (k_p, k_s), (v_p, v_s) = _inproj_b(y, mods_l, g, w_in)
    hy_p = _hyena_conv(hy_u, _hyena_spectrum(L_P, phy), phy, NB_P, L_P, HY_WIDTH, 0)
    hy_s = _hyena_conv(hy_u, _hyena_spectrum(L_S, phy), phy, NB_S, L_S, HY_WIDTH // 2, TOK_P)
    ctx = (ctx_k.reshape(NB_S * PAST, DFW), ctx_v.reshape(NB_S * PAST, DF_HEADS * DF_V))
    att_p = _diff_attention(q, k_p, v_p, lam_p, subln, lam_init, NB_P, L_P, L_P, 0)
    att_s = _diff_attention(q, k_s, v_s, lam_p, subln, lam_init, NB_S, L_S, TM // 2, TOK_P, ctx=ctx)
    y = _outproj(y, mods_l, (hy_p, hy_s), (att_p, att_s), w_out)
    new_k = k_p.reshape(NB_P, L_P, DF_HEADS, 2, DF_DH)
    new_v = v_p.reshape(NB_P, L_P, DF_HEADS, DF_V)
    return y, new_k, new_v


def kernel(x_prompt, x_sample, c, c_ctx, cache_mla_ckv, cache_mla_krope, state_s5, cache_diff_k, cache_diff_v, ada_w, ada_b, norm_g, ff_w_in, ff_w_out, w_in_a, w_out_a, mla_q_norm, mla_w_uq, mla_kv_norm, mla_w_ukv, s5_a_re, s5_a_im, s5_log_step, s5_b_re, s5_b_im, s5_c_re, s5_c_im, s5_d, s5_w_glu, w_in_b, w_out_b, hy_conv, hy_w1, hy_b1, hy_w2, hy_b2, hy_freq, hy_w3, hy_decay, hy_bias, df_lambda, df_subln, final_norm):
    depth = ada_w.shape[0]
    y = (x_prompt.reshape(TOK_P, D), x_sample.reshape(TOK_S, D))
    mods = _adaln(jnp.concatenate([c_ctx[None], c], axis=0), ada_w, ada_b)
    new_ckv, new_krope, new_s5, new_dk, new_dv = [], [], [], [], []
    for l in range(depth):
        y = _half_ffn(y, mods[l], norm_g[l, 0], ff_w_in, ff_w_out, l, 0)
        if l % 2 == 0:
            e = l // 2
            pa = (w_in_a[e], w_out_a[e], mla_q_norm[e], mla_w_uq[e], mla_kv_norm[e], mla_w_ukv[e])
            ps5 = (s5_a_re[e], s5_a_im[e], s5_log_step[e], s5_b_re[e], s5_b_im[e],
                   s5_c_re[e], s5_c_im[e], s5_d[e], s5_w_glu[e])
            y, ckv, krope, st = _even_mixer(y, mods[l], norm_g[l, 1], pa, ps5, cache_mla_ckv[:, e],
                                            cache_mla_krope[:, e], state_s5[:, e])
            new_ckv.append(ckv)
            new_krope.append(krope)
            new_s5.append(st)
        else:
            o = l // 2
            lam_init = 0.8 - 0.6 * math.exp(-0.3 * l)
            pb = (w_in_b[o], w_out_b[o], df_lambda[o], df_subln[o])
            phy = (hy_conv[o], hy_w1[o], hy_b1[o], hy_w2[o], hy_b2[o], hy_freq[o],
                   hy_w3[o], hy_decay[o], hy_bias[o])
            y, dk, dv = _odd_mixer(y, mods[l], norm_g[l, 1], pb, phy, cache_diff_k[:, o],
                                   cache_diff_v[:, o], lam_init)
            new_dk.append(dk)
            new_dv.append(dv)
        last = l == depth - 1
        y = _half_ffn(y, mods[l], norm_g[l, 2], ff_w_in, ff_w_out, l, 1,
                      final_g=final_norm if last else None)
    y_prompt = y[0].reshape(NB_P, L_P, D)
    y_sample = y[1].reshape(NB_S, L_S, D)
    return (y_prompt, y_sample, jnp.stack(new_ckv, axis=1), jnp.stack(new_krope, axis=1),
            jnp.stack(new_s5, axis=1), jnp.stack(new_dk, axis=1), jnp.stack(new_dv, axis=1))
```

```python
import functools
import math

import numpy as np
import jax
import jax.numpy as jnp
from jax import lax
from jax.experimental import pallas as pl
from jax.experimental.pallas import tpu as pltpu

F32 = jnp.float32
BF16 = jnp.bfloat16

D = 1024
NB_P, L_P = 16, 256
NB_S, L_S = 2, 1024
PAST = 256
GRID_W = 64
N_MOD = 9
FF = 2816
EPS = 1e-6
ROPE_BASE = 10000.0

MLA_HEADS, MLA_NOPE, MLA_ROPE, MLA_V = 8, 64, 32, 64
MLA_Q_RANK, MLA_KV_RANK = 384, 256
S5_WIDTH, S5_GROUP, S5_N = 512, 16, 64
S5_GROUPS = S5_WIDTH // S5_GROUP
HY_WIDTH, HY_BANDS, HY_FH = 512, 16, 64
HY_EMB = 2 * HY_BANDS + 1
DF_HEADS, DF_DH = 8, 32
DF_V = 2 * DF_DH

TOK_P = NB_P * L_P
TOK_S = NB_S * L_S
TOK = TOK_P + TOK_S
TM = 512
NT = TOK // TM
NT_P = TOK_P // TM
TILES_PER_SAMPLE = L_S // TM

LANES = 128
S5_T = 16
S5_CW = S5_T * S5_GROUP
CH_P = L_P // S5_T
CH_S = L_S // S5_T
S5_ROWS = NB_P * CH_P + NB_S * CH_S
S5_ROWS_P = NB_P * CH_P

VMEM_LIMIT = 56 * 1024 * 1024


def _params(n_grid, vmem=None):
    return pltpu.CompilerParams(dimension_semantics=("arbitrary",) * n_grid,
                                vmem_limit_bytes=vmem)


def _const_spec(shape):
    nd = len(shape)
    return pl.BlockSpec(shape, lambda *_: (0,) * nd, pipeline_mode=pl.Buffered(1))


def _mod_index(i):
    return jnp.where(i < NT_P, 0, 1 + (i - NT_P) // TILES_PER_SAMPLE)


def _pos_index(i):
    return jnp.where(i < NT_P, 0, 1 + (i - NT_P) % TILES_PER_SAMPLE)


def _row(i):
    return (i, 0)


def _row_p(i):
    return (jnp.minimum(i, NT_P - 1), 0)


def _row_s(i):
    return (jnp.maximum(i - NT_P, 0), 0)


def _tok_specs(x, width):
    if isinstance(x, tuple):
        return [pl.BlockSpec((TM, width), _row_p), pl.BlockSpec((TM, width), _row_s)], list(x)
    return [pl.BlockSpec((TM, width), _row)], [x]


def _tok_read(refs, split):
    if split:
        return jnp.where(pl.program_id(0) < NT_P, refs[0][...], refs[1][...]), refs[2:]
    return refs[0][...], refs[1:]


def _tok_write(p_ref, s_ref, value):
    i = pl.program_id(0)

    @pl.when(i < NT_P)
    def _():
        p_ref[...] = value

    @pl.when(i >= NT_P)
    def _():
        s_ref[...] = value.astype(s_ref.dtype)


def _split_out(width, sample_dtype=F32):
    shapes = [jax.ShapeDtypeStruct((TOK_P, width), F32), jax.ShapeDtypeStruct((TOK_S, width), sample_dtype)]
    specs = [pl.BlockSpec((TM, width), _row_p), pl.BlockSpec((TM, width), _row_s)]
    return shapes, specs


def _dot(a, b):
    return jnp.dot(a.astype(BF16), b.astype(BF16), preferred_element_type=F32)


def _dot_nt(a, b):
    return lax.dot_general(a, b, (((1,), (1,)), ((), ())), preferred_element_type=F32)


def _split(x):
    hi = x.astype(BF16)
    lo = (x - hi.astype(F32)).astype(BF16)
    return hi, lo


def _dot3(a, b):
    ah, al = _split(a)
    bh, bl = _split(b)
    d = functools.partial(jnp.dot, preferred_element_type=F32)
    return d(ah, bh) + d(ah, bl) + d(al, bh)


def _rmsnorm(x, g):
    return x * lax.rsqrt(jnp.mean(x * x, axis=-1, keepdims=True) + EPS) * g


def _modulate(y, g, shift, scale):
    return _rmsnorm(y, g) * (1.0 + scale) + shift


def _pair_swap(x):
    n = x.shape[-1]
    lane = lax.broadcasted_iota(jnp.int32, x.shape, x.ndim - 1)
    return jnp.where((lane & 1) == 0, pltpu.roll(x, n - 1, x.ndim - 1), pltpu.roll(x, 1, x.ndim - 1))


def _rope(x, cos, sin_signed):
    return x * cos + _pair_swap(x) * sin_signed


def _rope_angles():
    n_freq = MLA_ROPE // 4
    inv = 1.0 / (ROPE_BASE ** (np.arange(n_freq, dtype=np.float64) / n_freq))
    pos = np.arange(L_S)
    row = (pos // GRID_W).astype(np.float64)
    col = (pos % GRID_W).astype(np.float64)
    ang = np.concatenate([row[:, None] * inv, col[:, None] * inv], axis=-1)
    return np.cos(ang), np.sin(ang)


@functools.lru_cache(maxsize=None)
def _rope_tables(width, starts):
    cos, sin = _rope_angles()
    c = np.ones((TM + L_S, width), np.float32)
    s = np.zeros((TM + L_S, width), np.float32)
    sign = np.where(np.arange(MLA_ROPE) % 2 == 0, -1.0, 1.0)
    unit_c = np.repeat(cos, 2, axis=1)
    unit_s = np.repeat(sin, 2, axis=1) * sign
    for st in starts:
        c[TM:, st:st + MLA_ROPE] = unit_c
        s[TM:, st:st + MLA_ROPE] = unit_s
    return c, s


@functools.lru_cache(maxsize=None)
def _dft_tables(L):
    f = np.arange(L)[:, None]
    s = np.arange(L)[None, :]
    ang = np.pi * ((f * s) % (2 * L)).astype(np.float64) / L
    cs = np.concatenate([np.cos(ang), np.sin(ang)], axis=0)
    cs[L, :] = np.where(np.arange(L) % 2 == 0, 1.0, -1.0)
    cs = cs.astype(np.float32)
    return cs, np.ascontiguousarray(cs.T)


@functools.lru_cache(maxsize=None)
def _hyena_features(L):
    t = np.arange(L, dtype=np.float64) / L
    bands = np.arange(1, HY_BANDS + 1, dtype=np.float64)
    ang = 2.0 * math.pi * t[:, None] * bands
    feat = np.zeros((L, LANES), np.float32)
    feat[:, 0] = t
    feat[:, 1:1 + HY_BANDS] = np.cos(ang)
    feat[:, 1 + HY_BANDS:HY_EMB] = np.sin(ang)
    return feat


def _adaln_kernel(c_ref, w_ref, b_ref, o_ref):
    w = w_ref[0]
    o_ref[0] = jnp.zeros(o_ref.shape[1:], F32)
    for m in range(c_ref.shape[0]):
        col = jax.nn.silu(c_ref[m])
        o_ref[0, m:m + 1, :] = jnp.sum(w * col, axis=0, keepdims=True) + b_ref[0]


def _adaln(cvecs, ada_w, ada_b):
    depth = ada_w.shape[0]
    n_vec = cvecs.shape[0]
    tn = D
    out = pl.pallas_call(
        _adaln_kernel,
        out_shape=jax.ShapeDtypeStruct((depth, 8, N_MOD * D), F32),
        grid=(depth, N_MOD * D // tn),
        in_specs=[pl.BlockSpec((n_vec, D, 1), lambda l, j: (0, 0, 0)),
                  pl.BlockSpec((1, D, tn), lambda l, j: (l, 0, j)),
                  pl.BlockSpec((1, 1, tn), lambda l, j: (l, 0, j))],
        out_specs=pl.BlockSpec((1, 8, tn), lambda l, j: (l, 0, j)),
        compiler_params=_params(2),
        name="adaln",
    )(cvecs[:, :, None], ada_w, ada_b[:, None, :])
    return out[:, :n_vec].reshape(depth, n_vec, N_MOD, D)


FF_PIECE = 256
FF_LOADS = FF // FF_PIECE


def _ffn_kernel(base, final, split_in, mixer, layer, which, *refs):
    y, refs = _tok_read(refs, split_in)
    if mixer is not None:
        a1, refs = _tok_read(refs, mixer[0])
        a2, refs = _tok_read(refs, mixer[1])
        w1_ref, w2_ref, wg_ref = refs[:3]
        refs = refs[3:]
    mod_ref, g_ref, win_hbm, wout_hbm, fg_ref = refs[:5]
    n_out = 2 if final else 1
    outs = refs[5:5 + n_out]
    win_ref, wout_ref, stage_g, stage_u, stage_o, sems = refs[5 + n_out:]
    mod = mod_ref[0]
    if mixer is not None:
        if mixer[2]:
            a2 = jax.nn.gelu(a2)
            a2 = a2 * jax.nn.sigmoid(_dot(a2, wg_ref[...]))
        y = y + mod[5:6] * (_dot(a1, w1_ref[...]) + _dot(a2, w2_ref[...]))
    h = _modulate(y, g_ref[...], mod[base:base + 1], mod[base + 1:base + 2]).astype(BF16)

    def hidden(lo, width):
        gate = jnp.dot(h, win_ref[:, lo:lo + width], preferred_element_type=F32)
        up = jnp.dot(h, win_ref[:, FF + lo:FF + lo + width], preferred_element_type=F32)
        a = (jax.nn.silu(gate) * up).astype(BF16)
        return jnp.dot(a, wout_ref[lo:lo + width, :], preferred_element_type=F32)

    def finish(acc):
        out = y + 0.5 * mod[base + 2:base + 3] * acc
        if final:
            _tok_write(outs[0], outs[1], _rmsnorm(out, fg_ref[...]))
        else:
            outs[0][...] = out

    @pl.when(pl.program_id(0) == 0)
    def _():
        def copies(c, slot):
            cols = pl.ds(c * FF_PIECE, FF_PIECE)
            return (pltpu.make_async_copy(win_hbm.at[layer, which, :, cols], stage_g.at[slot], sems.at[0, slot]),
                    pltpu.make_async_copy(win_hbm.at[layer, which, :, pl.ds(FF + c * FF_PIECE, FF_PIECE)],
                                          stage_u.at[slot], sems.at[1, slot]),
                    pltpu.make_async_copy(wout_hbm.at[layer, which, cols, :], stage_o.at[slot], sems.at[2, slot]))

        for cp in copies(0, 0):
            cp.start()
        acc = jnp.zeros(y.shape, F32)
        for c in range(FF_LOADS):
            slot = c % 2
            lo = c * FF_PIECE
            if c + 1 < FF_LOADS:
                for cp in copies(c + 1, 1 - slot):
                    cp.start()
            for cp in copies(c, slot):
                cp.wait()
            win_ref[:, lo:lo + FF_PIECE] = stage_g[slot].astype(BF16)
            win_ref[:, FF + lo:FF + lo + FF_PIECE] = stage_u[slot].astype(BF16)
            wout_ref[lo:lo + FF_PIECE, :] = stage_o[slot].astype(BF16)
            acc = acc + hidden(lo, FF_PIECE)
        finish(acc)

    @pl.when(pl.program_id(0) > 0)
    def _():
        finish(hidden(0, FF))


def _half_ffn(y, mods_l, g, ff_w_in, ff_w_out, layer, which, final_g=None, mixer=None):
    final = final_g is not None
    fg = final_g if final else g
    y_specs, y_args = _tok_specs(y, D)
    mix_flags = None
    if mixer is not None:
        a1, a2, w_out, w_glu = mixer
        k1 = w_out.shape[0] // 2
        wg = (w_glu if w_glu is not None else jnp.zeros((8, LANES), F32)).astype(BF16)
        s1, a1_args = _tok_specs(a1, k1)
        s2, a2_args = _tok_specs(a2, k1)
        y_specs = y_specs + s1 + s2 + [_const_spec((k1, D)), _const_spec((k1, D)), _const_spec(wg.shape)]
        y_args = y_args + a1_args + a2_args + [w_out[:k1].astype(BF16), w_out[k1:].astype(BF16), wg]
        mix_flags = (isinstance(a1, tuple), isinstance(a2, tuple), w_glu is not None)
    if final:
        out_shape, out_specs = _split_out(D)
    else:
        out_shape, out_specs = jax.ShapeDtypeStruct((TOK, D), F32), pl.BlockSpec((TM, D), _row)
    return pl.pallas_call(
        functools.partial(_ffn_kernel, 6 * which, final, isinstance(y, tuple), mix_flags, layer, which),
        out_shape=out_shape,
        grid=(NT,),
        in_specs=y_specs + [pl.BlockSpec((1, N_MOD, D), lambda i: (_mod_index(i), 0, 0)),
                            _const_spec((1, D)),
                            pl.BlockSpec(memory_space=pl.ANY),
                            pl.BlockSpec(memory_space=pl.ANY),
                            _const_spec((1, D))],
        out_specs=out_specs,
        scratch_shapes=[pltpu.VMEM((D, 2 * FF), BF16), pltpu.VMEM((FF, D), BF16),
                        pltpu.VMEM((2, D, FF_PIECE), F32), pltpu.VMEM((2, D, FF_PIECE), F32),
                        pltpu.VMEM((2, FF_PIECE, D), F32), pltpu.SemaphoreType.DMA((3, 2))],
        compiler_params=_params(1, VMEM_LIMIT),
        name="half_ffn",
    )(*y_args, mods_l, g[None], ff_w_in, ff_w_out, fg[None])


def _linear_kernel(x_ref, w_ref, o_ref):
    o_ref[...] = _dot(x_ref[...], w_ref[...]).astype(o_ref.dtype)


def _linear(x, w, tm, out_dtype):
    m, k = x.shape
    n = w.shape[1]
    return pl.pallas_call(
        _linear_kernel,
        out_shape=jax.ShapeDtypeStruct((m, n), out_dtype),
        grid=(m // tm,),
        in_specs=[pl.BlockSpec((tm, k), lambda i: (i, 0)), _const_spec((k, n))],
        out_specs=pl.BlockSpec((tm, n), lambda i: (i, 0)),
        compiler_params=_params(1),
        name="linear",
    )(x, w.astype(BF16))


MLA_SCALE = (MLA_NOPE + MLA_ROPE) ** -0.5
QW = MLA_HEADS * LANES
KR_AT = MLA_NOPE
IN_A_PAD = MLA_Q_RANK + MLA_KV_RANK + S5_WIDTH + LANES


def _inproj_a_kernel(y_ref, mod_ref, g_ref, win_ref, qn_ref, wuq_ref, kvn_ref, wk_ref, wv_ref,
                     cq_ref, sq_ref, ck_ref, sk_ref,
                     q_ref, ckv_ref, kru_ref, krr_ref, kn_ref, v_ref, u_ref):
    mod = mod_ref[0]
    h = _modulate(y_ref[...], g_ref[...], mod[3:4], mod[4:5]).astype(BF16)
    p = jnp.dot(h, win_ref[...], preferred_element_type=F32)
    o1 = MLA_Q_RANK
    o2 = o1 + MLA_KV_RANK
    o3 = o2 + S5_WIDTH
    q = _dot(_rmsnorm(p[:, :o1], qn_ref[...]), wuq_ref[...])
    q_ref[...] = (_rope(q, cq_ref[...], sq_ref[...]) * MLA_SCALE).astype(BF16)
    ckv = _rmsnorm(p[:, o1:o2], kvn_ref[...])
    ckv_b = ckv.astype(BF16)
    kn_ref[...] = jnp.dot(ckv_b, wk_ref[...], preferred_element_type=F32).astype(BF16)
    v_ref[...] = jnp.dot(ckv_b, wv_ref[...], preferred_element_type=F32).astype(BF16)
    u_ref[...] = p[:, o2:o3]
    krp = p[:, o3:]
    krr_ref[...] = _rope(krp, ck_ref[...], sk_ref[...]).astype(BF16)

    @pl.when(pl.program_id(0) < NT_P)
    def _():
        ckv_ref[...] = ckv
        kru_ref[...] = krp


def _inproj_a(y, mods_l, g, w_in, q_norm, w_uq, kv_norm, w_ukv):
    o1 = MLA_Q_RANK
    o2 = o1 + MLA_KV_RANK
    o3 = o2 + MLA_ROPE
    kr_cols = jnp.pad(w_in[:, o2:o3], ((0, 0), (KR_AT, LANES - KR_AT - MLA_ROPE)))
    w_ext = jnp.concatenate([w_in[:, :o2], w_in[:, o3:], kr_cols], axis=1).astype(BF16)
    dq = MLA_NOPE + MLA_ROPE
    w_uq_pad = jnp.pad(w_uq.reshape(MLA_Q_RANK, MLA_HEADS, dq),
                       ((0, 0), (0, 0), (0, LANES - dq))).reshape(MLA_Q_RANK, QW).astype(BF16)
    w_kv = w_ukv.reshape(MLA_KV_RANK, MLA_HEADS, MLA_NOPE + MLA_V)
    w_k = jnp.pad(w_kv[:, :, :MLA_NOPE], ((0, 0), (0, 0), (0, LANES - MLA_NOPE))).reshape(MLA_KV_RANK, QW)
    w_v = w_kv[:, :, MLA_NOPE:].reshape(MLA_KV_RANK, MLA_HEADS * MLA_V)
    w_k, w_v = w_k.astype(BF16), w_v.astype(BF16)
    cq, sq = _rope_tables(QW, tuple(h * LANES + MLA_NOPE for h in range(MLA_HEADS)))
    ck, sk = _rope_tables(LANES, (KR_AT,))
    row = _row
    pos = lambda i: (_pos_index(i), 0)
    widths = (QW, MLA_KV_RANK, LANES, LANES, QW, MLA_HEADS * MLA_V, S5_WIDTH)
    prompt_only = (1, 2)
    mxu_only = (0, 3, 4, 5)
    outs = pl.pallas_call(
        _inproj_a_kernel,
        out_shape=[jax.ShapeDtypeStruct((TOK_P if k in prompt_only else TOK, w), BF16 if k in mxu_only else F32)
                   for k, w in enumerate(widths)],
        grid=(NT,),
        in_specs=[pl.BlockSpec((TM, D), row),
                  pl.BlockSpec((1, N_MOD, D), lambda i: (_mod_index(i), 0, 0)),
                  _const_spec((1, D)),
                  _const_spec((D, IN_A_PAD)),
                  _const_spec((1, MLA_Q_RANK)),
                  _const_spec((MLA_Q_RANK, QW)),
                  _const_spec((1, MLA_KV_RANK)),
                  _const_spec((MLA_KV_RANK, QW)),
                  _const_spec((MLA_KV_RANK, MLA_HEADS * MLA_V)),
                  pl.BlockSpec((TM, QW), pos), pl.BlockSpec((TM, QW), pos),
                  pl.BlockSpec((TM, LANES), pos), pl.BlockSpec((TM, LANES), pos)],
        out_specs=[pl.BlockSpec((TM, w), _row_p if k in prompt_only else row)
                   for k, w in enumerate(widths)],
        compiler_params=_params(1, VMEM_LIMIT),
        name="inproj_even",
    )(y, mods_l, g[None], w_ext, q_norm[None], w_uq_pad, kv_norm[None], w_k, w_v,
      jnp.asarray(cq), jnp.asarray(sq), jnp.asarray(ck), jnp.asarray(sk))
    q, ckv, kr_unrot, kr_rot, kn, v, u = outs
    return q, ckv, kr_unrot, kr_rot, kn, v, u, (w_k, w_v)


def _mla_attn_kernel(nseg, q_ref, *refs):
    o_ref = refs[-1]
    tq = q_ref.shape[0]
    lane = lax.broadcasted_iota(jnp.int32, (tq, LANES), 1)
    for pair in range(MLA_HEADS // 2):
        outs = []
        for hh in range(2):
            h = 2 * pair + hh
            hs = slice(h * LANES, (h + 1) * LANES)
            qh = q_ref[:, hs]
            scores = []
            for s in range(nseg):
                kn_ref, kr_ref = refs[3 * s], refs[3 * s + 1]
                kh = (kn_ref[:, hs] + kr_ref[...]).astype(BF16)
                scores.append(_dot_nt(qh, kh))
            m = functools.reduce(jnp.maximum, [jnp.max(s, axis=-1, keepdims=True) for s in scores])
            es = [jnp.exp(s - m) for s in scores]
            l = functools.reduce(jnp.add, [jnp.sum(e, axis=-1, keepdims=True) for e in es])
            o = None
            for s in range(nseg):
                v_ref = refs[3 * s + 2]
                part = _dot(es[s], v_ref[:, pair * LANES:(pair + 1) * LANES])
                o = part if o is None else o + part
            outs.append(o / l)
        o_ref[:, pair * LANES:(pair + 1) * LANES] = jnp.where(lane < MLA_V, outs[0], outs[1]).astype(o_ref.dtype)


def _mla_attention(q, kn, kr, v, n_batch, seq, tq, row0, ctx=None):
    qt = seq // tq
    qb0, kb0 = row0 // tq, row0 // seq
    in_specs = [pl.BlockSpec((tq, QW), lambda b, j: (qb0 + b * qt + j, 0))]
    args = [q]
    segs = []
    if ctx is not None:
        segs.append((ctx, PAST, 0))
    segs.append(((kn, kr, v), seq, kb0))
    for (a_kn, a_kr, a_v), ln, off in segs:
        idx = lambda b, j, off=off: (off + b, 0)
        in_specs += [pl.BlockSpec((ln, QW), idx), pl.BlockSpec((ln, LANES), idx),
                     pl.BlockSpec((ln, MLA_HEADS * MLA_V), idx)]
        args += [a_kn, a_kr, a_v]
    return pl.pallas_call(
        functools.partial(_mla_attn_kernel, len(segs)),
        out_shape=jax.ShapeDtypeStruct((n_batch * seq, MLA_HEADS * MLA_V), BF16),
        grid=(n_batch, qt),
        in_specs=in_specs,
        out_specs=pl.BlockSpec((tq, MLA_HEADS * MLA_V), lambda b, j: (b * qt + j, 0)),
        compiler_params=_params(2, VMEM_LIMIT),
        name="mla_attention",
    )(*args)


def _cpow(ar, ai, e, nbits):
    rr = jnp.ones_like(ar)
    ri = jnp.zeros_like(ar)
    br, bi = ar, ai
    for k in range(nbits):
        bit = ((e >> k) & 1) == 1
        nr = rr * br - ri * bi
        ni = rr * bi + ri * br
        rr = jnp.where(bit, nr, rr)
        ri = jnp.where(bit, ni, ri)
        if k + 1 < nbits:
            br, bi = br * br - bi * bi, 2.0 * br * bi
    return rr, ri


def _s5_abar_kernel(lr_ref, li_ref, ls_ref, o_ref):
    step = jnp.exp(ls_ref[...])
    lr = jnp.minimum(lr_ref[...], -1e-4)
    li = li_ref[...]
    mag = jnp.exp(lr * step)
    ar = mag * jnp.cos(li * step)
    ai = mag * jnp.sin(li * step)
    den = lr * lr + li * li
    o_ref[0] = ar
    o_ref[1] = ai
    o_ref[2] = ((ar - 1.0) * lr + ai * li) / den
    o_ref[3] = (ai * lr - (ar - 1.0) * li) / den


def _s5_prep_kernel(arow_ref, acol_ref, btr_ref, bti_ref, ctr_ref, cti_ref,
                    wi_ref, ws_ref, wo_ref, ap_ref):
    n2 = 2 * S5_N
    blk_o = lax.broadcasted_iota(jnp.int32, (S5_N, S5_CW), 1) >> 4
    lane_k = lax.broadcasted_iota(jnp.int32, (S5_GROUP, S5_CW), 1)
    row_k = lax.broadcasted_iota(jnp.int32, (S5_GROUP, S5_CW), 0)
    lane_b = lax.broadcasted_iota(jnp.int32, (S5_GROUP, n2), 1)
    lane_a = lax.broadcasted_iota(jnp.int32, (1, n2), 1)
    rep = ((lane_k & (S5_GROUP - 1)) == row_k).astype(BF16)

    def tile16(x):
        hi = x.astype(BF16)
        r1 = x - hi.astype(F32)
        mid = r1.astype(BF16)
        lo = (r1 - mid.astype(F32)).astype(BF16)
        d = functools.partial(jnp.dot, preferred_element_type=F32)
        return d(hi, rep) + d(mid, rep) + d(lo, rep)

    intra = [None] * S5_T
    for d in range(2):
        ar, ai, fr, fi = (arow_ref[d, 0, k:k + 1, :] for k in range(4))
        btr, bti = btr_ref[d, 0], bti_ref[d, 0]
        bbr = fr * btr - fi * bti
        bbi = fr * bti + fi * btr
        pws = [(jnp.ones_like(ar), jnp.zeros_like(ar))]
        for _ in range(S5_T):
            pr, pi = pws[-1]
            pws.append((pr * ar - pi * ai, pr * ai + pi * ar))
        for s in range(S5_T):
            pr, pi = pws[S5_T - 1 - s] if d == 0 else pws[s]
            ws_ref[d, 0, s * S5_GROUP:(s + 1) * S5_GROUP, :] = jnp.where(
                lane_b < S5_N, pr * bbr - pi * bbi, pr * bbi + pi * bbr).astype(BF16)

        acol = acol_ref[d, 0]
        arc = jnp.broadcast_to(acol[:, 0:1], (S5_N, S5_CW))
        aic = jnp.broadcast_to(acol[:, 1:2], (S5_N, S5_CW))
        ctr, cti = tile16(ctr_ref[d, 0]), tile16(cti_ref[d, 0])
        e_lag = blk_o if d == 0 else (S5_T - 1 - blk_o)
        pqr, pqi = _cpow(arc, aic, e_lag, 4)
        qr = pqr * ctr - pqi * cti
        qi = pqr * cti + pqi * ctr
        wo_ref[d, 0] = jnp.concatenate([qr * arc - qi * aic, -(qr * aic + qi * arc)], axis=0).astype(BF16)
        q_stack = jnp.concatenate([qr, qi], axis=0)
        bb_mix = jnp.where(lane_b < S5_N, bbr, -bbi)
        kt = _dot3(bb_mix, q_stack)
        for s in range(S5_T):
            if d == 0:
                blk = jnp.where(lane_k >= S5_GROUP * s, pltpu.roll(kt, S5_GROUP * s, 1), 0.0)
            else:
                blk = jnp.where(lane_k < S5_GROUP * (s + 1),
                                pltpu.roll(kt, (S5_GROUP * (s + 1)) % S5_CW, 1), 0.0)
            intra[s] = blk if intra[s] is None else intra[s] + blk

        pr1, pi1 = pws[S5_T]
        for k in range(6):
            ap_ref[d, 0, k:k + 1, :] = pr1
            ap_ref[d, 0, 8 + k:9 + k, :] = jnp.where(lane_a < S5_N, -pi1, pi1)
            pr1, pi1 = pr1 * pr1 - pi1 * pi1, 2.0 * pr1 * pi1
        ap_ref[d, 0, 6:8, :] = jnp.zeros((2, n2), F32)
        ap_ref[d, 0, 14:16, :] = jnp.zeros((2, n2), F32)
    for s in range(S5_T):
        wi_ref[0, s * S5_GROUP:(s + 1) * S5_GROUP, :] = intra[s].astype(BF16)


def _s5_prep(a_re, a_im, log_step, b_re, b_im, c_re, c_im):
    g, n, n2 = S5_GROUPS, S5_N, 2 * S5_N
    abar = pl.pallas_call(
        _s5_abar_kernel,
        out_shape=jax.ShapeDtypeStruct((4, 2 * g, n), F32),
        grid=(1,),
        in_specs=[_const_spec((2 * g, n)), _const_spec((2 * g, n)), _const_spec((2 * g, 1))],
        out_specs=pl.BlockSpec((4, 2 * g, n), lambda i: (0, 0, 0)),
        compiler_params=_params(1),
        name="s5_abar",
    )(a_re.reshape(2 * g, n), a_im.reshape(2 * g, n), log_step.reshape(2 * g, 1))
    abar = jnp.concatenate([abar, abar], axis=-1).reshape(4, 2, g, n2)
    arow = abar.transpose(1, 2, 0, 3)
    acol = abar[:2, :, :, :n].transpose(1, 2, 3, 0)
    bt = lambda b: jnp.concatenate([jnp.swapaxes(b, 2, 3)] * 2, axis=-1)
    ct = lambda c: jnp.swapaxes(c, 2, 3)
    spec4 = lambda r, c: pl.BlockSpec((2, 1, r, c), lambda i: (0, i, 0, 0))
    return pl.pallas_call(
        _s5_prep_kernel,
        out_shape=[jax.ShapeDtypeStruct((g, S5_CW, S5_CW), BF16),
                   jax.ShapeDtypeStruct((2, g, S5_CW, n2), BF16),
                   jax.ShapeDtypeStruct((2, g, n2, S5_CW), BF16),
                   jax.ShapeDtypeStruct((2, g, 16, n2), F32)],
        grid=(g,),
        in_specs=[spec4(4, n2), spec4(n, 2),
                  spec4(S5_GROUP, n2), spec4(S5_GROUP, n2), spec4(n, S5_GROUP), spec4(n, S5_GROUP)],
        out_specs=[pl.BlockSpec((1, S5_CW, S5_CW), lambda i: (i, 0, 0)),
                   spec4(S5_CW, n2), spec4(n2, S5_CW), spec4(16, n2)],
        compiler_params=_params(1),
        name="s5_prep",
    )(arow, acol, bt(b_re), bt(b_im), ct(c_re), ct(c_im))


def _cmul_rows(x, p1, p2):
    return x * p1 + pltpu.roll(x, S5_N, 1) * p2


S5_OCT = LANES // S5_GROUP
S5_RB_IN = 96
S5_RB_OUT = 48


def _s5_core_kernel(u_ref, wi_ref, ws_ref, wo_ref, ap_ref, h0_ref, d_ref, y_ref, fin_ref,
                    ug_ref, yg_ref, z_ref):
    n2 = 2 * S5_N

    def tok_rows(r0, t, nrows):
        return pl.ds(r0 * S5_T + t, nrows, stride=S5_T)

    def block_transpose(xs):
        n = S5_OCT
        blk = lax.broadcasted_iota(jnp.int32, xs[0].shape, 1) >> 4
        a = [pltpu.roll(x, i * S5_GROUP, 1) if i else x for i, x in enumerate(xs)]
        ys = []
        for d in range(n):
            diag = a[-d % n]
            for b in range(1, n):
                diag = jnp.where(blk == b, a[(b - d) % n], diag)
            ys.append(pltpu.roll(diag, LANES - d * S5_GROUP, 1) if d else diag)
        return ys

    def gather(rb, carry):
        r0 = pl.multiple_of(rb * S5_RB_IN, S5_RB_IN)
        for half in range(2):
            xs = [u_ref[tok_rows(r0, S5_OCT * half + tt, S5_RB_IN), :] for tt in range(S5_OCT)]
            for gl, x in enumerate(block_transpose(xs)):
                ug_ref[gl, pl.ds(r0, S5_RB_IN), half * LANES:(half + 1) * LANES] = x
        return carry

    lax.fori_loop(0, S5_ROWS // S5_RB_IN, gather, 0)

    r = lax.broadcasted_iota(jnp.int32, (S5_ROWS, n2), 0)
    in_p = r < S5_ROWS_P
    rib = jnp.where(in_p, r & (CH_P - 1), (r - S5_ROWS_P) & (CH_S - 1))
    nch = jnp.where(in_p, CH_P, CH_S)

    def one_group(gl, slot):
        ub = ug_ref[gl].astype(BF16)
        y = jnp.dot(ub, wi_ref[gl], preferred_element_type=F32)
        for d in range(2):
            p1, p2 = ap_ref[d, gl, 0:1, :], ap_ref[d, gl, 8:9, :]
            edge = [S5_ROWS_P + CH_S * b + (0 if d == 0 else CH_S - 1) for b in range(NB_S)]
            h0 = [h0_ref[gl, d, b:b + 1, :] for b in range(NB_S)]
            s = jnp.dot(ub, ws_ref[d, gl], preferred_element_type=F32)
            for b in range(NB_S):
                s = s + jnp.where(r == edge[b], _cmul_rows(h0[b], p1, p2), 0.0)
            for k in range(6):
                sh = 1 << k
                if d == 0:
                    t = jnp.where(rib >= sh, pltpu.roll(s, sh, 0), 0.0)
                else:
                    t = jnp.where(rib < nch - sh, pltpu.roll(s, S5_ROWS - sh, 0), 0.0)
                s = s + _cmul_rows(t, ap_ref[d, gl, k:k + 1, :], ap_ref[d, gl, 8 + k:9 + k, :])
            z_ref[slot, d] = s
            first = CH_P - 1 if d == 0 else 0
            fin_ref[gl, d] = z_ref[slot, d, pl.ds(first, NB_P, stride=CH_P), :]
            if d == 0:
                sp = jnp.where(rib >= 1, pltpu.roll(s, 1, 0), 0.0)
            else:
                sp = jnp.where(rib < nch - 1, pltpu.roll(s, S5_ROWS - 1, 0), 0.0)
            for b in range(NB_S):
                sp = jnp.where(r == edge[b], h0[b], sp)
            y = y + jnp.dot(sp.astype(BF16), wo_ref[d, gl], preferred_element_type=F32)
        yg_ref[gl] = y

    def group_pair(gp, carry):
        for slot in range(2):
            one_group(2 * gp + slot, slot)
        return carry

    lax.fori_loop(0, S5_OCT // 2, group_pair, 0)

    def scatter(rb, carry):
        r0 = pl.multiple_of(rb * S5_RB_OUT, S5_RB_OUT)
        for half in range(2):
            ys = [yg_ref[gl, pl.ds(r0, S5_RB_OUT), half * LANES:(half + 1) * LANES] for gl in range(S5_OCT)]
            for tt, acc in enumerate(block_transpose(ys)):
                rows = tok_rows(r0, S5_OCT * half + tt, S5_RB_OUT)
                y_ref[rows, :] = acc + d_ref[...] * u_ref[rows, :]
        return carry

    lax.fori_loop(0, S5_ROWS // S5_RB_OUT, scatter, 0)


def _s5_core(u, prep, h0, d_skip):
    w_intra, w_state, w_out, apow = prep
    g, n2 = S5_GROUPS, 2 * S5_N
    spec4 = lambda r, c: pl.BlockSpec((2, S5_OCT, r, c), lambda i: (0, i, 0, 0))
    slab = pl.BlockSpec((TOK, LANES), lambda i: (0, i))
    return pl.pallas_call(
        _s5_core_kernel,
        out_shape=[jax.ShapeDtypeStruct((TOK, S5_WIDTH), F32),
                   jax.ShapeDtypeStruct((g, 2, NB_P, n2), F32)],
        grid=(g // S5_OCT,),
        in_specs=[slab,
                  pl.BlockSpec((S5_OCT, S5_CW, S5_CW), lambda i: (i, 0, 0)),
                  spec4(S5_CW, n2), spec4(n2, S5_CW), spec4(16, n2),
                  pl.BlockSpec((S5_OCT, 2, 8, n2), lambda i: (i, 0, 0, 0)),
                  pl.BlockSpec((1, LANES), lambda i: (0, i))],
        out_specs=[slab, pl.BlockSpec((S5_OCT, 2, NB_P, n2), lambda i: (i, 0, 0, 0))],
        scratch_shapes=[pltpu.VMEM((S5_OCT, S5_ROWS, S5_CW), F32), pltpu.VMEM((S5_OCT, S5_ROWS, S5_CW), F32),
                        pltpu.VMEM((2, 2, S5_ROWS, n2), F32)],
        compiler_params=_params(1, VMEM_LIMIT),
        name="s5_scan",
    )(u, w_intra, w_state, w_out, apow, h0, d_skip[None])


DF_SCALE = DF_DH ** -0.5
DFW = DF_HEADS * 2 * DF_DH
IN_B = 3 * HY_WIDTH + 2 * DFW + DF_HEADS * DF_V


def _inproj_b_kernel(y_ref, mod_ref, g_ref, win_ref, c_ref, s_ref,
                     hy_ref, q_ref, kp_ref, ks_ref, vp_ref, vs_ref):
    mod = mod_ref[0]
    h = _modulate(y_ref[...], g_ref[...], mod[3:4], mod[4:5]).astype(BF16)
    p = jnp.dot(h, win_ref[...], preferred_element_type=F32)
    o1 = 3 * HY_WIDTH
    hy_ref[...] = p[:, :o1]
    q_ref[...] = (_rope(p[:, o1:o1 + DFW], c_ref[...], s_ref[...]) * DF_SCALE).astype(BF16)
    _tok_write(kp_ref, ks_ref, _rope(p[:, o1 + DFW:o1 + 2 * DFW], c_ref[...], s_ref[...]))
    _tok_write(vp_ref, vs_ref, p[:, o1 + 2 * DFW:])


def _inproj_b(y, mods_l, g, w_in):
    cs, sn = _rope_tables(DFW, tuple(range(0, DFW, DF_DH)))
    pos = lambda i: (_pos_index(i), 0)
    k_shapes, k_specs = _split_out(DFW, BF16)
    v_shapes, v_specs = _split_out(DF_HEADS * DF_V, BF16)
    hy_u, q, kp, ks, vp, vs = pl.pallas_call(
        _inproj_b_kernel,
        out_shape=[jax.ShapeDtypeStruct((TOK, 3 * HY_WIDTH), F32), jax.ShapeDtypeStruct((TOK, DFW), BF16)]
                  + k_shapes + v_shapes,
        grid=(NT,),
        in_specs=[pl.BlockSpec((TM, D), _row),
                  pl.BlockSpec((1, N_MOD, D), lambda i: (_mod_index(i), 0, 0)),
                  _const_spec((1, D)), _const_spec((D, IN_B)),
                  pl.BlockSpec((TM, DFW), pos), pl.BlockSpec((TM, DFW), pos)],
        out_specs=[pl.BlockSpec((TM, 3 * HY_WIDTH), _row), pl.BlockSpec((TM, DFW), _row)] + k_specs + v_specs,
        compiler_params=_params(1, VMEM_LIMIT),
        name="inproj_odd",
    )(y, mods_l, g[None], w_in.astype(BF16), jnp.asarray(cs), jnp.asarray(sn))
    return hy_u, q, (kp, ks), (vp, vs)


def _diff_attn_kernel(nseg, lam_init, q_ref, lam_ref, sub_ref, *refs):
    o_ref = refs[-1]
    lp = lam_ref[...]
    lam = (jnp.exp(jnp.sum(lp[0:1] * lp[1:2], axis=-1, keepdims=True))
           - jnp.exp(jnp.sum(lp[2:3] * lp[3:4], axis=-1, keepdims=True)) + lam_init)
    tq = q_ref.shape[0]
    lane = lax.broadcasted_iota(jnp.int32, (tq, LANES), 1)
    for pair in range(DF_HEADS // 2):
        cs = slice(pair * LANES, (pair + 1) * LANES)
        q = q_ref[:, cs]
        ks = [refs[2 * s][:, cs].astype(BF16) for s in range(nseg)]
        vs = [refs[2 * s + 1][:, cs].astype(BF16) for s in range(nseg)]
        outs = []
        for hh in range(2):
            parts = []
            for half in range(2):
                unit = 2 * hh + half
                qm = jnp.where((lane >> 5) == unit, q, jnp.zeros_like(q))
                scores = [_dot_nt(qm, k) for k in ks]
                m = functools.reduce(jnp.maximum, [jnp.max(s, axis=-1, keepdims=True) for s in scores])
                es = [jnp.exp(s - m) for s in scores]
                l = functools.reduce(jnp.add, [jnp.sum(e, axis=-1, keepdims=True) for e in es])
                pv = functools.reduce(jnp.add, [_dot(e, v) for e, v in zip(es, vs)])
                parts.append(pv * (1.0 / l))
            o = parts[0] - lam * parts[1]
            mine = (lane >> 6) == hh
            ms = jnp.sum(jnp.where(mine, o * o, 0.0), axis=-1, keepdims=True) * (1.0 / DF_V)
            outs.append(o * lax.rsqrt(ms + EPS))
        o = jnp.where(lane < DF_V, outs[0], outs[1]) * sub_ref[...] * (1.0 - lam_init)
        o_ref[:, cs] = o.astype(o_ref.dtype)


def _diff_attention(q, k, v, lam_p, subln, lam_init, n_batch, seq, tq, row0, ctx=None):
    qt = seq // tq
    qb0, kb0 = row0 // tq, 0
    in_specs = [pl.BlockSpec((tq, DFW), lambda b, j: (qb0 + b * qt + j, 0)),
                pl.BlockSpec((4, DF_DH), lambda b, j: (0, 0)),
                pl.BlockSpec((1, LANES), lambda b, j: (0, 0))]
    args = [q, lam_p, jnp.concatenate([subln, subln])[None]]
    segs = []
    if ctx is not None:
        segs.append((ctx, PAST, 0))
    segs.append(((k, v), seq, kb0))
    for (a_k, a_v), ln, off in segs:
        idx = lambda b, j, off=off: (off + b, 0)
        in_specs += [pl.BlockSpec((ln, DFW), idx), pl.BlockSpec((ln, DF_HEADS * DF_V), idx)]
        args += [a_k, a_v]
    return pl.pallas_call(
        functools.partial(_diff_attn_kernel, len(segs), lam_init),
        out_shape=jax.ShapeDtypeStruct((n_batch * seq, DF_HEADS * DF_V), BF16),
        grid=(n_batch, qt),
        in_specs=in_specs,
        out_specs=pl.BlockSpec((tq, DF_HEADS * DF_V), lambda b, j: (b * qt + j, 0)),
        compiler_params=_params(2, VMEM_LIMIT),
        name="diff_attention",
    )(*args)


def _hy_filter_kernel(feat_ref, w1_ref, b1_ref, w2_ref, b2_ref, fq_ref, w3_ref, dec_ref, o_ref):
    feat = feat_ref[...]
    fq = fq_ref[...]
    h = jnp.sin(fq * (_dot3(feat, w1_ref[...]) + b1_ref[...]))
    h = jnp.sin(fq * (_dot3(h, w2_ref[...]) + b2_ref[...]))
    window = jnp.exp(-feat[:, 0:1] * jnp.abs(dec_ref[...]))
    for j in range(4):
        cs = slice(j * HY_WIDTH, (j + 1) * HY_WIDTH)
        o_ref[:, cs] = _dot3(h, w3_ref[:, cs]) * window


def _hy_spectrum_kernel(L, cs_ref, hf_ref, hb_ref, o_ref):
    row = lax.broadcasted_iota(jnp.int32, (L, HY_WIDTH), 0)
    first = row == 0
    tf = _dot(cs_ref[...], hf_ref[...])
    tb = _dot(cs_ref[...], jnp.where(first, 0.0, hb_ref[...]))
    ka = tf[:L] + tb[:L]
    kb = jnp.where(first, tf[L:] + tb[L:], tf[L:] - tb[L:])
    wv = jnp.where(first, 1.0 / (2 * L), 2.0 / (2 * L))
    o_ref[0, 0] = ka * wv
    o_ref[0, 1] = jnp.where(first, 0.0, kb) * wv
    o_ref[0, 2] = jnp.where(first, kb, ka) * wv


def _hy_conv_kernel(L, cs_ref, ct_ref, kf_ref, v_ref, x1_ref, x2_ref,
                    wv_ref, w1_ref, w2_ref, bias_ref, o_ref):
    row = lax.broadcasted_iota(jnp.int32, v_ref.shape, 0)

    def short(x_ref, w_ref):
        x = x_ref[...]
        prev = jnp.where(row >= 1, pltpu.roll(x, 1, 0), 0.0)
        nxt = jnp.where(row <= L - 2, pltpu.roll(x, L - 1, 0), 0.0)
        return w_ref[0:1] * prev + w_ref[1:2] * x + w_ref[2:3] * nxt

    z = short(v_ref, wv_ref)
    gates = (short(x1_ref, w1_ref), short(x2_ref, w2_ref))
    for n in range(2):
        ab = _dot(cs_ref[...], z)
        a, b = ab[:L], ab[L:]
        ka, kb1, ka2 = kf_ref[n, 0], kf_ref[n, 1], kf_ref[n, 2]
        pq = jnp.concatenate([a * ka - b * kb1, a * kb1 + b * ka2], axis=0)
        conv = _dot(ct_ref[...], pq)
        z = gates[n] * (conv + bias_ref[n:n + 1] * z)
    o_ref[...] = z.astype(o_ref.dtype)


def _hyena_spectrum(L, phy):
    conv_w, w1, b1, w2, b2, freq, w3, decay, bias = phy
    feat = jnp.asarray(_hyena_features(L))
    w1p = jnp.pad(w1, ((0, LANES - HY_EMB), (0, 0)))
    filt = pl.pallas_call(
        _hy_filter_kernel,
        out_shape=jax.ShapeDtypeStruct((L, 4 * HY_WIDTH), F32),
        grid=(1,),
        in_specs=[_const_spec((L, LANES)), _const_spec((LANES, HY_FH)), _const_spec((1, HY_FH)),
                  _const_spec((HY_FH, HY_FH)), _const_spec((1, HY_FH)), _const_spec((1, HY_FH)),
                  _const_spec((HY_FH, 4 * HY_WIDTH)), _const_spec((1, HY_WIDTH))],
        out_specs=pl.BlockSpec((L, 4 * HY_WIDTH), lambda i: (0, 0)),
        compiler_params=_params(1, VMEM_LIMIT),
        name="hyena_filter",
    )(feat, w1p, b1[None], w2, b2[None], freq[None], w3, decay[None])
    cs = jnp.asarray(_dft_tables(L)[0]).astype(BF16)
    return pl.pallas_call(
        functools.partial(_hy_spectrum_kernel, L),
        out_shape=jax.ShapeDtypeStruct((2, 3, L, HY_WIDTH), F32),
        grid=(2,),
        in_specs=[_const_spec((2 * L, L)),
                  pl.BlockSpec((L, HY_WIDTH), lambda n: (0, n)),
                  pl.BlockSpec((L, HY_WIDTH), lambda n: (0, 2 + n))],
        out_specs=pl.BlockSpec((1, 3, L, HY_WIDTH), lambda n: (n, 0, 0, 0)),
        compiler_params=_params(1, VMEM_LIMIT),
        name="hyena_spectrum",
    )(cs, filt, filt)


def _hyena_conv(hy_u, spec, phy, n_batch, L, cb, row0):
    conv_w, bias = phy[0], phy[8]
    cs, ct = (jnp.asarray(t).astype(BF16) for t in _dft_tables(L))
    nc = HY_WIDTH // cb
    rb0 = row0 // L
    col = lambda off: (lambda b, c: (0, off * nc + c))
    tok = lambda off: (lambda b, c: (rb0 + b, off * nc + c))
    return pl.pallas_call(
        functools.partial(_hy_conv_kernel, L),
        out_shape=jax.ShapeDtypeStruct((n_batch * L, HY_WIDTH), BF16),
        grid=(n_batch, nc),
        in_specs=[_const_spec((2 * L, L)), _const_spec((L, 2 * L)),
                  pl.BlockSpec((2, 3, L, cb), lambda b, c: (0, 0, 0, c)),
                  pl.BlockSpec((L, cb), tok(0)), pl.BlockSpec((L, cb), tok(1)), pl.BlockSpec((L, cb), tok(2)),
                  pl.BlockSpec((3, cb), col(0)), pl.BlockSpec((3, cb), col(1)), pl.BlockSpec((3, cb), col(2)),
                  pl.BlockSpec((2, cb), col(0))],
        out_specs=pl.BlockSpec((L, cb), lambda b, c: (b, c)),
        compiler_params=_params(2, VMEM_LIMIT),
        name="hyena_conv",
    )(cs, ct, spec, hy_u, hy_u, hy_u, conv_w, conv_w, conv_w, bias)


def _even_mixer(y, mods_l, g, pa, ps5, ctx_ckv, ctx_krope, ctx_state):
    w_in, w_out, q_norm, w_uq, kv_norm, w_ukv = pa
    a_re, a_im, log_step, b_re, b_im, c_re, c_im, d_skip, w_glu = ps5
    q, ckv, kr_unrot, kr_rot, kn, v, u, (w_k, w_v) = _inproj_a(y, mods_l, g, w_in, q_norm, w_uq, kv_norm, w_ukv)

    ctx_flat = ctx_ckv.reshape(NB_S * PAST, MLA_KV_RANK)
    ctx_kn = _linear(ctx_flat, w_k, PAST, BF16)
    ctx_v = _linear(ctx_flat, w_v, PAST, BF16)
    ctx_kr = jnp.pad(ctx_krope.reshape(NB_S * PAST, MLA_ROPE),
                     ((0, 0), (KR_AT, LANES - KR_AT - MLA_ROPE))).astype(BF16)
    att_p = _mla_attention(q, kn, kr_rot, v, NB_P, L_P, L_P, 0)
    att_s = _mla_attention(q, kn, kr_rot, v, NB_S, L_S, TM, TOK_P, ctx=(ctx_kn, ctx_kr, ctx_v))

    prep = _s5_prep(a_re, a_im, log_step, b_re, b_im, c_re, c_im)
    h0 = ctx_state.transpose(3, 1, 0, 2, 4).reshape(S5_GROUPS, 2, NB_S, 2 * S5_N)
    h0 = jnp.pad(h0, ((0, 0), (0, 0), (0, 8 - NB_S), (0, 0)))
    s5y, fin = _s5_core(u, prep, h0, d_skip)

    mixer = ((att_p, att_s), s5y, w_out, w_glu)
    new_ckv = ckv.reshape(NB_P, L_P, MLA_KV_RANK)
    new_krope = kr_unrot[:, KR_AT:KR_AT + MLA_ROPE].reshape(NB_P, L_P, MLA_ROPE)
    new_state = fin.reshape(S5_GROUPS, 2, NB_P, 2, S5_N).transpose(2, 1, 3, 0, 4)
    return mixer, new_ckv, new_krope, new_state


def _odd_mixer(y, mods_l, g, pb, phy, ctx_k, ctx_v, lam_init):
    w_in, w_out, lam_p, subln = pb
    hy_u, q, (k_p, k_s), (v_p, v_s) = _inproj_b(y, mods_l, g, w_in)
    hy_p = _hyena_conv(hy_u, _hyena_spectrum(L_P, phy), phy, NB_P, L_P, HY_WIDTH, 0)
    hy_s = _hyena_conv(hy_u, _hyena_spectrum(L_S, phy), phy, NB_S, L_S, HY_WIDTH // 2, TOK_P)
    ctx = (ctx_k.reshape(NB_S * PAST, DFW), ctx_v.reshape(NB_S * PAST, DF_HEADS * DF_V))
    att_p = _diff_attention(q, k_p, v_p, lam_p, subln, lam_init, NB_P, L_P, L_P, 0)
    att_s = _diff_attention(q, k_s, v_s, lam_p, subln, lam_init, NB_S, L_S, TM // 2, TOK_P, ctx=ctx)
    mixer = ((hy_p, hy_s), (att_p, att_s), w_out, None)
    new_k = k_p.reshape(NB_P, L_P, DF_HEADS, 2, DF_DH)
    new_v = v_p.reshape(NB_P, L_P, DF_HEADS, DF_V)
    return mixer, new_k, new_v


def kernel(x_prompt, x_sample, c, c_ctx, cache_mla_ckv, cache_mla_krope, state_s5, cache_diff_k, cache_diff_v, ada_w, ada_b, norm_g, ff_w_in, ff_w_out, w_in_a, w_out_a, mla_q_norm, mla_w_uq, mla_kv_norm, mla_w_ukv, s5_a_re, s5_a_im, s5_log_step, s5_b_re, s5_b_im, s5_c_re, s5_c_im, s5_d, s5_w_glu, w_in_b, w_out_b, hy_conv, hy_w1, hy_b1, hy_w2, hy_b2, hy_freq, hy_w3, hy_decay, hy_bias, df_lambda, df_subln, final_norm):
    depth = ada_w.shape[0]
    y = (x_prompt.reshape(TOK_P, D), x_sample.reshape(TOK_S, D))
    mods = _adaln(jnp.concatenate([c_ctx[None], c], axis=0), ada_w, ada_b)
    new_ckv, new_krope, new_s5, new_dk, new_dv = [], [], [], [], []
    for l in range(depth):
        y = _half_ffn(y, mods[l], norm_g[l, 0], ff_w_in, ff_w_out, l, 0)
        if l % 2 == 0:
            e = l // 2
            pa = (w_in_a[e], w_out_a[e], mla_q_norm[e], mla_w_uq[e], mla_kv_norm[e], mla_w_ukv[e])
            ps5 = (s5_a_re[e], s5_a_im[e], s5_log_step[e], s5_b_re[e], s5_b_im[e],
                   s5_c_re[e], s5_c_im[e], s5_d[e], s5_w_glu[e])
            mixer, ckv, krope, st = _even_mixer(y, mods[l], norm_g[l, 1], pa, ps5, cache_mla_ckv[:, e],
                                            cache_mla_krope[:, e], state_s5[:, e])
            new_ckv.append(ckv)
            new_krope.append(krope)
            new_s5.append(st)
        else:
            o = l // 2
            lam_init = 0.8 - 0.6 * math.exp(-0.3 * l)
            pb = (w_in_b[o], w_out_b[o], df_lambda[o], df_subln[o])
            phy = (hy_conv[o], hy_w1[o], hy_b1[o], hy_w2[o], hy_b2[o], hy_freq[o],
                   hy_w3[o], hy_decay[o], hy_bias[o])
            mixer, dk, dv = _odd_mixer(y, mods[l], norm_g[l, 1], pb, phy, cache_diff_k[:, o],
                                   cache_diff_v[:, o], lam_init)
            new_dk.append(dk)
            new_dv.append(dv)
        last = l == depth - 1
        y = _half_ffn(y, mods[l], norm_g[l, 2], ff_w_in, ff_w_out, l, 1,
                      final_g=final_norm if last else None, mixer=mixer)
    y_prompt = y[0].reshape(NB_P, L_P, D)
    y_sample = y[1].reshape(NB_S, L_S, D)
    return (y_prompt, y_sample, jnp.stack(new_ckv, axis=1), jnp.stack(new_krope, axis=1),
            jnp.stack(new_s5, axis=1), jnp.stack(new_dk, axis=1), jnp.stack(new_dv, axis=1))
```

```python
import functools
import math

import numpy as np
import jax
import jax.numpy as jnp
from jax import lax
from jax.experimental import pallas as pl
from jax.experimental.pallas import tpu as pltpu

F32 = jnp.float32
BF16 = jnp.bfloat16

D = 1024
NB_P, L_P = 16, 256
NB_S, L_S = 2, 1024
PAST = 256
GRID_W = 64
N_MOD = 9
FF = 2816
EPS = 1e-6
ROPE_BASE = 10000.0

MLA_HEADS, MLA_NOPE, MLA_ROPE, MLA_V = 8, 64, 32, 64
MLA_Q_RANK, MLA_KV_RANK = 384, 256
S5_WIDTH, S5_GROUP, S5_N = 512, 16, 64
S5_GROUPS = S5_WIDTH // S5_GROUP
HY_WIDTH, HY_BANDS, HY_FH = 512, 16, 64
HY_EMB = 2 * HY_BANDS + 1
DF_HEADS, DF_DH = 8, 32
DF_V = 2 * DF_DH

TOK_P = NB_P * L_P
TOK_S = NB_S * L_S
TOK = TOK_P + TOK_S
TM = 512
NT = TOK // TM
NT_P = TOK_P // TM
TILES_PER_SAMPLE = L_S // TM

LANES = 128
S5_T = 16
S5_CW = S5_T * S5_GROUP
CH_P = L_P // S5_T
CH_S = L_S // S5_T
S5_ROWS = NB_P * CH_P + NB_S * CH_S
S5_ROWS_P = NB_P * CH_P

VMEM_LIMIT = 56 * 1024 * 1024


def _params(n_grid, vmem=None):
    return pltpu.CompilerParams(dimension_semantics=("arbitrary",) * n_grid,
                                vmem_limit_bytes=vmem)


def _const_spec(shape):
    nd = len(shape)
    return pl.BlockSpec(shape, lambda *_: (0,) * nd, pipeline_mode=pl.Buffered(1))


def _mod_index(i):
    return jnp.where(i < NT_P, 0, 1 + (i - NT_P) // TILES_PER_SAMPLE)


def _pos_index(i):
    return jnp.where(i < NT_P, 0, 1 + (i - NT_P) % TILES_PER_SAMPLE)


def _row(i):
    return (i, 0)


def _row_p(i):
    return (jnp.minimum(i, NT_P - 1), 0)


def _row_s(i):
    return (jnp.maximum(i - NT_P, 0), 0)


def _tok_specs(x, width):
    if isinstance(x, tuple):
        return [pl.BlockSpec((TM, width), _row_p), pl.BlockSpec((TM, width), _row_s)], list(x)
    return [pl.BlockSpec((TM, width), _row)], [x]


def _tok_read(refs, split):
    if split:
        return jnp.where(pl.program_id(0) < NT_P, refs[0][...], refs[1][...]), refs[2:]
    return refs[0][...], refs[1:]


def _tok_write(p_ref, s_ref, value):
    i = pl.program_id(0)

    @pl.when(i < NT_P)
    def _():
        p_ref[...] = value

    @pl.when(i >= NT_P)
    def _():
        s_ref[...] = value.astype(s_ref.dtype)


def _split_out(width, sample_dtype=F32):
    shapes = [jax.ShapeDtypeStruct((TOK_P, width), F32), jax.ShapeDtypeStruct((TOK_S, width), sample_dtype)]
    specs = [pl.BlockSpec((TM, width), _row_p), pl.BlockSpec((TM, width), _row_s)]
    return shapes, specs


def _dot(a, b):
    return jnp.dot(a.astype(BF16), b.astype(BF16), preferred_element_type=F32)


def _dot_nt(a, b):
    return lax.dot_general(a, b, (((1,), (1,)), ((), ())), preferred_element_type=F32)


def _split(x):
    hi = x.astype(BF16)
    lo = (x - hi.astype(F32)).astype(BF16)
    return hi, lo


def _dot3(a, b):
    ah, al = _split(a)
    bh, bl = _split(b)
    d = functools.partial(jnp.dot, preferred_element_type=F32)
    return d(ah, bh) + d(ah, bl) + d(al, bh)


def _rmsnorm(x, g):
    return x * lax.rsqrt(jnp.mean(x * x, axis=-1, keepdims=True) + EPS) * g


def _modulate(y, g, shift, scale):
    return _rmsnorm(y, g) * (1.0 + scale) + shift


def _pair_swap(x):
    n = x.shape[-1]
    lane = lax.broadcasted_iota(jnp.int32, x.shape, x.ndim - 1)
    return jnp.where((lane & 1) == 0, pltpu.roll(x, n - 1, x.ndim - 1), pltpu.roll(x, 1, x.ndim - 1))


def _rope(x, cos, sin_signed):
    return x * cos + _pair_swap(x) * sin_signed


def _rope_angles():
    n_freq = MLA_ROPE // 4
    inv = 1.0 / (ROPE_BASE ** (np.arange(n_freq, dtype=np.float64) / n_freq))
    pos = np.arange(L_S)
    row = (pos // GRID_W).astype(np.float64)
    col = (pos % GRID_W).astype(np.float64)
    ang = np.concatenate([row[:, None] * inv, col[:, None] * inv], axis=-1)
    return np.cos(ang), np.sin(ang)


@functools.lru_cache(maxsize=None)
def _rope_tables(width, starts):
    cos, sin = _rope_angles()
    c = np.ones((TM + L_S, width), np.float32)
    s = np.zeros((TM + L_S, width), np.float32)
    sign = np.where(np.arange(MLA_ROPE) % 2 == 0, -1.0, 1.0)
    unit_c = np.repeat(cos, 2, axis=1)
    unit_s = np.repeat(sin, 2, axis=1) * sign
    for st in starts:
        c[TM:, st:st + MLA_ROPE] = unit_c
        s[TM:, st:st + MLA_ROPE] = unit_s
    return c, s


@functools.lru_cache(maxsize=None)
def _dft_tables(L):
    f = np.arange(L)[:, None]
    s = np.arange(L)[None, :]
    ang = np.pi * ((f * s) % (2 * L)).astype(np.float64) / L
    cs = np.concatenate([np.cos(ang), np.sin(ang)], axis=0)
    cs[L, :] = np.where(np.arange(L) % 2 == 0, 1.0, -1.0)
    cs = cs.astype(np.float32)
    return cs, np.ascontiguousarray(cs.T)


@functools.lru_cache(maxsize=None)
def _hyena_features(L):
    t = np.arange(L, dtype=np.float64) / L
    bands = np.arange(1, HY_BANDS + 1, dtype=np.float64)
    ang = 2.0 * math.pi * t[:, None] * bands
    feat = np.zeros((L, LANES), np.float32)
    feat[:, 0] = t
    feat[:, 1:1 + HY_BANDS] = np.cos(ang)
    feat[:, 1 + HY_BANDS:HY_EMB] = np.sin(ang)
    return feat


def _adaln_kernel(c_ref, w_ref, b_ref, o_ref):
    s = jax.nn.silu(c_ref[...])
    s_hi = s.astype(BF16).astype(F32)
    stacked = jnp.concatenate([s_hi, s - s_hi], axis=0).astype(BF16)
    wh, wl = _split(w_ref[0])
    both = jnp.dot(stacked, wh, preferred_element_type=F32)
    rows = c_ref.shape[0]
    o_ref[0] = both[:rows] + both[rows:] + jnp.dot(stacked, wl, preferred_element_type=F32)[:rows] + b_ref[0]


def _adaln(cvecs, ada_w, ada_b):
    depth = ada_w.shape[0]
    n_vec = cvecs.shape[0]
    tn = D
    out = pl.pallas_call(
        _adaln_kernel,
        out_shape=jax.ShapeDtypeStruct((depth, 8, N_MOD * D), F32),
        grid=(depth, N_MOD * D // tn),
        in_specs=[pl.BlockSpec((8, D), lambda l, j: (0, 0)),
                  pl.BlockSpec((1, D, tn), lambda l, j: (l, 0, j)),
                  pl.BlockSpec((1, 1, tn), lambda l, j: (l, 0, j))],
        out_specs=pl.BlockSpec((1, 8, tn), lambda l, j: (l, 0, j)),
        compiler_params=_params(2),
        name="adaln",
    )(jnp.pad(cvecs, ((0, 8 - n_vec), (0, 0))), ada_w, ada_b[:, None, :])
    return out[:, :n_vec].reshape(depth, n_vec, N_MOD, D)


FF_PIECE = 256
FF_LOADS = FF // FF_PIECE


def _ffn_kernel(base, final, split_in, mixer, layer, which, *refs):
    y, refs = _tok_read(refs, split_in)
    if mixer is not None:
        a1, refs = _tok_read(refs, mixer[0])
        a2, refs = _tok_read(refs, mixer[1])
        w1_ref, w2_ref, wg_ref = refs[:3]
        refs = refs[3:]
    mod_ref, g_ref, win_hbm, wout_hbm, fg_ref = refs[:5]
    n_out = 2 if final else 1
    outs = refs[5:5 + n_out]
    win_ref, wout_ref, stage_g, stage_u, stage_o, sems = refs[5 + n_out:]
    mod = mod_ref[0]
    if mixer is not None:
        if mixer[2]:
            a2 = jax.nn.gelu(a2)
            a2 = a2 * jax.nn.sigmoid(_dot(a2, wg_ref[...]))
        y = y + mod[5:6] * (_dot(a1, w1_ref[...]) + _dot(a2, w2_ref[...]))
    h = _modulate(y, g_ref[...], mod[base:base + 1], mod[base + 1:base + 2]).astype(BF16)

    def hidden(lo, width):
        gate = jnp.dot(h, win_ref[:, lo:lo + width], preferred_element_type=F32)
        up = jnp.dot(h, win_ref[:, FF + lo:FF + lo + width], preferred_element_type=F32)
        a = (jax.nn.silu(gate) * up).astype(BF16)
        return jnp.dot(a, wout_ref[lo:lo + width, :], preferred_element_type=F32)

    def finish(acc):
        out = y + 0.5 * mod[base + 2:base + 3] * acc
        if final:
            _tok_write(outs[0], outs[1], _rmsnorm(out, fg_ref[...]))
        else:
            outs[0][...] = out

    @pl.when(pl.program_id(0) == 0)
    def _():
        def copies(c, slot):
            cols = pl.ds(c * FF_PIECE, FF_PIECE)
            return (pltpu.make_async_copy(win_hbm.at[layer, which, :, cols], stage_g.at[slot], sems.at[0, slot]),
                    pltpu.make_async_copy(win_hbm.at[layer, which, :, pl.ds(FF + c * FF_PIECE, FF_PIECE)],
                                          stage_u.at[slot], sems.at[1, slot]),
                    pltpu.make_async_copy(wout_hbm.at[layer, which, cols, :], stage_o.at[slot], sems.at[2, slot]))

        for cp in copies(0, 0):
            cp.start()
        acc = jnp.zeros(y.shape, F32)
        for c in range(FF_LOADS):
            slot = c % 2
            lo = c * FF_PIECE
            if c + 1 < FF_LOADS:
                for cp in copies(c + 1, 1 - slot):
                    cp.start()
            for cp in copies(c, slot):
                cp.wait()
            win_ref[:, lo:lo + FF_PIECE] = stage_g[slot].astype(BF16)
            win_ref[:, FF + lo:FF + lo + FF_PIECE] = stage_u[slot].astype(BF16)
            wout_ref[lo:lo + FF_PIECE, :] = stage_o[slot].astype(BF16)
            acc = acc + hidden(lo, FF_PIECE)
        finish(acc)

    @pl.when(pl.program_id(0) > 0)
    def _():
        finish(hidden(0, FF))


def _half_ffn(y, mods_l, g, ff_w_in, ff_w_out, layer, which, final_g=None, mixer=None):
    final = final_g is not None
    fg = final_g if final else g
    y_specs, y_args = _tok_specs(y, D)
    mix_flags = None
    if mixer is not None:
        a1, a2, w_out, w_glu = mixer
        k1 = w_out.shape[0] // 2
        wg = (w_glu if w_glu is not None else jnp.zeros((8, LANES), F32)).astype(BF16)
        s1, a1_args = _tok_specs(a1, k1)
        s2, a2_args = _tok_specs(a2, k1)
        y_specs = y_specs + s1 + s2 + [_const_spec((k1, D)), _const_spec((k1, D)), _const_spec(wg.shape)]
        y_args = y_args + a1_args + a2_args + [w_out[:k1].astype(BF16), w_out[k1:].astype(BF16), wg]
        mix_flags = (isinstance(a1, tuple), isinstance(a2, tuple), w_glu is not None)
    if final:
        out_shape, out_specs = _split_out(D)
    else:
        out_shape, out_specs = jax.ShapeDtypeStruct((TOK, D), F32), pl.BlockSpec((TM, D), _row)
    return pl.pallas_call(
        functools.partial(_ffn_kernel, 6 * which, final, isinstance(y, tuple), mix_flags, layer, which),
        out_shape=out_shape,
        grid=(NT,),
        in_specs=y_specs + [pl.BlockSpec((1, N_MOD, D), lambda i: (_mod_index(i), 0, 0)),
                            _const_spec((1, D)),
                            pl.BlockSpec(memory_space=pl.ANY),
                            pl.BlockSpec(memory_space=pl.ANY),
                            _const_spec((1, D))],
        out_specs=out_specs,
        scratch_shapes=[pltpu.VMEM((D, 2 * FF), BF16), pltpu.VMEM((FF, D), BF16),
                        pltpu.VMEM((2, D, FF_PIECE), F32), pltpu.VMEM((2, D, FF_PIECE), F32),
                        pltpu.VMEM((2, FF_PIECE, D), F32), pltpu.SemaphoreType.DMA((3, 2))],
        compiler_params=_params(1, VMEM_LIMIT),
        name="half_ffn",
    )(*y_args, mods_l, g[None], ff_w_in, ff_w_out, fg[None])


def _linear_kernel(x_ref, w_ref, o_ref):
    o_ref[...] = _dot(x_ref[...], w_ref[...]).astype(o_ref.dtype)


def _linear(x, w, tm, out_dtype):
    m, k = x.shape
    n = w.shape[1]
    return pl.pallas_call(
        _linear_kernel,
        out_shape=jax.ShapeDtypeStruct((m, n), out_dtype),
        grid=(m // tm,),
        in_specs=[pl.BlockSpec((tm, k), lambda i: (i, 0)), _const_spec((k, n))],
        out_specs=pl.BlockSpec((tm, n), lambda i: (i, 0)),
        compiler_params=_params(1),
        name="linear",
    )(x, w.astype(BF16))


MLA_SCALE = (MLA_NOPE + MLA_ROPE) ** -0.5
QW = MLA_HEADS * LANES
KR_AT = MLA_NOPE
IN_A_PAD = MLA_Q_RANK + MLA_KV_RANK + S5_WIDTH + LANES


def _inproj_a_kernel(y_ref, mod_ref, g_ref, win_ref, qn_ref, wuq_ref, kvn_ref, wk_ref, wv_ref,
                     cq_ref, sq_ref, ck_ref, sk_ref,
                     q_ref, ckv_ref, kru_ref, krr_ref, kn_ref, v_ref, u_ref):
    mod = mod_ref[0]
    h = _modulate(y_ref[...], g_ref[...], mod[3:4], mod[4:5]).astype(BF16)
    p = jnp.dot(h, win_ref[...], preferred_element_type=F32)
    o1 = MLA_Q_RANK
    o2 = o1 + MLA_KV_RANK
    o3 = o2 + S5_WIDTH
    q = _dot(_rmsnorm(p[:, :o1], qn_ref[...]), wuq_ref[...])
    q_ref[...] = (_rope(q, cq_ref[...], sq_ref[...]) * MLA_SCALE).astype(BF16)
    ckv = _rmsnorm(p[:, o1:o2], kvn_ref[...])
    ckv_b = ckv.astype(BF16)
    kn_ref[...] = jnp.dot(ckv_b, wk_ref[...], preferred_element_type=F32).astype(BF16)
    v_ref[...] = jnp.dot(ckv_b, wv_ref[...], preferred_element_type=F32).astype(BF16)
    u_ref[...] = p[:, o2:o3]
    krp = p[:, o3:]
    krr_ref[...] = _rope(krp, ck_ref[...], sk_ref[...]).astype(BF16)

    @pl.when(pl.program_id(0) < NT_P)
    def _():
        ckv_ref[...] = ckv
        kru_ref[...] = krp


def _inproj_a(y, mods_l, g, w_in, q_norm, w_uq, kv_norm, w_ukv):
    o1 = MLA_Q_RANK
    o2 = o1 + MLA_KV_RANK
    o3 = o2 + MLA_ROPE
    kr_cols = jnp.pad(w_in[:, o2:o3], ((0, 0), (KR_AT, LANES - KR_AT - MLA_ROPE)))
    w_ext = jnp.concatenate([w_in[:, :o2], w_in[:, o3:], kr_cols], axis=1).astype(BF16)
    dq = MLA_NOPE + MLA_ROPE
    w_uq_pad = jnp.pad(w_uq.reshape(MLA_Q_RANK, MLA_HEADS, dq),
                       ((0, 0), (0, 0), (0, LANES - dq))).reshape(MLA_Q_RANK, QW).astype(BF16)
    w_kv = w_ukv.reshape(MLA_KV_RANK, MLA_HEADS, MLA_NOPE + MLA_V)
    w_k = jnp.pad(w_kv[:, :, :MLA_NOPE], ((0, 0), (0, 0), (0, LANES - MLA_NOPE))).reshape(MLA_KV_RANK, QW)
    w_v = w_kv[:, :, MLA_NOPE:].reshape(MLA_KV_RANK, MLA_HEADS * MLA_V)
    w_k, w_v = w_k.astype(BF16), w_v.astype(BF16)
    cq, sq = _rope_tables(QW, tuple(h * LANES + MLA_NOPE for h in range(MLA_HEADS)))
    ck, sk = _rope_tables(LANES, (KR_AT,))
    row = _row
    pos = lambda i: (_pos_index(i), 0)
    widths = (QW, MLA_KV_RANK, LANES, LANES, QW, MLA_HEADS * MLA_V, S5_WIDTH)
    prompt_only = (1, 2)
    mxu_only = (0, 3, 4, 5)
    outs = pl.pallas_call(
        _inproj_a_kernel,
        out_shape=[jax.ShapeDtypeStruct((TOK_P if k in prompt_only else TOK, w), BF16 if k in mxu_only else F32)
                   for k, w in enumerate(widths)],
        grid=(NT,),
        in_specs=[pl.BlockSpec((TM, D), row),
                  pl.BlockSpec((1, N_MOD, D), lambda i: (_mod_index(i), 0, 0)),
                  _const_spec((1, D)),
                  _const_spec((D, IN_A_PAD)),
                  _const_spec((1, MLA_Q_RANK)),
                  _const_spec((MLA_Q_RANK, QW)),
                  _const_spec((1, MLA_KV_RANK)),
                  _const_spec((MLA_KV_RANK, QW)),
                  _const_spec((MLA_KV_RANK, MLA_HEADS * MLA_V)),
                  pl.BlockSpec((TM, QW), pos), pl.BlockSpec((TM, QW), pos),
                  pl.BlockSpec((TM, LANES), pos), pl.BlockSpec((TM, LANES), pos)],
        out_specs=[pl.BlockSpec((TM, w), _row_p if k in prompt_only else row)
                   for k, w in enumerate(widths)],
        compiler_params=_params(1, VMEM_LIMIT),
        name="inproj_even",
    )(y, mods_l, g[None], w_ext, q_norm[None], w_uq_pad, kv_norm[None], w_k, w_v,
      jnp.asarray(cq), jnp.asarray(sq), jnp.asarray(ck), jnp.asarray(sk))
    q, ckv, kr_unrot, kr_rot, kn, v, u = outs
    return q, ckv, kr_unrot, kr_rot, kn, v, u, (w_k, w_v)


def _mla_attn_kernel(nseg, q_ref, *refs):
    o_ref = refs[-1]
    tq = q_ref.shape[0]
    lane = lax.broadcasted_iota(jnp.int32, (tq, LANES), 1)
    for pair in range(MLA_HEADS // 2):
        outs = []
        for hh in range(2):
            h = 2 * pair + hh
            hs = slice(h * LANES, (h + 1) * LANES)
            qh = q_ref[:, hs]
            scores = []
            for s in range(nseg):
                kn_ref, kr_ref = refs[3 * s], refs[3 * s + 1]
                kh = (kn_ref[:, hs] + kr_ref[...]).astype(BF16)
                scores.append(_dot_nt(qh, kh))
            m = functools.reduce(jnp.maximum, [jnp.max(s, axis=-1, keepdims=True) for s in scores])
            es = [jnp.exp(s - m) for s in scores]
            l = functools.reduce(jnp.add, [jnp.sum(e, axis=-1, keepdims=True) for e in es])
            o = None
            for s in range(nseg):
                v_ref = refs[3 * s + 2]
                part = _dot(es[s], v_ref[:, pair * LANES:(pair + 1) * LANES])
                o = part if o is None else o + part
            outs.append(o / l)
        o_ref[:, pair * LANES:(pair + 1) * LANES] = jnp.where(lane < MLA_V, outs[0], outs[1]).astype(o_ref.dtype)


def _mla_attention(q, kn, kr, v, n_batch, seq, tq, row0, ctx=None):
    qt = seq // tq
    qb0, kb0 = row0 // tq, row0 // seq
    in_specs = [pl.BlockSpec((tq, QW), lambda b, j: (qb0 + b * qt + j, 0))]
    args = [q]
    segs = []
    if ctx is not None:
        segs.append((ctx, PAST, 0))
    segs.append(((kn, kr, v), seq, kb0))
    for (a_kn, a_kr, a_v), ln, off in segs:
        idx = lambda b, j, off=off: (off + b, 0)
        in_specs += [pl.BlockSpec((ln, QW), idx), pl.BlockSpec((ln, LANES), idx),
                     pl.BlockSpec((ln, MLA_HEADS * MLA_V), idx)]
        args += [a_kn, a_kr, a_v]
    return pl.pallas_call(
        functools.partial(_mla_attn_kernel, len(segs)),
        out_shape=jax.ShapeDtypeStruct((n_batch * seq, MLA_HEADS * MLA_V), BF16),
        grid=(n_batch, qt),
        in_specs=in_specs,
        out_specs=pl.BlockSpec((tq, MLA_HEADS * MLA_V), lambda b, j: (b * qt + j, 0)),
        compiler_params=_params(2, VMEM_LIMIT),
        name="mla_attention",
    )(*args)


def _cpow(ar, ai, e, nbits):
    rr = jnp.ones_like(ar)
    ri = jnp.zeros_like(ar)
    br, bi = ar, ai
    for k in range(nbits):
        bit = ((e >> k) & 1) == 1
        nr = rr * br - ri * bi
        ni = rr * bi + ri * br
        rr = jnp.where(bit, nr, rr)
        ri = jnp.where(bit, ni, ri)
        if k + 1 < nbits:
            br, bi = br * br - bi * bi, 2.0 * br * bi
    return rr, ri


def _s5_abar_kernel(lr_ref, li_ref, ls_ref, o_ref):
    step = jnp.exp(ls_ref[...])
    lr = jnp.minimum(lr_ref[...], -1e-4)
    li = li_ref[...]
    mag = jnp.exp(lr * step)
    ar = mag * jnp.cos(li * step)
    ai = mag * jnp.sin(li * step)
    den = lr * lr + li * li
    o_ref[0] = ar
    o_ref[1] = ai
    o_ref[2] = ((ar - 1.0) * lr + ai * li) / den
    o_ref[3] = (ai * lr - (ar - 1.0) * li) / den


def _s5_prep_kernel(arow_ref, acol_ref, btr_ref, bti_ref, ctr_ref, cti_ref,
                    wi_ref, ws_ref, wo_ref, ap_ref):
    n2 = 2 * S5_N
    blk_o = lax.broadcasted_iota(jnp.int32, (S5_N, S5_CW), 1) >> 4
    lane_k = lax.broadcasted_iota(jnp.int32, (S5_GROUP, S5_CW), 1)
    row_k = lax.broadcasted_iota(jnp.int32, (S5_GROUP, S5_CW), 0)
    lane_b = lax.broadcasted_iota(jnp.int32, (S5_GROUP, n2), 1)
    lane_a = lax.broadcasted_iota(jnp.int32, (1, n2), 1)
    rep = ((lane_k & (S5_GROUP - 1)) == row_k).astype(BF16)

    def tile16(x):
        hi = x.astype(BF16)
        r1 = x - hi.astype(F32)
        mid = r1.astype(BF16)
        lo = (r1 - mid.astype(F32)).astype(BF16)
        d = functools.partial(jnp.dot, preferred_element_type=F32)
        return d(hi, rep) + d(mid, rep) + d(lo, rep)

    intra = [None] * S5_T
    for d in range(2):
        ar, ai, fr, fi = (arow_ref[d, 0, k:k + 1, :] for k in range(4))
        btr, bti = btr_ref[d, 0], bti_ref[d, 0]
        bbr = fr * btr - fi * bti
        bbi = fr * bti + fi * btr
        pws = [(jnp.ones_like(ar), jnp.zeros_like(ar))]
        for _ in range(S5_T):
            pr, pi = pws[-1]
            pws.append((pr * ar - pi * ai, pr * ai + pi * ar))
        for s in range(S5_T):
            pr, pi = pws[S5_T - 1 - s] if d == 0 else pws[s]
            ws_ref[d, 0, s * S5_GROUP:(s + 1) * S5_GROUP, :] = jnp.where(
                lane_b < S5_N, pr * bbr - pi * bbi, pr * bbi + pi * bbr).astype(BF16)

        acol = acol_ref[d, 0]
        arc = jnp.broadcast_to(acol[:, 0:1], (S5_N, S5_CW))
        aic = jnp.broadcast_to(acol[:, 1:2], (S5_N, S5_CW))
        ctr, cti = tile16(ctr_ref[d, 0]), tile16(cti_ref[d, 0])
        e_lag = blk_o if d == 0 else (S5_T - 1 - blk_o)
        pqr, pqi = _cpow(arc, aic, e_lag, 4)
        qr = pqr * ctr - pqi * cti
        qi = pqr * cti + pqi * ctr
        wo_ref[d, 0] = jnp.concatenate([qr * arc - qi * aic, -(qr * aic + qi * arc)], axis=0).astype(BF16)
        q_stack = jnp.concatenate([qr, qi], axis=0)
        bb_mix = jnp.where(lane_b < S5_N, bbr, -bbi)
        kt = _dot3(bb_mix, q_stack)
        for s in range(S5_T):
            if d == 0:
                blk = jnp.where(lane_k >= S5_GROUP * s, pltpu.roll(kt, S5_GROUP * s, 1), 0.0)
            else:
                blk = jnp.where(lane_k < S5_GROUP * (s + 1),
                                pltpu.roll(kt, (S5_GROUP * (s + 1)) % S5_CW, 1), 0.0)
            intra[s] = blk if intra[s] is None else intra[s] + blk

        pr1, pi1 = pws[S5_T]
        for k in range(6):
            ap_ref[d, 0, k:k + 1, :] = pr1
            ap_ref[d, 0, 8 + k:9 + k, :] = jnp.where(lane_a < S5_N, -pi1, pi1)
            pr1, pi1 = pr1 * pr1 - pi1 * pi1, 2.0 * pr1 * pi1
        ap_ref[d, 0, 6:8, :] = jnp.zeros((2, n2), F32)
        ap_ref[d, 0, 14:16, :] = jnp.zeros((2, n2), F32)
    for s in range(S5_T):
        wi_ref[0, s * S5_GROUP:(s + 1) * S5_GROUP, :] = intra[s].astype(BF16)


def _s5_prep(a_re, a_im, log_step, b_re, b_im, c_re, c_im):
    g, n, n2 = S5_GROUPS, S5_N, 2 * S5_N
    abar = pl.pallas_call(
        _s5_abar_kernel,
        out_shape=jax.ShapeDtypeStruct((4, 2 * g, n), F32),
        grid=(1,),
        in_specs=[_const_spec((2 * g, n)), _const_spec((2 * g, n)), _const_spec((2 * g, 1))],
        out_specs=pl.BlockSpec((4, 2 * g, n), lambda i: (0, 0, 0)),
        compiler_params=_params(1),
        name="s5_abar",
    )(a_re.reshape(2 * g, n), a_im.reshape(2 * g, n), log_step.reshape(2 * g, 1))
    abar = jnp.concatenate([abar, abar], axis=-1).reshape(4, 2, g, n2)
    arow = abar.transpose(1, 2, 0, 3)
    acol = abar[:2, :, :, :n].transpose(1, 2, 3, 0)
    bt = lambda b: jnp.concatenate([jnp.swapaxes(b, 2, 3)] * 2, axis=-1)
    ct = lambda c: jnp.swapaxes(c, 2, 3)
    spec4 = lambda r, c: pl.BlockSpec((2, 1, r, c), lambda i: (0, i, 0, 0))
    return pl.pallas_call(
        _s5_prep_kernel,
        out_shape=[jax.ShapeDtypeStruct((g, S5_CW, S5_CW), BF16),
                   jax.ShapeDtypeStruct((2, g, S5_CW, n2), BF16),
                   jax.ShapeDtypeStruct((2, g, n2, S5_CW), BF16),
                   jax.ShapeDtypeStruct((2, g, 16, n2), F32)],
        grid=(g,),
        in_specs=[spec4(4, n2), spec4(n, 2),
                  spec4(S5_GROUP, n2), spec4(S5_GROUP, n2), spec4(n, S5_GROUP), spec4(n, S5_GROUP)],
        out_specs=[pl.BlockSpec((1, S5_CW, S5_CW), lambda i: (i, 0, 0)),
                   spec4(S5_CW, n2), spec4(n2, S5_CW), spec4(16, n2)],
        compiler_params=_params(1),
        name="s5_prep",
    )(arow, acol, bt(b_re), bt(b_im), ct(c_re), ct(c_im))


def _cmul_rows(x, p1, p2):
    return x * p1 + pltpu.roll(x, S5_N, 1) * p2


S5_OCT = LANES // S5_GROUP
S5_RB_IN = 96
S5_RB_OUT = 48


def _s5_core_kernel(u_ref, wi_ref, ws_ref, wo_ref, ap_ref, h0_ref, d_ref, y_ref, fin_ref,
                    ug_ref, yg_ref, z_ref):
    n2 = 2 * S5_N

    def tok_rows(r0, t, nrows):
        return pl.ds(r0 * S5_T + t, nrows, stride=S5_T)

    def block_transpose(xs):
        n = S5_OCT
        blk = lax.broadcasted_iota(jnp.int32, xs[0].shape, 1) >> 4
        a = [pltpu.roll(x, i * S5_GROUP, 1) if i else x for i, x in enumerate(xs)]
        ys = []
        for d in range(n):
            diag = a[-d % n]
            for b in range(1, n):
                diag = jnp.where(blk == b, a[(b - d) % n], diag)
            ys.append(pltpu.roll(diag, LANES - d * S5_GROUP, 1) if d else diag)
        return ys

    def gather(rb, carry):
        r0 = pl.multiple_of(rb * S5_RB_IN, S5_RB_IN)
        for half in range(2):
            xs = [u_ref[tok_rows(r0, S5_OCT * half + tt, S5_RB_IN), :] for tt in range(S5_OCT)]
            for gl, x in enumerate(block_transpose(xs)):
                ug_ref[gl, pl.ds(r0, S5_RB_IN), half * LANES:(half + 1) * LANES] = x
        return carry

    lax.fori_loop(0, S5_ROWS // S5_RB_IN, gather, 0)

    r = lax.broadcasted_iota(jnp.int32, (S5_ROWS, n2), 0)
    in_p = r < S5_ROWS_P
    rib = jnp.where(in_p, r & (CH_P - 1), (r - S5_ROWS_P) & (CH_S - 1))
    nch = jnp.where(in_p, CH_P, CH_S)

    def one_group(gl, slot):
        ub = ug_ref[gl].astype(BF16)
        y = jnp.dot(ub, wi_ref[gl], preferred_element_type=F32)
        for d in range(2):
            p1, p2 = ap_ref[d, gl, 0:1, :], ap_ref[d, gl, 8:9, :]
            edge = [S5_ROWS_P + CH_S * b + (0 if d == 0 else CH_S - 1) for b in range(NB_S)]
            h0 = [h0_ref[gl, d, b:b + 1, :] for b in range(NB_S)]
            s = jnp.dot(ub, ws_ref[d, gl], preferred_element_type=F32)
            for b in range(NB_S):
                s = s + jnp.where(r == edge[b], _cmul_rows(h0[b], p1, p2), 0.0)
            for k in range(6):
                sh = 1 << k
                if d == 0:
                    t = jnp.where(rib >= sh, pltpu.roll(s, sh, 0), 0.0)
                else:
                    t = jnp.where(rib < nch - sh, pltpu.roll(s, S5_ROWS - sh, 0), 0.0)
                s = s + _cmul_rows(t, ap_ref[d, gl, k:k + 1, :], ap_ref[d, gl, 8 + k:9 + k, :])
            z_ref[slot, d] = s
            first = CH_P - 1 if d == 0 else 0
            fin_ref[gl, d] = z_ref[slot, d, pl.ds(first, NB_P, stride=CH_P), :]
            if d == 0:
                sp = jnp.where(rib >= 1, pltpu.roll(s, 1, 0), 0.0)
            else:
                sp = jnp.where(rib < nch - 1, pltpu.roll(s, S5_ROWS - 1, 0), 0.0)
            for b in range(NB_S):
                sp = jnp.where(r == edge[b], h0[b], sp)
            y = y + jnp.dot(sp.astype(BF16), wo_ref[d, gl], preferred_element_type=F32)
        yg_ref[gl] = y

    def group_pair(gp, carry):
        for slot in range(2):
            one_group(2 * gp + slot, slot)
        return carry

    lax.fori_loop(0, S5_OCT // 2, group_pair, 0)

    def scatter(rb, carry):
        r0 = pl.multiple_of(rb * S5_RB_OUT, S5_RB_OUT)
        for half in range(2):
            ys = [yg_ref[gl, pl.ds(r0, S5_RB_OUT), half * LANES:(half + 1) * LANES] for gl in range(S5_OCT)]
            for tt, acc in enumerate(block_transpose(ys)):
                rows = tok_rows(r0, S5_OCT * half + tt, S5_RB_OUT)
                y_ref[rows, :] = acc + d_ref[...] * u_ref[rows, :]
        return carry

    lax.fori_loop(0, S5_ROWS // S5_RB_OUT, scatter, 0)


def _s5_core(u, prep, h0, d_skip):
    w_intra, w_state, w_out, apow = prep
    g, n2 = S5_GROUPS, 2 * S5_N
    spec4 = lambda r, c: pl.BlockSpec((2, S5_OCT, r, c), lambda i: (0, i, 0, 0))
    slab = pl.BlockSpec((TOK, LANES), lambda i: (0, i))
    return pl.pallas_call(
        _s5_core_kernel,
        out_shape=[jax.ShapeDtypeStruct((TOK, S5_WIDTH), F32),
                   jax.ShapeDtypeStruct((g, 2, NB_P, n2), F32)],
        grid=(g // S5_OCT,),
        in_specs=[slab,
                  pl.BlockSpec((S5_OCT, S5_CW, S5_CW), lambda i: (i, 0, 0)),
                  spec4(S5_CW, n2), spec4(n2, S5_CW), spec4(16, n2),
                  pl.BlockSpec((S5_OCT, 2, 8, n2), lambda i: (i, 0, 0, 0)),
                  pl.BlockSpec((1, LANES), lambda i: (0, i))],
        out_specs=[slab, pl.BlockSpec((S5_OCT, 2, NB_P, n2), lambda i: (i, 0, 0, 0))],
        scratch_shapes=[pltpu.VMEM((S5_OCT, S5_ROWS, S5_CW), F32), pltpu.VMEM((S5_OCT, S5_ROWS, S5_CW), F32),
                        pltpu.VMEM((2, 2, S5_ROWS, n2), F32)],
        compiler_params=_params(1, VMEM_LIMIT),
        name="s5_scan",
    )(u, w_intra, w_state, w_out, apow, h0, d_skip[None])


DF_SCALE = DF_DH ** -0.5
DFW = DF_HEADS * 2 * DF_DH
IN_B = 3 * HY_WIDTH + 2 * DFW + DF_HEADS * DF_V


def _inproj_b_kernel(y_ref, mod_ref, g_ref, win_ref, c_ref, s_ref,
                     hy_ref, q_ref, kp_ref, ks_ref, vp_ref, vs_ref):
    mod = mod_ref[0]
    h = _modulate(y_ref[...], g_ref[...], mod[3:4], mod[4:5]).astype(BF16)
    p = jnp.dot(h, win_ref[...], preferred_element_type=F32)
    o1 = 3 * HY_WIDTH
    hy_ref[...] = p[:, :o1]
    q_ref[...] = (_rope(p[:, o1:o1 + DFW], c_ref[...], s_ref[...]) * DF_SCALE).astype(BF16)
    _tok_write(kp_ref, ks_ref, _rope(p[:, o1 + DFW:o1 + 2 * DFW], c_ref[...], s_ref[...]))
    _tok_write(vp_ref, vs_ref, p[:, o1 + 2 * DFW:])


def _inproj_b(y, mods_l, g, w_in):
    cs, sn = _rope_tables(DFW, tuple(range(0, DFW, DF_DH)))
    pos = lambda i: (_pos_index(i), 0)
    k_shapes, k_specs = _split_out(DFW, BF16)
    v_shapes, v_specs = _split_out(DF_HEADS * DF_V, BF16)
    hy_u, q, kp, ks, vp, vs = pl.pallas_call(
        _inproj_b_kernel,
        out_shape=[jax.ShapeDtypeStruct((TOK, 3 * HY_WIDTH), F32), jax.ShapeDtypeStruct((TOK, DFW), BF16)]
                  + k_shapes + v_shapes,
        grid=(NT,),
        in_specs=[pl.BlockSpec((TM, D), _row),
                  pl.BlockSpec((1, N_MOD, D), lambda i: (_mod_index(i), 0, 0)),
                  _const_spec((1, D)), _const_spec((D, IN_B)),
                  pl.BlockSpec((TM, DFW), pos), pl.BlockSpec((TM, DFW), pos)],
        out_specs=[pl.BlockSpec((TM, 3 * HY_WIDTH), _row), pl.BlockSpec((TM, DFW), _row)] + k_specs + v_specs,
        compiler_params=_params(1, VMEM_LIMIT),
        name="inproj_odd",
    )(y, mods_l, g[None], w_in.astype(BF16), jnp.asarray(cs), jnp.asarray(sn))
    return hy_u, q, (kp, ks), (vp, vs)


def _diff_attn_kernel(nseg, lam_init, q_ref, lam_ref, sub_ref, *refs):
    o_ref = refs[-1]
    lp = lam_ref[...]
    lam = (jnp.exp(jnp.sum(lp[0:1] * lp[1:2], axis=-1, keepdims=True))
           - jnp.exp(jnp.sum(lp[2:3] * lp[3:4], axis=-1, keepdims=True)) + lam_init)
    tq = q_ref.shape[0]
    lane = lax.broadcasted_iota(jnp.int32, (tq, LANES), 1)
    for pair in range(DF_HEADS // 2):
        cs = slice(pair * LANES, (pair + 1) * LANES)
        q = q_ref[:, cs]
        ks = [refs[2 * s][:, cs].astype(BF16) for s in range(nseg)]
        vs = [refs[2 * s + 1][:, cs].astype(BF16) for s in range(nseg)]
        outs = []
        for hh in range(2):
            parts = []
            for half in range(2):
                unit = 2 * hh + half
                qm = jnp.where((lane >> 5) == unit, q, jnp.zeros_like(q))
                scores = [_dot_nt(qm, k) for k in ks]
                m = functools.reduce(jnp.maximum, [jnp.max(s, axis=-1, keepdims=True) for s in scores])
                es = [jnp.exp(s - m) for s in scores]
                l = functools.reduce(jnp.add, [jnp.sum(e, axis=-1, keepdims=True) for e in es])
                pv = functools.reduce(jnp.add, [_dot(e, v) for e, v in zip(es, vs)])
                parts.append(pv * (1.0 / l))
            o = parts[0] - lam * parts[1]
            mine = (lane >> 6) == hh
            ms = jnp.sum(jnp.where(mine, o * o, 0.0), axis=-1, keepdims=True) * (1.0 / DF_V)
            outs.append(o * lax.rsqrt(ms + EPS))
        o = jnp.where(lane < DF_V, outs[0], outs[1]) * sub_ref[...] * (1.0 - lam_init)
        o_ref[:, cs] = o.astype(o_ref.dtype)


def _diff_attention(q, k, v, lam_p, subln, lam_init, n_batch, seq, tq, row0, ctx=None):
    qt = seq // tq
    qb0, kb0 = row0 // tq, 0
    in_specs = [pl.BlockSpec((tq, DFW), lambda b, j: (qb0 + b * qt + j, 0)),
                pl.BlockSpec((4, DF_DH), lambda b, j: (0, 0)),
                pl.BlockSpec((1, LANES), lambda b, j: (0, 0))]
    args = [q, lam_p, jnp.concatenate([subln, subln])[None]]
    segs = []
    if ctx is not None:
        segs.append((ctx, PAST, 0))
    segs.append(((k, v), seq, kb0))
    for (a_k, a_v), ln, off in segs:
        idx = lambda b, j, off=off: (off + b, 0)
        in_specs += [pl.BlockSpec((ln, DFW), idx), pl.BlockSpec((ln, DF_HEADS * DF_V), idx)]
        args += [a_k, a_v]
    return pl.pallas_call(
        functools.partial(_diff_attn_kernel, len(segs), lam_init),
        out_shape=jax.ShapeDtypeStruct((n_batch * seq, DF_HEADS * DF_V), BF16),
        grid=(n_batch, qt),
        in_specs=in_specs,
        out_specs=pl.BlockSpec((tq, DF_HEADS * DF_V), lambda b, j: (b * qt + j, 0)),
        compiler_params=_params(2, VMEM_LIMIT),
        name="diff_attention",
    )(*args)


def _hy_filter_kernel(feat_ref, w1_ref, b1_ref, w2_ref, b2_ref, fq_ref, w3_ref, dec_ref, o_ref):
    feat = feat_ref[...]
    fq = fq_ref[...]
    h = jnp.sin(fq * (_dot3(feat, w1_ref[...]) + b1_ref[...]))
    h = jnp.sin(fq * (_dot3(h, w2_ref[...]) + b2_ref[...]))
    window = jnp.exp(-feat[:, 0:1] * jnp.abs(dec_ref[...]))
    for j in range(4):
        cs = slice(j * HY_WIDTH, (j + 1) * HY_WIDTH)
        o_ref[:, cs] = _dot3(h, w3_ref[:, cs]) * window


def _hy_spectrum_kernel(L, cs_ref, hf_ref, hb_ref, o_ref):
    row = lax.broadcasted_iota(jnp.int32, (L, HY_WIDTH), 0)
    first = row == 0
    tf = _dot(cs_ref[...], hf_ref[...])
    tb = _dot(cs_ref[...], jnp.where(first, 0.0, hb_ref[...]))
    ka = tf[:L] + tb[:L]
    kb = jnp.where(first, tf[L:] + tb[L:], tf[L:] - tb[L:])
    wv = jnp.where(first, 1.0 / (2 * L), 2.0 / (2 * L))
    o_ref[0, 0] = ka * wv
    o_ref[0, 1] = jnp.where(first, 0.0, kb) * wv
    o_ref[0, 2] = jnp.where(first, kb, ka) * wv


HY_CH = 256


def _hy_conv_kernel(L, cs_ref, ct_ref, kf_ref, v_ref, x1_ref, x2_ref,
                    wv_ref, w1_ref, w2_ref, bias_ref, o_ref):
    row = lax.broadcasted_iota(jnp.int32, (L, HY_CH), 0)

    def short(x, w):
        prev = jnp.where(row >= 1, pltpu.roll(x, 1, 0), 0.0)
        nxt = jnp.where(row <= L - 2, pltpu.roll(x, L - 1, 0), 0.0)
        return w[0:1] * prev + w[1:2] * x + w[2:3] * nxt

    for j in range(v_ref.shape[0] // L):
        rs = slice(j * L, (j + 1) * L)
        for k in range(HY_WIDTH // HY_CH):
            ch = slice(k * HY_CH, (k + 1) * HY_CH)
            z = short(v_ref[rs, ch], wv_ref[:, ch])
            gates = (short(x1_ref[rs, ch], w1_ref[:, ch]), short(x2_ref[rs, ch], w2_ref[:, ch]))
            for n in range(2):
                ab = _dot(cs_ref[...], z)
                a, b = ab[:L], ab[L:]
                ka, kb1, ka2 = kf_ref[n, 0, :, ch], kf_ref[n, 1, :, ch], kf_ref[n, 2, :, ch]
                pq = jnp.concatenate([a * ka - b * kb1, a * kb1 + b * ka2], axis=0)
                conv = _dot(ct_ref[...], pq)
                z = gates[n] * (conv + bias_ref[n:n + 1, ch] * z)
            o_ref[rs, ch] = z.astype(o_ref.dtype)


def _hyena_spectrum(L, phy):
    conv_w, w1, b1, w2, b2, freq, w3, decay, bias = phy
    feat = jnp.asarray(_hyena_features(L))
    w1p = jnp.pad(w1, ((0, LANES - HY_EMB), (0, 0)))
    filt = pl.pallas_call(
        _hy_filter_kernel,
        out_shape=jax.ShapeDtypeStruct((L, 4 * HY_WIDTH), F32),
        grid=(1,),
        in_specs=[_const_spec((L, LANES)), _const_spec((LANES, HY_FH)), _const_spec((1, HY_FH)),
                  _const_spec((HY_FH, HY_FH)), _const_spec((1, HY_FH)), _const_spec((1, HY_FH)),
                  _const_spec((HY_FH, 4 * HY_WIDTH)), _const_spec((1, HY_WIDTH))],
        out_specs=pl.BlockSpec((L, 4 * HY_WIDTH), lambda i: (0, 0)),
        compiler_params=_params(1, VMEM_LIMIT),
        name="hyena_filter",
    )(feat, w1p, b1[None], w2, b2[None], freq[None], w3, decay[None])
    cs = jnp.asarray(_dft_tables(L)[0]).astype(BF16)
    return pl.pallas_call(
        functools.partial(_hy_spectrum_kernel, L),
        out_shape=jax.ShapeDtypeStruct((2, 3, L, HY_WIDTH), F32),
        grid=(2,),
        in_specs=[_const_spec((2 * L, L)),
                  pl.BlockSpec((L, HY_WIDTH), lambda n: (0, n)),
                  pl.BlockSpec((L, HY_WIDTH), lambda n: (0, 2 + n))],
        out_specs=pl.BlockSpec((1, 3, L, HY_WIDTH), lambda n: (n, 0, 0, 0)),
        compiler_params=_params(1, VMEM_LIMIT),
        name="hyena_spectrum",
    )(cs, filt, filt)


def _hyena_conv(hy_u, spec, phy, n_batch, L, seqs, row0):
    conv_w, bias = phy[0], phy[8]
    cs, ct = (jnp.asarray(t).astype(BF16) for t in _dft_tables(L))
    rows = seqs * L
    rb0 = row0 // rows
    col = lambda off: (lambda b: (0, off))
    tok = lambda off: (lambda b: (rb0 + b, off))
    blk = lambda idx: pl.BlockSpec((rows, HY_WIDTH), idx)
    return pl.pallas_call(
        functools.partial(_hy_conv_kernel, L),
        out_shape=jax.ShapeDtypeStruct((n_batch * L, HY_WIDTH), BF16),
        grid=(n_batch // seqs,),
        in_specs=[_const_spec((2 * L, L)), _const_spec((L, 2 * L)), _const_spec((2, 3, L, HY_WIDTH)),
                  blk(tok(0)), blk(tok(1)), blk(tok(2)),
                  pl.BlockSpec((3, HY_WIDTH), col(0)), pl.BlockSpec((3, HY_WIDTH), col(1)),
                  pl.BlockSpec((3, HY_WIDTH), col(2)), _const_spec((2, HY_WIDTH))],
        out_specs=blk(lambda b: (b, 0)),
        compiler_params=_params(1, VMEM_LIMIT),
        name="hyena_conv",
    )(cs, ct, spec, hy_u, hy_u, hy_u, conv_w, conv_w, conv_w, bias)


def _even_mixer(y, mods_l, g, pa, ps5, ctx_ckv, ctx_krope, ctx_state):
    w_in, w_out, q_norm, w_uq, kv_norm, w_ukv = pa
    a_re, a_im, log_step, b_re, b_im, c_re, c_im, d_skip, w_glu = ps5
    q, ckv, kr_unrot, kr_rot, kn, v, u, (w_k, w_v) = _inproj_a(y, mods_l, g, w_in, q_norm, w_uq, kv_norm, w_ukv)

    ctx_flat = ctx_ckv.reshape(NB_S * PAST, MLA_KV_RANK)
    ctx_kn = _linear(ctx_flat, w_k, PAST, BF16)
    ctx_v = _linear(ctx_flat, w_v, PAST, BF16)
    ctx_kr = jnp.pad(ctx_krope.reshape(NB_S * PAST, MLA_ROPE),
                     ((0, 0), (KR_AT, LANES - KR_AT - MLA_ROPE))).astype(BF16)
    att_p = _mla_attention(q, kn, kr_rot, v, NB_P, L_P, L_P, 0)
    att_s = _mla_attention(q, kn, kr_rot, v, NB_S, L_S, TM, TOK_P, ctx=(ctx_kn, ctx_kr, ctx_v))

    prep = _s5_prep(a_re, a_im, log_step, b_re, b_im, c_re, c_im)
    h0 = ctx_state.transpose(3, 1, 0, 2, 4).reshape(S5_GROUPS, 2, NB_S, 2 * S5_N)
    h0 = jnp.pad(h0, ((0, 0), (0, 0), (0, 8 - NB_S), (0, 0)))
    s5y, fin = _s5_core(u, prep, h0, d_skip)

    mixer = ((att_p, att_s), s5y, w_out, w_glu)
    new_ckv = ckv.reshape(NB_P, L_P, MLA_KV_RANK)
    new_krope = kr_unrot[:, KR_AT:KR_AT + MLA_ROPE].reshape(NB_P, L_P, MLA_ROPE)
    new_state = fin.reshape(S5_GROUPS, 2, NB_P, 2, S5_N).transpose(2, 1, 3, 0, 4)
    return mixer, new_ckv, new_krope, new_state


def _odd_mixer(y, mods_l, g, pb, phy, ctx_k, ctx_v, lam_init):
    w_in, w_out, lam_p, subln = pb
    hy_u, q, (k_p, k_s), (v_p, v_s) = _inproj_b(y, mods_l, g, w_in)
    hy_p = _hyena_conv(hy_u, _hyena_spectrum(L_P, phy), phy, NB_P, L_P, 2, 0)
    hy_s = _hyena_conv(hy_u, _hyena_spectrum(L_S, phy), phy, NB_S, L_S, 1, TOK_P)
    ctx = (ctx_k.reshape(NB_S * PAST, DFW), ctx_v.reshape(NB_S * PAST, DF_HEADS * DF_V))
    att_p = _diff_attention(q, k_p, v_p, lam_p, subln, lam_init, NB_P, L_P, L_P, 0)
    att_s = _diff_attention(q, k_s, v_s, lam_p, subln, lam_init, NB_S, L_S, TM // 2, TOK_P, ctx=ctx)
    mixer = ((hy_p, hy_s), (att_p, att_s), w_out, None)
    new_k = k_p.reshape(NB_P, L_P, DF_HEADS, 2, DF_DH)
    new_v = v_p.reshape(NB_P, L_P, DF_HEADS, DF_V)
    return mixer, new_k, new_v


def kernel(x_prompt, x_sample, c, c_ctx, cache_mla_ckv, cache_mla_krope, state_s5, cache_diff_k, cache_diff_v, ada_w, ada_b, norm_g, ff_w_in, ff_w_out, w_in_a, w_out_a, mla_q_norm, mla_w_uq, mla_kv_norm, mla_w_ukv, s5_a_re, s5_a_im, s5_log_step, s5_b_re, s5_b_im, s5_c_re, s5_c_im, s5_d, s5_w_glu, w_in_b, w_out_b, hy_conv, hy_w1, hy_b1, hy_w2, hy_b2, hy_freq, hy_w3, hy_decay, hy_bias, df_lambda, df_subln, final_norm):
    depth = ada_w.shape[0]
    y = (x_prompt.reshape(TOK_P, D), x_sample.reshape(TOK_S, D))
    mods = _adaln(jnp.concatenate([c_ctx[None], c], axis=0), ada_w, ada_b)
    new_ckv, new_krope, new_s5, new_dk, new_dv = [], [], [], [], []
    for l in range(depth):
        y = _half_ffn(y, mods[l], norm_g[l, 0], ff_w_in, ff_w_out, l, 0)
        if l % 2 == 0:
            e = l // 2
            pa = (w_in_a[e], w_out_a[e], mla_q_norm[e], mla_w_uq[e], mla_kv_norm[e], mla_w_ukv[e])
            ps5 = (s5_a_re[e], s5_a_im[e], s5_log_step[e], s5_b_re[e], s5_b_im[e],
                   s5_c_re[e], s5_c_im[e], s5_d[e], s5_w_glu[e])
            mixer, ckv, krope, st = _even_mixer(y, mods[l], norm_g[l, 1], pa, ps5, cache_mla_ckv[:, e],
                                            cache_mla_krope[:, e], state_s5[:, e])
            new_ckv.append(ckv)
            new_krope.append(krope)
            new_s5.append(st)
        else:
            o = l // 2
            lam_init = 0.8 - 0.6 * math.exp(-0.3 * l)
            pb = (w_in_b[o], w_out_b[o], df_lambda[o], df_subln[o])
            phy = (hy_conv[o], hy_w1[o], hy_b1[o], hy_w2[o], hy_b2[o], hy_freq[o],
                   hy_w3[o], hy_decay[o], hy_bias[o])
            mixer, dk, dv = _odd_mixer(y, mods[l], norm_g[l, 1], pb, phy, cache_diff_k[:, o],
                                   cache_diff_v[:, o], lam_init)
            new_dk.append(dk)
            new_dv.append(dv)
        last = l == depth - 1
        y = _half_ffn(y, mods[l], norm_g[l, 2], ff_w_in, ff_w_out, l, 1,
                      final_g=final_norm if last else None, mixer=mixer)
    y_prompt = y[0].reshape(NB_P, L_P, D)
    y_sample = y[1].reshape(NB_S, L_S, D)
    return (y_prompt, y_sample, jnp.stack(new_ckv, axis=1), jnp.stack(new_krope, axis=1),
            jnp.stack(new_s5, axis=1), jnp.stack(new_dk, axis=1), jnp.stack(new_dv, axis=1))
```

```python
import functools
import math

import numpy as np
import jax
import jax.numpy as jnp
from jax import lax
from jax.experimental import pallas as pl
from jax.experimental.pallas import tpu as pltpu

F32 = jnp.float32
BF16 = jnp.bfloat16

D = 1024
NB_P, L_P = 16, 256
NB_S, L_S = 2, 1024
PAST = 256
GRID_W = 64
N_MOD = 9
FF = 2816
EPS = 1e-6
ROPE_BASE = 10000.0

MLA_HEADS, MLA_NOPE, MLA_ROPE, MLA_V = 8, 64, 32, 64
MLA_Q_RANK, MLA_KV_RANK = 384, 256
S5_WIDTH, S5_GROUP, S5_N = 512, 16, 64
S5_GROUPS = S5_WIDTH // S5_GROUP
HY_WIDTH, HY_BANDS, HY_FH = 512, 16, 64
HY_EMB = 2 * HY_BANDS + 1
DF_HEADS, DF_DH = 8, 32
DF_V = 2 * DF_DH

TOK_P = NB_P * L_P
TOK_S = NB_S * L_S
TOK = TOK_P + TOK_S
TM = 512
NT = TOK // TM
NT_P = TOK_P // TM
TILES_PER_SAMPLE = L_S // TM

LANES = 128
S5_T = 16
S5_CW = S5_T * S5_GROUP
CH_P = L_P // S5_T
CH_S = L_S // S5_T
S5_ROWS = NB_P * CH_P + NB_S * CH_S
S5_ROWS_P = NB_P * CH_P

VMEM_LIMIT = 56 * 1024 * 1024


def _params(n_grid, vmem=None):
    return pltpu.CompilerParams(dimension_semantics=("arbitrary",) * n_grid,
                                vmem_limit_bytes=vmem)


def _const_spec(shape):
    nd = len(shape)
    return pl.BlockSpec(shape, lambda *_: (0,) * nd, pipeline_mode=pl.Buffered(1))


def _mod_index(i):
    return jnp.where(i < NT_P, 0, 1 + (i - NT_P) // TILES_PER_SAMPLE)


def _mod_spec(layer):
    return pl.BlockSpec((1, 8, N_MOD * D), lambda *_: (layer, 0, 0), pipeline_mode=pl.Buffered(1))


def _mod_rows(mod_ref):
    row = mod_ref[0, pl.ds(_mod_index(pl.program_id(0)), 1), :]
    return [row[:, k * D:(k + 1) * D] for k in range(N_MOD)]


def _pos_index(i):
    return jnp.where(i < NT_P, 0, 1 + (i - NT_P) % TILES_PER_SAMPLE)


def _row(i):
    return (i, 0)


def _row_p(i):
    return (jnp.minimum(i, NT_P - 1), 0)


def _row_s(i):
    return (jnp.maximum(i - NT_P, 0), 0)


def _tok_specs(x, width):
    if isinstance(x, tuple):
        return [pl.BlockSpec((TM, width), _row_p), pl.BlockSpec((TM, width), _row_s)], list(x)
    return [pl.BlockSpec((TM, width), _row)], [x]


def _tok_read(refs, split):
    if split:
        return jnp.where(pl.program_id(0) < NT_P, refs[0][...], refs[1][...]), refs[2:]
    return refs[0][...], refs[1:]


def _tok_write(p_ref, s_ref, value):
    i = pl.program_id(0)

    @pl.when(i < NT_P)
    def _():
        p_ref[...] = value

    @pl.when(i >= NT_P)
    def _():
        s_ref[...] = value.astype(s_ref.dtype)


def _split_out(width, sample_dtype=F32):
    shapes = [jax.ShapeDtypeStruct((TOK_P, width), F32), jax.ShapeDtypeStruct((TOK_S, width), sample_dtype)]
    specs = [pl.BlockSpec((TM, width), _row_p), pl.BlockSpec((TM, width), _row_s)]
    return shapes, specs


def _dot(a, b):
    return jnp.dot(a.astype(BF16), b.astype(BF16), preferred_element_type=F32)


def _dot_nt(a, b):
    return lax.dot_general(a, b, (((1,), (1,)), ((), ())), preferred_element_type=F32)


def _split(x):
    hi = x.astype(BF16)
    lo = (x - hi.astype(F32)).astype(BF16)
    return hi, lo


def _dot3(a, b):
    ah, al = _split(a)
    bh, bl = _split(b)
    d = functools.partial(jnp.dot, preferred_element_type=F32)
    return d(ah, bh) + d(ah, bl) + d(al, bh)


def _rmsnorm(x, g):
    return x * lax.rsqrt(jnp.mean(x * x, axis=-1, keepdims=True) + EPS) * g


def _modulate(y, g, shift, scale):
    return _rmsnorm(y, g) * (1.0 + scale) + shift


def _pair_swap(x):
    n = x.shape[-1]
    lane = lax.broadcasted_iota(jnp.int32, x.shape, x.ndim - 1)
    return jnp.where((lane & 1) == 0, pltpu.roll(x, n - 1, x.ndim - 1), pltpu.roll(x, 1, x.ndim - 1))


def _rope(x, cos, sin_signed):
    return x * cos + _pair_swap(x) * sin_signed


def _rope_angles():
    n_freq = MLA_ROPE // 4
    inv = 1.0 / (ROPE_BASE ** (np.arange(n_freq, dtype=np.float64) / n_freq))
    pos = np.arange(L_S)
    row = (pos // GRID_W).astype(np.float64)
    col = (pos % GRID_W).astype(np.float64)
    ang = np.concatenate([row[:, None] * inv, col[:, None] * inv], axis=-1)
    return np.cos(ang), np.sin(ang)


@functools.lru_cache(maxsize=None)
def _rope_tables(width, starts):
    cos, sin = _rope_angles()
    c = np.ones((TM + L_S, width), np.float32)
    s = np.zeros((TM + L_S, width), np.float32)
    sign = np.where(np.arange(MLA_ROPE) % 2 == 0, -1.0, 1.0)
    unit_c = np.repeat(cos, 2, axis=1)
    unit_s = np.repeat(sin, 2, axis=1) * sign
    for st in starts:
        c[TM:, st:st + MLA_ROPE] = unit_c
        s[TM:, st:st + MLA_ROPE] = unit_s
    return c, s


@functools.lru_cache(maxsize=None)
def _dft_tables(L):
    f = np.arange(L)[:, None]
    s = np.arange(L)[None, :]
    ang = np.pi * ((f * s) % (2 * L)).astype(np.float64) / L
    cs = np.concatenate([np.cos(ang), np.sin(ang)], axis=0)
    cs[L, :] = np.where(np.arange(L) % 2 == 0, 1.0, -1.0)
    cs = cs.astype(np.float32)
    return cs, np.ascontiguousarray(cs.T)


@functools.lru_cache(maxsize=None)
def _hyena_features(L):
    t = np.arange(L, dtype=np.float64) / L
    bands = np.arange(1, HY_BANDS + 1, dtype=np.float64)
    ang = 2.0 * math.pi * t[:, None] * bands
    feat = np.zeros((L, LANES), np.float32)
    feat[:, 0] = t
    feat[:, 1:1 + HY_BANDS] = np.cos(ang)
    feat[:, 1 + HY_BANDS:HY_EMB] = np.sin(ang)
    return feat


def _adaln_kernel(c_ref, w_ref, b_ref, o_ref):
    s = jax.nn.silu(c_ref[...])
    s_hi = s.astype(BF16).astype(F32)
    stacked = jnp.concatenate([s_hi, s - s_hi], axis=0).astype(BF16)
    wh, wl = _split(w_ref[0])
    both = jnp.dot(stacked, wh, preferred_element_type=F32)
    rows = c_ref.shape[0]
    o_ref[0] = both[:rows] + both[rows:] + jnp.dot(stacked, wl, preferred_element_type=F32)[:rows] + b_ref[0]


def _adaln(cvecs, ada_w, ada_b):
    depth = ada_w.shape[0]
    n_vec = cvecs.shape[0]
    tn = D
    out = pl.pallas_call(
        _adaln_kernel,
        out_shape=jax.ShapeDtypeStruct((depth, 8, N_MOD * D), F32),
        grid=(depth, N_MOD * D // tn),
        in_specs=[pl.BlockSpec((8, D), lambda l, j: (0, 0)),
                  pl.BlockSpec((1, D, tn), lambda l, j: (l, 0, j)),
                  pl.BlockSpec((1, 1, tn), lambda l, j: (l, 0, j))],
        out_specs=pl.BlockSpec((1, 8, tn), lambda l, j: (l, 0, j)),
        compiler_params=_params(2),
        name="adaln",
    )(jnp.pad(cvecs, ((0, 8 - n_vec), (0, 0))), ada_w, ada_b[:, None, :])
    return out


FF_PIECE = 256
FF_LOADS = FF // FF_PIECE


def _ffn_kernel(base, final, split_in, mixer, layer, which, *refs):
    y, refs = _tok_read(refs, split_in)
    if mixer is not None:
        a1, refs = _tok_read(refs, mixer[0])
        a2, refs = _tok_read(refs, mixer[1])
        wmix_ref, wg_ref = refs[:2]
        refs = refs[2:]
    mod_ref, g_ref, win_hbm, wout_hbm, fg_ref = refs[:5]
    n_out = 2 if final else 1
    outs = refs[5:5 + n_out]
    win_ref, wout_ref, stage_g, stage_u, stage_o, sems = refs[5 + n_out:]
    mod = _mod_rows(mod_ref)
    if mixer is not None:
        if mixer[2]:
            a2 = jax.nn.gelu(a2)
            a2 = a2 * jax.nn.sigmoid(_dot(a2, wg_ref[...]))
        k1 = wmix_ref.shape[0] // 2
        y = y + mod[5] * (_dot(a1, wmix_ref[:k1]) + _dot(a2, wmix_ref[k1:]))
    h = _modulate(y, g_ref[...], mod[base], mod[base + 1]).astype(BF16)

    def hidden(lo, width):
        gate = jnp.dot(h, win_ref[:, lo:lo + width], preferred_element_type=F32)
        up = jnp.dot(h, win_ref[:, FF + lo:FF + lo + width], preferred_element_type=F32)
        a = (jax.nn.silu(gate) * up).astype(BF16)
        return jnp.dot(a, wout_ref[lo:lo + width, :], preferred_element_type=F32)

    def finish(acc):
        out = y + 0.5 * mod[base + 2] * acc
        if final:
            _tok_write(outs[0], outs[1], _rmsnorm(out, fg_ref[...]))
        else:
            outs[0][...] = out

    @pl.when(pl.program_id(0) == 0)
    def _():
        def copies(c, slot):
            cols = pl.ds(c * FF_PIECE, FF_PIECE)
            return (pltpu.make_async_copy(win_hbm.at[layer, which, :, cols], stage_g.at[slot], sems.at[0, slot]),
                    pltpu.make_async_copy(win_hbm.at[layer, which, :, pl.ds(FF + c * FF_PIECE, FF_PIECE)],
                                          stage_u.at[slot], sems.at[1, slot]),
                    pltpu.make_async_copy(wout_hbm.at[layer, which, cols, :], stage_o.at[slot], sems.at[2, slot]))

        for cp in copies(0, 0):
            cp.start()
        acc = jnp.zeros(y.shape, F32)
        for c in range(FF_LOADS):
            slot = c % 2
            lo = c * FF_PIECE
            if c + 1 < FF_LOADS:
                for cp in copies(c + 1, 1 - slot):
                    cp.start()
            for cp in copies(c, slot):
                cp.wait()
            win_ref[:, lo:lo + FF_PIECE] = stage_g[slot].astype(BF16)
            win_ref[:, FF + lo:FF + lo + FF_PIECE] = stage_u[slot].astype(BF16)
            wout_ref[lo:lo + FF_PIECE, :] = stage_o[slot].astype(BF16)
            acc = acc + hidden(lo, FF_PIECE)
        finish(acc)

    @pl.when(pl.program_id(0) > 0)
    def _():
        finish(hidden(0, FF))


def _half_ffn(y, mods_l, g, ff_w_in, ff_w_out, layer, which, final_g=None, mixer=None):
    final = final_g is not None
    fg = final_g if final else g
    y_specs, y_args = _tok_specs(y, D)
    mix_flags = None
    if mixer is not None:
        a1, a2, w_out, w_glu = mixer
        k1 = w_out.shape[0] // 2
        wg = w_glu if w_glu is not None else jnp.zeros((8, LANES), F32)
        s1, a1_args = _tok_specs(a1, k1)
        s2, a2_args = _tok_specs(a2, k1)
        y_specs = y_specs + s1 + s2 + [_const_spec(w_out.shape), _const_spec(wg.shape)]
        y_args = y_args + a1_args + a2_args + [w_out, wg]
        mix_flags = (isinstance(a1, tuple), isinstance(a2, tuple), w_glu is not None)
    if final:
        out_shape, out_specs = _split_out(D)
    else:
        out_shape, out_specs = jax.ShapeDtypeStruct((TOK, D), F32), pl.BlockSpec((TM, D), _row)
    return pl.pallas_call(
        functools.partial(_ffn_kernel, 6 * which, final, isinstance(y, tuple), mix_flags, layer, which),
        out_shape=out_shape,
        grid=(NT,),
        in_specs=y_specs + [_mod_spec(mods_l[1]),
                            _const_spec((1, D)),
                            pl.BlockSpec(memory_space=pl.ANY),
                            pl.BlockSpec(memory_space=pl.ANY),
                            _const_spec((1, D))],
        out_specs=out_specs,
        scratch_shapes=[pltpu.VMEM((D, 2 * FF), BF16), pltpu.VMEM((FF, D), BF16),
                        pltpu.VMEM((2, D, FF_PIECE), F32), pltpu.VMEM((2, D, FF_PIECE), F32),
                        pltpu.VMEM((2, FF_PIECE, D), F32), pltpu.SemaphoreType.DMA((3, 2))],
        compiler_params=_params(1, VMEM_LIMIT),
        name="half_ffn",
    )(*y_args, mods_l[0], g[None], ff_w_in, ff_w_out, fg[None])


def _linear_kernel(x_ref, w_ref, o_ref):
    o_ref[...] = _dot(x_ref[...], w_ref[...]).astype(o_ref.dtype)


def _linear(x, w, tm, out_dtype):
    m, k = x.shape
    n = w.shape[1]
    return pl.pallas_call(
        _linear_kernel,
        out_shape=jax.ShapeDtypeStruct((m, n), out_dtype),
        grid=(m // tm,),
        in_specs=[pl.BlockSpec((tm, k), lambda i: (i, 0)), _const_spec((k, n))],
        out_specs=pl.BlockSpec((tm, n), lambda i: (i, 0)),
        compiler_params=_params(1),
        name="linear",
    )(x, w.astype(BF16))


MLA_SCALE = (MLA_NOPE + MLA_ROPE) ** -0.5
QW = MLA_HEADS * LANES
KR_AT = MLA_NOPE
IN_A_PAD = MLA_Q_RANK + MLA_KV_RANK + S5_WIDTH + LANES


def _inproj_a_kernel(y_ref, mod_ref, g_ref, win_ref, qn_ref, wuq_ref, kvn_ref, wk_ref, wv_ref,
                     cq_ref, sq_ref, ck_ref, sk_ref,
                     q_ref, ckv_ref, kru_ref, krr_ref, kn_ref, v_ref, u_ref):
    mod = _mod_rows(mod_ref)
    h = _modulate(y_ref[...], g_ref[...], mod[3], mod[4]).astype(BF16)
    p = jnp.dot(h, win_ref[...], preferred_element_type=F32)
    o1 = MLA_Q_RANK
    o2 = o1 + MLA_KV_RANK
    o3 = o2 + S5_WIDTH
    q = _dot(_rmsnorm(p[:, :o1], qn_ref[...]), wuq_ref[...])
    q_ref[...] = (_rope(q, cq_ref[...], sq_ref[...]) * MLA_SCALE).astype(BF16)
    ckv = _rmsnorm(p[:, o1:o2], kvn_ref[...])
    ckv_b = ckv.astype(BF16)
    kn_ref[...] = jnp.dot(ckv_b, wk_ref[...], preferred_element_type=F32).astype(BF16)
    v_ref[...] = jnp.dot(ckv_b, wv_ref[...], preferred_element_type=F32).astype(BF16)
    u_ref[...] = p[:, o2:o3]
    krp = p[:, o3:]
    krr_ref[...] = _rope(krp, ck_ref[...], sk_ref[...]).astype(BF16)

    @pl.when(pl.program_id(0) < NT_P)
    def _():
        ckv_ref[...] = ckv
        kru_ref[...] = krp


def _inproj_a(y, mods_l, g, w_in, q_norm, w_uq, kv_norm, w_ukv):
    o1 = MLA_Q_RANK
    o2 = o1 + MLA_KV_RANK
    o3 = o2 + MLA_ROPE
    kr_cols = jnp.pad(w_in[:, o2:o3], ((0, 0), (KR_AT, LANES - KR_AT - MLA_ROPE)))
    w_ext = jnp.concatenate([w_in[:, :o2], w_in[:, o3:], kr_cols], axis=1).astype(BF16)
    dq = MLA_NOPE + MLA_ROPE
    w_uq_pad = jnp.pad(w_uq.reshape(MLA_Q_RANK, MLA_HEADS, dq),
                       ((0, 0), (0, 0), (0, LANES - dq))).reshape(MLA_Q_RANK, QW).astype(BF16)
    w_kv = w_ukv.reshape(MLA_KV_RANK, MLA_HEADS, MLA_NOPE + MLA_V)
    w_k = jnp.pad(w_kv[:, :, :MLA_NOPE], ((0, 0), (0, 0), (0, LANES - MLA_NOPE))).reshape(MLA_KV_RANK, QW)
    w_v = w_kv[:, :, MLA_NOPE:].reshape(MLA_KV_RANK, MLA_HEADS * MLA_V)
    w_k, w_v = w_k.astype(BF16), w_v.astype(BF16)
    cq, sq = _rope_tables(QW, tuple(h * LANES + MLA_NOPE for h in range(MLA_HEADS)))
    ck, sk = _rope_tables(LANES, (KR_AT,))
    row = _row
    pos = lambda i: (_pos_index(i), 0)
    widths = (QW, MLA_KV_RANK, LANES, LANES, QW, MLA_HEADS * MLA_V, S5_WIDTH)
    prompt_only = (1, 2)
    mxu_only = (0, 3, 4, 5)
    outs = pl.pallas_call(
        _inproj_a_kernel,
        out_shape=[jax.ShapeDtypeStruct((TOK_P if k in prompt_only else TOK, w), BF16 if k in mxu_only else F32)
                   for k, w in enumerate(widths)],
        grid=(NT,),
        in_specs=[pl.BlockSpec((TM, D), row),
                  _mod_spec(mods_l[1]),
                  _const_spec((1, D)),
                  _const_spec((D, IN_A_PAD)),
                  _const_spec((1, MLA_Q_RANK)),
                  _const_spec((MLA_Q_RANK, QW)),
                  _const_spec((1, MLA_KV_RANK)),
                  _const_spec((MLA_KV_RANK, QW)),
                  _const_spec((MLA_KV_RANK, MLA_HEADS * MLA_V)),
                  pl.BlockSpec((TM, QW), pos), pl.BlockSpec((TM, QW), pos),
                  pl.BlockSpec((TM, LANES), pos), pl.BlockSpec((TM, LANES), pos)],
        out_specs=[pl.BlockSpec((TM, w), _row_p if k in prompt_only else row)
                   for k, w in enumerate(widths)],
        compiler_params=_params(1, VMEM_LIMIT),
        name="inproj_even",
    )(y, mods_l[0], g[None], w_ext, q_norm[None], w_uq_pad, kv_norm[None], w_k, w_v,
      jnp.asarray(cq), jnp.asarray(sq), jnp.asarray(ck), jnp.asarray(sk))
    q, ckv, kr_unrot, kr_rot, kn, v, u = outs
    return q, ckv, kr_unrot, kr_rot, kn, v, u, (w_k, w_v)


def _mla_attn_kernel(nseg, q_ref, *refs):
    o_ref = refs[-1]
    tq = q_ref.shape[0]
    lane = lax.broadcasted_iota(jnp.int32, (tq, LANES), 1)
    for pair in range(MLA_HEADS // 2):
        outs = []
        for hh in range(2):
            h = 2 * pair + hh
            hs = slice(h * LANES, (h + 1) * LANES)
            qh = q_ref[:, hs]
            scores = []
            for s in range(nseg):
                kn_ref, kr_ref = refs[3 * s], refs[3 * s + 1]
                kh = (kn_ref[:, hs] + kr_ref[...]).astype(BF16)
                scores.append(_dot_nt(qh, kh))
            m = functools.reduce(jnp.maximum, [jnp.max(s, axis=-1, keepdims=True) for s in scores])
            es = [jnp.exp(s - m) for s in scores]
            l = functools.reduce(jnp.add, [jnp.sum(e, axis=-1, keepdims=True) for e in es])
            o = None
            for s in range(nseg):
                v_ref = refs[3 * s + 2]
                part = _dot(es[s], v_ref[:, pair * LANES:(pair + 1) * LANES])
                o = part if o is None else o + part
            outs.append(o / l)
        o_ref[:, pair * LANES:(pair + 1) * LANES] = jnp.where(lane < MLA_V, outs[0], outs[1]).astype(o_ref.dtype)


def _mla_attention(q, kn, kr, v, n_batch, seq, tq, row0, ctx=None):
    qt = seq // tq
    qb0, kb0 = row0 // tq, row0 // seq
    in_specs = [pl.BlockSpec((tq, QW), lambda b, j: (qb0 + b * qt + j, 0))]
    args = [q]
    segs = []
    if ctx is not None:
        segs.append((ctx, PAST, 0))
    segs.append(((kn, kr, v), seq, kb0))
    for (a_kn, a_kr, a_v), ln, off in segs:
        idx = lambda b, j, off=off: (off + b, 0)
        in_specs += [pl.BlockSpec((ln, QW), idx), pl.BlockSpec((ln, LANES), idx),
                     pl.BlockSpec((ln, MLA_HEADS * MLA_V), idx)]
        args += [a_kn, a_kr, a_v]
    return pl.pallas_call(
        functools.partial(_mla_attn_kernel, len(segs)),
        out_shape=jax.ShapeDtypeStruct((n_batch * seq, MLA_HEADS * MLA_V), BF16),
        grid=(n_batch, qt),
        in_specs=in_specs,
        out_specs=pl.BlockSpec((tq, MLA_HEADS * MLA_V), lambda b, j: (b * qt + j, 0)),
        compiler_params=_params(2, VMEM_LIMIT),
        name="mla_attention",
    )(*args)


def _cpow(ar, ai, e, nbits):
    rr = jnp.ones_like(ar)
    ri = jnp.zeros_like(ar)
    br, bi = ar, ai
    for k in range(nbits):
        bit = ((e >> k) & 1) == 1
        nr = rr * br - ri * bi
        ni = rr * bi + ri * br
        rr = jnp.where(bit, nr, rr)
        ri = jnp.where(bit, ni, ri)
        if k + 1 < nbits:
            br, bi = br * br - bi * bi, 2.0 * br * bi
    return rr, ri


def _s5_abar_kernel(lr_ref, li_ref, ls_ref, o_ref):
    step = jnp.exp(ls_ref[...])
    lr = jnp.minimum(lr_ref[...], -1e-4)
    li = li_ref[...]
    mag = jnp.exp(lr * step)
    ar = mag * jnp.cos(li * step)
    ai = mag * jnp.sin(li * step)
    den = lr * lr + li * li
    o_ref[0] = ar
    o_ref[1] = ai
    o_ref[2] = ((ar - 1.0) * lr + ai * li) / den
    o_ref[3] = (ai * lr - (ar - 1.0) * li) / den


def _s5_prep_kernel(arow_ref, acol_ref, btr_ref, bti_ref, ctr_ref, cti_ref,
                    wi_ref, ws_ref, wo_ref, ap_ref):
    n2 = 2 * S5_N
    blk_o = lax.broadcasted_iota(jnp.int32, (S5_N, S5_CW), 1) >> 4
    lane_k = lax.broadcasted_iota(jnp.int32, (S5_GROUP, S5_CW), 1)
    row_k = lax.broadcasted_iota(jnp.int32, (S5_GROUP, S5_CW), 0)
    lane_b = lax.broadcasted_iota(jnp.int32, (S5_GROUP, n2), 1)
    lane_a = lax.broadcasted_iota(jnp.int32, (1, n2), 1)
    rep = ((lane_k & (S5_GROUP - 1)) == row_k).astype(BF16)

    def tile16(x):
        hi = x.astype(BF16)
        r1 = x - hi.astype(F32)
        mid = r1.astype(BF16)
        lo = (r1 - mid.astype(F32)).astype(BF16)
        d = lambda a: lax.dot_general(a, rep, (((0,), (0,)), ((), ())), preferred_element_type=F32)
        return d(hi) + d(mid) + d(lo)

    intra = [None] * S5_T
    for d in range(2):
        ar, ai, fr, fi = (arow_ref[d, 0, k:k + 1, :] for k in range(4))
        btr, bti = btr_ref[d, 0], bti_ref[d, 0]
        bbr = fr * btr - fi * bti
        bbi = fr * bti + fi * btr
        pws = [(jnp.ones_like(ar), jnp.zeros_like(ar))]
        for _ in range(S5_T):
            pr, pi = pws[-1]
            pws.append((pr * ar - pi * ai, pr * ai + pi * ar))
        for s in range(S5_T):
            pr, pi = pws[S5_T - 1 - s] if d == 0 else pws[s]
            ws_ref[d, 0, s * S5_GROUP:(s + 1) * S5_GROUP, :] = jnp.where(
                lane_b < S5_N, pr * bbr - pi * bbi, pr * bbi + pi * bbr).astype(BF16)

        acol = acol_ref[d, 0]
        arc = jnp.broadcast_to(acol[:, 0:1], (S5_N, S5_CW))
        aic = jnp.broadcast_to(acol[:, 1:2], (S5_N, S5_CW))
        ctr, cti = tile16(ctr_ref[d, 0]), tile16(cti_ref[d, 0])
        e_lag = blk_o if d == 0 else (S5_T - 1 - blk_o)
        pqr, pqi = _cpow(arc, aic, e_lag, 4)
        qr = pqr * ctr - pqi * cti
        qi = pqr * cti + pqi * ctr
        wo_ref[d, 0] = jnp.concatenate([qr * arc - qi * aic, -(qr * aic + qi * arc)], axis=0).astype(BF16)
        q_stack = jnp.concatenate([qr, qi], axis=0)
        bb_mix = jnp.where(lane_b < S5_N, bbr, -bbi)
        kt = _dot3(bb_mix, q_stack)
        for s in range(S5_T):
            if d == 0:
                blk = jnp.where(lane_k >= S5_GROUP * s, pltpu.roll(kt, S5_GROUP * s, 1), 0.0)
            else:
                blk = jnp.where(lane_k < S5_GROUP * (s + 1),
                                pltpu.roll(kt, (S5_GROUP * (s + 1)) % S5_CW, 1), 0.0)
            intra[s] = blk if intra[s] is None else intra[s] + blk

        pr1, pi1 = pws[S5_T]
        for k in range(6):
            ap_ref[d, 0, k:k + 1, :] = pr1
            ap_ref[d, 0, 8 + k:9 + k, :] = jnp.where(lane_a < S5_N, -pi1, pi1)
            pr1, pi1 = pr1 * pr1 - pi1 * pi1, 2.0 * pr1 * pi1
        ap_ref[d, 0, 6:8, :] = jnp.zeros((2, n2), F32)
        ap_ref[d, 0, 14:16, :] = jnp.zeros((2, n2), F32)
    for s in range(S5_T):
        wi_ref[0, s * S5_GROUP:(s + 1) * S5_GROUP, :] = intra[s].astype(BF16)


def _s5_prep(a_re, a_im, log_step, b_re, b_im, c_re, c_im):
    g, n, n2 = S5_GROUPS, S5_N, 2 * S5_N
    abar = pl.pallas_call(
        _s5_abar_kernel,
        out_shape=jax.ShapeDtypeStruct((4, 2 * g, n), F32),
        grid=(1,),
        in_specs=[_const_spec((2 * g, n)), _const_spec((2 * g, n)), _const_spec((2 * g, 1))],
        out_specs=pl.BlockSpec((4, 2 * g, n), lambda i: (0, 0, 0)),
        compiler_params=_params(1),
        name="s5_abar",
    )(a_re.reshape(2 * g, n), a_im.reshape(2 * g, n), log_step.reshape(2 * g, 1))
    abar = jnp.concatenate([abar, abar], axis=-1).reshape(4, 2, g, n2)
    arow = abar.transpose(1, 2, 0, 3)
    acol = abar[:2, :, :, :n].transpose(1, 2, 3, 0)
    bt = lambda b: jnp.concatenate([jnp.swapaxes(b, 2, 3)] * 2, axis=-1)
    spec4 = lambda r, c: pl.BlockSpec((2, 1, r, c), lambda i: (0, i, 0, 0))
    return pl.pallas_call(
        _s5_prep_kernel,
        out_shape=[jax.ShapeDtypeStruct((g, S5_CW, S5_CW), BF16),
                   jax.ShapeDtypeStruct((2, g, S5_CW, n2), BF16),
                   jax.ShapeDtypeStruct((2, g, n2, S5_CW), BF16),
                   jax.ShapeDtypeStruct((2, g, 16, n2), F32)],
        grid=(g,),
        in_specs=[spec4(4, n2), spec4(n, 2),
                  spec4(S5_GROUP, n2), spec4(S5_GROUP, n2), spec4(S5_GROUP, n), spec4(S5_GROUP, n)],
        out_specs=[pl.BlockSpec((1, S5_CW, S5_CW), lambda i: (i, 0, 0)),
                   spec4(S5_CW, n2), spec4(n2, S5_CW), spec4(16, n2)],
        compiler_params=_params(1),
        name="s5_prep",
    )(arow, acol, bt(b_re), bt(b_im), c_re, c_im)


def _cmul_rows(x, p1, p2):
    return x * p1 + pltpu.roll(x, S5_N, 1) * p2


S5_OCT = LANES // S5_GROUP
S5_RB_IN = 96
S5_RB_OUT = 48


def _s5_core_kernel(u_ref, wi_ref, ws_ref, wo_ref, ap_ref, h0_ref, d_ref, y_ref, fin_ref,
                    ug_ref, yg_ref, z_ref):
    n2 = 2 * S5_N

    def tok_rows(r0, t, nrows):
        return pl.ds(r0 * S5_T + t, nrows, stride=S5_T)

    def block_transpose(xs):
        n = S5_OCT
        blk = lax.broadcasted_iota(jnp.int32, xs[0].shape, 1) >> 4
        a = [pltpu.roll(x, i * S5_GROUP, 1) if i else x for i, x in enumerate(xs)]
        ys = []
        for d in range(n):
            diag = a[-d % n]
            for b in range(1, n):
                diag = jnp.where(blk == b, a[(b - d) % n], diag)
            ys.append(pltpu.roll(diag, LANES - d * S5_GROUP, 1) if d else diag)
        return ys

    def gather(rb, carry):
        r0 = pl.multiple_of(rb * S5_RB_IN, S5_RB_IN)
        for half in range(2):
            xs = [u_ref[tok_rows(r0, S5_OCT * half + tt, S5_RB_IN), :] for tt in range(S5_OCT)]
            for gl, x in enumerate(block_transpose(xs)):
                ug_ref[gl, pl.ds(r0, S5_RB_IN), half * LANES:(half + 1) * LANES] = x
        return carry

    lax.fori_loop(0, S5_ROWS // S5_RB_IN, gather, 0)

    r = lax.broadcasted_iota(jnp.int32, (S5_ROWS, n2), 0)
    in_p = r < S5_ROWS_P
    rib = jnp.where(in_p, r & (CH_P - 1), (r - S5_ROWS_P) & (CH_S - 1))
    nch = jnp.where(in_p, CH_P, CH_S)

    def one_group(gl, slot):
        ub = ug_ref[gl].astype(BF16)
        y = jnp.dot(ub, wi_ref[gl], preferred_element_type=F32)
        for d in range(2):
            p1, p2 = ap_ref[d, gl, 0:1, :], ap_ref[d, gl, 8:9, :]
            edge = [S5_ROWS_P + CH_S * b + (0 if d == 0 else CH_S - 1) for b in range(NB_S)]
            h0 = [h0_ref[gl, d, b:b + 1, :] for b in range(NB_S)]
            s = jnp.dot(ub, ws_ref[d, gl], preferred_element_type=F32)
            for b in range(NB_S):
                s = s + jnp.where(r == edge[b], _cmul_rows(h0[b], p1, p2), 0.0)
            for k in range(6):
                sh = 1 << k
                if d == 0:
                    t = jnp.where(rib >= sh, pltpu.roll(s, sh, 0), 0.0)
                else:
                    t = jnp.where(rib < nch - sh, pltpu.roll(s, S5_ROWS - sh, 0), 0.0)
                s = s + _cmul_rows(t, ap_ref[d, gl, k:k + 1, :], ap_ref[d, gl, 8 + k:9 + k, :])
            z_ref[slot, d] = s
            first = CH_P - 1 if d == 0 else 0
            fin_ref[gl, d] = z_ref[slot, d, pl.ds(first, NB_P, stride=CH_P), :]
            if d == 0:
                sp = jnp.where(rib >= 1, pltpu.roll(s, 1, 0), 0.0)
            else:
                sp = jnp.where(rib < nch - 1, pltpu.roll(s, S5_ROWS - 1, 0), 0.0)
            for b in range(NB_S):
                sp = jnp.where(r == edge[b], h0[b], sp)
            y = y + jnp.dot(sp.astype(BF16), wo_ref[d, gl], preferred_element_type=F32)
        yg_ref[gl] = y

    def group_pair(gp, carry):
        for slot in range(2):
            one_group(2 * gp + slot, slot)
        return carry

    lax.fori_loop(0, S5_OCT // 2, group_pair, 0)

    def scatter(rb, carry):
        r0 = pl.multiple_of(rb * S5_RB_OUT, S5_RB_OUT)
        for half in range(2):
            ys = [yg_ref[gl, pl.ds(r0, S5_RB_OUT), half * LANES:(half + 1) * LANES] for gl in range(S5_OCT)]
            for tt, acc in enumerate(block_transpose(ys)):
                rows = tok_rows(r0, S5_OCT * half + tt, S5_RB_OUT)
                y_ref[rows, :] = acc + d_ref[...] * u_ref[rows, :]
        return carry

    lax.fori_loop(0, S5_ROWS // S5_RB_OUT, scatter, 0)


def _s5_core(u, prep, h0, d_skip):
    w_intra, w_state, w_out, apow = prep
    g, n2 = S5_GROUPS, 2 * S5_N
    spec4 = lambda r, c: pl.BlockSpec((2, S5_OCT, r, c), lambda i: (0, i, 0, 0))
    slab = pl.BlockSpec((TOK, LANES), lambda i: (0, i))
    return pl.pallas_call(
        _s5_core_kernel,
        out_shape=[jax.ShapeDtypeStruct((TOK, S5_WIDTH), F32),
                   jax.ShapeDtypeStruct((g, 2, NB_P, n2), F32)],
        grid=(g // S5_OCT,),
        in_specs=[slab,
                  pl.BlockSpec((S5_OCT, S5_CW, S5_CW), lambda i: (i, 0, 0)),
                  spec4(S5_CW, n2), spec4(n2, S5_CW), spec4(16, n2),
                  pl.BlockSpec((S5_OCT, 2, 8, n2), lambda i: (i, 0, 0, 0)),
                  pl.BlockSpec((1, LANES), lambda i: (0, i))],
        out_specs=[slab, pl.BlockSpec((S5_OCT, 2, NB_P, n2), lambda i: (i, 0, 0, 0))],
        scratch_shapes=[pltpu.VMEM((S5_OCT, S5_ROWS, S5_CW), F32), pltpu.VMEM((S5_OCT, S5_ROWS, S5_CW), F32),
                        pltpu.VMEM((2, 2, S5_ROWS, n2), F32)],
        compiler_params=_params(1, VMEM_LIMIT),
        name="s5_scan",
    )(u, w_intra, w_state, w_out, apow, h0, d_skip[None])


DF_SCALE = DF_DH ** -0.5
DFW = DF_HEADS * 2 * DF_DH
IN_B = 3 * HY_WIDTH + 2 * DFW + DF_HEADS * DF_V


def _inproj_b_kernel(y_ref, mod_ref, g_ref, win_ref, c_ref, s_ref,
                     hy_ref, q_ref, kp_ref, ks_ref, vp_ref, vs_ref, wbf_ref):
    @pl.when(pl.program_id(0) == 0)
    def _():
        wbf_ref[...] = win_ref[...].astype(BF16)

    mod = _mod_rows(mod_ref)
    h = _modulate(y_ref[...], g_ref[...], mod[3], mod[4]).astype(BF16)
    p = jnp.dot(h, wbf_ref[...], preferred_element_type=F32)
    o1 = 3 * HY_WIDTH
    hy_ref[...] = p[:, :o1]
    q_ref[...] = (_rope(p[:, o1:o1 + DFW], c_ref[...], s_ref[...]) * DF_SCALE).astype(BF16)
    _tok_write(kp_ref, ks_ref, _rope(p[:, o1 + DFW:o1 + 2 * DFW], c_ref[...], s_ref[...]))
    _tok_write(vp_ref, vs_ref, p[:, o1 + 2 * DFW:])


def _inproj_b(y, mods_l, g, w_in):
    cs, sn = _rope_tables(DFW, tuple(range(0, DFW, DF_DH)))
    pos = lambda i: (_pos_index(i), 0)
    k_shapes, k_specs = _split_out(DFW, BF16)
    v_shapes, v_specs = _split_out(DF_HEADS * DF_V, BF16)
    hy_u, q, kp, ks, vp, vs = pl.pallas_call(
        _inproj_b_kernel,
        out_shape=[jax.ShapeDtypeStruct((TOK, 3 * HY_WIDTH), F32), jax.ShapeDtypeStruct((TOK, DFW), BF16)]
                  + k_shapes + v_shapes,
        grid=(NT,),
        in_specs=[pl.BlockSpec((TM, D), _row),
                  _mod_spec(mods_l[1]),
                  _const_spec((1, D)), _const_spec((D, IN_B)),
                  pl.BlockSpec((TM, DFW), pos), pl.BlockSpec((TM, DFW), pos)],
        out_specs=[pl.BlockSpec((TM, 3 * HY_WIDTH), _row), pl.BlockSpec((TM, DFW), _row)] + k_specs + v_specs,
        scratch_shapes=[pltpu.VMEM((D, IN_B), BF16)],
        compiler_params=_params(1, VMEM_LIMIT),
        name="inproj_odd",
    )(y, mods_l[0], g[None], w_in, jnp.asarray(cs), jnp.asarray(sn))
    return hy_u, q, (kp, ks), (vp, vs)


def _diff_attn_kernel(nseg, lam_init, q_ref, lam_ref, sub_ref, *refs):
    o_ref = refs[-1]
    lp = lam_ref[...]
    lam = (jnp.exp(jnp.sum(lp[0:1] * lp[1:2], axis=-1, keepdims=True))
           - jnp.exp(jnp.sum(lp[2:3] * lp[3:4], axis=-1, keepdims=True)) + lam_init)
    tq = q_ref.shape[0]
    lane = lax.broadcasted_iota(jnp.int32, (tq, LANES), 1)
    for pair in range(DF_HEADS // 2):
        cs = slice(pair * LANES, (pair + 1) * LANES)
        q = q_ref[:, cs]
        ks = [refs[2 * s][:, cs].astype(BF16) for s in range(nseg)]
        vs = [refs[2 * s + 1][:, cs].astype(BF16) for s in range(nseg)]
        outs = []
        for hh in range(2):
            parts = []
            for half in range(2):
                unit = 2 * hh + half
                qm = jnp.where((lane >> 5) == unit, q, jnp.zeros_like(q))
                scores = [_dot_nt(qm, k) for k in ks]
                m = functools.reduce(jnp.maximum, [jnp.max(s, axis=-1, keepdims=True) for s in scores])
                es = [jnp.exp(s - m) for s in scores]
                l = functools.reduce(jnp.add, [jnp.sum(e, axis=-1, keepdims=True) for e in es])
                pv = functools.reduce(jnp.add, [_dot(e, v) for e, v in zip(es, vs)])
                parts.append(pv * (1.0 / l))
            o = parts[0] - lam * parts[1]
            mine = (lane >> 6) == hh
            ms = jnp.sum(jnp.where(mine, o * o, 0.0), axis=-1, keepdims=True) * (1.0 / DF_V)
            outs.append(o * lax.rsqrt(ms + EPS))
        o = jnp.where(lane < DF_V, outs[0], outs[1]) * sub_ref[...] * (1.0 - lam_init)
        o_ref[:, cs] = o.astype(o_ref.dtype)


def _diff_attention(q, k, v, lam_p, subln, lam_init, n_batch, seq, tq, row0, ctx=None):
    qt = seq // tq
    qb0, kb0 = row0 // tq, 0
    in_specs = [pl.BlockSpec((tq, DFW), lambda b, j: (qb0 + b * qt + j, 0)),
                pl.BlockSpec((4, DF_DH), lambda b, j: (0, 0)),
                pl.BlockSpec((1, LANES), lambda b, j: (0, 0))]
    args = [q, lam_p, jnp.concatenate([subln, subln])[None]]
    segs = []
    if ctx is not None:
        segs.append((ctx, PAST, 0))
    segs.append(((k, v), seq, kb0))
    for (a_k, a_v), ln, off in segs:
        idx = lambda b, j, off=off: (off + b, 0)
        in_specs += [pl.BlockSpec((ln, DFW), idx), pl.BlockSpec((ln, DF_HEADS * DF_V), idx)]
        args += [a_k, a_v]
    return pl.pallas_call(
        functools.partial(_diff_attn_kernel, len(segs), lam_init),
        out_shape=jax.ShapeDtypeStruct((n_batch * seq, DF_HEADS * DF_V), BF16),
        grid=(n_batch, qt),
        in_specs=in_specs,
        out_specs=pl.BlockSpec((tq, DF_HEADS * DF_V), lambda b, j: (b * qt + j, 0)),
        compiler_params=_params(2, VMEM_LIMIT),
        name="diff_attention",
    )(*args)


def _hy_filter_kernel(feat_ref, w1_ref, b1_ref, w2_ref, b2_ref, fq_ref, w3_ref, dec_ref, o_ref):
    feat = feat_ref[...]
    fq = fq_ref[...]
    h = jnp.sin(fq * (_dot3(feat, w1_ref[...]) + b1_ref[...]))
    h = jnp.sin(fq * (_dot3(h, w2_ref[...]) + b2_ref[...]))
    window = jnp.exp(-feat[:, 0:1] * jnp.abs(dec_ref[...]))
    for j in range(4):
        cs = slice(j * HY_WIDTH, (j + 1) * HY_WIDTH)
        o_ref[:, cs] = _dot3(h, w3_ref[:, cs]) * window


def _hy_spectrum_kernel(L, cs_ref, hf_ref, hb_ref, o_ref):
    row = lax.broadcasted_iota(jnp.int32, (L, HY_WIDTH), 0)
    first = row == 0
    tf = _dot(cs_ref[...], hf_ref[...])
    tb = _dot(cs_ref[...], jnp.where(first, 0.0, hb_ref[...]))
    ka = tf[:L] + tb[:L]
    kb = jnp.where(first, tf[L:] + tb[L:], tf[L:] - tb[L:])
    wv = jnp.where(first, 1.0 / (2 * L), 2.0 / (2 * L))
    o_ref[0, 0] = ka * wv
    o_ref[0, 1] = jnp.where(first, 0.0, kb) * wv
    o_ref[0, 2] = jnp.where(first, kb, ka) * wv


HY_CH = 256


def _hy_conv_kernel(L, cs_ref, ct_ref, kf_ref, v_ref, x1_ref, x2_ref,
                    wv_ref, w1_ref, w2_ref, bias_ref, o_ref):
    row = lax.broadcasted_iota(jnp.int32, (L, HY_CH), 0)

    def short(x, w):
        prev = jnp.where(row >= 1, pltpu.roll(x, 1, 0), 0.0)
        nxt = jnp.where(row <= L - 2, pltpu.roll(x, L - 1, 0), 0.0)
        return w[0:1] * prev + w[1:2] * x + w[2:3] * nxt

    for j in range(v_ref.shape[0] // L):
        rs = slice(j * L, (j + 1) * L)
        for k in range(HY_WIDTH // HY_CH):
            ch = slice(k * HY_CH, (k + 1) * HY_CH)
            z = short(v_ref[rs, ch], wv_ref[:, ch])
            gates = (short(x1_ref[rs, ch], w1_ref[:, ch]), short(x2_ref[rs, ch], w2_ref[:, ch]))
            for n in range(2):
                ab = _dot(cs_ref[...], z)
                a, b = ab[:L], ab[L:]
                ka, kb1, ka2 = kf_ref[n, 0, :, ch], kf_ref[n, 1, :, ch], kf_ref[n, 2, :, ch]
                pq = jnp.concatenate([a * ka - b * kb1, a * kb1 + b * ka2], axis=0)
                conv = _dot(ct_ref[...], pq)
                z = gates[n] * (conv + bias_ref[n:n + 1, ch] * z)
            o_ref[rs, ch] = z.astype(o_ref.dtype)


def _hyena_spectrum(L, phy):
    conv_w, w1, b1, w2, b2, freq, w3, decay, bias = phy
    feat = jnp.asarray(_hyena_features(L))
    w1p = jnp.pad(w1, ((0, LANES - HY_EMB), (0, 0)))
    filt = pl.pallas_call(
        _hy_filter_kernel,
        out_shape=jax.ShapeDtypeStruct((L, 4 * HY_WIDTH), F32),
        grid=(1,),
        in_specs=[_const_spec((L, LANES)), _const_spec((LANES, HY_FH)), _const_spec((1, HY_FH)),
                  _const_spec((HY_FH, HY_FH)), _const_spec((1, HY_FH)), _const_spec((1, HY_FH)),
                  _const_spec((HY_FH, 4 * HY_WIDTH)), _const_spec((1, HY_WIDTH))],
        out_specs=pl.BlockSpec((L, 4 * HY_WIDTH), lambda i: (0, 0)),
        compiler_params=_params(1, VMEM_LIMIT),
        name="hyena_filter",
    )(feat, w1p, b1[None], w2, b2[None], freq[None], w3, decay[None])
    cs = jnp.asarray(_dft_tables(L)[0]).astype(BF16)
    return pl.pallas_call(
        functools.partial(_hy_spectrum_kernel, L),
        out_shape=jax.ShapeDtypeStruct((2, 3, L, HY_WIDTH), F32),
        grid=(2,),
        in_specs=[_const_spec((2 * L, L)),
                  pl.BlockSpec((L, HY_WIDTH), lambda n: (0, n)),
                  pl.BlockSpec((L, HY_WIDTH), lambda n: (0, 2 + n))],
        out_specs=pl.BlockSpec((1, 3, L, HY_WIDTH), lambda n: (n, 0, 0, 0)),
        compiler_params=_params(1, VMEM_LIMIT),
        name="hyena_spectrum",
    )(cs, filt, filt)


def _hyena_conv(hy_u, spec, phy, n_batch, L, seqs, row0):
    conv_w, bias = phy[0], phy[8]
    cs, ct = (jnp.asarray(t).astype(BF16) for t in _dft_tables(L))
    rows = seqs * L
    rb0 = row0 // rows
    col = lambda off: (lambda b: (0, off))
    tok = lambda off: (lambda b: (rb0 + b, off))
    blk = lambda idx: pl.BlockSpec((rows, HY_WIDTH), idx)
    return pl.pallas_call(
        functools.partial(_hy_conv_kernel, L),
        out_shape=jax.ShapeDtypeStruct((n_batch * L, HY_WIDTH), BF16),
        grid=(n_batch // seqs,),
        in_specs=[_const_spec((2 * L, L)), _const_spec((L, 2 * L)), _const_spec((2, 3, L, HY_WIDTH)),
                  blk(tok(0)), blk(tok(1)), blk(tok(2)),
                  pl.BlockSpec((3, HY_WIDTH), col(0)), pl.BlockSpec((3, HY_WIDTH), col(1)),
                  pl.BlockSpec((3, HY_WIDTH), col(2)), _const_spec((2, HY_WIDTH))],
        out_specs=blk(lambda b: (b, 0)),
        compiler_params=_params(1, VMEM_LIMIT),
        name="hyena_conv",
    )(cs, ct, spec, hy_u, hy_u, hy_u, conv_w, conv_w, conv_w, bias)


def _even_mixer(y, mods_l, g, pa, ps5, ctx_ckv, ctx_krope, ctx_state):
    w_in, w_out, q_norm, w_uq, kv_norm, w_ukv = pa
    a_re, a_im, log_step, b_re, b_im, c_re, c_im, d_skip, w_glu = ps5
    q, ckv, kr_unrot, kr_rot, kn, v, u, (w_k, w_v) = _inproj_a(y, mods_l, g, w_in, q_norm, w_uq, kv_norm, w_ukv)

    ctx_flat = ctx_ckv.reshape(NB_S * PAST, MLA_KV_RANK)
    ctx_kn = _linear(ctx_flat, w_k, PAST, BF16)
    ctx_v = _linear(ctx_flat, w_v, PAST, BF16)
    ctx_kr = jnp.pad(ctx_krope.reshape(NB_S * PAST, MLA_ROPE),
                     ((0, 0), (KR_AT, LANES - KR_AT - MLA_ROPE))).astype(BF16)
    att_p = _mla_attention(q, kn, kr_rot, v, NB_P, L_P, L_P, 0)
    att_s = _mla_attention(q, kn, kr_rot, v, NB_S, L_S, TM, TOK_P, ctx=(ctx_kn, ctx_kr, ctx_v))

    prep = _s5_prep(a_re, a_im, log_step, b_re, b_im, c_re, c_im)
    h0 = ctx_state.transpose(3, 1, 0, 2, 4).reshape(S5_GROUPS, 2, NB_S, 2 * S5_N)
    h0 = jnp.pad(h0, ((0, 0), (0, 0), (0, 8 - NB_S), (0, 0)))
    s5y, fin = _s5_core(u, prep, h0, d_skip)

    mixer = ((att_p, att_s), s5y, w_out, w_glu)
    new_ckv = ckv.reshape(NB_P, L_P, MLA_KV_RANK)
    new_krope = kr_unrot[:, KR_AT:KR_AT + MLA_ROPE].reshape(NB_P, L_P, MLA_ROPE)
    new_state = fin.reshape(S5_GROUPS, 2, NB_P, 2, S5_N).transpose(2, 1, 3, 0, 4)
    return mixer, new_ckv, new_krope, new_state


def _odd_mixer(y, mods_l, g, pb, phy, ctx_k, ctx_v, lam_init):
    w_in, w_out, lam_p, subln = pb
    hy_u, q, (k_p, k_s), (v_p, v_s) = _inproj_b(y, mods_l, g, w_in)
    hy_p = _hyena_conv(hy_u, _hyena_spectrum(L_P, phy), phy, NB_P, L_P, 2, 0)
    hy_s = _hyena_conv(hy_u, _hyena_spectrum(L_S, phy), phy, NB_S, L_S, 1, TOK_P)
    ctx = (ctx_k.reshape(NB_S * PAST, DFW), ctx_v.reshape(NB_S * PAST, DF_HEADS * DF_V))
    att_p = _diff_attention(q, k_p, v_p, lam_p, subln, lam_init, NB_P, L_P, L_P, 0)
    att_s = _diff_attention(q, k_s, v_s, lam_p, subln, lam_init, NB_S, L_S, TM // 2, TOK_P, ctx=ctx)
    mixer = ((hy_p, hy_s), (att_p, att_s), w_out, None)
    new_k = k_p.reshape(NB_P, L_P, DF_HEADS, 2, DF_DH)
    new_v = v_p.reshape(NB_P, L_P, DF_HEADS, DF_V)
    return mixer, new_k, new_v


def kernel(x_prompt, x_sample, c, c_ctx, cache_mla_ckv, cache_mla_krope, state_s5, cache_diff_k, cache_diff_v, ada_w, ada_b, norm_g, ff_w_in, ff_w_out, w_in_a, w_out_a, mla_q_norm, mla_w_uq, mla_kv_norm, mla_w_ukv, s5_a_re, s5_a_im, s5_log_step, s5_b_re, s5_b_im, s5_c_re, s5_c_im, s5_d, s5_w_glu, w_in_b, w_out_b, hy_conv, hy_w1, hy_b1, hy_w2, hy_b2, hy_freq, hy_w3, hy_decay, hy_bias, df_lambda, df_subln, final_norm):
    depth = ada_w.shape[0]
    y = (x_prompt.reshape(TOK_P, D), x_sample.reshape(TOK_S, D))
    mods = _adaln(jnp.concatenate([c_ctx[None], c], axis=0), ada_w, ada_b)
    new_ckv, new_krope, new_s5, new_dk, new_dv = [], [], [], [], []
    for l in range(depth):
        y = _half_ffn(y, (mods, l), norm_g[l, 0], ff_w_in, ff_w_out, l, 0)
        if l % 2 == 0:
            e = l // 2
            pa = (w_in_a[e], w_out_a[e], mla_q_norm[e], mla_w_uq[e], mla_kv_norm[e], mla_w_ukv[e])
            ps5 = (s5_a_re[e], s5_a_im[e], s5_log_step[e], s5_b_re[e], s5_b_im[e],
                   s5_c_re[e], s5_c_im[e], s5_d[e], s5_w_glu[e])
            mixer, ckv, krope, st = _even_mixer(y, (mods, l), norm_g[l, 1], pa, ps5, cache_mla_ckv[:, e],
                                            cache_mla_krope[:, e], state_s5[:, e])
            new_ckv.append(ckv)
            new_krope.append(krope)
            new_s5.append(st)
        else:
            o = l // 2
            lam_init = 0.8 - 0.6 * math.exp(-0.3 * l)
            pb = (w_in_b[o], w_out_b[o], df_lambda[o], df_subln[o])
            phy = (hy_conv[o], hy_w1[o], hy_b1[o], hy_w2[o], hy_b2[o], hy_freq[o],
                   hy_w3[o], hy_decay[o], hy_bias[o])
            mixer, dk, dv = _odd_mixer(y, (mods, l), norm_g[l, 1], pb, phy, cache_diff_k[:, o],
                                   cache_diff_v[:, o], lam_init)
            new_dk.append(dk)
            new_dv.append(dv)
        last = l == depth - 1
        y = _half_ffn(y, (mods, l), norm_g[l, 2], ff_w_in, ff_w_out, l, 1,
                      final_g=final_norm if last else None, mixer=mixer)
    y_prompt = y[0].reshape(NB_P, L_P, D)
    y_sample = y[1].reshape(NB_S, L_S, D)
    return (y_prompt, y_sample, jnp.stack(new_ckv, axis=1), jnp.stack(new_krope, axis=1),
            jnp.stack(new_s5, axis=1), jnp.stack(new_dk, axis=1), jnp.stack(new_dv, axis=1))
```

```python
import functools
import math

import numpy as np
import jax
import jax.numpy as jnp
from jax import lax
from jax.experimental import pallas as pl
from jax.experimental.pallas import tpu as pltpu

F32 = jnp.float32
BF16 = jnp.bfloat16

D = 1024
NB_P, L_P = 16, 256
NB_S, L_S = 2, 1024
PAST = 256
GRID_W = 64
N_MOD = 9
FF = 2816
EPS = 1e-6
ROPE_BASE = 10000.0

MLA_HEADS, MLA_NOPE, MLA_ROPE, MLA_V = 8, 64, 32, 64
MLA_Q_RANK, MLA_KV_RANK = 384, 256
S5_WIDTH, S5_GROUP, S5_N = 512, 16, 64
S5_GROUPS = S5_WIDTH // S5_GROUP
HY_WIDTH, HY_BANDS, HY_FH = 512, 16, 64
HY_EMB = 2 * HY_BANDS + 1
DF_HEADS, DF_DH = 8, 32
DF_V = 2 * DF_DH

TOK_P = NB_P * L_P
TOK_S = NB_S * L_S
TOK = TOK_P + TOK_S
TM = 512
NT = TOK // TM
NT_P = TOK_P // TM
TILES_PER_SAMPLE = L_S // TM

LANES = 128
S5_T = 16
S5_CW = S5_T * S5_GROUP
CH_P = L_P // S5_T
CH_S = L_S // S5_T
S5_ROWS = NB_P * CH_P + NB_S * CH_S
S5_ROWS_P = NB_P * CH_P

VMEM_LIMIT = 56 * 1024 * 1024


def _params(n_grid, vmem=None):
    return pltpu.CompilerParams(dimension_semantics=("arbitrary",) * n_grid,
                                vmem_limit_bytes=vmem)


def _const_spec(shape):
    nd = len(shape)
    return pl.BlockSpec(shape, lambda *_: (0,) * nd, pipeline_mode=pl.Buffered(1))


def _mod_index(i):
    return jnp.where(i < NT_P, 0, 1 + (i - NT_P) // TILES_PER_SAMPLE)


def _mod_spec(layer):
    return pl.BlockSpec((1, 8, N_MOD * D), lambda *_: (layer, 0, 0), pipeline_mode=pl.Buffered(1))


def _mod_rows(mod_ref):
    row = mod_ref[0, pl.ds(_mod_index(pl.program_id(0)), 1), :]
    return [row[:, k * D:(k + 1) * D] for k in range(N_MOD)]


def _pos_index(i):
    return jnp.where(i < NT_P, 0, 1 + (i - NT_P) % TILES_PER_SAMPLE)


def _row(i):
    return (i, 0)


def _row_p(i):
    return (jnp.minimum(i, NT_P - 1), 0)


def _row_s(i):
    return (jnp.maximum(i - NT_P, 0), 0)


def _tok_specs(x, width):
    if isinstance(x, tuple):
        return [pl.BlockSpec((TM, width), _row_p), pl.BlockSpec((TM, width), _row_s)], list(x)
    return [pl.BlockSpec((TM, width), _row)], [x]


def _tok_read(refs, split):
    if split:
        return jnp.where(pl.program_id(0) < NT_P, refs[0][...], refs[1][...]), refs[2:]
    return refs[0][...], refs[1:]


def _tok_write(p_ref, s_ref, value):
    i = pl.program_id(0)

    @pl.when(i < NT_P)
    def _():
        p_ref[...] = value

    @pl.when(i >= NT_P)
    def _():
        s_ref[...] = value.astype(s_ref.dtype)


def _split_out(width, sample_dtype=F32):
    shapes = [jax.ShapeDtypeStruct((TOK_P, width), F32), jax.ShapeDtypeStruct((TOK_S, width), sample_dtype)]
    specs = [pl.BlockSpec((TM, width), _row_p), pl.BlockSpec((TM, width), _row_s)]
    return shapes, specs


def _dot(a, b):
    return jnp.dot(a.astype(BF16), b.astype(BF16), preferred_element_type=F32)


def _dot_nt(a, b):
    return lax.dot_general(a, b, (((1,), (1,)), ((), ())), preferred_element_type=F32)


def _split(x):
    hi = x.astype(BF16)
    lo = (x - hi.astype(F32)).astype(BF16)
    return hi, lo


def _dot3(a, b):
    ah, al = _split(a)
    bh, bl = _split(b)
    d = functools.partial(jnp.dot, preferred_element_type=F32)
    return d(ah, bh) + d(ah, bl) + d(al, bh)


def _rmsnorm(x, g):
    return x * lax.rsqrt(jnp.mean(x * x, axis=-1, keepdims=True) + EPS) * g


def _modulate(y, g, shift, scale):
    return _rmsnorm(y, g) * (1.0 + scale) + shift


def _pair_swap(x):
    n = x.shape[-1]
    lane = lax.broadcasted_iota(jnp.int32, x.shape, x.ndim - 1)
    return jnp.where((lane & 1) == 0, pltpu.roll(x, n - 1, x.ndim - 1), pltpu.roll(x, 1, x.ndim - 1))


def _rope(x, cos, sin_signed):
    return x * cos + _pair_swap(x) * sin_signed


def _rope_angles():
    n_freq = MLA_ROPE // 4
    inv = 1.0 / (ROPE_BASE ** (np.arange(n_freq, dtype=np.float64) / n_freq))
    pos = np.arange(L_S)
    row = (pos // GRID_W).astype(np.float64)
    col = (pos % GRID_W).astype(np.float64)
    ang = np.concatenate([row[:, None] * inv, col[:, None] * inv], axis=-1)
    return np.cos(ang), np.sin(ang)


@functools.lru_cache(maxsize=None)
def _rope_tables(width, starts):
    cos, sin = _rope_angles()
    c = np.ones((TM + L_S, width), np.float32)
    s = np.zeros((TM + L_S, width), np.float32)
    sign = np.where(np.arange(MLA_ROPE) % 2 == 0, -1.0, 1.0)
    unit_c = np.repeat(cos, 2, axis=1)
    unit_s = np.repeat(sin, 2, axis=1) * sign
    for st in starts:
        c[TM:, st:st + MLA_ROPE] = unit_c
        s[TM:, st:st + MLA_ROPE] = unit_s
    return c, s


@functools.lru_cache(maxsize=None)
def _dft_tables(L):
    f = np.arange(L)[:, None]
    s = np.arange(L)[None, :]
    ang = np.pi * ((f * s) % (2 * L)).astype(np.float64) / L
    cs = np.concatenate([np.cos(ang), np.sin(ang)], axis=0)
    cs[L, :] = np.where(np.arange(L) % 2 == 0, 1.0, -1.0)
    cs = cs.astype(np.float32)
    return cs, np.ascontiguousarray(cs.T)


@functools.lru_cache(maxsize=None)
def _hyena_features(L):
    t = np.arange(L, dtype=np.float64) / L
    bands = np.arange(1, HY_BANDS + 1, dtype=np.float64)
    ang = 2.0 * math.pi * t[:, None] * bands
    feat = np.zeros((L, LANES), np.float32)
    feat[:, 0] = t
    feat[:, 1:1 + HY_BANDS] = np.cos(ang)
    feat[:, 1 + HY_BANDS:HY_EMB] = np.sin(ang)
    return feat


def _adaln_kernel(c_ref, w_ref, b_ref, o_ref):
    s = jax.nn.silu(c_ref[...])
    s_hi = s.astype(BF16).astype(F32)
    stacked = jnp.concatenate([s_hi, s - s_hi], axis=0).astype(BF16)
    wh, wl = _split(w_ref[0])
    both = jnp.dot(stacked, wh, preferred_element_type=F32)
    rows = c_ref.shape[0]
    o_ref[0] = both[:rows] + both[rows:] + jnp.dot(stacked, wl, preferred_element_type=F32)[:rows] + b_ref[0]


def _adaln(cvecs, ada_w, ada_b):
    depth = ada_w.shape[0]
    n_vec = cvecs.shape[0]
    tn = N_MOD * D // 4
    out = pl.pallas_call(
        _adaln_kernel,
        out_shape=jax.ShapeDtypeStruct((depth, 8, N_MOD * D), F32),
        grid=(depth, N_MOD * D // tn),
        in_specs=[pl.BlockSpec((8, D), lambda l, j: (0, 0)),
                  pl.BlockSpec((1, D, tn), lambda l, j: (l, 0, j)),
                  pl.BlockSpec((1, 1, tn), lambda l, j: (l, 0, j))],
        out_specs=pl.BlockSpec((1, 8, tn), lambda l, j: (l, 0, j)),
        compiler_params=_params(2, VMEM_LIMIT),
        name="adaln",
    )(jnp.pad(cvecs, ((0, 8 - n_vec), (0, 0))), ada_w, ada_b[:, None, :])
    return out


FF_PIECE = 256
FF_LOADS = FF // FF_PIECE


def _ffn_kernel(base, final, split_in, mixer, layer, which, *refs):
    y, refs = _tok_read(refs, split_in)
    if mixer is not None:
        a1, refs = _tok_read(refs, mixer[0])
        a2, refs = _tok_read(refs, mixer[1])
        wmix_ref, wg_ref = refs[:2]
        refs = refs[2:]
    mod_ref, g_ref, win_hbm, wout_hbm, fg_ref = refs[:5]
    n_out = 2 if final else 1
    outs = refs[5:5 + n_out]
    win_ref, wout_ref, stage_g, stage_u, stage_o, sems = refs[5 + n_out:]
    mod = _mod_rows(mod_ref)
    if mixer is not None:
        if mixer[2]:
            a2 = jax.nn.gelu(a2)
            a2 = a2 * jax.nn.sigmoid(_dot(a2, wg_ref[...]))
        k1 = wmix_ref.shape[0] // 2
        y = y + mod[5] * (_dot(a1, wmix_ref[:k1]) + _dot(a2, wmix_ref[k1:]))
    h = _modulate(y, g_ref[...], mod[base], mod[base + 1]).astype(BF16)

    def hidden(lo, width):
        gate = jnp.dot(h, win_ref[:, lo:lo + width], preferred_element_type=F32)
        up = jnp.dot(h, win_ref[:, FF + lo:FF + lo + width], preferred_element_type=F32)
        a = (jax.nn.silu(gate) * up).astype(BF16)
        return jnp.dot(a, wout_ref[lo:lo + width, :], preferred_element_type=F32)

    def finish(acc):
        out = y + 0.5 * mod[base + 2] * acc
        if final:
            _tok_write(outs[0], outs[1], _rmsnorm(out, fg_ref[...]))
        else:
            outs[0][...] = out

    @pl.when(pl.program_id(0) == 0)
    def _():
        def copies(c, slot):
            cols = pl.ds(c * FF_PIECE, FF_PIECE)
            return (pltpu.make_async_copy(win_hbm.at[layer, which, :, cols], stage_g.at[slot], sems.at[0, slot]),
                    pltpu.make_async_copy(win_hbm.at[layer, which, :, pl.ds(FF + c * FF_PIECE, FF_PIECE)],
                                          stage_u.at[slot], sems.at[1, slot]),
                    pltpu.make_async_copy(wout_hbm.at[layer, which, cols, :], stage_o.at[slot], sems.at[2, slot]))

        for cp in copies(0, 0):
            cp.start()
        acc = jnp.zeros(y.shape, F32)
        for c in range(FF_LOADS):
            slot = c % 2
            lo = c * FF_PIECE
            if c + 1 < FF_LOADS:
                for cp in copies(c + 1, 1 - slot):
                    cp.start()
            for cp in copies(c, slot):
                cp.wait()
            win_ref[:, lo:lo + FF_PIECE] = stage_g[slot].astype(BF16)
            win_ref[:, FF + lo:FF + lo + FF_PIECE] = stage_u[slot].astype(BF16)
            wout_ref[lo:lo + FF_PIECE, :] = stage_o[slot].astype(BF16)
            acc = acc + hidden(lo, FF_PIECE)
        finish(acc)

    @pl.when(pl.program_id(0) > 0)
    def _():
        finish(hidden(0, FF))


def _half_ffn(y, mods_l, g, ff_w_in, ff_w_out, layer, which, final_g=None, mixer=None):
    final = final_g is not None
    fg = final_g if final else g
    y_specs, y_args = _tok_specs(y, D)
    mix_flags = None
    if mixer is not None:
        a1, a2, w_out, w_glu = mixer
        k1 = w_out.shape[0] // 2
        wg = w_glu if w_glu is not None else jnp.zeros((8, LANES), F32)
        s1, a1_args = _tok_specs(a1, k1)
        s2, a2_args = _tok_specs(a2, k1)
        y_specs = y_specs + s1 + s2 + [_const_spec(w_out.shape), _const_spec(wg.shape)]
        y_args = y_args + a1_args + a2_args + [w_out, wg]
        mix_flags = (isinstance(a1, tuple), isinstance(a2, tuple), w_glu is not None)
    if final:
        out_shape, out_specs = _split_out(D)
    else:
        out_shape, out_specs = jax.ShapeDtypeStruct((TOK, D), F32), pl.BlockSpec((TM, D), _row)
    return pl.pallas_call(
        functools.partial(_ffn_kernel, 6 * which, final, isinstance(y, tuple), mix_flags, layer, which),
        out_shape=out_shape,
        grid=(NT,),
        in_specs=y_specs + [_mod_spec(mods_l[1]),
                            _const_spec((1, D)),
                            pl.BlockSpec(memory_space=pl.ANY),
                            pl.BlockSpec(memory_space=pl.ANY),
                            _const_spec((1, D))],
        out_specs=out_specs,
        scratch_shapes=[pltpu.VMEM((D, 2 * FF), BF16), pltpu.VMEM((FF, D), BF16),
                        pltpu.VMEM((2, D, FF_PIECE), F32), pltpu.VMEM((2, D, FF_PIECE), F32),
                        pltpu.VMEM((2, FF_PIECE, D), F32), pltpu.SemaphoreType.DMA((3, 2))],
        compiler_params=_params(1, VMEM_LIMIT),
        name="half_ffn",
    )(*y_args, mods_l[0], g[None], ff_w_in, ff_w_out, fg[None])


def _linear_kernel(x_ref, w_ref, o_ref):
    o_ref[...] = _dot(x_ref[...], w_ref[...]).astype(o_ref.dtype)


def _linear(x, w, tm, out_dtype):
    m, k = x.shape
    n = w.shape[1]
    return pl.pallas_call(
        _linear_kernel,
        out_shape=jax.ShapeDtypeStruct((m, n), out_dtype),
        grid=(m // tm,),
        in_specs=[pl.BlockSpec((tm, k), lambda i: (i, 0)), _const_spec((k, n))],
        out_specs=pl.BlockSpec((tm, n), lambda i: (i, 0)),
        compiler_params=_params(1),
        name="linear",
    )(x, w.astype(BF16))


MLA_SCALE = (MLA_NOPE + MLA_ROPE) ** -0.5
QW = MLA_HEADS * LANES
KR_AT = MLA_NOPE
IN_A_PAD = MLA_Q_RANK + MLA_KV_RANK + S5_WIDTH + LANES


def _inproj_a_kernel(y_ref, mod_ref, g_ref, win_ref, qn_ref, wuq_ref, kvn_ref, wk_ref, wv_ref,
                     cq_ref, sq_ref, ck_ref, sk_ref,
                     q_ref, ckv_ref, kru_ref, krr_ref, kn_ref, v_ref, u_ref):
    mod = _mod_rows(mod_ref)
    h = _modulate(y_ref[...], g_ref[...], mod[3], mod[4]).astype(BF16)
    p = jnp.dot(h, win_ref[...], preferred_element_type=F32)
    o1 = MLA_Q_RANK
    o2 = o1 + MLA_KV_RANK
    o3 = o2 + S5_WIDTH
    q = _dot(_rmsnorm(p[:, :o1], qn_ref[...]), wuq_ref[...])
    q_ref[...] = (_rope(q, cq_ref[...], sq_ref[...]) * MLA_SCALE).astype(BF16)
    ckv = _rmsnorm(p[:, o1:o2], kvn_ref[...])
    ckv_b = ckv.astype(BF16)
    kn_ref[...] = jnp.dot(ckv_b, wk_ref[...], preferred_element_type=F32).astype(BF16)
    v_ref[...] = jnp.dot(ckv_b, wv_ref[...], preferred_element_type=F32).astype(BF16)
    u_ref[...] = p[:, o2:o3]
    krp = p[:, o3:]
    krr_ref[...] = _rope(krp, ck_ref[...], sk_ref[...]).astype(BF16)

    @pl.when(pl.program_id(0) < NT_P)
    def _():
        ckv_ref[...] = ckv
        kru_ref[...] = krp[:, KR_AT:KR_AT + MLA_ROPE]


def _inproj_a(y, mods_l, g, w_in, q_norm, w_uq, kv_norm, w_ukv):
    o1 = MLA_Q_RANK
    o2 = o1 + MLA_KV_RANK
    o3 = o2 + MLA_ROPE
    kr_cols = jnp.pad(w_in[:, o2:o3], ((0, 0), (KR_AT, LANES - KR_AT - MLA_ROPE)))
    w_ext = jnp.concatenate([w_in[:, :o2], w_in[:, o3:], kr_cols], axis=1).astype(BF16)
    dq = MLA_NOPE + MLA_ROPE
    w_uq_pad = jnp.pad(w_uq.reshape(MLA_Q_RANK, MLA_HEADS, dq),
                       ((0, 0), (0, 0), (0, LANES - dq))).reshape(MLA_Q_RANK, QW).astype(BF16)
    w_kv = w_ukv.reshape(MLA_KV_RANK, MLA_HEADS, MLA_NOPE + MLA_V)
    w_k = jnp.pad(w_kv[:, :, :MLA_NOPE], ((0, 0), (0, 0), (0, LANES - MLA_NOPE))).reshape(MLA_KV_RANK, QW)
    w_v = w_kv[:, :, MLA_NOPE:].reshape(MLA_KV_RANK, MLA_HEADS * MLA_V)
    w_k, w_v = w_k.astype(BF16), w_v.astype(BF16)
    cq, sq = _rope_tables(QW, tuple(h * LANES + MLA_NOPE for h in range(MLA_HEADS)))
    ck, sk = _rope_tables(LANES, (KR_AT,))
    row = _row
    pos = lambda i: (_pos_index(i), 0)
    widths = (QW, MLA_KV_RANK, MLA_ROPE, LANES, QW, MLA_HEADS * MLA_V, S5_WIDTH)
    prompt_only = (1, 2)
    mxu_only = (0, 3, 4, 5)
    outs = pl.pallas_call(
        _inproj_a_kernel,
        out_shape=[jax.ShapeDtypeStruct((TOK_P if k in prompt_only else TOK, w), BF16 if k in mxu_only else F32)
                   for k, w in enumerate(widths)],
        grid=(NT,),
        in_specs=[pl.BlockSpec((TM, D), row),
                  _mod_spec(mods_l[1]),
                  _const_spec((1, D)),
                  _const_spec((D, IN_A_PAD)),
                  _const_spec((1, MLA_Q_RANK)),
                  _const_spec((MLA_Q_RANK, QW)),
                  _const_spec((1, MLA_KV_RANK)),
                  _const_spec((MLA_KV_RANK, QW)),
                  _const_spec((MLA_KV_RANK, MLA_HEADS * MLA_V)),
                  pl.BlockSpec((TM, QW), pos), pl.BlockSpec((TM, QW), pos),
                  pl.BlockSpec((TM, LANES), pos), pl.BlockSpec((TM, LANES), pos)],
        out_specs=[pl.BlockSpec((TM, w), _row_p if k in prompt_only else row)
                   for k, w in enumerate(widths)],
        compiler_params=_params(1, VMEM_LIMIT),
        name="inproj_even",
    )(y, mods_l[0], g[None], w_ext, q_norm[None], w_uq_pad, kv_norm[None], w_k, w_v,
      jnp.asarray(cq), jnp.asarray(sq), jnp.asarray(ck), jnp.asarray(sk))
    q, ckv, kr_unrot, kr_rot, kn, v, u = outs
    return q, ckv, kr_unrot, kr_rot, kn, v, u, (w_k, w_v)


def _mla_attn_kernel(nseg, q_ref, *refs):
    o_ref = refs[-1]
    tq = q_ref.shape[0]
    lane = lax.broadcasted_iota(jnp.int32, (tq, LANES), 1)
    for pair in range(MLA_HEADS // 2):
        outs = []
        for hh in range(2):
            h = 2 * pair + hh
            hs = slice(h * LANES, (h + 1) * LANES)
            qh = q_ref[:, hs]
            scores = []
            for s in range(nseg):
                kn_ref, kr_ref = refs[3 * s], refs[3 * s + 1]
                kh = (kn_ref[:, hs] + kr_ref[...]).astype(BF16)
                scores.append(_dot_nt(qh, kh))
            m = functools.reduce(jnp.maximum, [jnp.max(s, axis=-1, keepdims=True) for s in scores])
            es = [jnp.exp(s - m) for s in scores]
            l = functools.reduce(jnp.add, [jnp.sum(e, axis=-1, keepdims=True) for e in es])
            o = None
            for s in range(nseg):
                v_ref = refs[3 * s + 2]
                part = _dot(es[s], v_ref[:, pair * LANES:(pair + 1) * LANES])
                o = part if o is None else o + part
            outs.append(o / l)
        o_ref[:, pair * LANES:(pair + 1) * LANES] = jnp.where(lane < MLA_V, outs[0], outs[1]).astype(o_ref.dtype)


def _mla_attention(q, kn, kr, v, n_batch, seq, tq, row0, ctx=None):
    qt = seq // tq
    qb0, kb0 = row0 // tq, row0 // seq
    in_specs = [pl.BlockSpec((tq, QW), lambda b, j: (qb0 + b * qt + j, 0))]
    args = [q]
    segs = []
    if ctx is not None:
        segs.append((ctx, PAST, 0))
    segs.append(((kn, kr, v), seq, kb0))
    for (a_kn, a_kr, a_v), ln, off in segs:
        idx = lambda b, j, off=off: (off + b, 0)
        in_specs += [pl.BlockSpec((ln, QW), idx), pl.BlockSpec((ln, LANES), idx),
                     pl.BlockSpec((ln, MLA_HEADS * MLA_V), idx)]
        args += [a_kn, a_kr, a_v]
    return pl.pallas_call(
        functools.partial(_mla_attn_kernel, len(segs)),
        out_shape=jax.ShapeDtypeStruct((n_batch * seq, MLA_HEADS * MLA_V), BF16),
        grid=(n_batch, qt),
        in_specs=in_specs,
        out_specs=pl.BlockSpec((tq, MLA_HEADS * MLA_V), lambda b, j: (b * qt + j, 0)),
        compiler_params=_params(2, VMEM_LIMIT),
        name="mla_attention",
    )(*args)


def _cpow(ar, ai, e, nbits):
    rr = jnp.ones_like(ar)
    ri = jnp.zeros_like(ar)
    br, bi = ar, ai
    for k in range(nbits):
        bit = ((e >> k) & 1) == 1
        nr = rr * br - ri * bi
        ni = rr * bi + ri * br
        rr = jnp.where(bit, nr, rr)
        ri = jnp.where(bit, ni, ri)
        if k + 1 < nbits:
            br, bi = br * br - bi * bi, 2.0 * br * bi
    return rr, ri


def _s5_abar_kernel(lr_ref, li_ref, ls_ref, o_ref):
    step = jnp.exp(ls_ref[...])
    lr = jnp.minimum(lr_ref[...], -1e-4)
    li = li_ref[...]
    mag = jnp.exp(lr * step)
    ar = mag * jnp.cos(li * step)
    ai = mag * jnp.sin(li * step)
    den = lr * lr + li * li
    o_ref[0] = ar
    o_ref[1] = ai
    o_ref[2] = ((ar - 1.0) * lr + ai * li) / den
    o_ref[3] = (ai * lr - (ar - 1.0) * li) / den


def _s5_prep_kernel(arow_ref, acol_ref, btr_ref, bti_ref, ctr_ref, cti_ref,
                    wi_ref, ws_ref, wo_ref, ap_ref):
    n2 = 2 * S5_N
    blk_o = lax.broadcasted_iota(jnp.int32, (S5_N, S5_CW), 1) >> 4
    lane_k = lax.broadcasted_iota(jnp.int32, (S5_GROUP, S5_CW), 1)
    row_k = lax.broadcasted_iota(jnp.int32, (S5_GROUP, S5_CW), 0)
    lane_b = lax.broadcasted_iota(jnp.int32, (S5_GROUP, n2), 1)
    lane_a = lax.broadcasted_iota(jnp.int32, (1, n2), 1)
    rep = ((lane_k & (S5_GROUP - 1)) == row_k).astype(BF16)

    def tile16(x):
        hi = x.astype(BF16)
        r1 = x - hi.astype(F32)
        mid = r1.astype(BF16)
        lo = (r1 - mid.astype(F32)).astype(BF16)
        d = lambda a: lax.dot_general(a, rep, (((0,), (0,)), ((), ())), preferred_element_type=F32)
        return d(hi) + d(mid) + d(lo)

    intra = [None] * S5_T
    for d in range(2):
        ar, ai, fr, fi = (arow_ref[d, 0, k:k + 1, :] for k in range(4))
        btr, bti = btr_ref[d, 0], bti_ref[d, 0]
        bbr = fr * btr - fi * bti
        bbi = fr * bti + fi * btr
        pws = [(jnp.ones_like(ar), jnp.zeros_like(ar))]
        for _ in range(S5_T):
            pr, pi = pws[-1]
            pws.append((pr * ar - pi * ai, pr * ai + pi * ar))
        for s in range(S5_T):
            pr, pi = pws[S5_T - 1 - s] if d == 0 else pws[s]
            ws_ref[d, 0, s * S5_GROUP:(s + 1) * S5_GROUP, :] = jnp.where(
                lane_b < S5_N, pr * bbr - pi * bbi, pr * bbi + pi * bbr).astype(BF16)

        acol = acol_ref[d, 0]
        arc = jnp.broadcast_to(acol[:, 0:1], (S5_N, S5_CW))
        aic = jnp.broadcast_to(acol[:, 1:2], (S5_N, S5_CW))
        ctr, cti = tile16(ctr_ref[d, 0]), tile16(cti_ref[d, 0])
        e_lag = blk_o if d == 0 else (S5_T - 1 - blk_o)
        pqr, pqi = _cpow(arc, aic, e_lag, 4)
        qr = pqr * ctr - pqi * cti
        qi = pqr * cti + pqi * ctr
        wo_ref[d, 0] = jnp.concatenate([qr * arc - qi * aic, -(qr * aic + qi * arc)], axis=0).astype(BF16)
        q_stack = jnp.concatenate([qr, qi], axis=0)
        bb_mix = jnp.where(lane_b < S5_N, bbr, -bbi)
        kt = _dot3(bb_mix, q_stack)
        for s in range(S5_T):
            if d == 0:
                blk = jnp.where(lane_k >= S5_GROUP * s, pltpu.roll(kt, S5_GROUP * s, 1), 0.0)
            else:
                blk = jnp.where(lane_k < S5_GROUP * (s + 1),
                                pltpu.roll(kt, (S5_GROUP * (s + 1)) % S5_CW, 1), 0.0)
            intra[s] = blk if intra[s] is None else intra[s] + blk

        pr1, pi1 = pws[S5_T]
        for k in range(6):
            ap_ref[d, 0, k:k + 1, :] = pr1
            ap_ref[d, 0, 8 + k:9 + k, :] = jnp.where(lane_a < S5_N, -pi1, pi1)
            pr1, pi1 = pr1 * pr1 - pi1 * pi1, 2.0 * pr1 * pi1
        ap_ref[d, 0, 6:8, :] = jnp.zeros((2, n2), F32)
        ap_ref[d, 0, 14:16, :] = jnp.zeros((2, n2), F32)
    for s in range(S5_T):
        wi_ref[0, s * S5_GROUP:(s + 1) * S5_GROUP, :] = intra[s].astype(BF16)


def _s5_prep(a_re, a_im, log_step, b_re, b_im, c_re, c_im):
    g, n, n2 = S5_GROUPS, S5_N, 2 * S5_N
    abar = pl.pallas_call(
        _s5_abar_kernel,
        out_shape=jax.ShapeDtypeStruct((4, 2 * g, n), F32),
        grid=(1,),
        in_specs=[_const_spec((2 * g, n)), _const_spec((2 * g, n)), _const_spec((2 * g, 1))],
        out_specs=pl.BlockSpec((4, 2 * g, n), lambda i: (0, 0, 0)),
        compiler_params=_params(1),
        name="s5_abar",
    )(a_re.reshape(2 * g, n), a_im.reshape(2 * g, n), log_step.reshape(2 * g, 1))
    abar = jnp.concatenate([abar, abar], axis=-1).reshape(4, 2, g, n2)
    arow = abar.transpose(1, 2, 0, 3)
    acol = abar[:2, :, :, :n].transpose(1, 2, 3, 0)
    bt = lambda b: jnp.concatenate([jnp.swapaxes(b, 2, 3)] * 2, axis=-1)
    spec4 = lambda r, c: pl.BlockSpec((2, 1, r, c), lambda i: (0, i, 0, 0))
    return pl.pallas_call(
        _s5_prep_kernel,
        out_shape=[jax.ShapeDtypeStruct((g, S5_CW, S5_CW), BF16),
                   jax.ShapeDtypeStruct((2, g, S5_CW, n2), BF16),
                   jax.ShapeDtypeStruct((2, g, n2, S5_CW), BF16),
                   jax.ShapeDtypeStruct((2, g, 16, n2), F32)],
        grid=(g,),
        in_specs=[spec4(4, n2), spec4(n, 2),
                  spec4(S5_GROUP, n2), spec4(S5_GROUP, n2), spec4(S5_GROUP, n), spec4(S5_GROUP, n)],
        out_specs=[pl.BlockSpec((1, S5_CW, S5_CW), lambda i: (i, 0, 0)),
                   spec4(S5_CW, n2), spec4(n2, S5_CW), spec4(16, n2)],
        compiler_params=_params(1),
        name="s5_prep",
    )(arow, acol, bt(b_re), bt(b_im), c_re, c_im)


def _cmul_rows(x, p1, p2):
    return x * p1 + pltpu.roll(x, S5_N, 1) * p2


S5_OCT = LANES // S5_GROUP
S5_RB_IN = 96
S5_RB_OUT = 48


def _s5_core_kernel(u_ref, wi_ref, ws_ref, wo_ref, ap_ref, h0_ref, d_ref, y_ref, fin_ref,
                    ug_ref, yg_ref, z_ref):
    n2 = 2 * S5_N

    def tok_rows(r0, t, nrows):
        return pl.ds(r0 * S5_T + t, nrows, stride=S5_T)

    def block_transpose(xs):
        n = S5_OCT
        blk = lax.broadcasted_iota(jnp.int32, xs[0].shape, 1) >> 4
        a = [pltpu.roll(x, i * S5_GROUP, 1) if i else x for i, x in enumerate(xs)]
        ys = []
        for d in range(n):
            diag = a[-d % n]
            for b in range(1, n):
                diag = jnp.where(blk == b, a[(b - d) % n], diag)
            ys.append(pltpu.roll(diag, LANES - d * S5_GROUP, 1) if d else diag)
        return ys

    def gather(rb, carry):
        r0 = pl.multiple_of(rb * S5_RB_IN, S5_RB_IN)
        for half in range(2):
            xs = [u_ref[tok_rows(r0, S5_OCT * half + tt, S5_RB_IN), :] for tt in range(S5_OCT)]
            for gl, x in enumerate(block_transpose(xs)):
                ug_ref[gl, pl.ds(r0, S5_RB_IN), half * LANES:(half + 1) * LANES] = x
        return carry

    lax.fori_loop(0, S5_ROWS // S5_RB_IN, gather, 0)

    r = lax.broadcasted_iota(jnp.int32, (S5_ROWS, n2), 0)
    in_p = r < S5_ROWS_P
    rib = jnp.where(in_p, r & (CH_P - 1), (r - S5_ROWS_P) & (CH_S - 1))
    nch = jnp.where(in_p, CH_P, CH_S)

    def one_group(gl, slot):
        ub = ug_ref[gl].astype(BF16)
        y = jnp.dot(ub, wi_ref[gl], preferred_element_type=F32)
        for d in range(2):
            p1, p2 = ap_ref[d, gl, 0:1, :], ap_ref[d, gl, 8:9, :]
            edge = [S5_ROWS_P + CH_S * b + (0 if d == 0 else CH_S - 1) for b in range(NB_S)]
            h0 = [h0_ref[gl, d, b:b + 1, :] for b in range(NB_S)]
            s = jnp.dot(ub, ws_ref[d, gl], preferred_element_type=F32)
            for b in range(NB_S):
                s = s + jnp.where(r == edge[b], _cmul_rows(h0[b], p1, p2), 0.0)
            for k in range(6):
                sh = 1 << k
                if d == 0:
                    t = jnp.where(rib >= sh, pltpu.roll(s, sh, 0), 0.0)
                else:
                    t = jnp.where(rib < nch - sh, pltpu.roll(s, S5_ROWS - sh, 0), 0.0)
                s = s + _cmul_rows(t, ap_ref[d, gl, k:k + 1, :], ap_ref[d, gl, 8 + k:9 + k, :])
            z_ref[slot, d] = s
            first = CH_P - 1 if d == 0 else 0
            fin_ref[gl, d] = z_ref[slot, d, pl.ds(first, NB_P, stride=CH_P), :]
            if d == 0:
                sp = jnp.where(rib >= 1, pltpu.roll(s, 1, 0), 0.0)
            else:
                sp = jnp.where(rib < nch - 1, pltpu.roll(s, S5_ROWS - 1, 0), 0.0)
            for b in range(NB_S):
                sp = jnp.where(r == edge[b], h0[b], sp)
            y = y + jnp.dot(sp.astype(BF16), wo_ref[d, gl], preferred_element_type=F32)
        yg_ref[gl] = y

    def group_pair(gp, carry):
        for slot in range(2):
            one_group(2 * gp + slot, slot)
        return carry

    lax.fori_loop(0, S5_OCT // 2, group_pair, 0)

    def scatter(rb, carry):
        r0 = pl.multiple_of(rb * S5_RB_OUT, S5_RB_OUT)
        for half in range(2):
            ys = [yg_ref[gl, pl.ds(r0, S5_RB_OUT), half * LANES:(half + 1) * LANES] for gl in range(S5_OCT)]
            for tt, acc in enumerate(block_transpose(ys)):
                rows = tok_rows(r0, S5_OCT * half + tt, S5_RB_OUT)
                y_ref[rows, :] = acc + d_ref[...] * u_ref[rows, :]
        return carry

    lax.fori_loop(0, S5_ROWS // S5_RB_OUT, scatter, 0)


def _s5_core(u, prep, h0, d_skip):
    w_intra, w_state, w_out, apow = prep
    g, n2 = S5_GROUPS, 2 * S5_N
    spec4 = lambda r, c: pl.BlockSpec((2, S5_OCT, r, c), lambda i: (0, i, 0, 0))
    slab = pl.BlockSpec((TOK, LANES), lambda i: (0, i))
    return pl.pallas_call(
        _s5_core_kernel,
        out_shape=[jax.ShapeDtypeStruct((TOK, S5_WIDTH), F32),
                   jax.ShapeDtypeStruct((g, 2, NB_P, n2), F32)],
        grid=(g // S5_OCT,),
        in_specs=[slab,
                  pl.BlockSpec((S5_OCT, S5_CW, S5_CW), lambda i: (i, 0, 0)),
                  spec4(S5_CW, n2), spec4(n2, S5_CW), spec4(16, n2),
                  pl.BlockSpec((S5_OCT, 2, 8, n2), lambda i: (i, 0, 0, 0)),
                  pl.BlockSpec((1, LANES), lambda i: (0, i))],
        out_specs=[slab, pl.BlockSpec((S5_OCT, 2, NB_P, n2), lambda i: (i, 0, 0, 0))],
        scratch_shapes=[pltpu.VMEM((S5_OCT, S5_ROWS, S5_CW), F32), pltpu.VMEM((S5_OCT, S5_ROWS, S5_CW), F32),
                        pltpu.VMEM((2, 2, S5_ROWS, n2), F32)],
        compiler_params=_params(1, VMEM_LIMIT),
        name="s5_scan",
    )(u, w_intra, w_state, w_out, apow, h0, d_skip[None])


DF_SCALE = DF_DH ** -0.5
DFW = DF_HEADS * 2 * DF_DH
IN_B = 3 * HY_WIDTH + 2 * DFW + DF_HEADS * DF_V


def _inproj_b_kernel(y_ref, mod_ref, g_ref, win_ref, c_ref, s_ref,
                     hy_ref, q_ref, kp_ref, ks_ref, vp_ref, vs_ref, vc_ref, wbf_ref):
    @pl.when(pl.program_id(0) == 0)
    def _():
        wbf_ref[...] = win_ref[...].astype(BF16)

    mod = _mod_rows(mod_ref)
    h = _modulate(y_ref[...], g_ref[...], mod[3], mod[4]).astype(BF16)
    p = jnp.dot(h, wbf_ref[...], preferred_element_type=F32)
    o1 = 3 * HY_WIDTH
    hy_ref[...] = p[:, :o1]
    q_ref[...] = (_rope(p[:, o1:o1 + DFW], c_ref[...], s_ref[...]) * DF_SCALE).astype(BF16)
    _tok_write(kp_ref, ks_ref, _rope(p[:, o1 + DFW:o1 + 2 * DFW], c_ref[...], s_ref[...]))
    v = p[:, o1 + 2 * DFW:]
    _tok_write(vp_ref, vs_ref, v.astype(BF16))

    @pl.when(pl.program_id(0) < NT_P)
    def _():
        for hd in range(DF_HEADS):
            vc_ref[pl.ds(hd, TM, stride=DF_HEADS), :] = v[:, hd * DF_V:(hd + 1) * DF_V]


def _inproj_b(y, mods_l, g, w_in):
    cs, sn = _rope_tables(DFW, tuple(range(0, DFW, DF_DH)))
    pos = lambda i: (_pos_index(i), 0)
    k_shapes, k_specs = _split_out(DFW, BF16)
    v_specs = _split_out(DF_HEADS * DF_V)[1]
    v_shapes = [jax.ShapeDtypeStruct((TOK_P, DF_HEADS * DF_V), BF16), jax.ShapeDtypeStruct((TOK_S, DF_HEADS * DF_V), BF16),
                jax.ShapeDtypeStruct((TOK_P * DF_HEADS, DF_V), F32)]
    v_specs = v_specs + [pl.BlockSpec((TM * DF_HEADS, DF_V), _row_p)]
    hy_u, q, kp, ks, vp, vs, v_cache = pl.pallas_call(
        _inproj_b_kernel,
        out_shape=[jax.ShapeDtypeStruct((TOK, 3 * HY_WIDTH), F32), jax.ShapeDtypeStruct((TOK, DFW), BF16)]
                  + k_shapes + v_shapes,
        grid=(NT,),
        in_specs=[pl.BlockSpec((TM, D), _row),
                  _mod_spec(mods_l[1]),
                  _const_spec((1, D)), _const_spec((D, IN_B)),
                  pl.BlockSpec((TM, DFW), pos), pl.BlockSpec((TM, DFW), pos)],
        out_specs=[pl.BlockSpec((TM, 3 * HY_WIDTH), _row), pl.BlockSpec((TM, DFW), _row)] + k_specs + v_specs,
        scratch_shapes=[pltpu.VMEM((D, IN_B), BF16)],
        compiler_params=_params(1, VMEM_LIMIT),
        name="inproj_odd",
    )(y, mods_l[0], g[None], w_in, jnp.asarray(cs), jnp.asarray(sn))
    return hy_u, q, (kp, ks), (vp, vs), v_cache


def _diff_attn_kernel(nseg, lam_init, q_ref, lam_ref, sub_ref, *refs):
    o_ref = refs[-1]
    lp = lam_ref[...]
    lam = (jnp.exp(jnp.sum(lp[0:1] * lp[1:2], axis=-1, keepdims=True))
           - jnp.exp(jnp.sum(lp[2:3] * lp[3:4], axis=-1, keepdims=True)) + lam_init)
    tq = q_ref.shape[0]
    lane = lax.broadcasted_iota(jnp.int32, (tq, LANES), 1)
    for pair in range(DF_HEADS // 2):
        cs = slice(pair * LANES, (pair + 1) * LANES)
        q = q_ref[:, cs]
        ks = [refs[2 * s][:, cs].astype(BF16) for s in range(nseg)]
        vs = [refs[2 * s + 1][:, cs].astype(BF16) for s in range(nseg)]
        outs = []
        for hh in range(2):
            parts = []
            for half in range(2):
                unit = 2 * hh + half
                qm = jnp.where((lane >> 5) == unit, q, jnp.zeros_like(q))
                scores = [_dot_nt(qm, k) for k in ks]
                m = functools.reduce(jnp.maximum, [jnp.max(s, axis=-1, keepdims=True) for s in scores])
                es = [jnp.exp(s - m) for s in scores]
                l = functools.reduce(jnp.add, [jnp.sum(e, axis=-1, keepdims=True) for e in es])
                pv = functools.reduce(jnp.add, [_dot(e, v) for e, v in zip(es, vs)])
                parts.append(pv * (1.0 / l))
            o = parts[0] - lam * parts[1]
            mine = (lane >> 6) == hh
            ms = jnp.sum(jnp.where(mine, o * o, 0.0), axis=-1, keepdims=True) * (1.0 / DF_V)
            outs.append(o * lax.rsqrt(ms + EPS))
        o = jnp.where(lane < DF_V, outs[0], outs[1]) * sub_ref[...] * (1.0 - lam_init)
        o_ref[:, cs] = o.astype(o_ref.dtype)


def _diff_attention(q, k, v, lam_p, subln, lam_init, n_batch, seq, tq, row0, ctx=None):
    qt = seq // tq
    qb0, kb0 = row0 // tq, 0
    in_specs = [pl.BlockSpec((tq, DFW), lambda b, j: (qb0 + b * qt + j, 0)),
                pl.BlockSpec((4, DF_DH), lambda b, j: (0, 0)),
                pl.BlockSpec((1, LANES), lambda b, j: (0, 0))]
    args = [q, lam_p, jnp.concatenate([subln, subln])[None]]
    segs = []
    if ctx is not None:
        segs.append((ctx, PAST, 0))
    segs.append(((k, v), seq, kb0))
    for (a_k, a_v), ln, off in segs:
        idx = lambda b, j, off=off: (off + b, 0)
        in_specs += [pl.BlockSpec((ln, DFW), idx), pl.BlockSpec((ln, DF_HEADS * DF_V), idx)]
        args += [a_k, a_v]
    return pl.pallas_call(
        functools.partial(_diff_attn_kernel, len(segs), lam_init),
        out_shape=jax.ShapeDtypeStruct((n_batch * seq, DF_HEADS * DF_V), BF16),
        grid=(n_batch, qt),
        in_specs=in_specs,
        out_specs=pl.BlockSpec((tq, DF_HEADS * DF_V), lambda b, j: (b * qt + j, 0)),
        compiler_params=_params(2, VMEM_LIMIT),
        name="diff_attention",
    )(*args)


def _hy_filter_kernel(feat_ref, w1_ref, b1_ref, w2_ref, b2_ref, fq_ref, w3_ref, dec_ref, o_ref):
    feat = feat_ref[...]
    fq = fq_ref[...]
    h = jnp.sin(fq * (_dot3(feat, w1_ref[...]) + b1_ref[...]))
    h = jnp.sin(fq * (_dot3(h, w2_ref[...]) + b2_ref[...]))
    window = jnp.exp(-feat[:, 0:1] * jnp.abs(dec_ref[...]))
    for j in range(4):
        cs = slice(j * HY_WIDTH, (j + 1) * HY_WIDTH)
        o_ref[:, cs] = _dot3(h, w3_ref[:, cs]) * window


def _hy_spectrum_kernel(L, cs_ref, hf_ref, hb_ref, o_ref):
    row = lax.broadcasted_iota(jnp.int32, (L, HY_WIDTH), 0)
    first = row == 0
    tf = _dot(cs_ref[...], hf_ref[...])
    tb = _dot(cs_ref[...], jnp.where(first, 0.0, hb_ref[...]))
    ka = tf[:L] + tb[:L]
    kb = jnp.where(first, tf[L:] + tb[L:], tf[L:] - tb[L:])
    wv = jnp.where(first, 1.0 / (2 * L), 2.0 / (2 * L))
    o_ref[0, 0] = ka * wv
    o_ref[0, 1] = jnp.where(first, 0.0, kb) * wv
    o_ref[0, 2] = jnp.where(first, kb, ka) * wv


HY_CH = 256


def _hy_conv_kernel(L, cs_ref, ct_ref, kf_ref, v_ref, x1_ref, x2_ref,
                    wv_ref, w1_ref, w2_ref, bias_ref, o_ref):
    row = lax.broadcasted_iota(jnp.int32, (L, HY_CH), 0)

    def short(x, w):
        prev = jnp.where(row >= 1, pltpu.roll(x, 1, 0), 0.0)
        nxt = jnp.where(row <= L - 2, pltpu.roll(x, L - 1, 0), 0.0)
        return w[0:1] * prev + w[1:2] * x + w[2:3] * nxt

    for j in range(v_ref.shape[0] // L):
        rs = slice(j * L, (j + 1) * L)
        for k in range(HY_WIDTH // HY_CH):
            ch = slice(k * HY_CH, (k + 1) * HY_CH)
            z = short(v_ref[rs, ch], wv_ref[:, ch])
            gates = (short(x1_ref[rs, ch], w1_ref[:, ch]), short(x2_ref[rs, ch], w2_ref[:, ch]))
            for n in range(2):
                ab = _dot(cs_ref[...], z)
                a, b = ab[:L], ab[L:]
                ka, kb1, ka2 = kf_ref[n, 0, :, ch], kf_ref[n, 1, :, ch], kf_ref[n, 2, :, ch]
                pq = jnp.concatenate([a * ka - b * kb1, a * kb1 + b * ka2], axis=0)
                conv = _dot(ct_ref[...], pq)
                z = gates[n] * (conv + bias_ref[n:n + 1, ch] * z)
            o_ref[rs, ch] = z.astype(o_ref.dtype)


def _hyena_spectrum(L, phy):
    conv_w, w1, b1, w2, b2, freq, w3, decay, bias = phy
    feat = jnp.asarray(_hyena_features(L))
    w1p = jnp.pad(w1, ((0, LANES - HY_EMB), (0, 0)))
    filt = pl.pallas_call(
        _hy_filter_kernel,
        out_shape=jax.ShapeDtypeStruct((L, 4 * HY_WIDTH), F32),
        grid=(1,),
        in_specs=[_const_spec((L, LANES)), _const_spec((LANES, HY_FH)), _const_spec((1, HY_FH)),
                  _const_spec((HY_FH, HY_FH)), _const_spec((1, HY_FH)), _const_spec((1, HY_FH)),
                  _const_spec((HY_FH, 4 * HY_WIDTH)), _const_spec((1, HY_WIDTH))],
        out_specs=pl.BlockSpec((L, 4 * HY_WIDTH), lambda i: (0, 0)),
        compiler_params=_params(1, VMEM_LIMIT),
        name="hyena_filter",
    )(feat, w1p, b1[None], w2, b2[None], freq[None], w3, decay[None])
    cs = jnp.asarray(_dft_tables(L)[0]).astype(BF16)
    return pl.pallas_call(
        functools.partial(_hy_spectrum_kernel, L),
        out_shape=jax.ShapeDtypeStruct((2, 3, L, HY_WIDTH), F32),
        grid=(2,),
        in_specs=[_const_spec((2 * L, L)),
                  pl.BlockSpec((L, HY_WIDTH), lambda n: (0, n)),
                  pl.BlockSpec((L, HY_WIDTH), lambda n: (0, 2 + n))],
        out_specs=pl.BlockSpec((1, 3, L, HY_WIDTH), lambda n: (n, 0, 0, 0)),
        compiler_params=_params(1, VMEM_LIMIT),
        name="hyena_spectrum",
    )(cs, filt, filt)


def _hyena_conv(hy_u, spec, phy, n_batch, L, seqs, row0):
    conv_w, bias = phy[0], phy[8]
    cs, ct = (jnp.asarray(t).astype(BF16) for t in _dft_tables(L))
    rows = seqs * L
    rb0 = row0 // rows
    col = lambda off: (lambda b: (0, off))
    tok = lambda off: (lambda b: (rb0 + b, off))
    blk = lambda idx: pl.BlockSpec((rows, HY_WIDTH), idx)
    return pl.pallas_call(
        functools.partial(_hy_conv_kernel, L),
        out_shape=jax.ShapeDtypeStruct((n_batch * L, HY_WIDTH), BF16),
        grid=(n_batch // seqs,),
        in_specs=[_const_spec((2 * L, L)), _const_spec((L, 2 * L)), _const_spec((2, 3, L, HY_WIDTH)),
                  blk(tok(0)), blk(tok(1)), blk(tok(2)),
                  pl.BlockSpec((3, HY_WIDTH), col(0)), pl.BlockSpec((3, HY_WIDTH), col(1)),
                  pl.BlockSpec((3, HY_WIDTH), col(2)), _const_spec((2, HY_WIDTH))],
        out_specs=blk(lambda b: (b, 0)),
        compiler_params=_params(1, VMEM_LIMIT),
        name="hyena_conv",
    )(cs, ct, spec, hy_u, hy_u, hy_u, conv_w, conv_w, conv_w, bias)


def _even_mixer(y, mods_l, g, pa, ps5, ctx_ckv, ctx_krope, ctx_state):
    w_in, w_out, q_norm, w_uq, kv_norm, w_ukv = pa
    a_re, a_im, log_step, b_re, b_im, c_re, c_im, d_skip, w_glu = ps5
    q, ckv, kr_unrot, kr_rot, kn, v, u, (w_k, w_v) = _inproj_a(y, mods_l, g, w_in, q_norm, w_uq, kv_norm, w_ukv)

    ctx_flat = ctx_ckv.reshape(NB_S * PAST, MLA_KV_RANK)
    ctx_kn = _linear(ctx_flat, w_k, PAST, BF16)
    ctx_v = _linear(ctx_flat, w_v, PAST, BF16)
    ctx_kr = jnp.pad(ctx_krope.reshape(NB_S * PAST, MLA_ROPE),
                     ((0, 0), (KR_AT, LANES - KR_AT - MLA_ROPE))).astype(BF16)
    att_p = _mla_attention(q, kn, kr_rot, v, NB_P, L_P, L_P, 0)
    att_s = _mla_attention(q, kn, kr_rot, v, NB_S, L_S, TM, TOK_P, ctx=(ctx_kn, ctx_kr, ctx_v))

    prep = _s5_prep(a_re, a_im, log_step, b_re, b_im, c_re, c_im)
    h0 = ctx_state.transpose(3, 1, 0, 2, 4).reshape(S5_GROUPS, 2, NB_S, 2 * S5_N)
    h0 = jnp.pad(h0, ((0, 0), (0, 0), (0, 8 - NB_S), (0, 0)))
    s5y, fin = _s5_core(u, prep, h0, d_skip)

    mixer = ((att_p, att_s), s5y, w_out, w_glu)
    new_ckv = ckv.reshape(NB_P, L_P, MLA_KV_RANK)
    new_krope = kr_unrot.reshape(NB_P, L_P, MLA_ROPE)
    new_state = fin.reshape(S5_GROUPS, 2, NB_P, 2, S5_N).transpose(2, 1, 3, 0, 4)
    return mixer, new_ckv, new_krope, new_state


def _odd_mixer(y, mods_l, g, pb, phy, ctx_k, ctx_v, lam_init):
    w_in, w_out, lam_p, subln = pb
    hy_u, q, (k_p, k_s), (v_p, v_s), v_cache = _inproj_b(y, mods_l, g, w_in)
    hy_p = _hyena_conv(hy_u, _hyena_spectrum(L_P, phy), phy, NB_P, L_P, 2, 0)
    hy_s = _hyena_conv(hy_u, _hyena_spectrum(L_S, phy), phy, NB_S, L_S, 1, TOK_P)
    ctx = (ctx_k.reshape(NB_S * PAST, DFW), ctx_v.reshape(NB_S * PAST, DF_HEADS * DF_V))
    att_p = _diff_attention(q, k_p, v_p, lam_p, subln, lam_init, NB_P, L_P, L_P, 0)
    att_s = _diff_attention(q, k_s, v_s, lam_p, subln, lam_init, NB_S, L_S, TM // 2, TOK_P, ctx=ctx)
    mixer = ((hy_p, hy_s), (att_p, att_s), w_out, None)
    new_k = k_p.reshape(NB_P, L_P, DF_HEADS, 2, DF_DH)
    new_v = v_cache.reshape(NB_P, L_P, DF_HEADS, DF_V)
    return mixer, new_k, new_v


def kernel(x_prompt, x_sample, c, c_ctx, cache_mla_ckv, cache_mla_krope, state_s5, cache_diff_k, cache_diff_v, ada_w, ada_b, norm_g, ff_w_in, ff_w_out, w_in_a, w_out_a, mla_q_norm, mla_w_uq, mla_kv_norm, mla_w_ukv, s5_a_re, s5_a_im, s5_log_step, s5_b_re, s5_b_im, s5_c_re, s5_c_im, s5_d, s5_w_glu, w_in_b, w_out_b, hy_conv, hy_w1, hy_b1, hy_w2, hy_b2, hy_freq, hy_w3, hy_decay, hy_bias, df_lambda, df_subln, final_norm):
    depth = ada_w.shape[0]
    y = (x_prompt.reshape(TOK_P, D), x_sample.reshape(TOK_S, D))
    mods = _adaln(jnp.concatenate([c_ctx[None], c], axis=0), ada_w, ada_b)
    new_ckv, new_krope, new_s5, new_dk, new_dv = [], [], [], [], []
    for l in range(depth):
        y = _half_ffn(y, (mods, l), norm_g[l, 0], ff_w_in, ff_w_out, l, 0)
        if l % 2 == 0:
            e = l // 2
            pa = (w_in_a[e], w_out_a[e], mla_q_norm[e], mla_w_uq[e], mla_kv_norm[e], mla_w_ukv[e])
            ps5 = (s5_a_re[e], s5_a_im[e], s5_log_step[e], s5_b_re[e], s5_b_im[e],
                   s5_c_re[e], s5_c_im[e], s5_d[e], s5_w_glu[e])
            mixer, ckv, krope, st = _even_mixer(y, (mods, l), norm_g[l, 1], pa, ps5, cache_mla_ckv[:, e],
                                            cache_mla_krope[:, e], state_s5[:, e])
            new_ckv.append(ckv)
            new_krope.append(krope)
            new_s5.append(st)
        else:
            o = l // 2
            lam_init = 0.8 - 0.6 * math.exp(-0.3 * l)
            pb = (w_in_b[o], w_out_b[o], df_lambda[o], df_subln[o])
            phy = (hy_conv[o], hy_w1[o], hy_b1[o], hy_w2[o], hy_b2[o], hy_freq[o],
                   hy_w3[o], hy_decay[o], hy_bias[o])
            mixer, dk, dv = _odd_mixer(y, (mods, l), norm_g[l, 1], pb, phy, cache_diff_k[:, o],
                                   cache_diff_v[:, o], lam_init)
            new_dk.append(dk)
            new_dv.append(dv)
        last = l == depth - 1
        y = _half_ffn(y, (mods, l), norm_g[l, 2], ff_w_in, ff_w_out, l, 1,
                      final_g=final_norm if last else None, mixer=mixer)
    y_prompt = y[0].reshape(NB_P, L_P, D)
    y_sample = y[1].reshape(NB_S, L_S, D)
    return (y_prompt, y_sample, jnp.stack(new_ckv, axis=1), jnp.stack(new_krope, axis=1),
            jnp.stack(new_s5, axis=1), jnp.stack(new_dk, axis=1), jnp.stack(new_dv, axis=1))
```

```python
import functools
import math

import numpy as np
import jax
import jax.numpy as jnp
from jax import lax
from jax.experimental import pallas as pl
from jax.experimental.pallas import tpu as pltpu

F32 = jnp.float32
BF16 = jnp.bfloat16

D = 1024
NB_P, L_P = 16, 256
NB_S, L_S = 2, 1024
PAST = 256
GRID_W = 64
N_MOD = 9
FF = 2816
EPS = 1e-6
ROPE_BASE = 10000.0

MLA_HEADS, MLA_NOPE, MLA_ROPE, MLA_V = 8, 64, 32, 64
MLA_Q_RANK, MLA_KV_RANK = 384, 256
S5_WIDTH, S5_GROUP, S5_N = 512, 16, 64
S5_GROUPS = S5_WIDTH // S5_GROUP
HY_WIDTH, HY_BANDS, HY_FH = 512, 16, 64
HY_EMB = 2 * HY_BANDS + 1
DF_HEADS, DF_DH = 8, 32
DF_V = 2 * DF_DH

TOK_P = NB_P * L_P
TOK_S = NB_S * L_S
TOK = TOK_P + TOK_S
TM = 512
NT = TOK // TM
NT_P = TOK_P // TM
TILES_PER_SAMPLE = L_S // TM

LANES = 128
S5_T = 16
S5_CW = S5_T * S5_GROUP
CH_P = L_P // S5_T
CH_S = L_S // S5_T
S5_ROWS = NB_P * CH_P + NB_S * CH_S
S5_ROWS_P = NB_P * CH_P

VMEM_LIMIT = 56 * 1024 * 1024


def _params(n_grid, vmem=None):
    return pltpu.CompilerParams(dimension_semantics=("arbitrary",) * n_grid,
                                vmem_limit_bytes=vmem)


def _const_spec(shape):
    nd = len(shape)
    return pl.BlockSpec(shape, lambda *_: (0,) * nd, pipeline_mode=pl.Buffered(1))


def _mod_index(i):
    return jnp.where(i < NT_P, 0, 1 + (i - NT_P) // TILES_PER_SAMPLE)


def _mod_spec(layer):
    return pl.BlockSpec((1, 8, N_MOD * D), lambda *_: (layer, 0, 0), pipeline_mode=pl.Buffered(1))


def _mod_rows(mod_ref):
    row = mod_ref[0, pl.ds(_mod_index(pl.program_id(0)), 1), :]
    return [row[:, k * D:(k + 1) * D] for k in range(N_MOD)]


def _pos_index(i):
    return jnp.where(i < NT_P, 0, 1 + (i - NT_P) % TILES_PER_SAMPLE)


def _row(i):
    return (i, 0)


def _row_p(i):
    return (jnp.minimum(i, NT_P - 1), 0)


def _row_s(i):
    return (jnp.maximum(i - NT_P, 0), 0)


def _tok_specs(x, width):
    if isinstance(x, tuple):
        return [pl.BlockSpec((TM, width), _row_p), pl.BlockSpec((TM, width), _row_s)], list(x)
    return [pl.BlockSpec((TM, width), _row)], [x]


def _tok_read(refs, split):
    if split:
        return jnp.where(pl.program_id(0) < NT_P, refs[0][...], refs[1][...]), refs[2:]
    return refs[0][...], refs[1:]


def _tok_write(p_ref, s_ref, value):
    i = pl.program_id(0)

    @pl.when(i < NT_P)
    def _():
        p_ref[...] = value

    @pl.when(i >= NT_P)
    def _():
        s_ref[...] = value.astype(s_ref.dtype)


def _split_out(width, sample_dtype=F32):
    shapes = [jax.ShapeDtypeStruct((TOK_P, width), F32), jax.ShapeDtypeStruct((TOK_S, width), sample_dtype)]
    specs = [pl.BlockSpec((TM, width), _row_p), pl.BlockSpec((TM, width), _row_s)]
    return shapes, specs


def _dot(a, b):
    return jnp.dot(a.astype(BF16), b.astype(BF16), preferred_element_type=F32)


def _dot_nt(a, b):
    return lax.dot_general(a, b, (((1,), (1,)), ((), ())), preferred_element_type=F32)


def _split(x):
    hi = x.astype(BF16)
    lo = (x - hi.astype(F32)).astype(BF16)
    return hi, lo


def _dot3(a, b):
    ah, al = _split(a)
    bh, bl = _split(b)
    d = functools.partial(jnp.dot, preferred_element_type=F32)
    return d(ah, bh) + d(ah, bl) + d(al, bh)


def _rmsnorm(x, g):
    return x * lax.rsqrt(jnp.mean(x * x, axis=-1, keepdims=True) + EPS) * g


def _modulate(y, g, shift, scale):
    return _rmsnorm(y, g) * (1.0 + scale) + shift


def _pair_swap(x):
    n = x.shape[-1]
    lane = lax.broadcasted_iota(jnp.int32, x.shape, x.ndim - 1)
    return jnp.where((lane & 1) == 0, pltpu.roll(x, n - 1, x.ndim - 1), pltpu.roll(x, 1, x.ndim - 1))


def _rope(x, cos, sin_signed):
    return x * cos + _pair_swap(x) * sin_signed


def _rope_angles():
    n_freq = MLA_ROPE // 4
    inv = 1.0 / (ROPE_BASE ** (np.arange(n_freq, dtype=np.float64) / n_freq))
    pos = np.arange(L_S)
    row = (pos // GRID_W).astype(np.float64)
    col = (pos % GRID_W).astype(np.float64)
    ang = np.concatenate([row[:, None] * inv, col[:, None] * inv], axis=-1)
    return np.cos(ang), np.sin(ang)


@functools.lru_cache(maxsize=None)
def _rope_tables(width, starts):
    cos, sin = _rope_angles()
    c = np.ones((TM + L_S, width), np.float32)
    s = np.zeros((TM + L_S, width), np.float32)
    sign = np.where(np.arange(MLA_ROPE) % 2 == 0, -1.0, 1.0)
    unit_c = np.repeat(cos, 2, axis=1)
    unit_s = np.repeat(sin, 2, axis=1) * sign
    for st in starts:
        c[TM:, st:st + MLA_ROPE] = unit_c
        s[TM:, st:st + MLA_ROPE] = unit_s
    return c, s


@functools.lru_cache(maxsize=None)
def _dft_tables(L):
    f = np.arange(L)[:, None]
    s = np.arange(L)[None, :]
    ang = np.pi * ((f * s) % (2 * L)).astype(np.float64) / L
    cs = np.concatenate([np.cos(ang), np.sin(ang)], axis=0)
    cs[L, :] = np.where(np.arange(L) % 2 == 0, 1.0, -1.0)
    cs = cs.astype(np.float32)
    return cs, np.ascontiguousarray(cs.T)


@functools.lru_cache(maxsize=None)
def _hyena_features(L):
    t = np.arange(L, dtype=np.float64) / L
    bands = np.arange(1, HY_BANDS + 1, dtype=np.float64)
    ang = 2.0 * math.pi * t[:, None] * bands
    feat = np.zeros((L, LANES), np.float32)
    feat[:, 0] = t
    feat[:, 1:1 + HY_BANDS] = np.cos(ang)
    feat[:, 1 + HY_BANDS:HY_EMB] = np.sin(ang)
    return feat


def _adaln_kernel(c_ref, w_ref, b_ref, o_ref):
    s = jax.nn.silu(c_ref[...])
    s_hi = s.astype(BF16).astype(F32)
    stacked = jnp.concatenate([s_hi, s - s_hi], axis=0).astype(BF16)
    wh, wl = _split(w_ref[0])
    both = jnp.dot(stacked, wh, preferred_element_type=F32)
    rows = c_ref.shape[0]
    o_ref[0] = both[:rows] + both[rows:] + jnp.dot(stacked, wl, preferred_element_type=F32)[:rows] + b_ref[0]


def _adaln(cvecs, ada_w, ada_b):
    depth = ada_w.shape[0]
    n_vec = cvecs.shape[0]
    tn = N_MOD * D // 4
    out = pl.pallas_call(
        _adaln_kernel,
        out_shape=jax.ShapeDtypeStruct((depth, 8, N_MOD * D), F32),
        grid=(depth, N_MOD * D // tn),
        in_specs=[pl.BlockSpec((8, D), lambda l, j: (0, 0)),
                  pl.BlockSpec((1, D, tn), lambda l, j: (l, 0, j)),
                  pl.BlockSpec((1, 1, tn), lambda l, j: (l, 0, j))],
        out_specs=pl.BlockSpec((1, 8, tn), lambda l, j: (l, 0, j)),
        compiler_params=_params(2, VMEM_LIMIT),
        name="adaln",
    )(jnp.pad(cvecs, ((0, 8 - n_vec), (0, 0))), ada_w, ada_b[:, None, :])
    return out


FF_PIECE = 256
FF_LOADS = FF // FF_PIECE


def _ffn_kernel(base, final, split_in, mixer, layer, which, *refs):
    y, refs = _tok_read(refs, split_in)
    if mixer is not None:
        a1, refs = _tok_read(refs, mixer[0])
        a2, refs = _tok_read(refs, mixer[1])
        wmix_ref, wg_ref = refs[:2]
        refs = refs[2:]
    mod_ref, g_ref, win_hbm, wout_hbm, fg_ref = refs[:5]
    n_out = 2 if final else 1
    outs = refs[5:5 + n_out]
    win_ref, wout_ref, stage_g, stage_u, stage_o, sems = refs[5 + n_out:]
    mod = _mod_rows(mod_ref)
    if mixer is not None:
        if mixer[2]:
            a2 = jax.nn.gelu(a2)
            a2 = a2 * jax.nn.sigmoid(_dot(a2, wg_ref[...]))
        k1 = wmix_ref.shape[0] // 2
        y = y + mod[5] * (_dot(a1, wmix_ref[:k1]) + _dot(a2, wmix_ref[k1:]))
    h = _modulate(y, g_ref[...], mod[base], mod[base + 1]).astype(BF16)

    def hidden(lo, width):
        gate = jnp.dot(h, win_ref[:, lo:lo + width], preferred_element_type=F32)
        up = jnp.dot(h, win_ref[:, FF + lo:FF + lo + width], preferred_element_type=F32)
        a = (jax.nn.silu(gate) * up).astype(BF16)
        return jnp.dot(a, wout_ref[lo:lo + width, :], preferred_element_type=F32)

    def finish(acc):
        out = y + 0.5 * mod[base + 2] * acc
        if final:
            _tok_write(outs[0], outs[1], _rmsnorm(out, fg_ref[...]))
        else:
            outs[0][...] = out

    @pl.when(pl.program_id(0) == 0)
    def _():
        def copies(c, slot):
            cols = pl.ds(c * FF_PIECE, FF_PIECE)
            return (pltpu.make_async_copy(win_hbm.at[layer, which, :, cols], stage_g.at[slot], sems.at[0, slot]),
                    pltpu.make_async_copy(win_hbm.at[layer, which, :, pl.ds(FF + c * FF_PIECE, FF_PIECE)],
                                          stage_u.at[slot], sems.at[1, slot]),
                    pltpu.make_async_copy(wout_hbm.at[layer, which, cols, :], stage_o.at[slot], sems.at[2, slot]))

        for cp in copies(0, 0):
            cp.start()
        acc = jnp.zeros(y.shape, F32)
        for c in range(FF_LOADS):
            slot = c % 2
            lo = c * FF_PIECE
            if c + 1 < FF_LOADS:
                for cp in copies(c + 1, 1 - slot):
                    cp.start()
            for cp in copies(c, slot):
                cp.wait()
            win_ref[:, lo:lo + FF_PIECE] = stage_g[slot].astype(BF16)
            win_ref[:, FF + lo:FF + lo + FF_PIECE] = stage_u[slot].astype(BF16)
            wout_ref[lo:lo + FF_PIECE, :] = stage_o[slot].astype(BF16)
            acc = acc + hidden(lo, FF_PIECE)
        finish(acc)

    @pl.when(pl.program_id(0) > 0)
    def _():
        finish(hidden(0, FF))


def _half_ffn(y, mods_l, g, ff_w_in, ff_w_out, layer, which, final_g=None, mixer=None):
    final = final_g is not None
    fg = final_g if final else g
    y_specs, y_args = _tok_specs(y, D)
    mix_flags = None
    if mixer is not None:
        a1, a2, w_out, w_glu = mixer
        k1 = w_out.shape[0] // 2
        wg = w_glu if w_glu is not None else jnp.zeros((8, LANES), F32)
        s1, a1_args = _tok_specs(a1, k1)
        s2, a2_args = _tok_specs(a2, k1)
        y_specs = y_specs + s1 + s2 + [_const_spec(w_out.shape), _const_spec(wg.shape)]
        y_args = y_args + a1_args + a2_args + [w_out, wg]
        mix_flags = (isinstance(a1, tuple), isinstance(a2, tuple), w_glu is not None)
    if final:
        out_shape, out_specs = _split_out(D)
    else:
        out_shape, out_specs = jax.ShapeDtypeStruct((TOK, D), F32), pl.BlockSpec((TM, D), _row)
    return pl.pallas_call(
        functools.partial(_ffn_kernel, 6 * which, final, isinstance(y, tuple), mix_flags, layer, which),
        out_shape=out_shape,
        grid=(NT,),
        in_specs=y_specs + [_mod_spec(mods_l[1]),
                            _const_spec((1, D)),
                            pl.BlockSpec(memory_space=pl.ANY),
                            pl.BlockSpec(memory_space=pl.ANY),
                            _const_spec((1, D))],
        out_specs=out_specs,
        scratch_shapes=[pltpu.VMEM((D, 2 * FF), BF16), pltpu.VMEM((FF, D), BF16),
                        pltpu.VMEM((2, D, FF_PIECE), F32), pltpu.VMEM((2, D, FF_PIECE), F32),
                        pltpu.VMEM((2, FF_PIECE, D), F32), pltpu.SemaphoreType.DMA((3, 2))],
        compiler_params=_params(1, VMEM_LIMIT),
        name="half_ffn",
    )(*y_args, mods_l[0], g[None], ff_w_in, ff_w_out, fg[None])


def _linear_kernel(x_ref, w_ref, o_ref):
    o_ref[...] = _dot(x_ref[...], w_ref[...]).astype(o_ref.dtype)


def _linear(x, w, tm, out_dtype):
    m, k = x.shape
    n = w.shape[1]
    return pl.pallas_call(
        _linear_kernel,
        out_shape=jax.ShapeDtypeStruct((m, n), out_dtype),
        grid=(m // tm,),
        in_specs=[pl.BlockSpec((tm, k), lambda i: (i, 0)), _const_spec((k, n))],
        out_specs=pl.BlockSpec((tm, n), lambda i: (i, 0)),
        compiler_params=_params(1),
        name="linear",
    )(x, w.astype(BF16))


LOG2E = math.log2(math.e)
MLA_SCALE = (MLA_NOPE + MLA_ROPE) ** -0.5 * LOG2E
QW = MLA_HEADS * LANES
KR_AT = MLA_NOPE
IN_A_PAD = MLA_Q_RANK + MLA_KV_RANK + S5_WIDTH + LANES


def _inproj_a_kernel(y_ref, mod_ref, g_ref, win_ref, qn_ref, wuq_ref, kvn_ref, wk_ref, wv_ref,
                     cq_ref, sq_ref, ck_ref, sk_ref,
                     q_ref, ckv_ref, kru_ref, krr_ref, kn_ref, v_ref, u_ref):
    mod = _mod_rows(mod_ref)
    h = _modulate(y_ref[...], g_ref[...], mod[3], mod[4]).astype(BF16)
    p = jnp.dot(h, win_ref[...], preferred_element_type=F32)
    o1 = MLA_Q_RANK
    o2 = o1 + MLA_KV_RANK
    o3 = o2 + S5_WIDTH
    q = _dot(_rmsnorm(p[:, :o1], qn_ref[...]), wuq_ref[...])
    q_ref[...] = (_rope(q, cq_ref[...], sq_ref[...]) * MLA_SCALE).astype(BF16)
    ckv = _rmsnorm(p[:, o1:o2], kvn_ref[...])
    ckv_b = ckv.astype(BF16)
    kn_ref[...] = jnp.dot(ckv_b, wk_ref[...], preferred_element_type=F32).astype(BF16)
    v_ref[...] = jnp.dot(ckv_b, wv_ref[...], preferred_element_type=F32).astype(BF16)
    u_ref[...] = p[:, o2:o3]
    krp = p[:, o3:]
    krr_ref[...] = _rope(krp, ck_ref[...], sk_ref[...]).astype(BF16)

    @pl.when(pl.program_id(0) < NT_P)
    def _():
        ckv_ref[...] = ckv
        kru_ref[...] = krp[:, KR_AT:KR_AT + MLA_ROPE]


def _inproj_a(y, mods_l, g, w_in, q_norm, w_uq, kv_norm, w_ukv):
    o1 = MLA_Q_RANK
    o2 = o1 + MLA_KV_RANK
    o3 = o2 + MLA_ROPE
    kr_cols = jnp.pad(w_in[:, o2:o3], ((0, 0), (KR_AT, LANES - KR_AT - MLA_ROPE)))
    w_ext = jnp.concatenate([w_in[:, :o2], w_in[:, o3:], kr_cols], axis=1).astype(BF16)
    dq = MLA_NOPE + MLA_ROPE
    w_uq_pad = jnp.pad(w_uq.reshape(MLA_Q_RANK, MLA_HEADS, dq),
                       ((0, 0), (0, 0), (0, LANES - dq))).reshape(MLA_Q_RANK, QW).astype(BF16)
    w_kv = w_ukv.reshape(MLA_KV_RANK, MLA_HEADS, MLA_NOPE + MLA_V)
    w_k = jnp.pad(w_kv[:, :, :MLA_NOPE], ((0, 0), (0, 0), (0, LANES - MLA_NOPE))).reshape(MLA_KV_RANK, QW)
    w_v = w_kv[:, :, MLA_NOPE:].reshape(MLA_KV_RANK, MLA_HEADS * MLA_V)
    w_k, w_v = w_k.astype(BF16), w_v.astype(BF16)
    cq, sq = _rope_tables(QW, tuple(h * LANES + MLA_NOPE for h in range(MLA_HEADS)))
    ck, sk = _rope_tables(LANES, (KR_AT,))
    row = _row
    pos = lambda i: (_pos_index(i), 0)
    widths = (QW, MLA_KV_RANK, MLA_ROPE, LANES, QW, MLA_HEADS * MLA_V, S5_WIDTH)
    prompt_only = (1, 2)
    mxu_only = (0, 3, 4, 5)
    outs = pl.pallas_call(
        _inproj_a_kernel,
        out_shape=[jax.ShapeDtypeStruct((TOK_P if k in prompt_only else TOK, w), BF16 if k in mxu_only else F32)
                   for k, w in enumerate(widths)],
        grid=(NT,),
        in_specs=[pl.BlockSpec((TM, D), row),
                  _mod_spec(mods_l[1]),
                  _const_spec((1, D)),
                  _const_spec((D, IN_A_PAD)),
                  _const_spec((1, MLA_Q_RANK)),
                  _const_spec((MLA_Q_RANK, QW)),
                  _const_spec((1, MLA_KV_RANK)),
                  _const_spec((MLA_KV_RANK, QW)),
                  _const_spec((MLA_KV_RANK, MLA_HEADS * MLA_V)),
                  pl.BlockSpec((TM, QW), pos), pl.BlockSpec((TM, QW), pos),
                  pl.BlockSpec((TM, LANES), pos), pl.BlockSpec((TM, LANES), pos)],
        out_specs=[pl.BlockSpec((TM, w), _row_p if k in prompt_only else row)
                   for k, w in enumerate(widths)],
        compiler_params=_params(1, VMEM_LIMIT),
        name="inproj_even",
    )(y, mods_l[0], g[None], w_ext, q_norm[None], w_uq_pad, kv_norm[None], w_k, w_v,
      jnp.asarray(cq), jnp.asarray(sq), jnp.asarray(ck), jnp.asarray(sk))
    q, ckv, kr_unrot, kr_rot, kn, v, u = outs
    return q, ckv, kr_unrot, kr_rot, kn, v, u, (w_k, w_v)


def _mla_attn_kernel(nseg, nseq, q_ref, *refs):
    o_ref = refs[-1]
    tq = q_ref.shape[0] // nseq
    lane = lax.broadcasted_iota(jnp.int32, (tq, LANES), 1)
    for j in range(nseq):
        qr = slice(j * tq, (j + 1) * tq)
        krs = [slice(j * (refs[3 * s].shape[0] // nseq), (j + 1) * (refs[3 * s].shape[0] // nseq))
               for s in range(nseg)]
        for pair in range(MLA_HEADS // 2):
            outs = []
            for hh in range(2):
                h = 2 * pair + hh
                hs = slice(h * LANES, (h + 1) * LANES)
                qh = q_ref[qr, hs]
                scores = []
                for s in range(nseg):
                    kn_ref, kr_ref = refs[3 * s], refs[3 * s + 1]
                    kh = (kn_ref[krs[s], hs] + kr_ref[krs[s], :]).astype(BF16)
                    scores.append(_dot_nt(qh, kh))
                m = functools.reduce(jnp.maximum, [jnp.max(s, axis=-1, keepdims=True) for s in scores])
                es = [jnp.exp2(s - m) for s in scores]
                l = functools.reduce(jnp.add, [jnp.sum(e, axis=-1, keepdims=True) for e in es])
                o = None
                for s in range(nseg):
                    v_ref = refs[3 * s + 2]
                    part = _dot(es[s], v_ref[krs[s], pair * LANES:(pair + 1) * LANES])
                    o = part if o is None else o + part
                outs.append(o / l)
            o_ref[qr, pair * LANES:(pair + 1) * LANES] = jnp.where(lane < MLA_V, outs[0], outs[1]).astype(o_ref.dtype)


def _mla_attention(q, kn, kr, v, n_batch, seq, tq, row0, ctx=None, nseq=1):
    qt = seq // tq
    qb0, kb0 = row0 // (nseq * tq), row0 // (nseq * seq)
    in_specs = [pl.BlockSpec((nseq * tq, QW), lambda b, j: (qb0 + b * qt + j, 0))]
    args = [q]
    segs = []
    if ctx is not None:
        segs.append((ctx, PAST, 0))
    segs.append(((kn, kr, v), seq, kb0))
    for (a_kn, a_kr, a_v), ln, off in segs:
        idx = lambda b, j, off=off: (off + b, 0)
        in_specs += [pl.BlockSpec((nseq * ln, QW), idx), pl.BlockSpec((nseq * ln, LANES), idx),
                     pl.BlockSpec((nseq * ln, MLA_HEADS * MLA_V), idx)]
        args += [a_kn, a_kr, a_v]
    return pl.pallas_call(
        functools.partial(_mla_attn_kernel, len(segs), nseq),
        out_shape=jax.ShapeDtypeStruct((n_batch * seq, MLA_HEADS * MLA_V), BF16),
        grid=(n_batch // nseq, qt),
        in_specs=in_specs,
        out_specs=pl.BlockSpec((nseq * tq, MLA_HEADS * MLA_V), lambda b, j: (b * qt + j, 0)),
        compiler_params=_params(2, VMEM_LIMIT),
        name="mla_attention",
    )(*args)


def _cpow(ar, ai, e, nbits):
    rr = jnp.ones_like(ar)
    ri = jnp.zeros_like(ar)
    br, bi = ar, ai
    for k in range(nbits):
        bit = ((e >> k) & 1) == 1
        nr = rr * br - ri * bi
        ni = rr * bi + ri * br
        rr = jnp.where(bit, nr, rr)
        ri = jnp.where(bit, ni, ri)
        if k + 1 < nbits:
            br, bi = br * br - bi * bi, 2.0 * br * bi
    return rr, ri


def _s5_abar_kernel(lr_ref, li_ref, ls_ref, o_ref):
    step = jnp.exp(ls_ref[...])
    lr = jnp.minimum(lr_ref[...], -1e-4)
    li = li_ref[...]
    mag = jnp.exp(lr * step)
    ar = mag * jnp.cos(li * step)
    ai = mag * jnp.sin(li * step)
    den = lr * lr + li * li
    o_ref[0] = ar
    o_ref[1] = ai
    o_ref[2] = ((ar - 1.0) * lr + ai * li) / den
    o_ref[3] = (ai * lr - (ar - 1.0) * li) / den


def _s5_prep_kernel(arow_ref, acol_ref, btr_ref, bti_ref, ctr_ref, cti_ref,
                    wi_ref, ws_ref, wo_ref, ap_ref):
    n2 = 2 * S5_N
    blk_o = lax.broadcasted_iota(jnp.int32, (S5_N, S5_CW), 1) >> 4
    lane_k = lax.broadcasted_iota(jnp.int32, (S5_GROUP, S5_CW), 1)
    row_k = lax.broadcasted_iota(jnp.int32, (S5_GROUP, S5_CW), 0)
    lane_b = lax.broadcasted_iota(jnp.int32, (S5_GROUP, n2), 1)
    lane_a = lax.broadcasted_iota(jnp.int32, (1, n2), 1)
    rep = ((lane_k & (S5_GROUP - 1)) == row_k).astype(BF16)

    def tile16(x):
        hi = x.astype(BF16)
        r1 = x - hi.astype(F32)
        mid = r1.astype(BF16)
        lo = (r1 - mid.astype(F32)).astype(BF16)
        d = lambda a: lax.dot_general(a, rep, (((0,), (0,)), ((), ())), preferred_element_type=F32)
        return d(hi) + d(mid) + d(lo)

    intra = [None] * S5_T
    for d in range(2):
        ar, ai, fr, fi = (arow_ref[d, 0, k:k + 1, :] for k in range(4))
        btr, bti = btr_ref[d, 0], bti_ref[d, 0]
        bbr = fr * btr - fi * bti
        bbi = fr * bti + fi * btr
        pws = [(jnp.ones_like(ar), jnp.zeros_like(ar))]
        for _ in range(S5_T):
            pr, pi = pws[-1]
            pws.append((pr * ar - pi * ai, pr * ai + pi * ar))
        for s in range(S5_T):
            pr, pi = pws[S5_T - 1 - s] if d == 0 else pws[s]
            ws_ref[d, 0, s * S5_GROUP:(s + 1) * S5_GROUP, :] = jnp.where(
                lane_b < S5_N, pr * bbr - pi * bbi, pr * bbi + pi * bbr).astype(BF16)

        acol = acol_ref[d, 0]
        arc = jnp.broadcast_to(acol[:, 0:1], (S5_N, S5_CW))
        aic = jnp.broadcast_to(acol[:, 1:2], (S5_N, S5_CW))
        ctr, cti = tile16(ctr_ref[d, 0]), tile16(cti_ref[d, 0])
        e_lag = blk_o if d == 0 else (S5_T - 1 - blk_o)
        pqr, pqi = _cpow(arc, aic, e_lag, 4)
        qr = pqr * ctr - pqi * cti
        qi = pqr * cti + pqi * ctr
        wo_ref[d, 0] = jnp.concatenate([qr * arc - qi * aic, -(qr * aic + qi * arc)], axis=0).astype(BF16)
        q_stack = jnp.concatenate([qr, qi], axis=0)
        bb_mix = jnp.where(lane_b < S5_N, bbr, -bbi)
        kt = _dot3(bb_mix, q_stack)
        for s in range(S5_T):
            if d == 0:
                blk = jnp.where(lane_k >= S5_GROUP * s, pltpu.roll(kt, S5_GROUP * s, 1), 0.0)
            else:
                blk = jnp.where(lane_k < S5_GROUP * (s + 1),
                                pltpu.roll(kt, (S5_GROUP * (s + 1)) % S5_CW, 1), 0.0)
            intra[s] = blk if intra[s] is None else intra[s] + blk

        pr1, pi1 = pws[S5_T]
        for k in range(6):
            ap_ref[d, 0, k:k + 1, :] = pr1
            ap_ref[d, 0, 8 + k:9 + k, :] = jnp.where(lane_a < S5_N, -pi1, pi1)
            pr1, pi1 = pr1 * pr1 - pi1 * pi1, 2.0 * pr1 * pi1
        ap_ref[d, 0, 6:8, :] = jnp.zeros((2, n2), F32)
        ap_ref[d, 0, 14:16, :] = jnp.zeros((2, n2), F32)
    for s in range(S5_T):
        wi_ref[0, s * S5_GROUP:(s + 1) * S5_GROUP, :] = intra[s].astype(BF16)


def _s5_prep(a_re, a_im, log_step, b_re, b_im, c_re, c_im):
    g, n, n2 = S5_GROUPS, S5_N, 2 * S5_N
    abar = pl.pallas_call(
        _s5_abar_kernel,
        out_shape=jax.ShapeDtypeStruct((4, 2 * g, n), F32),
        grid=(1,),
        in_specs=[_const_spec((2 * g, n)), _const_spec((2 * g, n)), _const_spec((2 * g, 1))],
        out_specs=pl.BlockSpec((4, 2 * g, n), lambda i: (0, 0, 0)),
        compiler_params=_params(1),
        name="s5_abar",
    )(a_re.reshape(2 * g, n), a_im.reshape(2 * g, n), log_step.reshape(2 * g, 1))
    abar = jnp.concatenate([abar, abar], axis=-1).reshape(4, 2, g, n2)
    arow = abar.transpose(1, 2, 0, 3)
    acol = abar[:2, :, :, :n].transpose(1, 2, 3, 0)
    bt = lambda b: jnp.concatenate([jnp.swapaxes(b, 2, 3)] * 2, axis=-1)
    spec4 = lambda r, c: pl.BlockSpec((2, 1, r, c), lambda i: (0, i, 0, 0))
    return pl.pallas_call(
        _s5_prep_kernel,
        out_shape=[jax.ShapeDtypeStruct((g, S5_CW, S5_CW), BF16),
                   jax.ShapeDtypeStruct((2, g, S5_CW, n2), BF16),
                   jax.ShapeDtypeStruct((2, g, n2, S5_CW), BF16),
                   jax.ShapeDtypeStruct((2, g, 16, n2), F32)],
        grid=(g,),
        in_specs=[spec4(4, n2), spec4(n, 2),
                  spec4(S5_GROUP, n2), spec4(S5_GROUP, n2), spec4(S5_GROUP, n), spec4(S5_GROUP, n)],
        out_specs=[pl.BlockSpec((1, S5_CW, S5_CW), lambda i: (i, 0, 0)),
                   spec4(S5_CW, n2), spec4(n2, S5_CW), spec4(16, n2)],
        compiler_params=_params(1),
        name="s5_prep",
    )(arow, acol, bt(b_re), bt(b_im), c_re, c_im)


def _cmul_rows(x, p1, p2):
    return x * p1 + pltpu.roll(x, S5_N, 1) * p2


S5_OCT = LANES // S5_GROUP
S5_RB_IN = 96
S5_RB_OUT = 48


def _s5_core_kernel(u_ref, wi_ref, ws_ref, wo_ref, ap_ref, h0_ref, d_ref, y_ref, fin_ref,
                    ug_ref, yg_ref, z_ref):
    n2 = 2 * S5_N

    def tok_rows(r0, t, nrows):
        return pl.ds(r0 * S5_T + t, nrows, stride=S5_T)

    def block_transpose(xs):
        n = S5_OCT
        blk = lax.broadcasted_iota(jnp.int32, xs[0].shape, 1) >> 4
        a = [pltpu.roll(x, i * S5_GROUP, 1) if i else x for i, x in enumerate(xs)]
        ys = []
        for d in range(n):
            diag = a[-d % n]
            for b in range(1, n):
                diag = jnp.where(blk == b, a[(b - d) % n], diag)
            ys.append(pltpu.roll(diag, LANES - d * S5_GROUP, 1) if d else diag)
        return ys

    def gather(rb, carry):
        r0 = pl.multiple_of(rb * S5_RB_IN, S5_RB_IN)
        for half in range(2):
            xs = [u_ref[tok_rows(r0, S5_OCT * half + tt, S5_RB_IN), :] for tt in range(S5_OCT)]
            for gl, x in enumerate(block_transpose(xs)):
                ug_ref[gl, pl.ds(r0, S5_RB_IN), half * LANES:(half + 1) * LANES] = x
        return carry

    lax.fori_loop(0, S5_ROWS // S5_RB_IN, gather, 0)

    r = lax.broadcasted_iota(jnp.int32, (S5_ROWS, n2), 0)
    in_p = r < S5_ROWS_P
    rib = jnp.where(in_p, r & (CH_P - 1), (r - S5_ROWS_P) & (CH_S - 1))
    nch = jnp.where(in_p, CH_P, CH_S)

    def one_group(gl, slot):
        ub = ug_ref[gl].astype(BF16)
        y = jnp.dot(ub, wi_ref[gl], preferred_element_type=F32)
        for d in range(2):
            p1, p2 = ap_ref[d, gl, 0:1, :], ap_ref[d, gl, 8:9, :]
            edge = [S5_ROWS_P + CH_S * b + (0 if d == 0 else CH_S - 1) for b in range(NB_S)]
            h0 = [h0_ref[gl, d, b:b + 1, :] for b in range(NB_S)]
            s = jnp.dot(ub, ws_ref[d, gl], preferred_element_type=F32)
            for b in range(NB_S):
                s = s + jnp.where(r == edge[b], _cmul_rows(h0[b], p1, p2), 0.0)
            def scan_step(x, k, pos, count):
                sh = 1 << k
                if d == 0:
                    t = jnp.where(pos >= sh, pltpu.roll(x, sh, 0), 0.0)
                else:
                    t = jnp.where(pos < count - sh, pltpu.roll(x, x.shape[0] - sh, 0), 0.0)
                return x + _cmul_rows(t, ap_ref[d, gl, k:k + 1, :], ap_ref[d, gl, 8 + k:9 + k, :])

            for k in range(CH_P.bit_length() - 1):
                s = scan_step(s, k, rib, nch)
            tail = s[S5_ROWS_P:]
            for k in range(CH_P.bit_length() - 1, CH_S.bit_length() - 1):
                tail = scan_step(tail, k, rib[S5_ROWS_P:], CH_S)
            s = jnp.concatenate([s[:S5_ROWS_P], tail], axis=0)
            z_ref[slot, d] = s
            first = CH_P - 1 if d == 0 else 0
            fin_ref[gl, d] = z_ref[slot, d, pl.ds(first, NB_P, stride=CH_P), :]
            if d == 0:
                sp = jnp.where(rib >= 1, pltpu.roll(s, 1, 0), 0.0)
            else:
                sp = jnp.where(rib < nch - 1, pltpu.roll(s, S5_ROWS - 1, 0), 0.0)
            for b in range(NB_S):
                sp = jnp.where(r == edge[b], h0[b], sp)
            y = y + jnp.dot(sp.astype(BF16), wo_ref[d, gl], preferred_element_type=F32)
        yg_ref[gl] = y

    def group_pair(gp, carry):
        for slot in range(2):
            one_group(2 * gp + slot, slot)
        return carry

    lax.fori_loop(0, S5_OCT // 2, group_pair, 0)

    def scatter(rb, carry):
        r0 = pl.multiple_of(rb * S5_RB_OUT, S5_RB_OUT)
        for half in range(2):
            ys = [yg_ref[gl, pl.ds(r0, S5_RB_OUT), half * LANES:(half + 1) * LANES] for gl in range(S5_OCT)]
            for tt, acc in enumerate(block_transpose(ys)):
                rows = tok_rows(r0, S5_OCT * half + tt, S5_RB_OUT)
                y_ref[rows, :] = acc + d_ref[...] * u_ref[rows, :]
        return carry

    lax.fori_loop(0, S5_ROWS // S5_RB_OUT, scatter, 0)


def _s5_core(u, prep, h0, d_skip):
    w_intra, w_state, w_out, apow = prep
    g, n2 = S5_GROUPS, 2 * S5_N
    spec4 = lambda r, c: pl.BlockSpec((2, S5_OCT, r, c), lambda i: (0, i, 0, 0))
    slab = pl.BlockSpec((TOK, LANES), lambda i: (0, i))
    return pl.pallas_call(
        _s5_core_kernel,
        out_shape=[jax.ShapeDtypeStruct((TOK, S5_WIDTH), F32),
                   jax.ShapeDtypeStruct((g, 2, NB_P, n2), F32)],
        grid=(g // S5_OCT,),
        in_specs=[slab,
                  pl.BlockSpec((S5_OCT, S5_CW, S5_CW), lambda i: (i, 0, 0)),
                  spec4(S5_CW, n2), spec4(n2, S5_CW), spec4(16, n2),
                  pl.BlockSpec((S5_OCT, 2, 8, n2), lambda i: (i, 0, 0, 0)),
                  pl.BlockSpec((1, LANES), lambda i: (0, i))],
        out_specs=[slab, pl.BlockSpec((S5_OCT, 2, NB_P, n2), lambda i: (i, 0, 0, 0))],
        scratch_shapes=[pltpu.VMEM((S5_OCT, S5_ROWS, S5_CW), F32), pltpu.VMEM((S5_OCT, S5_ROWS, S5_CW), F32),
                        pltpu.VMEM((2, 2, S5_ROWS, n2), F32)],
        compiler_params=_params(1, VMEM_LIMIT),
        name="s5_scan",
    )(u, w_intra, w_state, w_out, apow, h0, d_skip[None])


DF_SCALE = DF_DH ** -0.5 * LOG2E
DFW = DF_HEADS * 2 * DF_DH
IN_B = 3 * HY_WIDTH + 2 * DFW + DF_HEADS * DF_V


def _inproj_b_kernel(y_ref, mod_ref, g_ref, win_ref, c_ref, s_ref,
                     hy_ref, q_ref, kp_ref, ks_ref, vp_ref, vs_ref, vc_ref, wbf_ref):
    @pl.when(pl.program_id(0) == 0)
    def _():
        wbf_ref[...] = win_ref[...].astype(BF16)

    mod = _mod_rows(mod_ref)
    h = _modulate(y_ref[...], g_ref[...], mod[3], mod[4]).astype(BF16)
    p = jnp.dot(h, wbf_ref[...], preferred_element_type=F32)
    o1 = 3 * HY_WIDTH
    hy_ref[...] = p[:, :o1]
    q_ref[...] = (_rope(p[:, o1:o1 + DFW], c_ref[...], s_ref[...]) * DF_SCALE).astype(BF16)
    _tok_write(kp_ref, ks_ref, _rope(p[:, o1 + DFW:o1 + 2 * DFW], c_ref[...], s_ref[...]))
    v = p[:, o1 + 2 * DFW:]
    _tok_write(vp_ref, vs_ref, v.astype(BF16))

    @pl.when(pl.program_id(0) < NT_P)
    def _():
        for hd in range(DF_HEADS):
            vc_ref[pl.ds(hd, TM, stride=DF_HEADS), :] = v[:, hd * DF_V:(hd + 1) * DF_V]


def _inproj_b(y, mods_l, g, w_in):
    cs, sn = _rope_tables(DFW, tuple(range(0, DFW, DF_DH)))
    pos = lambda i: (_pos_index(i), 0)
    k_shapes, k_specs = _split_out(DFW, BF16)
    v_specs = _split_out(DF_HEADS * DF_V)[1]
    v_shapes = [jax.ShapeDtypeStruct((TOK_P, DF_HEADS * DF_V), BF16), jax.ShapeDtypeStruct((TOK_S, DF_HEADS * DF_V), BF16),
                jax.ShapeDtypeStruct((TOK_P * DF_HEADS, DF_V), F32)]
    v_specs = v_specs + [pl.BlockSpec((TM * DF_HEADS, DF_V), _row_p)]
    hy_u, q, kp, ks, vp, vs, v_cache = pl.pallas_call(
        _inproj_b_kernel,
        out_shape=[jax.ShapeDtypeStruct((TOK, 3 * HY_WIDTH), F32), jax.ShapeDtypeStruct((TOK, DFW), BF16)]
                  + k_shapes + v_shapes,
        grid=(NT,),
        in_specs=[pl.BlockSpec((TM, D), _row),
                  _mod_spec(mods_l[1]),
                  _const_spec((1, D)), _const_spec((D, IN_B)),
                  pl.BlockSpec((TM, DFW), pos), pl.BlockSpec((TM, DFW), pos)],
        out_specs=[pl.BlockSpec((TM, 3 * HY_WIDTH), _row), pl.BlockSpec((TM, DFW), _row)] + k_specs + v_specs,
        scratch_shapes=[pltpu.VMEM((D, IN_B), BF16)],
        compiler_params=_params(1, VMEM_LIMIT),
        name="inproj_odd",
    )(y, mods_l[0], g[None], w_in, jnp.asarray(cs), jnp.asarray(sn))
    return hy_u, q, (kp, ks), (vp, vs), v_cache


def _diff_attn_kernel(nseg, nseq, lam_init, q_ref, lam_ref, sub_ref, *refs):
    o_ref = refs[-1]
    lp = lam_ref[...]
    lam = (jnp.exp(jnp.sum(lp[0:1] * lp[1:2], axis=-1, keepdims=True))
           - jnp.exp(jnp.sum(lp[2:3] * lp[3:4], axis=-1, keepdims=True)) + lam_init)
    tq = q_ref.shape[0] // nseq
    lane = lax.broadcasted_iota(jnp.int32, (tq, LANES), 1)
    for j in range(nseq):
        qr = slice(j * tq, (j + 1) * tq)
        krs = [slice(j * (refs[2 * s].shape[0] // nseq), (j + 1) * (refs[2 * s].shape[0] // nseq))
               for s in range(nseg)]
        for pair in range(DF_HEADS // 2):
            cs = slice(pair * LANES, (pair + 1) * LANES)
            q = q_ref[qr, cs]
            ks = [refs[2 * s][krs[s], cs].astype(BF16) for s in range(nseg)]
            vs = [refs[2 * s + 1][krs[s], cs].astype(BF16) for s in range(nseg)]
            outs = []
            for hh in range(2):
                parts = []
                for half in range(2):
                    unit = 2 * hh + half
                    qm = jnp.where((lane >> 5) == unit, q, jnp.zeros_like(q))
                    scores = [_dot_nt(qm, k) for k in ks]
                    m = functools.reduce(jnp.maximum, [jnp.max(s, axis=-1, keepdims=True) for s in scores])
                    es = [jnp.exp2(s - m) for s in scores]
                    l = functools.reduce(jnp.add, [jnp.sum(e, axis=-1, keepdims=True) for e in es])
                    pv = functools.reduce(jnp.add, [_dot(e, v) for e, v in zip(es, vs)])
                    parts.append(pv * (1.0 / l))
                o = parts[0] - lam * parts[1]
                mine = (lane >> 6) == hh
                ms = jnp.sum(jnp.where(mine, o * o, 0.0), axis=-1, keepdims=True) * (1.0 / DF_V)
                outs.append(o * lax.rsqrt(ms + EPS))
            o = jnp.where(lane < DF_V, outs[0], outs[1]) * sub_ref[...] * (1.0 - lam_init)
            o_ref[qr, cs] = o.astype(o_ref.dtype)


def _diff_attention(q, k, v, lam_p, subln, lam_init, n_batch, seq, tq, row0, ctx=None, nseq=1):
    qt = seq // tq
    qb0, kb0 = row0 // (nseq * tq), 0
    in_specs = [pl.BlockSpec((nseq * tq, DFW), lambda b, j: (qb0 + b * qt + j, 0)),
                pl.BlockSpec((4, DF_DH), lambda b, j: (0, 0)),
                pl.BlockSpec((1, LANES), lambda b, j: (0, 0))]
    args = [q, lam_p, jnp.concatenate([subln, subln])[None]]
    segs = []
    if ctx is not None:
        segs.append((ctx, PAST, 0))
    segs.append(((k, v), seq, kb0))
    for (a_k, a_v), ln, off in segs:
        idx = lambda b, j, off=off: (off + b, 0)
        in_specs += [pl.BlockSpec((nseq * ln, DFW), idx), pl.BlockSpec((nseq * ln, DF_HEADS * DF_V), idx)]
        args += [a_k, a_v]
    return pl.pallas_call(
        functools.partial(_diff_attn_kernel, len(segs), nseq, lam_init),
        out_shape=jax.ShapeDtypeStruct((n_batch * seq, DF_HEADS * DF_V), BF16),
        grid=(n_batch // nseq, qt),
        in_specs=in_specs,
        out_specs=pl.BlockSpec((nseq * tq, DF_HEADS * DF_V), lambda b, j: (b * qt + j, 0)),
        compiler_params=_params(2, VMEM_LIMIT),
        name="diff_attention",
    )(*args)


def _hy_filter_kernel(feat_ref, w1_ref, b1_ref, w2_ref, b2_ref, fq_ref, w3_ref, dec_ref, o_ref):
    feat = feat_ref[...]
    fq = fq_ref[...]
    h = jnp.sin(fq * (_dot3(feat, w1_ref[...]) + b1_ref[...]))
    h = jnp.sin(fq * (_dot3(h, w2_ref[...]) + b2_ref[...]))
    window = jnp.exp(-feat[:, 0:1] * jnp.abs(dec_ref[...]))
    for j in range(4):
        cs = slice(j * HY_WIDTH, (j + 1) * HY_WIDTH)
        o_ref[:, cs] = _dot3(h, w3_ref[:, cs]) * window


def _hy_spectrum_kernel(L, cs_ref, hf_ref, hb_ref, o_ref):
    row = lax.broadcasted_iota(jnp.int32, (L, HY_WIDTH), 0)
    first = row == 0
    tf = _dot(cs_ref[...], hf_ref[...])
    tb = _dot(cs_ref[...], jnp.where(first, 0.0, hb_ref[...]))
    ka = tf[:L] + tb[:L]
    kb = jnp.where(first, tf[L:] + tb[L:], tf[L:] - tb[L:])
    wv = jnp.where(first, 1.0 / (2 * L), 2.0 / (2 * L))
    o_ref[0, 0] = ka * wv
    o_ref[0, 1] = jnp.where(first, 0.0, kb) * wv
    o_ref[0, 2] = jnp.where(first, kb, ka) * wv


HY_CH = 256


def _hy_conv_kernel(L, cs_ref, ct_ref, kf_ref, v_ref, x1_ref, x2_ref,
                    wv_ref, w1_ref, w2_ref, bias_ref, o_ref):
    row = lax.broadcasted_iota(jnp.int32, (L, HY_CH), 0)

    def short(x, w):
        prev = jnp.where(row >= 1, pltpu.roll(x, 1, 0), 0.0)
        nxt = jnp.where(row <= L - 2, pltpu.roll(x, L - 1, 0), 0.0)
        return w[0:1] * prev + w[1:2] * x + w[2:3] * nxt

    for j in range(v_ref.shape[0] // L):
        rs = slice(j * L, (j + 1) * L)
        for k in range(HY_WIDTH // HY_CH):
            ch = slice(k * HY_CH, (k + 1) * HY_CH)
            z = short(v_ref[rs, ch], wv_ref[:, ch])
            gates = (short(x1_ref[rs, ch], w1_ref[:, ch]), short(x2_ref[rs, ch], w2_ref[:, ch]))
            for n in range(2):
                ab = _dot(cs_ref[...], z)
                a, b = ab[:L], ab[L:]
                ka, kb1, ka2 = kf_ref[n, 0, :, ch], kf_ref[n, 1, :, ch], kf_ref[n, 2, :, ch]
                pq = jnp.concatenate([a * ka - b * kb1, a * kb1 + b * ka2], axis=0)
                conv = _dot(ct_ref[...], pq)
                z = gates[n] * (conv + bias_ref[n:n + 1, ch] * z)
            o_ref[rs, ch] = z.astype(o_ref.dtype)


def _hyena_spectrum(L, phy):
    conv_w, w1, b1, w2, b2, freq, w3, decay, bias = phy
    feat = jnp.asarray(_hyena_features(L))
    w1p = jnp.pad(w1, ((0, LANES - HY_EMB), (0, 0)))
    filt = pl.pallas_call(
        _hy_filter_kernel,
        out_shape=jax.ShapeDtypeStruct((L, 4 * HY_WIDTH), F32),
        grid=(1,),
        in_specs=[_const_spec((L, LANES)), _const_spec((LANES, HY_FH)), _const_spec((1, HY_FH)),
                  _const_spec((HY_FH, HY_FH)), _const_spec((1, HY_FH)), _const_spec((1, HY_FH)),
                  _const_spec((HY_FH, 4 * HY_WIDTH)), _const_spec((1, HY_WIDTH))],
        out_specs=pl.BlockSpec((L, 4 * HY_WIDTH), lambda i: (0, 0)),
        compiler_params=_params(1, VMEM_LIMIT),
        name="hyena_filter",
    )(feat, w1p, b1[None], w2, b2[None], freq[None], w3, decay[None])
    cs = jnp.asarray(_dft_tables(L)[0]).astype(BF16)
    return pl.pallas_call(
        functools.partial(_hy_spectrum_kernel, L),
        out_shape=jax.ShapeDtypeStruct((2, 3, L, HY_WIDTH), F32),
        grid=(2,),
        in_specs=[_const_spec((2 * L, L)),
                  pl.BlockSpec((L, HY_WIDTH), lambda n: (0, n)),
                  pl.BlockSpec((L, HY_WIDTH), lambda n: (0, 2 + n))],
        out_specs=pl.BlockSpec((1, 3, L, HY_WIDTH), lambda n: (n, 0, 0, 0)),
        compiler_params=_params(1, VMEM_LIMIT),
        name="hyena_spectrum",
    )(cs, filt, filt)


def _hyena_conv(hy_u, spec, phy, n_batch, L, seqs, row0):
    conv_w, bias = phy[0], phy[8]
    cs, ct = (jnp.asarray(t).astype(BF16) for t in _dft_tables(L))
    rows = seqs * L
    rb0 = row0 // rows
    col = lambda off: (lambda b: (0, off))
    tok = lambda off: (lambda b: (rb0 + b, off))
    blk = lambda idx: pl.BlockSpec((rows, HY_WIDTH), idx)
    return pl.pallas_call(
        functools.partial(_hy_conv_kernel, L),
        out_shape=jax.ShapeDtypeStruct((n_batch * L, HY_WIDTH), BF16),
        grid=(n_batch // seqs,),
        in_specs=[_const_spec((2 * L, L)), _const_spec((L, 2 * L)), _const_spec((2, 3, L, HY_WIDTH)),
                  blk(tok(0)), blk(tok(1)), blk(tok(2)),
                  pl.BlockSpec((3, HY_WIDTH), col(0)), pl.BlockSpec((3, HY_WIDTH), col(1)),
                  pl.BlockSpec((3, HY_WIDTH), col(2)), _const_spec((2, HY_WIDTH))],
        out_specs=blk(lambda b: (b, 0)),
        compiler_params=_params(1, VMEM_LIMIT),
        name="hyena_conv",
    )(cs, ct, spec, hy_u, hy_u, hy_u, conv_w, conv_w, conv_w, bias)


def _even_mixer(y, mods_l, g, pa, ps5, ctx_ckv, ctx_krope, ctx_state):
    w_in, w_out, q_norm, w_uq, kv_norm, w_ukv = pa
    a_re, a_im, log_step, b_re, b_im, c_re, c_im, d_skip, w_glu = ps5
    q, ckv, kr_unrot, kr_rot, kn, v, u, (w_k, w_v) = _inproj_a(y, mods_l, g, w_in, q_norm, w_uq, kv_norm, w_ukv)

    ctx_flat = ctx_ckv.reshape(NB_S * PAST, MLA_KV_RANK)
    ctx_kn = _linear(ctx_flat, w_k, PAST, BF16)
    ctx_v = _linear(ctx_flat, w_v, PAST, BF16)
    ctx_kr = jnp.pad(ctx_krope.reshape(NB_S * PAST, MLA_ROPE),
                     ((0, 0), (KR_AT, LANES - KR_AT - MLA_ROPE))).astype(BF16)
    att_p = _mla_attention(q, kn, kr_rot, v, NB_P, L_P, L_P, 0, nseq=2)
    att_s = _mla_attention(q, kn, kr_rot, v, NB_S, L_S, TM, TOK_P, ctx=(ctx_kn, ctx_kr, ctx_v))

    prep = _s5_prep(a_re, a_im, log_step, b_re, b_im, c_re, c_im)
    h0 = ctx_state.transpose(3, 1, 0, 2, 4).reshape(S5_GROUPS, 2, NB_S, 2 * S5_N)
    h0 = jnp.pad(h0, ((0, 0), (0, 0), (0, 8 - NB_S), (0, 0)))
    s5y, fin = _s5_core(u, prep, h0, d_skip)

    mixer = ((att_p, att_s), s5y, w_out, w_glu)
    new_ckv = ckv.reshape(NB_P, L_P, MLA_KV_RANK)
    new_krope = kr_unrot.reshape(NB_P, L_P, MLA_ROPE)
    new_state = fin.reshape(S5_GROUPS, 2, NB_P, 2, S5_N).transpose(2, 1, 3, 0, 4)
    return mixer, new_ckv, new_krope, new_state


def _odd_mixer(y, mods_l, g, pb, phy, ctx_k, ctx_v, lam_init):
    w_in, w_out, lam_p, subln = pb
    hy_u, q, (k_p, k_s), (v_p, v_s), v_cache = _inproj_b(y, mods_l, g, w_in)
    hy_p = _hyena_conv(hy_u, _hyena_spectrum(L_P, phy), phy, NB_P, L_P, 2, 0)
    hy_s = _hyena_conv(hy_u, _hyena_spectrum(L_S, phy), phy, NB_S, L_S, 1, TOK_P)
    ctx = (ctx_k.reshape(NB_S * PAST, DFW), ctx_v.reshape(NB_S * PAST, DF_HEADS * DF_V))
    att_p = _diff_attention(q, k_p, v_p, lam_p, subln, lam_init, NB_P, L_P, L_P, 0, nseq=2)
    att_s = _diff_attention(q, k_s, v_s, lam_p, subln, lam_init, NB_S, L_S, TM // 2, TOK_P, ctx=ctx)
    mixer = ((hy_p, hy_s), (att_p, att_s), w_out, None)
    new_k = k_p.reshape(NB_P, L_P, DF_HEADS, 2, DF_DH)
    new_v = v_cache.reshape(NB_P, L_P, DF_HEADS, DF_V)
    return mixer, new_k, new_v


def kernel(x_prompt, x_sample, c, c_ctx, cache_mla_ckv, cache_mla_krope, state_s5, cache_diff_k, cache_diff_v, ada_w, ada_b, norm_g, ff_w_in, ff_w_out, w_in_a, w_out_a, mla_q_norm, mla_w_uq, mla_kv_norm, mla_w_ukv, s5_a_re, s5_a_im, s5_log_step, s5_b_re, s5_b_im, s5_c_re, s5_c_im, s5_d, s5_w_glu, w_in_b, w_out_b, hy_conv, hy_w1, hy_b1, hy_w2, hy_b2, hy_freq, hy_w3, hy_decay, hy_bias, df_lambda, df_subln, final_norm):
    depth = ada_w.shape[0]
    y = (x_prompt.reshape(TOK_P, D), x_sample.reshape(TOK_S, D))
    mods = _adaln(jnp.concatenate([c_ctx[None], c], axis=0), ada_w, ada_b)
    new_ckv, new_krope, new_s5, new_dk, new_dv = [], [], [], [], []
    for l in range(depth):
        y = _half_ffn(y, (mods, l), norm_g[l, 0], ff_w_in, ff_w_out, l, 0)
        if l % 2 == 0:
            e = l // 2
            pa = (w_in_a[e], w_out_a[e], mla_q_norm[e], mla_w_uq[e], mla_kv_norm[e], mla_w_ukv[e])
            ps5 = (s5_a_re[e], s5_a_im[e], s5_log_step[e], s5_b_re[e], s5_b_im[e],
                   s5_c_re[e], s5_c_im[e], s5_d[e], s5_w_glu[e])
            mixer, ckv, krope, st = _even_mixer(y, (mods, l), norm_g[l, 1], pa, ps5, cache_mla_ckv[:, e],
                                            cache_mla_krope[:, e], state_s5[:, e])
            new_ckv.append(ckv)
            new_krope.append(krope)
            new_s5.append(st)
        else:
            o = l // 2
            lam_init = 0.8 - 0.6 * math.exp(-0.3 * l)
            pb = (w_in_b[o], w_out_b[o], df_lambda[o], df_subln[o])
            phy = (hy_conv[o], hy_w1[o], hy_b1[o], hy_w2[o], hy_b2[o], hy_freq[o],
                   hy_w3[o], hy_decay[o], hy_bias[o])
            mixer, dk, dv = _odd_mixer(y, (mods, l), norm_g[l, 1], pb, phy, cache_diff_k[:, o],
                                   cache_diff_v[:, o], lam_init)
            new_dk.append(dk)
            new_dv.append(dv)
        last = l == depth - 1
        y = _half_ffn(y, (mods, l), norm_g[l, 2], ff_w_in, ff_w_out, l, 1,
                      final_g=final_norm if last else None, mixer=mixer)
    y_prompt = y[0].reshape(NB_P, L_P, D)
    y_sample = y[1].reshape(NB_S, L_S, D)
    return (y_prompt, y_sample, jnp.stack(new_ckv, axis=1), jnp.stack(new_krope, axis=1),
            jnp.stack(new_s5, axis=1), jnp.stack(new_dk, axis=1), jnp.stack(new_dv, axis=1))
```

```python
import functools
import math

import numpy as np
import jax
import jax.numpy as jnp
from jax import lax
from jax.experimental import pallas as pl
from jax.experimental.pallas import tpu as pltpu

F32 = jnp.float32
BF16 = jnp.bfloat16

D = 1024
NB_P, L_P = 16, 256
NB_S, L_S = 2, 1024
PAST = 256
GRID_W = 64
N_MOD = 9
FF = 2816
EPS = 1e-6
ROPE_BASE = 10000.0

MLA_HEADS, MLA_NOPE, MLA_ROPE, MLA_V = 8, 64, 32, 64
MLA_Q_RANK, MLA_KV_RANK = 384, 256
S5_WIDTH, S5_GROUP, S5_N = 512, 16, 64
S5_GROUPS = S5_WIDTH // S5_GROUP
HY_WIDTH, HY_BANDS, HY_FH = 512, 16, 64
HY_EMB = 2 * HY_BANDS + 1
DF_HEADS, DF_DH = 8, 32
DF_V = 2 * DF_DH

TOK_P = NB_P * L_P
TOK_S = NB_S * L_S
TOK = TOK_P + TOK_S
TM = 512
NT = TOK // TM
NT_P = TOK_P // TM
TILES_PER_SAMPLE = L_S // TM

LANES = 128
S5_T = 16
S5_CW = S5_T * S5_GROUP
CH_P = L_P // S5_T
CH_S = L_S // S5_T
S5_ROWS = NB_P * CH_P + NB_S * CH_S
S5_ROWS_P = NB_P * CH_P

VMEM_LIMIT = 56 * 1024 * 1024


def _params(n_grid, vmem=None):
    return pltpu.CompilerParams(dimension_semantics=("arbitrary",) * n_grid,
                                vmem_limit_bytes=vmem)


def _const_spec(shape):
    nd = len(shape)
    return pl.BlockSpec(shape, lambda *_: (0,) * nd, pipeline_mode=pl.Buffered(1))


def _mod_index(i):
    return jnp.where(i < NT_P, 0, 1 + (i - NT_P) // TILES_PER_SAMPLE)


def _mod_spec(layer):
    return pl.BlockSpec((1, 8, N_MOD * D), lambda *_: (layer, 0, 0), pipeline_mode=pl.Buffered(1))


def _mod_rows(mod_ref):
    row = mod_ref[0, pl.ds(_mod_index(pl.program_id(0)), 1), :]
    return [row[:, k * D:(k + 1) * D] for k in range(N_MOD)]


def _pos_index(i):
    return jnp.where(i < NT_P, 0, 1 + (i - NT_P) % TILES_PER_SAMPLE)


def _row(i):
    return (i, 0)


def _row_p(i):
    return (jnp.minimum(i, NT_P - 1), 0)


def _row_s(i):
    return (jnp.maximum(i - NT_P, 0), 0)


def _tok_specs(x, width):
    if isinstance(x, tuple):
        return [pl.BlockSpec((TM, width), _row_p), pl.BlockSpec((TM, width), _row_s)], list(x)
    return [pl.BlockSpec((TM, width), _row)], [x]


def _tok_read(refs, split):
    if split:
        return jnp.where(pl.program_id(0) < NT_P, refs[0][...], refs[1][...]), refs[2:]
    return refs[0][...], refs[1:]


def _tok_write(p_ref, s_ref, value):
    i = pl.program_id(0)

    @pl.when(i < NT_P)
    def _():
        p_ref[...] = value

    @pl.when(i >= NT_P)
    def _():
        s_ref[...] = value.astype(s_ref.dtype)


def _split_out(width, sample_dtype=F32):
    shapes = [jax.ShapeDtypeStruct((TOK_P, width), F32), jax.ShapeDtypeStruct((TOK_S, width), sample_dtype)]
    specs = [pl.BlockSpec((TM, width), _row_p), pl.BlockSpec((TM, width), _row_s)]
    return shapes, specs


def _dot(a, b):
    return jnp.dot(a.astype(BF16), b.astype(BF16), preferred_element_type=F32)


def _dot_nt(a, b):
    return lax.dot_general(a, b, (((1,), (1,)), ((), ())), preferred_element_type=F32)


def _split(x):
    hi = x.astype(BF16)
    lo = (x - hi.astype(F32)).astype(BF16)
    return hi, lo


def _dot3(a, b):
    ah, al = _split(a)
    bh, bl = _split(b)
    d = functools.partial(jnp.dot, preferred_element_type=F32)
    return d(ah, bh) + d(ah, bl) + d(al, bh)


def _rmsnorm(x, g):
    return x * lax.rsqrt(jnp.mean(x * x, axis=-1, keepdims=True) + EPS) * g


def _modulate(y, g, shift, scale):
    return _rmsnorm(y, g) * (1.0 + scale) + shift


def _pair_swap(x):
    n = x.shape[-1]
    lane = lax.broadcasted_iota(jnp.int32, x.shape, x.ndim - 1)
    return jnp.where((lane & 1) == 0, pltpu.roll(x, n - 1, x.ndim - 1), pltpu.roll(x, 1, x.ndim - 1))


def _rope(x, cos, sin_signed):
    return x * cos + _pair_swap(x) * sin_signed


def _rope_angles():
    n_freq = MLA_ROPE // 4
    inv = 1.0 / (ROPE_BASE ** (np.arange(n_freq, dtype=np.float64) / n_freq))
    pos = np.arange(L_S)
    row = (pos // GRID_W).astype(np.float64)
    col = (pos % GRID_W).astype(np.float64)
    ang = np.concatenate([row[:, None] * inv, col[:, None] * inv], axis=-1)
    return np.cos(ang), np.sin(ang)


@functools.lru_cache(maxsize=None)
def _rope_tables(width, starts):
    cos, sin = _rope_angles()
    c = np.ones((TM + L_S, width), np.float32)
    s = np.zeros((TM + L_S, width), np.float32)
    sign = np.where(np.arange(MLA_ROPE) % 2 == 0, -1.0, 1.0)
    unit_c = np.repeat(cos, 2, axis=1)
    unit_s = np.repeat(sin, 2, axis=1) * sign
    for st in starts:
        c[TM:, st:st + MLA_ROPE] = unit_c
        s[TM:, st:st + MLA_ROPE] = unit_s
    return c, s


@functools.lru_cache(maxsize=None)
def _dft_tables(L):
    f = np.arange(L)[:, None]
    s = np.arange(L)[None, :]
    ang = np.pi * ((f * s) % (2 * L)).astype(np.float64) / L
    cs = np.concatenate([np.cos(ang), np.sin(ang)], axis=0)
    cs[L, :] = np.where(np.arange(L) % 2 == 0, 1.0, -1.0)
    cs = cs.astype(np.float32)
    return cs, np.ascontiguousarray(cs.T)


@functools.lru_cache(maxsize=None)
def _hyena_features(L):
    t = np.arange(L, dtype=np.float64) / L
    bands = np.arange(1, HY_BANDS + 1, dtype=np.float64)
    ang = 2.0 * math.pi * t[:, None] * bands
    feat = np.zeros((L, LANES), np.float32)
    feat[:, 0] = t
    feat[:, 1:1 + HY_BANDS] = np.cos(ang)
    feat[:, 1 + HY_BANDS:HY_EMB] = np.sin(ang)
    return feat


def _adaln_kernel(c_ref, w_ref, b_ref, o_ref):
    s = jax.nn.silu(c_ref[...])
    s_hi = s.astype(BF16).astype(F32)
    stacked = jnp.concatenate([s_hi, s - s_hi], axis=0).astype(BF16)
    wh, wl = _split(w_ref[0])
    both = jnp.dot(stacked, wh, preferred_element_type=F32)
    rows = c_ref.shape[0]
    o_ref[0] = both[:rows] + both[rows:] + jnp.dot(stacked, wl, preferred_element_type=F32)[:rows] + b_ref[0]


def _adaln(cvecs, ada_w, ada_b):
    depth = ada_w.shape[0]
    n_vec = cvecs.shape[0]
    tn = N_MOD * D // 4
    out = pl.pallas_call(
        _adaln_kernel,
        out_shape=jax.ShapeDtypeStruct((depth, 8, N_MOD * D), F32),
        grid=(depth, N_MOD * D // tn),
        in_specs=[pl.BlockSpec((8, D), lambda l, j: (0, 0)),
                  pl.BlockSpec((1, D, tn), lambda l, j: (l, 0, j)),
                  pl.BlockSpec((1, 1, tn), lambda l, j: (l, 0, j))],
        out_specs=pl.BlockSpec((1, 8, tn), lambda l, j: (l, 0, j)),
        compiler_params=_params(2, VMEM_LIMIT),
        name="adaln",
    )(jnp.pad(cvecs, ((0, 8 - n_vec), (0, 0))), ada_w, ada_b[:, None, :])
    return out


FF_PIECE = 256
FF_LOADS = FF // FF_PIECE


def _ffn_kernel(base, final, split_in, mixer, layer, which, *refs):
    y, refs = _tok_read(refs, split_in)
    if mixer is not None:
        a1, refs = _tok_read(refs, mixer[0])
        if mixer[2]:
            a2_chunks, refs = refs[0], refs[1:]
        else:
            a2, refs = _tok_read(refs, mixer[1])
        wmix_ref, wg_ref = refs[:2]
        refs = refs[2:]
    mod_ref, g_ref, win_hbm, wout_hbm, fg_ref = refs[:5]
    n_out = 2 if final else 1
    outs = refs[5:5 + n_out]
    win_ref, wout_ref, stage_g, stage_u, stage_o, sems = refs[5 + n_out:11 + n_out]
    mod = _mod_rows(mod_ref)
    if mixer is not None:
        if mixer[2]:
            a2_scr = refs[11 + n_out]
            _chunks_to_tokens(a2_chunks, a2_scr)
            a2 = jax.nn.gelu(jnp.concatenate([a2_scr[o] for o in range(a2_scr.shape[0])], axis=1))
            a2 = a2 * jax.nn.sigmoid(_dot(a2, wg_ref[...]))
        k1 = wmix_ref.shape[0] // 2
        y = y + mod[5] * (_dot(a1, wmix_ref[:k1]) + _dot(a2, wmix_ref[k1:]))
    h = _modulate(y, g_ref[...], mod[base], mod[base + 1]).astype(BF16)

    def hidden(lo, width):
        gate = jnp.dot(h, win_ref[:, lo:lo + width], preferred_element_type=F32)
        up = jnp.dot(h, win_ref[:, FF + lo:FF + lo + width], preferred_element_type=F32)
        a = (jax.nn.silu(gate) * up).astype(BF16)
        return jnp.dot(a, wout_ref[lo:lo + width, :], preferred_element_type=F32)

    def finish(acc):
        out = y + 0.5 * mod[base + 2] * acc
        if final:
            _tok_write(outs[0], outs[1], _rmsnorm(out, fg_ref[...]))
        else:
            outs[0][...] = out

    @pl.when(pl.program_id(0) == 0)
    def _():
        def copies(c, slot):
            cols = pl.ds(c * FF_PIECE, FF_PIECE)
            return (pltpu.make_async_copy(win_hbm.at[layer, which, :, cols], stage_g.at[slot], sems.at[0, slot]),
                    pltpu.make_async_copy(win_hbm.at[layer, which, :, pl.ds(FF + c * FF_PIECE, FF_PIECE)],
                                          stage_u.at[slot], sems.at[1, slot]),
                    pltpu.make_async_copy(wout_hbm.at[layer, which, cols, :], stage_o.at[slot], sems.at[2, slot]))

        for cp in copies(0, 0):
            cp.start()
        acc = jnp.zeros(y.shape, F32)
        for c in range(FF_LOADS):
            slot = c % 2
            lo = c * FF_PIECE
            if c + 1 < FF_LOADS:
                for cp in copies(c + 1, 1 - slot):
                    cp.start()
            for cp in copies(c, slot):
                cp.wait()
            win_ref[:, lo:lo + FF_PIECE] = stage_g[slot].astype(BF16)
            win_ref[:, FF + lo:FF + lo + FF_PIECE] = stage_u[slot].astype(BF16)
            wout_ref[lo:lo + FF_PIECE, :] = stage_o[slot].astype(BF16)
            acc = acc + hidden(lo, FF_PIECE)
        finish(acc)

    @pl.when(pl.program_id(0) > 0)
    def _():
        finish(hidden(0, FF))


def _half_ffn(y, mods_l, g, ff_w_in, ff_w_out, layer, which, final_g=None, mixer=None):
    final = final_g is not None
    fg = final_g if final else g
    y_specs, y_args = _tok_specs(y, D)
    mix_flags = None
    extra_scratch = []
    if mixer is not None:
        a1, a2, w_out, w_glu = mixer
        k1 = w_out.shape[0] // 2
        wg = w_glu if w_glu is not None else jnp.zeros((8, LANES), F32)
        s1, a1_args = _tok_specs(a1, k1)
        if w_glu is not None:
            s2 = [pl.BlockSpec((S5_GROUPS, TM // S5_T, S5_CW), lambda i: (0, i, 0))]
            a2_args = [a2]
            extra_scratch = [pltpu.VMEM((k1 // LANES, TM, LANES), F32)]
        else:
            s2, a2_args = _tok_specs(a2, k1)
        y_specs = y_specs + s1 + s2 + [_const_spec(w_out.shape), _const_spec(wg.shape)]
        y_args = y_args + a1_args + a2_args + [w_out, wg]
        mix_flags = (isinstance(a1, tuple), isinstance(a2, tuple), w_glu is not None)
    if final:
        out_shape, out_specs = _split_out(D)
    else:
        out_shape, out_specs = jax.ShapeDtypeStruct((TOK, D), F32), pl.BlockSpec((TM, D), _row)
    return pl.pallas_call(
        functools.partial(_ffn_kernel, 6 * which, final, isinstance(y, tuple), mix_flags, layer, which),
        out_shape=out_shape,
        grid=(NT,),
        in_specs=y_specs + [_mod_spec(mods_l[1]),
                            _const_spec((1, D)),
                            pl.BlockSpec(memory_space=pl.ANY),
                            pl.BlockSpec(memory_space=pl.ANY),
                            _const_spec((1, D))],
        out_specs=out_specs,
        scratch_shapes=[pltpu.VMEM((D, 2 * FF), BF16), pltpu.VMEM((FF, D), BF16),
                        pltpu.VMEM((2, D, FF_PIECE), F32), pltpu.VMEM((2, D, FF_PIECE), F32),
                        pltpu.VMEM((2, FF_PIECE, D), F32), pltpu.SemaphoreType.DMA((3, 2))] + extra_scratch,
        compiler_params=_params(1, VMEM_LIMIT),
        name="half_ffn",
    )(*y_args, mods_l[0], g[None], ff_w_in, ff_w_out, fg[None])


def _linear_kernel(x_ref, w_ref, o_ref):
    o_ref[...] = _dot(x_ref[...], w_ref[...]).astype(o_ref.dtype)


def _linear(x, w, tm, out_dtype):
    m, k = x.shape
    n = w.shape[1]
    return pl.pallas_call(
        _linear_kernel,
        out_shape=jax.ShapeDtypeStruct((m, n), out_dtype),
        grid=(m // tm,),
        in_specs=[pl.BlockSpec((tm, k), lambda i: (i, 0)), _const_spec((k, n))],
        out_specs=pl.BlockSpec((tm, n), lambda i: (i, 0)),
        compiler_params=_params(1),
        name="linear",
    )(x, w.astype(BF16))


LOG2E = math.log2(math.e)
MLA_SCALE = (MLA_NOPE + MLA_ROPE) ** -0.5 * LOG2E
QW = MLA_HEADS * LANES
KR_AT = MLA_NOPE
IN_A_PAD = MLA_Q_RANK + MLA_KV_RANK + S5_WIDTH + LANES


def _inproj_a_kernel(y_ref, mod_ref, g_ref, win_ref, qn_ref, wuq_ref, kvn_ref, wk_ref, wv_ref,
                     cq_ref, sq_ref, ck_ref, sk_ref,
                     q_ref, ckv_ref, kru_ref, krr_ref, kn_ref, v_ref, ug_ref, u_scr):
    mod = _mod_rows(mod_ref)
    h = _modulate(y_ref[...], g_ref[...], mod[3], mod[4]).astype(BF16)
    p = jnp.dot(h, win_ref[...], preferred_element_type=F32)
    o1 = MLA_Q_RANK
    o2 = o1 + MLA_KV_RANK
    o3 = o2 + S5_WIDTH
    q = _dot(_rmsnorm(p[:, :o1], qn_ref[...]), wuq_ref[...])
    q_ref[...] = (_rope(q, cq_ref[...], sq_ref[...]) * MLA_SCALE).astype(BF16)
    ckv = _rmsnorm(p[:, o1:o2], kvn_ref[...])
    ckv_b = ckv.astype(BF16)
    kn_ref[...] = jnp.dot(ckv_b, wk_ref[...], preferred_element_type=F32).astype(BF16)
    v_ref[...] = jnp.dot(ckv_b, wv_ref[...], preferred_element_type=F32).astype(BF16)
    for octet in range(S5_WIDTH // LANES):
        u_scr[octet] = p[:, o2 + octet * LANES:o2 + (octet + 1) * LANES]
    _tokens_to_chunks(u_scr, ug_ref)
    krp = p[:, o3:]
    krr_ref[...] = _rope(krp, ck_ref[...], sk_ref[...]).astype(BF16)

    @pl.when(pl.program_id(0) < NT_P)
    def _():
        ckv_ref[...] = ckv
        kru_ref[...] = krp[:, KR_AT:KR_AT + MLA_ROPE]


def _inproj_a(y, mods_l, g, w_in, q_norm, w_uq, kv_norm, w_ukv):
    o1 = MLA_Q_RANK
    o2 = o1 + MLA_KV_RANK
    o3 = o2 + MLA_ROPE
    kr_cols = jnp.pad(w_in[:, o2:o3], ((0, 0), (KR_AT, LANES - KR_AT - MLA_ROPE)))
    w_ext = jnp.concatenate([w_in[:, :o2], w_in[:, o3:], kr_cols], axis=1).astype(BF16)
    dq = MLA_NOPE + MLA_ROPE
    w_uq_pad = jnp.pad(w_uq.reshape(MLA_Q_RANK, MLA_HEADS, dq),
                       ((0, 0), (0, 0), (0, LANES - dq))).reshape(MLA_Q_RANK, QW).astype(BF16)
    w_kv = w_ukv.reshape(MLA_KV_RANK, MLA_HEADS, MLA_NOPE + MLA_V)
    w_k = jnp.pad(w_kv[:, :, :MLA_NOPE], ((0, 0), (0, 0), (0, LANES - MLA_NOPE))).reshape(MLA_KV_RANK, QW)
    w_v = w_kv[:, :, MLA_NOPE:].reshape(MLA_KV_RANK, MLA_HEADS * MLA_V)
    w_k, w_v = w_k.astype(BF16), w_v.astype(BF16)
    cq, sq = _rope_tables(QW, tuple(h * LANES + MLA_NOPE for h in range(MLA_HEADS)))
    ck, sk = _rope_tables(LANES, (KR_AT,))
    row = _row
    pos = lambda i: (_pos_index(i), 0)
    widths = (QW, MLA_KV_RANK, MLA_ROPE, LANES, QW, MLA_HEADS * MLA_V)
    prompt_only = (1, 2)
    mxu_only = (0, 3, 4, 5)
    outs = pl.pallas_call(
        _inproj_a_kernel,
        out_shape=[jax.ShapeDtypeStruct((TOK_P if k in prompt_only else TOK, w), BF16 if k in mxu_only else F32)
                   for k, w in enumerate(widths)]
                  + [jax.ShapeDtypeStruct((S5_GROUPS, S5_ROWS, S5_CW), F32)],
        grid=(NT,),
        in_specs=[pl.BlockSpec((TM, D), row),
                  _mod_spec(mods_l[1]),
                  _const_spec((1, D)),
                  _const_spec((D, IN_A_PAD)),
                  _const_spec((1, MLA_Q_RANK)),
                  _const_spec((MLA_Q_RANK, QW)),
                  _const_spec((1, MLA_KV_RANK)),
                  _const_spec((MLA_KV_RANK, QW)),
                  _const_spec((MLA_KV_RANK, MLA_HEADS * MLA_V)),
                  pl.BlockSpec((TM, QW), pos), pl.BlockSpec((TM, QW), pos),
                  pl.BlockSpec((TM, LANES), pos), pl.BlockSpec((TM, LANES), pos)],
        out_specs=[pl.BlockSpec((TM, w), _row_p if k in prompt_only else row)
                   for k, w in enumerate(widths)]
                  + [pl.BlockSpec((S5_GROUPS, TM // S5_T, S5_CW), lambda i: (0, i, 0))],
        scratch_shapes=[pltpu.VMEM((S5_WIDTH // LANES, TM, LANES), F32)],
        compiler_params=_params(1, VMEM_LIMIT),
        name="inproj_even",
    )(y, mods_l[0], g[None], w_ext, q_norm[None], w_uq_pad, kv_norm[None], w_k, w_v,
      jnp.asarray(cq), jnp.asarray(sq), jnp.asarray(ck), jnp.asarray(sk))
    q, ckv, kr_unrot, kr_rot, kn, v, ug = outs
    return q, ckv, kr_unrot, kr_rot, kn, v, ug, (w_k, w_v)


def _softmax_pv(scores, vals, half):
    m = functools.reduce(jnp.maximum, [jnp.max(s, axis=-1, keepdims=True) for s in scores])
    pv = None
    for s, v in zip(scores, vals):
        own_k = (lax.broadcasted_iota(jnp.int32, v.shape, 1) >> 6) == half
        part = jnp.dot(jnp.exp2(s - m).astype(BF16), jnp.where(own_k, v, jnp.ones_like(v)),
                       preferred_element_type=F32)
        pv = part if pv is None else pv + part
    own = (lax.broadcasted_iota(jnp.int32, pv.shape, 1) >> 6) == half
    denom = jnp.max(jnp.where(own, 0.0, pv), axis=-1, keepdims=True)
    return pv * (1.0 / denom)


def _mla_attn_kernel(nseg, nseq, q_ref, *refs):
    o_ref = refs[-1]
    tq = q_ref.shape[0] // nseq
    lane = lax.broadcasted_iota(jnp.int32, (tq, LANES), 1)
    for j in range(nseq):
        qr = slice(j * tq, (j + 1) * tq)
        krs = [slice(j * (refs[3 * s].shape[0] // nseq), (j + 1) * (refs[3 * s].shape[0] // nseq))
               for s in range(nseg)]
        for pair in range(MLA_HEADS // 2):
            outs = []
            for hh in range(2):
                h = 2 * pair + hh
                hs = slice(h * LANES, (h + 1) * LANES)
                qh = q_ref[qr, hs]
                scores = []
                for s in range(nseg):
                    kn_ref, kr_ref = refs[3 * s], refs[3 * s + 1]
                    kh = (kn_ref[krs[s], hs] + kr_ref[krs[s], :]).astype(BF16)
                    scores.append(_dot_nt(qh, kh))
                vals = [refs[3 * s + 2][krs[s], pair * LANES:(pair + 1) * LANES] for s in range(nseg)]
                outs.append(_softmax_pv(scores, vals, hh))
            o_ref[qr, pair * LANES:(pair + 1) * LANES] = jnp.where(lane < MLA_V, outs[0], outs[1]).astype(o_ref.dtype)


def _mla_attention(q, kn, kr, v, n_batch, seq, tq, row0, ctx=None, nseq=1):
    qt = seq // tq
    qb0, kb0 = row0 // (nseq * tq), row0 // (nseq * seq)
    in_specs = [pl.BlockSpec((nseq * tq, QW), lambda b, j: (qb0 + b * qt + j, 0))]
    args = [q]
    segs = []
    if ctx is not None:
        segs.append((ctx, PAST, 0))
    segs.append(((kn, kr, v), seq, kb0))
    for (a_kn, a_kr, a_v), ln, off in segs:
        idx = lambda b, j, off=off: (off + b, 0)
        in_specs += [pl.BlockSpec((nseq * ln, QW), idx), pl.BlockSpec((nseq * ln, LANES), idx),
                     pl.BlockSpec((nseq * ln, MLA_HEADS * MLA_V), idx)]
        args += [a_kn, a_kr, a_v]
    return pl.pallas_call(
        functools.partial(_mla_attn_kernel, len(segs), nseq),
        out_shape=jax.ShapeDtypeStruct((n_batch * seq, MLA_HEADS * MLA_V), BF16),
        grid=(n_batch // nseq, qt),
        in_specs=in_specs,
        out_specs=pl.BlockSpec((nseq * tq, MLA_HEADS * MLA_V), lambda b, j: (b * qt + j, 0)),
        compiler_params=_params(2, VMEM_LIMIT),
        name="mla_attention",
    )(*args)


def _cpow(ar, ai, e, nbits):
    rr = jnp.ones_like(ar)
    ri = jnp.zeros_like(ar)
    br, bi = ar, ai
    for k in range(nbits):
        bit = ((e >> k) & 1) == 1
        nr = rr * br - ri * bi
        ni = rr * bi + ri * br
        rr = jnp.where(bit, nr, rr)
        ri = jnp.where(bit, ni, ri)
        if k + 1 < nbits:
            br, bi = br * br - bi * bi, 2.0 * br * bi
    return rr, ri


def _s5_abar_kernel(lr_ref, li_ref, ls_ref, o_ref):
    step = jnp.exp(ls_ref[...])
    lr = jnp.minimum(lr_ref[...], -1e-4)
    li = li_ref[...]
    mag = jnp.exp(lr * step)
    ar = mag * jnp.cos(li * step)
    ai = mag * jnp.sin(li * step)
    den = lr * lr + li * li
    o_ref[0] = ar
    o_ref[1] = ai
    o_ref[2] = ((ar - 1.0) * lr + ai * li) / den
    o_ref[3] = (ai * lr - (ar - 1.0) * li) / den


def _s5_prep_kernel(arow_ref, acol_ref, btr_ref, bti_ref, ctr_ref, cti_ref,
                    wi_ref, ws_ref, wo_ref, ap_ref):
    n2 = 2 * S5_N
    blk_o = lax.broadcasted_iota(jnp.int32, (S5_N, S5_CW), 1) >> 4
    lane_k = lax.broadcasted_iota(jnp.int32, (S5_GROUP, S5_CW), 1)
    row_k = lax.broadcasted_iota(jnp.int32, (S5_GROUP, S5_CW), 0)
    lane_b = lax.broadcasted_iota(jnp.int32, (S5_GROUP, n2), 1)
    lane_a = lax.broadcasted_iota(jnp.int32, (1, n2), 1)
    rep = ((lane_k & (S5_GROUP - 1)) == row_k).astype(BF16)

    def tile16(x):
        hi = x.astype(BF16)
        r1 = x - hi.astype(F32)
        mid = r1.astype(BF16)
        lo = (r1 - mid.astype(F32)).astype(BF16)
        d = lambda a: lax.dot_general(a, rep, (((0,), (0,)), ((), ())), preferred_element_type=F32)
        return d(hi) + d(mid) + d(lo)

    intra = [None] * S5_T
    for d in range(2):
        ar, ai, fr, fi = (arow_ref[d, 0, k:k + 1, :] for k in range(4))
        btr, bti = btr_ref[d, 0], bti_ref[d, 0]
        bbr = fr * btr - fi * bti
        bbi = fr * bti + fi * btr
        pws = [(jnp.ones_like(ar), jnp.zeros_like(ar))]
        for _ in range(S5_T):
            pr, pi = pws[-1]
            pws.append((pr * ar - pi * ai, pr * ai + pi * ar))
        for s in range(S5_T):
            pr, pi = pws[S5_T - 1 - s] if d == 0 else pws[s]
            ws_ref[d, 0, s * S5_GROUP:(s + 1) * S5_GROUP, :] = jnp.where(
                lane_b < S5_N, pr * bbr - pi * bbi, pr * bbi + pi * bbr).astype(BF16)

        acol = acol_ref[d, 0]
        arc = jnp.broadcast_to(acol[:, 0:1], (S5_N, S5_CW))
        aic = jnp.broadcast_to(acol[:, 1:2], (S5_N, S5_CW))
        ctr, cti = tile16(ctr_ref[d, 0]), tile16(cti_ref[d, 0])
        e_lag = blk_o if d == 0 else (S5_T - 1 - blk_o)
        pqr, pqi = _cpow(arc, aic, e_lag, 4)
        qr = pqr * ctr - pqi * cti
        qi = pqr * cti + pqi * ctr
        wo_ref[d, 0] = jnp.concatenate([qr * arc - qi * aic, -(qr * aic + qi * arc)], axis=0).astype(BF16)
        q_stack = jnp.concatenate([qr, qi], axis=0)
        bb_mix = jnp.where(lane_b < S5_N, bbr, -bbi)
        kt = _dot3(bb_mix, q_stack)
        for s in range(S5_T):
            if d == 0:
                blk = jnp.where(lane_k >= S5_GROUP * s, pltpu.roll(kt, S5_GROUP * s, 1), 0.0)
            else:
                blk = jnp.where(lane_k < S5_GROUP * (s + 1),
                                pltpu.roll(kt, (S5_GROUP * (s + 1)) % S5_CW, 1), 0.0)
            intra[s] = blk if intra[s] is None else intra[s] + blk

        pr1, pi1 = pws[S5_T]
        for k in range(6):
            ap_ref[d, 0, k:k + 1, :] = pr1
            ap_ref[d, 0, 8 + k:9 + k, :] = jnp.where(lane_a < S5_N, -pi1, pi1)
            pr1, pi1 = pr1 * pr1 - pi1 * pi1, 2.0 * pr1 * pi1
        ap_ref[d, 0, 6:8, :] = jnp.zeros((2, n2), F32)
        ap_ref[d, 0, 14:16, :] = jnp.zeros((2, n2), F32)
    for s in range(S5_T):
        wi_ref[0, s * S5_GROUP:(s + 1) * S5_GROUP, :] = intra[s].astype(BF16)


def _s5_prep(a_re, a_im, log_step, b_re, b_im, c_re, c_im):
    g, n, n2 = S5_GROUPS, S5_N, 2 * S5_N
    abar = pl.pallas_call(
        _s5_abar_kernel,
        out_shape=jax.ShapeDtypeStruct((4, 2 * g, n), F32),
        grid=(1,),
        in_specs=[_const_spec((2 * g, n)), _const_spec((2 * g, n)), _const_spec((2 * g, 1))],
        out_specs=pl.BlockSpec((4, 2 * g, n), lambda i: (0, 0, 0)),
        compiler_params=_params(1),
        name="s5_abar",
    )(a_re.reshape(2 * g, n), a_im.reshape(2 * g, n), log_step.reshape(2 * g, 1))
    abar = jnp.concatenate([abar, abar], axis=-1).reshape(4, 2, g, n2)
    arow = abar.transpose(1, 2, 0, 3)
    acol = abar[:2, :, :, :n].transpose(1, 2, 3, 0)
    bt = lambda b: jnp.concatenate([jnp.swapaxes(b, 2, 3)] * 2, axis=-1)
    spec4 = lambda r, c: pl.BlockSpec((2, 1, r, c), lambda i: (0, i, 0, 0))
    return pl.pallas_call(
        _s5_prep_kernel,
        out_shape=[jax.ShapeDtypeStruct((g, S5_CW, S5_CW), BF16),
                   jax.ShapeDtypeStruct((2, g, S5_CW, n2), BF16),
                   jax.ShapeDtypeStruct((2, g, n2, S5_CW), BF16),
                   jax.ShapeDtypeStruct((2, g, 16, n2), F32)],
        grid=(g,),
        in_specs=[spec4(4, n2), spec4(n, 2),
                  spec4(S5_GROUP, n2), spec4(S5_GROUP, n2), spec4(S5_GROUP, n), spec4(S5_GROUP, n)],
        out_specs=[pl.BlockSpec((1, S5_CW, S5_CW), lambda i: (i, 0, 0)),
                   spec4(S5_CW, n2), spec4(n2, S5_CW), spec4(16, n2)],
        compiler_params=_params(1),
        name="s5_prep",
    )(arow, acol, bt(b_re), bt(b_im), c_re, c_im)


def _cmul_rows(x, p1, p2):
    return x * p1 + pltpu.roll(x, S5_N, 1) * p2


S5_OCT = LANES // S5_GROUP


def _block_transpose(xs):
    n = S5_OCT
    blk = lax.broadcasted_iota(jnp.int32, xs[0].shape, 1) >> 4
    a = [pltpu.roll(x, i * S5_GROUP, 1) if i else x for i, x in enumerate(xs)]
    ys = []
    for d in range(n):
        diag = a[-d % n]
        for b in range(1, n):
            diag = jnp.where(blk == b, a[(b - d) % n], diag)
        ys.append(pltpu.roll(diag, LANES - d * S5_GROUP, 1) if d else diag)
    return ys


def _tokens_to_chunks(u_ref, ug_ref):
    rows = u_ref.shape[1] // S5_T
    for octet in range(S5_GROUPS // S5_OCT):
        for half in range(2):
            xs = [u_ref[octet, pl.ds(S5_OCT * half + tt, rows, stride=S5_T), :] for tt in range(S5_OCT)]
            for gl, x in enumerate(_block_transpose(xs)):
                ug_ref[octet * S5_OCT + gl, :, half * LANES:(half + 1) * LANES] = x


def _chunks_to_tokens(yg_ref, y_ref):
    rows = y_ref.shape[1] // S5_T
    for octet in range(S5_GROUPS // S5_OCT):
        for half in range(2):
            ys = [yg_ref[octet * S5_OCT + gl, :, half * LANES:(half + 1) * LANES] for gl in range(S5_OCT)]
            for tt, y in enumerate(_block_transpose(ys)):
                y_ref[octet, pl.ds(S5_OCT * half + tt, rows, stride=S5_T), :] = y


def _s5_core_kernel(ug_ref, wi_ref, ws_ref, wo_ref, ap_ref, h0_ref, dv_ref, yg_ref, fin_ref, z_ref):
    n2 = 2 * S5_N
    r = lax.broadcasted_iota(jnp.int32, (S5_ROWS, n2), 0)
    in_p = r < S5_ROWS_P
    rib = jnp.where(in_p, r & (CH_P - 1), (r - S5_ROWS_P) & (CH_S - 1))
    nch = jnp.where(in_p, CH_P, CH_S)

    def one_group(gl, slot):
        ub = ug_ref[gl].astype(BF16)
        y = jnp.dot(ub, wi_ref[gl], preferred_element_type=F32)
        for d in range(2):
            p1, p2 = ap_ref[d, gl, 0:1, :], ap_ref[d, gl, 8:9, :]
            edge = [S5_ROWS_P + CH_S * b + (0 if d == 0 else CH_S - 1) for b in range(NB_S)]
            h0 = [h0_ref[gl, d, b:b + 1, :] for b in range(NB_S)]
            s = jnp.dot(ub, ws_ref[d, gl], preferred_element_type=F32)
            for b in range(NB_S):
                s = s + jnp.where(r == edge[b], _cmul_rows(h0[b], p1, p2), 0.0)
            def scan_step(x, k, pos, count):
                sh = 1 << k
                if d == 0:
                    t = jnp.where(pos >= sh, pltpu.roll(x, sh, 0), 0.0)
                else:
                    t = jnp.where(pos < count - sh, pltpu.roll(x, x.shape[0] - sh, 0), 0.0)
                return x + _cmul_rows(t, ap_ref[d, gl, k:k + 1, :], ap_ref[d, gl, 8 + k:9 + k, :])

            for k in range(CH_P.bit_length() - 1):
                s = scan_step(s, k, rib, nch)
            tail = s[S5_ROWS_P:]
            for k in range(CH_P.bit_length() - 1, CH_S.bit_length() - 1):
                tail = scan_step(tail, k, rib[S5_ROWS_P:], CH_S)
            s = jnp.concatenate([s[:S5_ROWS_P], tail], axis=0)
            z_ref[slot, d] = s
            first = CH_P - 1 if d == 0 else 0
            fin_ref[gl, d] = z_ref[slot, d, pl.ds(first, NB_P, stride=CH_P), :]
            if d == 0:
                sp = jnp.where(rib >= 1, pltpu.roll(s, 1, 0), 0.0)
            else:
                sp = jnp.where(rib < nch - 1, pltpu.roll(s, S5_ROWS - 1, 0), 0.0)
            for b in range(NB_S):
                sp = jnp.where(r == edge[b], h0[b], sp)
            y = y + jnp.dot(sp.astype(BF16), wo_ref[d, gl], preferred_element_type=F32)
        yg_ref[gl] = y + dv_ref[gl] * ug_ref[gl]

    def group_pair(gp, carry):
        for slot in range(2):
            one_group(2 * gp + slot, slot)
        return carry

    lax.fori_loop(0, S5_OCT // 2, group_pair, 0)


def _s5_core(ug, prep, h0, d_skip):
    w_intra, w_state, w_out, apow = prep
    g, n2 = S5_GROUPS, 2 * S5_N
    spec4 = lambda r, c: pl.BlockSpec((2, S5_OCT, r, c), lambda i: (0, i, 0, 0))
    chunks = pl.BlockSpec((S5_OCT, S5_ROWS, S5_CW), lambda i: (i, 0, 0))
    dvec = jnp.tile(d_skip.reshape(g, 1, S5_GROUP), (1, 1, S5_T))
    return pl.pallas_call(
        _s5_core_kernel,
        out_shape=[jax.ShapeDtypeStruct((g, S5_ROWS, S5_CW), F32),
                   jax.ShapeDtypeStruct((g, 2, NB_P, n2), F32)],
        grid=(g // S5_OCT,),
        in_specs=[chunks,
                  pl.BlockSpec((S5_OCT, S5_CW, S5_CW), lambda i: (i, 0, 0)),
                  spec4(S5_CW, n2), spec4(n2, S5_CW), spec4(16, n2),
                  pl.BlockSpec((S5_OCT, 2, 8, n2), lambda i: (i, 0, 0, 0)),
                  pl.BlockSpec((S5_OCT, 1, S5_CW), lambda i: (i, 0, 0))],
        out_specs=[chunks, pl.BlockSpec((S5_OCT, 2, NB_P, n2), lambda i: (i, 0, 0, 0))],
        scratch_shapes=[pltpu.VMEM((2, 2, S5_ROWS, n2), F32)],
        compiler_params=_params(1, VMEM_LIMIT),
        name="s5_scan",
    )(ug, w_intra, w_state, w_out, apow, h0, dvec)


DF_SCALE = DF_DH ** -0.5 * LOG2E
DFW = DF_HEADS * 2 * DF_DH
IN_B = 3 * HY_WIDTH + 2 * DFW + DF_HEADS * DF_V


def _inproj_b_kernel(y_ref, mod_ref, g_ref, win_ref, c_ref, s_ref,
                     hy_ref, q_ref, kp_ref, ks_ref, vp_ref, vs_ref, vc_ref, wbf_ref):
    @pl.when(pl.program_id(0) == 0)
    def _():
        wbf_ref[...] = win_ref[...].astype(BF16)

    mod = _mod_rows(mod_ref)
    h = _modulate(y_ref[...], g_ref[...], mod[3], mod[4]).astype(BF16)
    p = jnp.dot(h, wbf_ref[...], preferred_element_type=F32)
    o1 = 3 * HY_WIDTH
    hy_ref[...] = p[:, :o1]
    q_ref[...] = (_rope(p[:, o1:o1 + DFW], c_ref[...], s_ref[...]) * DF_SCALE).astype(BF16)
    _tok_write(kp_ref, ks_ref, _rope(p[:, o1 + DFW:o1 + 2 * DFW], c_ref[...], s_ref[...]))
    v = p[:, o1 + 2 * DFW:]
    _tok_write(vp_ref, vs_ref, v.astype(BF16))

    @pl.when(pl.program_id(0) < NT_P)
    def _():
        for hd in range(DF_HEADS):
            vc_ref[pl.ds(hd, TM, stride=DF_HEADS), :] = v[:, hd * DF_V:(hd + 1) * DF_V]


def _inproj_b(y, mods_l, g, w_in):
    cs, sn = _rope_tables(DFW, tuple(range(0, DFW, DF_DH)))
    pos = lambda i: (_pos_index(i), 0)
    k_shapes, k_specs = _split_out(DFW, BF16)
    v_specs = _split_out(DF_HEADS * DF_V)[1]
    v_shapes = [jax.ShapeDtypeStruct((TOK_P, DF_HEADS * DF_V), BF16), jax.ShapeDtypeStruct((TOK_S, DF_HEADS * DF_V), BF16),
                jax.ShapeDtypeStruct((TOK_P * DF_HEADS, DF_V), F32)]
    v_specs = v_specs + [pl.BlockSpec((TM * DF_HEADS, DF_V), _row_p)]
    hy_u, q, kp, ks, vp, vs, v_cache = pl.pallas_call(
        _inproj_b_kernel,
        out_shape=[jax.ShapeDtypeStruct((TOK, 3 * HY_WIDTH), F32), jax.ShapeDtypeStruct((TOK, DFW), BF16)]
                  + k_shapes + v_shapes,
        grid=(NT,),
        in_specs=[pl.BlockSpec((TM, D), _row),
                  _mod_spec(mods_l[1]),
                  _const_spec((1, D)), _const_spec((D, IN_B)),
                  pl.BlockSpec((TM, DFW), pos), pl.BlockSpec((TM, DFW), pos)],
        out_specs=[pl.BlockSpec((TM, 3 * HY_WIDTH), _row), pl.BlockSpec((TM, DFW), _row)] + k_specs + v_specs,
        scratch_shapes=[pltpu.VMEM((D, IN_B), BF16)],
        compiler_params=_params(1, VMEM_LIMIT),
        name="inproj_odd",
    )(y, mods_l[0], g[None], w_in, jnp.asarray(cs), jnp.asarray(sn))
    return hy_u, q, (kp, ks), (vp, vs), v_cache


def _diff_attn_kernel(nseg, nseq, lam_init, q_ref, lam_ref, sub_ref, *refs):
    o_ref = refs[-1]
    lp = lam_ref[...]
    lam = (jnp.exp(jnp.sum(lp[0:1] * lp[1:2], axis=-1, keepdims=True))
           - jnp.exp(jnp.sum(lp[2:3] * lp[3:4], axis=-1, keepdims=True)) + lam_init)
    tq = q_ref.shape[0] // nseq
    lane = lax.broadcasted_iota(jnp.int32, (tq, LANES), 1)
    for j in range(nseq):
        qr = slice(j * tq, (j + 1) * tq)
        krs = [slice(j * (refs[2 * s].shape[0] // nseq), (j + 1) * (refs[2 * s].shape[0] // nseq))
               for s in range(nseg)]
        for pair in range(DF_HEADS // 2):
            cs = slice(pair * LANES, (pair + 1) * LANES)
            q = q_ref[qr, cs]
            ks = [refs[2 * s][krs[s], cs].astype(BF16) for s in range(nseg)]
            vs = [refs[2 * s + 1][krs[s], cs].astype(BF16) for s in range(nseg)]
            outs = []
            for hh in range(2):
                parts = []
                for half in range(2):
                    unit = 2 * hh + half
                    qm = jnp.where((lane >> 5) == unit, q, jnp.zeros_like(q))
                    scores = [_dot_nt(qm, k) for k in ks]
                    parts.append(_softmax_pv(scores, vs, hh))
                o = parts[0] - lam * parts[1]
                mine = (lane >> 6) == hh
                ms = jnp.sum(jnp.where(mine, o * o, 0.0), axis=-1, keepdims=True) * (1.0 / DF_V)
                outs.append(o * lax.rsqrt(ms + EPS))
            o = jnp.where(lane < DF_V, outs[0], outs[1]) * sub_ref[...] * (1.0 - lam_init)
            o_ref[qr, cs] = o.astype(o_ref.dtype)


def _diff_attention(q, k, v, lam_p, subln, lam_init, n_batch, seq, tq, row0, ctx=None, nseq=1):
    qt = seq // tq
    qb0, kb0 = row0 // (nseq * tq), 0
    in_specs = [pl.BlockSpec((nseq * tq, DFW), lambda b, j: (qb0 + b * qt + j, 0)),
                pl.BlockSpec((4, DF_DH), lambda b, j: (0, 0)),
                pl.BlockSpec((1, LANES), lambda b, j: (0, 0))]
    args = [q, lam_p, jnp.concatenate([subln, subln])[None]]
    segs = []
    if ctx is not None:
        segs.append((ctx, PAST, 0))
    segs.append(((k, v), seq, kb0))
    for (a_k, a_v), ln, off in segs:
        idx = lambda b, j, off=off: (off + b, 0)
        in_specs += [pl.BlockSpec((nseq * ln, DFW), idx), pl.BlockSpec((nseq * ln, DF_HEADS * DF_V), idx)]
        args += [a_k, a_v]
    return pl.pallas_call(
        functools.partial(_diff_attn_kernel, len(segs), nseq, lam_init),
        out_shape=jax.ShapeDtypeStruct((n_batch * seq, DF_HEADS * DF_V), BF16),
        grid=(n_batch // nseq, qt),
        in_specs=in_specs,
        out_specs=pl.BlockSpec((nseq * tq, DF_HEADS * DF_V), lambda b, j: (b * qt + j, 0)),
        compiler_params=_params(2, VMEM_LIMIT),
        name="diff_attention",
    )(*args)


def _hy_filter_kernel(feat_ref, w1_ref, b1_ref, w2_ref, b2_ref, fq_ref, w3_ref, dec_ref, o_ref):
    feat = feat_ref[...]
    fq = fq_ref[...]
    h = jnp.sin(fq * (_dot3(feat, w1_ref[...]) + b1_ref[...]))
    h = jnp.sin(fq * (_dot3(h, w2_ref[...]) + b2_ref[...]))
    window = jnp.exp(-feat[:, 0:1] * jnp.abs(dec_ref[...]))
    for j in range(4):
        cs = slice(j * HY_WIDTH, (j + 1) * HY_WIDTH)
        o_ref[:, cs] = _dot3(h, w3_ref[:, cs]) * window


def _hy_spectrum_kernel(L, cs_ref, hf_ref, hb_ref, o_ref):
    row = lax.broadcasted_iota(jnp.int32, (L, HY_WIDTH), 0)
    first = row == 0
    tf = _dot(cs_ref[...], hf_ref[...])
    tb = _dot(cs_ref[...], jnp.where(first, 0.0, hb_ref[...]))
    ka = tf[:L] + tb[:L]
    kb = jnp.where(first, tf[L:] + tb[L:], tf[L:] - tb[L:])
    wv = jnp.where(first, 1.0 / (2 * L), 2.0 / (2 * L))
    o_ref[0, 0] = ka * wv
    o_ref[0, 1] = jnp.where(first, 0.0, kb) * wv
    o_ref[0, 2] = jnp.where(first, kb, ka) * wv


HY_CH = 256


def _hy_conv_kernel(L, cs_ref, ct_ref, kf_ref, v_ref, x1_ref, x2_ref,
                    wv_ref, w1_ref, w2_ref, bias_ref, o_ref):
    row = lax.broadcasted_iota(jnp.int32, (L, HY_CH), 0)

    def short(x, w):
        prev = jnp.where(row >= 1, pltpu.roll(x, 1, 0), 0.0)
        nxt = jnp.where(row <= L - 2, pltpu.roll(x, L - 1, 0), 0.0)
        return w[0:1] * prev + w[1:2] * x + w[2:3] * nxt

    for j in range(v_ref.shape[0] // L):
        rs = slice(j * L, (j + 1) * L)
        for k in range(HY_WIDTH // HY_CH):
            ch = slice(k * HY_CH, (k + 1) * HY_CH)
            z = short(v_ref[rs, ch], wv_ref[:, ch])
            gates = (short(x1_ref[rs, ch], w1_ref[:, ch]), short(x2_ref[rs, ch], w2_ref[:, ch]))
            for n in range(2):
                ab = _dot(cs_ref[...], z)
                a, b = ab[:L], ab[L:]
                ka, kb1, ka2 = kf_ref[n, 0, :, ch], kf_ref[n, 1, :, ch], kf_ref[n, 2, :, ch]
                pq = jnp.concatenate([a * ka - b * kb1, a * kb1 + b * ka2], axis=0)
                conv = _dot(ct_ref[...], pq)
                z = gates[n] * (conv + bias_ref[n:n + 1, ch] * z)
            o_ref[rs, ch] = z.astype(o_ref.dtype)


def _hyena_spectrum(L, phy):
    conv_w, w1, b1, w2, b2, freq, w3, decay, bias = phy
    feat = jnp.asarray(_hyena_features(L))
    w1p = jnp.pad(w1, ((0, LANES - HY_EMB), (0, 0)))
    filt = pl.pallas_call(
        _hy_filter_kernel,
        out_shape=jax.ShapeDtypeStruct((L, 4 * HY_WIDTH), F32),
        grid=(1,),
        in_specs=[_const_spec((L, LANES)), _const_spec((LANES, HY_FH)), _const_spec((1, HY_FH)),
                  _const_spec((HY_FH, HY_FH)), _const_spec((1, HY_FH)), _const_spec((1, HY_FH)),
                  _const_spec((HY_FH, 4 * HY_WIDTH)), _const_spec((1, HY_WIDTH))],
        out_specs=pl.BlockSpec((L, 4 * HY_WIDTH), lambda i: (0, 0)),
        compiler_params=_params(1, VMEM_LIMIT),
        name="hyena_filter",
    )(feat, w1p, b1[None], w2, b2[None], freq[None], w3, decay[None])
    cs = jnp.asarray(_dft_tables(L)[0]).astype(BF16)
    return pl.pallas_call(
        functools.partial(_hy_spectrum_kernel, L),
        out_shape=jax.ShapeDtypeStruct((2, 3, L, HY_WIDTH), F32),
        grid=(2,),
        in_specs=[_const_spec((2 * L, L)),
                  pl.BlockSpec((L, HY_WIDTH), lambda n: (0, n)),
                  pl.BlockSpec((L, HY_WIDTH), lambda n: (0, 2 + n))],
        out_specs=pl.BlockSpec((1, 3, L, HY_WIDTH), lambda n: (n, 0, 0, 0)),
        compiler_params=_params(1, VMEM_LIMIT),
        name="hyena_spectrum",
    )(cs, filt, filt)


def _hyena_conv(hy_u, spec, phy, n_batch, L, seqs, row0):
    conv_w, bias = phy[0], phy[8]
    cs, ct = (jnp.asarray(t).astype(BF16) for t in _dft_tables(L))
    rows = seqs * L
    rb0 = row0 // rows
    col = lambda off: (lambda b: (0, off))
    tok = lambda off: (lambda b: (rb0 + b, off))
    blk = lambda idx: pl.BlockSpec((rows, HY_WIDTH), idx)
    return pl.pallas_call(
        functools.partial(_hy_conv_kernel, L),
        out_shape=jax.ShapeDtypeStruct((n_batch * L, HY_WIDTH), BF16),
        grid=(n_batch // seqs,),
        in_specs=[_const_spec((2 * L, L)), _const_spec((L, 2 * L)), _const_spec((2, 3, L, HY_WIDTH)),
                  blk(tok(0)), blk(tok(1)), blk(tok(2)),
                  pl.BlockSpec((3, HY_WIDTH), col(0)), pl.BlockSpec((3, HY_WIDTH), col(1)),
                  pl.BlockSpec((3, HY_WIDTH), col(2)), _const_spec((2, HY_WIDTH))],
        out_specs=blk(lambda b: (b, 0)),
        compiler_params=_params(1, VMEM_LIMIT),
        name="hyena_conv",
    )(cs, ct, spec, hy_u, hy_u, hy_u, conv_w, conv_w, conv_w, bias)


def _even_mixer(y, mods_l, g, pa, ps5, ctx_ckv, ctx_krope, ctx_state):
    w_in, w_out, q_norm, w_uq, kv_norm, w_ukv = pa
    a_re, a_im, log_step, b_re, b_im, c_re, c_im, d_skip, w_glu = ps5
    q, ckv, kr_unrot, kr_rot, kn, v, ug, (w_k, w_v) = _inproj_a(y, mods_l, g, w_in, q_norm, w_uq, kv_norm, w_ukv)

    ctx_flat = ctx_ckv.reshape(NB_S * PAST, MLA_KV_RANK)
    ctx_kn = _linear(ctx_flat, w_k, PAST, BF16)
    ctx_v = _linear(ctx_flat, w_v, PAST, BF16)
    ctx_kr = jnp.pad(ctx_krope.reshape(NB_S * PAST, MLA_ROPE),
                     ((0, 0), (KR_AT, LANES - KR_AT - MLA_ROPE))).astype(BF16)
    att_p = _mla_attention(q, kn, kr_rot, v, NB_P, L_P, L_P, 0, nseq=2)
    att_s = _mla_attention(q, kn, kr_rot, v, NB_S, L_S, TM, TOK_P, ctx=(ctx_kn, ctx_kr, ctx_v))

    prep = _s5_prep(a_re, a_im, log_step, b_re, b_im, c_re, c_im)
    h0 = ctx_state.transpose(3, 1, 0, 2, 4).reshape(S5_GROUPS, 2, NB_S, 2 * S5_N)
    h0 = jnp.pad(h0, ((0, 0), (0, 0), (0, 8 - NB_S), (0, 0)))
    s5y, fin = _s5_core(ug, prep, h0, d_skip)

    mixer = ((att_p, att_s), s5y, w_out, w_glu)
    new_ckv = ckv.reshape(NB_P, L_P, MLA_KV_RANK)
    new_krope = kr_unrot.reshape(NB_P, L_P, MLA_ROPE)
    new_state = fin.reshape(S5_GROUPS, 2, NB_P, 2, S5_N).transpose(2, 1, 3, 0, 4)
    return mixer, new_ckv, new_krope, new_state


def _odd_mixer(y, mods_l, g, pb, phy, ctx_k, ctx_v, lam_init):
    w_in, w_out, lam_p, subln = pb
    hy_u, q, (k_p, k_s), (v_p, v_s), v_cache = _inproj_b(y, mods_l, g, w_in)
    hy_p = _hyena_conv(hy_u, _hyena_spectrum(L_P, phy), phy, NB_P, L_P, 2, 0)
    hy_s = _hyena_conv(hy_u, _hyena_spectrum(L_S, phy), phy, NB_S, L_S, 1, TOK_P)
    ctx = (ctx_k.reshape(NB_S * PAST, DFW), ctx_v.reshape(NB_S * PAST, DF_HEADS * DF_V))
    att_p = _diff_attention(q, k_p, v_p, lam_p, subln, lam_init, NB_P, L_P, L_P, 0, nseq=2)
    att_s = _diff_attention(q, k_s, v_s, lam_p, subln, lam_init, NB_S, L_S, TM // 2, TOK_P, ctx=ctx)
    mixer = ((hy_p, hy_s), (att_p, att_s), w_out, None)
    new_k = k_p.reshape(NB_P, L_P, DF_HEADS, 2, DF_DH)
    new_v = v_cache.reshape(NB_P, L_P, DF_HEADS, DF_V)
    return mixer, new_k, new_v


def kernel(x_prompt, x_sample, c, c_ctx, cache_mla_ckv, cache_mla_krope, state_s5, cache_diff_k, cache_diff_v, ada_w, ada_b, norm_g, ff_w_in, ff_w_out, w_in_a, w_out_a, mla_q_norm, mla_w_uq, mla_kv_norm, mla_w_ukv, s5_a_re, s5_a_im, s5_log_step, s5_b_re, s5_b_im, s5_c_re, s5_c_im, s5_d, s5_w_glu, w_in_b, w_out_b, hy_conv, hy_w1, hy_b1, hy_w2, hy_b2, hy_freq, hy_w3, hy_decay, hy_bias, df_lambda, df_subln, final_norm):
    depth = ada_w.shape[0]
    y = (x_prompt.reshape(TOK_P, D), x_sample.reshape(TOK_S, D))
    mods = _adaln(jnp.concatenate([c_ctx[None], c], axis=0), ada_w, ada_b)
    new_ckv, new_krope, new_s5, new_dk, new_dv = [], [], [], [], []
    for l in range(depth):
        y = _half_ffn(y, (mods, l), norm_g[l, 0], ff_w_in, ff_w_out, l, 0)
        if l % 2 == 0:
            e = l // 2
            pa = (w_in_a[e], w_out_a[e], mla_q_norm[e], mla_w_uq[e], mla_kv_norm[e], mla_w_ukv[e])
            ps5 = (s5_a_re[e], s5_a_im[e], s5_log_step[e], s5_b_re[e], s5_b_im[e],
                   s5_c_re[e], s5_c_im[e], s5_d[e], s5_w_glu[e])
            mixer, ckv, krope, st = _even_mixer(y, (mods, l), norm_g[l, 1], pa, ps5, cache_mla_ckv[:, e],
                                            cache_mla_krope[:, e], state_s5[:, e])
            new_ckv.append(ckv)
            new_krope.append(krope)
            new_s5.append(st)
        else:
            o = l // 2
            lam_init = 0.8 - 0.6 * math.exp(-0.3 * l)
            pb = (w_in_b[o], w_out_b[o], df_lambda[o], df_subln[o])
            phy = (hy_conv[o], hy_w1[o], hy_b1[o], hy_w2[o], hy_b2[o], hy_freq[o],
                   hy_w3[o], hy_decay[o], hy_bias[o])
            mixer, dk, dv = _odd_mixer(y, (mods, l), norm_g[l, 1], pb, phy, cache_diff_k[:, o],
                                   cache_diff_v[:, o], lam_init)
            new_dk.append(dk)
            new_dv.append(dv)
        last = l == depth - 1
        y = _half_ffn(y, (mods, l), norm_g[l, 2], ff_w_in, ff_w_out, l, 1,
                      final_g=final_norm if last else None, mixer=mixer)
    y_prompt = y[0].reshape(NB_P, L_P, D)
    y_sample = y[1].reshape(NB_S, L_S, D)
    return (y_prompt, y_sample, jnp.stack(new_ckv, axis=1), jnp.stack(new_krope, axis=1),
            jnp.stack(new_s5, axis=1), jnp.stack(new_dk, axis=1), jnp.stack(new_dv, axis=1))
```

```python
import functools
import math

import numpy as np
import jax
import jax.numpy as jnp
from jax import lax
from jax.experimental import pallas as pl
from jax.experimental.pallas import tpu as pltpu

F32 = jnp.float32
BF16 = jnp.bfloat16

D = 1024
NB_P, L_P = 16, 256
NB_S, L_S = 2, 1024
PAST = 256
GRID_W = 64
N_MOD = 9
FF = 2816
EPS = 1e-6
ROPE_BASE = 10000.0

MLA_HEADS, MLA_NOPE, MLA_ROPE, MLA_V = 8, 64, 32, 64
MLA_Q_RANK, MLA_KV_RANK = 384, 256
S5_WIDTH, S5_GROUP, S5_N = 512, 16, 64
S5_GROUPS = S5_WIDTH // S5_GROUP
HY_WIDTH, HY_BANDS, HY_FH = 512, 16, 64
HY_EMB = 2 * HY_BANDS + 1
DF_HEADS, DF_DH = 8, 32
DF_V = 2 * DF_DH

TOK_P = NB_P * L_P
TOK_S = NB_S * L_S
TOK = TOK_P + TOK_S
TM = 512
NT = TOK // TM
NT_P = TOK_P // TM
TILES_PER_SAMPLE = L_S // TM

LANES = 128
S5_T = 16
S5_CW = S5_T * S5_GROUP
CH_P = L_P // S5_T
CH_S = L_S // S5_T
S5_ROWS = NB_P * CH_P + NB_S * CH_S
S5_ROWS_P = NB_P * CH_P

VMEM_LIMIT = 56 * 1024 * 1024


def _params(n_grid, vmem=None):
    return pltpu.CompilerParams(dimension_semantics=("arbitrary",) * n_grid,
                                vmem_limit_bytes=vmem)


def _const_spec(shape):
    nd = len(shape)
    return pl.BlockSpec(shape, lambda *_: (0,) * nd, pipeline_mode=pl.Buffered(1))


def _mod_index(i):
    return jnp.where(i < NT_P, 0, 1 + (i - NT_P) // TILES_PER_SAMPLE)


def _mod_spec(layer):
    return pl.BlockSpec((1, 8, N_MOD * D), lambda *_: (layer, 0, 0), pipeline_mode=pl.Buffered(1))


def _mod_rows(mod_ref):
    row = mod_ref[0, pl.ds(_mod_index(pl.program_id(0)), 1), :]
    return [row[:, k * D:(k + 1) * D] for k in range(N_MOD)]


def _pos_index(i):
    return jnp.where(i < NT_P, 0, 1 + (i - NT_P) % TILES_PER_SAMPLE)


def _row(i):
    return (i, 0)


def _row_p(i):
    return (jnp.minimum(i, NT_P - 1), 0)


def _row_s(i):
    return (jnp.maximum(i - NT_P, 0), 0)


def _tok_specs(x, width):
    if isinstance(x, tuple):
        return [pl.BlockSpec((TM, width), _row_p), pl.BlockSpec((TM, width), _row_s)], list(x)
    return [pl.BlockSpec((TM, width), _row)], [x]


def _tok_read(refs, split):
    if split:
        return jnp.where(pl.program_id(0) < NT_P, refs[0][...], refs[1][...]), refs[2:]
    return refs[0][...], refs[1:]


def _tok_write(p_ref, s_ref, value):
    i = pl.program_id(0)

    @pl.when(i < NT_P)
    def _():
        p_ref[...] = value

    @pl.when(i >= NT_P)
    def _():
        s_ref[...] = value.astype(s_ref.dtype)


def _split_out(width, sample_dtype=F32):
    shapes = [jax.ShapeDtypeStruct((TOK_P, width), F32), jax.ShapeDtypeStruct((TOK_S, width), sample_dtype)]
    specs = [pl.BlockSpec((TM, width), _row_p), pl.BlockSpec((TM, width), _row_s)]
    return shapes, specs


def _dot(a, b):
    return jnp.dot(a.astype(BF16), b.astype(BF16), preferred_element_type=F32)


def _dot_nt(a, b):
    return lax.dot_general(a, b, (((1,), (1,)), ((), ())), preferred_element_type=F32)


def _split(x):
    hi = x.astype(BF16)
    lo = (x - hi.astype(F32)).astype(BF16)
    return hi, lo


def _dot3(a, b):
    ah, al = _split(a)
    bh, bl = _split(b)
    d = functools.partial(jnp.dot, preferred_element_type=F32)
    return d(ah, bh) + d(ah, bl) + d(al, bh)


def _rmsnorm(x, g):
    return x * lax.rsqrt(jnp.mean(x * x, axis=-1, keepdims=True) + EPS) * g


def _modulate(y, g, shift, scale):
    return _rmsnorm(y, g) * (1.0 + scale) + shift


def _pair_swap(x):
    n = x.shape[-1]
    lane = lax.broadcasted_iota(jnp.int32, x.shape, x.ndim - 1)
    return jnp.where((lane & 1) == 0, pltpu.roll(x, n - 1, x.ndim - 1), pltpu.roll(x, 1, x.ndim - 1))


def _rope(x, cos, sin_signed):
    return x * cos + _pair_swap(x) * sin_signed


def _rope_angles():
    n_freq = MLA_ROPE // 4
    inv = 1.0 / (ROPE_BASE ** (np.arange(n_freq, dtype=np.float64) / n_freq))
    pos = np.arange(L_S)
    row = (pos // GRID_W).astype(np.float64)
    col = (pos % GRID_W).astype(np.float64)
    ang = np.concatenate([row[:, None] * inv, col[:, None] * inv], axis=-1)
    return np.cos(ang), np.sin(ang)


@functools.lru_cache(maxsize=None)
def _rope_tables(width, starts):
    cos, sin = _rope_angles()
    c = np.ones((TM + L_S, width), np.float32)
    s = np.zeros((TM + L_S, width), np.float32)
    sign = np.where(np.arange(MLA_ROPE) % 2 == 0, -1.0, 1.0)
    unit_c = np.repeat(cos, 2, axis=1)
    unit_s = np.repeat(sin, 2, axis=1) * sign
    for st in starts:
        c[TM:, st:st + MLA_ROPE] = unit_c
        s[TM:, st:st + MLA_ROPE] = unit_s
    return c, s


@functools.lru_cache(maxsize=None)
def _dft_tables(L):
    f = np.arange(L)[:, None]
    s = np.arange(L)[None, :]
    ang = np.pi * ((f * s) % (2 * L)).astype(np.float64) / L
    cs = np.concatenate([np.cos(ang), np.sin(ang)], axis=0)
    cs[L, :] = np.where(np.arange(L) % 2 == 0, 1.0, -1.0)
    cs = cs.astype(np.float32)
    return cs, np.ascontiguousarray(cs.T)


@functools.lru_cache(maxsize=None)
def _hyena_features(L):
    t = np.arange(L, dtype=np.float64) / L
    bands = np.arange(1, HY_BANDS + 1, dtype=np.float64)
    ang = 2.0 * math.pi * t[:, None] * bands
    feat = np.zeros((L, LANES), np.float32)
    feat[:, 0] = t
    feat[:, 1:1 + HY_BANDS] = np.cos(ang)
    feat[:, 1 + HY_BANDS:HY_EMB] = np.sin(ang)
    return feat


def _adaln_kernel(c_ref, w_ref, b_ref, o_ref):
    s = jax.nn.silu(c_ref[...])
    s_hi = s.astype(BF16).astype(F32)
    stacked = jnp.concatenate([s_hi, s - s_hi], axis=0).astype(BF16)
    wh, wl = _split(w_ref[0])
    both = jnp.dot(stacked, wh, preferred_element_type=F32)
    rows = c_ref.shape[0]
    o_ref[0] = both[:rows] + both[rows:] + jnp.dot(stacked, wl, preferred_element_type=F32)[:rows] + b_ref[0]


def _adaln(cvecs, ada_w, ada_b):
    depth = ada_w.shape[0]
    n_vec = cvecs.shape[0]
    tn = N_MOD * D // 4
    out = pl.pallas_call(
        _adaln_kernel,
        out_shape=jax.ShapeDtypeStruct((depth, 8, N_MOD * D), F32),
        grid=(depth, N_MOD * D // tn),
        in_specs=[pl.BlockSpec((8, D), lambda l, j: (0, 0)),
                  pl.BlockSpec((1, D, tn), lambda l, j: (l, 0, j)),
                  pl.BlockSpec((1, 1, tn), lambda l, j: (l, 0, j))],
        out_specs=pl.BlockSpec((1, 8, tn), lambda l, j: (l, 0, j)),
        compiler_params=_params(2, VMEM_LIMIT),
        name="adaln",
    )(jnp.pad(cvecs, ((0, 8 - n_vec), (0, 0))), ada_w, ada_b[:, None, :])
    return out


FF_PIECE = 256
FF_LOADS = FF // FF_PIECE


def _ffn_kernel(base, final, split_in, mixer, layer, which, *refs):
    y, refs = _tok_read(refs, split_in)
    if mixer is not None:
        a1, refs = _tok_read(refs, mixer[0])
        if mixer[2]:
            a2_chunks, refs = refs[0], refs[1:]
        else:
            a2, refs = _tok_read(refs, mixer[1])
        wmix_ref, wg_ref = refs[:2]
        refs = refs[2:]
    mod_ref, g_ref, win_hbm, wout_hbm, fg_ref = refs[:5]
    n_out = 2 if final else 1
    outs = refs[5:5 + n_out]
    win_ref, wout_ref, stage_g, stage_u, stage_o, sems = refs[5 + n_out:11 + n_out]
    mod = _mod_rows(mod_ref)
    if mixer is not None:
        if mixer[2]:
            a2_scr = refs[11 + n_out]
            _chunks_to_tokens(a2_chunks, a2_scr)
            a2 = jax.nn.gelu(jnp.concatenate([a2_scr[o] for o in range(a2_scr.shape[0])], axis=1))
            a2 = a2 * jax.nn.sigmoid(_dot(a2, wg_ref[...]))
        k1 = wmix_ref.shape[0] // 2
        y = y + mod[5] * (_dot(a1, wmix_ref[:k1]) + _dot(a2, wmix_ref[k1:]))
    h = _modulate(y, g_ref[...], mod[base], mod[base + 1]).astype(BF16)

    def hidden(lo, width):
        gate = jnp.dot(h, win_ref[:, lo:lo + width], preferred_element_type=F32)
        up = jnp.dot(h, win_ref[:, FF + lo:FF + lo + width], preferred_element_type=F32)
        a = (jax.nn.silu(gate) * up).astype(BF16)
        return jnp.dot(a, wout_ref[lo:lo + width, :], preferred_element_type=F32)

    def finish(acc):
        out = y + 0.5 * mod[base + 2] * acc
        if final:
            _tok_write(outs[0], outs[1], _rmsnorm(out, fg_ref[...]))
        else:
            outs[0][...] = out

    @pl.when(pl.program_id(0) == 0)
    def _():
        def copies(c, slot):
            cols = pl.ds(c * FF_PIECE, FF_PIECE)
            return (pltpu.make_async_copy(win_hbm.at[layer, which, :, cols], stage_g.at[slot], sems.at[0, slot]),
                    pltpu.make_async_copy(win_hbm.at[layer, which, :, pl.ds(FF + c * FF_PIECE, FF_PIECE)],
                                          stage_u.at[slot], sems.at[1, slot]),
                    pltpu.make_async_copy(wout_hbm.at[layer, which, cols, :], stage_o.at[slot], sems.at[2, slot]))

        for cp in copies(0, 0):
            cp.start()
        acc = jnp.zeros(y.shape, F32)
        for c in range(FF_LOADS):
            slot = c % 2
            lo = c * FF_PIECE
            if c + 1 < FF_LOADS:
                for cp in copies(c + 1, 1 - slot):
                    cp.start()
            for cp in copies(c, slot):
                cp.wait()
            win_ref[:, lo:lo + FF_PIECE] = stage_g[slot].astype(BF16)
            win_ref[:, FF + lo:FF + lo + FF_PIECE] = stage_u[slot].astype(BF16)
            wout_ref[lo:lo + FF_PIECE, :] = stage_o[slot].astype(BF16)
            acc = acc + hidden(lo, FF_PIECE)
        finish(acc)

    @pl.when(pl.program_id(0) > 0)
    def _():
        finish(hidden(0, FF))


def _half_ffn(y, mods_l, g, ff_w_in, ff_w_out, layer, which, final_g=None, mixer=None):
    final = final_g is not None
    fg = final_g if final else g
    y_specs, y_args = _tok_specs(y, D)
    mix_flags = None
    extra_scratch = []
    if mixer is not None:
        a1, a2, w_out, w_glu = mixer
        k1 = w_out.shape[0] // 2
        wg = w_glu if w_glu is not None else jnp.zeros((8, LANES), F32)
        s1, a1_args = _tok_specs(a1, k1)
        if w_glu is not None:
            s2 = [pl.BlockSpec((S5_GROUPS, TM // S5_T, S5_CW), lambda i: (0, i, 0))]
            a2_args = [a2]
            extra_scratch = [pltpu.VMEM((k1 // LANES, TM, LANES), F32)]
        else:
            s2, a2_args = _tok_specs(a2, k1)
        y_specs = y_specs + s1 + s2 + [_const_spec(w_out.shape), _const_spec(wg.shape)]
        y_args = y_args + a1_args + a2_args + [w_out, wg]
        mix_flags = (isinstance(a1, tuple), isinstance(a2, tuple), w_glu is not None)
    if final:
        out_shape, out_specs = _split_out(D)
    else:
        out_shape, out_specs = jax.ShapeDtypeStruct((TOK, D), F32), pl.BlockSpec((TM, D), _row)
    return pl.pallas_call(
        functools.partial(_ffn_kernel, 6 * which, final, isinstance(y, tuple), mix_flags, layer, which),
        out_shape=out_shape,
        grid=(NT,),
        in_specs=y_specs + [_mod_spec(mods_l[1]),
                            _const_spec((1, D)),
                            pl.BlockSpec(memory_space=pl.ANY),
                            pl.BlockSpec(memory_space=pl.ANY),
                            _const_spec((1, D))],
        out_specs=out_specs,
        scratch_shapes=[pltpu.VMEM((D, 2 * FF), BF16), pltpu.VMEM((FF, D), BF16),
                        pltpu.VMEM((2, D, FF_PIECE), F32), pltpu.VMEM((2, D, FF_PIECE), F32),
                        pltpu.VMEM((2, FF_PIECE, D), F32), pltpu.SemaphoreType.DMA((3, 2))] + extra_scratch,
        compiler_params=_params(1, VMEM_LIMIT),
        name="half_ffn",
    )(*y_args, mods_l[0], g[None], ff_w_in, ff_w_out, fg[None])


def _linear_kernel(x_ref, w_ref, o_ref):
    o_ref[...] = _dot(x_ref[...], w_ref[...]).astype(o_ref.dtype)


def _linear(x, w, tm, out_dtype):
    m, k = x.shape
    n = w.shape[1]
    return pl.pallas_call(
        _linear_kernel,
        out_shape=jax.ShapeDtypeStruct((m, n), out_dtype),
        grid=(m // tm,),
        in_specs=[pl.BlockSpec((tm, k), lambda i: (i, 0)), _const_spec((k, n))],
        out_specs=pl.BlockSpec((tm, n), lambda i: (i, 0)),
        compiler_params=_params(1),
        name="linear",
    )(x, w.astype(BF16))


LOG2E = math.log2(math.e)
MLA_SCALE = (MLA_NOPE + MLA_ROPE) ** -0.5 * LOG2E
QW = MLA_HEADS * LANES
KR_AT = MLA_NOPE
IN_A_PAD = MLA_Q_RANK + MLA_KV_RANK + S5_WIDTH + LANES


def _inproj_a_kernel(y_ref, mod_ref, g_ref, win_ref, qn_ref, wuq_ref, kvn_ref, wk_ref, wv_ref,
                     cq_ref, sq_ref, ck_ref, sk_ref,
                     q_ref, ckv_ref, kru_ref, krr_ref, kn_ref, v_ref, ug_ref, u_scr):
    mod = _mod_rows(mod_ref)
    h = _modulate(y_ref[...], g_ref[...], mod[3], mod[4]).astype(BF16)
    p = jnp.dot(h, win_ref[...], preferred_element_type=F32)
    o1 = MLA_Q_RANK
    o2 = o1 + MLA_KV_RANK
    o3 = o2 + S5_WIDTH
    q = _dot(_rmsnorm(p[:, :o1], qn_ref[...]), wuq_ref[...])
    q_ref[...] = (_rope(q, cq_ref[...], sq_ref[...]) * MLA_SCALE).astype(BF16)
    ckv = _rmsnorm(p[:, o1:o2], kvn_ref[...])
    ckv_b = ckv.astype(BF16)
    kn_ref[...] = jnp.dot(ckv_b, wk_ref[...], preferred_element_type=F32).astype(BF16)
    v_ref[...] = jnp.dot(ckv_b, wv_ref[...], preferred_element_type=F32).astype(BF16)
    for octet in range(S5_WIDTH // LANES):
        u_scr[octet] = p[:, o2 + octet * LANES:o2 + (octet + 1) * LANES]
    _tokens_to_chunks(u_scr, ug_ref)
    krp = p[:, o3:]
    krr_ref[...] = _rope(krp, ck_ref[...], sk_ref[...]).astype(BF16)

    @pl.when(pl.program_id(0) < NT_P)
    def _():
        ckv_ref[...] = ckv
        kru_ref[...] = krp[:, KR_AT:KR_AT + MLA_ROPE]


def _inproj_a(y, mods_l, g, w_in, q_norm, w_uq, kv_norm, w_ukv):
    o1 = MLA_Q_RANK
    o2 = o1 + MLA_KV_RANK
    o3 = o2 + MLA_ROPE
    kr_cols = jnp.pad(w_in[:, o2:o3], ((0, 0), (KR_AT, LANES - KR_AT - MLA_ROPE)))
    w_ext = jnp.concatenate([w_in[:, :o2], w_in[:, o3:], kr_cols], axis=1).astype(BF16)
    dq = MLA_NOPE + MLA_ROPE
    w_uq_pad = jnp.pad(w_uq.reshape(MLA_Q_RANK, MLA_HEADS, dq),
                       ((0, 0), (0, 0), (0, LANES - dq))).reshape(MLA_Q_RANK, QW).astype(BF16)
    w_kv = w_ukv.reshape(MLA_KV_RANK, MLA_HEADS, MLA_NOPE + MLA_V)
    w_k = jnp.pad(w_kv[:, :, :MLA_NOPE], ((0, 0), (0, 0), (0, LANES - MLA_NOPE))).reshape(MLA_KV_RANK, QW)
    w_v = w_kv[:, :, MLA_NOPE:].reshape(MLA_KV_RANK, MLA_HEADS * MLA_V)
    w_k, w_v = w_k.astype(BF16), w_v.astype(BF16)
    cq, sq = _rope_tables(QW, tuple(h * LANES + MLA_NOPE for h in range(MLA_HEADS)))
    ck, sk = _rope_tables(LANES, (KR_AT,))
    row = _row
    pos = lambda i: (_pos_index(i), 0)
    widths = (QW, MLA_KV_RANK, MLA_ROPE, LANES, QW, MLA_HEADS * MLA_V)
    prompt_only = (1, 2)
    mxu_only = (0, 3, 4, 5)
    outs = pl.pallas_call(
        _inproj_a_kernel,
        out_shape=[jax.ShapeDtypeStruct((TOK_P if k in prompt_only else TOK, w), BF16 if k in mxu_only else F32)
                   for k, w in enumerate(widths)]
                  + [jax.ShapeDtypeStruct((S5_GROUPS, S5_ROWS, S5_CW), F32)],
        grid=(NT,),
        in_specs=[pl.BlockSpec((TM, D), row),
                  _mod_spec(mods_l[1]),
                  _const_spec((1, D)),
                  _const_spec((D, IN_A_PAD)),
                  _const_spec((1, MLA_Q_RANK)),
                  _const_spec((MLA_Q_RANK, QW)),
                  _const_spec((1, MLA_KV_RANK)),
                  _const_spec((MLA_KV_RANK, QW)),
                  _const_spec((MLA_KV_RANK, MLA_HEADS * MLA_V)),
                  pl.BlockSpec((TM, QW), pos), pl.BlockSpec((TM, QW), pos),
                  pl.BlockSpec((TM, LANES), pos), pl.BlockSpec((TM, LANES), pos)],
        out_specs=[pl.BlockSpec((TM, w), _row_p if k in prompt_only else row)
                   for k, w in enumerate(widths)]
                  + [pl.BlockSpec((S5_GROUPS, TM // S5_T, S5_CW), lambda i: (0, i, 0))],
        scratch_shapes=[pltpu.VMEM((S5_WIDTH // LANES, TM, LANES), F32)],
        compiler_params=_params(1, VMEM_LIMIT),
        name="inproj_even",
    )(y, mods_l[0], g[None], w_ext, q_norm[None], w_uq_pad, kv_norm[None], w_k, w_v,
      jnp.asarray(cq), jnp.asarray(sq), jnp.asarray(ck), jnp.asarray(sk))
    q, ckv, kr_unrot, kr_rot, kn, v, ug = outs
    return q, ckv, kr_unrot, kr_rot, kn, v, ug, (w_k, w_v)


def _softmax_pv(scores, vals, half):
    m = functools.reduce(jnp.maximum, [jnp.max(s, axis=-1, keepdims=True) for s in scores])
    pv = None
    for s, v in zip(scores, vals):
        own_k = (lax.broadcasted_iota(jnp.int32, v.shape, 1) >> 6) == half
        part = jnp.dot(jnp.exp2(s - m).astype(BF16), jnp.where(own_k, v, jnp.ones_like(v)),
                       preferred_element_type=F32)
        pv = part if pv is None else pv + part
    own = (lax.broadcasted_iota(jnp.int32, pv.shape, 1) >> 6) == half
    denom = jnp.max(jnp.where(own, 0.0, pv), axis=-1, keepdims=True)
    return pv * (1.0 / denom)


def _mla_attn_kernel(nseg, nseq, q_ref, *refs):
    o_ref = refs[-1]
    tq = q_ref.shape[0] // nseq
    lane = lax.broadcasted_iota(jnp.int32, (tq, LANES), 1)
    for j in range(nseq):
        qr = slice(j * tq, (j + 1) * tq)
        krs = [slice(j * (refs[3 * s].shape[0] // nseq), (j + 1) * (refs[3 * s].shape[0] // nseq))
               for s in range(nseg)]
        for pair in range(MLA_HEADS // 2):
            outs = []
            for hh in range(2):
                h = 2 * pair + hh
                hs = slice(h * LANES, (h + 1) * LANES)
                qh = q_ref[qr, hs]
                scores = []
                for s in range(nseg):
                    kn_ref, kr_ref = refs[3 * s], refs[3 * s + 1]
                    kh = (kn_ref[krs[s], hs] + kr_ref[krs[s], :]).astype(BF16)
                    scores.append(_dot_nt(qh, kh))
                vals = [refs[3 * s + 2][krs[s], pair * LANES:(pair + 1) * LANES] for s in range(nseg)]
                outs.append(_softmax_pv(scores, vals, hh))
            o_ref[qr, pair * LANES:(pair + 1) * LANES] = jnp.where(lane < MLA_V, outs[0], outs[1]).astype(o_ref.dtype)


def _mla_attention(q, kn, kr, v, n_batch, seq, tq, row0, ctx=None, nseq=1):
    qt = seq // tq
    qb0, kb0 = row0 // (nseq * tq), row0 // (nseq * seq)
    in_specs = [pl.BlockSpec((nseq * tq, QW), lambda b, j: (qb0 + b * qt + j, 0))]
    args = [q]
    segs = []
    if ctx is not None:
        segs.append((ctx, PAST, 0))
    segs.append(((kn, kr, v), seq, kb0))
    for (a_kn, a_kr, a_v), ln, off in segs:
        idx = lambda b, j, off=off: (off + b, 0)
        in_specs += [pl.BlockSpec((nseq * ln, QW), idx), pl.BlockSpec((nseq * ln, LANES), idx),
                     pl.BlockSpec((nseq * ln, MLA_HEADS * MLA_V), idx)]
        args += [a_kn, a_kr, a_v]
    return pl.pallas_call(
        functools.partial(_mla_attn_kernel, len(segs), nseq),
        out_shape=jax.ShapeDtypeStruct((n_batch * seq, MLA_HEADS * MLA_V), BF16),
        grid=(n_batch // nseq, qt),
        in_specs=in_specs,
        out_specs=pl.BlockSpec((nseq * tq, MLA_HEADS * MLA_V), lambda b, j: (b * qt + j, 0)),
        compiler_params=_params(2, VMEM_LIMIT),
        name="mla_attention",
    )(*args)


def _cpow(ar, ai, e, nbits):
    rr = jnp.ones_like(ar)
    ri = jnp.zeros_like(ar)
    br, bi = ar, ai
    for k in range(nbits):
        bit = ((e >> k) & 1) == 1
        nr = rr * br - ri * bi
        ni = rr * bi + ri * br
        rr = jnp.where(bit, nr, rr)
        ri = jnp.where(bit, ni, ri)
        if k + 1 < nbits:
            br, bi = br * br - bi * bi, 2.0 * br * bi
    return rr, ri


def _s5_abar_kernel(lr_ref, li_ref, ls_ref, o_ref):
    step = jnp.exp(ls_ref[...])
    lr = jnp.minimum(lr_ref[...], -1e-4)
    li = li_ref[...]
    mag = jnp.exp(lr * step)
    ar = mag * jnp.cos(li * step)
    ai = mag * jnp.sin(li * step)
    den = lr * lr + li * li
    o_ref[0] = ar
    o_ref[1] = ai
    o_ref[2] = ((ar - 1.0) * lr + ai * li) / den
    o_ref[3] = (ai * lr - (ar - 1.0) * li) / den


S5_PREP_GROUPS = 4


def _s5_prep_kernel(*refs):
    for gi in range(S5_PREP_GROUPS):
        _s5_prep_group(gi, *refs)


def _s5_prep_group(gi, arow_ref, acol_ref, btr_ref, bti_ref, ctr_ref, cti_ref,
                   wi_ref, ws_ref, wo_ref, ap_ref):
    n2 = 2 * S5_N
    blk_o = lax.broadcasted_iota(jnp.int32, (S5_N, S5_CW), 1) >> 4
    lane_k = lax.broadcasted_iota(jnp.int32, (S5_GROUP, S5_CW), 1)
    row_k = lax.broadcasted_iota(jnp.int32, (S5_GROUP, S5_CW), 0)
    lane_b = lax.broadcasted_iota(jnp.int32, (S5_GROUP, n2), 1)
    lane_a = lax.broadcasted_iota(jnp.int32, (1, n2), 1)
    rep = ((lane_k & (S5_GROUP - 1)) == row_k).astype(BF16)

    def tile16(x):
        hi = x.astype(BF16)
        r1 = x - hi.astype(F32)
        mid = r1.astype(BF16)
        lo = (r1 - mid.astype(F32)).astype(BF16)
        d = lambda a: lax.dot_general(a, rep, (((0,), (0,)), ((), ())), preferred_element_type=F32)
        return d(hi) + d(mid) + d(lo)

    intra = [None] * S5_T
    for d in range(2):
        ar, ai, fr, fi = (arow_ref[d, gi, k:k + 1, :] for k in range(4))
        btr, bti = btr_ref[d, gi], bti_ref[d, gi]
        bbr = fr * btr - fi * bti
        bbi = fr * bti + fi * btr
        pws = [(jnp.ones_like(ar), jnp.zeros_like(ar))]
        for _ in range(S5_T):
            pr, pi = pws[-1]
            pws.append((pr * ar - pi * ai, pr * ai + pi * ar))
        for s in range(S5_T):
            pr, pi = pws[S5_T - 1 - s] if d == 0 else pws[s]
            ws_ref[d, gi, s * S5_GROUP:(s + 1) * S5_GROUP, :] = jnp.where(
                lane_b < S5_N, pr * bbr - pi * bbi, pr * bbi + pi * bbr).astype(BF16)

        acol = acol_ref[d, gi]
        arc = jnp.broadcast_to(acol[:, 0:1], (S5_N, S5_CW))
        aic = jnp.broadcast_to(acol[:, 1:2], (S5_N, S5_CW))
        ctr, cti = tile16(ctr_ref[d, gi]), tile16(cti_ref[d, gi])
        e_lag = blk_o if d == 0 else (S5_T - 1 - blk_o)
        pqr, pqi = _cpow(arc, aic, e_lag, 4)
        qr = pqr * ctr - pqi * cti
        qi = pqr * cti + pqi * ctr
        wo_ref[d, gi] = jnp.concatenate([qr * arc - qi * aic, -(qr * aic + qi * arc)], axis=0).astype(BF16)
        q_stack = jnp.concatenate([qr, qi], axis=0)
        bb_mix = jnp.where(lane_b < S5_N, bbr, -bbi)
        kt = _dot3(bb_mix, q_stack)
        for s in range(S5_T):
            if d == 0:
                blk = jnp.where(lane_k >= S5_GROUP * s, pltpu.roll(kt, S5_GROUP * s, 1), 0.0)
            else:
                blk = jnp.where(lane_k < S5_GROUP * (s + 1),
                                pltpu.roll(kt, (S5_GROUP * (s + 1)) % S5_CW, 1), 0.0)
            intra[s] = blk if intra[s] is None else intra[s] + blk

        pr1, pi1 = pws[S5_T]
        for k in range(6):
            ap_ref[d, gi, k:k + 1, :] = pr1
            ap_ref[d, gi, 8 + k:9 + k, :] = jnp.where(lane_a < S5_N, -pi1, pi1)
            pr1, pi1 = pr1 * pr1 - pi1 * pi1, 2.0 * pr1 * pi1
        ap_ref[d, gi, 6:8, :] = jnp.zeros((2, n2), F32)
        ap_ref[d, gi, 14:16, :] = jnp.zeros((2, n2), F32)
    for s in range(S5_T):
        wi_ref[gi, s * S5_GROUP:(s + 1) * S5_GROUP, :] = intra[s].astype(BF16)


def _s5_prep(a_re, a_im, log_step, b_re, b_im, c_re, c_im):
    g, n, n2 = S5_GROUPS, S5_N, 2 * S5_N
    abar = pl.pallas_call(
        _s5_abar_kernel,
        out_shape=jax.ShapeDtypeStruct((4, 2 * g, n), F32),
        grid=(1,),
        in_specs=[_const_spec((2 * g, n)), _const_spec((2 * g, n)), _const_spec((2 * g, 1))],
        out_specs=pl.BlockSpec((4, 2 * g, n), lambda i: (0, 0, 0)),
        compiler_params=_params(1),
        name="s5_abar",
    )(a_re.reshape(2 * g, n), a_im.reshape(2 * g, n), log_step.reshape(2 * g, 1))
    abar = jnp.concatenate([abar, abar], axis=-1).reshape(4, 2, g, n2)
    arow = abar.transpose(1, 2, 0, 3)
    acol = abar[:2, :, :, :n].transpose(1, 2, 3, 0)
    bt = lambda b: jnp.concatenate([jnp.swapaxes(b, 2, 3)] * 2, axis=-1)
    spec4 = lambda r, c: pl.BlockSpec((2, S5_PREP_GROUPS, r, c), lambda i: (0, i, 0, 0))
    return pl.pallas_call(
        _s5_prep_kernel,
        out_shape=[jax.ShapeDtypeStruct((g, S5_CW, S5_CW), BF16),
                   jax.ShapeDtypeStruct((2, g, S5_CW, n2), BF16),
                   jax.ShapeDtypeStruct((2, g, n2, S5_CW), BF16),
                   jax.ShapeDtypeStruct((2, g, 16, n2), F32)],
        grid=(g // S5_PREP_GROUPS,),
        in_specs=[spec4(4, n2), spec4(n, 2),
                  spec4(S5_GROUP, n2), spec4(S5_GROUP, n2), spec4(S5_GROUP, n), spec4(S5_GROUP, n)],
        out_specs=[pl.BlockSpec((S5_PREP_GROUPS, S5_CW, S5_CW), lambda i: (i, 0, 0)),
                   spec4(S5_CW, n2), spec4(n2, S5_CW), spec4(16, n2)],
        compiler_params=_params(1),
        name="s5_prep",
    )(arow, acol, bt(b_re), bt(b_im), c_re, c_im)


def _cmul_rows(x, p1, p2):
    return x * p1 + pltpu.roll(x, S5_N, 1) * p2


S5_OCT = LANES // S5_GROUP


def _block_transpose(xs):
    n = S5_OCT
    blk = lax.broadcasted_iota(jnp.int32, xs[0].shape, 1) >> 4
    a = [pltpu.roll(x, i * S5_GROUP, 1) if i else x for i, x in enumerate(xs)]
    ys = []
    for d in range(n):
        diag = a[-d % n]
        for b in range(1, n):
            diag = jnp.where(blk == b, a[(b - d) % n], diag)
        ys.append(pltpu.roll(diag, LANES - d * S5_GROUP, 1) if d else diag)
    return ys


def _tokens_to_chunks(u_ref, ug_ref):
    rows = u_ref.shape[1] // S5_T
    for octet in range(S5_GROUPS // S5_OCT):
        for half in range(2):
            xs = [u_ref[octet, pl.ds(S5_OCT * half + tt, rows, stride=S5_T), :] for tt in range(S5_OCT)]
            for gl, x in enumerate(_block_transpose(xs)):
                ug_ref[octet * S5_OCT + gl, :, half * LANES:(half + 1) * LANES] = x


def _chunks_to_tokens(yg_ref, y_ref):
    rows = y_ref.shape[1] // S5_T
    for octet in range(S5_GROUPS // S5_OCT):
        for half in range(2):
            ys = [yg_ref[octet * S5_OCT + gl, :, half * LANES:(half + 1) * LANES] for gl in range(S5_OCT)]
            for tt, y in enumerate(_block_transpose(ys)):
                y_ref[octet, pl.ds(S5_OCT * half + tt, rows, stride=S5_T), :] = y


def _s5_core_kernel(ug_ref, wi_ref, ws_ref, wo_ref, ap_ref, h0_ref, dv_ref, yg_ref, fin_ref, z_ref):
    n2 = 2 * S5_N
    r = lax.broadcasted_iota(jnp.int32, (S5_ROWS, n2), 0)
    in_p = r < S5_ROWS_P
    rib = jnp.where(in_p, r & (CH_P - 1), (r - S5_ROWS_P) & (CH_S - 1))
    nch = jnp.where(in_p, CH_P, CH_S)

    def one_group(gl, slot):
        ub = ug_ref[gl].astype(BF16)
        y = jnp.dot(ub, wi_ref[gl], preferred_element_type=F32)
        for d in range(2):
            p1, p2 = ap_ref[d, gl, 0:1, :], ap_ref[d, gl, 8:9, :]
            edge = [S5_ROWS_P + CH_S * b + (0 if d == 0 else CH_S - 1) for b in range(NB_S)]
            h0 = [h0_ref[gl, d, b:b + 1, :] for b in range(NB_S)]
            s = jnp.dot(ub, ws_ref[d, gl], preferred_element_type=F32)
            for b in range(NB_S):
                s = s + jnp.where(r == edge[b], _cmul_rows(h0[b], p1, p2), 0.0)
            def scan_step(x, k, pos, count):
                sh = 1 << k
                if d == 0:
                    t = jnp.where(pos >= sh, pltpu.roll(x, sh, 0), 0.0)
                else:
                    t = jnp.where(pos < count - sh, pltpu.roll(x, x.shape[0] - sh, 0), 0.0)
                return x + _cmul_rows(t, ap_ref[d, gl, k:k + 1, :], ap_ref[d, gl, 8 + k:9 + k, :])

            for k in range(CH_P.bit_length() - 1):
                s = scan_step(s, k, rib, nch)
            tail = s[S5_ROWS_P:]
            for k in range(CH_P.bit_length() - 1, CH_S.bit_length() - 1):
                tail = scan_step(tail, k, rib[S5_ROWS_P:], CH_S)
            s = jnp.concatenate([s[:S5_ROWS_P], tail], axis=0)
            z_ref[slot, d] = s
            first = CH_P - 1 if d == 0 else 0
            fin_ref[gl, d] = z_ref[slot, d, pl.ds(first, NB_P, stride=CH_P), :]
            if d == 0:
                sp = jnp.where(rib >= 1, pltpu.roll(s, 1, 0), 0.0)
            else:
                sp = jnp.where(rib < nch - 1, pltpu.roll(s, S5_ROWS - 1, 0), 0.0)
            for b in range(NB_S):
                sp = jnp.where(r == edge[b], h0[b], sp)
            y = y + jnp.dot(sp.astype(BF16), wo_ref[d, gl], preferred_element_type=F32)
        yg_ref[gl] = y + dv_ref[gl] * ug_ref[gl]

    def group_pair(gp, carry):
        for slot in range(2):
            one_group(2 * gp + slot, slot)
        return carry

    lax.fori_loop(0, S5_OCT // 2, group_pair, 0)


def _s5_core(ug, prep, h0, d_skip):
    w_intra, w_state, w_out, apow = prep
    g, n2 = S5_GROUPS, 2 * S5_N
    spec4 = lambda r, c: pl.BlockSpec((2, S5_OCT, r, c), lambda i: (0, i, 0, 0))
    chunks = pl.BlockSpec((S5_OCT, S5_ROWS, S5_CW), lambda i: (i, 0, 0))
    dvec = jnp.tile(d_skip.reshape(g, 1, S5_GROUP), (1, 1, S5_T))
    return pl.pallas_call(
        _s5_core_kernel,
        out_shape=[jax.ShapeDtypeStruct((g, S5_ROWS, S5_CW), F32),
                   jax.ShapeDtypeStruct((g, 2, NB_P, n2), F32)],
        grid=(g // S5_OCT,),
        in_specs=[chunks,
                  pl.BlockSpec((S5_OCT, S5_CW, S5_CW), lambda i: (i, 0, 0)),
                  spec4(S5_CW, n2), spec4(n2, S5_CW), spec4(16, n2),
                  pl.BlockSpec((S5_OCT, 2, 8, n2), lambda i: (i, 0, 0, 0)),
                  pl.BlockSpec((S5_OCT, 1, S5_CW), lambda i: (i, 0, 0))],
        out_specs=[chunks, pl.BlockSpec((S5_OCT, 2, NB_P, n2), lambda i: (i, 0, 0, 0))],
        scratch_shapes=[pltpu.VMEM((2, 2, S5_ROWS, n2), F32)],
        compiler_params=_params(1, VMEM_LIMIT),
        name="s5_scan",
    )(ug, w_intra, w_state, w_out, apow, h0, dvec)


DF_SCALE = DF_DH ** -0.5 * LOG2E
DFW = DF_HEADS * 2 * DF_DH
IN_B = 3 * HY_WIDTH + 2 * DFW + DF_HEADS * DF_V


def _inproj_b_kernel(y_ref, mod_ref, g_ref, win_ref, c_ref, s_ref,
                     hy_ref, q_ref, kp_ref, ks_ref, vp_ref, vs_ref, vc_ref, wbf_ref):
    @pl.when(pl.program_id(0) == 0)
    def _():
        wbf_ref[...] = win_ref[...].astype(BF16)

    mod = _mod_rows(mod_ref)
    h = _modulate(y_ref[...], g_ref[...], mod[3], mod[4]).astype(BF16)
    p = jnp.dot(h, wbf_ref[...], preferred_element_type=F32)
    o1 = 3 * HY_WIDTH
    hy_ref[...] = p[:, :o1]
    q_ref[...] = (_rope(p[:, o1:o1 + DFW], c_ref[...], s_ref[...]) * DF_SCALE).astype(BF16)
    _tok_write(kp_ref, ks_ref, _rope(p[:, o1 + DFW:o1 + 2 * DFW], c_ref[...], s_ref[...]))
    v = p[:, o1 + 2 * DFW:]
    _tok_write(vp_ref, vs_ref, v.astype(BF16))

    @pl.when(pl.program_id(0) < NT_P)
    def _():
        for hd in range(DF_HEADS):
            vc_ref[pl.ds(hd, TM, stride=DF_HEADS), :] = v[:, hd * DF_V:(hd + 1) * DF_V]


def _inproj_b(y, mods_l, g, w_in):
    cs, sn = _rope_tables(DFW, tuple(range(0, DFW, DF_DH)))
    pos = lambda i: (_pos_index(i), 0)
    k_shapes, k_specs = _split_out(DFW, BF16)
    v_specs = _split_out(DF_HEADS * DF_V)[1]
    v_shapes = [jax.ShapeDtypeStruct((TOK_P, DF_HEADS * DF_V), BF16), jax.ShapeDtypeStruct((TOK_S, DF_HEADS * DF_V), BF16),
                jax.ShapeDtypeStruct((TOK_P * DF_HEADS, DF_V), F32)]
    v_specs = v_specs + [pl.BlockSpec((TM * DF_HEADS, DF_V), _row_p)]
    hy_u, q, kp, ks, vp, vs, v_cache = pl.pallas_call(
        _inproj_b_kernel,
        out_shape=[jax.ShapeDtypeStruct((TOK, 3 * HY_WIDTH), F32), jax.ShapeDtypeStruct((TOK, DFW), BF16)]
                  + k_shapes + v_shapes,
        grid=(NT,),
        in_specs=[pl.BlockSpec((TM, D), _row),
                  _mod_spec(mods_l[1]),
                  _const_spec((1, D)), _const_spec((D, IN_B)),
                  pl.BlockSpec((TM, DFW), pos), pl.BlockSpec((TM, DFW), pos)],
        out_specs=[pl.BlockSpec((TM, 3 * HY_WIDTH), _row), pl.BlockSpec((TM, DFW), _row)] + k_specs + v_specs,
        scratch_shapes=[pltpu.VMEM((D, IN_B), BF16)],
        compiler_params=_params(1, VMEM_LIMIT),
        name="inproj_odd",
    )(y, mods_l[0], g[None], w_in, jnp.asarray(cs), jnp.asarray(sn))
    return hy_u, q, (kp, ks), (vp, vs), v_cache


def _diff_attn_kernel(nseg, nseq, lam_init, q_ref, lam_ref, sub_ref, *refs):
    o_ref = refs[-1]
    lp = lam_ref[...]
    lam = (jnp.exp(jnp.sum(lp[0:1] * lp[1:2], axis=-1, keepdims=True))
           - jnp.exp(jnp.sum(lp[2:3] * lp[3:4], axis=-1, keepdims=True)) + lam_init)
    tq = q_ref.shape[0] // nseq
    lane = lax.broadcasted_iota(jnp.int32, (tq, LANES), 1)
    for j in range(nseq):
        qr = slice(j * tq, (j + 1) * tq)
        krs = [slice(j * (refs[2 * s].shape[0] // nseq), (j + 1) * (refs[2 * s].shape[0] // nseq))
               for s in range(nseg)]
        for pair in range(DF_HEADS // 2):
            cs = slice(pair * LANES, (pair + 1) * LANES)
            q = q_ref[qr, cs]
            ks = [refs[2 * s][krs[s], cs].astype(BF16) for s in range(nseg)]
            vs = [refs[2 * s + 1][krs[s], cs].astype(BF16) for s in range(nseg)]
            outs = []
            for hh in range(2):
                parts = []
                for half in range(2):
                    unit = 2 * hh + half
                    qm = jnp.where((lane >> 5) == unit, q, jnp.zeros_like(q))
                    scores = [_dot_nt(qm, k) for k in ks]
                    parts.append(_softmax_pv(scores, vs, hh))
                o = parts[0] - lam * parts[1]
                mine = (lane >> 6) == hh
                ms = jnp.sum(jnp.where(mine, o * o, 0.0), axis=-1, keepdims=True) * (1.0 / DF_V)
                outs.append(o * lax.rsqrt(ms + EPS))
            o = jnp.where(lane < DF_V, outs[0], outs[1]) * sub_ref[...] * (1.0 - lam_init)
            o_ref[qr, cs] = o.astype(o_ref.dtype)


def _diff_attention(q, k, v, lam_p, subln, lam_init, n_batch, seq, tq, row0, ctx=None, nseq=1):
    qt = seq // tq
    qb0, kb0 = row0 // (nseq * tq), 0
    in_specs = [pl.BlockSpec((nseq * tq, DFW), lambda b, j: (qb0 + b * qt + j, 0)),
                pl.BlockSpec((4, DF_DH), lambda b, j: (0, 0)),
                pl.BlockSpec((1, LANES), lambda b, j: (0, 0))]
    args = [q, lam_p, jnp.concatenate([subln, subln])[None]]
    segs = []
    if ctx is not None:
        segs.append((ctx, PAST, 0))
    segs.append(((k, v), seq, kb0))
    for (a_k, a_v), ln, off in segs:
        idx = lambda b, j, off=off: (off + b, 0)
        in_specs += [pl.BlockSpec((nseq * ln, DFW), idx), pl.BlockSpec((nseq * ln, DF_HEADS * DF_V), idx)]
        args += [a_k, a_v]
    return pl.pallas_call(
        functools.partial(_diff_attn_kernel, len(segs), nseq, lam_init),
        out_shape=jax.ShapeDtypeStruct((n_batch * seq, DF_HEADS * DF_V), BF16),
        grid=(n_batch // nseq, qt),
        in_specs=in_specs,
        out_specs=pl.BlockSpec((nseq * tq, DF_HEADS * DF_V), lambda b, j: (b * qt + j, 0)),
        compiler_params=_params(2, VMEM_LIMIT),
        name="diff_attention",
    )(*args)


def _hy_filter_kernel(feat_ref, w1_ref, b1_ref, w2_ref, b2_ref, fq_ref, w3_ref, dec_ref, o_ref):
    feat = feat_ref[...]
    fq = fq_ref[...]
    h = jnp.sin(fq * (_dot3(feat, w1_ref[...]) + b1_ref[...]))
    h = jnp.sin(fq * (_dot3(h, w2_ref[...]) + b2_ref[...]))
    window = jnp.exp(-feat[:, 0:1] * jnp.abs(dec_ref[...]))
    for j in range(4):
        cs = slice(j * HY_WIDTH, (j + 1) * HY_WIDTH)
        o_ref[:, cs] = _dot3(h, w3_ref[:, cs]) * window


def _hy_spectrum_kernel(L, cs_ref, hf_ref, hb_ref, o_ref):
    row = lax.broadcasted_iota(jnp.int32, (L, HY_WIDTH), 0)
    first = row == 0
    tf = _dot(cs_ref[...], hf_ref[...])
    tb = _dot(cs_ref[...], jnp.where(first, 0.0, hb_ref[...]))
    ka = tf[:L] + tb[:L]
    kb = jnp.where(first, tf[L:] + tb[L:], tf[L:] - tb[L:])
    wv = jnp.where(first, 1.0 / (2 * L), 2.0 / (2 * L))
    o_ref[0, 0] = ka * wv
    o_ref[0, 1] = jnp.where(first, 0.0, kb) * wv
    o_ref[0, 2] = jnp.where(first, kb, ka) * wv


HY_CH = 256


def _hy_conv_kernel(L, cs_ref, ct_ref, kf_ref, v_ref, x1_ref, x2_ref,
                    wv_ref, w1_ref, w2_ref, bias_ref, o_ref):
    row = lax.broadcasted_iota(jnp.int32, (L, HY_CH), 0)

    def short(x, w):
        prev = jnp.where(row >= 1, pltpu.roll(x, 1, 0), 0.0)
        nxt = jnp.where(row <= L - 2, pltpu.roll(x, L - 1, 0), 0.0)
        return w[0:1] * prev + w[1:2] * x + w[2:3] * nxt

    for j in range(v_ref.shape[0] // L):
        rs = slice(j * L, (j + 1) * L)
        for k in range(HY_WIDTH // HY_CH):
            ch = slice(k * HY_CH, (k + 1) * HY_CH)
            z = short(v_ref[rs, ch], wv_ref[:, ch])
            gates = (short(x1_ref[rs, ch], w1_ref[:, ch]), short(x2_ref[rs, ch], w2_ref[:, ch]))
            for n in range(2):
                ab = _dot(cs_ref[...], z)
                a, b = ab[:L], ab[L:]
                ka, kb1, ka2 = kf_ref[n, 0, :, ch], kf_ref[n, 1, :, ch], kf_ref[n, 2, :, ch]
                pq = jnp.concatenate([a * ka - b * kb1, a * kb1 + b * ka2], axis=0)
                conv = _dot(ct_ref[...], pq)
                z = gates[n] * (conv + bias_ref[n:n + 1, ch] * z)
            o_ref[rs, ch] = z.astype(o_ref.dtype)


def _hyena_spectrum(L, phy):
    conv_w, w1, b1, w2, b2, freq, w3, decay, bias = phy
    feat = jnp.asarray(_hyena_features(L))
    w1p = jnp.pad(w1, ((0, LANES - HY_EMB), (0, 0)))
    filt = pl.pallas_call(
        _hy_filter_kernel,
        out_shape=jax.ShapeDtypeStruct((L, 4 * HY_WIDTH), F32),
        grid=(1,),
        in_specs=[_const_spec((L, LANES)), _const_spec((LANES, HY_FH)), _const_spec((1, HY_FH)),
                  _const_spec((HY_FH, HY_FH)), _const_spec((1, HY_FH)), _const_spec((1, HY_FH)),
                  _const_spec((HY_FH, 4 * HY_WIDTH)), _const_spec((1, HY_WIDTH))],
        out_specs=pl.BlockSpec((L, 4 * HY_WIDTH), lambda i: (0, 0)),
        compiler_params=_params(1, VMEM_LIMIT),
        name="hyena_filter",
    )(feat, w1p, b1[None], w2, b2[None], freq[None], w3, decay[None])
    cs = jnp.asarray(_dft_tables(L)[0]).astype(BF16)
    return pl.pallas_call(
        functools.partial(_hy_spectrum_kernel, L),
        out_shape=jax.ShapeDtypeStruct((2, 3, L, HY_WIDTH), F32),
        grid=(2,),
        in_specs=[_const_spec((2 * L, L)),
                  pl.BlockSpec((L, HY_WIDTH), lambda n: (0, n)),
                  pl.BlockSpec((L, HY_WIDTH), lambda n: (0, 2 + n))],
        out_specs=pl.BlockSpec((1, 3, L, HY_WIDTH), lambda n: (n, 0, 0, 0)),
        compiler_params=_params(1, VMEM_LIMIT),
        name="hyena_spectrum",
    )(cs, filt, filt)


def _hyena_conv(hy_u, spec, phy, n_batch, L, seqs, row0):
    conv_w, bias = phy[0], phy[8]
    cs, ct = (jnp.asarray(t).astype(BF16) for t in _dft_tables(L))
    rows = seqs * L
    rb0 = row0 // rows
    col = lambda off: (lambda b: (0, off))
    tok = lambda off: (lambda b: (rb0 + b, off))
    blk = lambda idx: pl.BlockSpec((rows, HY_WIDTH), idx)
    return pl.pallas_call(
        functools.partial(_hy_conv_kernel, L),
        out_shape=jax.ShapeDtypeStruct((n_batch * L, HY_WIDTH), BF16),
        grid=(n_batch // seqs,),
        in_specs=[_const_spec((2 * L, L)), _const_spec((L, 2 * L)), _const_spec((2, 3, L, HY_WIDTH)),
                  blk(tok(0)), blk(tok(1)), blk(tok(2)),
                  pl.BlockSpec((3, HY_WIDTH), col(0)), pl.BlockSpec((3, HY_WIDTH), col(1)),
                  pl.BlockSpec((3, HY_WIDTH), col(2)), _const_spec((2, HY_WIDTH))],
        out_specs=blk(lambda b: (b, 0)),
        compiler_params=_params(1, VMEM_LIMIT),
        name="hyena_conv",
    )(cs, ct, spec, hy_u, hy_u, hy_u, conv_w, conv_w, conv_w, bias)


def _even_mixer(y, mods_l, g, pa, ps5, ctx_ckv, ctx_krope, ctx_state):
    w_in, w_out, q_norm, w_uq, kv_norm, w_ukv = pa
    a_re, a_im, log_step, b_re, b_im, c_re, c_im, d_skip, w_glu = ps5
    q, ckv, kr_unrot, kr_rot, kn, v, ug, (w_k, w_v) = _inproj_a(y, mods_l, g, w_in, q_norm, w_uq, kv_norm, w_ukv)

    ctx_flat = ctx_ckv.reshape(NB_S * PAST, MLA_KV_RANK)
    ctx_kn = _linear(ctx_flat, w_k, PAST, BF16)
    ctx_v = _linear(ctx_flat, w_v, PAST, BF16)
    ctx_kr = jnp.pad(ctx_krope.reshape(NB_S * PAST, MLA_ROPE),
                     ((0, 0), (KR_AT, LANES - KR_AT - MLA_ROPE))).astype(BF16)
    att_p = _mla_attention(q, kn, kr_rot, v, NB_P, L_P, L_P, 0, nseq=2)
    att_s = _mla_attention(q, kn, kr_rot, v, NB_S, L_S, TM, TOK_P, ctx=(ctx_kn, ctx_kr, ctx_v))

    prep = _s5_prep(a_re, a_im, log_step, b_re, b_im, c_re, c_im)
    h0 = ctx_state.transpose(3, 1, 0, 2, 4).reshape(S5_GROUPS, 2, NB_S, 2 * S5_N)
    h0 = jnp.pad(h0, ((0, 0), (0, 0), (0, 8 - NB_S), (0, 0)))
    s5y, fin = _s5_core(ug, prep, h0, d_skip)

    mixer = ((att_p, att_s), s5y, w_out, w_glu)
    new_ckv = ckv.reshape(NB_P, L_P, MLA_KV_RANK)
    new_krope = kr_unrot.reshape(NB_P, L_P, MLA_ROPE)
    new_state = fin.reshape(S5_GROUPS, 2, NB_P, 2, S5_N).transpose(2, 1, 3, 0, 4)
    return mixer, new_ckv, new_krope, new_state


def _odd_mixer(y, mods_l, g, pb, phy, ctx_k, ctx_v, lam_init):
    w_in, w_out, lam_p, subln = pb
    hy_u, q, (k_p, k_s), (v_p, v_s), v_cache = _inproj_b(y, mods_l, g, w_in)
    hy_p = _hyena_conv(hy_u, _hyena_spectrum(L_P, phy), phy, NB_P, L_P, 4, 0)
    hy_s = _hyena_conv(hy_u, _hyena_spectrum(L_S, phy), phy, NB_S, L_S, 1, TOK_P)
    ctx = (ctx_k.reshape(NB_S * PAST, DFW), ctx_v.reshape(NB_S * PAST, DF_HEADS * DF_V))
    att_p = _diff_attention(q, k_p, v_p, lam_p, subln, lam_init, NB_P, L_P, L_P, 0, nseq=2)
    att_s = _diff_attention(q, k_s, v_s, lam_p, subln, lam_init, NB_S, L_S, TM // 2, TOK_P, ctx=ctx)
    mixer = ((hy_p, hy_s), (att_p, att_s), w_out, None)
    new_k = k_p.reshape(NB_P, L_P, DF_HEADS, 2, DF_DH)
    new_v = v_cache.reshape(NB_P, L_P, DF_HEADS, DF_V)
    return mixer, new_k, new_v


def kernel(x_prompt, x_sample, c, c_ctx, cache_mla_ckv, cache_mla_krope, state_s5, cache_diff_k, cache_diff_v, ada_w, ada_b, norm_g, ff_w_in, ff_w_out, w_in_a, w_out_a, mla_q_norm, mla_w_uq, mla_kv_norm, mla_w_ukv, s5_a_re, s5_a_im, s5_log_step, s5_b_re, s5_b_im, s5_c_re, s5_c_im, s5_d, s5_w_glu, w_in_b, w_out_b, hy_conv, hy_w1, hy_b1, hy_w2, hy_b2, hy_freq, hy_w3, hy_decay, hy_bias, df_lambda, df_subln, final_norm):
    depth = ada_w.shape[0]
    y = (x_prompt.reshape(TOK_P, D), x_sample.reshape(TOK_S, D))
    mods = _adaln(jnp.concatenate([c_ctx[None], c], axis=0), ada_w, ada_b)
    new_ckv, new_krope, new_s5, new_dk, new_dv = [], [], [], [], []
    for l in range(depth):
        y = _half_ffn(y, (mods, l), norm_g[l, 0], ff_w_in, ff_w_out, l, 0)
        if l % 2 == 0:
            e = l // 2
            pa = (w_in_a[e], w_out_a[e], mla_q_norm[e], mla_w_uq[e], mla_kv_norm[e], mla_w_ukv[e])
            ps5 = (s5_a_re[e], s5_a_im[e], s5_log_step[e], s5_b_re[e], s5_b_im[e],
                   s5_c_re[e], s5_c_im[e], s5_d[e], s5_w_glu[e])
            mixer, ckv, krope, st = _even_mixer(y, (mods, l), norm_g[l, 1], pa, ps5, cache_mla_ckv[:, e],
                                            cache_mla_krope[:, e], state_s5[:, e])
            new_ckv.append(ckv)
            new_krope.append(krope)
            new_s5.append(st)
        else:
            o = l // 2
            lam_init = 0.8 - 0.6 * math.exp(-0.3 * l)
            pb = (w_in_b[o], w_out_b[o], df_lambda[o], df_subln[o])
            phy = (hy_conv[o], hy_w1[o], hy_b1[o], hy_w2[o], hy_b2[o], hy_freq[o],
                   hy_w3[o], hy_decay[o], hy_bias[o])
            mixer, dk, dv = _odd_mixer(y, (mods, l), norm_g[l, 1], pb, phy, cache_diff_k[:, o],
                                   cache_diff_v[:, o], lam_init)
            new_dk.append(dk)
            new_dv.append(dv)
        last = l == depth - 1
        y = _half_ffn(y, (mods, l), norm_g[l, 2], ff_w_in, ff_w_out, l, 1,
                      final_g=final_norm if last else None, mixer=mixer)
    y_prompt = y[0].reshape(NB_P, L_P, D)
    y_sample = y[1].reshape(NB_S, L_S, D)
    return (y_prompt, y_sample, jnp.stack(new_ckv, axis=1), jnp.stack(new_krope, axis=1),
            jnp.stack(new_s5, axis=1), jnp.stack(new_dk, axis=1), jnp.stack(new_dv, axis=1))
```

```python
import functools
import math

import numpy as np
import jax
import jax.numpy as jnp
from jax import lax
from jax.experimental import pallas as pl
from jax.experimental.pallas import tpu as pltpu

F32 = jnp.float32
BF16 = jnp.bfloat16

D = 1024
NB_P, L_P = 16, 256
NB_S, L_S = 2, 1024
PAST = 256
GRID_W = 64
N_MOD = 9
FF = 2816
EPS = 1e-6
ROPE_BASE = 10000.0

MLA_HEADS, MLA_NOPE, MLA_ROPE, MLA_V = 8, 64, 32, 64
MLA_Q_RANK, MLA_KV_RANK = 384, 256
S5_WIDTH, S5_GROUP, S5_N = 512, 16, 64
S5_GROUPS = S5_WIDTH // S5_GROUP
HY_WIDTH, HY_BANDS, HY_FH = 512, 16, 64
HY_EMB = 2 * HY_BANDS + 1
DF_HEADS, DF_DH = 8, 32
DF_V = 2 * DF_DH

TOK_P = NB_P * L_P
TOK_S = NB_S * L_S
TOK = TOK_P + TOK_S
TM = 512
NT = TOK // TM
NT_P = TOK_P // TM
TILES_PER_SAMPLE = L_S // TM

LANES = 128
S5_T = 16
S5_CW = S5_T * S5_GROUP
CH_P = L_P // S5_T
CH_S = L_S // S5_T
S5_ROWS = NB_P * CH_P + NB_S * CH_S
S5_ROWS_P = NB_P * CH_P

VMEM_LIMIT = 56 * 1024 * 1024


def _params(n_grid, vmem=None):
    return pltpu.CompilerParams(dimension_semantics=("arbitrary",) * n_grid,
                                vmem_limit_bytes=vmem)


def _const_spec(shape):
    nd = len(shape)
    return pl.BlockSpec(shape, lambda *_: (0,) * nd, pipeline_mode=pl.Buffered(1))


def _mod_index(i):
    return jnp.where(i < NT_P, 0, 1 + (i - NT_P) // TILES_PER_SAMPLE)


def _mod_spec(layer):
    return pl.BlockSpec((1, 8, N_MOD * D), lambda *_: (layer, 0, 0), pipeline_mode=pl.Buffered(1))


def _mod_rows(mod_ref):
    row = mod_ref[0, pl.ds(_mod_index(pl.program_id(0)), 1), :]
    return [row[:, k * D:(k + 1) * D] for k in range(N_MOD)]


def _pos_index(i):
    return jnp.where(i < NT_P, 0, 1 + (i - NT_P) % TILES_PER_SAMPLE)


def _row(i):
    return (i, 0)


def _row_p(i):
    return (jnp.minimum(i, NT_P - 1), 0)


def _row_s(i):
    return (jnp.maximum(i - NT_P, 0), 0)


def _tok_specs(x, width):
    if isinstance(x, tuple):
        return [pl.BlockSpec((TM, width), _row_p), pl.BlockSpec((TM, width), _row_s)], list(x)
    return [pl.BlockSpec((TM, width), _row)], [x]


def _tok_read(refs, split):
    if split:
        return jnp.where(pl.program_id(0) < NT_P, refs[0][...], refs[1][...]), refs[2:]
    return refs[0][...], refs[1:]


def _tok_write(p_ref, s_ref, value):
    i = pl.program_id(0)

    @pl.when(i < NT_P)
    def _():
        p_ref[...] = value

    @pl.when(i >= NT_P)
    def _():
        s_ref[...] = value.astype(s_ref.dtype)


def _split_out(width, sample_dtype=F32):
    shapes = [jax.ShapeDtypeStruct((TOK_P, width), F32), jax.ShapeDtypeStruct((TOK_S, width), sample_dtype)]
    specs = [pl.BlockSpec((TM, width), _row_p), pl.BlockSpec((TM, width), _row_s)]
    return shapes, specs


def _dot(a, b):
    return jnp.dot(a.astype(BF16), b.astype(BF16), preferred_element_type=F32)


def _dot_nt(a, b):
    return lax.dot_general(a, b, (((1,), (1,)), ((), ())), preferred_element_type=F32)


def _split(x):
    hi = x.astype(BF16)
    lo = (x - hi.astype(F32)).astype(BF16)
    return hi, lo


def _dot3(a, b):
    ah, al = _split(a)
    bh, bl = _split(b)
    d = functools.partial(jnp.dot, preferred_element_type=F32)
    return d(ah, bh) + d(ah, bl) + d(al, bh)


def _rmsnorm(x, g):
    return x * lax.rsqrt(jnp.mean(x * x, axis=-1, keepdims=True) + EPS) * g


def _modulate(y, g, shift, scale):
    return _rmsnorm(y, g) * (1.0 + scale) + shift


def _pair_swap(x):
    n = x.shape[-1]
    lane = lax.broadcasted_iota(jnp.int32, x.shape, x.ndim - 1)
    return jnp.where((lane & 1) == 0, pltpu.roll(x, n - 1, x.ndim - 1), pltpu.roll(x, 1, x.ndim - 1))


def _rope(x, cos, sin_signed):
    return x * cos + _pair_swap(x) * sin_signed


def _rope_angles():
    n_freq = MLA_ROPE // 4
    inv = 1.0 / (ROPE_BASE ** (np.arange(n_freq, dtype=np.float64) / n_freq))
    pos = np.arange(L_S)
    row = (pos // GRID_W).astype(np.float64)
    col = (pos % GRID_W).astype(np.float64)
    ang = np.concatenate([row[:, None] * inv, col[:, None] * inv], axis=-1)
    return np.cos(ang), np.sin(ang)


@functools.lru_cache(maxsize=None)
def _rope_tables(width, starts):
    cos, sin = _rope_angles()
    c = np.ones((TM + L_S, width), np.float32)
    s = np.zeros((TM + L_S, width), np.float32)
    sign = np.where(np.arange(MLA_ROPE) % 2 == 0, -1.0, 1.0)
    unit_c = np.repeat(cos, 2, axis=1)
    unit_s = np.repeat(sin, 2, axis=1) * sign
    for st in starts:
        c[TM:, st:st + MLA_ROPE] = unit_c
        s[TM:, st:st + MLA_ROPE] = unit_s
    return c, s


@functools.lru_cache(maxsize=None)
def _dft_tables(L):
    f = np.arange(L)[:, None]
    s = np.arange(L)[None, :]
    ang = np.pi * ((f * s) % (2 * L)).astype(np.float64) / L
    cs = np.concatenate([np.cos(ang), np.sin(ang)], axis=0)
    cs[L, :] = np.where(np.arange(L) % 2 == 0, 1.0, -1.0)
    cs = cs.astype(np.float32)
    return cs, np.ascontiguousarray(cs.T)


@functools.lru_cache(maxsize=None)
def _hyena_features(L):
    t = np.arange(L, dtype=np.float64) / L
    bands = np.arange(1, HY_BANDS + 1, dtype=np.float64)
    ang = 2.0 * math.pi * t[:, None] * bands
    feat = np.zeros((L, LANES), np.float32)
    feat[:, 0] = t
    feat[:, 1:1 + HY_BANDS] = np.cos(ang)
    feat[:, 1 + HY_BANDS:HY_EMB] = np.sin(ang)
    return feat


def _adaln_kernel(c_ref, w_ref, b_ref, o_ref):
    s = jax.nn.silu(c_ref[...])
    s_hi = s.astype(BF16).astype(F32)
    stacked = jnp.concatenate([s_hi, s - s_hi], axis=0).astype(BF16)
    wh, wl = _split(w_ref[0])
    both = jnp.dot(stacked, wh, preferred_element_type=F32)
    rows = c_ref.shape[0]
    o_ref[0] = both[:rows] + both[rows:] + jnp.dot(stacked, wl, preferred_element_type=F32)[:rows] + b_ref[0]


def _adaln(cvecs, ada_w, ada_b):
    depth = ada_w.shape[0]
    n_vec = cvecs.shape[0]
    tn = N_MOD * D // 4
    out = pl.pallas_call(
        _adaln_kernel,
        out_shape=jax.ShapeDtypeStruct((depth, 8, N_MOD * D), F32),
        grid=(depth, N_MOD * D // tn),
        in_specs=[pl.BlockSpec((8, D), lambda l, j: (0, 0)),
                  pl.BlockSpec((1, D, tn), lambda l, j: (l, 0, j)),
                  pl.BlockSpec((1, 1, tn), lambda l, j: (l, 0, j))],
        out_specs=pl.BlockSpec((1, 8, tn), lambda l, j: (l, 0, j)),
        compiler_params=_params(2, VMEM_LIMIT),
        name="adaln",
    )(jnp.pad(cvecs, ((0, 8 - n_vec), (0, 0))), ada_w, ada_b[:, None, :])
    return out


FF_PIECE = 256
FF_LOADS = FF // FF_PIECE


def _ffn_kernel(base, final, split_in, mixer, layer, which, *refs):
    y, refs = _tok_read(refs, split_in)
    if mixer is not None:
        a1, refs = _tok_read(refs, mixer[0])
        if mixer[2]:
            a2_chunks, refs = refs[0], refs[1:]
        else:
            a2, refs = _tok_read(refs, mixer[1])
        wmix_ref, wg_ref = refs[:2]
        refs = refs[2:]
    mod_ref, g_ref, win_hbm, wout_hbm, fg_ref = refs[:5]
    n_out = 2 if final else 1
    outs = refs[5:5 + n_out]
    win_ref, wout_ref, stage_g, stage_u, stage_o, sems = refs[5 + n_out:11 + n_out]
    mod = _mod_rows(mod_ref)
    if mixer is not None:
        if mixer[2]:
            a2_scr = refs[11 + n_out]
            _chunks_to_tokens(a2_chunks, a2_scr)
            a2 = jax.nn.gelu(jnp.concatenate([a2_scr[o] for o in range(a2_scr.shape[0])], axis=1))
            a2 = a2 * jax.nn.sigmoid(_dot(a2, wg_ref[...]))
        k1 = wmix_ref.shape[0] // 2
        y = y + mod[5] * (_dot(a1, wmix_ref[:k1]) + _dot(a2, wmix_ref[k1:]))
    h = _modulate(y, g_ref[...], mod[base], mod[base + 1]).astype(BF16)

    def hidden(lo, width):
        gate = jnp.dot(h, win_ref[:, lo:lo + width], preferred_element_type=F32)
        up = jnp.dot(h, win_ref[:, FF + lo:FF + lo + width], preferred_element_type=F32)
        a = (jax.nn.silu(gate) * up).astype(BF16)
        return jnp.dot(a, wout_ref[lo:lo + width, :], preferred_element_type=F32)

    def finish(acc):
        out = y + 0.5 * mod[base + 2] * acc
        if final:
            _tok_write(outs[0], outs[1], _rmsnorm(out, fg_ref[...]))
        else:
            outs[0][...] = out

    @pl.when(pl.program_id(0) == 0)
    def _():
        def copies(c, slot):
            cols = pl.ds(c * FF_PIECE, FF_PIECE)
            return (pltpu.make_async_copy(win_hbm.at[layer, which, :, cols], stage_g.at[slot], sems.at[0, slot]),
                    pltpu.make_async_copy(win_hbm.at[layer, which, :, pl.ds(FF + c * FF_PIECE, FF_PIECE)],
                                          stage_u.at[slot], sems.at[1, slot]),
                    pltpu.make_async_copy(wout_hbm.at[layer, which, cols, :], stage_o.at[slot], sems.at[2, slot]))

        for cp in copies(0, 0):
            cp.start()
        acc = jnp.zeros(y.shape, F32)
        for c in range(FF_LOADS):
            slot = c % 2
            lo = c * FF_PIECE
            if c + 1 < FF_LOADS:
                for cp in copies(c + 1, 1 - slot):
                    cp.start()
            for cp in copies(c, slot):
                cp.wait()
            win_ref[:, lo:lo + FF_PIECE] = stage_g[slot].astype(BF16)
            win_ref[:, FF + lo:FF + lo + FF_PIECE] = stage_u[slot].astype(BF16)
            wout_ref[lo:lo + FF_PIECE, :] = stage_o[slot].astype(BF16)
            acc = acc + hidden(lo, FF_PIECE)
        finish(acc)

    @pl.when(pl.program_id(0) > 0)
    def _():
        finish(hidden(0, FF))


def _half_ffn(y, mods_l, g, ff_w_in, ff_w_out, layer, which, final_g=None, mixer=None):
    final = final_g is not None
    fg = final_g if final else g
    y_specs, y_args = _tok_specs(y, D)
    mix_flags = None
    extra_scratch = []
    if mixer is not None:
        a1, a2, w_out, w_glu = mixer
        k1 = w_out.shape[0] // 2
        wg = w_glu if w_glu is not None else jnp.zeros((8, LANES), F32)
        s1, a1_args = _tok_specs(a1, k1)
        if w_glu is not None:
            s2 = [pl.BlockSpec((S5_GROUPS, TM // S5_T, S5_CW), lambda i: (0, i, 0))]
            a2_args = [a2]
            extra_scratch = [pltpu.VMEM((k1 // LANES, TM, LANES), F32)]
        else:
            s2, a2_args = _tok_specs(a2, k1)
        y_specs = y_specs + s1 + s2 + [_const_spec(w_out.shape), _const_spec(wg.shape)]
        y_args = y_args + a1_args + a2_args + [w_out, wg]
        mix_flags = (isinstance(a1, tuple), isinstance(a2, tuple), w_glu is not None)
    if final:
        out_shape, out_specs = _split_out(D)
    else:
        out_shape, out_specs = jax.ShapeDtypeStruct((TOK, D), F32), pl.BlockSpec((TM, D), _row)
    return pl.pallas_call(
        functools.partial(_ffn_kernel, 6 * which, final, isinstance(y, tuple), mix_flags, layer, which),
        out_shape=out_shape,
        grid=(NT,),
        in_specs=y_specs + [_mod_spec(mods_l[1]),
                            _const_spec((1, D)),
                            pl.BlockSpec(memory_space=pl.ANY),
                            pl.BlockSpec(memory_space=pl.ANY),
                            _const_spec((1, D))],
        out_specs=out_specs,
        scratch_shapes=[pltpu.VMEM((D, 2 * FF), BF16), pltpu.VMEM((FF, D), BF16),
                        pltpu.VMEM((2, D, FF_PIECE), F32), pltpu.VMEM((2, D, FF_PIECE), F32),
                        pltpu.VMEM((2, FF_PIECE, D), F32), pltpu.SemaphoreType.DMA((3, 2))] + extra_scratch,
        compiler_params=_params(1, VMEM_LIMIT),
        name="half_ffn",
    )(*y_args, mods_l[0], g[None], ff_w_in, ff_w_out, fg[None])


def _linear_kernel(x_ref, w_ref, o_ref):
    o_ref[...] = _dot(x_ref[...], w_ref[...]).astype(o_ref.dtype)


def _linear(x, w, tm, out_dtype):
    m, k = x.shape
    n = w.shape[1]
    return pl.pallas_call(
        _linear_kernel,
        out_shape=jax.ShapeDtypeStruct((m, n), out_dtype),
        grid=(m // tm,),
        in_specs=[pl.BlockSpec((tm, k), lambda i: (i, 0)), _const_spec((k, n))],
        out_specs=pl.BlockSpec((tm, n), lambda i: (i, 0)),
        compiler_params=_params(1),
        name="linear",
    )(x, w.astype(BF16))


LOG2E = math.log2(math.e)
MLA_SCALE = (MLA_NOPE + MLA_ROPE) ** -0.5 * LOG2E
QW = MLA_HEADS * LANES
KR_AT = MLA_NOPE
IN_A_PAD = MLA_Q_RANK + MLA_KV_RANK + S5_WIDTH + LANES


def _inproj_a_kernel(y_ref, mod_ref, g_ref, win_ref, qn_ref, wuq_ref, kvn_ref, wk_ref, wv_ref,
                     cq_ref, sq_ref, ck_ref, sk_ref,
                     q_ref, ckv_ref, kru_ref, krr_ref, kn_ref, v_ref, ug_ref, u_scr):
    mod = _mod_rows(mod_ref)
    h = _modulate(y_ref[...], g_ref[...], mod[3], mod[4]).astype(BF16)
    p = jnp.dot(h, win_ref[...], preferred_element_type=F32)
    o1 = MLA_Q_RANK
    o2 = o1 + MLA_KV_RANK
    o3 = o2 + S5_WIDTH
    q = _dot(_rmsnorm(p[:, :o1], qn_ref[...]), wuq_ref[...])
    q_ref[...] = (_rope(q, cq_ref[...], sq_ref[...]) * MLA_SCALE).astype(BF16)
    ckv = _rmsnorm(p[:, o1:o2], kvn_ref[...])
    ckv_b = ckv.astype(BF16)
    kn_ref[...] = jnp.dot(ckv_b, wk_ref[...], preferred_element_type=F32).astype(BF16)
    v_ref[...] = jnp.dot(ckv_b, wv_ref[...], preferred_element_type=F32).astype(BF16)
    for octet in range(S5_WIDTH // LANES):
        u_scr[octet] = p[:, o2 + octet * LANES:o2 + (octet + 1) * LANES]
    _tokens_to_chunks(u_scr, ug_ref)
    krp = p[:, o3:]
    krr_ref[...] = _rope(krp, ck_ref[...], sk_ref[...]).astype(BF16)

    @pl.when(pl.program_id(0) < NT_P)
    def _():
        ckv_ref[...] = ckv
        kru_ref[...] = krp[:, KR_AT:KR_AT + MLA_ROPE]


def _inproj_a(y, mods_l, g, w_in, q_norm, w_uq, kv_norm, w_ukv):
    o1 = MLA_Q_RANK
    o2 = o1 + MLA_KV_RANK
    o3 = o2 + MLA_ROPE
    kr_cols = jnp.pad(w_in[:, o2:o3], ((0, 0), (KR_AT, LANES - KR_AT - MLA_ROPE)))
    w_ext = jnp.concatenate([w_in[:, :o2], w_in[:, o3:], kr_cols], axis=1).astype(BF16)
    dq = MLA_NOPE + MLA_ROPE
    w_uq_pad = jnp.pad(w_uq.reshape(MLA_Q_RANK, MLA_HEADS, dq),
                       ((0, 0), (0, 0), (0, LANES - dq))).reshape(MLA_Q_RANK, QW).astype(BF16)
    w_kv = w_ukv.reshape(MLA_KV_RANK, MLA_HEADS, MLA_NOPE + MLA_V)
    w_k = jnp.pad(w_kv[:, :, :MLA_NOPE], ((0, 0), (0, 0), (0, LANES - MLA_NOPE))).reshape(MLA_KV_RANK, QW)
    w_v = w_kv[:, :, MLA_NOPE:].reshape(MLA_KV_RANK, MLA_HEADS * MLA_V)
    w_k, w_v = w_k.astype(BF16), w_v.astype(BF16)
    cq, sq = _rope_tables(QW, tuple(h * LANES + MLA_NOPE for h in range(MLA_HEADS)))
    ck, sk = _rope_tables(LANES, (KR_AT,))
    row = _row
    pos = lambda i: (_pos_index(i), 0)
    widths = (QW, MLA_KV_RANK, MLA_ROPE, LANES, QW, MLA_HEADS * MLA_V)
    prompt_only = (1, 2)
    mxu_only = (0, 3, 4, 5)
    outs = pl.pallas_call(
        _inproj_a_kernel,
        out_shape=[jax.ShapeDtypeStruct((TOK_P if k in prompt_only else TOK, w), BF16 if k in mxu_only else F32)
                   for k, w in enumerate(widths)]
                  + [jax.ShapeDtypeStruct((S5_GROUPS, S5_ROWS, S5_CW), F32)],
        grid=(NT,),
        in_specs=[pl.BlockSpec((TM, D), row),
                  _mod_spec(mods_l[1]),
                  _const_spec((1, D)),
                  _const_spec((D, IN_A_PAD)),
                  _const_spec((1, MLA_Q_RANK)),
                  _const_spec((MLA_Q_RANK, QW)),
                  _const_spec((1, MLA_KV_RANK)),
                  _const_spec((MLA_KV_RANK, QW)),
                  _const_spec((MLA_KV_RANK, MLA_HEADS * MLA_V)),
                  pl.BlockSpec((TM, QW), pos), pl.BlockSpec((TM, QW), pos),
                  pl.BlockSpec((TM, LANES), pos), pl.BlockSpec((TM, LANES), pos)],
        out_specs=[pl.BlockSpec((TM, w), _row_p if k in prompt_only else row)
                   for k, w in enumerate(widths)]
                  + [pl.BlockSpec((S5_GROUPS, TM // S5_T, S5_CW), lambda i: (0, i, 0))],
        scratch_shapes=[pltpu.VMEM((S5_WIDTH // LANES, TM, LANES), F32)],
        compiler_params=_params(1, VMEM_LIMIT),
        name="inproj_even",
    )(y, mods_l[0], g[None], w_ext, q_norm[None], w_uq_pad, kv_norm[None], w_k, w_v,
      jnp.asarray(cq), jnp.asarray(sq), jnp.asarray(ck), jnp.asarray(sk))
    q, ckv, kr_unrot, kr_rot, kn, v, ug = outs
    return q, ckv, kr_unrot, kr_rot, kn, v, ug, (w_k, w_v)


def _softmax_pv(scores, vals, half):
    m = functools.reduce(jnp.maximum, [jnp.max(s, axis=-1, keepdims=True) for s in scores])
    pv = None
    for s, v in zip(scores, vals):
        own_k = (lax.broadcasted_iota(jnp.int32, v.shape, 1) >> 6) == half
        part = jnp.dot(jnp.exp2(s - m).astype(BF16), jnp.where(own_k, v, jnp.ones_like(v)),
                       preferred_element_type=F32)
        pv = part if pv is None else pv + part
    own = (lax.broadcasted_iota(jnp.int32, pv.shape, 1) >> 6) == half
    denom = jnp.max(jnp.where(own, 0.0, pv), axis=-1, keepdims=True)
    return pv * (1.0 / denom)


def _mla_attn_kernel(nseg, nseq, q_ref, *refs):
    o_ref = refs[-1]
    tq = q_ref.shape[0] // nseq
    lane = lax.broadcasted_iota(jnp.int32, (tq, LANES), 1)
    for j in range(nseq):
        qr = slice(j * tq, (j + 1) * tq)
        krs = [slice(j * (refs[3 * s].shape[0] // nseq), (j + 1) * (refs[3 * s].shape[0] // nseq))
               for s in range(nseg)]
        for pair in range(MLA_HEADS // 2):
            outs = []
            for hh in range(2):
                h = 2 * pair + hh
                hs = slice(h * LANES, (h + 1) * LANES)
                qh = q_ref[qr, hs]
                scores = []
                for s in range(nseg):
                    kn_ref, kr_ref = refs[3 * s], refs[3 * s + 1]
                    kh = (kn_ref[krs[s], hs] + kr_ref[krs[s], :]).astype(BF16)
                    scores.append(_dot_nt(qh, kh))
                vals = [refs[3 * s + 2][krs[s], pair * LANES:(pair + 1) * LANES] for s in range(nseg)]
                outs.append(_softmax_pv(scores, vals, hh))
            o_ref[qr, pair * LANES:(pair + 1) * LANES] = jnp.where(lane < MLA_V, outs[0], outs[1]).astype(o_ref.dtype)


def _mla_attention(q, kn, kr, v, n_batch, seq, tq, row0, ctx=None, nseq=1):
    qt = seq // tq
    qb0, kb0 = row0 // (nseq * tq), row0 // (nseq * seq)
    in_specs = [pl.BlockSpec((nseq * tq, QW), lambda b, j: (qb0 + b * qt + j, 0))]
    args = [q]
    segs = []
    if ctx is not None:
        segs.append((ctx, PAST, 0))
    segs.append(((kn, kr, v), seq, kb0))
    for (a_kn, a_kr, a_v), ln, off in segs:
        idx = lambda b, j, off=off: (off + b, 0)
        in_specs += [pl.BlockSpec((nseq * ln, QW), idx), pl.BlockSpec((nseq * ln, LANES), idx),
                     pl.BlockSpec((nseq * ln, MLA_HEADS * MLA_V), idx)]
        args += [a_kn, a_kr, a_v]
    return pl.pallas_call(
        functools.partial(_mla_attn_kernel, len(segs), nseq),
        out_shape=jax.ShapeDtypeStruct((n_batch * seq, MLA_HEADS * MLA_V), BF16),
        grid=(n_batch // nseq, qt),
        in_specs=in_specs,
        out_specs=pl.BlockSpec((nseq * tq, MLA_HEADS * MLA_V), lambda b, j: (b * qt + j, 0)),
        compiler_params=_params(2, VMEM_LIMIT),
        name="mla_attention",
    )(*args)


def _cpow(ar, ai, e, nbits):
    rr = jnp.ones_like(ar)
    ri = jnp.zeros_like(ar)
    br, bi = ar, ai
    for k in range(nbits):
        bit = ((e >> k) & 1) == 1
        nr = rr * br - ri * bi
        ni = rr * bi + ri * br
        rr = jnp.where(bit, nr, rr)
        ri = jnp.where(bit, ni, ri)
        if k + 1 < nbits:
            br, bi = br * br - bi * bi, 2.0 * br * bi
    return rr, ri


def _s5_abar_kernel(lr_ref, li_ref, ls_ref, o_ref):
    step = jnp.exp(ls_ref[...])
    lr = jnp.minimum(lr_ref[...], -1e-4)
    li = li_ref[...]
    mag = jnp.exp(lr * step)
    ar = mag * jnp.cos(li * step)
    ai = mag * jnp.sin(li * step)
    den = lr * lr + li * li
    o_ref[0] = ar
    o_ref[1] = ai
    o_ref[2] = ((ar - 1.0) * lr + ai * li) / den
    o_ref[3] = (ai * lr - (ar - 1.0) * li) / den


S5_PREP_GROUPS = 4


def _s5_prep_kernel(*refs):
    for gi in range(S5_PREP_GROUPS):
        _s5_prep_group(gi, *refs)


def _s5_prep_group(gi, arow_ref, acol_ref, btr_ref, bti_ref, ctr_ref, cti_ref,
                   wi_ref, ws_ref, wo_ref, ap_ref):
    n2 = 2 * S5_N
    blk_o = lax.broadcasted_iota(jnp.int32, (S5_N, S5_CW), 1) >> 4
    lane_k = lax.broadcasted_iota(jnp.int32, (S5_GROUP, S5_CW), 1)
    row_k = lax.broadcasted_iota(jnp.int32, (S5_GROUP, S5_CW), 0)
    lane_b = lax.broadcasted_iota(jnp.int32, (S5_GROUP, n2), 1)
    lane_a = lax.broadcasted_iota(jnp.int32, (1, n2), 1)
    rep = ((lane_k & (S5_GROUP - 1)) == row_k).astype(BF16)

    def tile16(x):
        hi = x.astype(BF16)
        r1 = x - hi.astype(F32)
        mid = r1.astype(BF16)
        lo = (r1 - mid.astype(F32)).astype(BF16)
        d = lambda a: lax.dot_general(a, rep, (((0,), (0,)), ((), ())), preferred_element_type=F32)
        return d(hi) + d(mid) + d(lo)

    intra = [None] * S5_T
    for d in range(2):
        ar, ai, fr, fi = (arow_ref[d, gi, k:k + 1, :] for k in range(4))
        btr, bti = btr_ref[d, gi], bti_ref[d, gi]
        bbr = fr * btr - fi * bti
        bbi = fr * bti + fi * btr
        pws = [(jnp.ones_like(ar), jnp.zeros_like(ar))]
        for _ in range(S5_T):
            pr, pi = pws[-1]
            pws.append((pr * ar - pi * ai, pr * ai + pi * ar))
        for s in range(S5_T):
            pr, pi = pws[S5_T - 1 - s] if d == 0 else pws[s]
            ws_ref[d, gi, s * S5_GROUP:(s + 1) * S5_GROUP, :] = jnp.where(
                lane_b < S5_N, pr * bbr - pi * bbi, pr * bbi + pi * bbr).astype(BF16)

        acol = acol_ref[d, gi]
        arc = jnp.broadcast_to(acol[:, 0:1], (S5_N, S5_CW))
        aic = jnp.broadcast_to(acol[:, 1:2], (S5_N, S5_CW))
        ctr, cti = tile16(ctr_ref[d, gi]), tile16(cti_ref[d, gi])
        e_lag = blk_o if d == 0 else (S5_T - 1 - blk_o)
        pqr, pqi = _cpow(arc, aic, e_lag, 4)
        qr = pqr * ctr - pqi * cti
        qi = pqr * cti + pqi * ctr
        wo_ref[d, gi] = jnp.concatenate([qr * arc - qi * aic, -(qr * aic + qi * arc)], axis=0).astype(BF16)
        q_stack = jnp.concatenate([qr, qi], axis=0)
        bb_mix = jnp.where(lane_b < S5_N, bbr, -bbi)
        kt = _dot3(bb_mix, q_stack)
        for s in range(S5_T):
            if d == 0:
                blk = jnp.where(lane_k >= S5_GROUP * s, pltpu.roll(kt, S5_GROUP * s, 1), 0.0)
            else:
                blk = jnp.where(lane_k < S5_GROUP * (s + 1),
                                pltpu.roll(kt, (S5_GROUP * (s + 1)) % S5_CW, 1), 0.0)
            intra[s] = blk if intra[s] is None else intra[s] + blk

        pr1, pi1 = pws[S5_T]
        for k in range(6):
            ap_ref[d, gi, k:k + 1, :] = pr1
            ap_ref[d, gi, 8 + k:9 + k, :] = jnp.where(lane_a < S5_N, -pi1, pi1)
            pr1, pi1 = pr1 * pr1 - pi1 * pi1, 2.0 * pr1 * pi1
        ap_ref[d, gi, 6:8, :] = jnp.zeros((2, n2), F32)
        ap_ref[d, gi, 14:16, :] = jnp.zeros((2, n2), F32)
    for s in range(S5_T):
        wi_ref[gi, s * S5_GROUP:(s + 1) * S5_GROUP, :] = intra[s].astype(BF16)


def _s5_prep(a_re, a_im, log_step, b_re, b_im, c_re, c_im):
    g, n, n2 = S5_GROUPS, S5_N, 2 * S5_N
    abar = pl.pallas_call(
        _s5_abar_kernel,
        out_shape=jax.ShapeDtypeStruct((4, 2 * g, n), F32),
        grid=(1,),
        in_specs=[_const_spec((2 * g, n)), _const_spec((2 * g, n)), _const_spec((2 * g, 1))],
        out_specs=pl.BlockSpec((4, 2 * g, n), lambda i: (0, 0, 0)),
        compiler_params=_params(1),
        name="s5_abar",
    )(a_re.reshape(2 * g, n), a_im.reshape(2 * g, n), log_step.reshape(2 * g, 1))
    abar = jnp.concatenate([abar, abar], axis=-1).reshape(4, 2, g, n2)
    arow = abar.transpose(1, 2, 0, 3)
    acol = abar[:2, :, :, :n].transpose(1, 2, 3, 0)
    bt = lambda b: jnp.concatenate([jnp.swapaxes(b, 2, 3)] * 2, axis=-1)
    spec4 = lambda r, c: pl.BlockSpec((2, S5_PREP_GROUPS, r, c), lambda i: (0, i, 0, 0))
    return pl.pallas_call(
        _s5_prep_kernel,
        out_shape=[jax.ShapeDtypeStruct((g, S5_CW, S5_CW), BF16),
                   jax.ShapeDtypeStruct((2, g, S5_CW, n2), BF16),
                   jax.ShapeDtypeStruct((2, g, n2, S5_CW), BF16),
                   jax.ShapeDtypeStruct((2, g, 16, n2), F32)],
        grid=(g // S5_PREP_GROUPS,),
        in_specs=[spec4(4, n2), spec4(n, 2),
                  spec4(S5_GROUP, n2), spec4(S5_GROUP, n2), spec4(S5_GROUP, n), spec4(S5_GROUP, n)],
        out_specs=[pl.BlockSpec((S5_PREP_GROUPS, S5_CW, S5_CW), lambda i: (i, 0, 0)),
                   spec4(S5_CW, n2), spec4(n2, S5_CW), spec4(16, n2)],
        compiler_params=_params(1),
        name="s5_prep",
    )(arow, acol, bt(b_re), bt(b_im), c_re, c_im)


def _cmul_rows(x, p1, p2):
    return x * p1 + pltpu.roll(x, S5_N, 1) * p2


S5_OCT = LANES // S5_GROUP


def _block_transpose(xs):
    n = S5_OCT
    blk = lax.broadcasted_iota(jnp.int32, xs[0].shape, 1) >> 4
    a = [pltpu.roll(x, i * S5_GROUP, 1) if i else x for i, x in enumerate(xs)]
    ys = []
    for d in range(n):
        diag = a[-d % n]
        for b in range(1, n):
            diag = jnp.where(blk == b, a[(b - d) % n], diag)
        ys.append(pltpu.roll(diag, LANES - d * S5_GROUP, 1) if d else diag)
    return ys


def _tokens_to_chunks(u_ref, ug_ref):
    rows = u_ref.shape[1] // S5_T
    for octet in range(S5_GROUPS // S5_OCT):
        for half in range(2):
            xs = [u_ref[octet, pl.ds(S5_OCT * half + tt, rows, stride=S5_T), :] for tt in range(S5_OCT)]
            for gl, x in enumerate(_block_transpose(xs)):
                ug_ref[octet * S5_OCT + gl, :, half * LANES:(half + 1) * LANES] = x


def _chunks_to_tokens(yg_ref, y_ref):
    rows = y_ref.shape[1] // S5_T
    for octet in range(S5_GROUPS // S5_OCT):
        for half in range(2):
            ys = [yg_ref[octet * S5_OCT + gl, :, half * LANES:(half + 1) * LANES] for gl in range(S5_OCT)]
            for tt, y in enumerate(_block_transpose(ys)):
                y_ref[octet, pl.ds(S5_OCT * half + tt, rows, stride=S5_T), :] = y


def _s5_core_kernel(ug_ref, wi_ref, ws_ref, wo_ref, ap_ref, h0_ref, dv_ref, yg_ref, fin_ref, z_ref):
    n2 = 2 * S5_N
    r = lax.broadcasted_iota(jnp.int32, (S5_ROWS, n2), 0)
    in_p = r < S5_ROWS_P
    rib = jnp.where(in_p, r & (CH_P - 1), (r - S5_ROWS_P) & (CH_S - 1))
    nch = jnp.where(in_p, CH_P, CH_S)

    def one_group(gl, slot):
        ub = ug_ref[gl].astype(BF16)
        y = jnp.dot(ub, wi_ref[gl], preferred_element_type=F32)
        for d in range(2):
            p1, p2 = ap_ref[d, gl, 0:1, :], ap_ref[d, gl, 8:9, :]
            edge = [S5_ROWS_P + CH_S * b + (0 if d == 0 else CH_S - 1) for b in range(NB_S)]
            h0 = [h0_ref[gl, d, b:b + 1, :] for b in range(NB_S)]
            s = jnp.dot(ub, ws_ref[d, gl], preferred_element_type=F32)
            for b in range(NB_S):
                s = s + jnp.where(r == edge[b], _cmul_rows(h0[b], p1, p2), 0.0)
            def scan_step(x, k, pos, count):
                sh = 1 << k
                if d == 0:
                    t = jnp.where(pos >= sh, pltpu.roll(x, sh, 0), 0.0)
                else:
                    t = jnp.where(pos < count - sh, pltpu.roll(x, x.shape[0] - sh, 0), 0.0)
                return x + _cmul_rows(t, ap_ref[d, gl, k:k + 1, :], ap_ref[d, gl, 8 + k:9 + k, :])

            for k in range(CH_P.bit_length() - 1):
                s = scan_step(s, k, rib, nch)
            tail = s[S5_ROWS_P:]
            for k in range(CH_P.bit_length() - 1, CH_S.bit_length() - 1):
                tail = scan_step(tail, k, rib[S5_ROWS_P:], CH_S)
            s = jnp.concatenate([s[:S5_ROWS_P], tail], axis=0)
            z_ref[slot, d] = s
            first = CH_P - 1 if d == 0 else 0
            fin_ref[gl, d] = z_ref[slot, d, pl.ds(first, NB_P, stride=CH_P), :]
            if d == 0:
                sp = jnp.where(rib >= 1, pltpu.roll(s, 1, 0), 0.0)
            else:
                sp = jnp.where(rib < nch - 1, pltpu.roll(s, S5_ROWS - 1, 0), 0.0)
            for b in range(NB_S):
                sp = jnp.where(r == edge[b], h0[b], sp)
            y = y + jnp.dot(sp.astype(BF16), wo_ref[d, gl], preferred_element_type=F32)
        yg_ref[gl] = y + dv_ref[gl] * ug_ref[gl]

    def group_pair(gp, carry):
        for slot in range(2):
            one_group(2 * gp + slot, slot)
        return carry

    lax.fori_loop(0, S5_OCT // 2, group_pair, 0)


def _s5_core(ug, prep, h0, d_skip):
    w_intra, w_state, w_out, apow = prep
    g, n2 = S5_GROUPS, 2 * S5_N
    spec4 = lambda r, c: pl.BlockSpec((2, S5_OCT, r, c), lambda i: (0, i, 0, 0))
    chunks = pl.BlockSpec((S5_OCT, S5_ROWS, S5_CW), lambda i: (i, 0, 0))
    dvec = jnp.tile(d_skip.reshape(g, 1, S5_GROUP), (1, 1, S5_T))
    return pl.pallas_call(
        _s5_core_kernel,
        out_shape=[jax.ShapeDtypeStruct((g, S5_ROWS, S5_CW), F32),
                   jax.ShapeDtypeStruct((g, 2, NB_P, n2), F32)],
        grid=(g // S5_OCT,),
        in_specs=[chunks,
                  pl.BlockSpec((S5_OCT, S5_CW, S5_CW), lambda i: (i, 0, 0)),
                  spec4(S5_CW, n2), spec4(n2, S5_CW), spec4(16, n2),
                  pl.BlockSpec((S5_OCT, 2, 8, n2), lambda i: (i, 0, 0, 0)),
                  pl.BlockSpec((S5_OCT, 1, S5_CW), lambda i: (i, 0, 0))],
        out_specs=[chunks, pl.BlockSpec((S5_OCT, 2, NB_P, n2), lambda i: (i, 0, 0, 0))],
        scratch_shapes=[pltpu.VMEM((2, 2, S5_ROWS, n2), F32)],
        compiler_params=_params(1, VMEM_LIMIT),
        name="s5_scan",
    )(ug, w_intra, w_state, w_out, apow, h0, dvec)


DF_SCALE = DF_DH ** -0.5 * LOG2E
DFW = DF_HEADS * 2 * DF_DH
IN_B = 3 * HY_WIDTH + 2 * DFW + DF_HEADS * DF_V


def _inproj_b_kernel(y_ref, mod_ref, g_ref, win_ref, wkt_ref, wvt_ref, c_ref, s_ref,
                     hy_ref, q_ref, kp_ref, ks_ref, vp_ref, vs_ref, kc_ref, vc_ref, wbf_ref):
    @pl.when(pl.program_id(0) == 0)
    def _():
        wbf_ref[...] = win_ref[...].astype(BF16)

    mod = _mod_rows(mod_ref)
    h = _modulate(y_ref[...], g_ref[...], mod[3], mod[4]).astype(BF16)
    p = jnp.dot(h, wbf_ref[...], preferred_element_type=F32)
    o1 = 3 * HY_WIDTH
    hy_ref[...] = p[:, :o1]
    q_ref[...] = (_rope(p[:, o1:o1 + DFW], c_ref[...], s_ref[...]) * DF_SCALE).astype(BF16)
    _tok_write(kp_ref, ks_ref, _rope(p[:, o1 + DFW:o1 + 2 * DFW], c_ref[...], s_ref[...]).astype(BF16))
    _tok_write(vp_ref, vs_ref, p[:, o1 + 2 * DFW:].astype(BF16))

    @pl.when(pl.program_id(0) < NT_P)
    def _():
        for w_ref, cache_ref in ((wkt_ref, kc_ref), (wvt_ref, vc_ref)):
            t = _dot_nt(w_ref[...], h)
            for j in range(TM // L_P):
                cache_ref[j] = t[:, j * L_P:(j + 1) * L_P]


def _inproj_b(y, mods_l, g, w_in):
    cs, sn = _rope_tables(DFW, tuple(range(0, DFW, DF_DH)))
    pos = lambda i: (_pos_index(i), 0)
    kv_shapes, kv_specs = [], []
    for width in (DFW, DF_HEADS * DF_V):
        kv_shapes += [jax.ShapeDtypeStruct((TOK_P, width), BF16), jax.ShapeDtypeStruct((TOK_S, width), BF16)]
        kv_specs += _split_out(width)[1]
    seqs = TM // L_P
    cache_shapes = [jax.ShapeDtypeStruct((NB_P, w, L_P), F32) for w in (DFW, DF_HEADS * DF_V)]
    cache_specs = [pl.BlockSpec((seqs, w, L_P), lambda i: (jnp.minimum(i, NT_P - 1), 0, 0))
                   for w in (DFW, DF_HEADS * DF_V)]
    o1 = 3 * HY_WIDTH
    wkt = w_in[:, o1 + DFW:o1 + 2 * DFW].T.astype(BF16)
    wvt = w_in[:, o1 + 2 * DFW:].T.astype(BF16)
    hy_u, q, kp, ks, vp, vs, k_cache, v_cache = pl.pallas_call(
        _inproj_b_kernel,
        out_shape=[jax.ShapeDtypeStruct((TOK, 3 * HY_WIDTH), F32), jax.ShapeDtypeStruct((TOK, DFW), BF16)]
                  + kv_shapes + cache_shapes,
        grid=(NT,),
        in_specs=[pl.BlockSpec((TM, D), _row),
                  _mod_spec(mods_l[1]),
                  _const_spec((1, D)), _const_spec((D, IN_B)),
                  _const_spec((DFW, D)), _const_spec((DF_HEADS * DF_V, D)),
                  pl.BlockSpec((TM, DFW), pos), pl.BlockSpec((TM, DFW), pos)],
        out_specs=[pl.BlockSpec((TM, 3 * HY_WIDTH), _row), pl.BlockSpec((TM, DFW), _row)] + kv_specs + cache_specs,
        scratch_shapes=[pltpu.VMEM((D, IN_B), BF16)],
        compiler_params=_params(1, VMEM_LIMIT),
        name="inproj_odd",
    )(y, mods_l[0], g[None], w_in, wkt, wvt, jnp.asarray(cs), jnp.asarray(sn))
    return hy_u, q, (kp, ks), (vp, vs), k_cache, v_cache


def _diff_attn_kernel(nseg, nseq, lam_init, q_ref, lam_ref, sub_ref, *refs):
    o_ref = refs[-1]
    lp = lam_ref[...]
    lam = (jnp.exp(jnp.sum(lp[0:1] * lp[1:2], axis=-1, keepdims=True))
           - jnp.exp(jnp.sum(lp[2:3] * lp[3:4], axis=-1, keepdims=True)) + lam_init)
    tq = q_ref.shape[0] // nseq
    lane = lax.broadcasted_iota(jnp.int32, (tq, LANES), 1)
    for j in range(nseq):
        qr = slice(j * tq, (j + 1) * tq)
        krs = [slice(j * (refs[2 * s].shape[0] // nseq), (j + 1) * (refs[2 * s].shape[0] // nseq))
               for s in range(nseg)]
        for pair in range(DF_HEADS // 2):
            cs = slice(pair * LANES, (pair + 1) * LANES)
            q = q_ref[qr, cs]
            ks = [refs[2 * s][krs[s], cs].astype(BF16) for s in range(nseg)]
            vs = [refs[2 * s + 1][krs[s], cs].astype(BF16) for s in range(nseg)]
            outs = []
            for hh in range(2):
                parts = []
                for half in range(2):
                    unit = 2 * hh + half
                    qm = jnp.where((lane >> 5) == unit, q, jnp.zeros_like(q))
                    scores = [_dot_nt(qm, k) for k in ks]
                    parts.append(_softmax_pv(scores, vs, hh))
                o = parts[0] - lam * parts[1]
                mine = (lane >> 6) == hh
                ms = jnp.sum(jnp.where(mine, o * o, 0.0), axis=-1, keepdims=True) * (1.0 / DF_V)
                outs.append(o * lax.rsqrt(ms + EPS))
            o = jnp.where(lane < DF_V, outs[0], outs[1]) * sub_ref[...] * (1.0 - lam_init)
            o_ref[qr, cs] = o.astype(o_ref.dtype)


def _diff_attention(q, k, v, lam_p, subln, lam_init, n_batch, seq, tq, row0, ctx=None, nseq=1):
    qt = seq // tq
    qb0, kb0 = row0 // (nseq * tq), 0
    in_specs = [pl.BlockSpec((nseq * tq, DFW), lambda b, j: (qb0 + b * qt + j, 0)),
                pl.BlockSpec((4, DF_DH), lambda b, j: (0, 0)),
                pl.BlockSpec((1, LANES), lambda b, j: (0, 0))]
    args = [q, lam_p, jnp.concatenate([subln, subln])[None]]
    segs = []
    if ctx is not None:
        segs.append((ctx, PAST, 0))
    segs.append(((k, v), seq, kb0))
    for (a_k, a_v), ln, off in segs:
        idx = lambda b, j, off=off: (off + b, 0)
        in_specs += [pl.BlockSpec((nseq * ln, DFW), idx), pl.BlockSpec((nseq * ln, DF_HEADS * DF_V), idx)]
        args += [a_k, a_v]
    return pl.pallas_call(
        functools.partial(_diff_attn_kernel, len(segs), nseq, lam_init),
        out_shape=jax.ShapeDtypeStruct((n_batch * seq, DF_HEADS * DF_V), BF16),
        grid=(n_batch // nseq, qt),
        in_specs=in_specs,
        out_specs=pl.BlockSpec((nseq * tq, DF_HEADS * DF_V), lambda b, j: (b * qt + j, 0)),
        compiler_params=_params(2, VMEM_LIMIT),
        name="diff_attention",
    )(*args)


def _hy_filter_kernel(feat_ref, w1_ref, b1_ref, w2_ref, b2_ref, fq_ref, w3_ref, dec_ref, o_ref):
    feat = feat_ref[...]
    fq = fq_ref[...]
    h = jnp.sin(fq * (_dot3(feat, w1_ref[...]) + b1_ref[...]))
    h = jnp.sin(fq * (_dot3(h, w2_ref[...]) + b2_ref[...]))
    window = jnp.exp(-feat[:, 0:1] * jnp.abs(dec_ref[...]))
    for j in range(4):
        cs = slice(j * HY_WIDTH, (j + 1) * HY_WIDTH)
        o_ref[:, cs] = _dot3(h, w3_ref[:, cs]) * window


def _hy_spectrum_kernel(L, cs_ref, hf_ref, hb_ref, o_ref):
    row = lax.broadcasted_iota(jnp.int32, (L, HY_WIDTH), 0)
    first = row == 0
    tf = _dot(cs_ref[...], hf_ref[...])
    tb = _dot(cs_ref[...], jnp.where(first, 0.0, hb_ref[...]))
    ka = tf[:L] + tb[:L]
    kb = jnp.where(first, tf[L:] + tb[L:], tf[L:] - tb[L:])
    wv = jnp.where(first, 1.0 / (2 * L), 2.0 / (2 * L))
    o_ref[0, 0] = ka * wv
    o_ref[0, 1] = jnp.where(first, 0.0, kb) * wv
    o_ref[0, 2] = jnp.where(first, kb, ka) * wv


HY_CH = 256


def _hy_conv_kernel(L, cs_ref, ct_ref, kf_ref, v_ref, x1_ref, x2_ref,
                    wv_ref, w1_ref, w2_ref, bias_ref, o_ref):
    row = lax.broadcasted_iota(jnp.int32, (L, HY_CH), 0)

    def short(x, w):
        prev = jnp.where(row >= 1, pltpu.roll(x, 1, 0), 0.0)
        nxt = jnp.where(row <= L - 2, pltpu.roll(x, L - 1, 0), 0.0)
        return w[0:1] * prev + w[1:2] * x + w[2:3] * nxt

    for j in range(v_ref.shape[0] // L):
        rs = slice(j * L, (j + 1) * L)
        for k in range(HY_WIDTH // HY_CH):
            ch = slice(k * HY_CH, (k + 1) * HY_CH)
            z = short(v_ref[rs, ch], wv_ref[:, ch])
            gates = (short(x1_ref[rs, ch], w1_ref[:, ch]), short(x2_ref[rs, ch], w2_ref[:, ch]))
            for n in range(2):
                ab = _dot(cs_ref[...], z)
                a, b = ab[:L], ab[L:]
                ka, kb1, ka2 = kf_ref[n, 0, :, ch], kf_ref[n, 1, :, ch], kf_ref[n, 2, :, ch]
                pq = jnp.concatenate([a * ka - b * kb1, a * kb1 + b * ka2], axis=0)
                conv = _dot(ct_ref[...], pq)
                z = gates[n] * (conv + bias_ref[n:n + 1, ch] * z)
            o_ref[rs, ch] = z.astype(o_ref.dtype)


def _hyena_spectrum(L, phy):
    conv_w, w1, b1, w2, b2, freq, w3, decay, bias = phy
    feat = jnp.asarray(_hyena_features(L))
    w1p = jnp.pad(w1, ((0, LANES - HY_EMB), (0, 0)))
    filt = pl.pallas_call(
        _hy_filter_kernel,
        out_shape=jax.ShapeDtypeStruct((L, 4 * HY_WIDTH), F32),
        grid=(1,),
        in_specs=[_const_spec((L, LANES)), _const_spec((LANES, HY_FH)), _const_spec((1, HY_FH)),
                  _const_spec((HY_FH, HY_FH)), _const_spec((1, HY_FH)), _const_spec((1, HY_FH)),
                  _const_spec((HY_FH, 4 * HY_WIDTH)), _const_spec((1, HY_WIDTH))],
        out_specs=pl.BlockSpec((L, 4 * HY_WIDTH), lambda i: (0, 0)),
        compiler_params=_params(1, VMEM_LIMIT),
        name="hyena_filter",
    )(feat, w1p, b1[None], w2, b2[None], freq[None], w3, decay[None])
    cs = jnp.asarray(_dft_tables(L)[0]).astype(BF16)
    return pl.pallas_call(
        functools.partial(_hy_spectrum_kernel, L),
        out_shape=jax.ShapeDtypeStruct((2, 3, L, HY_WIDTH), F32),
        grid=(2,),
        in_specs=[_const_spec((2 * L, L)),
                  pl.BlockSpec((L, HY_WIDTH), lambda n: (0, n)),
                  pl.BlockSpec((L, HY_WIDTH), lambda n: (0, 2 + n))],
        out_specs=pl.BlockSpec((1, 3, L, HY_WIDTH), lambda n: (n, 0, 0, 0)),
        compiler_params=_params(1, VMEM_LIMIT),
        name="hyena_spectrum",
    )(cs, filt, filt)


def _hyena_conv(hy_u, spec, phy, n_batch, L, seqs, row0):
    conv_w, bias = phy[0], phy[8]
    cs, ct = (jnp.asarray(t).astype(BF16) for t in _dft_tables(L))
    rows = seqs * L
    rb0 = row0 // rows
    col = lambda off: (lambda b: (0, off))
    tok = lambda off: (lambda b: (rb0 + b, off))
    blk = lambda idx: pl.BlockSpec((rows, HY_WIDTH), idx)
    return pl.pallas_call(
        functools.partial(_hy_conv_kernel, L),
        out_shape=jax.ShapeDtypeStruct((n_batch * L, HY_WIDTH), BF16),
        grid=(n_batch // seqs,),
        in_specs=[_const_spec((2 * L, L)), _const_spec((L, 2 * L)), _const_spec((2, 3, L, HY_WIDTH)),
                  blk(tok(0)), blk(tok(1)), blk(tok(2)),
                  pl.BlockSpec((3, HY_WIDTH), col(0)), pl.BlockSpec((3, HY_WIDTH), col(1)),
                  pl.BlockSpec((3, HY_WIDTH), col(2)), _const_spec((2, HY_WIDTH))],
        out_specs=blk(lambda b: (b, 0)),
        compiler_params=_params(1, VMEM_LIMIT),
        name="hyena_conv",
    )(cs, ct, spec, hy_u, hy_u, hy_u, conv_w, conv_w, conv_w, bias)


def _even_mixer(y, mods_l, g, pa, ps5, ctx_ckv, ctx_krope, ctx_state):
    w_in, w_out, q_norm, w_uq, kv_norm, w_ukv = pa
    a_re, a_im, log_step, b_re, b_im, c_re, c_im, d_skip, w_glu = ps5
    q, ckv, kr_unrot, kr_rot, kn, v, ug, (w_k, w_v) = _inproj_a(y, mods_l, g, w_in, q_norm, w_uq, kv_norm, w_ukv)

    ctx_flat = ctx_ckv.reshape(NB_S * PAST, MLA_KV_RANK)
    ctx_kn = _linear(ctx_flat, w_k, PAST, BF16)
    ctx_v = _linear(ctx_flat, w_v, PAST, BF16)
    ctx_kr = jnp.pad(ctx_krope.reshape(NB_S * PAST, MLA_ROPE),
                     ((0, 0), (KR_AT, LANES - KR_AT - MLA_ROPE))).astype(BF16)
    att_p = _mla_attention(q, kn, kr_rot, v, NB_P, L_P, L_P, 0, nseq=2)
    att_s = _mla_attention(q, kn, kr_rot, v, NB_S, L_S, TM, TOK_P, ctx=(ctx_kn, ctx_kr, ctx_v))

    prep = _s5_prep(a_re, a_im, log_step, b_re, b_im, c_re, c_im)
    h0 = ctx_state.transpose(3, 1, 0, 2, 4).reshape(S5_GROUPS, 2, NB_S, 2 * S5_N)
    h0 = jnp.pad(h0, ((0, 0), (0, 0), (0, 8 - NB_S), (0, 0)))
    s5y, fin = _s5_core(ug, prep, h0, d_skip)

    mixer = ((att_p, att_s), s5y, w_out, w_glu)
    new_ckv = ckv.reshape(NB_P, L_P, MLA_KV_RANK)
    new_krope = kr_unrot.reshape(NB_P, L_P, MLA_ROPE)
    new_state = fin.reshape(S5_GROUPS, 2, NB_P, 2, S5_N).transpose(2, 1, 3, 0, 4)
    return mixer, new_ckv, new_krope, new_state


def _odd_mixer(y, mods_l, g, pb, phy, ctx_k, ctx_v, lam_init):
    w_in, w_out, lam_p, subln = pb
    hy_u, q, (k_p, k_s), (v_p, v_s), k_cache, v_cache = _inproj_b(y, mods_l, g, w_in)
    hy_p = _hyena_conv(hy_u, _hyena_spectrum(L_P, phy), phy, NB_P, L_P, 4, 0)
    hy_s = _hyena_conv(hy_u, _hyena_spectrum(L_S, phy), phy, NB_S, L_S, 1, TOK_P)
    ctx = (ctx_k.reshape(NB_S * PAST, DFW), ctx_v.reshape(NB_S * PAST, DF_HEADS * DF_V))
    att_p = _diff_attention(q, k_p, v_p, lam_p, subln, lam_init, NB_P, L_P, L_P, 0, nseq=2)
    att_s = _diff_attention(q, k_s, v_s, lam_p, subln, lam_init, NB_S, L_S, TM // 2, TOK_P, ctx=ctx)
    mixer = ((hy_p, hy_s), (att_p, att_s), w_out, None)
    new_k = k_cache.reshape(NB_P, DF_HEADS, 2, DF_DH, L_P).transpose(0, 4, 1, 2, 3)
    new_v = v_cache.reshape(NB_P, DF_HEADS, DF_V, L_P).transpose(0, 3, 1, 2)
    return mixer, new_k, new_v


def kernel(x_prompt, x_sample, c, c_ctx, cache_mla_ckv, cache_mla_krope, state_s5, cache_diff_k, cache_diff_v, ada_w, ada_b, norm_g, ff_w_in, ff_w_out, w_in_a, w_out_a, mla_q_norm, mla_w_uq, mla_kv_norm, mla_w_ukv, s5_a_re, s5_a_im, s5_log_step, s5_b_re, s5_b_im, s5_c_re, s5_c_im, s5_d, s5_w_glu, w_in_b, w_out_b, hy_conv, hy_w1, hy_b1, hy_w2, hy_b2, hy_freq, hy_w3, hy_decay, hy_bias, df_lambda, df_subln, final_norm):
    depth = ada_w.shape[0]
    y = (x_prompt.reshape(TOK_P, D), x_sample.reshape(TOK_S, D))
    mods = _adaln(jnp.concatenate([c_ctx[None], c], axis=0), ada_w, ada_b)
    new_ckv, new_krope, new_s5, new_dk, new_dv = [], [], [], [], []
    for l in range(depth):
        y = _half_ffn(y, (mods, l), norm_g[l, 0], ff_w_in, ff_w_out, l, 0)
        if l % 2 == 0:
            e = l // 2
            pa = (w_in_a[e], w_out_a[e], mla_q_norm[e], mla_w_uq[e], mla_kv_norm[e], mla_w_ukv[e])
            ps5 = (s5_a_re[e], s5_a_im[e], s5_log_step[e], s5_b_re[e], s5_b_im[e],
                   s5_c_re[e], s5_c_im[e], s5_d[e], s5_w_glu[e])
            mixer, ckv, krope, st = _even_mixer(y, (mods, l), norm_g[l, 1], pa, ps5, cache_mla_ckv[:, e],
                                            cache_mla_krope[:, e], state_s5[:, e])
            new_ckv.append(ckv)
            new_krope.append(krope)
            new_s5.append(st)
        else:
            o = l // 2
            lam_init = 0.8 - 0.6 * math.exp(-0.3 * l)
            pb = (w_in_b[o], w_out_b[o], df_lambda[o], df_subln[o])
            phy = (hy_conv[o], hy_w1[o], hy_b1[o], hy_w2[o], hy_b2[o], hy_freq[o],
                   hy_w3[o], hy_decay[o], hy_bias[o])
            mixer, dk, dv = _odd_mixer(y, (mods, l), norm_g[l, 1], pb, phy, cache_diff_k[:, o],
                                   cache_diff_v[:, o], lam_init)
            new_dk.append(dk)
            new_dv.append(dv)
        last = l == depth - 1
        y = _half_ffn(y, (mods, l), norm_g[l, 2], ff_w_in, ff_w_out, l, 1,
                      final_g=final_norm if last else None, mixer=mixer)
    y_prompt = y[0].reshape(NB_P, L_P, D)
    y_sample = y[1].reshape(NB_S, L_S, D)
    return (y_prompt, y_sample, jnp.stack(new_ckv, axis=1), jnp.stack(new_krope, axis=1),
            jnp.stack(new_s5, axis=1), jnp.stack(new_dk, axis=1), jnp.stack(new_dv, axis=1))
```

```python
import functools
import math

import numpy as np
import jax
import jax.numpy as jnp
from jax import lax
from jax.experimental import pallas as pl
from jax.experimental.pallas import tpu as pltpu

F32 = jnp.float32
BF16 = jnp.bfloat16

D = 1024
NB_P, L_P = 16, 256
NB_S, L_S = 2, 1024
PAST = 256
GRID_W = 64
N_MOD = 9
FF = 2816
EPS = 1e-6
ROPE_BASE = 10000.0

MLA_HEADS, MLA_NOPE, MLA_ROPE, MLA_V = 8, 64, 32, 64
MLA_Q_RANK, MLA_KV_RANK = 384, 256
S5_WIDTH, S5_GROUP, S5_N = 512, 16, 64
S5_GROUPS = S5_WIDTH // S5_GROUP
HY_WIDTH, HY_BANDS, HY_FH = 512, 16, 64
HY_EMB = 2 * HY_BANDS + 1
DF_HEADS, DF_DH = 8, 32
DF_V = 2 * DF_DH

TOK_P = NB_P * L_P
TOK_S = NB_S * L_S
TOK = TOK_P + TOK_S
TM = 512
NT = TOK // TM
NT_P = TOK_P // TM
TILES_PER_SAMPLE = L_S // TM

LANES = 128
S5_T = 16
S5_CW = S5_T * S5_GROUP
CH_P = L_P // S5_T
CH_S = L_S // S5_T
S5_ROWS = NB_P * CH_P + NB_S * CH_S
S5_ROWS_P = NB_P * CH_P

VMEM_LIMIT = 56 * 1024 * 1024


def _params(n_grid, vmem=None):
    return pltpu.CompilerParams(dimension_semantics=("arbitrary",) * n_grid,
                                vmem_limit_bytes=vmem)


def _const_spec(shape):
    nd = len(shape)
    return pl.BlockSpec(shape, lambda *_: (0,) * nd, pipeline_mode=pl.Buffered(1))


def _mod_index(i):
    return jnp.where(i < NT_P, 0, 1 + (i - NT_P) // TILES_PER_SAMPLE)


def _mod_spec(layer):
    return pl.BlockSpec((1, 8, N_MOD * D), lambda *_: (layer, 0, 0), pipeline_mode=pl.Buffered(1))


def _mod_rows(mod_ref):
    row = mod_ref[0, pl.ds(_mod_index(pl.program_id(0)), 1), :]
    return [row[:, k * D:(k + 1) * D] for k in range(N_MOD)]


def _pos_index(i):
    return jnp.where(i < NT_P, 0, 1 + (i - NT_P) % TILES_PER_SAMPLE)


def _row(i):
    return (i, 0)


def _row_p(i):
    return (jnp.minimum(i, NT_P - 1), 0)


def _row_s(i):
    return (jnp.maximum(i - NT_P, 0), 0)


def _tok_specs(x, width):
    if isinstance(x, tuple):
        return [pl.BlockSpec((TM, width), _row_p), pl.BlockSpec((TM, width), _row_s)], list(x)
    return [pl.BlockSpec((TM, width), _row)], [x]


def _tok_read(refs, split):
    if split:
        return jnp.where(pl.program_id(0) < NT_P, refs[0][...], refs[1][...]), refs[2:]
    return refs[0][...], refs[1:]


def _tok_write(p_ref, s_ref, value):
    i = pl.program_id(0)

    @pl.when(i < NT_P)
    def _():
        p_ref[...] = value

    @pl.when(i >= NT_P)
    def _():
        s_ref[...] = value.astype(s_ref.dtype)


def _split_out(width, sample_dtype=F32):
    shapes = [jax.ShapeDtypeStruct((TOK_P, width), F32), jax.ShapeDtypeStruct((TOK_S, width), sample_dtype)]
    specs = [pl.BlockSpec((TM, width), _row_p), pl.BlockSpec((TM, width), _row_s)]
    return shapes, specs


def _dot(a, b):
    return jnp.dot(a.astype(BF16), b.astype(BF16), preferred_element_type=F32)


def _dot_nt(a, b):
    return lax.dot_general(a, b, (((1,), (1,)), ((), ())), preferred_element_type=F32)


def _split(x):
    hi = x.astype(BF16)
    lo = (x - hi.astype(F32)).astype(BF16)
    return hi, lo


def _dot3(a, b):
    ah, al = _split(a)
    bh, bl = _split(b)
    d = functools.partial(jnp.dot, preferred_element_type=F32)
    return d(ah, bh) + d(ah, bl) + d(al, bh)


def _rmsnorm(x, g):
    return x * lax.rsqrt(jnp.mean(x * x, axis=-1, keepdims=True) + EPS) * g


def _modulate(y, g, shift, scale):
    return _rmsnorm(y, g) * (1.0 + scale) + shift


def _pair_swap(x):
    n = x.shape[-1]
    lane = lax.broadcasted_iota(jnp.int32, x.shape, x.ndim - 1)
    return jnp.where((lane & 1) == 0, pltpu.roll(x, n - 1, x.ndim - 1), pltpu.roll(x, 1, x.ndim - 1))


def _rope(x, cos, sin_signed):
    return x * cos + _pair_swap(x) * sin_signed


def _rope_angles():
    n_freq = MLA_ROPE // 4
    inv = 1.0 / (ROPE_BASE ** (np.arange(n_freq, dtype=np.float64) / n_freq))
    pos = np.arange(L_S)
    row = (pos // GRID_W).astype(np.float64)
    col = (pos % GRID_W).astype(np.float64)
    ang = np.concatenate([row[:, None] * inv, col[:, None] * inv], axis=-1)
    return np.cos(ang), np.sin(ang)


@functools.lru_cache(maxsize=None)
def _rope_tables(width, starts):
    cos, sin = _rope_angles()
    c = np.ones((TM + L_S, width), np.float32)
    s = np.zeros((TM + L_S, width), np.float32)
    sign = np.where(np.arange(MLA_ROPE) % 2 == 0, -1.0, 1.0)
    unit_c = np.repeat(cos, 2, axis=1)
    unit_s = np.repeat(sin, 2, axis=1) * sign
    for st in starts:
        c[TM:, st:st + MLA_ROPE] = unit_c
        s[TM:, st:st + MLA_ROPE] = unit_s
    return c, s


@functools.lru_cache(maxsize=None)
def _dft_tables(L):
    f = np.arange(L)[:, None]
    s = np.arange(L)[None, :]
    ang = np.pi * ((f * s) % (2 * L)).astype(np.float64) / L
    cs = np.concatenate([np.cos(ang), np.sin(ang)], axis=0)
    cs[L, :] = np.where(np.arange(L) % 2 == 0, 1.0, -1.0)
    cs = cs.astype(np.float32)
    return cs, np.ascontiguousarray(cs.T)


@functools.lru_cache(maxsize=None)
def _hyena_features(L):
    t = np.arange(L, dtype=np.float64) / L
    bands = np.arange(1, HY_BANDS + 1, dtype=np.float64)
    ang = 2.0 * math.pi * t[:, None] * bands
    feat = np.zeros((L, LANES), np.float32)
    feat[:, 0] = t
    feat[:, 1:1 + HY_BANDS] = np.cos(ang)
    feat[:, 1 + HY_BANDS:HY_EMB] = np.sin(ang)
    return feat


def _adaln_kernel(c_ref, w_ref, b_ref, o_ref):
    s = jax.nn.silu(c_ref[...])
    s_hi = s.astype(BF16).astype(F32)
    stacked = jnp.concatenate([s_hi, s - s_hi], axis=0).astype(BF16)
    wh, wl = _split(w_ref[0])
    both = jnp.dot(stacked, wh, preferred_element_type=F32)
    rows = c_ref.shape[0]
    o_ref[0] = both[:rows] + both[rows:] + jnp.dot(stacked, wl, preferred_element_type=F32)[:rows] + b_ref[0]


def _adaln(cvecs, ada_w, ada_b):
    depth = ada_w.shape[0]
    n_vec = cvecs.shape[0]
    tn = N_MOD * D // 4
    out = pl.pallas_call(
        _adaln_kernel,
        out_shape=jax.ShapeDtypeStruct((depth, 8, N_MOD * D), F32),
        grid=(depth, N_MOD * D // tn),
        in_specs=[pl.BlockSpec((8, D), lambda l, j: (0, 0)),
                  pl.BlockSpec((1, D, tn), lambda l, j: (l, 0, j)),
                  pl.BlockSpec((1, 1, tn), lambda l, j: (l, 0, j))],
        out_specs=pl.BlockSpec((1, 8, tn), lambda l, j: (l, 0, j)),
        compiler_params=_params(2, VMEM_LIMIT),
        name="adaln",
    )(jnp.pad(cvecs, ((0, 8 - n_vec), (0, 0))), ada_w, ada_b[:, None, :])
    return out


FF_PIECE = 256
FF_LOADS = FF // FF_PIECE


def _ffn_kernel(base, final, split_in, mixer, layer, which, *refs):
    y, refs = _tok_read(refs, split_in)
    if mixer is not None:
        a1, refs = _tok_read(refs, mixer[0])
        if mixer[2]:
            a2_chunks, refs = refs[0], refs[1:]
        else:
            a2, refs = _tok_read(refs, mixer[1])
        wmix_ref, wg_ref = refs[:2]
        refs = refs[2:]
    mod_ref, g_ref, win_hbm, wout_hbm, fg_ref = refs[:5]
    n_out = 2 if final else 1
    outs = refs[5:5 + n_out]
    win_ref, wout_ref, stage_g, stage_u, stage_o, sems = refs[5 + n_out:11 + n_out]
    mod = _mod_rows(mod_ref)
    if mixer is not None:
        if mixer[2]:
            a2_scr = refs[11 + n_out]
            _chunks_to_tokens(a2_chunks, a2_scr)
            a2 = jax.nn.gelu(jnp.concatenate([a2_scr[o] for o in range(a2_scr.shape[0])], axis=1))
            a2 = a2 * jax.nn.sigmoid(_dot(a2, wg_ref[...]))
        k1 = wmix_ref.shape[0] // 2
        y = y + mod[5] * (_dot(a1, wmix_ref[:k1]) + _dot(a2, wmix_ref[k1:]))
    h = _modulate(y, g_ref[...], mod[base], mod[base + 1]).astype(BF16)

    def hidden(lo, width):
        gate = jnp.dot(h, win_ref[:, lo:lo + width], preferred_element_type=F32)
        up = jnp.dot(h, win_ref[:, FF + lo:FF + lo + width], preferred_element_type=F32)
        a = (jax.nn.silu(gate) * up).astype(BF16)
        return jnp.dot(a, wout_ref[lo:lo + width, :], preferred_element_type=F32)

    def finish(acc):
        out = y + 0.5 * mod[base + 2] * acc
        if final:
            _tok_write(outs[0], outs[1], _rmsnorm(out, fg_ref[...]))
        else:
            outs[0][...] = out

    @pl.when(pl.program_id(0) == 0)
    def _():
        def copies(c, slot):
            cols = pl.ds(c * FF_PIECE, FF_PIECE)
            return (pltpu.make_async_copy(win_hbm.at[layer, which, :, cols], stage_g.at[slot], sems.at[0, slot]),
                    pltpu.make_async_copy(win_hbm.at[layer, which, :, pl.ds(FF + c * FF_PIECE, FF_PIECE)],
                                          stage_u.at[slot], sems.at[1, slot]),
                    pltpu.make_async_copy(wout_hbm.at[layer, which, cols, :], stage_o.at[slot], sems.at[2, slot]))

        for cp in copies(0, 0):
            cp.start()
        acc = jnp.zeros(y.shape, F32)
        for c in range(FF_LOADS):
            slot = c % 2
            lo = c * FF_PIECE
            if c + 1 < FF_LOADS:
                for cp in copies(c + 1, 1 - slot):
                    cp.start()
            for cp in copies(c, slot):
                cp.wait()
            win_ref[:, lo:lo + FF_PIECE] = stage_g[slot].astype(BF16)
            win_ref[:, FF + lo:FF + lo + FF_PIECE] = stage_u[slot].astype(BF16)
            wout_ref[lo:lo + FF_PIECE, :] = stage_o[slot].astype(BF16)
            acc = acc + hidden(lo, FF_PIECE)
        finish(acc)

    @pl.when(pl.program_id(0) > 0)
    def _():
        finish(hidden(0, FF))


def _half_ffn(y, mods_l, g, ff_w_in, ff_w_out, layer, which, final_g=None, mixer=None):
    final = final_g is not None
    fg = final_g if final else g
    y_specs, y_args = _tok_specs(y, D)
    mix_flags = None
    extra_scratch = []
    if mixer is not None:
        a1, a2, w_out, w_glu = mixer
        k1 = w_out.shape[0] // 2
        wg = w_glu if w_glu is not None else jnp.zeros((8, LANES), F32)
        s1, a1_args = _tok_specs(a1, k1)
        if w_glu is not None:
            s2 = [pl.BlockSpec((S5_GROUPS, TM // S5_T, S5_CW), lambda i: (0, i, 0))]
            a2_args = [a2]
            extra_scratch = [pltpu.VMEM((k1 // LANES, TM, LANES), F32)]
        else:
            s2, a2_args = _tok_specs(a2, k1)
        y_specs = y_specs + s1 + s2 + [_const_spec(w_out.shape), _const_spec(wg.shape)]
        y_args = y_args + a1_args + a2_args + [w_out, wg]
        mix_flags = (isinstance(a1, tuple), isinstance(a2, tuple), w_glu is not None)
    if final:
        out_shape, out_specs = _split_out(D)
    else:
        out_shape, out_specs = jax.ShapeDtypeStruct((TOK, D), F32), pl.BlockSpec((TM, D), _row)
    return pl.pallas_call(
        functools.partial(_ffn_kernel, 6 * which, final, isinstance(y, tuple), mix_flags, layer, which),
        out_shape=out_shape,
        grid=(NT,),
        in_specs=y_specs + [_mod_spec(mods_l[1]),
                            _const_spec((1, D)),
                            pl.BlockSpec(memory_space=pl.ANY),
                            pl.BlockSpec(memory_space=pl.ANY),
                            _const_spec((1, D))],
        out_specs=out_specs,
        scratch_shapes=[pltpu.VMEM((D, 2 * FF), BF16), pltpu.VMEM((FF, D), BF16),
                        pltpu.VMEM((2, D, FF_PIECE), F32), pltpu.VMEM((2, D, FF_PIECE), F32),
                        pltpu.VMEM((2, FF_PIECE, D), F32), pltpu.SemaphoreType.DMA((3, 2))] + extra_scratch,
        compiler_params=_params(1, VMEM_LIMIT),
        name="half_ffn",
    )(*y_args, mods_l[0], g[None], ff_w_in, ff_w_out, fg[None])


def _linear_kernel(x_ref, w_ref, o_ref):
    o_ref[...] = _dot(x_ref[...], w_ref[...]).astype(o_ref.dtype)


def _linear(x, w, tm, out_dtype):
    m, k = x.shape
    n = w.shape[1]
    return pl.pallas_call(
        _linear_kernel,
        out_shape=jax.ShapeDtypeStruct((m, n), out_dtype),
        grid=(m // tm,),
        in_specs=[pl.BlockSpec((tm, k), lambda i: (i, 0)), _const_spec((k, n))],
        out_specs=pl.BlockSpec((tm, n), lambda i: (i, 0)),
        compiler_params=_params(1),
        name="linear",
    )(x, w.astype(BF16))


LOG2E = math.log2(math.e)
MLA_SCALE = (MLA_NOPE + MLA_ROPE) ** -0.5 * LOG2E
QW = MLA_HEADS * LANES
KR_AT = MLA_NOPE
IN_A_PAD = MLA_Q_RANK + MLA_KV_RANK + S5_WIDTH + LANES


def _inproj_a_kernel(y_ref, mod_ref, g_ref, win_ref, qn_ref, wuq_ref, kvn_ref, wk_ref, wv_ref,
                     cq_ref, sq_ref, ck_ref, sk_ref,
                     q_ref, ckv_ref, kru_ref, krr_ref, kn_ref, v_ref, ug_ref, u_scr):
    mod = _mod_rows(mod_ref)
    h = _modulate(y_ref[...], g_ref[...], mod[3], mod[4]).astype(BF16)
    p = jnp.dot(h, win_ref[...], preferred_element_type=F32)
    o1 = MLA_Q_RANK
    o2 = o1 + MLA_KV_RANK
    o3 = o2 + S5_WIDTH
    q = _dot(_rmsnorm(p[:, :o1], qn_ref[...]), wuq_ref[...])
    q_ref[...] = (_rope(q, cq_ref[...], sq_ref[...]) * MLA_SCALE).astype(BF16)
    ckv = _rmsnorm(p[:, o1:o2], kvn_ref[...])
    ckv_b = ckv.astype(BF16)
    kn_ref[...] = jnp.dot(ckv_b, wk_ref[...], preferred_element_type=F32).astype(BF16)
    v_ref[...] = jnp.dot(ckv_b, wv_ref[...], preferred_element_type=F32).astype(BF16)
    for octet in range(S5_WIDTH // LANES):
        u_scr[octet] = p[:, o2 + octet * LANES:o2 + (octet + 1) * LANES]
    _tokens_to_chunks(u_scr, ug_ref)
    krp = p[:, o3:]
    krr_ref[...] = _rope(krp, ck_ref[...], sk_ref[...]).astype(BF16)

    @pl.when(pl.program_id(0) < NT_P)
    def _():
        ckv_ref[...] = ckv
        kru_ref[...] = krp[:, KR_AT:KR_AT + MLA_ROPE]


def _inproj_a(y, mods_l, g, w_in, q_norm, w_uq, kv_norm, w_ukv):
    o1 = MLA_Q_RANK
    o2 = o1 + MLA_KV_RANK
    o3 = o2 + MLA_ROPE
    kr_cols = jnp.pad(w_in[:, o2:o3], ((0, 0), (KR_AT, LANES - KR_AT - MLA_ROPE)))
    w_ext = jnp.concatenate([w_in[:, :o2], w_in[:, o3:], kr_cols], axis=1).astype(BF16)
    dq = MLA_NOPE + MLA_ROPE
    w_uq_pad = jnp.pad(w_uq.reshape(MLA_Q_RANK, MLA_HEADS, dq),
                       ((0, 0), (0, 0), (0, LANES - dq))).reshape(MLA_Q_RANK, QW).astype(BF16)
    w_kv = w_ukv.reshape(MLA_KV_RANK, MLA_HEADS, MLA_NOPE + MLA_V)
    w_k = jnp.pad(w_kv[:, :, :MLA_NOPE], ((0, 0), (0, 0), (0, LANES - MLA_NOPE))).reshape(MLA_KV_RANK, QW)
    w_v = w_kv[:, :, MLA_NOPE:].reshape(MLA_KV_RANK, MLA_HEADS * MLA_V)
    w_k, w_v = w_k.astype(BF16), w_v.astype(BF16)
    cq, sq = _rope_tables(QW, tuple(h * LANES + MLA_NOPE for h in range(MLA_HEADS)))
    ck, sk = _rope_tables(LANES, (KR_AT,))
    row = _row
    pos = lambda i: (_pos_index(i), 0)
    widths = (QW, MLA_KV_RANK, MLA_ROPE, LANES, QW, MLA_HEADS * MLA_V)
    prompt_only = (1, 2)
    mxu_only = (0, 3, 4, 5)
    outs = pl.pallas_call(
        _inproj_a_kernel,
        out_shape=[jax.ShapeDtypeStruct((TOK_P if k in prompt_only else TOK, w), BF16 if k in mxu_only else F32)
                   for k, w in enumerate(widths)]
                  + [jax.ShapeDtypeStruct((S5_GROUPS, S5_ROWS, S5_CW), F32)],
        grid=(NT,),
        in_specs=[pl.BlockSpec((TM, D), row),
                  _mod_spec(mods_l[1]),
                  _const_spec((1, D)),
                  _const_spec((D, IN_A_PAD)),
                  _const_spec((1, MLA_Q_RANK)),
                  _const_spec((MLA_Q_RANK, QW)),
                  _const_spec((1, MLA_KV_RANK)),
                  _const_spec((MLA_KV_RANK, QW)),
                  _const_spec((MLA_KV_RANK, MLA_HEADS * MLA_V)),
                  pl.BlockSpec((TM, QW), pos), pl.BlockSpec((TM, QW), pos),
                  pl.BlockSpec((TM, LANES), pos), pl.BlockSpec((TM, LANES), pos)],
        out_specs=[pl.BlockSpec((TM, w), _row_p if k in prompt_only else row)
                   for k, w in enumerate(widths)]
                  + [pl.BlockSpec((S5_GROUPS, TM // S5_T, S5_CW), lambda i: (0, i, 0))],
        scratch_shapes=[pltpu.VMEM((S5_WIDTH // LANES, TM, LANES), F32)],
        compiler_params=_params(1, VMEM_LIMIT),
        name="inproj_even",
    )(y, mods_l[0], g[None], w_ext, q_norm[None], w_uq_pad, kv_norm[None], w_k, w_v,
      jnp.asarray(cq), jnp.asarray(sq), jnp.asarray(ck), jnp.asarray(sk))
    q, ckv, kr_unrot, kr_rot, kn, v, ug = outs
    return q, ckv, kr_unrot, kr_rot, kn, v, ug, (w_k, w_v)


def _softmax_pv(scores, vals, half):
    m = functools.reduce(jnp.maximum, [jnp.max(s, axis=-1, keepdims=True) for s in scores])
    pv = None
    for s, v in zip(scores, vals):
        own_k = (lax.broadcasted_iota(jnp.int32, v.shape, 1) >> 6) == half
        part = jnp.dot(jnp.exp2(s - m).astype(BF16), jnp.where(own_k, v, jnp.ones_like(v)),
                       preferred_element_type=F32)
        pv = part if pv is None else pv + part
    own = (lax.broadcasted_iota(jnp.int32, pv.shape, 1) >> 6) == half
    denom = jnp.max(jnp.where(own, 0.0, pv), axis=-1, keepdims=True)
    return pv * (1.0 / denom)


def _mla_attn_kernel(nseg, nseq, q_ref, *refs):
    o_ref = refs[-1]
    tq = q_ref.shape[0] // nseq
    lane = lax.broadcasted_iota(jnp.int32, (tq, LANES), 1)
    for j in range(nseq):
        qr = slice(j * tq, (j + 1) * tq)
        krs = [slice(j * (refs[3 * s].shape[0] // nseq), (j + 1) * (refs[3 * s].shape[0] // nseq))
               for s in range(nseg)]
        for pair in range(MLA_HEADS // 2):
            outs = []
            for hh in range(2):
                h = 2 * pair + hh
                hs = slice(h * LANES, (h + 1) * LANES)
                qh = q_ref[qr, hs]
                scores = []
                for s in range(nseg):
                    kn_ref, kr_ref = refs[3 * s], refs[3 * s + 1]
                    kh = (kn_ref[krs[s], hs] + kr_ref[krs[s], :]).astype(BF16)
                    scores.append(_dot_nt(qh, kh))
                vals = [refs[3 * s + 2][krs[s], pair * LANES:(pair + 1) * LANES] for s in range(nseg)]
                outs.append(_softmax_pv(scores, vals, hh))
            o_ref[qr, pair * LANES:(pair + 1) * LANES] = jnp.where(lane < MLA_V, outs[0], outs[1]).astype(o_ref.dtype)


def _mla_attention(q, kn, kr, v, n_batch, seq, tq, row0, ctx=None, nseq=1):
    qt = seq // tq
    qb0, kb0 = row0 // (nseq * tq), row0 // (nseq * seq)
    in_specs = [pl.BlockSpec((nseq * tq, QW), lambda b, j: (qb0 + b * qt + j, 0))]
    args = [q]
    segs = []
    if ctx is not None:
        segs.append((ctx, PAST, 0))
    segs.append(((kn, kr, v), seq, kb0))
    for (a_kn, a_kr, a_v), ln, off in segs:
        idx = lambda b, j, off=off: (off + b, 0)
        in_specs += [pl.BlockSpec((nseq * ln, QW), idx), pl.BlockSpec((nseq * ln, LANES), idx),
                     pl.BlockSpec((nseq * ln, MLA_HEADS * MLA_V), idx)]
        args += [a_kn, a_kr, a_v]
    return pl.pallas_call(
        functools.partial(_mla_attn_kernel, len(segs), nseq),
        out_shape=jax.ShapeDtypeStruct((n_batch * seq, MLA_HEADS * MLA_V), BF16),
        grid=(n_batch // nseq, qt),
        in_specs=in_specs,
        out_specs=pl.BlockSpec((nseq * tq, MLA_HEADS * MLA_V), lambda b, j: (b * qt + j, 0)),
        compiler_params=_params(2, VMEM_LIMIT),
        name="mla_attention",
    )(*args)


def _cpow(ar, ai, e, nbits):
    rr = jnp.ones_like(ar)
    ri = jnp.zeros_like(ar)
    br, bi = ar, ai
    for k in range(nbits):
        bit = ((e >> k) & 1) == 1
        nr = rr * br - ri * bi
        ni = rr * bi + ri * br
        rr = jnp.where(bit, nr, rr)
        ri = jnp.where(bit, ni, ri)
        if k + 1 < nbits:
            br, bi = br * br - bi * bi, 2.0 * br * bi
    return rr, ri


def _s5_abar_kernel(lr_ref, li_ref, ls_ref, o_ref):
    step = jnp.exp(ls_ref[...])
    lr = jnp.minimum(lr_ref[...], -1e-4)
    li = li_ref[...]
    mag = jnp.exp(lr * step)
    ar = mag * jnp.cos(li * step)
    ai = mag * jnp.sin(li * step)
    den = lr * lr + li * li
    o_ref[0] = ar
    o_ref[1] = ai
    o_ref[2] = ((ar - 1.0) * lr + ai * li) / den
    o_ref[3] = (ai * lr - (ar - 1.0) * li) / den


S5_PREP_GROUPS = 4


def _s5_prep_kernel(*refs):
    for gi in range(S5_PREP_GROUPS):
        _s5_prep_group(gi, *refs)


def _s5_prep_group(gi, arow_ref, acol_ref, btr_ref, bti_ref, ctr_ref, cti_ref,
                   wi_ref, ws_ref, wo_ref, ap_ref):
    n2 = 2 * S5_N
    blk_o = lax.broadcasted_iota(jnp.int32, (S5_N, S5_CW), 1) >> 4
    lane_k = lax.broadcasted_iota(jnp.int32, (S5_GROUP, S5_CW), 1)
    row_k = lax.broadcasted_iota(jnp.int32, (S5_GROUP, S5_CW), 0)
    lane_b = lax.broadcasted_iota(jnp.int32, (S5_GROUP, n2), 1)
    lane_a = lax.broadcasted_iota(jnp.int32, (1, n2), 1)
    rep = ((lane_k & (S5_GROUP - 1)) == row_k).astype(BF16)

    def tile16(x):
        hi = x.astype(BF16)
        r1 = x - hi.astype(F32)
        mid = r1.astype(BF16)
        lo = (r1 - mid.astype(F32)).astype(BF16)
        d = lambda a: lax.dot_general(a, rep, (((0,), (0,)), ((), ())), preferred_element_type=F32)
        return d(hi) + d(mid) + d(lo)

    intra = [None] * S5_T
    for d in range(2):
        ar, ai, fr, fi = (arow_ref[d, gi, k:k + 1, :] for k in range(4))
        btr, bti = btr_ref[d, gi], bti_ref[d, gi]
        bbr = fr * btr - fi * bti
        bbi = fr * bti + fi * btr
        pws = [(jnp.ones_like(ar), jnp.zeros_like(ar))]
        for _ in range(S5_T):
            pr, pi = pws[-1]
            pws.append((pr * ar - pi * ai, pr * ai + pi * ar))
        for s in range(S5_T):
            pr, pi = pws[S5_T - 1 - s] if d == 0 else pws[s]
            ws_ref[d, gi, s * S5_GROUP:(s + 1) * S5_GROUP, :] = jnp.where(
                lane_b < S5_N, pr * bbr - pi * bbi, pr * bbi + pi * bbr).astype(BF16)

        acol = acol_ref[d, gi]
        arc = jnp.broadcast_to(acol[:, 0:1], (S5_N, S5_CW))
        aic = jnp.broadcast_to(acol[:, 1:2], (S5_N, S5_CW))
        ctr, cti = tile16(ctr_ref[d, gi]), tile16(cti_ref[d, gi])
        e_lag = blk_o if d == 0 else (S5_T - 1 - blk_o)
        pqr, pqi = _cpow(arc, aic, e_lag, 4)
        qr = pqr * ctr - pqi * cti
        qi = pqr * cti + pqi * ctr
        wo_ref[d, gi] = jnp.concatenate([qr * arc - qi * aic, -(qr * aic + qi * arc)], axis=0).astype(BF16)
        q_stack = jnp.concatenate([qr, qi], axis=0)
        bb_mix = jnp.where(lane_b < S5_N, bbr, -bbi)
        kt = _dot3(bb_mix, q_stack)
        for s in range(S5_T):
            if d == 0:
                blk = jnp.where(lane_k >= S5_GROUP * s, pltpu.roll(kt, S5_GROUP * s, 1), 0.0)
            else:
                blk = jnp.where(lane_k < S5_GROUP * (s + 1),
                                pltpu.roll(kt, (S5_GROUP * (s + 1)) % S5_CW, 1), 0.0)
            intra[s] = blk if intra[s] is None else intra[s] + blk

        pr1, pi1 = pws[S5_T]
        for k in range(6):
            ap_ref[d, gi, k:k + 1, :] = pr1
            ap_ref[d, gi, 8 + k:9 + k, :] = jnp.where(lane_a < S5_N, -pi1, pi1)
            pr1, pi1 = pr1 * pr1 - pi1 * pi1, 2.0 * pr1 * pi1
        ap_ref[d, gi, 6:8, :] = jnp.zeros((2, n2), F32)
        ap_ref[d, gi, 14:16, :] = jnp.zeros((2, n2), F32)
    for s in range(S5_T):
        wi_ref[gi, s * S5_GROUP:(s + 1) * S5_GROUP, :] = intra[s].astype(BF16)


def _s5_prep(a_re, a_im, log_step, b_re, b_im, c_re, c_im):
    g, n, n2 = S5_GROUPS, S5_N, 2 * S5_N
    abar = pl.pallas_call(
        _s5_abar_kernel,
        out_shape=jax.ShapeDtypeStruct((4, 2 * g, n), F32),
        grid=(1,),
        in_specs=[_const_spec((2 * g, n)), _const_spec((2 * g, n)), _const_spec((2 * g, 1))],
        out_specs=pl.BlockSpec((4, 2 * g, n), lambda i: (0, 0, 0)),
        compiler_params=_params(1),
        name="s5_abar",
    )(a_re.reshape(2 * g, n), a_im.reshape(2 * g, n), log_step.reshape(2 * g, 1))
    abar = jnp.concatenate([abar, abar], axis=-1).reshape(4, 2, g, n2)
    arow = abar.transpose(1, 2, 0, 3)
    acol = abar[:2, :, :, :n].transpose(1, 2, 3, 0)
    bt = lambda b: jnp.concatenate([jnp.swapaxes(b, 2, 3)] * 2, axis=-1)
    spec4 = lambda r, c: pl.BlockSpec((2, S5_PREP_GROUPS, r, c), lambda i: (0, i, 0, 0))
    return pl.pallas_call(
        _s5_prep_kernel,
        out_shape=[jax.ShapeDtypeStruct((g, S5_CW, S5_CW), BF16),
                   jax.ShapeDtypeStruct((2, g, S5_CW, n2), BF16),
                   jax.ShapeDtypeStruct((2, g, n2, S5_CW), BF16),
                   jax.ShapeDtypeStruct((2, g, 16, n2), F32)],
        grid=(g // S5_PREP_GROUPS,),
        in_specs=[spec4(4, n2), spec4(n, 2),
                  spec4(S5_GROUP, n2), spec4(S5_GROUP, n2), spec4(S5_GROUP, n), spec4(S5_GROUP, n)],
        out_specs=[pl.BlockSpec((S5_PREP_GROUPS, S5_CW, S5_CW), lambda i: (i, 0, 0)),
                   spec4(S5_CW, n2), spec4(n2, S5_CW), spec4(16, n2)],
        compiler_params=_params(1),
        name="s5_prep",
    )(arow, acol, bt(b_re), bt(b_im), c_re, c_im)


def _cmul_rows(x, p1, p2):
    return x * p1 + pltpu.roll(x, S5_N, 1) * p2


S5_OCT = LANES // S5_GROUP


def _block_transpose(xs):
    n = S5_OCT
    blk = lax.broadcasted_iota(jnp.int32, xs[0].shape, 1) >> 4
    a = [pltpu.roll(x, i * S5_GROUP, 1) if i else x for i, x in enumerate(xs)]
    ys = []
    for d in range(n):
        diag = a[-d % n]
        for b in range(1, n):
            diag = jnp.where(blk == b, a[(b - d) % n], diag)
        ys.append(pltpu.roll(diag, LANES - d * S5_GROUP, 1) if d else diag)
    return ys


def _tokens_to_chunks(u_ref, ug_ref):
    rows = u_ref.shape[1] // S5_T
    for octet in range(S5_GROUPS // S5_OCT):
        for half in range(2):
            xs = [u_ref[octet, pl.ds(S5_OCT * half + tt, rows, stride=S5_T), :] for tt in range(S5_OCT)]
            for gl, x in enumerate(_block_transpose(xs)):
                ug_ref[octet * S5_OCT + gl, :, half * LANES:(half + 1) * LANES] = x


def _chunks_to_tokens(yg_ref, y_ref):
    rows = y_ref.shape[1] // S5_T
    for octet in range(S5_GROUPS // S5_OCT):
        for half in range(2):
            ys = [yg_ref[octet * S5_OCT + gl, :, half * LANES:(half + 1) * LANES] for gl in range(S5_OCT)]
            for tt, y in enumerate(_block_transpose(ys)):
                y_ref[octet, pl.ds(S5_OCT * half + tt, rows, stride=S5_T), :] = y


def _s5_core_kernel(ug_ref, wi_ref, ws_ref, wo_ref, ap_ref, h0_ref, dv_ref, yg_ref, fin_ref, z_ref):
    n2 = 2 * S5_N
    r = lax.broadcasted_iota(jnp.int32, (S5_ROWS, n2), 0)
    in_p = r < S5_ROWS_P
    rib = jnp.where(in_p, r & (CH_P - 1), (r - S5_ROWS_P) & (CH_S - 1))
    nch = jnp.where(in_p, CH_P, CH_S)

    def one_group(gl, slot):
        ub = ug_ref[gl].astype(BF16)
        y = jnp.dot(ub, wi_ref[gl], preferred_element_type=F32)
        for d in range(2):
            p1, p2 = ap_ref[d, gl, 0:1, :], ap_ref[d, gl, 8:9, :]
            edge = [S5_ROWS_P + CH_S * b + (0 if d == 0 else CH_S - 1) for b in range(NB_S)]
            h0 = [h0_ref[gl, d, b:b + 1, :] for b in range(NB_S)]
            s = jnp.dot(ub, ws_ref[d, gl], preferred_element_type=F32)
            for b in range(NB_S):
                s = s + jnp.where(r == edge[b], _cmul_rows(h0[b], p1, p2), 0.0)
            def scan_step(x, k, pos, count):
                sh = 1 << k
                if d == 0:
                    t = jnp.where(pos >= sh, pltpu.roll(x, sh, 0), 0.0)
                else:
                    t = jnp.where(pos < count - sh, pltpu.roll(x, x.shape[0] - sh, 0), 0.0)
                return x + _cmul_rows(t, ap_ref[d, gl, k:k + 1, :], ap_ref[d, gl, 8 + k:9 + k, :])

            for k in range(CH_P.bit_length() - 1):
                s = scan_step(s, k, rib, nch)
            tail = s[S5_ROWS_P:]
            for k in range(CH_P.bit_length() - 1, CH_S.bit_length() - 1):
                tail = scan_step(tail, k, rib[S5_ROWS_P:], CH_S)
            s = jnp.concatenate([s[:S5_ROWS_P], tail], axis=0)
            z_ref[slot, d] = s
            first = CH_P - 1 if d == 0 else 0
            fin_ref[gl, d] = z_ref[slot, d, pl.ds(first, NB_P, stride=CH_P), :]
            if d == 0:
                sp = jnp.where(rib >= 1, pltpu.roll(s, 1, 0), 0.0)
            else:
                sp = jnp.where(rib < nch - 1, pltpu.roll(s, S5_ROWS - 1, 0), 0.0)
            for b in range(NB_S):
                sp = jnp.where(r == edge[b], h0[b], sp)
            y = y + jnp.dot(sp.astype(BF16), wo_ref[d, gl], preferred_element_type=F32)
        yg_ref[gl] = y + dv_ref[gl] * ug_ref[gl]

    def group_pair(gp, carry):
        for slot in range(2):
            one_group(2 * gp + slot, slot)
        return carry

    lax.fori_loop(0, S5_OCT // 2, group_pair, 0)


def _s5_core(ug, prep, h0, d_skip):
    w_intra, w_state, w_out, apow = prep
    g, n2 = S5_GROUPS, 2 * S5_N
    spec4 = lambda r, c: pl.BlockSpec((2, S5_OCT, r, c), lambda i: (0, i, 0, 0))
    chunks = pl.BlockSpec((S5_OCT, S5_ROWS, S5_CW), lambda i: (i, 0, 0))
    dvec = jnp.tile(d_skip.reshape(g, 1, S5_GROUP), (1, 1, S5_T))
    return pl.pallas_call(
        _s5_core_kernel,
        out_shape=[jax.ShapeDtypeStruct((g, S5_ROWS, S5_CW), F32),
                   jax.ShapeDtypeStruct((g, 2, NB_P, n2), F32)],
        grid=(g // S5_OCT,),
        in_specs=[chunks,
                  pl.BlockSpec((S5_OCT, S5_CW, S5_CW), lambda i: (i, 0, 0)),
                  spec4(S5_CW, n2), spec4(n2, S5_CW), spec4(16, n2),
                  pl.BlockSpec((S5_OCT, 2, 8, n2), lambda i: (i, 0, 0, 0)),
                  pl.BlockSpec((S5_OCT, 1, S5_CW), lambda i: (i, 0, 0))],
        out_specs=[chunks, pl.BlockSpec((S5_OCT, 2, NB_P, n2), lambda i: (i, 0, 0, 0))],
        scratch_shapes=[pltpu.VMEM((2, 2, S5_ROWS, n2), F32)],
        compiler_params=_params(1, VMEM_LIMIT),
        name="s5_scan",
    )(ug, w_intra, w_state, w_out, apow, h0, dvec)


DF_SCALE = DF_DH ** -0.5 * LOG2E
DFW = DF_HEADS * 2 * DF_DH
IN_B = 3 * HY_WIDTH + 2 * DFW + DF_HEADS * DF_V


def _inproj_b_kernel(y_ref, mod_ref, g_ref, win_ref, c_ref, s_ref,
                     hy_ref, q_ref, kp_ref, ks_ref, vp_ref, vs_ref, kc_ref, vc_ref, wbf_ref):
    @pl.when(pl.program_id(0) == 0)
    def _():
        wbf_ref[...] = win_ref[...].astype(BF16)

    mod = _mod_rows(mod_ref)
    h = _modulate(y_ref[...], g_ref[...], mod[3], mod[4]).astype(BF16)
    p = jnp.dot(h, wbf_ref[...], preferred_element_type=F32)
    o1 = 3 * HY_WIDTH
    hy_ref[...] = p[:, :o1]
    q_ref[...] = (_rope(p[:, o1:o1 + DFW], c_ref[...], s_ref[...]) * DF_SCALE).astype(BF16)
    _tok_write(kp_ref, ks_ref, _rope(p[:, o1 + DFW:o1 + 2 * DFW], c_ref[...], s_ref[...]).astype(BF16))
    _tok_write(vp_ref, vs_ref, p[:, o1 + 2 * DFW:].astype(BF16))

    @pl.when(pl.program_id(0) < NT_P)
    def _():
        for lo, cache_ref in ((o1 + DFW, kc_ref), (o1 + 2 * DFW, vc_ref)):
            t = lax.dot_general(wbf_ref[:, lo:lo + DFW], h, (((0,), (1,)), ((), ())),
                                preferred_element_type=F32)
            for j in range(TM // L_P):
                cache_ref[j] = t[:, j * L_P:(j + 1) * L_P]


def _inproj_b(y, mods_l, g, w_in):
    cs, sn = _rope_tables(DFW, tuple(range(0, DFW, DF_DH)))
    pos = lambda i: (_pos_index(i), 0)
    kv_shapes, kv_specs = [], []
    for width in (DFW, DF_HEADS * DF_V):
        kv_shapes += [jax.ShapeDtypeStruct((TOK_P, width), BF16), jax.ShapeDtypeStruct((TOK_S, width), BF16)]
        kv_specs += _split_out(width)[1]
    seqs = TM // L_P
    cache_shapes = [jax.ShapeDtypeStruct((NB_P, w, L_P), F32) for w in (DFW, DF_HEADS * DF_V)]
    cache_specs = [pl.BlockSpec((seqs, w, L_P), lambda i: (jnp.minimum(i, NT_P - 1), 0, 0))
                   for w in (DFW, DF_HEADS * DF_V)]
    hy_u, q, kp, ks, vp, vs, k_cache, v_cache = pl.pallas_call(
        _inproj_b_kernel,
        out_shape=[jax.ShapeDtypeStruct((TOK, 3 * HY_WIDTH), F32), jax.ShapeDtypeStruct((TOK, DFW), BF16)]
                  + kv_shapes + cache_shapes,
        grid=(NT,),
        in_specs=[pl.BlockSpec((TM, D), _row),
                  _mod_spec(mods_l[1]),
                  _const_spec((1, D)), _const_spec((D, IN_B)),
                  pl.BlockSpec((TM, DFW), pos), pl.BlockSpec((TM, DFW), pos)],
        out_specs=[pl.BlockSpec((TM, 3 * HY_WIDTH), _row), pl.BlockSpec((TM, DFW), _row)] + kv_specs + cache_specs,
        scratch_shapes=[pltpu.VMEM((D, IN_B), BF16)],
        compiler_params=_params(1, VMEM_LIMIT),
        name="inproj_odd",
    )(y, mods_l[0], g[None], w_in, jnp.asarray(cs), jnp.asarray(sn))
    return hy_u, q, (kp, ks), (vp, vs), k_cache, v_cache


def _diff_attn_kernel(nseg, nseq, lam_init, q_ref, lam_ref, sub_ref, *refs):
    o_ref = refs[-1]
    lp = lam_ref[...]
    lam = (jnp.exp(jnp.sum(lp[0:1] * lp[1:2], axis=-1, keepdims=True))
           - jnp.exp(jnp.sum(lp[2:3] * lp[3:4], axis=-1, keepdims=True)) + lam_init)
    tq = q_ref.shape[0] // nseq
    lane = lax.broadcasted_iota(jnp.int32, (tq, LANES), 1)
    for j in range(nseq):
        qr = slice(j * tq, (j + 1) * tq)
        krs = [slice(j * (refs[2 * s].shape[0] // nseq), (j + 1) * (refs[2 * s].shape[0] // nseq))
               for s in range(nseg)]
        for pair in range(DF_HEADS // 2):
            cs = slice(pair * LANES, (pair + 1) * LANES)
            q = q_ref[qr, cs]
            ks = [refs[2 * s][krs[s], cs].astype(BF16) for s in range(nseg)]
            vs = [refs[2 * s + 1][krs[s], cs].astype(BF16) for s in range(nseg)]
            outs = []
            for hh in range(2):
                parts = []
                for half in range(2):
                    unit = 2 * hh + half
                    qm = jnp.where((lane >> 5) == unit, q, jnp.zeros_like(q))
                    scores = [_dot_nt(qm, k) for k in ks]
                    parts.append(_softmax_pv(scores, vs, hh))
                o = parts[0] - lam * parts[1]
                mine = (lane >> 6) == hh
                ms = jnp.sum(jnp.where(mine, o * o, 0.0), axis=-1, keepdims=True) * (1.0 / DF_V)
                outs.append(o * lax.rsqrt(ms + EPS))
            o = jnp.where(lane < DF_V, outs[0], outs[1]) * sub_ref[...] * (1.0 - lam_init)
            o_ref[qr, cs] = o.astype(o_ref.dtype)


def _diff_attention(q, k, v, lam_p, subln, lam_init, n_batch, seq, tq, row0, ctx=None, nseq=1):
    qt = seq // tq
    qb0, kb0 = row0 // (nseq * tq), 0
    in_specs = [pl.BlockSpec((nseq * tq, DFW), lambda b, j: (qb0 + b * qt + j, 0)),
                pl.BlockSpec((4, DF_DH), lambda b, j: (0, 0)),
                pl.BlockSpec((1, LANES), lambda b, j: (0, 0))]
    args = [q, lam_p, jnp.concatenate([subln, subln])[None]]
    segs = []
    if ctx is not None:
        segs.append((ctx, PAST, 0))
    segs.append(((k, v), seq, kb0))
    for (a_k, a_v), ln, off in segs:
        idx = lambda b, j, off=off: (off + b, 0)
        in_specs += [pl.BlockSpec((nseq * ln, DFW), idx), pl.BlockSpec((nseq * ln, DF_HEADS * DF_V), idx)]
        args += [a_k, a_v]
    return pl.pallas_call(
        functools.partial(_diff_attn_kernel, len(segs), nseq, lam_init),
        out_shape=jax.ShapeDtypeStruct((n_batch * seq, DF_HEADS * DF_V), BF16),
        grid=(n_batch // nseq, qt),
        in_specs=in_specs,
        out_specs=pl.BlockSpec((nseq * tq, DF_HEADS * DF_V), lambda b, j: (b * qt + j, 0)),
        compiler_params=_params(2, VMEM_LIMIT),
        name="diff_attention",
    )(*args)


def _hy_filter_kernel(feat_ref, w1_ref, b1_ref, w2_ref, b2_ref, fq_ref, w3_ref, dec_ref, o_ref):
    feat = feat_ref[...]
    fq = fq_ref[...]
    h = jnp.sin(fq * (_dot3(feat, w1_ref[...]) + b1_ref[...]))
    h = jnp.sin(fq * (_dot3(h, w2_ref[...]) + b2_ref[...]))
    window = jnp.exp(-feat[:, 0:1] * jnp.abs(dec_ref[...]))
    for j in range(4):
        cs = slice(j * HY_WIDTH, (j + 1) * HY_WIDTH)
        o_ref[:, cs] = _dot3(h, w3_ref[:, cs]) * window


def _hy_spectrum_kernel(L, cs_ref, hf_ref, hb_ref, o_ref):
    row = lax.broadcasted_iota(jnp.int32, (L, HY_WIDTH), 0)
    first = row == 0
    tf = _dot(cs_ref[...], hf_ref[...])
    tb = _dot(cs_ref[...], jnp.where(first, 0.0, hb_ref[...]))
    ka = tf[:L] + tb[:L]
    kb = jnp.where(first, tf[L:] + tb[L:], tf[L:] - tb[L:])
    wv = jnp.where(first, 1.0 / (2 * L), 2.0 / (2 * L))
    o_ref[0, 0] = ka * wv
    o_ref[0, 1] = jnp.where(first, 0.0, kb) * wv
    o_ref[0, 2] = jnp.where(first, kb, ka) * wv


HY_CH = 256


def _hy_conv_kernel(L, cs_ref, ct_ref, kf_ref, v_ref, x1_ref, x2_ref,
                    wv_ref, w1_ref, w2_ref, bias_ref, o_ref):
    row = lax.broadcasted_iota(jnp.int32, (L, HY_CH), 0)

    def short(x, w):
        prev = jnp.where(row >= 1, pltpu.roll(x, 1, 0), 0.0)
        nxt = jnp.where(row <= L - 2, pltpu.roll(x, L - 1, 0), 0.0)
        return w[0:1] * prev + w[1:2] * x + w[2:3] * nxt

    for j in range(v_ref.shape[0] // L):
        rs = slice(j * L, (j + 1) * L)
        for k in range(HY_WIDTH // HY_CH):
            ch = slice(k * HY_CH, (k + 1) * HY_CH)
            z = short(v_ref[rs, ch], wv_ref[:, ch])
            gates = (short(x1_ref[rs, ch], w1_ref[:, ch]), short(x2_ref[rs, ch], w2_ref[:, ch]))
            for n in range(2):
                ab = _dot(cs_ref[...], z)
                a, b = ab[:L], ab[L:]
                ka, kb1, ka2 = kf_ref[n, 0, :, ch], kf_ref[n, 1, :, ch], kf_ref[n, 2, :, ch]
                pq = jnp.concatenate([a * ka - b * kb1, a * kb1 + b * ka2], axis=0)
                conv = _dot(ct_ref[...], pq)
                z = gates[n] * (conv + bias_ref[n:n + 1, ch] * z)
            o_ref[rs, ch] = z.astype(o_ref.dtype)


def _hyena_spectrum(L, phy):
    conv_w, w1, b1, w2, b2, freq, w3, decay, bias = phy
    feat = jnp.asarray(_hyena_features(L))
    w1p = jnp.pad(w1, ((0, LANES - HY_EMB), (0, 0)))
    filt = pl.pallas_call(
        _hy_filter_kernel,
        out_shape=jax.ShapeDtypeStruct((L, 4 * HY_WIDTH), F32),
        grid=(1,),
        in_specs=[_const_spec((L, LANES)), _const_spec((LANES, HY_FH)), _const_spec((1, HY_FH)),
                  _const_spec((HY_FH, HY_FH)), _const_spec((1, HY_FH)), _const_spec((1, HY_FH)),
                  _const_spec((HY_FH, 4 * HY_WIDTH)), _const_spec((1, HY_WIDTH))],
        out_specs=pl.BlockSpec((L, 4 * HY_WIDTH), lambda i: (0, 0)),
        compiler_params=_params(1, VMEM_LIMIT),
        name="hyena_filter",
    )(feat, w1p, b1[None], w2, b2[None], freq[None], w3, decay[None])
    cs = jnp.asarray(_dft_tables(L)[0]).astype(BF16)
    return pl.pallas_call(
        functools.partial(_hy_spectrum_kernel, L),
        out_shape=jax.ShapeDtypeStruct((2, 3, L, HY_WIDTH), F32),
        grid=(2,),
        in_specs=[_const_spec((2 * L, L)),
                  pl.BlockSpec((L, HY_WIDTH), lambda n: (0, n)),
                  pl.BlockSpec((L, HY_WIDTH), lambda n: (0, 2 + n))],
        out_specs=pl.BlockSpec((1, 3, L, HY_WIDTH), lambda n: (n, 0, 0, 0)),
        compiler_params=_params(1, VMEM_LIMIT),
        name="hyena_spectrum",
    )(cs, filt, filt)


def _hyena_conv(hy_u, spec, phy, n_batch, L, seqs, row0):
    conv_w, bias = phy[0], phy[8]
    cs, ct = (jnp.asarray(t).astype(BF16) for t in _dft_tables(L))
    rows = seqs * L
    rb0 = row0 // rows
    col = lambda off: (lambda b: (0, off))
    tok = lambda off: (lambda b: (rb0 + b, off))
    blk = lambda idx: pl.BlockSpec((rows, HY_WIDTH), idx)
    return pl.pallas_call(
        functools.partial(_hy_conv_kernel, L),
        out_shape=jax.ShapeDtypeStruct((n_batch * L, HY_WIDTH), BF16),
        grid=(n_batch // seqs,),
        in_specs=[_const_spec((2 * L, L)), _const_spec((L, 2 * L)), _const_spec((2, 3, L, HY_WIDTH)),
                  blk(tok(0)), blk(tok(1)), blk(tok(2)),
                  pl.BlockSpec((3, HY_WIDTH), col(0)), pl.BlockSpec((3, HY_WIDTH), col(1)),
                  pl.BlockSpec((3, HY_WIDTH), col(2)), _const_spec((2, HY_WIDTH))],
        out_specs=blk(lambda b: (b, 0)),
        compiler_params=_params(1, VMEM_LIMIT),
        name="hyena_conv",
    )(cs, ct, spec, hy_u, hy_u, hy_u, conv_w, conv_w, conv_w, bias)


def _even_mixer(y, mods_l, g, pa, ps5, ctx_ckv, ctx_krope, ctx_state):
    w_in, w_out, q_norm, w_uq, kv_norm, w_ukv = pa
    a_re, a_im, log_step, b_re, b_im, c_re, c_im, d_skip, w_glu = ps5
    q, ckv, kr_unrot, kr_rot, kn, v, ug, (w_k, w_v) = _inproj_a(y, mods_l, g, w_in, q_norm, w_uq, kv_norm, w_ukv)

    ctx_flat = ctx_ckv.reshape(NB_S * PAST, MLA_KV_RANK)
    ctx_kn = _linear(ctx_flat, w_k, PAST, BF16)
    ctx_v = _linear(ctx_flat, w_v, PAST, BF16)
    ctx_kr = jnp.pad(ctx_krope.reshape(NB_S * PAST, MLA_ROPE),
                     ((0, 0), (KR_AT, LANES - KR_AT - MLA_ROPE))).astype(BF16)
    att_p = _mla_attention(q, kn, kr_rot, v, NB_P, L_P, L_P, 0, nseq=2)
    att_s = _mla_attention(q, kn, kr_rot, v, NB_S, L_S, TM, TOK_P, ctx=(ctx_kn, ctx_kr, ctx_v))

    prep = _s5_prep(a_re, a_im, log_step, b_re, b_im, c_re, c_im)
    h0 = ctx_state.transpose(3, 1, 0, 2, 4).reshape(S5_GROUPS, 2, NB_S, 2 * S5_N)
    h0 = jnp.pad(h0, ((0, 0), (0, 0), (0, 8 - NB_S), (0, 0)))
    s5y, fin = _s5_core(ug, prep, h0, d_skip)

    mixer = ((att_p, att_s), s5y, w_out, w_glu)
    new_ckv = ckv.reshape(NB_P, L_P, MLA_KV_RANK)
    new_krope = kr_unrot.reshape(NB_P, L_P, MLA_ROPE)
    new_state = fin.reshape(S5_GROUPS, 2, NB_P, 2, S5_N).transpose(2, 1, 3, 0, 4)
    return mixer, new_ckv, new_krope, new_state


def _odd_mixer(y, mods_l, g, pb, phy, ctx_k, ctx_v, lam_init):
    w_in, w_out, lam_p, subln = pb
    hy_u, q, (k_p, k_s), (v_p, v_s), k_cache, v_cache = _inproj_b(y, mods_l, g, w_in)
    hy_p = _hyena_conv(hy_u, _hyena_spectrum(L_P, phy), phy, NB_P, L_P, 4, 0)
    hy_s = _hyena_conv(hy_u, _hyena_spectrum(L_S, phy), phy, NB_S, L_S, 1, TOK_P)
    ctx = (ctx_k.reshape(NB_S * PAST, DFW), ctx_v.reshape(NB_S * PAST, DF_HEADS * DF_V))
    att_p = _diff_attention(q, k_p, v_p, lam_p, subln, lam_init, NB_P, L_P, L_P, 0, nseq=2)
    att_s = _diff_attention(q, k_s, v_s, lam_p, subln, lam_init, NB_S, L_S, TM // 2, TOK_P, ctx=ctx)
    mixer = ((hy_p, hy_s), (att_p, att_s), w_out, None)
    new_k = k_cache.reshape(NB_P, DF_HEADS, 2, DF_DH, L_P).transpose(0, 4, 1, 2, 3)
    new_v = v_cache.reshape(NB_P, DF_HEADS, DF_V, L_P).transpose(0, 3, 1, 2)
    return mixer, new_k, new_v


def kernel(x_prompt, x_sample, c, c_ctx, cache_mla_ckv, cache_mla_krope, state_s5, cache_diff_k, cache_diff_v, ada_w, ada_b, norm_g, ff_w_in, ff_w_out, w_in_a, w_out_a, mla_q_norm, mla_w_uq, mla_kv_norm, mla_w_ukv, s5_a_re, s5_a_im, s5_log_step, s5_b_re, s5_b_im, s5_c_re, s5_c_im, s5_d, s5_w_glu, w_in_b, w_out_b, hy_conv, hy_w1, hy_b1, hy_w2, hy_b2, hy_freq, hy_w3, hy_decay, hy_bias, df_lambda, df_subln, final_norm):
    depth = ada_w.shape[0]
    y = (x_prompt.reshape(TOK_P, D), x_sample.reshape(TOK_S, D))
    mods = _adaln(jnp.concatenate([c_ctx[None], c], axis=0), ada_w, ada_b)
    new_ckv, new_krope, new_s5, new_dk, new_dv = [], [], [], [], []
    for l in range(depth):
        y = _half_ffn(y, (mods, l), norm_g[l, 0], ff_w_in, ff_w_out, l, 0)
        if l % 2 == 0:
            e = l // 2
            pa = (w_in_a[e], w_out_a[e], mla_q_norm[e], mla_w_uq[e], mla_kv_norm[e], mla_w_ukv[e])
            ps5 = (s5_a_re[e], s5_a_im[e], s5_log_step[e], s5_b_re[e], s5_b_im[e],
                   s5_c_re[e], s5_c_im[e], s5_d[e], s5_w_glu[e])
            mixer, ckv, krope, st = _even_mixer(y, (mods, l), norm_g[l, 1], pa, ps5, cache_mla_ckv[:, e],
                                            cache_mla_krope[:, e], state_s5[:, e])
            new_ckv.append(ckv)
            new_krope.append(krope)
            new_s5.append(st)
        else:
            o = l // 2
            lam_init = 0.8 - 0.6 * math.exp(-0.3 * l)
            pb = (w_in_b[o], w_out_b[o], df_lambda[o], df_subln[o])
            phy = (hy_conv[o], hy_w1[o], hy_b1[o], hy_w2[o], hy_b2[o], hy_freq[o],
                   hy_w3[o], hy_decay[o], hy_bias[o])
            mixer, dk, dv = _odd_mixer(y, (mods, l), norm_g[l, 1], pb, phy, cache_diff_k[:, o],
                                   cache_diff_v[:, o], lam_init)
            new_dk.append(dk)
            new_dv.append(dv)
        last = l == depth - 1
        y = _half_ffn(y, (mods, l), norm_g[l, 2], ff_w_in, ff_w_out, l, 1,
                      final_g=final_norm if last else None, mixer=mixer)
    y_prompt = y[0].reshape(NB_P, L_P, D)
    y_sample = y[1].reshape(NB_S, L_S, D)
    return (y_prompt, y_sample, jnp.stack(new_ckv, axis=1), jnp.stack(new_krope, axis=1),
            jnp.stack(new_s5, axis=1), jnp.stack(new_dk, axis=1), jnp.stack(new_dv, axis=1))
```

```python
import functools
import math

import numpy as np
import jax
import jax.numpy as jnp
from jax import lax
from jax.experimental import pallas as pl
from jax.experimental.pallas import tpu as pltpu

F32 = jnp.float32
BF16 = jnp.bfloat16

D = 1024
NB_P, L_P = 16, 256
NB_S, L_S = 2, 1024
PAST = 256
GRID_W = 64
N_MOD = 9
FF = 2816
EPS = 1e-6
ROPE_BASE = 10000.0

MLA_HEADS, MLA_NOPE, MLA_ROPE, MLA_V = 8, 64, 32, 64
MLA_Q_RANK, MLA_KV_RANK = 384, 256
S5_WIDTH, S5_GROUP, S5_N = 512, 16, 64
S5_GROUPS = S5_WIDTH // S5_GROUP
HY_WIDTH, HY_BANDS, HY_FH = 512, 16, 64
HY_EMB = 2 * HY_BANDS + 1
DF_HEADS, DF_DH = 8, 32
DF_V = 2 * DF_DH

TOK_P = NB_P * L_P
TOK_S = NB_S * L_S
TOK = TOK_P + TOK_S
TM = 512
NT = TOK // TM
NT_P = TOK_P // TM
TILES_PER_SAMPLE = L_S // TM

LANES = 128
HALF_SHIFT = (LANES // 2).bit_length() - 1
S5_GROUP_SHIFT = S5_GROUP.bit_length() - 1
DF_DH_SHIFT = DF_DH.bit_length() - 1
S5_T = 16
S5_CW = S5_T * S5_GROUP
CH_P = L_P // S5_T
CH_S = L_S // S5_T
S5_ROWS = NB_P * CH_P + NB_S * CH_S
S5_ROWS_P = NB_P * CH_P

VMEM_LIMIT = 56 * 1024 * 1024


def _params(n_grid, vmem=None):
    return pltpu.CompilerParams(dimension_semantics=("arbitrary",) * n_grid,
                                vmem_limit_bytes=vmem)


def _const_spec(shape):
    nd = len(shape)
    return pl.BlockSpec(shape, lambda *_: (0,) * nd, pipeline_mode=pl.Buffered(1))


def _mod_index(i):
    return jnp.where(i < NT_P, 0, 1 + (i - NT_P) // TILES_PER_SAMPLE)


def _mod_spec(layer):
    return pl.BlockSpec((1, 8, N_MOD * D), lambda *_: (layer, 0, 0), pipeline_mode=pl.Buffered(1))


def _mod_rows(mod_ref):
    row = mod_ref[0, pl.ds(_mod_index(pl.program_id(0)), 1), :]
    return [row[:, k * D:(k + 1) * D] for k in range(N_MOD)]


def _pos_index(i):
    return jnp.where(i < NT_P, 0, 1 + (i - NT_P) % TILES_PER_SAMPLE)


def _row(i):
    return (i, 0)


def _row_p(i):
    return (jnp.minimum(i, NT_P - 1), 0)


def _row_s(i):
    return (jnp.maximum(i - NT_P, 0), 0)


def _tok_specs(x, width):
    if isinstance(x, tuple):
        return [pl.BlockSpec((TM, width), _row_p), pl.BlockSpec((TM, width), _row_s)], list(x)
    return [pl.BlockSpec((TM, width), _row)], [x]


def _tok_read(refs, split):
    if split:
        return jnp.where(pl.program_id(0) < NT_P, refs[0][...], refs[1][...]), refs[2:]
    return refs[0][...], refs[1:]


def _tok_write(p_ref, s_ref, value):
    i = pl.program_id(0)

    @pl.when(i < NT_P)
    def _():
        p_ref[...] = value

    @pl.when(i >= NT_P)
    def _():
        s_ref[...] = value.astype(s_ref.dtype)


def _split_out(width, sample_dtype=F32):
    shapes = [jax.ShapeDtypeStruct((TOK_P, width), F32), jax.ShapeDtypeStruct((TOK_S, width), sample_dtype)]
    specs = [pl.BlockSpec((TM, width), _row_p), pl.BlockSpec((TM, width), _row_s)]
    return shapes, specs


def _dot(a, b):
    return jnp.dot(a.astype(BF16), b.astype(BF16), preferred_element_type=F32)


def _dot_nt(a, b):
    return lax.dot_general(a, b, (((1,), (1,)), ((), ())), preferred_element_type=F32)


def _split(x):
    hi = x.astype(BF16)
    lo = (x - hi.astype(F32)).astype(BF16)
    return hi, lo


def _dot3(a, b):
    ah, al = _split(a)
    bh, bl = _split(b)
    d = functools.partial(jnp.dot, preferred_element_type=F32)
    return d(ah, bh) + d(ah, bl) + d(al, bh)


def _rmsnorm(x, g):
    return x * lax.rsqrt(jnp.mean(x * x, axis=-1, keepdims=True) + EPS) * g


def _modulate(y, g, shift, scale):
    return _rmsnorm(y, g) * (1.0 + scale) + shift


def _pair_swap(x):
    n = x.shape[-1]
    lane = lax.broadcasted_iota(jnp.int32, x.shape, x.ndim - 1)
    return jnp.where((lane & 1) == 0, pltpu.roll(x, n - 1, x.ndim - 1), pltpu.roll(x, 1, x.ndim - 1))


def _rope(x, cos, sin_signed):
    return x * cos + _pair_swap(x) * sin_signed


def _rope_angles():
    n_freq = MLA_ROPE // 4
    inv = 1.0 / (ROPE_BASE ** (np.arange(n_freq, dtype=np.float64) / n_freq))
    pos = np.arange(L_S)
    row = (pos // GRID_W).astype(np.float64)
    col = (pos % GRID_W).astype(np.float64)
    ang = np.concatenate([row[:, None] * inv, col[:, None] * inv], axis=-1)
    return np.cos(ang), np.sin(ang)


@functools.lru_cache(maxsize=None)
def _rope_tables(width, starts):
    cos, sin = _rope_angles()
    c = np.ones((TM + L_S, width), np.float32)
    s = np.zeros((TM + L_S, width), np.float32)
    sign = np.where(np.arange(MLA_ROPE) % 2 == 0, -1.0, 1.0)
    unit_c = np.repeat(cos, 2, axis=1)
    unit_s = np.repeat(sin, 2, axis=1) * sign
    for st in starts:
        c[TM:, st:st + MLA_ROPE] = unit_c
        s[TM:, st:st + MLA_ROPE] = unit_s
    return c, s


@functools.lru_cache(maxsize=None)
def _dft_tables(L):
    f = np.arange(L)[:, None]
    s = np.arange(L)[None, :]
    ang = np.pi * ((f * s) % (2 * L)).astype(np.float64) / L
    cs = np.concatenate([np.cos(ang), np.sin(ang)], axis=0)
    cs[L, :] = np.where(np.arange(L) % 2 == 0, 1.0, -1.0)
    cs = cs.astype(np.float32)
    return cs, np.ascontiguousarray(cs.T)


@functools.lru_cache(maxsize=None)
def _hyena_features(L):
    t = np.arange(L, dtype=np.float64) / L
    bands = np.arange(1, HY_BANDS + 1, dtype=np.float64)
    ang = 2.0 * math.pi * t[:, None] * bands
    feat = np.zeros((L, LANES), np.float32)
    feat[:, 0] = t
    feat[:, 1:1 + HY_BANDS] = np.cos(ang)
    feat[:, 1 + HY_BANDS:HY_EMB] = np.sin(ang)
    return feat


def _adaln_kernel(c_ref, w_ref, b_ref, o_ref):
    s = jax.nn.silu(c_ref[...])
    s_hi = s.astype(BF16).astype(F32)
    stacked = jnp.concatenate([s_hi, s - s_hi], axis=0).astype(BF16)
    wh, wl = _split(w_ref[0])
    both = jnp.dot(stacked, wh, preferred_element_type=F32)
    rows = c_ref.shape[0]
    o_ref[0] = both[:rows] + both[rows:] + jnp.dot(stacked, wl, preferred_element_type=F32)[:rows] + b_ref[0]


def _adaln(cvecs, ada_w, ada_b):
    depth = ada_w.shape[0]
    n_vec = cvecs.shape[0]
    tn = N_MOD * D // 4
    out = pl.pallas_call(
        _adaln_kernel,
        out_shape=jax.ShapeDtypeStruct((depth, 8, N_MOD * D), F32),
        grid=(depth, N_MOD * D // tn),
        in_specs=[pl.BlockSpec((8, D), lambda l, j: (0, 0)),
                  pl.BlockSpec((1, D, tn), lambda l, j: (l, 0, j)),
                  pl.BlockSpec((1, 1, tn), lambda l, j: (l, 0, j))],
        out_specs=pl.BlockSpec((1, 8, tn), lambda l, j: (l, 0, j)),
        compiler_params=_params(2, VMEM_LIMIT),
        name="adaln",
    )(jnp.pad(cvecs, ((0, 8 - n_vec), (0, 0))), ada_w, ada_b[:, None, :])
    return out


FF_PIECE = 256
FF_LOADS = FF // FF_PIECE


def _ffn_kernel(base, final, split_in, mixer, layer, which, *refs):
    y, refs = _tok_read(refs, split_in)
    if mixer is not None:
        a1, refs = _tok_read(refs, mixer[0])
        if mixer[2]:
            a2_chunks, refs = refs[0], refs[1:]
        else:
            a2, refs = _tok_read(refs, mixer[1])
        wmix_ref, wg_ref = refs[:2]
        refs = refs[2:]
    mod_ref, g_ref, win_hbm, wout_hbm, fg_ref = refs[:5]
    n_out = 2 if final else 1
    outs = refs[5:5 + n_out]
    win_ref, wout_ref, stage_g, stage_u, stage_o, sems = refs[5 + n_out:11 + n_out]
    mod = _mod_rows(mod_ref)
    if mixer is not None:
        if mixer[2]:
            a2_scr = refs[11 + n_out]
            _chunks_to_tokens(a2_chunks, a2_scr)
            a2 = jax.nn.gelu(jnp.concatenate([a2_scr[o] for o in range(a2_scr.shape[0])], axis=1))
            a2 = a2 * jax.nn.sigmoid(_dot(a2, wg_ref[...]))
        k1 = wmix_ref.shape[0] // 2
        y = y + mod[5] * (_dot(a1, wmix_ref[:k1]) + _dot(a2, wmix_ref[k1:]))
    h = _modulate(y, g_ref[...], mod[base], mod[base + 1]).astype(BF16)

    def hidden(lo, width):
        gate = jnp.dot(h, win_ref[:, lo:lo + width], preferred_element_type=F32)
        up = jnp.dot(h, win_ref[:, FF + lo:FF + lo + width], preferred_element_type=F32)
        a = (jax.nn.silu(gate) * up).astype(BF16)
        return jnp.dot(a, wout_ref[lo:lo + width, :], preferred_element_type=F32)

    def finish(acc):
        out = y + 0.5 * mod[base + 2] * acc
        if final:
            _tok_write(outs[0], outs[1], _rmsnorm(out, fg_ref[...]))
        else:
            outs[0][...] = out

    @pl.when(pl.program_id(0) == 0)
    def _():
        def copies(c, slot):
            cols = pl.ds(c * FF_PIECE, FF_PIECE)
            return (pltpu.make_async_copy(win_hbm.at[layer, which, :, cols], stage_g.at[slot], sems.at[0, slot]),
                    pltpu.make_async_copy(win_hbm.at[layer, which, :, pl.ds(FF + c * FF_PIECE, FF_PIECE)],
                                          stage_u.at[slot], sems.at[1, slot]),
                    pltpu.make_async_copy(wout_hbm.at[layer, which, cols, :], stage_o.at[slot], sems.at[2, slot]))

        for cp in copies(0, 0):
            cp.start()
        acc = jnp.zeros(y.shape, F32)
        for c in range(FF_LOADS):
            slot = c % 2
            lo = c * FF_PIECE
            if c + 1 < FF_LOADS:
                for cp in copies(c + 1, 1 - slot):
                    cp.start()
            for cp in copies(c, slot):
                cp.wait()
            win_ref[:, lo:lo + FF_PIECE] = stage_g[slot].astype(BF16)
            win_ref[:, FF + lo:FF + lo + FF_PIECE] = stage_u[slot].astype(BF16)
            wout_ref[lo:lo + FF_PIECE, :] = stage_o[slot].astype(BF16)
            acc = acc + hidden(lo, FF_PIECE)
        finish(acc)

    @pl.when(pl.program_id(0) > 0)
    def _():
        finish(hidden(0, FF))


def _half_ffn(y, mods_l, g, ff_w_in, ff_w_out, layer, which, final_g=None, mixer=None):
    final = final_g is not None
    fg = final_g if final else g
    y_specs, y_args = _tok_specs(y, D)
    mix_flags = None
    extra_scratch = []
    if mixer is not None:
        a1, a2, w_out, w_glu = mixer
        k1 = w_out.shape[0] // 2
        wg = w_glu if w_glu is not None else jnp.zeros((8, LANES), F32)
        s1, a1_args = _tok_specs(a1, k1)
        if w_glu is not None:
            s2 = [pl.BlockSpec((S5_GROUPS, TM // S5_T, S5_CW), lambda i: (0, i, 0))]
            a2_args = [a2]
            extra_scratch = [pltpu.VMEM((k1 // LANES, TM, LANES), F32)]
        else:
            s2, a2_args = _tok_specs(a2, k1)
        y_specs = y_specs + s1 + s2 + [_const_spec(w_out.shape), _const_spec(wg.shape)]
        y_args = y_args + a1_args + a2_args + [w_out, wg]
        mix_flags = (isinstance(a1, tuple), isinstance(a2, tuple), w_glu is not None)
    if final:
        out_shape, out_specs = _split_out(D)
    else:
        out_shape, out_specs = jax.ShapeDtypeStruct((TOK, D), F32), pl.BlockSpec((TM, D), _row)
    return pl.pallas_call(
        functools.partial(_ffn_kernel, 6 * which, final, isinstance(y, tuple), mix_flags, layer, which),
        out_shape=out_shape,
        grid=(NT,),
        in_specs=y_specs + [_mod_spec(mods_l[1]),
                            _const_spec((1, D)),
                            pl.BlockSpec(memory_space=pl.ANY),
                            pl.BlockSpec(memory_space=pl.ANY),
                            _const_spec((1, D))],
        out_specs=out_specs,
        scratch_shapes=[pltpu.VMEM((D, 2 * FF), BF16), pltpu.VMEM((FF, D), BF16),
                        pltpu.VMEM((2, D, FF_PIECE), F32), pltpu.VMEM((2, D, FF_PIECE), F32),
                        pltpu.VMEM((2, FF_PIECE, D), F32), pltpu.SemaphoreType.DMA((3, 2))] + extra_scratch,
        compiler_params=_params(1, VMEM_LIMIT),
        name="half_ffn",
    )(*y_args, mods_l[0], g[None], ff_w_in, ff_w_out, fg[None])


def _linear_kernel(x_ref, w_ref, o_ref):
    o_ref[...] = _dot(x_ref[...], w_ref[...]).astype(o_ref.dtype)


def _linear(x, w, tm, out_dtype):
    m, k = x.shape
    n = w.shape[1]
    return pl.pallas_call(
        _linear_kernel,
        out_shape=jax.ShapeDtypeStruct((m, n), out_dtype),
        grid=(m // tm,),
        in_specs=[pl.BlockSpec((tm, k), lambda i: (i, 0)), _const_spec((k, n))],
        out_specs=pl.BlockSpec((tm, n), lambda i: (i, 0)),
        compiler_params=_params(1),
        name="linear",
    )(x, w.astype(BF16))


LOG2E = math.log2(math.e)
MLA_SCALE = (MLA_NOPE + MLA_ROPE) ** -0.5 * LOG2E
QW = MLA_HEADS * LANES
KR_AT = MLA_NOPE
IN_A_PAD = MLA_Q_RANK + MLA_KV_RANK + S5_WIDTH + LANES


def _inproj_a_kernel(y_ref, mod_ref, g_ref, win_ref, qn_ref, wuq_ref, kvn_ref, wk_ref, wv_ref,
                     cq_ref, sq_ref, ck_ref, sk_ref,
                     q_ref, ckv_ref, kru_ref, krr_ref, kn_ref, v_ref, ug_ref, u_scr):
    mod = _mod_rows(mod_ref)
    h = _modulate(y_ref[...], g_ref[...], mod[3], mod[4]).astype(BF16)
    p = jnp.dot(h, win_ref[...], preferred_element_type=F32)
    o1 = MLA_Q_RANK
    o2 = o1 + MLA_KV_RANK
    o3 = o2 + S5_WIDTH
    q = _dot(_rmsnorm(p[:, :o1], qn_ref[...]), wuq_ref[...])
    q_ref[...] = (_rope(q, cq_ref[...], sq_ref[...]) * MLA_SCALE).astype(BF16)
    ckv = _rmsnorm(p[:, o1:o2], kvn_ref[...])
    ckv_b = ckv.astype(BF16)
    kn_ref[...] = jnp.dot(ckv_b, wk_ref[...], preferred_element_type=F32).astype(BF16)
    v_ref[...] = jnp.dot(ckv_b, wv_ref[...], preferred_element_type=F32).astype(BF16)
    for octet in range(S5_WIDTH // LANES):
        u_scr[octet] = p[:, o2 + octet * LANES:o2 + (octet + 1) * LANES]
    _tokens_to_chunks(u_scr, ug_ref)
    krp = p[:, o3:]
    krr_ref[...] = _rope(krp, ck_ref[...], sk_ref[...]).astype(BF16)

    @pl.when(pl.program_id(0) < NT_P)
    def _():
        ckv_ref[...] = ckv
        kru_ref[...] = krp[:, KR_AT:KR_AT + MLA_ROPE]


def _inproj_a(y, mods_l, g, w_in, q_norm, w_uq, kv_norm, w_ukv):
    o1 = MLA_Q_RANK
    o2 = o1 + MLA_KV_RANK
    o3 = o2 + MLA_ROPE
    kr_cols = jnp.pad(w_in[:, o2:o3], ((0, 0), (KR_AT, LANES - KR_AT - MLA_ROPE)))
    w_ext = jnp.concatenate([w_in[:, :o2], w_in[:, o3:], kr_cols], axis=1).astype(BF16)
    dq = MLA_NOPE + MLA_ROPE
    w_uq_pad = jnp.pad(w_uq.reshape(MLA_Q_RANK, MLA_HEADS, dq),
                       ((0, 0), (0, 0), (0, LANES - dq))).reshape(MLA_Q_RANK, QW).astype(BF16)
    w_kv = w_ukv.reshape(MLA_KV_RANK, MLA_HEADS, MLA_NOPE + MLA_V)
    w_k = jnp.pad(w_kv[:, :, :MLA_NOPE], ((0, 0), (0, 0), (0, LANES - MLA_NOPE))).reshape(MLA_KV_RANK, QW)
    w_v = w_kv[:, :, MLA_NOPE:].reshape(MLA_KV_RANK, MLA_HEADS * MLA_V)
    w_k, w_v = w_k.astype(BF16), w_v.astype(BF16)
    cq, sq = _rope_tables(QW, tuple(h * LANES + MLA_NOPE for h in range(MLA_HEADS)))
    ck, sk = _rope_tables(LANES, (KR_AT,))
    row = _row
    pos = lambda i: (_pos_index(i), 0)
    widths = (QW, MLA_KV_RANK, MLA_ROPE, LANES, QW, MLA_HEADS * MLA_V)
    prompt_only = (1, 2)
    mxu_only = (0, 3, 4, 5)
    outs = pl.pallas_call(
        _inproj_a_kernel,
        out_shape=[jax.ShapeDtypeStruct((TOK_P if k in prompt_only else TOK, w), BF16 if k in mxu_only else F32)
                   for k, w in enumerate(widths)]
                  + [jax.ShapeDtypeStruct((S5_GROUPS, S5_ROWS, S5_CW), F32)],
        grid=(NT,),
        in_specs=[pl.BlockSpec((TM, D), row),
                  _mod_spec(mods_l[1]),
                  _const_spec((1, D)),
                  _const_spec((D, IN_A_PAD)),
                  _const_spec((1, MLA_Q_RANK)),
                  _const_spec((MLA_Q_RANK, QW)),
                  _const_spec((1, MLA_KV_RANK)),
                  _const_spec((MLA_KV_RANK, QW)),
                  _const_spec((MLA_KV_RANK, MLA_HEADS * MLA_V)),
                  pl.BlockSpec((TM, QW), pos), pl.BlockSpec((TM, QW), pos),
                  pl.BlockSpec((TM, LANES), pos), pl.BlockSpec((TM, LANES), pos)],
        out_specs=[pl.BlockSpec((TM, w), _row_p if k in prompt_only else row)
                   for k, w in enumerate(widths)]
                  + [pl.BlockSpec((S5_GROUPS, TM // S5_T, S5_CW), lambda i: (0, i, 0))],
        scratch_shapes=[pltpu.VMEM((S5_WIDTH // LANES, TM, LANES), F32)],
        compiler_params=_params(1, VMEM_LIMIT),
        name="inproj_even",
    )(y, mods_l[0], g[None], w_ext, q_norm[None], w_uq_pad, kv_norm[None], w_k, w_v,
      jnp.asarray(cq), jnp.asarray(sq), jnp.asarray(ck), jnp.asarray(sk))
    q, ckv, kr_unrot, kr_rot, kn, v, ug = outs
    return q, ckv, kr_unrot, kr_rot, kn, v, ug, (w_k, w_v)


def _softmax_pv(scores, vals, half):
    m = functools.reduce(jnp.maximum, [jnp.max(s, axis=-1, keepdims=True) for s in scores])
    pv = None
    for s, v in zip(scores, vals):
        own_k = (lax.broadcasted_iota(jnp.int32, v.shape, 1) >> HALF_SHIFT) == half
        part = jnp.dot(jnp.exp2(s - m).astype(BF16), jnp.where(own_k, v, jnp.ones_like(v)),
                       preferred_element_type=F32)
        pv = part if pv is None else pv + part
    own = (lax.broadcasted_iota(jnp.int32, pv.shape, 1) >> HALF_SHIFT) == half
    denom = jnp.max(jnp.where(own, 0.0, pv), axis=-1, keepdims=True)
    return pv * (1.0 / denom)


def _mla_attn_kernel(nseg, nseq, q_ref, *refs):
    o_ref = refs[-1]
    tq = q_ref.shape[0] // nseq
    lane = lax.broadcasted_iota(jnp.int32, (tq, LANES), 1)
    for j in range(nseq):
        qr = slice(j * tq, (j + 1) * tq)
        krs = [slice(j * (refs[3 * s].shape[0] // nseq), (j + 1) * (refs[3 * s].shape[0] // nseq))
               for s in range(nseg)]
        for pair in range(MLA_HEADS // 2):
            outs = []
            for hh in range(2):
                h = 2 * pair + hh
                hs = slice(h * LANES, (h + 1) * LANES)
                qh = q_ref[qr, hs]
                scores = []
                for s in range(nseg):
                    kn_ref, kr_ref = refs[3 * s], refs[3 * s + 1]
                    kh = (kn_ref[krs[s], hs] + kr_ref[krs[s], :]).astype(BF16)
                    scores.append(_dot_nt(qh, kh))
                vals = [refs[3 * s + 2][krs[s], pair * LANES:(pair + 1) * LANES] for s in range(nseg)]
                outs.append(_softmax_pv(scores, vals, hh))
            o_ref[qr, pair * LANES:(pair + 1) * LANES] = jnp.where(lane < MLA_V, outs[0], outs[1]).astype(o_ref.dtype)


def _mla_attention(q, kn, kr, v, n_batch, seq, tq, row0, ctx=None, nseq=1):
    qt = seq // tq
    qb0, kb0 = row0 // (nseq * tq), row0 // (nseq * seq)
    in_specs = [pl.BlockSpec((nseq * tq, QW), lambda b, j: (qb0 + b * qt + j, 0))]
    args = [q]
    segs = []
    if ctx is not None:
        segs.append((ctx, PAST, 0))
    segs.append(((kn, kr, v), seq, kb0))
    for (a_kn, a_kr, a_v), ln, off in segs:
        idx = lambda b, j, off=off: (off + b, 0)
        in_specs += [pl.BlockSpec((nseq * ln, QW), idx), pl.BlockSpec((nseq * ln, LANES), idx),
                     pl.BlockSpec((nseq * ln, MLA_HEADS * MLA_V), idx)]
        args += [a_kn, a_kr, a_v]
    return pl.pallas_call(
        functools.partial(_mla_attn_kernel, len(segs), nseq),
        out_shape=jax.ShapeDtypeStruct((n_batch * seq, MLA_HEADS * MLA_V), BF16),
        grid=(n_batch // nseq, qt),
        in_specs=in_specs,
        out_specs=pl.BlockSpec((nseq * tq, MLA_HEADS * MLA_V), lambda b, j: (b * qt + j, 0)),
        compiler_params=_params(2, VMEM_LIMIT),
        name="mla_attention",
    )(*args)


def _cpow(ar, ai, e, nbits):
    rr = jnp.ones_like(ar)
    ri = jnp.zeros_like(ar)
    br, bi = ar, ai
    for k in range(nbits):
        bit = ((e >> k) & 1) == 1
        nr = rr * br - ri * bi
        ni = rr * bi + ri * br
        rr = jnp.where(bit, nr, rr)
        ri = jnp.where(bit, ni, ri)
        if k + 1 < nbits:
            br, bi = br * br - bi * bi, 2.0 * br * bi
    return rr, ri


def _s5_abar_kernel(lr_ref, li_ref, ls_ref, o_ref):
    step = jnp.exp(ls_ref[...])
    lr = jnp.minimum(lr_ref[...], -1e-4)
    li = li_ref[...]
    mag = jnp.exp(lr * step)
    ar = mag * jnp.cos(li * step)
    ai = mag * jnp.sin(li * step)
    den = lr * lr + li * li
    o_ref[0] = ar
    o_ref[1] = ai
    o_ref[2] = ((ar - 1.0) * lr + ai * li) / den
    o_ref[3] = (ai * lr - (ar - 1.0) * li) / den


S5_PREP_GROUPS = 4


def _s5_prep_kernel(*refs):
    for gi in range(S5_PREP_GROUPS):
        _s5_prep_group(gi, *refs)


def _s5_prep_group(gi, arow_ref, acol_ref, btr_ref, bti_ref, ctr_ref, cti_ref,
                   wi_ref, ws_ref, wo_ref, ap_ref):
    n2 = 2 * S5_N
    blk_o = lax.broadcasted_iota(jnp.int32, (S5_N, S5_CW), 1) >> S5_GROUP_SHIFT
    lane_k = lax.broadcasted_iota(jnp.int32, (S5_GROUP, S5_CW), 1)
    row_k = lax.broadcasted_iota(jnp.int32, (S5_GROUP, S5_CW), 0)
    lane_b = lax.broadcasted_iota(jnp.int32, (S5_GROUP, n2), 1)
    lane_a = lax.broadcasted_iota(jnp.int32, (1, n2), 1)
    rep = ((lane_k & (S5_GROUP - 1)) == row_k).astype(BF16)

    def tile16(x):
        hi = x.astype(BF16)
        r1 = x - hi.astype(F32)
        mid = r1.astype(BF16)
        lo = (r1 - mid.astype(F32)).astype(BF16)
        d = lambda a: lax.dot_general(a, rep, (((0,), (0,)), ((), ())), preferred_element_type=F32)
        return d(hi) + d(mid) + d(lo)

    intra = [None] * S5_T
    for d in range(2):
        ar, ai, fr, fi = (arow_ref[d, gi, k:k + 1, :] for k in range(4))
        btr, bti = btr_ref[d, gi], bti_ref[d, gi]
        bbr = fr * btr - fi * bti
        bbi = fr * bti + fi * btr
        pws = [(jnp.ones_like(ar), jnp.zeros_like(ar))]
        for _ in range(S5_T):
            pr, pi = pws[-1]
            pws.append((pr * ar - pi * ai, pr * ai + pi * ar))
        for s in range(S5_T):
            pr, pi = pws[S5_T - 1 - s] if d == 0 else pws[s]
            ws_ref[d, gi, s * S5_GROUP:(s + 1) * S5_GROUP, :] = jnp.where(
                lane_b < S5_N, pr * bbr - pi * bbi, pr * bbi + pi * bbr).astype(BF16)

        acol = acol_ref[d, gi]
        arc = jnp.broadcast_to(acol[:, 0:1], (S5_N, S5_CW))
        aic = jnp.broadcast_to(acol[:, 1:2], (S5_N, S5_CW))
        ctr, cti = tile16(ctr_ref[d, gi]), tile16(cti_ref[d, gi])
        e_lag = blk_o if d == 0 else (S5_T - 1 - blk_o)
        pqr, pqi = _cpow(arc, aic, e_lag, 4)
        qr = pqr * ctr - pqi * cti
        qi = pqr * cti + pqi * ctr
        wo_ref[d, gi] = jnp.concatenate([qr * arc - qi * aic, -(qr * aic + qi * arc)], axis=0).astype(BF16)
        q_stack = jnp.concatenate([qr, qi], axis=0)
        bb_mix = jnp.where(lane_b < S5_N, bbr, -bbi)
        kt = _dot3(bb_mix, q_stack)
        for s in range(S5_T):
            if d == 0:
                blk = jnp.where(lane_k >= S5_GROUP * s, pltpu.roll(kt, S5_GROUP * s, 1), 0.0)
            else:
                blk = jnp.where(lane_k < S5_GROUP * (s + 1),
                                pltpu.roll(kt, (S5_GROUP * (s + 1)) % S5_CW, 1), 0.0)
            intra[s] = blk if intra[s] is None else intra[s] + blk

        pr1, pi1 = pws[S5_T]
        for k in range(6):
            ap_ref[d, gi, k:k + 1, :] = pr1
            ap_ref[d, gi, 8 + k:9 + k, :] = jnp.where(lane_a < S5_N, -pi1, pi1)
            pr1, pi1 = pr1 * pr1 - pi1 * pi1, 2.0 * pr1 * pi1
        ap_ref[d, gi, 6:8, :] = jnp.zeros((2, n2), F32)
        ap_ref[d, gi, 14:16, :] = jnp.zeros((2, n2), F32)
    for s in range(S5_T):
        wi_ref[gi, s * S5_GROUP:(s + 1) * S5_GROUP, :] = intra[s].astype(BF16)


def _s5_prep(a_re, a_im, log_step, b_re, b_im, c_re, c_im):
    g, n, n2 = S5_GROUPS, S5_N, 2 * S5_N
    abar = pl.pallas_call(
        _s5_abar_kernel,
        out_shape=jax.ShapeDtypeStruct((4, 2 * g, n), F32),
        grid=(1,),
        in_specs=[_const_spec((2 * g, n)), _const_spec((2 * g, n)), _const_spec((2 * g, 1))],
        out_specs=pl.BlockSpec((4, 2 * g, n), lambda i: (0, 0, 0)),
        compiler_params=_params(1),
        name="s5_abar",
    )(a_re.reshape(2 * g, n), a_im.reshape(2 * g, n), log_step.reshape(2 * g, 1))
    abar = jnp.concatenate([abar, abar], axis=-1).reshape(4, 2, g, n2)
    arow = abar.transpose(1, 2, 0, 3)
    acol = abar[:2, :, :, :n].transpose(1, 2, 3, 0)
    bt = lambda b: jnp.concatenate([jnp.swapaxes(b, 2, 3)] * 2, axis=-1)
    spec4 = lambda r, c: pl.BlockSpec((2, S5_PREP_GROUPS, r, c), lambda i: (0, i, 0, 0))
    return pl.pallas_call(
        _s5_prep_kernel,
        out_shape=[jax.ShapeDtypeStruct((g, S5_CW, S5_CW), BF16),
                   jax.ShapeDtypeStruct((2, g, S5_CW, n2), BF16),
                   jax.ShapeDtypeStruct((2, g, n2, S5_CW), BF16),
                   jax.ShapeDtypeStruct((2, g, 16, n2), F32)],
        grid=(g // S5_PREP_GROUPS,),
        in_specs=[spec4(4, n2), spec4(n, 2),
                  spec4(S5_GROUP, n2), spec4(S5_GROUP, n2), spec4(S5_GROUP, n), spec4(S5_GROUP, n)],
        out_specs=[pl.BlockSpec((S5_PREP_GROUPS, S5_CW, S5_CW), lambda i: (i, 0, 0)),
                   spec4(S5_CW, n2), spec4(n2, S5_CW), spec4(16, n2)],
        compiler_params=_params(1),
        name="s5_prep",
    )(arow, acol, bt(b_re), bt(b_im), c_re, c_im)


def _cmul_rows(x, p1, p2):
    return x * p1 + pltpu.roll(x, S5_N, 1) * p2


S5_OCT = LANES // S5_GROUP


def _block_transpose(xs):
    n = S5_OCT
    blk = lax.broadcasted_iota(jnp.int32, xs[0].shape, 1) >> S5_GROUP_SHIFT
    a = [pltpu.roll(x, i * S5_GROUP, 1) if i else x for i, x in enumerate(xs)]
    ys = []
    for d in range(n):
        diag = a[-d % n]
        for b in range(1, n):
            diag = jnp.where(blk == b, a[(b - d) % n], diag)
        ys.append(pltpu.roll(diag, LANES - d * S5_GROUP, 1) if d else diag)
    return ys


def _tokens_to_chunks(u_ref, ug_ref):
    rows = u_ref.shape[1] // S5_T
    for octet in range(S5_GROUPS // S5_OCT):
        for half in range(2):
            xs = [u_ref[octet, pl.ds(S5_OCT * half + tt, rows, stride=S5_T), :] for tt in range(S5_OCT)]
            for gl, x in enumerate(_block_transpose(xs)):
                ug_ref[octet * S5_OCT + gl, :, half * LANES:(half + 1) * LANES] = x


def _chunks_to_tokens(yg_ref, y_ref):
    rows = y_ref.shape[1] // S5_T
    for octet in range(S5_GROUPS // S5_OCT):
        for half in range(2):
            ys = [yg_ref[octet * S5_OCT + gl, :, half * LANES:(half + 1) * LANES] for gl in range(S5_OCT)]
            for tt, y in enumerate(_block_transpose(ys)):
                y_ref[octet, pl.ds(S5_OCT * half + tt, rows, stride=S5_T), :] = y


def _s5_core_kernel(ug_ref, wi_ref, ws_ref, wo_ref, ap_ref, h0_ref, dv_ref, yg_ref, fin_ref, z_ref):
    n2 = 2 * S5_N
    r = lax.broadcasted_iota(jnp.int32, (S5_ROWS, n2), 0)
    in_p = r < S5_ROWS_P
    rib = jnp.where(in_p, r & (CH_P - 1), (r - S5_ROWS_P) & (CH_S - 1))
    nch = jnp.where(in_p, CH_P, CH_S)

    def one_group(gl, slot):
        ub = ug_ref[gl].astype(BF16)
        y = jnp.dot(ub, wi_ref[gl], preferred_element_type=F32)
        for d in range(2):
            p1, p2 = ap_ref[d, gl, 0:1, :], ap_ref[d, gl, 8:9, :]
            edge = [S5_ROWS_P + CH_S * b + (0 if d == 0 else CH_S - 1) for b in range(NB_S)]
            h0 = [h0_ref[gl, d, b:b + 1, :] for b in range(NB_S)]
            s = jnp.dot(ub, ws_ref[d, gl], preferred_element_type=F32)
            for b in range(NB_S):
                s = s + jnp.where(r == edge[b], _cmul_rows(h0[b], p1, p2), 0.0)
            def scan_step(x, k, pos, count):
                sh = 1 << k
                if d == 0:
                    t = jnp.where(pos >= sh, pltpu.roll(x, sh, 0), 0.0)
                else:
                    t = jnp.where(pos < count - sh, pltpu.roll(x, x.shape[0] - sh, 0), 0.0)
                return x + _cmul_rows(t, ap_ref[d, gl, k:k + 1, :], ap_ref[d, gl, 8 + k:9 + k, :])

            for k in range(CH_P.bit_length() - 1):
                s = scan_step(s, k, rib, nch)
            tail = s[S5_ROWS_P:]
            for k in range(CH_P.bit_length() - 1, CH_S.bit_length() - 1):
                tail = scan_step(tail, k, rib[S5_ROWS_P:], CH_S)
            s = jnp.concatenate([s[:S5_ROWS_P], tail], axis=0)
            z_ref[slot, d] = s
            first = CH_P - 1 if d == 0 else 0
            fin_ref[gl, d] = z_ref[slot, d, pl.ds(first, NB_P, stride=CH_P), :]
            if d == 0:
                sp = jnp.where(rib >= 1, pltpu.roll(s, 1, 0), 0.0)
            else:
                sp = jnp.where(rib < nch - 1, pltpu.roll(s, S5_ROWS - 1, 0), 0.0)
            for b in range(NB_S):
                sp = jnp.where(r == edge[b], h0[b], sp)
            y = y + jnp.dot(sp.astype(BF16), wo_ref[d, gl], preferred_element_type=F32)
        yg_ref[gl] = y + dv_ref[gl] * ug_ref[gl]

    def group_pair(gp, carry):
        for slot in range(2):
            one_group(2 * gp + slot, slot)
        return carry

    lax.fori_loop(0, S5_OCT // 2, group_pair, 0)


def _s5_core(ug, prep, h0, d_skip):
    w_intra, w_state, w_out, apow = prep
    g, n2 = S5_GROUPS, 2 * S5_N
    spec4 = lambda r, c: pl.BlockSpec((2, S5_OCT, r, c), lambda i: (0, i, 0, 0))
    chunks = pl.BlockSpec((S5_OCT, S5_ROWS, S5_CW), lambda i: (i, 0, 0))
    dvec = jnp.tile(d_skip.reshape(g, 1, S5_GROUP), (1, 1, S5_T))
    return pl.pallas_call(
        _s5_core_kernel,
        out_shape=[jax.ShapeDtypeStruct((g, S5_ROWS, S5_CW), F32),
                   jax.ShapeDtypeStruct((g, 2, NB_P, n2), F32)],
        grid=(g // S5_OCT,),
        in_specs=[chunks,
                  pl.BlockSpec((S5_OCT, S5_CW, S5_CW), lambda i: (i, 0, 0)),
                  spec4(S5_CW, n2), spec4(n2, S5_CW), spec4(16, n2),
                  pl.BlockSpec((S5_OCT, 2, 8, n2), lambda i: (i, 0, 0, 0)),
                  pl.BlockSpec((S5_OCT, 1, S5_CW), lambda i: (i, 0, 0))],
        out_specs=[chunks, pl.BlockSpec((S5_OCT, 2, NB_P, n2), lambda i: (i, 0, 0, 0))],
        scratch_shapes=[pltpu.VMEM((2, 2, S5_ROWS, n2), F32)],
        compiler_params=_params(1, VMEM_LIMIT),
        name="s5_scan",
    )(ug, w_intra, w_state, w_out, apow, h0, dvec)


DF_SCALE = DF_DH ** -0.5 * LOG2E
DFW = DF_HEADS * 2 * DF_DH
IN_B = 3 * HY_WIDTH + 2 * DFW + DF_HEADS * DF_V


def _inproj_b_kernel(y_ref, mod_ref, g_ref, win_ref, c_ref, s_ref,
                     hy_ref, q_ref, kp_ref, ks_ref, vp_ref, vs_ref, kc_ref, vc_ref, wbf_ref):
    @pl.when(pl.program_id(0) == 0)
    def _():
        wbf_ref[...] = win_ref[...].astype(BF16)

    mod = _mod_rows(mod_ref)
    h = _modulate(y_ref[...], g_ref[...], mod[3], mod[4]).astype(BF16)
    p = jnp.dot(h, wbf_ref[...], preferred_element_type=F32)
    o1 = 3 * HY_WIDTH
    hy_ref[...] = p[:, :o1]
    q_ref[...] = (_rope(p[:, o1:o1 + DFW], c_ref[...], s_ref[...]) * DF_SCALE).astype(BF16)
    _tok_write(kp_ref, ks_ref, _rope(p[:, o1 + DFW:o1 + 2 * DFW], c_ref[...], s_ref[...]).astype(BF16))
    _tok_write(vp_ref, vs_ref, p[:, o1 + 2 * DFW:].astype(BF16))

    @pl.when(pl.program_id(0) < NT_P)
    def _():
        for lo, cache_ref in ((o1 + DFW, kc_ref), (o1 + 2 * DFW, vc_ref)):
            t = lax.dot_general(wbf_ref[:, lo:lo + DFW], h, (((0,), (1,)), ((), ())),
                                preferred_element_type=F32)
            for j in range(TM // L_P):
                cache_ref[j] = t[:, j * L_P:(j + 1) * L_P]


def _inproj_b(y, mods_l, g, w_in):
    cs, sn = _rope_tables(DFW, tuple(range(0, DFW, DF_DH)))
    pos = lambda i: (_pos_index(i), 0)
    kv_shapes, kv_specs = [], []
    for width in (DFW, DF_HEADS * DF_V):
        kv_shapes += [jax.ShapeDtypeStruct((TOK_P, width), BF16), jax.ShapeDtypeStruct((TOK_S, width), BF16)]
        kv_specs += _split_out(width)[1]
    seqs = TM // L_P
    cache_shapes = [jax.ShapeDtypeStruct((NB_P, w, L_P), F32) for w in (DFW, DF_HEADS * DF_V)]
    cache_specs = [pl.BlockSpec((seqs, w, L_P), lambda i: (jnp.minimum(i, NT_P - 1), 0, 0))
                   for w in (DFW, DF_HEADS * DF_V)]
    hy_u, q, kp, ks, vp, vs, k_cache, v_cache = pl.pallas_call(
        _inproj_b_kernel,
        out_shape=[jax.ShapeDtypeStruct((TOK, 3 * HY_WIDTH), F32), jax.ShapeDtypeStruct((TOK, DFW), BF16)]
                  + kv_shapes + cache_shapes,
        grid=(NT,),
        in_specs=[pl.BlockSpec((TM, D), _row),
                  _mod_spec(mods_l[1]),
                  _const_spec((1, D)), _const_spec((D, IN_B)),
                  pl.BlockSpec((TM, DFW), pos), pl.BlockSpec((TM, DFW), pos)],
        out_specs=[pl.BlockSpec((TM, 3 * HY_WIDTH), _row), pl.BlockSpec((TM, DFW), _row)] + kv_specs + cache_specs,
        scratch_shapes=[pltpu.VMEM((D, IN_B), BF16)],
        compiler_params=_params(1, VMEM_LIMIT),
        name="inproj_odd",
    )(y, mods_l[0], g[None], w_in, jnp.asarray(cs), jnp.asarray(sn))
    return hy_u, q, (kp, ks), (vp, vs), k_cache, v_cache


def _diff_attn_kernel(nseg, nseq, lam_init, q_ref, lam_ref, sub_ref, *refs):
    o_ref = refs[-1]
    lp = lam_ref[...]
    lam = (jnp.exp(jnp.sum(lp[0:1] * lp[1:2], axis=-1, keepdims=True))
           - jnp.exp(jnp.sum(lp[2:3] * lp[3:4], axis=-1, keepdims=True)) + lam_init)
    tq = q_ref.shape[0] // nseq
    lane = lax.broadcasted_iota(jnp.int32, (tq, LANES), 1)
    for j in range(nseq):
        qr = slice(j * tq, (j + 1) * tq)
        krs = [slice(j * (refs[2 * s].shape[0] // nseq), (j + 1) * (refs[2 * s].shape[0] // nseq))
               for s in range(nseg)]
        for pair in range(DF_HEADS // 2):
            cs = slice(pair * LANES, (pair + 1) * LANES)
            q = q_ref[qr, cs]
            ks = [refs[2 * s][krs[s], cs].astype(BF16) for s in range(nseg)]
            vs = [refs[2 * s + 1][krs[s], cs].astype(BF16) for s in range(nseg)]
            outs = []
            for hh in range(2):
                parts = []
                for half in range(2):
                    unit = 2 * hh + half
                    qm = jnp.where((lane >> DF_DH_SHIFT) == unit, q, jnp.zeros_like(q))
                    scores = [_dot_nt(qm, k) for k in ks]
                    parts.append(_softmax_pv(scores, vs, hh))
                o = parts[0] - lam * parts[1]
                mine = (lane >> HALF_SHIFT) == hh
                ms = jnp.sum(jnp.where(mine, o * o, 0.0), axis=-1, keepdims=True) * (1.0 / DF_V)
                outs.append(o * lax.rsqrt(ms + EPS))
            o = jnp.where(lane < DF_V, outs[0], outs[1]) * sub_ref[...] * (1.0 - lam_init)
            o_ref[qr, cs] = o.astype(o_ref.dtype)


def _diff_attention(q, k, v, lam_p, subln, lam_init, n_batch, seq, tq, row0, ctx=None, nseq=1):
    qt = seq // tq
    qb0, kb0 = row0 // (nseq * tq), 0
    in_specs = [pl.BlockSpec((nseq * tq, DFW), lambda b, j: (qb0 + b * qt + j, 0)),
                pl.BlockSpec((4, DF_DH), lambda b, j: (0, 0)),
                pl.BlockSpec((1, LANES), lambda b, j: (0, 0))]
    args = [q, lam_p, jnp.concatenate([subln, subln])[None]]
    segs = []
    if ctx is not None:
        segs.append((ctx, PAST, 0))
    segs.append(((k, v), seq, kb0))
    for (a_k, a_v), ln, off in segs:
        idx = lambda b, j, off=off: (off + b, 0)
        in_specs += [pl.BlockSpec((nseq * ln, DFW), idx), pl.BlockSpec((nseq * ln, DF_HEADS * DF_V), idx)]
        args += [a_k, a_v]
    return pl.pallas_call(
        functools.partial(_diff_attn_kernel, len(segs), nseq, lam_init),
        out_shape=jax.ShapeDtypeStruct((n_batch * seq, DF_HEADS * DF_V), BF16),
        grid=(n_batch // nseq, qt),
        in_specs=in_specs,
        out_specs=pl.BlockSpec((nseq * tq, DF_HEADS * DF_V), lambda b, j: (b * qt + j, 0)),
        compiler_params=_params(2, VMEM_LIMIT),
        name="diff_attention",
    )(*args)


def _hy_filter_kernel(feat_ref, w1_ref, b1_ref, w2_ref, b2_ref, fq_ref, w3_ref, dec_ref, o_ref):
    feat = feat_ref[...]
    fq = fq_ref[...]
    h = jnp.sin(fq * (_dot3(feat, w1_ref[...]) + b1_ref[...]))
    h = jnp.sin(fq * (_dot3(h, w2_ref[...]) + b2_ref[...]))
    window = jnp.exp(-feat[:, 0:1] * jnp.abs(dec_ref[...]))
    for j in range(4):
        cs = slice(j * HY_WIDTH, (j + 1) * HY_WIDTH)
        o_ref[:, cs] = _dot3(h, w3_ref[:, cs]) * window


def _hy_spectrum_kernel(L, cs_ref, hf_ref, hb_ref, o_ref):
    row = lax.broadcasted_iota(jnp.int32, (L, HY_WIDTH), 0)
    first = row == 0
    tf = _dot(cs_ref[...], hf_ref[...])
    tb = _dot(cs_ref[...], jnp.where(first, 0.0, hb_ref[...]))
    ka = tf[:L] + tb[:L]
    kb = jnp.where(first, tf[L:] + tb[L:], tf[L:] - tb[L:])
    wv = jnp.where(first, 1.0 / (2 * L), 2.0 / (2 * L))
    o_ref[0, 0] = ka * wv
    o_ref[0, 1] = jnp.where(first, 0.0, kb) * wv
    o_ref[0, 2] = jnp.where(first, kb, ka) * wv


HY_CH = 256


def _hy_conv_kernel(L, cs_ref, ct_ref, kf_ref, v_ref, x1_ref, x2_ref,
                    wv_ref, w1_ref, w2_ref, bias_ref, o_ref):
    row = lax.broadcasted_iota(jnp.int32, (L, HY_CH), 0)

    def short(x, w):
        prev = jnp.where(row >= 1, pltpu.roll(x, 1, 0), 0.0)
        nxt = jnp.where(row <= L - 2, pltpu.roll(x, L - 1, 0), 0.0)
        return w[0:1] * prev + w[1:2] * x + w[2:3] * nxt

    for j in range(v_ref.shape[0] // L):
        rs = slice(j * L, (j + 1) * L)
        for k in range(HY_WIDTH // HY_CH):
            ch = slice(k * HY_CH, (k + 1) * HY_CH)
            z = short(v_ref[rs, ch], wv_ref[:, ch])
            gates = (short(x1_ref[rs, ch], w1_ref[:, ch]), short(x2_ref[rs, ch], w2_ref[:, ch]))
            for n in range(2):
                ab = _dot(cs_ref[...], z)
                a, b = ab[:L], ab[L:]
                ka, kb1, ka2 = kf_ref[n, 0, :, ch], kf_ref[n, 1, :, ch], kf_ref[n, 2, :, ch]
                pq = jnp.concatenate([a * ka - b * kb1, a * kb1 + b * ka2], axis=0)
                conv = _dot(ct_ref[...], pq)
                z = gates[n] * (conv + bias_ref[n:n + 1, ch] * z)
            o_ref[rs, ch] = z.astype(o_ref.dtype)


def _hyena_spectrum(L, phy):
    conv_w, w1, b1, w2, b2, freq, w3, decay, bias = phy
    feat = jnp.asarray(_hyena_features(L))
    w1p = jnp.pad(w1, ((0, LANES - HY_EMB), (0, 0)))
    filt = pl.pallas_call(
        _hy_filter_kernel,
        out_shape=jax.ShapeDtypeStruct((L, 4 * HY_WIDTH), F32),
        grid=(1,),
        in_specs=[_const_spec((L, LANES)), _const_spec((LANES, HY_FH)), _const_spec((1, HY_FH)),
                  _const_spec((HY_FH, HY_FH)), _const_spec((1, HY_FH)), _const_spec((1, HY_FH)),
                  _const_spec((HY_FH, 4 * HY_WIDTH)), _const_spec((1, HY_WIDTH))],
        out_specs=pl.BlockSpec((L, 4 * HY_WIDTH), lambda i: (0, 0)),
        compiler_params=_params(1, VMEM_LIMIT),
        name="hyena_filter",
    )(feat, w1p, b1[None], w2, b2[None], freq[None], w3, decay[None])
    cs = jnp.asarray(_dft_tables(L)[0]).astype(BF16)
    return pl.pallas_call(
        functools.partial(_hy_spectrum_kernel, L),
        out_shape=jax.ShapeDtypeStruct((2, 3, L, HY_WIDTH), F32),
        grid=(2,),
        in_specs=[_const_spec((2 * L, L)),
                  pl.BlockSpec((L, HY_WIDTH), lambda n: (0, n)),
                  pl.BlockSpec((L, HY_WIDTH), lambda n: (0, 2 + n))],
        out_specs=pl.BlockSpec((1, 3, L, HY_WIDTH), lambda n: (n, 0, 0, 0)),
        compiler_params=_params(1, VMEM_LIMIT),
        name="hyena_spectrum",
    )(cs, filt, filt)


def _hyena_conv(hy_u, spec, phy, n_batch, L, seqs, row0):
    conv_w, bias = phy[0], phy[8]
    cs, ct = (jnp.asarray(t).astype(BF16) for t in _dft_tables(L))
    rows = seqs * L
    rb0 = row0 // rows
    col = lambda off: (lambda b: (0, off))
    tok = lambda off: (lambda b: (rb0 + b, off))
    blk = lambda idx: pl.BlockSpec((rows, HY_WIDTH), idx)
    return pl.pallas_call(
        functools.partial(_hy_conv_kernel, L),
        out_shape=jax.ShapeDtypeStruct((n_batch * L, HY_WIDTH), BF16),
        grid=(n_batch // seqs,),
        in_specs=[_const_spec((2 * L, L)), _const_spec((L, 2 * L)), _const_spec((2, 3, L, HY_WIDTH)),
                  blk(tok(0)), blk(tok(1)), blk(tok(2)),
                  pl.BlockSpec((3, HY_WIDTH), col(0)), pl.BlockSpec((3, HY_WIDTH), col(1)),
                  pl.BlockSpec((3, HY_WIDTH), col(2)), _const_spec((2, HY_WIDTH))],
        out_specs=blk(lambda b: (b, 0)),
        compiler_params=_params(1, VMEM_LIMIT),
        name="hyena_conv",
    )(cs, ct, spec, hy_u, hy_u, hy_u, conv_w, conv_w, conv_w, bias)


def _even_mixer(y, mods_l, g, pa, ps5, ctx_ckv, ctx_krope, ctx_state):
    w_in, w_out, q_norm, w_uq, kv_norm, w_ukv = pa
    a_re, a_im, log_step, b_re, b_im, c_re, c_im, d_skip, w_glu = ps5
    q, ckv, kr_unrot, kr_rot, kn, v, ug, (w_k, w_v) = _inproj_a(y, mods_l, g, w_in, q_norm, w_uq, kv_norm, w_ukv)

    ctx_flat = ctx_ckv.reshape(NB_S * PAST, MLA_KV_RANK)
    ctx_kn = _linear(ctx_flat, w_k, PAST, BF16)
    ctx_v = _linear(ctx_flat, w_v, PAST, BF16)
    ctx_kr = jnp.pad(ctx_krope.reshape(NB_S * PAST, MLA_ROPE),
                     ((0, 0), (KR_AT, LANES - KR_AT - MLA_ROPE))).astype(BF16)
    att_p = _mla_attention(q, kn, kr_rot, v, NB_P, L_P, L_P, 0, nseq=2)
    att_s = _mla_attention(q, kn, kr_rot, v, NB_S, L_S, TM, TOK_P, ctx=(ctx_kn, ctx_kr, ctx_v))

    prep = _s5_prep(a_re, a_im, log_step, b_re, b_im, c_re, c_im)
    h0 = ctx_state.transpose(3, 1, 0, 2, 4).reshape(S5_GROUPS, 2, NB_S, 2 * S5_N)
    h0 = jnp.pad(h0, ((0, 0), (0, 0), (0, 8 - NB_S), (0, 0)))
    s5y, fin = _s5_core(ug, prep, h0, d_skip)

    mixer = ((att_p, att_s), s5y, w_out, w_glu)
    new_ckv = ckv.reshape(NB_P, L_P, MLA_KV_RANK)
    new_krope = kr_unrot.reshape(NB_P, L_P, MLA_ROPE)
    new_state = fin.reshape(S5_GROUPS, 2, NB_P, 2, S5_N).transpose(2, 1, 3, 0, 4)
    return mixer, new_ckv, new_krope, new_state


def _odd_mixer(y, mods_l, g, pb, phy, ctx_k, ctx_v, lam_init):
    w_in, w_out, lam_p, subln = pb
    hy_u, q, (k_p, k_s), (v_p, v_s), k_cache, v_cache = _inproj_b(y, mods_l, g, w_in)
    hy_p = _hyena_conv(hy_u, _hyena_spectrum(L_P, phy), phy, NB_P, L_P, 4, 0)
    hy_s = _hyena_conv(hy_u, _hyena_spectrum(L_S, phy), phy, NB_S, L_S, 1, TOK_P)
    ctx = (ctx_k.reshape(NB_S * PAST, DFW), ctx_v.reshape(NB_S * PAST, DF_HEADS * DF_V))
    att_p = _diff_attention(q, k_p, v_p, lam_p, subln, lam_init, NB_P, L_P, L_P, 0, nseq=2)
    att_s = _diff_attention(q, k_s, v_s, lam_p, subln, lam_init, NB_S, L_S, TM // 2, TOK_P, ctx=ctx)
    mixer = ((hy_p, hy_s), (att_p, att_s), w_out, None)
    new_k = k_cache.reshape(NB_P, DF_HEADS, 2, DF_DH, L_P).transpose(0, 4, 1, 2, 3)
    new_v = v_cache.reshape(NB_P, DF_HEADS, DF_V, L_P).transpose(0, 3, 1, 2)
    return mixer, new_k, new_v


def kernel(x_prompt, x_sample, c, c_ctx, cache_mla_ckv, cache_mla_krope, state_s5, cache_diff_k, cache_diff_v, ada_w, ada_b, norm_g, ff_w_in, ff_w_out, w_in_a, w_out_a, mla_q_norm, mla_w_uq, mla_kv_norm, mla_w_ukv, s5_a_re, s5_a_im, s5_log_step, s5_b_re, s5_b_im, s5_c_re, s5_c_im, s5_d, s5_w_glu, w_in_b, w_out_b, hy_conv, hy_w1, hy_b1, hy_w2, hy_b2, hy_freq, hy_w3, hy_decay, hy_bias, df_lambda, df_subln, final_norm):
    depth = ada_w.shape[0]
    y = (x_prompt.reshape(TOK_P, D), x_sample.reshape(TOK_S, D))
    mods = _adaln(jnp.concatenate([c_ctx[None], c], axis=0), ada_w, ada_b)
    new_ckv, new_krope, new_s5, new_dk, new_dv = [], [], [], [], []
    for l in range(depth):
        y = _half_ffn(y, (mods, l), norm_g[l, 0], ff_w_in, ff_w_out, l, 0)
        if l % 2 == 0:
            e = l // 2
            pa = (w_in_a[e], w_out_a[e], mla_q_norm[e], mla_w_uq[e], mla_kv_norm[e], mla_w_ukv[e])
            ps5 = (s5_a_re[e], s5_a_im[e], s5_log_step[e], s5_b_re[e], s5_b_im[e],
                   s5_c_re[e], s5_c_im[e], s5_d[e], s5_w_glu[e])
            mixer, ckv, krope, st = _even_mixer(y, (mods, l), norm_g[l, 1], pa, ps5, cache_mla_ckv[:, e],
                                                cache_mla_krope[:, e], state_s5[:, e])
            new_ckv.append(ckv)
            new_krope.append(krope)
            new_s5.append(st)
        else:
            o = l // 2
            lam_init = 0.8 - 0.6 * math.exp(-0.3 * l)
            pb = (w_in_b[o], w_out_b[o], df_lambda[o], df_subln[o])
            phy = (hy_conv[o], hy_w1[o], hy_b1[o], hy_w2[o], hy_b2[o], hy_freq[o],
                   hy_w3[o], hy_decay[o], hy_bias[o])
            mixer, dk, dv = _odd_mixer(y, (mods, l), norm_g[l, 1], pb, phy, cache_diff_k[:, o],
                                       cache_diff_v[:, o], lam_init)
            new_dk.append(dk)
            new_dv.append(dv)
        last = l == depth - 1
        y = _half_ffn(y, (mods, l), norm_g[l, 2], ff_w_in, ff_w_out, l, 1,
                      final_g=final_norm if last else None, mixer=mixer)
    y_prompt = y[0].reshape(NB_P, L_P, D)
    y_sample = y[1].reshape(NB_S, L_S, D)
    return (y_prompt, y_sample, jnp.stack(new_ckv, axis=1), jnp.stack(new_krope, axis=1),
            jnp.stack(new_s5, axis=1), jnp.stack(new_dk, axis=1), jnp.stack(new_dv, axis=1))
```

```python
import functools
import math

import numpy as np
import jax
import jax.numpy as jnp
from jax import lax
from jax.experimental import pallas as pl
from jax.experimental.pallas import tpu as pltpu

F32 = jnp.float32
BF16 = jnp.bfloat16

D = 1024
NB_P, L_P = 16, 256
NB_S, L_S = 2, 1024
PAST = 256
GRID_W = 64
N_MOD = 9
FF = 2816
EPS = 1e-6
ROPE_BASE = 10000.0

MLA_HEADS, MLA_NOPE, MLA_ROPE, MLA_V = 8, 64, 32, 64
MLA_Q_RANK, MLA_KV_RANK = 384, 256
S5_WIDTH, S5_GROUP, S5_N = 512, 16, 64
S5_GROUPS = S5_WIDTH // S5_GROUP
HY_WIDTH, HY_BANDS, HY_FH = 512, 16, 64
HY_EMB = 2 * HY_BANDS + 1
DF_HEADS, DF_DH = 8, 32
DF_V = 2 * DF_DH

TOK_P = NB_P * L_P
TOK_S = NB_S * L_S
TOK = TOK_P + TOK_S
TM = 512
NT = TOK // TM
NT_P = TOK_P // TM
TILES_PER_SAMPLE = L_S // TM

LANES = 128
HALF_SHIFT = (LANES // 2).bit_length() - 1
S5_GROUP_SHIFT = S5_GROUP.bit_length() - 1
DF_DH_SHIFT = DF_DH.bit_length() - 1
S5_T = 16
S5_CW = S5_T * S5_GROUP
CH_P = L_P // S5_T
CH_S = L_S // S5_T
S5_ROWS = NB_P * CH_P + NB_S * CH_S
S5_ROWS_P = NB_P * CH_P

VMEM_LIMIT = 56 * 1024 * 1024


def _params(n_grid, vmem=None):
    return pltpu.CompilerParams(dimension_semantics=("arbitrary",) * n_grid,
                                vmem_limit_bytes=vmem)


def _const_spec(shape):
    nd = len(shape)
    return pl.BlockSpec(shape, lambda *_: (0,) * nd, pipeline_mode=pl.Buffered(1))


def _mod_index(i):
    return jnp.where(i < NT_P, 0, 1 + (i - NT_P) // TILES_PER_SAMPLE)


def _mod_spec(layer):
    return pl.BlockSpec((1, 8, N_MOD * D), lambda *_: (layer, 0, 0), pipeline_mode=pl.Buffered(1))


def _mod_rows(mod_ref):
    row = mod_ref[0, pl.ds(_mod_index(pl.program_id(0)), 1), :]
    return [row[:, k * D:(k + 1) * D] for k in range(N_MOD)]


def _pos_index(i):
    return jnp.where(i < NT_P, 0, 1 + (i - NT_P) % TILES_PER_SAMPLE)


def _row(i):
    return (i, 0)


def _row_p(i):
    return (jnp.minimum(i, NT_P - 1), 0)


def _row_s(i):
    return (jnp.maximum(i - NT_P, 0), 0)


def _tok_specs(x, width):
    if isinstance(x, tuple):
        return [pl.BlockSpec((TM, width), _row_p), pl.BlockSpec((TM, width), _row_s)], list(x)
    return [pl.BlockSpec((TM, width), _row)], [x]


def _tok_read(refs, split):
    if split:
        return jnp.where(pl.program_id(0) < NT_P, refs[0][...], refs[1][...]), refs[2:]
    return refs[0][...], refs[1:]


def _tok_write(p_ref, s_ref, value):
    i = pl.program_id(0)

    @pl.when(i < NT_P)
    def _():
        p_ref[...] = value

    @pl.when(i >= NT_P)
    def _():
        s_ref[...] = value.astype(s_ref.dtype)


def _split_out(width, sample_dtype=F32):
    shapes = [jax.ShapeDtypeStruct((TOK_P, width), F32), jax.ShapeDtypeStruct((TOK_S, width), sample_dtype)]
    specs = [pl.BlockSpec((TM, width), _row_p), pl.BlockSpec((TM, width), _row_s)]
    return shapes, specs


def _dot(a, b):
    return jnp.dot(a.astype(BF16), b.astype(BF16), preferred_element_type=F32)


def _dot_nt(a, b):
    return lax.dot_general(a, b, (((1,), (1,)), ((), ())), preferred_element_type=F32)


def _split(x):
    hi = x.astype(BF16)
    lo = (x - hi.astype(F32)).astype(BF16)
    return hi, lo


def _dot3(a, b):
    ah, al = _split(a)
    bh, bl = _split(b)
    d = functools.partial(jnp.dot, preferred_element_type=F32)
    return d(ah, bh) + d(ah, bl) + d(al, bh)


def _rmsnorm(x, g):
    return x * lax.rsqrt(jnp.mean(x * x, axis=-1, keepdims=True) + EPS) * g


def _modulate(y, g, shift, scale):
    return _rmsnorm(y, g) * (1.0 + scale) + shift


def _pair_swap(x):
    n = x.shape[-1]
    lane = lax.broadcasted_iota(jnp.int32, x.shape, x.ndim - 1)
    return jnp.where((lane & 1) == 0, pltpu.roll(x, n - 1, x.ndim - 1), pltpu.roll(x, 1, x.ndim - 1))


def _rope(x, cos, sin_signed):
    return x * cos + _pair_swap(x) * sin_signed


def _rope_angles():
    n_freq = MLA_ROPE // 4
    inv = 1.0 / (ROPE_BASE ** (np.arange(n_freq, dtype=np.float64) / n_freq))
    pos = np.arange(L_S)
    row = (pos // GRID_W).astype(np.float64)
    col = (pos % GRID_W).astype(np.float64)
    ang = np.concatenate([row[:, None] * inv, col[:, None] * inv], axis=-1)
    return np.cos(ang), np.sin(ang)


@functools.lru_cache(maxsize=None)
def _rope_tables(width, starts):
    cos, sin = _rope_angles()
    c = np.ones((TM + L_S, width), np.float32)
    s = np.zeros((TM + L_S, width), np.float32)
    sign = np.where(np.arange(MLA_ROPE) % 2 == 0, -1.0, 1.0)
    unit_c = np.repeat(cos, 2, axis=1)
    unit_s = np.repeat(sin, 2, axis=1) * sign
    for st in starts:
        c[TM:, st:st + MLA_ROPE] = unit_c
        s[TM:, st:st + MLA_ROPE] = unit_s
    return c, s


@functools.lru_cache(maxsize=None)
def _dft_tables(L):
    f = np.arange(L)[:, None]
    s = np.arange(L)[None, :]
    ang = np.pi * ((f * s) % (2 * L)).astype(np.float64) / L
    cs = np.concatenate([np.cos(ang), np.sin(ang)], axis=0)
    cs[L, :] = np.where(np.arange(L) % 2 == 0, 1.0, -1.0)
    cs = cs.astype(np.float32)
    return cs, np.ascontiguousarray(cs.T)


@functools.lru_cache(maxsize=None)
def _hyena_features(L):
    t = np.arange(L, dtype=np.float64) / L
    bands = np.arange(1, HY_BANDS + 1, dtype=np.float64)
    ang = 2.0 * math.pi * t[:, None] * bands
    feat = np.zeros((L, LANES), np.float32)
    feat[:, 0] = t
    feat[:, 1:1 + HY_BANDS] = np.cos(ang)
    feat[:, 1 + HY_BANDS:HY_EMB] = np.sin(ang)
    return feat


def _adaln_kernel(c_ref, w_ref, b_ref, o_ref):
    s = jax.nn.silu(c_ref[...])
    s_hi = s.astype(BF16).astype(F32)
    stacked = jnp.concatenate([s_hi, s - s_hi], axis=0).astype(BF16)
    wh, wl = _split(w_ref[0])
    both = jnp.dot(stacked, wh, preferred_element_type=F32)
    rows = c_ref.shape[0]
    o_ref[0] = both[:rows] + both[rows:] + jnp.dot(stacked, wl, preferred_element_type=F32)[:rows] + b_ref[0]


def _adaln(cvecs, ada_w, ada_b):
    depth = ada_w.shape[0]
    n_vec = cvecs.shape[0]
    tn = N_MOD * D // 4
    out = pl.pallas_call(
        _adaln_kernel,
        out_shape=jax.ShapeDtypeStruct((depth, 8, N_MOD * D), F32),
        grid=(depth, N_MOD * D // tn),
        in_specs=[pl.BlockSpec((8, D), lambda l, j: (0, 0)),
                  pl.BlockSpec((1, D, tn), lambda l, j: (l, 0, j)),
                  pl.BlockSpec((1, 1, tn), lambda l, j: (l, 0, j))],
        out_specs=pl.BlockSpec((1, 8, tn), lambda l, j: (l, 0, j)),
        compiler_params=_params(2, VMEM_LIMIT),
        name="adaln",
    )(jnp.pad(cvecs, ((0, 8 - n_vec), (0, 0))), ada_w, ada_b[:, None, :])
    return out


FF_PIECE = 256
FF_LOADS = FF // FF_PIECE


def _ffn_kernel(base, final, split_in, mixer, layer, which, *refs):
    y, refs = _tok_read(refs, split_in)
    if mixer is not None:
        a1, refs = _tok_read(refs, mixer[0])
        if mixer[2]:
            a2_chunks, refs = refs[0], refs[1:]
        else:
            a2, refs = _tok_read(refs, mixer[1])
        wmix_ref, wg_ref = refs[:2]
        refs = refs[2:]
    mod_ref, g_ref, win_hbm, wout_hbm, fg_ref = refs[:5]
    n_out = 2 if final else 1
    outs = refs[5:5 + n_out]
    win_ref, wout_ref, stage_g, stage_u, stage_o, sems = refs[5 + n_out:11 + n_out]
    mod = _mod_rows(mod_ref)
    if mixer is not None:
        if mixer[2]:
            a2_scr = refs[11 + n_out]
            _chunks_to_tokens(a2_chunks, a2_scr)
            a2 = jax.nn.gelu(jnp.concatenate([a2_scr[o] for o in range(a2_scr.shape[0])], axis=1))
            a2 = a2 * jax.nn.sigmoid(_dot(a2, wg_ref[...]))
        k1 = wmix_ref.shape[0] // 2
        y = y + mod[5] * (_dot(a1, wmix_ref[:k1]) + _dot(a2, wmix_ref[k1:]))
    h = _modulate(y, g_ref[...], mod[base], mod[base + 1]).astype(BF16)

    def hidden(lo, width):
        gate = jnp.dot(h, win_ref[:, lo:lo + width], preferred_element_type=F32)
        up = jnp.dot(h, win_ref[:, FF + lo:FF + lo + width], preferred_element_type=F32)
        a = (jax.nn.silu(gate) * up).astype(BF16)
        return jnp.dot(a, wout_ref[lo:lo + width, :], preferred_element_type=F32)

    def finish(acc):
        out = y + 0.5 * mod[base + 2] * acc
        if final:
            _tok_write(outs[0], outs[1], _rmsnorm(out, fg_ref[...]))
        else:
            outs[0][...] = out

    @pl.when(pl.program_id(0) == 0)
    def _():
        def copies(c, slot):
            cols = pl.ds(c * FF_PIECE, FF_PIECE)
            return (pltpu.make_async_copy(win_hbm.at[layer, which, :, cols], stage_g.at[slot], sems.at[0, slot]),
                    pltpu.make_async_copy(win_hbm.at[layer, which, :, pl.ds(FF + c * FF_PIECE, FF_PIECE)],
                                          stage_u.at[slot], sems.at[1, slot]),
                    pltpu.make_async_copy(wout_hbm.at[layer, which, cols, :], stage_o.at[slot], sems.at[2, slot]))

        for cp in copies(0, 0):
            cp.start()
        acc = jnp.zeros(y.shape, F32)
        for c in range(FF_LOADS):
            slot = c % 2
            lo = c * FF_PIECE
            if c + 1 < FF_LOADS:
                for cp in copies(c + 1, 1 - slot):
                    cp.start()
            for cp in copies(c, slot):
                cp.wait()
            win_ref[:, lo:lo + FF_PIECE] = stage_g[slot].astype(BF16)
            win_ref[:, FF + lo:FF + lo + FF_PIECE] = stage_u[slot].astype(BF16)
            wout_ref[lo:lo + FF_PIECE, :] = stage_o[slot].astype(BF16)
            acc = acc + hidden(lo, FF_PIECE)
        finish(acc)

    @pl.when(pl.program_id(0) > 0)
    def _():
        finish(hidden(0, FF))


def _half_ffn(y, mods_l, g, ff_w_in, ff_w_out, layer, which, final_g=None, mixer=None):
    final = final_g is not None
    fg = final_g if final else g
    y_specs, y_args = _tok_specs(y, D)
    mix_flags = None
    extra_scratch = []
    if mixer is not None:
        a1, a2, w_out, w_glu = mixer
        k1 = w_out.shape[0] // 2
        wg = w_glu if w_glu is not None else jnp.zeros((8, LANES), F32)
        s1, a1_args = _tok_specs(a1, k1)
        if w_glu is not None:
            s2 = [pl.BlockSpec((S5_GROUPS, TM // S5_T, S5_CW), lambda i: (0, i, 0))]
            a2_args = [a2]
            extra_scratch = [pltpu.VMEM((k1 // LANES, TM, LANES), F32)]
        else:
            s2, a2_args = _tok_specs(a2, k1)
        y_specs = y_specs + s1 + s2 + [_const_spec(w_out.shape), _const_spec(wg.shape)]
        y_args = y_args + a1_args + a2_args + [w_out, wg]
        mix_flags = (isinstance(a1, tuple), isinstance(a2, tuple), w_glu is not None)
    if final:
        out_shape, out_specs = _split_out(D)
    else:
        out_shape, out_specs = jax.ShapeDtypeStruct((TOK, D), F32), pl.BlockSpec((TM, D), _row)
    return pl.pallas_call(
        functools.partial(_ffn_kernel, 6 * which, final, isinstance(y, tuple), mix_flags, layer, which),
        out_shape=out_shape,
        grid=(NT,),
        in_specs=y_specs + [_mod_spec(mods_l[1]),
                            _const_spec((1, D)),
                            pl.BlockSpec(memory_space=pl.ANY),
                            pl.BlockSpec(memory_space=pl.ANY),
                            _const_spec((1, D))],
        out_specs=out_specs,
        scratch_shapes=[pltpu.VMEM((D, 2 * FF), BF16), pltpu.VMEM((FF, D), BF16),
                        pltpu.VMEM((2, D, FF_PIECE), F32), pltpu.VMEM((2, D, FF_PIECE), F32),
                        pltpu.VMEM((2, FF_PIECE, D), F32), pltpu.SemaphoreType.DMA((3, 2))] + extra_scratch,
        compiler_params=_params(1, VMEM_LIMIT),
        name="half_ffn",
    )(*y_args, mods_l[0], g[None], ff_w_in, ff_w_out, fg[None])


def _linear_kernel(x_ref, w_ref, o_ref):
    o_ref[...] = _dot(x_ref[...], w_ref[...]).astype(o_ref.dtype)


def _linear(x, w, tm, out_dtype):
    m, k = x.shape
    n = w.shape[1]
    return pl.pallas_call(
        _linear_kernel,
        out_shape=jax.ShapeDtypeStruct((m, n), out_dtype),
        grid=(m // tm,),
        in_specs=[pl.BlockSpec((tm, k), lambda i: (i, 0)), _const_spec((k, n))],
        out_specs=pl.BlockSpec((tm, n), lambda i: (i, 0)),
        compiler_params=_params(1),
        name="linear",
    )(x, w.astype(BF16))


LOG2E = math.log2(math.e)
MLA_SCALE = (MLA_NOPE + MLA_ROPE) ** -0.5 * LOG2E
QW = MLA_HEADS * LANES
KR_AT = MLA_NOPE
IN_A_PAD = MLA_Q_RANK + MLA_KV_RANK + S5_WIDTH + LANES


def _inproj_a_kernel(y_ref, mod_ref, g_ref, win_ref, qn_ref, wuq_ref, kvn_ref, wk_ref, wv_ref,
                     cq_ref, sq_ref, ck_ref, sk_ref,
                     q_ref, ckv_ref, kru_ref, krr_ref, kn_ref, v_ref, ug_ref, u_scr):
    mod = _mod_rows(mod_ref)
    h = _modulate(y_ref[...], g_ref[...], mod[3], mod[4]).astype(BF16)
    p = jnp.dot(h, win_ref[...], preferred_element_type=F32)
    o1 = MLA_Q_RANK
    o2 = o1 + MLA_KV_RANK
    o3 = o2 + S5_WIDTH
    q = _dot(_rmsnorm(p[:, :o1], qn_ref[...]), wuq_ref[...])
    q_ref[...] = (_rope(q, cq_ref[...], sq_ref[...]) * MLA_SCALE).astype(BF16)
    ckv = _rmsnorm(p[:, o1:o2], kvn_ref[...])
    ckv_b = ckv.astype(BF16)
    kn_ref[...] = jnp.dot(ckv_b, wk_ref[...], preferred_element_type=F32).astype(BF16)
    v_ref[...] = jnp.dot(ckv_b, wv_ref[...], preferred_element_type=F32).astype(BF16)
    for octet in range(S5_WIDTH // LANES):
        u_scr[octet] = p[:, o2 + octet * LANES:o2 + (octet + 1) * LANES]
    _tokens_to_chunks(u_scr, ug_ref)
    krp = p[:, o3:]
    krr_ref[...] = _rope(krp, ck_ref[...], sk_ref[...]).astype(BF16)

    @pl.when(pl.program_id(0) < NT_P)
    def _():
        ckv_ref[...] = ckv
        kru_ref[...] = krp[:, KR_AT:KR_AT + MLA_ROPE]


def _inproj_a(y, mods_l, g, w_in, q_norm, w_uq, kv_norm, w_ukv):
    o1 = MLA_Q_RANK
    o2 = o1 + MLA_KV_RANK
    o3 = o2 + MLA_ROPE
    kr_cols = jnp.pad(w_in[:, o2:o3], ((0, 0), (KR_AT, LANES - KR_AT - MLA_ROPE)))
    w_ext = jnp.concatenate([w_in[:, :o2], w_in[:, o3:], kr_cols], axis=1).astype(BF16)
    dq = MLA_NOPE + MLA_ROPE
    w_uq_pad = jnp.pad(w_uq.reshape(MLA_Q_RANK, MLA_HEADS, dq),
                       ((0, 0), (0, 0), (0, LANES - dq))).reshape(MLA_Q_RANK, QW).astype(BF16)
    w_kv = w_ukv.reshape(MLA_KV_RANK, MLA_HEADS, MLA_NOPE + MLA_V)
    w_k = jnp.pad(w_kv[:, :, :MLA_NOPE], ((0, 0), (0, 0), (0, LANES - MLA_NOPE))).reshape(MLA_KV_RANK, QW)
    w_v = w_kv[:, :, MLA_NOPE:].reshape(MLA_KV_RANK, MLA_HEADS * MLA_V)
    w_k, w_v = w_k.astype(BF16), w_v.astype(BF16)
    cq, sq = _rope_tables(QW, tuple(h * LANES + MLA_NOPE for h in range(MLA_HEADS)))
    ck, sk = _rope_tables(LANES, (KR_AT,))
    row = _row
    pos = lambda i: (_pos_index(i), 0)
    widths = (QW, MLA_KV_RANK, MLA_ROPE, LANES, QW, MLA_HEADS * MLA_V)
    prompt_only = (1, 2)
    mxu_only = (0, 3, 4, 5)
    outs = pl.pallas_call(
        _inproj_a_kernel,
        out_shape=[jax.ShapeDtypeStruct((TOK_P if k in prompt_only else TOK, w), BF16 if k in mxu_only else F32)
                   for k, w in enumerate(widths)]
                  + [jax.ShapeDtypeStruct((S5_GROUPS, S5_ROWS, S5_CW), F32)],
        grid=(NT,),
        in_specs=[pl.BlockSpec((TM, D), row),
                  _mod_spec(mods_l[1]),
                  _const_spec((1, D)),
                  _const_spec((D, IN_A_PAD)),
                  _const_spec((1, MLA_Q_RANK)),
                  _const_spec((MLA_Q_RANK, QW)),
                  _const_spec((1, MLA_KV_RANK)),
                  _const_spec((MLA_KV_RANK, QW)),
                  _const_spec((MLA_KV_RANK, MLA_HEADS * MLA_V)),
                  pl.BlockSpec((TM, QW), pos), pl.BlockSpec((TM, QW), pos),
                  pl.BlockSpec((TM, LANES), pos), pl.BlockSpec((TM, LANES), pos)],
        out_specs=[pl.BlockSpec((TM, w), _row_p if k in prompt_only else row)
                   for k, w in enumerate(widths)]
                  + [pl.BlockSpec((S5_GROUPS, TM // S5_T, S5_CW), lambda i: (0, i, 0))],
        scratch_shapes=[pltpu.VMEM((S5_WIDTH // LANES, TM, LANES), F32)],
        compiler_params=_params(1, VMEM_LIMIT),
        name="inproj_even",
    )(y, mods_l[0], g[None], w_ext, q_norm[None], w_uq_pad, kv_norm[None], w_k, w_v,
      jnp.asarray(cq), jnp.asarray(sq), jnp.asarray(ck), jnp.asarray(sk))
    q, ckv, kr_unrot, kr_rot, kn, v, ug = outs
    return q, ckv, kr_unrot, kr_rot, kn, v, ug, (w_k, w_v)


def _softmax_pv(scores, vals, half):
    m = functools.reduce(jnp.maximum, [jnp.max(s, axis=-1, keepdims=True) for s in scores])
    pv = None
    for s, v in zip(scores, vals):
        own_k = (lax.broadcasted_iota(jnp.int32, v.shape, 1) >> HALF_SHIFT) == half
        part = jnp.dot(jnp.exp2(s - m).astype(BF16), jnp.where(own_k, v, jnp.ones_like(v)),
                       preferred_element_type=F32)
        pv = part if pv is None else pv + part
    own = (lax.broadcasted_iota(jnp.int32, pv.shape, 1) >> HALF_SHIFT) == half
    denom = jnp.max(jnp.where(own, 0.0, pv), axis=-1, keepdims=True)
    return pv * (1.0 / denom)


def _mla_attn_kernel(nseg, nseq, q_ref, *refs):
    o_ref = refs[-1]
    tq = q_ref.shape[0] // nseq
    lane = lax.broadcasted_iota(jnp.int32, (tq, LANES), 1)
    for j in range(nseq):
        qr = slice(j * tq, (j + 1) * tq)
        krs = [slice(j * (refs[3 * s].shape[0] // nseq), (j + 1) * (refs[3 * s].shape[0] // nseq))
               for s in range(nseg)]
        for pair in range(MLA_HEADS // 2):
            outs = []
            for hh in range(2):
                h = 2 * pair + hh
                hs = slice(h * LANES, (h + 1) * LANES)
                qh = q_ref[qr, hs]
                scores = []
                for s in range(nseg):
                    kn_ref, kr_ref = refs[3 * s], refs[3 * s + 1]
                    kh = (kn_ref[krs[s], hs] + kr_ref[krs[s], :]).astype(BF16)
                    scores.append(_dot_nt(qh, kh))
                vals = [refs[3 * s + 2][krs[s], pair * LANES:(pair + 1) * LANES] for s in range(nseg)]
                outs.append(_softmax_pv(scores, vals, hh))
            o_ref[qr, pair * LANES:(pair + 1) * LANES] = jnp.where(lane < MLA_V, outs[0], outs[1]).astype(o_ref.dtype)


def _mla_attention(q, kn, kr, v, n_batch, seq, tq, row0, ctx=None, nseq=1):
    qt = seq // tq
    qb0, kb0 = row0 // (nseq * tq), row0 // (nseq * seq)
    in_specs = [pl.BlockSpec((nseq * tq, QW), lambda b, j: (qb0 + b * qt + j, 0))]
    args = [q]
    segs = []
    if ctx is not None:
        segs.append((ctx, PAST, 0))
    segs.append(((kn, kr, v), seq, kb0))
    for (a_kn, a_kr, a_v), ln, off in segs:
        idx = lambda b, j, off=off: (off + b, 0)
        in_specs += [pl.BlockSpec((nseq * ln, QW), idx), pl.BlockSpec((nseq * ln, LANES), idx),
                     pl.BlockSpec((nseq * ln, MLA_HEADS * MLA_V), idx)]
        args += [a_kn, a_kr, a_v]
    return pl.pallas_call(
        functools.partial(_mla_attn_kernel, len(segs), nseq),
        out_shape=jax.ShapeDtypeStruct((n_batch * seq, MLA_HEADS * MLA_V), BF16),
        grid=(n_batch // nseq, qt),
        in_specs=in_specs,
        out_specs=pl.BlockSpec((nseq * tq, MLA_HEADS * MLA_V), lambda b, j: (b * qt + j, 0)),
        compiler_params=_params(2, VMEM_LIMIT),
        name="mla_attention",
    )(*args)


def _cpow(ar, ai, e, nbits):
    rr = jnp.ones_like(ar)
    ri = jnp.zeros_like(ar)
    br, bi = ar, ai
    for k in range(nbits):
        bit = ((e >> k) & 1) == 1
        nr = rr * br - ri * bi
        ni = rr * bi + ri * br
        rr = jnp.where(bit, nr, rr)
        ri = jnp.where(bit, ni, ri)
        if k + 1 < nbits:
            br, bi = br * br - bi * bi, 2.0 * br * bi
    return rr, ri


def _s5_abar_kernel(lr_ref, li_ref, ls_ref, o_ref):
    step = jnp.exp(ls_ref[...])
    lr = jnp.minimum(lr_ref[...], -1e-4)
    li = li_ref[...]
    mag = jnp.exp(lr * step)
    ar = mag * jnp.cos(li * step)
    ai = mag * jnp.sin(li * step)
    den = lr * lr + li * li
    o_ref[0] = ar
    o_ref[1] = ai
    o_ref[2] = ((ar - 1.0) * lr + ai * li) / den
    o_ref[3] = (ai * lr - (ar - 1.0) * li) / den


S5_PREP_GROUPS = 4


def _s5_prep_kernel(*refs):
    for gi in range(S5_PREP_GROUPS):
        _s5_prep_group(gi, *refs)


def _s5_prep_group(gi, arow_ref, acol_ref, btr_ref, bti_ref, ctr_ref, cti_ref,
                   wi_ref, ws_ref, wo_ref, ap_ref):
    n2 = 2 * S5_N
    blk_o = lax.broadcasted_iota(jnp.int32, (S5_N, S5_CW), 1) >> S5_GROUP_SHIFT
    lane_k = lax.broadcasted_iota(jnp.int32, (S5_GROUP, S5_CW), 1)
    row_k = lax.broadcasted_iota(jnp.int32, (S5_GROUP, S5_CW), 0)
    lane_b = lax.broadcasted_iota(jnp.int32, (S5_GROUP, n2), 1)
    lane_a = lax.broadcasted_iota(jnp.int32, (1, n2), 1)
    rep = ((lane_k & (S5_GROUP - 1)) == row_k).astype(BF16)

    def tile16(x):
        hi = x.astype(BF16)
        r1 = x - hi.astype(F32)
        mid = r1.astype(BF16)
        lo = (r1 - mid.astype(F32)).astype(BF16)
        d = lambda a: lax.dot_general(a, rep, (((0,), (0,)), ((), ())), preferred_element_type=F32)
        return d(hi) + d(mid) + d(lo)

    intra = [None] * S5_T
    for d in range(2):
        ar, ai, fr, fi = (arow_ref[d, gi, k:k + 1, :] for k in range(4))
        btr, bti = btr_ref[d, gi], bti_ref[d, gi]
        bbr = fr * btr - fi * bti
        bbi = fr * bti + fi * btr
        pws = [(jnp.ones_like(ar), jnp.zeros_like(ar))]
        for _ in range(S5_T):
            pr, pi = pws[-1]
            pws.append((pr * ar - pi * ai, pr * ai + pi * ar))
        for s in range(S5_T):
            pr, pi = pws[S5_T - 1 - s] if d == 0 else pws[s]
            ws_ref[d, gi, s * S5_GROUP:(s + 1) * S5_GROUP, :] = jnp.where(
                lane_b < S5_N, pr * bbr - pi * bbi, pr * bbi + pi * bbr).astype(BF16)

        acol = acol_ref[d, gi]
        arc = jnp.broadcast_to(acol[:, 0:1], (S5_N, S5_CW))
        aic = jnp.broadcast_to(acol[:, 1:2], (S5_N, S5_CW))
        ctr, cti = tile16(ctr_ref[d, gi]), tile16(cti_ref[d, gi])
        e_lag = blk_o if d == 0 else (S5_T - 1 - blk_o)
        pqr, pqi = _cpow(arc, aic, e_lag, 4)
        qr = pqr * ctr - pqi * cti
        qi = pqr * cti + pqi * ctr
        wo_ref[d, gi] = jnp.concatenate([qr * arc - qi * aic, -(qr * aic + qi * arc)], axis=0).astype(BF16)
        q_stack = jnp.concatenate([qr, qi], axis=0)
        bb_mix = jnp.where(lane_b < S5_N, bbr, -bbi)
        kt = _dot3(bb_mix, q_stack)
        for s in range(S5_T):
            if d == 0:
                blk = jnp.where(lane_k >= S5_GROUP * s, pltpu.roll(kt, S5_GROUP * s, 1), 0.0)
            else:
                blk = jnp.where(lane_k < S5_GROUP * (s + 1),
                                pltpu.roll(kt, (S5_GROUP * (s + 1)) % S5_CW, 1), 0.0)
            intra[s] = blk if intra[s] is None else intra[s] + blk

        pr1, pi1 = pws[S5_T]
        for k in range(6):
            ap_ref[d, gi, k:k + 1, :] = pr1
            ap_ref[d, gi, 8 + k:9 + k, :] = jnp.where(lane_a < S5_N, -pi1, pi1)
            pr1, pi1 = pr1 * pr1 - pi1 * pi1, 2.0 * pr1 * pi1
        ap_ref[d, gi, 6:8, :] = jnp.zeros((2, n2), F32)
        ap_ref[d, gi, 14:16, :] = jnp.zeros((2, n2), F32)
    for s in range(S5_T):
        wi_ref[gi, s * S5_GROUP:(s + 1) * S5_GROUP, :] = intra[s].astype(BF16)


def _s5_prep(a_re, a_im, log_step, b_re, b_im, c_re, c_im):
    g, n, n2 = S5_GROUPS, S5_N, 2 * S5_N
    abar = pl.pallas_call(
        _s5_abar_kernel,
        out_shape=jax.ShapeDtypeStruct((4, 2 * g, n), F32),
        grid=(1,),
        in_specs=[_const_spec((2 * g, n)), _const_spec((2 * g, n)), _const_spec((2 * g, 1))],
        out_specs=pl.BlockSpec((4, 2 * g, n), lambda i: (0, 0, 0)),
        compiler_params=_params(1),
        name="s5_abar",
    )(a_re.reshape(2 * g, n), a_im.reshape(2 * g, n), log_step.reshape(2 * g, 1))
    abar = jnp.concatenate([abar, abar], axis=-1).reshape(4, 2, g, n2)
    arow = abar.transpose(1, 2, 0, 3)
    acol = abar[:2, :, :, :n].transpose(1, 2, 3, 0)
    bt = lambda b: jnp.concatenate([jnp.swapaxes(b, 2, 3)] * 2, axis=-1)
    spec4 = lambda r, c: pl.BlockSpec((2, S5_PREP_GROUPS, r, c), lambda i: (0, i, 0, 0))
    return pl.pallas_call(
        _s5_prep_kernel,
        out_shape=[jax.ShapeDtypeStruct((g, S5_CW, S5_CW), BF16),
                   jax.ShapeDtypeStruct((2, g, S5_CW, n2), BF16),
                   jax.ShapeDtypeStruct((2, g, n2, S5_CW), BF16),
                   jax.ShapeDtypeStruct((2, g, 16, n2), F32)],
        grid=(g // S5_PREP_GROUPS,),
        in_specs=[spec4(4, n2), spec4(n, 2),
                  spec4(S5_GROUP, n2), spec4(S5_GROUP, n2), spec4(S5_GROUP, n), spec4(S5_GROUP, n)],
        out_specs=[pl.BlockSpec((S5_PREP_GROUPS, S5_CW, S5_CW), lambda i: (i, 0, 0)),
                   spec4(S5_CW, n2), spec4(n2, S5_CW), spec4(16, n2)],
        compiler_params=_params(1),
        name="s5_prep",
    )(arow, acol, bt(b_re), bt(b_im), c_re, c_im)


def _cmul_rows(x, p1, p2):
    return x * p1 + pltpu.roll(x, S5_N, 1) * p2


S5_OCT = LANES // S5_GROUP


def _block_transpose(xs):
    n = S5_OCT
    blk = lax.broadcasted_iota(jnp.int32, xs[0].shape, 1) >> S5_GROUP_SHIFT
    a = [pltpu.roll(x, i * S5_GROUP, 1) if i else x for i, x in enumerate(xs)]
    ys = []
    for d in range(n):
        diag = a[-d % n]
        for b in range(1, n):
            diag = jnp.where(blk == b, a[(b - d) % n], diag)
        ys.append(pltpu.roll(diag, LANES - d * S5_GROUP, 1) if d else diag)
    return ys


def _tokens_to_chunks(u_ref, ug_ref):
    rows = u_ref.shape[1] // S5_T
    for octet in range(S5_GROUPS // S5_OCT):
        for half in range(2):
            xs = [u_ref[octet, pl.ds(S5_OCT * half + tt, rows, stride=S5_T), :] for tt in range(S5_OCT)]
            for gl, x in enumerate(_block_transpose(xs)):
                ug_ref[octet * S5_OCT + gl, :, half * LANES:(half + 1) * LANES] = x


def _chunks_to_tokens(yg_ref, y_ref):
    rows = y_ref.shape[1] // S5_T
    for octet in range(S5_GROUPS // S5_OCT):
        for half in range(2):
            ys = [yg_ref[octet * S5_OCT + gl, :, half * LANES:(half + 1) * LANES] for gl in range(S5_OCT)]
            for tt, y in enumerate(_block_transpose(ys)):
                y_ref[octet, pl.ds(S5_OCT * half + tt, rows, stride=S5_T), :] = y


def _s5_core_kernel(ug_ref, wi_ref, ws_ref, wo_ref, ap_ref, h0_ref, dv_ref, yg_ref, fin_ref, z_ref):
    n2 = 2 * S5_N
    r = lax.broadcasted_iota(jnp.int32, (S5_ROWS, n2), 0)
    in_p = r < S5_ROWS_P
    rib = jnp.where(in_p, r & (CH_P - 1), (r - S5_ROWS_P) & (CH_S - 1))
    nch = jnp.where(in_p, CH_P, CH_S)

    def one_group(gl, slot):
        ub = ug_ref[gl].astype(BF16)
        y = jnp.dot(ub, wi_ref[gl], preferred_element_type=F32)
        for d in range(2):
            p1, p2 = ap_ref[d, gl, 0:1, :], ap_ref[d, gl, 8:9, :]
            edge = [S5_ROWS_P + CH_S * b + (0 if d == 0 else CH_S - 1) for b in range(NB_S)]
            h0 = [h0_ref[gl, d, b:b + 1, :] for b in range(NB_S)]
            s = jnp.dot(ub, ws_ref[d, gl], preferred_element_type=F32)
            for b in range(NB_S):
                s = s + jnp.where(r == edge[b], _cmul_rows(h0[b], p1, p2), 0.0)
            def scan_step(x, k, pos, count):
                sh = 1 << k
                if d == 0:
                    t = jnp.where(pos >= sh, pltpu.roll(x, sh, 0), 0.0)
                else:
                    t = jnp.where(pos < count - sh, pltpu.roll(x, x.shape[0] - sh, 0), 0.0)
                return x + _cmul_rows(t, ap_ref[d, gl, k:k + 1, :], ap_ref[d, gl, 8 + k:9 + k, :])

            for k in range(CH_P.bit_length() - 1):
                s = scan_step(s, k, rib, nch)
            tail = s[S5_ROWS_P:]
            for k in range(CH_P.bit_length() - 1, CH_S.bit_length() - 1):
                tail = scan_step(tail, k, rib[S5_ROWS_P:], CH_S)
            s = jnp.concatenate([s[:S5_ROWS_P], tail], axis=0)
            z_ref[slot, d] = s
            first = CH_P - 1 if d == 0 else 0
            fin_ref[gl, d] = z_ref[slot, d, pl.ds(first, NB_P, stride=CH_P), :]
            if d == 0:
                sp = jnp.where(rib >= 1, pltpu.roll(s, 1, 0), 0.0)
            else:
                sp = jnp.where(rib < nch - 1, pltpu.roll(s, S5_ROWS - 1, 0), 0.0)
            for b in range(NB_S):
                sp = jnp.where(r == edge[b], h0[b], sp)
            y = y + jnp.dot(sp.astype(BF16), wo_ref[d, gl], preferred_element_type=F32)
        yg_ref[gl] = y + dv_ref[gl] * ug_ref[gl]

    def group_pair(gp, carry):
        for slot in range(2):
            one_group(2 * gp + slot, slot)
        return carry

    lax.fori_loop(0, S5_OCT // 2, group_pair, 0)


def _s5_core(ug, prep, h0, d_skip):
    w_intra, w_state, w_out, apow = prep
    g, n2 = S5_GROUPS, 2 * S5_N
    spec4 = lambda r, c: pl.BlockSpec((2, S5_OCT, r, c), lambda i: (0, i, 0, 0))
    chunks = pl.BlockSpec((S5_OCT, S5_ROWS, S5_CW), lambda i: (i, 0, 0))
    dvec = jnp.tile(d_skip.reshape(g, 1, S5_GROUP), (1, 1, S5_T))
    return pl.pallas_call(
        _s5_core_kernel,
        out_shape=[jax.ShapeDtypeStruct((g, S5_ROWS, S5_CW), F32),
                   jax.ShapeDtypeStruct((g, 2, NB_P, n2), F32)],
        grid=(g // S5_OCT,),
        in_specs=[chunks,
                  pl.BlockSpec((S5_OCT, S5_CW, S5_CW), lambda i: (i, 0, 0)),
                  spec4(S5_CW, n2), spec4(n2, S5_CW), spec4(16, n2),
                  pl.BlockSpec((S5_OCT, 2, 8, n2), lambda i: (i, 0, 0, 0)),
                  pl.BlockSpec((S5_OCT, 1, S5_CW), lambda i: (i, 0, 0))],
        out_specs=[chunks, pl.BlockSpec((S5_OCT, 2, NB_P, n2), lambda i: (i, 0, 0, 0))],
        scratch_shapes=[pltpu.VMEM((2, 2, S5_ROWS, n2), F32)],
        compiler_params=_params(1, VMEM_LIMIT),
        name="s5_scan",
    )(ug, w_intra, w_state, w_out, apow, h0, dvec)


DF_SCALE = DF_DH ** -0.5 * LOG2E
DFW = DF_HEADS * 2 * DF_DH
IN_B = 3 * HY_WIDTH + 2 * DFW + DF_HEADS * DF_V


def _inproj_b_kernel(y_ref, mod_ref, g_ref, win_ref, c_ref, s_ref,
                     hy_ref, q_ref, kp_ref, ks_ref, vp_ref, vs_ref, kc_ref, vc_ref, wbf_ref):
    @pl.when(pl.program_id(0) == 0)
    def _():
        wbf_ref[...] = win_ref[...].astype(BF16)

    mod = _mod_rows(mod_ref)
    h = _modulate(y_ref[...], g_ref[...], mod[3], mod[4]).astype(BF16)
    p = jnp.dot(h, wbf_ref[...], preferred_element_type=F32)
    o1 = 3 * HY_WIDTH
    hy_ref[...] = p[:, :o1]
    q_ref[...] = (_rope(p[:, o1:o1 + DFW], c_ref[...], s_ref[...]) * DF_SCALE).astype(BF16)
    _tok_write(kp_ref, ks_ref, _rope(p[:, o1 + DFW:o1 + 2 * DFW], c_ref[...], s_ref[...]).astype(BF16))
    _tok_write(vp_ref, vs_ref, p[:, o1 + 2 * DFW:].astype(BF16))

    @pl.when(pl.program_id(0) < NT_P)
    def _():
        for lo, cache_ref in ((o1 + DFW, kc_ref), (o1 + 2 * DFW, vc_ref)):
            t = lax.dot_general(wbf_ref[:, lo:lo + DFW], h, (((0,), (1,)), ((), ())),
                                preferred_element_type=F32)
            for j in range(TM // L_P):
                cache_ref[j] = t[:, j * L_P:(j + 1) * L_P]


def _inproj_b(y, mods_l, g, w_in):
    cs, sn = _rope_tables(DFW, tuple(range(0, DFW, DF_DH)))
    pos = lambda i: (_pos_index(i), 0)
    kv_shapes, kv_specs = [], []
    for width in (DFW, DF_HEADS * DF_V):
        kv_shapes += [jax.ShapeDtypeStruct((TOK_P, width), BF16), jax.ShapeDtypeStruct((TOK_S, width), BF16)]
        kv_specs += _split_out(width)[1]
    seqs = TM // L_P
    cache_shapes = [jax.ShapeDtypeStruct((NB_P, w, L_P), F32) for w in (DFW, DF_HEADS * DF_V)]
    cache_specs = [pl.BlockSpec((seqs, w, L_P), lambda i: (jnp.minimum(i, NT_P - 1), 0, 0))
                   for w in (DFW, DF_HEADS * DF_V)]
    hy_u, q, kp, ks, vp, vs, k_cache, v_cache = pl.pallas_call(
        _inproj_b_kernel,
        out_shape=[jax.ShapeDtypeStruct((TOK, 3 * HY_WIDTH), F32), jax.ShapeDtypeStruct((TOK, DFW), BF16)]
                  + kv_shapes + cache_shapes,
        grid=(NT,),
        in_specs=[pl.BlockSpec((TM, D), _row),
                  _mod_spec(mods_l[1]),
                  _const_spec((1, D)), _const_spec((D, IN_B)),
                  pl.BlockSpec((TM, DFW), pos), pl.BlockSpec((TM, DFW), pos)],
        out_specs=[pl.BlockSpec((TM, 3 * HY_WIDTH), _row), pl.BlockSpec((TM, DFW), _row)] + kv_specs + cache_specs,
        scratch_shapes=[pltpu.VMEM((D, IN_B), BF16)],
        compiler_params=_params(1, VMEM_LIMIT),
        name="inproj_odd",
    )(y, mods_l[0], g[None], w_in, jnp.asarray(cs), jnp.asarray(sn))
    return hy_u, q, (kp, ks), (vp, vs), k_cache, v_cache


def _diff_attn_kernel(nseg, nseq, lam_init, q_ref, lam_ref, sub_ref, *refs):
    o_ref = refs[-1]
    lp = lam_ref[...]
    lam = (jnp.exp(jnp.sum(lp[0:1] * lp[1:2], axis=-1, keepdims=True))
           - jnp.exp(jnp.sum(lp[2:3] * lp[3:4], axis=-1, keepdims=True)) + lam_init)
    tq = q_ref.shape[0] // nseq
    lane = lax.broadcasted_iota(jnp.int32, (tq, LANES), 1)
    for j in range(nseq):
        qr = slice(j * tq, (j + 1) * tq)
        krs = [slice(j * (refs[2 * s].shape[0] // nseq), (j + 1) * (refs[2 * s].shape[0] // nseq))
               for s in range(nseg)]
        for pair in range(DF_HEADS // 2):
            cs = slice(pair * LANES, (pair + 1) * LANES)
            q = q_ref[qr, cs]
            ks = [refs[2 * s][krs[s], cs].astype(BF16) for s in range(nseg)]
            vs = [refs[2 * s + 1][krs[s], cs].astype(BF16) for s in range(nseg)]
            outs = []
            for hh in range(2):
                parts = []
                for half in range(2):
                    unit = 2 * hh + half
                    qm = jnp.where((lane >> DF_DH_SHIFT) == unit, q, jnp.zeros_like(q))
                    scores = [_dot_nt(qm, k) for k in ks]
                    parts.append(_softmax_pv(scores, vs, hh))
                o = parts[0] - lam * parts[1]
                mine = (lane >> HALF_SHIFT) == hh
                ms = jnp.sum(jnp.where(mine, o * o, 0.0), axis=-1, keepdims=True) * (1.0 / DF_V)
                outs.append(o * lax.rsqrt(ms + EPS))
            o = jnp.where(lane < DF_V, outs[0], outs[1]) * sub_ref[...] * (1.0 - lam_init)
            o_ref[qr, cs] = o.astype(o_ref.dtype)


def _diff_attention(q, k, v, lam_p, subln, lam_init, n_batch, seq, tq, row0, ctx=None, nseq=1):
    qt = seq // tq
    qb0, kb0 = row0 // (nseq * tq), 0
    in_specs = [pl.BlockSpec((nseq * tq, DFW), lambda b, j: (qb0 + b * qt + j, 0)),
                pl.BlockSpec((4, DF_DH), lambda b, j: (0, 0)),
                pl.BlockSpec((1, LANES), lambda b, j: (0, 0))]
    args = [q, lam_p, jnp.concatenate([subln, subln])[None]]
    segs = []
    if ctx is not None:
        segs.append((ctx, PAST, 0))
    segs.append(((k, v), seq, kb0))
    for (a_k, a_v), ln, off in segs:
        idx = lambda b, j, off=off: (off + b, 0)
        in_specs += [pl.BlockSpec((nseq * ln, DFW), idx), pl.BlockSpec((nseq * ln, DF_HEADS * DF_V), idx)]
        args += [a_k, a_v]
    return pl.pallas_call(
        functools.partial(_diff_attn_kernel, len(segs), nseq, lam_init),
        out_shape=jax.ShapeDtypeStruct((n_batch * seq, DF_HEADS * DF_V), BF16),
        grid=(n_batch // nseq, qt),
        in_specs=in_specs,
        out_specs=pl.BlockSpec((nseq * tq, DF_HEADS * DF_V), lambda b, j: (b * qt + j, 0)),
        compiler_params=_params(2, VMEM_LIMIT),
        name="diff_attention",
    )(*args)


def _hy_filter_kernel(feat_ref, w1_ref, b1_ref, w2_ref, b2_ref, fq_ref, w3_ref, dec_ref, o_ref):
    feat = feat_ref[...]
    fq = fq_ref[...]
    h = jnp.sin(fq * (_dot3(feat, w1_ref[...]) + b1_ref[...]))
    h = jnp.sin(fq * (_dot3(h, w2_ref[...]) + b2_ref[...]))
    window = jnp.exp(-feat[:, 0:1] * jnp.abs(dec_ref[...]))
    for j in range(4):
        cs = slice(j * HY_WIDTH, (j + 1) * HY_WIDTH)
        o_ref[:, cs] = _dot3(h, w3_ref[:, cs]) * window


def _hy_spectrum_kernel(L, cs_ref, hf_ref, hb_ref, o_ref):
    row = lax.broadcasted_iota(jnp.int32, (L, HY_WIDTH), 0)
    first = row == 0
    tf = _dot(cs_ref[...], hf_ref[...])
    tb = _dot(cs_ref[...], jnp.where(first, 0.0, hb_ref[...]))
    ka = tf[:L] + tb[:L]
    kb = jnp.where(first, tf[L:] + tb[L:], tf[L:] - tb[L:])
    wv = jnp.where(first, 1.0 / (2 * L), 2.0 / (2 * L))
    o_ref[0, 0] = ka * wv
    o_ref[0, 1] = jnp.where(first, 0.0, kb) * wv
    o_ref[0, 2] = jnp.where(first, kb, ka) * wv


HY_CH = 256


def _hy_conv_kernel(L, cs_ref, ct_ref, kf_ref, v_ref, x1_ref, x2_ref,
                    wv_ref, w1_ref, w2_ref, bias_ref, o_ref):
    row = lax.broadcasted_iota(jnp.int32, (L, HY_CH), 0)

    def short(x, w):
        prev = jnp.where(row >= 1, pltpu.roll(x, 1, 0), 0.0)
        nxt = jnp.where(row <= L - 2, pltpu.roll(x, L - 1, 0), 0.0)
        return w[0:1] * prev + w[1:2] * x + w[2:3] * nxt

    for j in range(v_ref.shape[0] // L):
        rs = slice(j * L, (j + 1) * L)
        for k in range(HY_WIDTH // HY_CH):
            ch = slice(k * HY_CH, (k + 1) * HY_CH)
            z = short(v_ref[rs, ch], wv_ref[:, ch])
            gates = (short(x1_ref[rs, ch], w1_ref[:, ch]), short(x2_ref[rs, ch], w2_ref[:, ch]))
            for n in range(2):
                ab = _dot(cs_ref[...], z)
                a, b = ab[:L], ab[L:]
                ka, kb1, ka2 = kf_ref[n, 0, :, ch], kf_ref[n, 1, :, ch], kf_ref[n, 2, :, ch]
                pq = jnp.concatenate([a * ka - b * kb1, a * kb1 + b * ka2], axis=0)
                conv = _dot(ct_ref[...], pq)
                z = gates[n] * (conv + bias_ref[n:n + 1, ch] * z)
            o_ref[rs, ch] = z.astype(o_ref.dtype)


def _hyena_spectrum(L, phy):
    conv_w, w1, b1, w2, b2, freq, w3, decay, bias = phy
    feat = jnp.asarray(_hyena_features(L))
    w1p = jnp.pad(w1, ((0, LANES - HY_EMB), (0, 0)))
    filt = pl.pallas_call(
        _hy_filter_kernel,
        out_shape=jax.ShapeDtypeStruct((L, 4 * HY_WIDTH), F32),
        grid=(1,),
        in_specs=[_const_spec((L, LANES)), _const_spec((LANES, HY_FH)), _const_spec((1, HY_FH)),
                  _const_spec((HY_FH, HY_FH)), _const_spec((1, HY_FH)), _const_spec((1, HY_FH)),
                  _const_spec((HY_FH, 4 * HY_WIDTH)), _const_spec((1, HY_WIDTH))],
        out_specs=pl.BlockSpec((L, 4 * HY_WIDTH), lambda i: (0, 0)),
        compiler_params=_params(1, VMEM_LIMIT),
        name="hyena_filter",
    )(feat, w1p, b1[None], w2, b2[None], freq[None], w3, decay[None])
    cs = jnp.asarray(_dft_tables(L)[0]).astype(BF16)
    return pl.pallas_call(
        functools.partial(_hy_spectrum_kernel, L),
        out_shape=jax.ShapeDtypeStruct((2, 3, L, HY_WIDTH), F32),
        grid=(2,),
        in_specs=[_const_spec((2 * L, L)),
                  pl.BlockSpec((L, HY_WIDTH), lambda n: (0, n)),
                  pl.BlockSpec((L, HY_WIDTH), lambda n: (0, 2 + n))],
        out_specs=pl.BlockSpec((1, 3, L, HY_WIDTH), lambda n: (n, 0, 0, 0)),
        compiler_params=_params(1, VMEM_LIMIT),
        name="hyena_spectrum",
    )(cs, filt, filt)


def _hyena_conv(hy_u, spec, phy, n_batch, L, seqs, row0):
    conv_w, bias = phy[0], phy[8]
    cs, ct = (jnp.asarray(t).astype(BF16) for t in _dft_tables(L))
    rows = seqs * L
    rb0 = row0 // rows
    col = lambda off: (lambda b: (0, off))
    tok = lambda off: (lambda b: (rb0 + b, off))
    blk = lambda idx: pl.BlockSpec((rows, HY_WIDTH), idx)
    return pl.pallas_call(
        functools.partial(_hy_conv_kernel, L),
        out_shape=jax.ShapeDtypeStruct((n_batch * L, HY_WIDTH), BF16),
        grid=(n_batch // seqs,),
        in_specs=[_const_spec((2 * L, L)), _const_spec((L, 2 * L)), _const_spec((2, 3, L, HY_WIDTH)),
                  blk(tok(0)), blk(tok(1)), blk(tok(2)),
                  pl.BlockSpec((3, HY_WIDTH), col(0)), pl.BlockSpec((3, HY_WIDTH), col(1)),
                  pl.BlockSpec((3, HY_WIDTH), col(2)), _const_spec((2, HY_WIDTH))],
        out_specs=blk(lambda b: (b, 0)),
        compiler_params=_params(1, VMEM_LIMIT),
        name="hyena_conv",
    )(cs, ct, spec, hy_u, hy_u, hy_u, conv_w, conv_w, conv_w, bias)


def _even_mixer(y, mods_l, g, pa, ps5, ctx_ckv, ctx_krope, ctx_state):
    w_in, w_out, q_norm, w_uq, kv_norm, w_ukv = pa
    a_re, a_im, log_step, b_re, b_im, c_re, c_im, d_skip, w_glu = ps5
    q, ckv, kr_unrot, kr_rot, kn, v, ug, (w_k, w_v) = _inproj_a(y, mods_l, g, w_in, q_norm, w_uq, kv_norm, w_ukv)

    ctx_flat = ctx_ckv.reshape(NB_S * PAST, MLA_KV_RANK)
    ctx_kn = _linear(ctx_flat, w_k, PAST, BF16)
    ctx_v = _linear(ctx_flat, w_v, PAST, BF16)
    ctx_kr = jnp.pad(ctx_krope.reshape(NB_S * PAST, MLA_ROPE),
                     ((0, 0), (KR_AT, LANES - KR_AT - MLA_ROPE))).astype(BF16)
    att_p = _mla_attention(q, kn, kr_rot, v, NB_P, L_P, L_P, 0, nseq=2)
    att_s = _mla_attention(q, kn, kr_rot, v, NB_S, L_S, L_S, TOK_P, ctx=(ctx_kn, ctx_kr, ctx_v))

    prep = _s5_prep(a_re, a_im, log_step, b_re, b_im, c_re, c_im)
    h0 = ctx_state.transpose(3, 1, 0, 2, 4).reshape(S5_GROUPS, 2, NB_S, 2 * S5_N)
    h0 = jnp.pad(h0, ((0, 0), (0, 0), (0, 8 - NB_S), (0, 0)))
    s5y, fin = _s5_core(ug, prep, h0, d_skip)

    mixer = ((att_p, att_s), s5y, w_out, w_glu)
    new_ckv = ckv.reshape(NB_P, L_P, MLA_KV_RANK)
    new_krope = kr_unrot.reshape(NB_P, L_P, MLA_ROPE)
    new_state = fin.reshape(S5_GROUPS, 2, NB_P, 2, S5_N).transpose(2, 1, 3, 0, 4)
    return mixer, new_ckv, new_krope, new_state


def _odd_mixer(y, mods_l, g, pb, phy, ctx_k, ctx_v, lam_init):
    w_in, w_out, lam_p, subln = pb
    hy_u, q, (k_p, k_s), (v_p, v_s), k_cache, v_cache = _inproj_b(y, mods_l, g, w_in)
    hy_p = _hyena_conv(hy_u, _hyena_spectrum(L_P, phy), phy, NB_P, L_P, 4, 0)
    hy_s = _hyena_conv(hy_u, _hyena_spectrum(L_S, phy), phy, NB_S, L_S, 1, TOK_P)
    ctx = (ctx_k.reshape(NB_S * PAST, DFW), ctx_v.reshape(NB_S * PAST, DF_HEADS * DF_V))
    att_p = _diff_attention(q, k_p, v_p, lam_p, subln, lam_init, NB_P, L_P, L_P, 0, nseq=2)
    att_s = _diff_attention(q, k_s, v_s, lam_p, subln, lam_init, NB_S, L_S, L_S, TOK_P, ctx=ctx)
    mixer = ((hy_p, hy_s), (att_p, att_s), w_out, None)
    new_k = k_cache.reshape(NB_P, DF_HEADS, 2, DF_DH, L_P).transpose(0, 4, 1, 2, 3)
    new_v = v_cache.reshape(NB_P, DF_HEADS, DF_V, L_P).transpose(0, 3, 1, 2)
    return mixer, new_k, new_v


def kernel(x_prompt, x_sample, c, c_ctx, cache_mla_ckv, cache_mla_krope, state_s5, cache_diff_k, cache_diff_v, ada_w, ada_b, norm_g, ff_w_in, ff_w_out, w_in_a, w_out_a, mla_q_norm, mla_w_uq, mla_kv_norm, mla_w_ukv, s5_a_re, s5_a_im, s5_log_step, s5_b_re, s5_b_im, s5_c_re, s5_c_im, s5_d, s5_w_glu, w_in_b, w_out_b, hy_conv, hy_w1, hy_b1, hy_w2, hy_b2, hy_freq, hy_w3, hy_decay, hy_bias, df_lambda, df_subln, final_norm):
    depth = ada_w.shape[0]
    y = (x_prompt.reshape(TOK_P, D), x_sample.reshape(TOK_S, D))
    mods = _adaln(jnp.concatenate([c_ctx[None], c], axis=0), ada_w, ada_b)
    new_ckv, new_krope, new_s5, new_dk, new_dv = [], [], [], [], []
    for l in range(depth):
        y = _half_ffn(y, (mods, l), norm_g[l, 0], ff_w_in, ff_w_out, l, 0)
        if l % 2 == 0:
            e = l // 2
            pa = (w_in_a[e], w_out_a[e], mla_q_norm[e], mla_w_uq[e], mla_kv_norm[e], mla_w_ukv[e])
            ps5 = (s5_a_re[e], s5_a_im[e], s5_log_step[e], s5_b_re[e], s5_b_im[e],
                   s5_c_re[e], s5_c_im[e], s5_d[e], s5_w_glu[e])
            mixer, ckv, krope, st = _even_mixer(y, (mods, l), norm_g[l, 1], pa, ps5, cache_mla_ckv[:, e],
                                                cache_mla_krope[:, e], state_s5[:, e])
            new_ckv.append(ckv)
            new_krope.append(krope)
            new_s5.append(st)
        else:
            o = l // 2
            lam_init = 0.8 - 0.6 * math.exp(-0.3 * l)
            pb = (w_in_b[o], w_out_b[o], df_lambda[o], df_subln[o])
            phy = (hy_conv[o], hy_w1[o], hy_b1[o], hy_w2[o], hy_b2[o], hy_freq[o],
                   hy_w3[o], hy_decay[o], hy_bias[o])
            mixer, dk, dv = _odd_mixer(y, (mods, l), norm_g[l, 1], pb, phy, cache_diff_k[:, o],
                                       cache_diff_v[:, o], lam_init)
            new_dk.append(dk)
            new_dv.append(dv)
        last = l == depth - 1
        y = _half_ffn(y, (mods, l), norm_g[l, 2], ff_w_in, ff_w_out, l, 1,
                      final_g=final_norm if last else None, mixer=mixer)
    y_prompt = y[0].reshape(NB_P, L_P, D)
    y_sample = y[1].reshape(NB_S, L_S, D)
    return (y_prompt, y_sample, jnp.stack(new_ckv, axis=1), jnp.stack(new_krope, axis=1),
            jnp.stack(new_s5, axis=1), jnp.stack(new_dk, axis=1), jnp.stack(new_dv, axis=1))
```
